```python
import jax
import jax.numpy as jnp
from jax import lax
import numpy as np

D_MODEL = 1024
BATCH = 8
SEQ = 4096
DEPTH = 2

GRID_W = 64
CTX_LEN = 256
BLOCK_Q = 128
ROPE_THETA = 10000.0
EPS = 1e-6
NEG_INF = -1e30

N_BRANCH = 4
BRANCH_W = D_MODEL // N_BRANCH
HEAD_DIM = 64

A_HEADS = BRANCH_W // HEAD_DIM
A_Q_RANK = D_MODEL // 4
A_KV_RANK = D_MODEL // 8
A_NOPE = 64
A_ROPE = 32
A_V = BRANCH_W // A_HEADS

B_HEADS = BRANCH_W // HEAD_DIM
B_KV_HEADS = 2
WINDOW = 128

C_HEADS = BRANCH_W // HEAD_DIM
NA_KH = 8
NA_KW = 16
NA_QROWS = BLOCK_Q // GRID_W

D_HEADS = BRANCH_W // HEAD_DIM
D_KV_HEADS = 2

A_COLS = A_Q_RANK + A_KV_RANK + A_ROPE
B_COLS = (B_HEADS + 2 * B_KV_HEADS) * HEAD_DIM
C_COLS = 3 * C_HEADS * HEAD_DIM
D_COLS = (D_HEADS + 2 * D_KV_HEADS) * HEAD_DIM
IN_COLS = A_COLS + B_COLS + C_COLS + D_COLS

N_GROUPS = 4
EXPERTS_PER_GROUP = 8
N_EXPERTS = N_GROUPS * EXPERTS_PER_GROUP
TOP_K_IN_GROUP = 2
EXPERT_FF = D_MODEL // 4
MOE_BLOCK = 128

kernel_name = 'hybrid_gated_mixers_hier_moe_dit'


def rmsnorm(x, g):
    xf = x.astype(jnp.float32)
    y = xf * lax.rsqrt(jnp.mean(xf * xf, axis=-1, keepdims=True) + EPS)
    return (y * g.astype(jnp.float32)).astype(x.dtype)


def modulate(x, g, shift, scale):
    return rmsnorm(x, g) * (1 + scale) + shift


def rope_tables(n_tok, rot_dim):
    t = jnp.arange(n_tok, dtype=jnp.int32)
    rows = (t // GRID_W).astype(jnp.float32)
    cols = (t % GRID_W).astype(jnp.float32)
    half = rot_dim // 2
    inv = ROPE_THETA ** (-jnp.arange(0, half, 2, dtype=jnp.float32) / half)
    ang = jnp.concatenate([rows[:, None] * inv, cols[:, None] * inv], axis=-1)
    return jnp.cos(ang), jnp.sin(ang)


def _rot_half(x, cos, sin):
    x1, x2 = jnp.split(x, 2, axis=-1)
    return jnp.concatenate([x1 * cos - x2 * sin, x2 * cos + x1 * sin], axis=-1)


def apply_axial_rope(x, cos, sin):
    xr, xc = jnp.split(x, 2, axis=-1)
    cr, cc = jnp.split(cos.astype(x.dtype), 2, axis=-1)
    sr, sc = jnp.split(sin.astype(x.dtype), 2, axis=-1)
    return jnp.concatenate([_rot_half(xr, cr[:, None], sr[:, None]),
                            _rot_half(xc, cc[:, None], sc[:, None])], axis=-1)


def attend(q, k, v, scale, bias=None, mask=None, sink=None):
    s = jnp.einsum('bqhgd,blhd->bhgql', q, k, preferred_element_type=jnp.float32) * scale
    if bias is not None:
        s = s + bias.astype(jnp.float32)
    if mask is not None:
        s = jnp.where(mask, s, NEG_INF)
    if sink is None:
        p = jax.nn.softmax(s, axis=-1)
    else:
        sk = sink.astype(jnp.float32)[None, :, :, None, None]
        m = jnp.maximum(jnp.max(s, axis=-1, keepdims=True), sk)
        e = jnp.exp(s - m)
        p = e / (jnp.sum(e, axis=-1, keepdims=True) + jnp.exp(sk - m))
    return jnp.einsum('bhgql,blhd->bqhgd', p.astype(v.dtype), v)


def sweep_blocks(fn, q):
    b, s = q.shape[:2]
    nb = s // BLOCK_Q
    qb = jnp.moveaxis(q.reshape((b, nb, BLOCK_Q) + q.shape[2:]), 1, 0)
    out = lax.map(lambda args: fn(args[0], args[1]), (qb, jnp.arange(nb, dtype=jnp.int32)))
    out = jnp.moveaxis(out, 0, 1)
    return out.reshape((b, s) + out.shape[3:])


def mla_project(pa, g_q_a, w_q_b, g_kv_a, w_kv_b, rope, with_q):
    b, s, _ = pa.shape
    cq, ckv, kr = jnp.split(pa, [A_Q_RANK, A_Q_RANK + A_KV_RANK], axis=-1)
    kv = (rmsnorm(ckv, g_kv_a) @ w_kv_b).reshape(b, s, A_HEADS, A_NOPE + A_V)
    k_nope, v = jnp.split(kv, [A_NOPE], axis=-1)
    kr = kr[:, :, None, :]
    if rope is not None:
        kr = apply_axial_rope(kr, *rope)
    k = jnp.concatenate([k_nope, jnp.broadcast_to(kr, (b, s, A_HEADS, A_ROPE))], axis=-1)
    q = None
    if with_q:
        q = (rmsnorm(cq, g_q_a) @ w_q_b).reshape(b, s, A_HEADS, A_NOPE + A_ROPE)
        if rope is not None:
            q = jnp.concatenate([q[..., :A_NOPE], apply_axial_rope(q[..., A_NOPE:], *rope)], axis=-1)
        q = q[:, :, :, None, :]
    return q, k, v


def branch_mla(pa, pa_c, rope, g_q_a, w_q_b, g_kv_a, w_kv_b, with_ctx):
    b, s, _ = pa.shape
    n_ctx = pa_c.shape[1]
    scale = (A_NOPE + A_ROPE) ** -0.5
    q, k, v = mla_project(pa, g_q_a, w_q_b, g_kv_a, w_kv_b, rope, True)
    qc, kc, vc = mla_project(pa_c, g_q_a, w_q_b, g_kv_a, w_kv_b, None, with_ctx)
    k_all = jnp.concatenate([k, kc], axis=1)
    v_all = jnp.concatenate([v, vc], axis=1)
    o = sweep_blocks(lambda qb, i: attend(qb, k_all, v_all, scale), q).reshape(b, s, BRANCH_W)
    oc = attend(qc, kc, vc, scale).reshape(b, n_ctx, BRANCH_W) if with_ctx else None
    return o, oc


def branch_window(pb, pb_c, rope, sink, with_ctx):
    b, s, _ = pb.shape
    n_ctx = pb_c.shape[1]
    grp = B_HEADS // B_KV_HEADS
    scale = HEAD_DIM ** -0.5
    cuts = [B_HEADS * HEAD_DIM, (B_HEADS + B_KV_HEADS) * HEAD_DIM]
    q, k, v = jnp.split(pb, cuts, axis=-1)
    q = apply_axial_rope(q.reshape(b, s, B_HEADS, HEAD_DIM), *rope).reshape(b, s, B_KV_HEADS, grp, HEAD_DIM)
    k = apply_axial_rope(k.reshape(b, s, B_KV_HEADS, HEAD_DIM), *rope)
    v = v.reshape(b, s, B_KV_HEADS, HEAD_DIM)
    qc, kc, vc = jnp.split(pb_c, cuts, axis=-1)
    kc = kc.reshape(b, n_ctx, B_KV_HEADS, HEAD_DIM)
    vc = vc.reshape(b, n_ctx, B_KV_HEADS, HEAD_DIM)
    sink_g = sink.reshape(B_KV_HEADS, grp)
    pad = ((0, 0), (WINDOW, WINDOW), (0, 0), (0, 0))
    kp = jnp.pad(k, pad)
    vp = jnp.pad(v, pad)
    band = BLOCK_Q + 2 * WINDOW
    qi = jnp.arange(BLOCK_Q, dtype=jnp.int32)[:, None]
    kj = jnp.arange(band, dtype=jnp.int32)[None, :]
    in_window = jnp.abs(qi + WINDOW - kj) <= WINDOW
    ctx_ok = jnp.ones((BLOCK_Q, n_ctx), dtype=bool)

    def block(qb, i):
        start = i * BLOCK_Q
        kb = lax.dynamic_slice_in_dim(kp, start, band, axis=1)
        vb = lax.dynamic_slice_in_dim(vp, start, band, axis=1)
        kpos = start - WINDOW + kj
        mask = jnp.concatenate([in_window & (kpos >= 0) & (kpos < s), ctx_ok], axis=-1)
        return attend(qb, jnp.concatenate([kb, kc], axis=1), jnp.concatenate([vb, vc], axis=1),
                      scale, mask=mask, sink=sink_g)

    o = sweep_blocks(block, q).reshape(b, s, BRANCH_W)
    oc = None
    if with_ctx:
        qc = qc.reshape(b, n_ctx, B_KV_HEADS, grp, HEAD_DIM)
        oc = attend(qc, kc, vc, scale, sink=sink_g).reshape(b, n_ctx, BRANCH_W)
    return o, oc


def branch_neighbourhood(pn, pn_c, rows, rpb, with_ctx):
    b, s, _ = pn.shape
    n_ctx = pn_c.shape[1]
    scale = HEAD_DIM ** -0.5
    q, k, v = jnp.split(pn, 3, axis=-1)
    q = q.reshape(b, s, C_HEADS, 1, HEAD_DIM)
    kg = k.reshape(b, rows, GRID_W, C_HEADS, HEAD_DIM)
    vg = v.reshape(b, rows, GRID_W, C_HEADS, HEAD_DIM)
    qc, kc, vc = jnp.split(pn_c, 3, axis=-1)
    kc = kc.reshape(b, n_ctx, C_HEADS, HEAD_DIM)
    vc = vc.reshape(b, n_ctx, C_HEADS, HEAD_DIM)
    kh = min(NA_KH, rows)
    nbr = min(kh + NA_QROWS - 1, rows)
    a = jnp.arange(BLOCK_Q, dtype=jnp.int32)
    j = jnp.arange(nbr * GRID_W, dtype=jnp.int32)
    q_col = a % GRID_W
    k_col = j % GRID_W
    c_start = jnp.clip(q_col - NA_KW // 2, 0, GRID_W - NA_KW)
    col_ok = (k_col[None] >= c_start[:, None]) & (k_col[None] < c_start[:, None] + NA_KW)
    dc = jnp.clip(k_col[None] - q_col[:, None], -(NA_KW - 1), NA_KW - 1) + NA_KW - 1
    ctx_ok = jnp.ones((BLOCK_Q, n_ctx), dtype=bool)
    ctx_bias = jnp.zeros((C_HEADS, BLOCK_Q, n_ctx), dtype=rpb.dtype)

    def block(qb, i):
        r0 = i * NA_QROWS
        s0 = jnp.clip(r0 - kh // 2, 0, rows - kh)
        band_start = jnp.clip(s0, 0, rows - nbr)
        kb = lax.dynamic_slice_in_dim(kg, band_start, nbr, axis=1).reshape(b, nbr * GRID_W, C_HEADS, HEAD_DIM)
        vb = lax.dynamic_slice_in_dim(vg, band_start, nbr, axis=1).reshape(b, nbr * GRID_W, C_HEADS, HEAD_DIM)
        q_row = r0 + a // GRID_W
        k_row = band_start + j // GRID_W
        r_start = jnp.clip(q_row - kh // 2, 0, rows - kh)
        row_ok = (k_row[None] >= r_start[:, None]) & (k_row[None] < r_start[:, None] + kh)
        dr = jnp.clip(k_row[None] - q_row[:, None], -(NA_KH - 1), NA_KH - 1) + NA_KH - 1
        bias = jnp.concatenate([rpb[:, dr, dc], ctx_bias], axis=-1)[None, :, None]
        mask = jnp.concatenate([row_ok & col_ok, ctx_ok], axis=-1)
        return attend(qb, jnp.concatenate([kb, kc], axis=1), jnp.concatenate([vb, vc], axis=1),
                      scale, bias=bias, mask=mask)

    o = sweep_blocks(block, q).reshape(b, s, BRANCH_W)
    oc = None
    if with_ctx:
        qc = qc.reshape(b, n_ctx, C_HEADS, 1, HEAD_DIM)
        oc = attend(qc, kc, vc, scale).reshape(b, n_ctx, BRANCH_W)
    return o, oc


def branch_qknorm(pd, pd_c, rope, g_qn, g_kn, with_ctx):
    b, s, _ = pd.shape
    n_ctx = pd_c.shape[1]
    grp = D_HEADS // D_KV_HEADS
    scale = HEAD_DIM ** -0.5
    cuts = [D_HEADS * HEAD_DIM, (D_HEADS + D_KV_HEADS) * HEAD_DIM]
    q, k, v = jnp.split(pd, cuts, axis=-1)
    q = apply_axial_rope(rmsnorm(q.reshape(b, s, D_HEADS, HEAD_DIM), g_qn), *rope)
    q = q.reshape(b, s, D_KV_HEADS, grp, HEAD_DIM)
    k = apply_axial_rope(rmsnorm(k.reshape(b, s, D_KV_HEADS, HEAD_DIM), g_kn), *rope)
    v = v.reshape(b, s, D_KV_HEADS, HEAD_DIM)
    qc, kc, vc = jnp.split(pd_c, cuts, axis=-1)
    kc = rmsnorm(kc.reshape(b, n_ctx, D_KV_HEADS, HEAD_DIM), g_kn)
    vc = vc.reshape(b, n_ctx, D_KV_HEADS, HEAD_DIM)
    k_all = jnp.concatenate([k, kc], axis=1)
    v_all = jnp.concatenate([v, vc], axis=1)
    o = sweep_blocks(lambda qb, i: attend(qb, k_all, v_all, scale), q).reshape(b, s, BRANCH_W)
    oc = None
    if with_ctx:
        qc = rmsnorm(qc.reshape(b, n_ctx, D_HEADS, HEAD_DIM), g_qn).reshape(b, n_ctx, D_KV_HEADS, grp, HEAD_DIM)
        oc = attend(qc, kc, vc, scale).reshape(b, n_ctx, BRANCH_W)
    return o, oc


def merge_branches(h, outs, w_gate, b_gate, w_branch, w_out):
    y = None
    for n in range(N_BRANCH):
        t = jax.nn.sigmoid(h @ w_gate[n] + b_gate[n]) * (outs[n] @ w_branch[n])
        y = t if y is None else y + t
    return y @ w_out


def token_mixer(h, hc, rows, rope_a, rope_h, w_in, g_q_a, w_q_b, g_kv_a, w_kv_b,
                sink, rpb, g_qn, g_kn, w_gate, b_gate, w_branch, w_out, with_ctx):
    cuts = [A_COLS, A_COLS + B_COLS, A_COLS + B_COLS + C_COLS]
    pa, pb, pn, pd = jnp.split(h @ w_in, cuts, axis=-1)
    pa_c, pb_c, pn_c, pd_c = jnp.split(hc @ w_in, cuts, axis=-1)
    oa, oa_c = branch_mla(pa, pa_c, rope_a, g_q_a, w_q_b, g_kv_a, w_kv_b, with_ctx)
    ob, ob_c = branch_window(pb, pb_c, rope_h, sink, with_ctx)
    on, on_c = branch_neighbourhood(pn, pn_c, rows, rpb, with_ctx)
    od, od_c = branch_qknorm(pd, pd_c, rope_h, g_qn, g_kn, with_ctx)
    y = merge_branches(h, (oa, ob, on, od), w_gate, b_gate, w_branch, w_out)
    yc = None
    if with_ctx:
        yc = merge_branches(hc, (oa_c, ob_c, on_c, od_c), w_gate, b_gate, w_branch, w_out)
    return y, yc


def hier_moe(h, w_group, b_group, w_router, b_router, w_ff1, w_ff3, w_ff2):
    shape = h.shape
    x = h.reshape(-1, shape[-1])
    n_tok = x.shape[0]
    g_logit = (x @ w_group + b_group).astype(jnp.float32)
    g_prob = jax.nn.softmax(g_logit, axis=-1)
    g_sel = jnp.argmax(g_logit, axis=-1).astype(jnp.int32)
    g_w = jnp.take_along_axis(g_prob, g_sel[:, None], axis=-1)
    e_logit = (x @ w_router + b_router).astype(jnp.float32).reshape(n_tok, N_GROUPS, EXPERTS_PER_GROUP)
    e_logit = jnp.take_along_axis(e_logit, g_sel[:, None, None], axis=1)[:, 0]
    top_v, top_i = lax.top_k(e_logit, TOP_K_IN_GROUP)
    wts = (g_w * jax.nn.softmax(top_v, axis=-1)).reshape(-1)
    eid = (g_sel[:, None] * EXPERTS_PER_GROUP + top_i).reshape(-1).astype(jnp.int32)
    tok = jnp.repeat(jnp.arange(n_tok, dtype=jnp.int32), TOP_K_IN_GROUP)
    n_assign = n_tok * TOP_K_IN_GROUP
    order = jnp.argsort(eid)
    e_s, tok_s, w_s = eid[order], tok[order], wts[order]
    counts = jnp.zeros((N_EXPERTS,), jnp.int32).at[eid].add(1)
    padded = (counts + MOE_BLOCK - 1) // MOE_BLOCK * MOE_BLOCK
    starts = jnp.cumsum(counts) - counts
    p_ends = jnp.cumsum(padded)
    p_starts = p_ends - padded
    dest = p_starts[e_s] + jnp.arange(n_assign, dtype=jnp.int32) - starts[e_s]
    n_blocks = (n_assign + MOE_BLOCK - 1) // MOE_BLOCK + N_EXPERTS
    n_slots = n_blocks * MOE_BLOCK
    slot_tok = jnp.zeros((n_slots,), jnp.int32).at[dest].set(tok_s)
    slot_w = jnp.zeros((n_slots,), jnp.float32).at[dest].set(w_s)
    blk_start = jnp.arange(n_blocks, dtype=jnp.int32) * MOE_BLOCK
    blk_e = jnp.minimum(jnp.searchsorted(p_ends, blk_start, side='right'), N_EXPERTS - 1)
    xs = x[slot_tok].reshape(n_blocks, MOE_BLOCK, shape[-1])

    def expert_block(args):
        xb, e = args
        return (jax.nn.silu(xb @ w_ff1[e]) * (xb @ w_ff3[e])) @ w_ff2[e]

    ys = lax.map(expert_block, (xs, blk_e)).reshape(n_slots, shape[-1])
    out = jnp.zeros_like(x).at[slot_tok].add(ys * slot_w[:, None].astype(ys.dtype))
    return out.reshape(shape)


def setup_inputs(seed: int = 0) -> dict:
    key = jax.random.key(seed)
    keys = iter(jax.random.split(key, 40))
    L, D = DEPTH, D_MODEL

    def normal(shape, std=1.0):
        return std * jax.random.normal(next(keys), shape, jnp.float32)

    def dense(shape, fan_in, gain=1.0):
        return normal(shape, gain * fan_in ** -0.5)

    def norm_gain(shape):
        return 1.0 + normal(shape, 0.01)

    return {
        'x': normal((BATCH, SEQ, D)),
        'c': normal((BATCH, D)),
        'ctx': normal((BATCH, CTX_LEN, D)),
        'c_ctx': normal((D,)),
        'w_mod': dense((L, D, 6 * D), D, 0.5),
        'b_mod': normal((L, 6 * D), 0.01),
        'g_norm_mix': norm_gain((L, D)),
        'w_in': dense((L, D, IN_COLS), D),
        'g_q_a': norm_gain((L, A_Q_RANK)),
        'w_q_b': dense((L, A_Q_RANK, A_HEADS * (A_NOPE + A_ROPE)), A_Q_RANK),
        'g_kv_a': norm_gain((L, A_KV_RANK)),
        'w_kv_b': dense((L, A_KV_RANK, A_HEADS * (A_NOPE + A_V)), A_KV_RANK),
        'sink_b': normal((L, B_HEADS), 0.5),
        'rpb_c': normal((L, C_HEADS, 2 * NA_KH - 1, 2 * NA_KW - 1), 0.1),
        'g_q_d': norm_gain((L, HEAD_DIM)),
        'g_k_d': norm_gain((L, HEAD_DIM)),
        'w_gate': dense((L, N_BRANCH, D, D), D),
        'b_gate': normal((L, N_BRANCH, D), 0.01),
        'w_branch': dense((L, N_BRANCH, BRANCH_W, D), BRANCH_W),
        'w_out': dense((L, D, D), D),
        'g_norm_ffn': norm_gain((L, D)),
        'w_group': dense((L, D, N_GROUPS), D),
        'b_group': normal((L, N_GROUPS), 0.01),
        'w_router': dense((L, D, N_EXPERTS), D),
        'b_router': normal((L, N_EXPERTS), 0.01),
        'w_ff1': dense((L, N_EXPERTS, D, EXPERT_FF), D),
        'w_ff3': dense((L, N_EXPERTS, D, EXPERT_FF), D),
        'w_ff2': dense((L, N_EXPERTS, EXPERT_FF, D), EXPERT_FF),
        'g_final': norm_gain((D,)),
    }


def reference(x, c, ctx, c_ctx, w_mod, b_mod, g_norm_mix, w_in, g_q_a, w_q_b, g_kv_a, w_kv_b,
              sink_b, rpb_c, g_q_d, g_k_d, w_gate, b_gate, w_branch, w_out, g_norm_ffn,
              w_group, b_group, w_router, b_router, w_ff1, w_ff3, w_ff2, g_final):
    s = x.shape[1]
    n_ctx = ctx.shape[1]
    rows = s // GRID_W
    rope_a = rope_tables(s, A_ROPE)
    rope_h = rope_tables(s, HEAD_DIM)
    silu_c = jax.nn.silu(c)
    silu_cc = jax.nn.silu(c_ctx)
    for l in range(DEPTH):
        with_ctx = l < DEPTH - 1
        mod = (silu_c @ w_mod[l] + b_mod[l])[:, None, :]
        mod_c = silu_cc @ w_mod[l] + b_mod[l]
        sh1, sc1, gt1, sh2, sc2, gt2 = jnp.split(mod, 6, axis=-1)
        csh1, csc1, cgt1, csh2, csc2, cgt2 = jnp.split(mod_c, 6, axis=-1)
        h = modulate(x, g_norm_mix[l], sh1, sc1)
        hc = modulate(ctx, g_norm_mix[l], csh1, csc1)
        y, yc = token_mixer(h, hc, rows, rope_a, rope_h, w_in[l], g_q_a[l], w_q_b[l], g_kv_a[l], w_kv_b[l],
                            sink_b[l], rpb_c[l], g_q_d[l], g_k_d[l], w_gate[l], b_gate[l], w_branch[l],
                            w_out[l], with_ctx)
        x = x + gt1 * y
        h2 = modulate(x, g_norm_ffn[l], sh2, sc2)
        if with_ctx:
            ctx = ctx + cgt1 * yc
            h2c = modulate(ctx, g_norm_ffn[l], csh2, csc2)
            f = hier_moe(jnp.concatenate([h2c, h2], axis=1), w_group[l], b_group[l], w_router[l],
                         b_router[l], w_ff1[l], w_ff3[l], w_ff2[l])
            ctx = ctx + cgt2 * f[:, :n_ctx]
            x = x + gt2 * f[:, n_ctx:]
        else:
            x = x + gt2 * hier_moe(h2, w_group[l], b_group[l], w_router[l], b_router[l],
                                   w_ff1[l], w_ff3[l], w_ff2[l])
    return rmsnorm(x, g_final)
```

```python
import functools

import numpy as np
import jax
import jax.numpy as jnp
from jax import lax
from jax.experimental import pallas as pl
from jax.experimental.pallas import tpu as pltpu

F32 = jnp.float32
BF16 = jnp.bfloat16
HIGHEST = lax.Precision.HIGHEST

GRID_W = 64
ROPE_THETA = 10000.0
EPS = 1e-6
NEG_INF = -1e30
HEAD_DIM = 64
N_HEADS = 4
BRANCH_W = 256
A_Q_RANK = 256
A_KV_RANK = 128
A_NOPE = 64
A_ROPE = 32
NA_KH = 8
NA_KW = 16
WINDOW = 128
N_GROUPS = 4
EXPERTS_PER_GROUP = 8
N_EXPERTS = 32
EXPERT_FF = 256

LANES = 128
SUBLANES = 8
ROW_BLK = 256
VMEM_LIMIT = 56 * 1024 * 1024

_PROJ_GROUPS = (("cq", 256), ("ckv", 128), ("kr", 128), ("qb", 256), ("kb", 128), ("vb", 128),
                ("qc", 256), ("kc", 256), ("vc", 256), ("qd", 256), ("kd", 128), ("vd", 128))
_PROJ_OFF = {}
_o = 0
for _n, _w in _PROJ_GROUPS:
    _PROJ_OFF[_n] = (_o, _w)
    _o += _w
PROJ_COLS = _o
_GQA_PERM = (0, 2, 1, 3)


def _cparams(sem):
    return pltpu.CompilerParams(dimension_semantics=sem, vmem_limit_bytes=VMEM_LIMIT)


def _lane_iota(shape):
    return lax.broadcasted_iota(jnp.int32, shape, len(shape) - 1)


def _sigmoid(x):
    return 1.0 / (1.0 + jnp.exp(-x))


def _mod_kernel(c_ref, w_ref, b_ref, o_ref):
    cf = c_ref[...]
    s = cf * _sigmoid(cf)
    o_ref[0] = jnp.dot(s, w_ref[0], precision=HIGHEST, preferred_element_type=F32) + b_ref[0]


def _modulation(c_all, w_mod, b_mod):
    n_layers, d, n_out = w_mod.shape
    rows = c_all.shape[0]
    tn = 1536
    return pl.pallas_call(
        _mod_kernel,
        out_shape=jax.ShapeDtypeStruct((n_layers, rows, n_out), F32),
        grid=(n_layers, n_out // tn),
        in_specs=[pl.BlockSpec((rows, d), lambda l, j: (0, 0)),
                  pl.BlockSpec((1, d, tn), lambda l, j: (l, 0, j)),
                  pl.BlockSpec((1, 1, tn), lambda l, j: (l, 0, j))],
        out_specs=pl.BlockSpec((1, rows, tn), lambda l, j: (l, 0, j)),
        compiler_params=_cparams(("arbitrary", "arbitrary")),
        name="modulation",
    )(c_all, w_mod, b_mod.reshape(n_layers, 1, n_out))


def _rms(x):
    return x * lax.rsqrt(jnp.mean(x * x, axis=-1, keepdims=True) + EPS)


def _swap_blocks(x, blk):
    lane = _lane_iota(x.shape)
    up = pltpu.roll(x, LANES - blk, 1)
    dn = pltpu.roll(x, blk, 1)
    return jnp.where((lane // blk) % 2 == 0, up, dn)


def _rope(x, cos, sin, blk):
    return x * cos + _swap_blocks(x, blk) * sin


def _pair_norm(x, g):
    lo = _lane_iota(x.shape) < HEAD_DIM
    sq = x * x
    s_lo = jnp.sum(jnp.where(lo, sq, 0.0), axis=-1, keepdims=True)
    s_hi = jnp.sum(jnp.where(lo, 0.0, sq), axis=-1, keepdims=True)
    ms = jnp.where(lo, s_lo, s_hi) * (1.0 / HEAD_DIM)
    return x * lax.rsqrt(ms + EPS) * g


def _load_token_major(ref, rows):
    return jnp.concatenate(
        [ref[pl.ds(c, rows, stride=SUBLANES), :] for c in range(SUBLANES)], axis=-1)


def _proj_kernel(*refs, with_f, tm, scale_a):
    it = iter(refs)
    x_ref = next(it)
    if with_f:
        f_ref = next(it)
        modp_ref = next(it)
    mod_ref = next(it)
    gmix_ref, win_ref, gqa_ref, wqb_ref, gkva_ref, wkvb_ref, gqd_ref, gkd_ref = (next(it) for _ in range(8))
    cosh_ref, sinh_ref, cosa_ref, sina_ref = (next(it) for _ in range(4))
    if with_f:
        x2_ref = next(it)
    h_ref = next(it)
    qa_ref, ka_ref, va_ref, qb_ref, kb_ref, vb_ref, qc_ref, kc_ref, vc_ref, qd_ref, kd_ref, vd_ref = (
        next(it) for _ in range(12))

    xf = x_ref[...]
    if with_f:
        xf = xf + modp_ref[0, 5:6, :] * _load_token_major(f_ref, tm)
        x2_ref[...] = xf
    h = _rms(xf) * gmix_ref[...] * (1.0 + mod_ref[0, 1:2, :]) + mod_ref[0, 0:1, :]
    hb = h.astype(BF16)
    h_ref[...] = hb
    p = jnp.dot(hb, win_ref[...], preferred_element_type=F32)

    def grp(name):
        o, w = _PROJ_OFF[name]
        return p[:, o:o + w]

    cosh, sinh = cosh_ref[...], sinh_ref[...]
    cosa, sina = cosa_ref[...], sina_ref[...]

    cq = (_rms(grp("cq")) * gqa_ref[...]).astype(BF16)
    qa = jnp.dot(cq, wqb_ref[...], preferred_element_type=F32)
    for hd in range(N_HEADS):
        sl = slice(hd * LANES, (hd + 1) * LANES)
        qa_ref[:, sl] = (_rope(qa[:, sl], cosa, sina, 8) * scale_a).astype(BF16)
    ckv = (_rms(grp("ckv")) * gkva_ref[...]).astype(BF16)
    kva = jnp.dot(ckv, wkvb_ref[...], preferred_element_type=F32)
    kr = _rope(grp("kr"), cosa, sina, 8)
    for hd in range(N_HEADS):
        sl = slice(hd * LANES, (hd + 1) * LANES)
        ka_ref[:, sl] = (kva[:, sl] + kr).astype(BF16)
    va_ref[...] = kva[:, N_HEADS * LANES:].astype(BF16)

    qb = grp("qb")
    for j in range(2):
        sl = slice(j * LANES, (j + 1) * LANES)
        qb_ref[:, sl] = _rope(qb[:, sl], cosh, sinh, 16).astype(BF16)
    kb_ref[...] = _rope(grp("kb"), cosh, sinh, 16).astype(BF16)
    vb_ref[...] = grp("vb").astype(BF16)

    qc_ref[...] = grp("qc").astype(BF16)
    kc_ref[...] = grp("kc").astype(BF16)
    vc_ref[...] = grp("vc").astype(BF16)

    qd = grp("qd")
    for j in range(2):
        sl = slice(j * LANES, (j + 1) * LANES)
        qd_ref[:, sl] = _rope(_pair_norm(qd[:, sl], gqd_ref[...]), cosh, sinh, 16).astype(BF16)
    kd_ref[...] = _rope(_pair_norm(grp("kd"), gkd_ref[...]), cosh, sinh, 16).astype(BF16)
    vd_ref[...] = grp("vd").astype(BF16)


def _proj_call(lay, x, f, modp, mod, layer, w, tabs):
    T, D, tm = lay["T"], lay["D"], lay["tm"]
    nct, tpb, B = lay["nct"], lay["tpb"], lay["B"]
    with_f = f is not None
    n_tiles = T // tm

    def mod_row(i):
        return jnp.where(i < nct, B, (i - nct) // tpb)

    def tab_blk(i):
        return jnp.where(i < nct, tpb, (i - nct) % tpb)

    row = lambda i: (i, 0)
    const = lambda i: (0, 0)
    in_specs = [pl.BlockSpec((tm, D), row)]
    args = [x]
    if with_f:
        in_specs += [pl.BlockSpec((tm * SUBLANES, LANES), row),
                     pl.BlockSpec((1, 6, D), lambda i: ((layer - 1) * lay["mod_rows"] + mod_row(i), 0, 0))]
        args += [f, modp]
    in_specs += [pl.BlockSpec((1, 6, D), lambda i: (layer * lay["mod_rows"] + mod_row(i), 0, 0))]
    args += [mod]
    for name in ("g_mix", "w_in", "g_q_a", "w_q_b", "g_kv_a", "w_kv_b", "g_q_d", "g_k_d"):
        a = w[name]
        in_specs.append(pl.BlockSpec(a.shape, const))
        args.append(a)
    for tname in ("cos_h", "sin_h", "cos_a", "sin_a"):
        in_specs.append(pl.BlockSpec((tm, LANES), lambda i: (tab_blk(i), 0)))
        args.append(tabs[tname])

    widths = [("h", D), ("qa", 512), ("ka", 512), ("va", 256), ("qb", 256), ("kb", 128), ("vb", 128),
              ("qc", 256), ("kc", 256), ("vc", 256), ("qd", 256), ("kd", 128), ("vd", 128)]
    out_shape, out_specs = [], []
    if with_f:
        out_shape.append(jax.ShapeDtypeStruct((T, D), F32))
        out_specs.append(pl.BlockSpec((tm, D), row))
    for _, wd in widths:
        out_shape.append(jax.ShapeDtypeStruct((T, wd), BF16))
        out_specs.append(pl.BlockSpec((tm, wd), row))

    outs = pl.pallas_call(
        functools.partial(_proj_kernel, with_f=with_f, tm=tm, scale_a=float((A_NOPE + A_ROPE) ** -0.5)),
        out_shape=out_shape, grid=(n_tiles,), in_specs=in_specs, out_specs=out_specs,
        compiler_params=_cparams(("parallel",)),
        name="proj_in",
    )(*args)
    outs = list(outs)
    x2 = outs.pop(0) if with_f else x
    names = [n for n, _ in widths]
    return x2, dict(zip(names, outs))


def _attn_groups(kind):
    if kind == "A":
        return tuple((r, r + 1, r, r // 2) for r in range(N_HEADS))
    if kind == "C":
        return ((0, 2, 0, 0), (2, 4, 1, 1))
    return ((0, 4, 0, 0),)


def _attn_kernel(*refs, kind, lay, q_off, n_steps):
    tq = ROW_BLK
    ncb, lb, S = lay["ncb"], lay["lb"], lay["S"]
    it = iter(refs)
    q_ref, kc_ref, vc_ref, kl_ref, vl_ref = (next(it) for _ in range(5))
    sink_ref = next(it) if kind == "B" else None
    bias_ref = next(it) if kind == "C" else None
    o_ref = next(it)
    qs_ref, m_ref, l_ref, acc_ref = (next(it) for _ in range(4))

    i = pl.program_id(0) + q_off
    s = pl.program_id(1)
    is_lat = i >= ncb
    t = jnp.where(is_lat, (i - ncb) % lb, 0)
    groups = _attn_groups(kind)

    def process(k_ref, v_ref, lat):
        for (r0, r1, ks, vs) in groups:
            rows = slice(r0 * tq, r1 * tq)
            nr = (r1 - r0) * tq
            kk = k_ref[:, ks * LANES:(ks + 1) * LANES]
            sc = lax.dot_general(qs_ref[rows, :], kk, (((1,), (1,)), ((), ())),
                                 preferred_element_type=F32)
            if lat and kind == "B":
                tk = sc.shape[1]
                qpos = t * tq + (lax.broadcasted_iota(jnp.int32, (nr, tk), 0) % tq)
                kpos = (t + s - 2) * tk + lax.broadcasted_iota(jnp.int32, (nr, tk), 1)
                ok = (jnp.abs(qpos - kpos) <= WINDOW) & (kpos >= 0) & (kpos < S)
                sc = jnp.where(ok, sc, NEG_INF)
            if lat and kind == "C":
                var = jnp.where(t == 0, 0, jnp.where(t == lb - 1, 2, 1))
                sc = sc + bias_ref[var, s - 1, rows, :]
            m_prev = m_ref[rows, :]
            m_new = jnp.maximum(m_prev, jnp.max(sc, axis=-1, keepdims=True))
            alpha = jnp.exp(m_prev - m_new)
            p = jnp.exp(sc - m_new)
            l_ref[rows, :] = alpha * l_ref[rows, :] + jnp.sum(p, axis=-1, keepdims=True)
            m_ref[rows, :] = m_new
            vv = v_ref[:, vs * LANES:(vs + 1) * LANES]
            acc_ref[rows, :] = alpha * acc_ref[rows, :] + jnp.dot(
                p.astype(BF16), vv, preferred_element_type=F32)

    @pl.when(s == 0)
    def _():
        lane = _lane_iota((tq, LANES))
        for r in range(N_HEADS):
            if kind == "A":
                blk = q_ref[:, r * LANES:(r + 1) * LANES]
            else:
                src = q_ref[:, (r // 2) * LANES:(r // 2 + 1) * LANES]
                keep = (lane < HEAD_DIM) if r % 2 == 0 else (lane >= HEAD_DIM)
                blk = jnp.where(keep, src, jnp.zeros_like(src))
            qs_ref[r * tq:(r + 1) * tq, :] = blk
            if kind == "B":
                m_ref[r * tq:(r + 1) * tq, :] = jnp.full((tq, 1), sink_ref[_GQA_PERM[r]], F32)
                l_ref[r * tq:(r + 1) * tq, :] = jnp.ones((tq, 1), F32)
            else:
                m_ref[r * tq:(r + 1) * tq, :] = jnp.full((tq, 1), NEG_INF, F32)
                l_ref[r * tq:(r + 1) * tq, :] = jnp.zeros((tq, 1), F32)
        acc_ref[...] = jnp.zeros_like(acc_ref)
        process(kc_ref, vc_ref, False)

    @pl.when((s > 0) & is_lat)
    def _():
        process(kl_ref, vl_ref, True)

    @pl.when(s == n_steps - 1)
    def _():
        lane = _lane_iota((tq, LANES))
        for g in range(2):
            halves = []
            for j in range(2):
                r = 2 * g + j
                rows = slice(r * tq, (r + 1) * tq)
                halves.append(acc_ref[rows, :] / l_ref[rows, :])
            o_ref[:, g * LANES:(g + 1) * LANES] = jnp.where(
                lane < HEAD_DIM, halves[0], halves[1]).astype(o_ref.dtype)


def _attn_call(lay, kind, q, k, v, *, with_ctx, sink=None, bias=None):
    T, B, ncb, lb, S = lay["T"], lay["B"], lay["ncb"], lay["lb"], lay["S"]
    tq = ROW_BLK
    dense = kind in ("A", "D")
    tk = lay["tk_dense"] if dense else ROW_BLK
    n_steps = 1 + (S // tk if dense else 3)
    q_off = 0 if with_ctx else ncb
    n_q = ncb + B * lb - q_off
    qw, kw, vw = q.shape[1], k.shape[1], v.shape[1]
    lat0 = ncb * ROW_BLK // tk
    per_b = S // tk

    def batch_of(i):
        return jnp.where(i < ncb, i, (i - ncb) // lb)

    def lat_blk(i, s):
        ii = i + q_off
        b = batch_of(ii)
        if dense:
            return lat0 + b * per_b + jnp.maximum(s - 1, 0)
        t = jnp.where(ii < ncb, 0, (ii - ncb) % lb)
        nb = jnp.clip(t + jnp.maximum(s, 1) - 2, 0, lb - 1)
        return lat0 + b * per_b + nb

    in_specs = [
        pl.BlockSpec((tq, qw), lambda i, s: (i + q_off, 0)),
        pl.BlockSpec((ROW_BLK, kw), lambda i, s: (batch_of(i + q_off), 0)),
        pl.BlockSpec((ROW_BLK, vw), lambda i, s: (batch_of(i + q_off), 0)),
        pl.BlockSpec((tk, kw), lambda i, s: (lat_blk(i, s), 0)),
        pl.BlockSpec((tk, vw), lambda i, s: (lat_blk(i, s), 0)),
    ]
    args = [q, k, v, k, v]
    if kind == "B":
        in_specs.append(pl.BlockSpec(memory_space=pltpu.SMEM))
        args.append(sink)
    if kind == "C":
        in_specs.append(pl.BlockSpec(bias.shape, lambda i, s: (0, 0, 0, 0)))
        args.append(bias)
    return pl.pallas_call(
        functools.partial(_attn_kernel, kind=kind, lay=lay, q_off=q_off, n_steps=n_steps),
        out_shape=jax.ShapeDtypeStruct((T, BRANCH_W), BF16),
        grid=(n_q, n_steps),
        in_specs=in_specs,
        out_specs=pl.BlockSpec((tq, BRANCH_W), lambda i, s: (i + q_off, 0)),
        scratch_shapes=[pltpu.VMEM((N_HEADS * tq, LANES), BF16),
                        pltpu.VMEM((N_HEADS * tq, 1), F32),
                        pltpu.VMEM((N_HEADS * tq, 1), F32),
                        pltpu.VMEM((N_HEADS * tq, LANES), F32)],
        compiler_params=_cparams(("parallel", "arbitrary")),
        name="attn_" + kind,
    )(*args)


def _neighbourhood_bias(rpb, rows_total):
    lb = rows_total * GRID_W // ROW_BLK
    rpt = ROW_BLK // GRID_W
    kh = min(NA_KH, rows_total)
    a = np.arange(ROW_BLK)
    q_sub, q_col = a // GRID_W, a % GRID_W
    out = []
    for t_rep in (0, 1, lb - 1):
        per_delta = []
        for delta in (-1, 0, 1):
            kt = t_rep + delta
            q_row = t_rep * rpt + q_sub
            k_row = kt * rpt + q_sub
            k_col = q_col
            r_start = np.clip(q_row - kh // 2, 0, rows_total - kh)
            row_ok = (k_row[None] >= r_start[:, None]) & (k_row[None] < r_start[:, None] + kh)
            c_start = np.clip(q_col - NA_KW // 2, 0, GRID_W - NA_KW)
            col_ok = (k_col[None] >= c_start[:, None]) & (k_col[None] < c_start[:, None] + NA_KW)
            ok = row_ok & col_ok & (0 <= kt < lb)
            dr = np.clip(k_row[None] - q_row[:, None], -(NA_KH - 1), NA_KH - 1) + NA_KH - 1
            dc = np.clip(k_col[None] - q_col[:, None], -(NA_KW - 1), NA_KW - 1) + NA_KW - 1
            vals = rpb[:, dr, dc].astype(F32)
            per_delta.append(jnp.where(jnp.asarray(ok)[None], vals, NEG_INF).reshape(-1, ROW_BLK))
        out.append(jnp.stack(per_delta))
    return jnp.stack(out)


def _merge_kernel(h_ref, oa_ref, ob_ref, oc_ref, od_ref, x_ref, mod_ref, gffn_ref, wg_ref, bg_ref,
                  wb_ref, wout_ref, wr_ref, br_ref, x1_ref, h2_ref, route_ref, *, tm):
    hb = h_ref[...]
    y = None
    for n, o_ref in enumerate((oa_ref, ob_ref, oc_ref, od_ref)):
        gate = _sigmoid(jnp.dot(hb, wg_ref[n], preferred_element_type=F32) + bg_ref[n])
        t = gate * jnp.dot(o_ref[...], wb_ref[n], preferred_element_type=F32)
        y = t if y is None else y + t
    z = jnp.dot(y.astype(BF16), wout_ref[...], preferred_element_type=F32)
    x1 = x_ref[...] + mod_ref[0, 2:3, :] * z
    x1_ref[...] = x1
    h2 = _rms(x1) * gffn_ref[...] * (1.0 + mod_ref[0, 4:5, :]) + mod_ref[0, 3:4, :]
    for c in range(SUBLANES):
        h2_ref[pl.ds(c, tm, stride=SUBLANES), :] = h2[:, c * LANES:(c + 1) * LANES]

    logit = jnp.dot(h2, wr_ref[...], precision=HIGHEST, preferred_element_type=F32) + br_ref[...]
    lane = _lane_iota(logit.shape)
    big = jnp.int32(1 << 20)
    is_g = (lane >= N_EXPERTS) & (lane < N_EXPERTS + N_GROUPS)
    gl = jnp.where(is_g, logit, NEG_INF)
    gmax = jnp.max(gl, axis=-1, keepdims=True)
    gsel = jnp.min(jnp.where(gl == gmax, lane - N_EXPERTS, big), axis=-1, keepdims=True)
    gw = 1.0 / jnp.sum(jnp.where(is_g, jnp.exp(gl - gmax), 0.0), axis=-1, keepdims=True)
    in_grp = (lane < N_EXPERTS) & ((lane // EXPERTS_PER_GROUP) == gsel)
    el = jnp.where(in_grp, logit, NEG_INF)
    v1 = jnp.max(el, axis=-1, keepdims=True)
    i1 = jnp.min(jnp.where(el == v1, lane, big), axis=-1, keepdims=True)
    el2 = jnp.where(lane == i1, NEG_INF, el)
    v2 = jnp.max(el2, axis=-1, keepdims=True)
    i2 = jnp.min(jnp.where(el2 == v2, lane, big), axis=-1, keepdims=True)
    e21 = jnp.exp(v2 - v1)
    w1 = gw / (1.0 + e21)
    w2 = gw * e21 / (1.0 + e21)
    route_ref[...] = jnp.where(lane == 0, i1.astype(F32),
                               jnp.where(lane == 1, i2.astype(F32),
                                         jnp.where(lane == 2, w1, jnp.where(lane == 3, w2, 0.0))))


def _merge_call(lay, layer, h, o, x, mod, w, *, with_ctx):
    T, D, tm = lay["T"], lay["D"], lay["tm"]
    nct, tpb, B = lay["nct"], lay["tpb"], lay["B"]
    off = 0 if with_ctx else nct
    n_tiles = T // tm - off

    def mod_row(i):
        return jnp.where(i < nct, B, (i - nct) // tpb)

    row = lambda i: (i + off, 0)
    const2 = lambda i: (0, 0)
    const3 = lambda i: (0, 0, 0)
    in_specs = [pl.BlockSpec((tm, D), row)]
    in_specs += [pl.BlockSpec((tm, BRANCH_W), row)] * 4
    in_specs += [pl.BlockSpec((tm, D), row),
                 pl.BlockSpec((1, 6, D), lambda i: (layer * lay["mod_rows"] + mod_row(i + off), 0, 0)),
                 pl.BlockSpec((1, D), const2),
                 pl.BlockSpec(w["w_gate"].shape, const3),
                 pl.BlockSpec(w["b_gate"].shape, const3),
                 pl.BlockSpec(w["w_branch"].shape, const3),
                 pl.BlockSpec((D, D), const2),
                 pl.BlockSpec((D, LANES), const2),
                 pl.BlockSpec((1, LANES), const2)]
    out_shape = [jax.ShapeDtypeStruct((T, D), F32),
                 jax.ShapeDtypeStruct((T * SUBLANES, LANES), F32),
                 jax.ShapeDtypeStruct((T, LANES), F32)]
    out_specs = [pl.BlockSpec((tm, D), row),
                 pl.BlockSpec((tm * SUBLANES, LANES), row),
                 pl.BlockSpec((tm, LANES), row)]
    return pl.pallas_call(
        functools.partial(_merge_kernel, tm=tm),
        out_shape=out_shape, grid=(n_tiles,), in_specs=in_specs, out_specs=out_specs,
        compiler_params=_cparams(("parallel",)),
        name="merge",
    )(h, o["A"], o["B"], o["C"], o["D"], x, mod, w["g_ffn"], w["w_gate"], w["b_gate"],
      w["w_branch"], w["w_out"], w["w_route"], w["b_route"])


def _moe_kernel(tab_ref, tok_ref, h2_ref, sw_ref, w1_ref, w3_ref, w2_ref, f_ref,
                xg_ref, yb_ref, *, blk, slots, tile_off):
    ti = pl.program_id(0)
    e = pl.program_id(1)

    @pl.when(e == 0)
    def _():
        f_ref[...] = jnp.zeros_like(f_ref)

    base = ((ti + tile_off) * N_EXPERTS + e) * 2
    seg0 = tab_ref[base]
    n_e = tab_ref[base + 1]
    tok_base = (ti + tile_off) * slots
    n_blocks = (n_e + blk - 1) // blk

    def block_body(b, carry):
        off = pl.multiple_of(seg0 + b * blk, SUBLANES)
        n_valid = jnp.minimum(n_e - b * blk, blk)

        def gather(r, c):
            tok = tok_ref[tok_base + off + r]
            src = pl.multiple_of(tok * SUBLANES, SUBLANES)
            dst = pl.multiple_of(r * SUBLANES, SUBLANES)
            xg_ref[pl.ds(dst, SUBLANES), :] = h2_ref[pl.ds(src, SUBLANES), :]
            return c

        lax.fori_loop(0, blk, gather, 0, unroll=8)
        xb = _load_token_major(xg_ref, blk).astype(BF16)
        a = jnp.dot(xb, w1_ref[0], preferred_element_type=F32)
        g = jnp.dot(xb, w3_ref[0], preferred_element_type=F32)
        hid = (a * _sigmoid(a) * g).astype(BF16)
        y = jnp.dot(hid, w2_ref[0], preferred_element_type=F32)
        y = y * sw_ref[0, pl.ds(off, blk), :]
        for c in range(SUBLANES):
            yb_ref[pl.ds(c, blk, stride=SUBLANES), :] = y[:, c * LANES:(c + 1) * LANES]

        def scatter(r, c):
            tok = tok_ref[tok_base + off + r]
            dst = pl.multiple_of(tok * SUBLANES, SUBLANES)
            src = pl.multiple_of(r * SUBLANES, SUBLANES)
            f_ref[pl.ds(dst, SUBLANES), :] += yb_ref[pl.ds(src, SUBLANES), :]
            return c

        lax.fori_loop(0, n_valid, scatter, 0)
        return carry

    lax.fori_loop(0, n_blocks, block_body, 0)


def _moe_call(lay, route, h2, w, *, with_ctx):
    T, D, tt = lay["T"], lay["D"], lay["tt"]
    blk = lay["moe_blk"]
    n_tiles_all = T // tt
    tile_off = 0 if with_ctx else lay["ncb"] * ROW_BLK // tt
    n_tiles = n_tiles_all - tile_off
    n_assign = 2 * tt
    slots = n_assign + N_EXPERTS * SUBLANES + blk

    eid = route[:, 0:2].astype(jnp.int32).reshape(n_tiles_all, n_assign)
    wts = route[:, 2:4].reshape(n_tiles_all, n_assign)
    order = jnp.argsort(eid, axis=1, stable=True)
    e_sorted = jnp.take_along_axis(eid, order, axis=1)
    w_sorted = jnp.take_along_axis(wts, order, axis=1)
    tok_sorted = (order // 2).astype(jnp.int32)
    ar = jnp.arange(N_EXPERTS + 1, dtype=jnp.int32)
    starts = jax.vmap(lambda es: jnp.searchsorted(es, ar, side="left"))(e_sorted).astype(jnp.int32)
    counts = starts[:, 1:] - starts[:, :-1]
    padded = (counts + SUBLANES - 1) // SUBLANES * SUBLANES
    seg0 = (jnp.cumsum(padded, axis=1) - padded).astype(jnp.int32)
    rank = jnp.arange(n_assign, dtype=jnp.int32)[None] - jnp.take_along_axis(starts[:, :-1], e_sorted, axis=1)
    dest = jnp.take_along_axis(seg0, e_sorted, axis=1) + rank
    tile_ix = jnp.arange(n_tiles_all, dtype=jnp.int32)[:, None]
    slot_tok = jnp.zeros((n_tiles_all, slots), jnp.int32).at[tile_ix, dest].set(tok_sorted)
    slot_w = jnp.zeros((n_tiles_all, slots), F32).at[tile_ix, dest].set(w_sorted)
    tab = jnp.stack([seg0, counts], axis=-1).reshape(-1)

    return pl.pallas_call(
        functools.partial(_moe_kernel, blk=blk, slots=slots, tile_off=tile_off),
        out_shape=jax.ShapeDtypeStruct((T * SUBLANES, LANES), F32),
        grid_spec=pltpu.PrefetchScalarGridSpec(
            num_scalar_prefetch=2,
            grid=(n_tiles, N_EXPERTS),
            in_specs=[
                pl.BlockSpec((tt * SUBLANES, LANES), lambda t, e, *_: (t + tile_off, 0)),
                pl.BlockSpec((1, slots, 1), lambda t, e, *_: (t + tile_off, 0, 0)),
                pl.BlockSpec((1, D, EXPERT_FF), lambda t, e, *_: (e, 0, 0)),
                pl.BlockSpec((1, D, EXPERT_FF), lambda t, e, *_: (e, 0, 0)),
                pl.BlockSpec((1, EXPERT_FF, D), lambda t, e, *_: (e, 0, 0)),
            ],
            out_specs=pl.BlockSpec((tt * SUBLANES, LANES), lambda t, e, *_: (t + tile_off, 0)),
            scratch_shapes=[pltpu.VMEM((blk * SUBLANES, LANES), F32),
                            pltpu.VMEM((blk * SUBLANES, LANES), F32)],
        ),
        compiler_params=_cparams(("parallel", "arbitrary")),
        name="moe_experts",
    )(tab, slot_tok.reshape(-1), h2, slot_w.reshape(n_tiles_all, slots, 1),
      w["w_ff1"], w["w_ff3"], w["w_ff2"])


def _final_kernel(x_ref, f_ref, mod_ref, g_ref, o_ref, *, tm):
    xf = x_ref[...] + mod_ref[0, 5:6, :] * _load_token_major(f_ref, tm)
    o_ref[...] = _rms(xf) * g_ref[...]


def _final_call(lay, layer, x1, f, mod, g_final):
    T, D, tm = lay["T"], lay["D"], lay["tm"]
    nct, tpb = lay["nct"], lay["tpb"]
    n_lat = T // tm - nct
    return pl.pallas_call(
        functools.partial(_final_kernel, tm=tm),
        out_shape=jax.ShapeDtypeStruct((n_lat * tm, D), F32),
        grid=(n_lat,),
        in_specs=[pl.BlockSpec((tm, D), lambda i: (i + nct, 0)),
                  pl.BlockSpec((tm * SUBLANES, LANES), lambda i: (i + nct, 0)),
                  pl.BlockSpec((1, 6, D), lambda i: (layer * lay["mod_rows"] + i // tpb, 0, 0)),
                  pl.BlockSpec((1, D), lambda i: (0, 0))],
        out_specs=pl.BlockSpec((tm, D), lambda i: (i, 0)),
        compiler_params=_cparams(("parallel",)),
        name="final_norm",
    )(x1, f, mod, g_final)


def _take_cols(wm, idx):
    idx = np.asarray(idx)
    cols = jnp.take(wm, jnp.asarray(np.maximum(idx, 0)), axis=1)
    return jnp.where(jnp.asarray(idx >= 0)[None, :], cols, 0.0)


def _prep_layer(l, p):
    a_cols = A_Q_RANK + A_KV_RANK + A_ROPE
    b_off = a_cols
    c_off = b_off + 512
    d_off = c_off + 768
    ar = np.arange
    none = lambda n: -np.ones(n, np.int64)
    gqa_q = lambda off: np.concatenate([off + hh * HEAD_DIM + ar(HEAD_DIM) for hh in _GQA_PERM])
    idx = np.concatenate([
        ar(256), 256 + ar(128), none(64), 384 + ar(32), none(32),
        gqa_q(b_off), b_off + 256 + ar(128), b_off + 384 + ar(128),
        c_off + ar(256), c_off + 256 + ar(256), c_off + 512 + ar(256),
        gqa_q(d_off), d_off + 256 + ar(128), d_off + 384 + ar(128)])
    w_in = _take_cols(p["w_in"][l], idx)
    col_scale = np.ones(PROJ_COLS, np.float32)
    for name in ("qb", "qc"):
        o, wd = _PROJ_OFF[name]
        col_scale[o:o + wd] = HEAD_DIM ** -0.5
    w_in = (w_in * jnp.asarray(col_scale)[None, :]).astype(BF16)

    hq = A_NOPE + A_ROPE
    idx_q = np.concatenate([np.concatenate([hh * hq + ar(hq), none(LANES - hq)]) for hh in range(N_HEADS)])
    w_q_b = _take_cols(p["w_q_b"][l], idx_q).astype(BF16)
    hk = A_NOPE + HEAD_DIM
    idx_k = np.concatenate([np.concatenate([hh * hk + ar(A_NOPE), none(LANES - A_NOPE)]) for hh in range(N_HEADS)]
                           + [hh * hk + A_NOPE + ar(HEAD_DIM) for hh in range(N_HEADS)])
    w_kv_b = _take_cols(p["w_kv_b"][l], idx_k).astype(BF16)

    perm_rows = np.concatenate([hh * HEAD_DIM + ar(HEAD_DIM) for hh in _GQA_PERM])
    wb = p["w_branch"][l]
    w_branch = jnp.stack([wb[0], wb[1][perm_rows], wb[2], wb[3][perm_rows]]).astype(BF16)

    d = p["w_in"].shape[1]
    w_route = jnp.zeros((d, LANES), F32)
    w_route = w_route.at[:, :N_EXPERTS].set(p["w_router"][l]).at[:, N_EXPERTS:N_EXPERTS + N_GROUPS].set(p["w_group"][l])
    b_route = jnp.zeros((1, LANES), F32)
    b_route = b_route.at[0, :N_EXPERTS].set(p["b_router"][l]).at[0, N_EXPERTS:N_EXPERTS + N_GROUPS].set(p["b_group"][l])
    return {
        "g_mix": p["g_norm_mix"][l][None, :],
        "w_in": w_in,
        "g_q_a": p["g_q_a"][l][None, :],
        "w_q_b": w_q_b,
        "g_kv_a": p["g_kv_a"][l][None, :],
        "w_kv_b": w_kv_b,
        "g_q_d": (jnp.tile(p["g_q_d"][l], 2) * (HEAD_DIM ** -0.5))[None, :],
        "g_k_d": jnp.tile(p["g_k_d"][l], 2)[None, :],
        "sink": p["sink_b"][l],
        "rpb": p["rpb_c"][l],
        "w_gate": p["w_gate"][l].astype(BF16),
        "b_gate": p["b_gate"][l][:, None, :],
        "w_branch": w_branch,
        "w_out": p["w_out"][l].astype(BF16),
        "g_ffn": p["g_norm_ffn"][l][None, :],
        "w_route": w_route,
        "b_route": b_route,
        "w_ff1": p["w_ff1"][l].astype(BF16),
        "w_ff3": p["w_ff3"][l].astype(BF16),
        "w_ff2": p["w_ff2"][l].astype(BF16),
    }


def _rope_tables(S, tm):
    t = jnp.arange(S, dtype=jnp.int32)
    rows = (t // GRID_W).astype(F32)
    cols = (t % GRID_W).astype(F32)

    def cs(rot):
        half = rot // 2
        inv = ROPE_THETA ** (-jnp.arange(0, half, 2, dtype=F32) / half)
        ar_, ac_ = rows[:, None] * inv, cols[:, None] * inv
        cos = jnp.concatenate([jnp.cos(ar_), jnp.cos(ar_), jnp.cos(ac_), jnp.cos(ac_)], axis=-1)
        sin = jnp.concatenate([-jnp.sin(ar_), jnp.sin(ar_), -jnp.sin(ac_), jnp.sin(ac_)], axis=-1)
        return cos, sin

    cos64, sin64 = cs(HEAD_DIM)
    cos32, sin32 = cs(A_ROPE)
    ones = lambda n: jnp.ones((S, n), F32)
    zeros = lambda n: jnp.zeros((S, n), F32)
    tabs = {
        "cos_h": jnp.concatenate([cos64, cos64], axis=-1),
        "sin_h": jnp.concatenate([sin64, sin64], axis=-1),
        "cos_a": jnp.concatenate([ones(A_NOPE), cos32, ones(LANES - A_NOPE - A_ROPE)], axis=-1),
        "sin_a": jnp.concatenate([zeros(A_NOPE), sin32, zeros(LANES - A_NOPE - A_ROPE)], axis=-1),
    }
    ident = {"cos_h": 1.0, "sin_h": 0.0, "cos_a": 1.0, "sin_a": 0.0}
    return {k: jnp.concatenate([v, jnp.full((tm, LANES), ident[k], F32)], axis=0) for k, v in tabs.items()}


def _layout(B, S, n_ctx, D):
    assert n_ctx == ROW_BLK and S % 1024 == 0 and S // GRID_W >= 3 * (ROW_BLK // GRID_W)
    T = B * (n_ctx + S)
    tm = 512 if (B * n_ctx) % 512 == 0 else 256
    tk_dense = 1024 if (B * n_ctx) % 1024 == 0 else (512 if (B * n_ctx) % 512 == 0 else 256)
    tt = 2048 if (B * n_ctx) % 2048 == 0 else B * n_ctx
    assert S % tt == 0
    return {
        "B": B, "S": S, "D": D, "T": T, "tm": tm,
        "ncb": B * n_ctx // ROW_BLK,
        "lb": S // ROW_BLK,
        "nct": B * n_ctx // tm,
        "tpb": S // tm,
        "tk_dense": tk_dense,
        "tt": tt, "moe_blk": 128,
        "mod_rows": 16,
    }


def kernel(x, c, ctx, c_ctx, w_mod, b_mod, g_norm_mix, w_in, g_q_a, w_q_b, g_kv_a, w_kv_b, sink_b, rpb_c,
           g_q_d, g_k_d, w_gate, b_gate, w_branch, w_out, g_norm_ffn, w_group, b_group, w_router, b_router,
           w_ff1, w_ff3, w_ff2, g_final):
    B, S, D = x.shape
    n_ctx = ctx.shape[1]
    depth = w_mod.shape[0]
    lay = _layout(B, S, n_ctx, D)
    params = dict(w_in=w_in, g_norm_mix=g_norm_mix, g_q_a=g_q_a, w_q_b=w_q_b, g_kv_a=g_kv_a, w_kv_b=w_kv_b,
                  sink_b=sink_b, rpb_c=rpb_c, g_q_d=g_q_d, g_k_d=g_k_d, w_gate=w_gate, b_gate=b_gate,
                  w_branch=w_branch, w_out=w_out, g_norm_ffn=g_norm_ffn, w_group=w_group, b_group=b_group,
                  w_router=w_router, b_router=b_router, w_ff1=w_ff1, w_ff3=w_ff3, w_ff2=w_ff2)

    c_all = jnp.zeros((lay["mod_rows"], D), F32).at[:B].set(c).at[B].set(c_ctx)
    mod = _modulation(c_all, w_mod, b_mod).reshape(depth * lay["mod_rows"], 6, D)
    tabs = _rope_tables(S, lay["tm"])
    xf = jnp.concatenate([ctx.reshape(B * n_ctx, D), x.reshape(B * S, D)], axis=0)

    f = None
    for l in range(depth):
        with_ctx = l < depth - 1
        w = _prep_layer(l, params)
        xf, pr = _proj_call(lay, xf, f, mod, mod, l, w, tabs)
        bias = _neighbourhood_bias(w["rpb"], S // GRID_W)
        o = {
            "A": _attn_call(lay, "A", pr["qa"], pr["ka"], pr["va"], with_ctx=with_ctx),
            "B": _attn_call(lay, "B", pr["qb"], pr["kb"], pr["vb"], with_ctx=with_ctx, sink=w["sink"]),
            "C": _attn_call(lay, "C", pr["qc"], pr["kc"], pr["vc"], with_ctx=with_ctx, bias=bias),
            "D": _attn_call(lay, "D", pr["qd"], pr["kd"], pr["vd"], with_ctx=with_ctx),
        }
        xf, h2, route = _merge_call(lay, l, pr["h"], o, xf, mod, w, with_ctx=with_ctx)
        f = _moe_call(lay, route, h2, w, with_ctx=with_ctx)
    out = _final_call(lay, depth - 1, xf, f, mod, g_final[None, :])
    return out.reshape(B, S, D)
```

```python
import functools

import numpy as np
import jax
import jax.numpy as jnp
from jax import lax
from jax.experimental import pallas as pl
from jax.experimental.pallas import tpu as pltpu

F32 = jnp.float32
BF16 = jnp.bfloat16
HIGHEST = lax.Precision.HIGHEST

GRID_W = 64
ROPE_THETA = 10000.0
EPS = 1e-6
NEG_INF = -1e30
HEAD_DIM = 64
N_HEADS = 4
BRANCH_W = 256
A_Q_RANK = 256
A_KV_RANK = 128
A_NOPE = 64
A_ROPE = 32
NA_KH = 8
NA_KW = 16
WINDOW = 128
N_GROUPS = 4
EXPERTS_PER_GROUP = 8
N_EXPERTS = 32
EXPERT_FF = 256

LANES = 128
SUBLANES = 8
ROW_BLK = 256
VMEM_LIMIT = 56 * 1024 * 1024

_PROJ_GROUPS = (("cq", 256), ("ckv", 128), ("kr", 128), ("qb", 256), ("kb", 128), ("vb", 128),
                ("qc", 256), ("kc", 256), ("vc", 256), ("qd", 256), ("kd", 128), ("vd", 128))
_PROJ_OFF = {}
_o = 0
for _n, _w in _PROJ_GROUPS:
    _PROJ_OFF[_n] = (_o, _w)
    _o += _w
PROJ_COLS = _o
_GQA_PERM = (0, 2, 1, 3)


def _cparams(sem):
    return pltpu.CompilerParams(dimension_semantics=sem, vmem_limit_bytes=VMEM_LIMIT)


def _lane_iota(shape):
    return lax.broadcasted_iota(jnp.int32, shape, len(shape) - 1)


def _sigmoid(x):
    return 1.0 / (1.0 + jnp.exp(-x))


def _mod_kernel(c_ref, w_ref, b_ref, o_ref):
    cf = c_ref[...]
    s = cf * _sigmoid(cf)
    o_ref[0] = jnp.dot(s, w_ref[0], precision=HIGHEST, preferred_element_type=F32) + b_ref[0]


def _modulation(c_all, w_mod, b_mod):
    n_layers, d, n_out = w_mod.shape
    rows = c_all.shape[0]
    tn = 1536
    return pl.pallas_call(
        _mod_kernel,
        out_shape=jax.ShapeDtypeStruct((n_layers, rows, n_out), F32),
        grid=(n_layers, n_out // tn),
        in_specs=[pl.BlockSpec((rows, d), lambda l, j: (0, 0)),
                  pl.BlockSpec((1, d, tn), lambda l, j: (l, 0, j)),
                  pl.BlockSpec((1, 1, tn), lambda l, j: (l, 0, j))],
        out_specs=pl.BlockSpec((1, rows, tn), lambda l, j: (l, 0, j)),
        compiler_params=_cparams(("arbitrary", "arbitrary")),
        name="modulation",
    )(c_all, w_mod, b_mod.reshape(n_layers, 1, n_out))


def _rms(x):
    return x * lax.rsqrt(jnp.mean(x * x, axis=-1, keepdims=True) + EPS)


def _swap_blocks(x, blk):
    lane = _lane_iota(x.shape)
    up = pltpu.roll(x, LANES - blk, 1)
    dn = pltpu.roll(x, blk, 1)
    return jnp.where((lane // blk) % 2 == 0, up, dn)


def _rope(x, cos, sin, blk):
    return x * cos + _swap_blocks(x, blk) * sin


def _pair_norm(x, g):
    lo = _lane_iota(x.shape) < HEAD_DIM
    sq = x * x
    s_lo = jnp.sum(jnp.where(lo, sq, 0.0), axis=-1, keepdims=True)
    s_hi = jnp.sum(jnp.where(lo, 0.0, sq), axis=-1, keepdims=True)
    ms = jnp.where(lo, s_lo, s_hi) * (1.0 / HEAD_DIM)
    return x * lax.rsqrt(ms + EPS) * g


def _load_token_major(ref, rows):
    return jnp.concatenate(
        [ref[pl.ds(c, rows, stride=SUBLANES), :] for c in range(SUBLANES)], axis=-1)


def _proj_kernel(*refs, with_f, tm, scale_a):
    it = iter(refs)
    x_ref = next(it)
    if with_f:
        f_ref = next(it)
        modp_ref = next(it)
    mod_ref = next(it)
    gmix_ref, win_ref, gqa_ref, wqb_ref, gkva_ref, wkvb_ref, gqd_ref, gkd_ref = (next(it) for _ in range(8))
    cosh_ref, sinh_ref, cosa_ref, sina_ref = (next(it) for _ in range(4))
    if with_f:
        x2_ref = next(it)
    h_ref = next(it)
    qa_ref, ka_ref, va_ref, qb_ref, kb_ref, vb_ref, qc_ref, kc_ref, vc_ref, qd_ref, kd_ref, vd_ref = (
        next(it) for _ in range(12))

    xf = x_ref[...]
    if with_f:
        xf = xf + modp_ref[0, 5:6, :] * _load_token_major(f_ref, tm)
        x2_ref[...] = xf
    h = _rms(xf) * gmix_ref[...] * (1.0 + mod_ref[0, 1:2, :]) + mod_ref[0, 0:1, :]
    hb = h.astype(BF16)
    h_ref[...] = hb
    p = jnp.dot(hb, win_ref[...], preferred_element_type=F32)

    def grp(name):
        o, w = _PROJ_OFF[name]
        return p[:, o:o + w]

    cosh, sinh = cosh_ref[...], sinh_ref[...]
    cosa, sina = cosa_ref[...], sina_ref[...]

    cq = (_rms(grp("cq")) * gqa_ref[...]).astype(BF16)
    qa = jnp.dot(cq, wqb_ref[...], preferred_element_type=F32)
    for hd in range(N_HEADS):
        sl = slice(hd * LANES, (hd + 1) * LANES)
        qa_ref[:, sl] = (_rope(qa[:, sl], cosa, sina, 8) * scale_a).astype(BF16)
    ckv = (_rms(grp("ckv")) * gkva_ref[...]).astype(BF16)
    kva = jnp.dot(ckv, wkvb_ref[...], preferred_element_type=F32)
    kr = _rope(grp("kr"), cosa, sina, 8)
    for hd in range(N_HEADS):
        sl = slice(hd * LANES, (hd + 1) * LANES)
        ka_ref[:, sl] = (kva[:, sl] + kr).astype(BF16)
    va_ref[...] = kva[:, N_HEADS * LANES:].astype(BF16)

    qb = grp("qb")
    for j in range(2):
        sl = slice(j * LANES, (j + 1) * LANES)
        qb_ref[:, sl] = _rope(qb[:, sl], cosh, sinh, 16).astype(BF16)
    kb_ref[...] = _rope(grp("kb"), cosh, sinh, 16).astype(BF16)
    vb_ref[...] = grp("vb").astype(BF16)

    qc_ref[...] = grp("qc").astype(BF16)
    kc_ref[...] = grp("kc").astype(BF16)
    vc_ref[...] = grp("vc").astype(BF16)

    qd = grp("qd")
    for j in range(2):
        sl = slice(j * LANES, (j + 1) * LANES)
        qd_ref[:, sl] = _rope(_pair_norm(qd[:, sl], gqd_ref[...]), cosh, sinh, 16).astype(BF16)
    kd_ref[...] = _rope(_pair_norm(grp("kd"), gkd_ref[...]), cosh, sinh, 16).astype(BF16)
    vd_ref[...] = grp("vd").astype(BF16)


def _proj_call(lay, x, f, modp, mod, layer, w, tabs):
    T, D, tm = lay["T"], lay["D"], lay["tm"]
    nct, tpb, B = lay["nct"], lay["tpb"], lay["B"]
    with_f = f is not None
    n_tiles = T // tm

    def mod_row(i):
        return jnp.where(i < nct, B, (i - nct) // tpb)

    def tab_blk(i):
        return jnp.where(i < nct, tpb, (i - nct) % tpb)

    row = lambda i: (i, 0)
    const = lambda i: (0, 0)
    in_specs = [pl.BlockSpec((tm, D), row)]
    args = [x]
    if with_f:
        in_specs += [pl.BlockSpec((tm * SUBLANES, LANES), row),
                     pl.BlockSpec((1, 6, D), lambda i: ((layer - 1) * lay["mod_rows"] + mod_row(i), 0, 0))]
        args += [f, modp]
    in_specs += [pl.BlockSpec((1, 6, D), lambda i: (layer * lay["mod_rows"] + mod_row(i), 0, 0))]
    args += [mod]
    for name in ("g_mix", "w_in", "g_q_a", "w_q_b", "g_kv_a", "w_kv_b", "g_q_d", "g_k_d"):
        a = w[name]
        in_specs.append(pl.BlockSpec(a.shape, const))
        args.append(a)
    for tname in ("cos_h", "sin_h", "cos_a", "sin_a"):
        in_specs.append(pl.BlockSpec((tm, LANES), lambda i: (tab_blk(i), 0)))
        args.append(tabs[tname])

    widths = [("h", D), ("qa", 512), ("ka", 512), ("va", 256), ("qb", 256), ("kb", 128), ("vb", 128),
              ("qc", 256), ("kc", 256), ("vc", 256), ("qd", 256), ("kd", 128), ("vd", 128)]
    out_shape, out_specs = [], []
    if with_f:
        out_shape.append(jax.ShapeDtypeStruct((T, D), F32))
        out_specs.append(pl.BlockSpec((tm, D), row))
    for _, wd in widths:
        out_shape.append(jax.ShapeDtypeStruct((T, wd), BF16))
        out_specs.append(pl.BlockSpec((tm, wd), row))

    outs = pl.pallas_call(
        functools.partial(_proj_kernel, with_f=with_f, tm=tm, scale_a=float((A_NOPE + A_ROPE) ** -0.5)),
        out_shape=out_shape, grid=(n_tiles,), in_specs=in_specs, out_specs=out_specs,
        compiler_params=_cparams(("parallel",)),
        name="proj_in",
    )(*args)
    outs = list(outs)
    x2 = outs.pop(0) if with_f else x
    names = [n for n, _ in widths]
    return x2, dict(zip(names, outs))


_NT = (((1,), (1,)), ((), ()))


def _head_plan(kind):
    if kind == "A":
        return tuple((r, None, r, r // 2) for r in range(N_HEADS))
    if kind == "C":
        return tuple((r // 2, r % 2, r // 2, r // 2) for r in range(N_HEADS))
    return tuple((r // 2, r % 2, 0, 0) for r in range(N_HEADS))


def _head_query(q_ref, plan_r):
    qt, half, _, _ = plan_r
    src = q_ref[:, qt * LANES:(qt + 1) * LANES]
    if half is None:
        return src
    lane = _lane_iota(src.shape)
    keep = (lane < HEAD_DIM) if half == 0 else (lane >= HEAD_DIM)
    return jnp.where(keep, src, jnp.zeros_like(src))


def _score_chunks(q, k_blocks, bias_blocks):
    chunks = []
    for kb, bb in zip(k_blocks, bias_blocks):
        s = lax.dot_general(q, kb, _NT, preferred_element_type=F32)
        if bb is not None:
            s = s + bb
        chunks += [s[:, c * LANES:(c + 1) * LANES] for c in range(s.shape[1] // LANES)]
    return chunks


def _row_max(chunks):
    m = functools.reduce(jnp.maximum, chunks)
    return jnp.broadcast_to(jnp.max(m, axis=-1, keepdims=True), m.shape)


def _row_sum(x):
    return jnp.broadcast_to(jnp.sum(x, axis=-1, keepdims=True), x.shape)


def _weighted_values(p_chunks, v_blocks):
    pv, idx = None, 0
    for vb in v_blocks:
        n = vb.shape[0] // LANES
        p = jnp.concatenate(p_chunks[idx:idx + n], axis=1).astype(BF16)
        idx += n
        d = jnp.dot(p, vb, preferred_element_type=F32)
        pv = d if pv is None else pv + d
    return pv


def _softmax_once(q, k_blocks, v_blocks, bias_blocks, sink):
    chunks = _score_chunks(q, k_blocks, bias_blocks)
    m = _row_max(chunks)
    if sink is not None:
        m = jnp.maximum(m, sink)
    p = [jnp.exp(c - m) for c in chunks]
    l = _row_sum(functools.reduce(jnp.add, p))
    if sink is not None:
        l = l + jnp.exp(sink - m)
    return _weighted_values(p, v_blocks) / l


def _store_heads(o_ref, outs):
    lane = _lane_iota(outs[0].shape)
    for g in range(2):
        o_ref[:, g * LANES:(g + 1) * LANES] = jnp.where(
            lane < HEAD_DIM, outs[2 * g], outs[2 * g + 1]).astype(o_ref.dtype)


def _attn_band_kernel(*refs, kind, lb, n_kv):
    tq = ROW_BLK
    it = iter(refs)
    q_ref = next(it)
    k_refs = [next(it) for _ in range(n_kv)]
    v_refs = [next(it) for _ in range(n_kv)]
    bias_ref = next(it) if n_kv > 1 else None
    sink_ref = next(it) if kind == "B" else None
    if n_kv == 1:
        next(it)
    o_ref = next(it)
    t = pl.program_id(0) % lb
    var = jnp.where(t == 0, 0, jnp.where(t == lb - 1, 2, 1))
    outs = []
    for r, plan_r in enumerate(_head_plan(kind)):
        _, _, kt, vt = plan_r
        ksl = slice(kt * LANES, (kt + 1) * LANES)
        vsl = slice(vt * LANES, (vt + 1) * LANES)
        bias = [None]
        for j in range(n_kv - 1):
            if kind == "C":
                bias.append(bias_ref[var, j, r * tq:(r + 1) * tq, :])
            else:
                bias.append(bias_ref[var, j])
        sink = sink_ref[_GQA_PERM[r]] if kind == "B" else None
        outs.append(_softmax_once(_head_query(q_ref, plan_r), [k[:, ksl] for k in k_refs],
                                  [v[:, vsl] for v in v_refs], bias, sink))
    _store_heads(o_ref, outs)


def _attn_dense_kernel(q_ref, kc_ref, vc_ref, kl_ref, vl_ref, o_ref, qs_ref, m_ref, l_ref, acc_ref,
                       *, kind, n_steps):
    tq = ROW_BLK
    s = pl.program_id(1)
    plan = _head_plan(kind)

    def update(first):
        for r, (_, _, kt, vt) in enumerate(plan):
            rows = slice(r * tq, (r + 1) * tq)
            ksl = slice(kt * LANES, (kt + 1) * LANES)
            vsl = slice(vt * LANES, (vt + 1) * LANES)
            if first:
                q = _head_query(q_ref, plan[r])
                qs_ref[rows, :] = q
                k_blocks = [kc_ref[:, ksl], kl_ref[:, ksl]]
                v_blocks = [vc_ref[:, vsl], vl_ref[:, vsl]]
            else:
                q = qs_ref[rows, :]
                k_blocks = [kl_ref[:, ksl]]
                v_blocks = [vl_ref[:, vsl]]
            chunks = _score_chunks(q, k_blocks, [None] * len(k_blocks))
            m_cur = _row_max(chunks)
            if first:
                m_new = m_cur
            else:
                m_prev = m_ref[rows, :]
                m_new = jnp.maximum(m_prev, m_cur)
                alpha = jnp.exp(m_prev - m_new)
            p = [jnp.exp(c - m_new) for c in chunks]
            l_add = functools.reduce(jnp.add, p)
            pv = _weighted_values(p, v_blocks)
            if first:
                l_ref[rows, :] = l_add
                acc_ref[rows, :] = pv
            else:
                l_ref[rows, :] = alpha * l_ref[rows, :] + l_add
                acc_ref[rows, :] = alpha * acc_ref[rows, :] + pv
            m_ref[rows, :] = m_new

    @pl.when(s == 0)
    def _():
        update(True)

    @pl.when(s > 0)
    def _():
        update(False)

    @pl.when(s == n_steps - 1)
    def _():
        outs = []
        for r in range(N_HEADS):
            rows = slice(r * tq, (r + 1) * tq)
            outs.append(acc_ref[rows, :] / _row_sum(l_ref[rows, :]))
        _store_heads(o_ref, outs)


def _attn_latent_call(lay, kind, q, k, v, *, sink=None, bias=None):
    T, B, ncb, lb, S = lay["T"], lay["B"], lay["ncb"], lay["lb"], lay["S"]
    tq = ROW_BLK
    qw, kw, vw = q.shape[1], k.shape[1], v.shape[1]
    out_shape = jax.ShapeDtypeStruct((T, BRANCH_W), BF16)
    if kind in ("A", "D"):
        tk = lay["tk_dense"]
        n_steps = S // tk
        lat0 = ncb * ROW_BLK // tk
        lat_blk = lambda i, s: (lat0 + (i // lb) * n_steps + s, 0)
        return pl.pallas_call(
            functools.partial(_attn_dense_kernel, kind=kind, n_steps=n_steps),
            out_shape=out_shape,
            grid=(B * lb, n_steps),
            in_specs=[pl.BlockSpec((tq, qw), lambda i, s: (ncb + i, 0)),
                      pl.BlockSpec((ROW_BLK, kw), lambda i, s: (i // lb, 0)),
                      pl.BlockSpec((ROW_BLK, vw), lambda i, s: (i // lb, 0)),
                      pl.BlockSpec((tk, kw), lat_blk),
                      pl.BlockSpec((tk, vw), lat_blk)],
            out_specs=pl.BlockSpec((tq, BRANCH_W), lambda i, s: (ncb + i, 0)),
            scratch_shapes=[pltpu.VMEM((N_HEADS * tq, LANES), BF16),
                            pltpu.VMEM((N_HEADS * tq, LANES), F32),
                            pltpu.VMEM((N_HEADS * tq, LANES), F32),
                            pltpu.VMEM((N_HEADS * tq, LANES), F32)],
            compiler_params=_cparams(("parallel", "arbitrary")),
            name="attn_" + kind,
        )(q, k, v, k, v)

    def nb(i, d):
        return (ncb + (i // lb) * lb + jnp.clip(i % lb + d, 0, lb - 1), 0)

    kv_maps = [lambda i: (i // lb, 0), lambda i: nb(i, -1), lambda i: nb(i, 0), lambda i: nb(i, 1)]
    in_specs = [pl.BlockSpec((tq, qw), lambda i: (ncb + i, 0))]
    in_specs += [pl.BlockSpec((ROW_BLK, kw), m) for m in kv_maps]
    in_specs += [pl.BlockSpec((ROW_BLK, vw), m) for m in kv_maps]
    in_specs.append(pl.BlockSpec(bias.shape, lambda i: (0,) * bias.ndim))
    args = [q, k, k, k, k, v, v, v, v, bias]
    if kind == "B":
        in_specs.append(pl.BlockSpec(memory_space=pltpu.SMEM))
        args.append(sink)
    return pl.pallas_call(
        functools.partial(_attn_band_kernel, kind=kind, lb=lb, n_kv=4),
        out_shape=out_shape,
        grid=(B * lb,),
        in_specs=in_specs,
        out_specs=pl.BlockSpec((tq, BRANCH_W), lambda i: (ncb + i, 0)),
        compiler_params=_cparams(("parallel",)),
        name="attn_" + kind,
    )(*args)


def _attn_context_call(lay, kind, q, k, v, o, *, sink=None):
    ncb = lay["ncb"]
    qw, kw, vw = q.shape[1], k.shape[1], v.shape[1]
    blk = lambda i: (i, 0)
    in_specs = [pl.BlockSpec((ROW_BLK, qw), blk), pl.BlockSpec((ROW_BLK, kw), blk),
                pl.BlockSpec((ROW_BLK, vw), blk)]
    args = [q, k, v]
    if kind == "B":
        in_specs.append(pl.BlockSpec(memory_space=pltpu.SMEM))
        args.append(sink)
    in_specs.append(pl.BlockSpec(memory_space=pl.ANY))
    args.append(o)
    return pl.pallas_call(
        functools.partial(_attn_band_kernel, kind=kind, lb=1, n_kv=1),
        out_shape=jax.ShapeDtypeStruct(o.shape, o.dtype),
        grid=(ncb,),
        in_specs=in_specs,
        out_specs=pl.BlockSpec((ROW_BLK, BRANCH_W), blk),
        input_output_aliases={len(args) - 1: 0},
        compiler_params=_cparams(("parallel",)),
        name="attn_ctx_" + kind,
    )(*args)


def _window_bias(lb):
    a = np.arange(ROW_BLK)
    out = np.full((3, 3, ROW_BLK, ROW_BLK), NEG_INF, np.float32)
    for vi, t_rep in enumerate((0, 1, lb - 1)):
        for di, delta in enumerate((-1, 0, 1)):
            kt = t_rep + delta
            if not 0 <= kt < lb:
                continue
            qpos = t_rep * ROW_BLK + a
            kpos = kt * ROW_BLK + a
            ok = np.abs(qpos[:, None] - kpos[None, :]) <= WINDOW
            out[vi, di] = np.where(ok, 0.0, NEG_INF)
    return jnp.asarray(out)


def _neighbourhood_bias(rpb, rows_total):
    lb = rows_total * GRID_W // ROW_BLK
    rpt = ROW_BLK // GRID_W
    kh = min(NA_KH, rows_total)
    a = np.arange(ROW_BLK)
    q_sub, q_col = a // GRID_W, a % GRID_W
    out = []
    for t_rep in (0, 1, lb - 1):
        per_delta = []
        for delta in (-1, 0, 1):
            kt = t_rep + delta
            q_row = t_rep * rpt + q_sub
            k_row = kt * rpt + q_sub
            k_col = q_col
            r_start = np.clip(q_row - kh // 2, 0, rows_total - kh)
            row_ok = (k_row[None] >= r_start[:, None]) & (k_row[None] < r_start[:, None] + kh)
            c_start = np.clip(q_col - NA_KW // 2, 0, GRID_W - NA_KW)
            col_ok = (k_col[None] >= c_start[:, None]) & (k_col[None] < c_start[:, None] + NA_KW)
            ok = row_ok & col_ok & (0 <= kt < lb)
            dr = np.clip(k_row[None] - q_row[:, None], -(NA_KH - 1), NA_KH - 1) + NA_KH - 1
            dc = np.clip(k_col[None] - q_col[:, None], -(NA_KW - 1), NA_KW - 1) + NA_KW - 1
            vals = rpb[:, dr, dc].astype(F32)
            per_delta.append(jnp.where(jnp.asarray(ok)[None], vals, NEG_INF).reshape(-1, ROW_BLK))
        out.append(jnp.stack(per_delta))
    return jnp.stack(out)


def _merge_kernel(h_ref, oa_ref, ob_ref, oc_ref, od_ref, x_ref, mod_ref, gffn_ref, wg_ref, bg_ref,
                  wb_ref, wout_ref, wr_ref, br_ref, x1_ref, h2_ref, route_ref, *, tm):
    hb = h_ref[...]
    y = None
    for n, o_ref in enumerate((oa_ref, ob_ref, oc_ref, od_ref)):
        gate = _sigmoid(jnp.dot(hb, wg_ref[n], preferred_element_type=F32) + bg_ref[n])
        t = gate * jnp.dot(o_ref[...], wb_ref[n], preferred_element_type=F32)
        y = t if y is None else y + t
    z = jnp.dot(y.astype(BF16), wout_ref[...], preferred_element_type=F32)
    x1 = x_ref[...] + mod_ref[0, 2:3, :] * z
    x1_ref[...] = x1
    h2 = _rms(x1) * gffn_ref[...] * (1.0 + mod_ref[0, 4:5, :]) + mod_ref[0, 3:4, :]
    for c in range(SUBLANES):
        h2_ref[pl.ds(c, tm, stride=SUBLANES), :] = h2[:, c * LANES:(c + 1) * LANES]

    logit = jnp.dot(h2, wr_ref[...], precision=HIGHEST, preferred_element_type=F32) + br_ref[...]
    lane = _lane_iota(logit.shape)
    big = jnp.int32(1 << 20)
    is_g = (lane >= N_EXPERTS) & (lane < N_EXPERTS + N_GROUPS)
    gl = jnp.where(is_g, logit, NEG_INF)
    gmax = jnp.max(gl, axis=-1, keepdims=True)
    gsel = jnp.min(jnp.where(gl == gmax, lane - N_EXPERTS, big), axis=-1, keepdims=True)
    gw = 1.0 / jnp.sum(jnp.where(is_g, jnp.exp(gl - gmax), 0.0), axis=-1, keepdims=True)
    in_grp = (lane < N_EXPERTS) & ((lane // EXPERTS_PER_GROUP) == gsel)
    el = jnp.where(in_grp, logit, NEG_INF)
    v1 = jnp.max(el, axis=-1, keepdims=True)
    i1 = jnp.min(jnp.where(el == v1, lane, big), axis=-1, keepdims=True)
    el2 = jnp.where(lane == i1, NEG_INF, el)
    v2 = jnp.max(el2, axis=-1, keepdims=True)
    i2 = jnp.min(jnp.where(el2 == v2, lane, big), axis=-1, keepdims=True)
    e21 = jnp.exp(v2 - v1)
    w1 = gw / (1.0 + e21)
    w2 = gw * e21 / (1.0 + e21)
    route_ref[...] = jnp.where(lane == 0, i1.astype(F32),
                               jnp.where(lane == 1, i2.astype(F32),
                                         jnp.where(lane == 2, w1, jnp.where(lane == 3, w2, 0.0))))


def _merge_call(lay, layer, h, o, x, mod, w, *, with_ctx):
    T, D, tm = lay["T"], lay["D"], lay["tm"]
    nct, tpb, B = lay["nct"], lay["tpb"], lay["B"]
    off = 0 if with_ctx else nct
    n_tiles = T // tm - off

    def mod_row(i):
        return jnp.where(i < nct, B, (i - nct) // tpb)

    row = lambda i: (i + off, 0)
    const2 = lambda i: (0, 0)
    const3 = lambda i: (0, 0, 0)
    in_specs = [pl.BlockSpec((tm, D), row)]
    in_specs += [pl.BlockSpec((tm, BRANCH_W), row)] * 4
    in_specs += [pl.BlockSpec((tm, D), row),
                 pl.BlockSpec((1, 6, D), lambda i: (layer * lay["mod_rows"] + mod_row(i + off), 0, 0)),
                 pl.BlockSpec((1, D), const2),
                 pl.BlockSpec(w["w_gate"].shape, const3),
                 pl.BlockSpec(w["b_gate"].shape, const3),
                 pl.BlockSpec(w["w_branch"].shape, const3),
                 pl.BlockSpec((D, D), const2),
                 pl.BlockSpec((D, LANES), const2),
                 pl.BlockSpec((1, LANES), const2)]
    out_shape = [jax.ShapeDtypeStruct((T, D), F32),
                 jax.ShapeDtypeStruct((T * SUBLANES, LANES), F32),
                 jax.ShapeDtypeStruct((T, LANES), F32)]
    out_specs = [pl.BlockSpec((tm, D), row),
                 pl.BlockSpec((tm * SUBLANES, LANES), row),
                 pl.BlockSpec((tm, LANES), row)]
    return pl.pallas_call(
        functools.partial(_merge_kernel, tm=tm),
        out_shape=out_shape, grid=(n_tiles,), in_specs=in_specs, out_specs=out_specs,
        compiler_params=_cparams(("parallel",)),
        name="merge",
    )(h, o["A"], o["B"], o["C"], o["D"], x, mod, w["g_ffn"], w["w_gate"], w["b_gate"],
      w["w_branch"], w["w_out"], w["w_route"], w["b_route"])


def _moe_kernel(tab_ref, tok_ref, h2_ref, sw_ref, w1_ref, w3_ref, w2_ref, f_ref,
                xg_ref, yb_ref, *, blk, slots, tile_off):
    ti = pl.program_id(0)
    e = pl.program_id(1)

    @pl.when(e == 0)
    def _():
        f_ref[...] = jnp.zeros_like(f_ref)

    base = ((ti + tile_off) * N_EXPERTS + e) * 2
    seg0 = tab_ref[base]
    n_e = tab_ref[base + 1]
    tok_base = (ti + tile_off) * slots
    n_blocks = (n_e + blk - 1) // blk

    def block_body(b, carry):
        off = pl.multiple_of(seg0 + b * blk, SUBLANES)
        n_valid = jnp.minimum(n_e - b * blk, blk)

        def gather(r, c):
            tok = tok_ref[tok_base + off + r]
            src = pl.multiple_of(tok * SUBLANES, SUBLANES)
            dst = pl.multiple_of(r * SUBLANES, SUBLANES)
            xg_ref[pl.ds(dst, SUBLANES), :] = h2_ref[pl.ds(src, SUBLANES), :]
            return c

        lax.fori_loop(0, blk, gather, 0, unroll=8)
        xb = _load_token_major(xg_ref, blk).astype(BF16)
        a = jnp.dot(xb, w1_ref[0], preferred_element_type=F32)
        g = jnp.dot(xb, w3_ref[0], preferred_element_type=F32)
        hid = (a * _sigmoid(a) * g).astype(BF16)
        y = jnp.dot(hid, w2_ref[0], preferred_element_type=F32)
        y = y * sw_ref[0, pl.ds(off, blk), :]
        for c in range(SUBLANES):
            yb_ref[pl.ds(c, blk, stride=SUBLANES), :] = y[:, c * LANES:(c + 1) * LANES]

        def scatter(r, c):
            tok = tok_ref[tok_base + off + r]
            dst = pl.multiple_of(tok * SUBLANES, SUBLANES)
            src = pl.multiple_of(r * SUBLANES, SUBLANES)
            f_ref[pl.ds(dst, SUBLANES), :] += yb_ref[pl.ds(src, SUBLANES), :]
            return c

        lax.fori_loop(0, n_valid, scatter, 0)
        return carry

    lax.fori_loop(0, n_blocks, block_body, 0)


def _moe_call(lay, route, h2, w, *, with_ctx):
    T, D, tt = lay["T"], lay["D"], lay["tt"]
    blk = lay["moe_blk"]
    n_tiles_all = T // tt
    tile_off = 0 if with_ctx else lay["ncb"] * ROW_BLK // tt
    n_tiles = n_tiles_all - tile_off
    n_assign = 2 * tt
    slots = n_assign + N_EXPERTS * SUBLANES + blk

    eid = route[:, 0:2].astype(jnp.int32).reshape(n_tiles_all, n_assign)
    wts = route[:, 2:4].reshape(n_tiles_all, n_assign)
    order = jnp.argsort(eid, axis=1, stable=True)
    w_sorted = jnp.take_along_axis(wts, order, axis=1)
    tok_sorted = (order // 2).astype(jnp.int32)
    ex = jnp.arange(N_EXPERTS, dtype=jnp.int32)
    counts = jnp.sum(eid[:, :, None] == ex[None, None, :], axis=1, dtype=jnp.int32)
    starts = jnp.cumsum(counts, axis=1) - counts
    padded = (counts + SUBLANES - 1) // SUBLANES * SUBLANES
    seg_end = jnp.cumsum(padded, axis=1)
    seg0 = seg_end - padded
    j = jnp.arange(slots, dtype=jnp.int32)[None, :]
    e_slot = jnp.sum(j[:, :, None] >= seg_end[:, None, :], axis=2, dtype=jnp.int32)
    e_c = jnp.minimum(e_slot, N_EXPERTS - 1)
    r = j - jnp.take_along_axis(seg0, e_c, axis=1)
    valid = (e_slot < N_EXPERTS) & (r < jnp.take_along_axis(counts, e_c, axis=1))
    src = jnp.clip(jnp.take_along_axis(starts, e_c, axis=1) + r, 0, n_assign - 1)
    slot_tok = jnp.where(valid, jnp.take_along_axis(tok_sorted, src, axis=1), 0)
    slot_w = jnp.where(valid, jnp.take_along_axis(w_sorted, src, axis=1), 0.0)
    tab = jnp.stack([seg0, counts], axis=-1).reshape(-1).astype(jnp.int32)

    return pl.pallas_call(
        functools.partial(_moe_kernel, blk=blk, slots=slots, tile_off=tile_off),
        out_shape=jax.ShapeDtypeStruct((T * SUBLANES, LANES), F32),
        grid_spec=pltpu.PrefetchScalarGridSpec(
            num_scalar_prefetch=2,
            grid=(n_tiles, N_EXPERTS),
            in_specs=[
                pl.BlockSpec((tt * SUBLANES, LANES), lambda t, e, *_: (t + tile_off, 0)),
                pl.BlockSpec((1, slots, 1), lambda t, e, *_: (t + tile_off, 0, 0)),
                pl.BlockSpec((1, D, EXPERT_FF), lambda t, e, *_: (e, 0, 0)),
                pl.BlockSpec((1, D, EXPERT_FF), lambda t, e, *_: (e, 0, 0)),
                pl.BlockSpec((1, EXPERT_FF, D), lambda t, e, *_: (e, 0, 0)),
            ],
            out_specs=pl.BlockSpec((tt * SUBLANES, LANES), lambda t, e, *_: (t + tile_off, 0)),
            scratch_shapes=[pltpu.VMEM((blk * SUBLANES, LANES), F32),
                            pltpu.VMEM((blk * SUBLANES, LANES), F32)],
        ),
        compiler_params=_cparams(("parallel", "arbitrary")),
        name="moe_experts",
    )(tab, slot_tok.reshape(-1), h2, slot_w.reshape(n_tiles_all, slots, 1),
      w["w_ff1"], w["w_ff3"], w["w_ff2"])


def _final_kernel(x_ref, f_ref, mod_ref, g_ref, o_ref, *, tm):
    xf = x_ref[...] + mod_ref[0, 5:6, :] * _load_token_major(f_ref, tm)
    o_ref[...] = _rms(xf) * g_ref[...]


def _final_call(lay, layer, x1, f, mod, g_final):
    T, D, tm = lay["T"], lay["D"], lay["tm"]
    nct, tpb = lay["nct"], lay["tpb"]
    n_lat = T // tm - nct
    return pl.pallas_call(
        functools.partial(_final_kernel, tm=tm),
        out_shape=jax.ShapeDtypeStruct((n_lat * tm, D), F32),
        grid=(n_lat,),
        in_specs=[pl.BlockSpec((tm, D), lambda i: (i + nct, 0)),
                  pl.BlockSpec((tm * SUBLANES, LANES), lambda i: (i + nct, 0)),
                  pl.BlockSpec((1, 6, D), lambda i: (layer * lay["mod_rows"] + i // tpb, 0, 0)),
                  pl.BlockSpec((1, D), lambda i: (0, 0))],
        out_specs=pl.BlockSpec((tm, D), lambda i: (i, 0)),
        compiler_params=_cparams(("parallel",)),
        name="final_norm",
    )(x1, f, mod, g_final)


def _select_cols(wm, segs, scale=None):
    parts = []
    for k, (start, width) in enumerate(segs):
        if start is None:
            parts.append(jnp.zeros((wm.shape[0], width), wm.dtype))
        else:
            blk = wm[:, start:start + width]
            parts.append(blk if scale is None or scale[k] is None else blk * scale[k])
    return jnp.concatenate(parts, axis=1)


def _prep_layer(l, p):
    a_cols = A_Q_RANK + A_KV_RANK + A_ROPE
    b_off = a_cols
    c_off = b_off + 512
    d_off = c_off + 768
    qk_scale = HEAD_DIM ** -0.5
    gqa_q = lambda off: [(off + hh * HEAD_DIM, HEAD_DIM) for hh in _GQA_PERM]
    segs = ([(0, 256), (256, 128), (None, 64), (384, 32), (None, 32)]
            + gqa_q(b_off) + [(b_off + 256, 128), (b_off + 384, 128)]
            + [(c_off, 256), (c_off + 256, 256), (c_off + 512, 256)]
            + gqa_q(d_off) + [(d_off + 256, 128), (d_off + 384, 128)])
    scale = [None] * len(segs)
    for k in (5, 6, 7, 8, 11):
        scale[k] = qk_scale
    w_in = _select_cols(p["w_in"][l], segs, scale).astype(BF16)
    assert w_in.shape[1] == PROJ_COLS

    hq = A_NOPE + A_ROPE
    segs_q = []
    for hh in range(N_HEADS):
        segs_q += [(hh * hq, hq), (None, LANES - hq)]
    w_q_b = _select_cols(p["w_q_b"][l], segs_q).astype(BF16)
    hk = A_NOPE + HEAD_DIM
    segs_k = []
    for hh in range(N_HEADS):
        segs_k += [(hh * hk, A_NOPE), (None, LANES - A_NOPE)]
    segs_k += [(hh * hk + A_NOPE, HEAD_DIM) for hh in range(N_HEADS)]
    w_kv_b = _select_cols(p["w_kv_b"][l], segs_k).astype(BF16)

    wb = p["w_branch"][l]
    perm_rows = lambda m: jnp.concatenate([m[hh * HEAD_DIM:(hh + 1) * HEAD_DIM] for hh in _GQA_PERM], axis=0)
    w_branch = jnp.stack([wb[0], perm_rows(wb[1]), wb[2], perm_rows(wb[3])]).astype(BF16)

    d = p["w_in"].shape[1]
    w_route = jnp.zeros((d, LANES), F32)
    w_route = w_route.at[:, :N_EXPERTS].set(p["w_router"][l]).at[:, N_EXPERTS:N_EXPERTS + N_GROUPS].set(p["w_group"][l])
    b_route = jnp.zeros((1, LANES), F32)
    b_route = b_route.at[0, :N_EXPERTS].set(p["b_router"][l]).at[0, N_EXPERTS:N_EXPERTS + N_GROUPS].set(p["b_group"][l])
    return {
        "g_mix": p["g_norm_mix"][l][None, :],
        "w_in": w_in,
        "g_q_a": p["g_q_a"][l][None, :],
        "w_q_b": w_q_b,
        "g_kv_a": p["g_kv_a"][l][None, :],
        "w_kv_b": w_kv_b,
        "g_q_d": (jnp.tile(p["g_q_d"][l], 2) * (HEAD_DIM ** -0.5))[None, :],
        "g_k_d": jnp.tile(p["g_k_d"][l], 2)[None, :],
        "sink": p["sink_b"][l],
        "rpb": p["rpb_c"][l],
        "w_gate": p["w_gate"][l].astype(BF16),
        "b_gate": p["b_gate"][l][:, None, :],
        "w_branch": w_branch,
        "w_out": p["w_out"][l].astype(BF16),
        "g_ffn": p["g_norm_ffn"][l][None, :],
        "w_route": w_route,
        "b_route": b_route,
        "w_ff1": p["w_ff1"][l].astype(BF16),
        "w_ff3": p["w_ff3"][l].astype(BF16),
        "w_ff2": p["w_ff2"][l].astype(BF16),
    }


def _rope_tables(S, tm):
    t = jnp.arange(S, dtype=jnp.int32)
    rows = (t // GRID_W).astype(F32)
    cols = (t % GRID_W).astype(F32)

    def cs(rot):
        half = rot // 2
        inv = ROPE_THETA ** (-jnp.arange(0, half, 2, dtype=F32) / half)
        ar_, ac_ = rows[:, None] * inv, cols[:, None] * inv
        cos = jnp.concatenate([jnp.cos(ar_), jnp.cos(ar_), jnp.cos(ac_), jnp.cos(ac_)], axis=-1)
        sin = jnp.concatenate([-jnp.sin(ar_), jnp.sin(ar_), -jnp.sin(ac_), jnp.sin(ac_)], axis=-1)
        return cos, sin

    cos64, sin64 = cs(HEAD_DIM)
    cos32, sin32 = cs(A_ROPE)
    ones = lambda n: jnp.ones((S, n), F32)
    zeros = lambda n: jnp.zeros((S, n), F32)
    tabs = {
        "cos_h": jnp.concatenate([cos64, cos64], axis=-1),
        "sin_h": jnp.concatenate([sin64, sin64], axis=-1),
        "cos_a": jnp.concatenate([ones(A_NOPE), cos32, ones(LANES - A_NOPE - A_ROPE)], axis=-1),
        "sin_a": jnp.concatenate([zeros(A_NOPE), sin32, zeros(LANES - A_NOPE - A_ROPE)], axis=-1),
    }
    ident = {"cos_h": 1.0, "sin_h": 0.0, "cos_a": 1.0, "sin_a": 0.0}
    return {k: jnp.concatenate([v, jnp.full((tm, LANES), ident[k], F32)], axis=0) for k, v in tabs.items()}


def _layout(B, S, n_ctx, D):
    assert n_ctx == ROW_BLK and S % 1024 == 0 and S // GRID_W >= 3 * (ROW_BLK // GRID_W)
    T = B * (n_ctx + S)
    tm = 512 if (B * n_ctx) % 512 == 0 else 256
    tk_dense = 1024 if (B * n_ctx) % 1024 == 0 else (512 if (B * n_ctx) % 512 == 0 else 256)
    tt = 2048 if (B * n_ctx) % 2048 == 0 else B * n_ctx
    assert S % tt == 0
    return {
        "B": B, "S": S, "D": D, "T": T, "tm": tm,
        "ncb": B * n_ctx // ROW_BLK,
        "lb": S // ROW_BLK,
        "nct": B * n_ctx // tm,
        "tpb": S // tm,
        "tk_dense": tk_dense,
        "tt": tt, "moe_blk": 128,
        "mod_rows": 16,
    }


def kernel(x, c, ctx, c_ctx, w_mod, b_mod, g_norm_mix, w_in, g_q_a, w_q_b, g_kv_a, w_kv_b, sink_b, rpb_c,
           g_q_d, g_k_d, w_gate, b_gate, w_branch, w_out, g_norm_ffn, w_group, b_group, w_router, b_router,
           w_ff1, w_ff3, w_ff2, g_final):
    B, S, D = x.shape
    n_ctx = ctx.shape[1]
    depth = w_mod.shape[0]
    lay = _layout(B, S, n_ctx, D)
    params = dict(w_in=w_in, g_norm_mix=g_norm_mix, g_q_a=g_q_a, w_q_b=w_q_b, g_kv_a=g_kv_a, w_kv_b=w_kv_b,
                  sink_b=sink_b, rpb_c=rpb_c, g_q_d=g_q_d, g_k_d=g_k_d, w_gate=w_gate, b_gate=b_gate,
                  w_branch=w_branch, w_out=w_out, g_norm_ffn=g_norm_ffn, w_group=w_group, b_group=b_group,
                  w_router=w_router, b_router=b_router, w_ff1=w_ff1, w_ff3=w_ff3, w_ff2=w_ff2)

    c_all = jnp.zeros((lay["mod_rows"], D), F32).at[:B].set(c).at[B].set(c_ctx)
    mod = _modulation(c_all, w_mod, b_mod).reshape(depth * lay["mod_rows"], 6, D)
    tabs = _rope_tables(S, lay["tm"])
    win_bias = _window_bias(lay["lb"])
    xf = jnp.concatenate([ctx.reshape(B * n_ctx, D), x.reshape(B * S, D)], axis=0)

    f = None
    for l in range(depth):
        with_ctx = l < depth - 1
        w = _prep_layer(l, params)
        xf, pr = _proj_call(lay, xf, f, mod, mod, l, w, tabs)
        o = {}
        for kind, kl in (("A", "a"), ("B", "b"), ("C", "c"), ("D", "d")):
            q, k, v = pr["q" + kl], pr["k" + kl], pr["v" + kl]
            sink = w["sink"] if kind == "B" else None
            bias = {"B": win_bias, "C": _neighbourhood_bias(w["rpb"], S // GRID_W)}.get(kind)
            o[kind] = _attn_latent_call(lay, kind, q, k, v, sink=sink, bias=bias)
            if with_ctx:
                o[kind] = _attn_context_call(lay, kind, q, k, v, o[kind], sink=sink)
        xf, h2, route = _merge_call(lay, l, pr["h"], o, xf, mod, w, with_ctx=with_ctx)
        f = _moe_call(lay, route, h2, w, with_ctx=with_ctx)
    out = _final_call(lay, depth - 1, xf, f, mod, g_final[None, :])
    return out.reshape(B, S, D)
```

```python
import functools

import numpy as np
import jax
import jax.numpy as jnp
from jax import lax
from jax.experimental import pallas as pl
from jax.experimental.pallas import tpu as pltpu

F32 = jnp.float32
BF16 = jnp.bfloat16
HIGHEST = lax.Precision.HIGHEST

GRID_W = 64
ROPE_THETA = 10000.0
EPS = 1e-6
NEG_INF = -1e30
LOG2E = 1.4426950408889634
HEAD_DIM = 64
N_HEADS = 4
BRANCH_W = 256
A_Q_RANK = 256
A_KV_RANK = 128
A_NOPE = 64
A_ROPE = 32
NA_KH = 8
NA_KW = 16
WINDOW = 128
N_GROUPS = 4
EXPERTS_PER_GROUP = 8
N_EXPERTS = 32
EXPERT_FF = 256

LANES = 128
SUBLANES = 8
ROW_BLK = 256
VMEM_LIMIT = 56 * 1024 * 1024

_PROJ_GROUPS = (("cq", 256), ("ckv", 128), ("kr", 128), ("qb", 256), ("kb", 128), ("vb", 128),
                ("qc", 256), ("kc", 256), ("vc", 256), ("qd", 256), ("kd", 128), ("vd", 128))
_PROJ_OFF = {}
_o = 0
for _n, _w in _PROJ_GROUPS:
    _PROJ_OFF[_n] = (_o, _w)
    _o += _w
PROJ_COLS = _o
_GQA_PERM = (0, 2, 1, 3)


def _cparams(sem):
    return pltpu.CompilerParams(dimension_semantics=sem, vmem_limit_bytes=VMEM_LIMIT)


def _lane_iota(shape):
    return lax.broadcasted_iota(jnp.int32, shape, len(shape) - 1)


def _sigmoid(x):
    return 1.0 / (1.0 + jnp.exp(-x))


def _mod_kernel(c_ref, w_ref, b_ref, o_ref):
    cf = c_ref[...]
    s = cf * _sigmoid(cf)
    o_ref[0] = jnp.dot(s, w_ref[0], precision=HIGHEST, preferred_element_type=F32) + b_ref[0]


def _modulation(c_all, w_mod, b_mod):
    n_layers, d, n_out = w_mod.shape
    rows = c_all.shape[0]
    tn = 1536
    return pl.pallas_call(
        _mod_kernel,
        out_shape=jax.ShapeDtypeStruct((n_layers, rows, n_out), F32),
        grid=(n_layers, n_out // tn),
        in_specs=[pl.BlockSpec((rows, d), lambda l, j: (0, 0)),
                  pl.BlockSpec((1, d, tn), lambda l, j: (l, 0, j)),
                  pl.BlockSpec((1, 1, tn), lambda l, j: (l, 0, j))],
        out_specs=pl.BlockSpec((1, rows, tn), lambda l, j: (l, 0, j)),
        compiler_params=_cparams(("arbitrary", "arbitrary")),
        name="modulation",
    )(c_all, w_mod, b_mod.reshape(n_layers, 1, n_out))


def _rms(x):
    return x * lax.rsqrt(jnp.mean(x * x, axis=-1, keepdims=True) + EPS)


def _swap_blocks(x, blk):
    lane = _lane_iota(x.shape)
    up = pltpu.roll(x, LANES - blk, 1)
    dn = pltpu.roll(x, blk, 1)
    return jnp.where((lane // blk) % 2 == 0, up, dn)


def _rope(x, cos, sin, blk):
    return x * cos + _swap_blocks(x, blk) * sin


def _pair_norm(x, g):
    lo = _lane_iota(x.shape) < HEAD_DIM
    sq = x * x
    s_lo = jnp.sum(jnp.where(lo, sq, 0.0), axis=-1, keepdims=True)
    s_hi = jnp.sum(jnp.where(lo, 0.0, sq), axis=-1, keepdims=True)
    ms = jnp.where(lo, s_lo, s_hi) * (1.0 / HEAD_DIM)
    return x * lax.rsqrt(ms + EPS) * g


def _load_token_major(ref, rows):
    return jnp.concatenate(
        [ref[pl.ds(c, rows, stride=SUBLANES), :] for c in range(SUBLANES)], axis=-1)


def _proj_kernel(*refs, with_f, tm, scale_a):
    it = iter(refs)
    x_ref = next(it)
    if with_f:
        f_ref = next(it)
        modp_ref = next(it)
    mod_ref = next(it)
    gmix_ref, win_ref, gqa_ref, wqb_ref, gkva_ref, wkvb_ref, gqd_ref, gkd_ref = (next(it) for _ in range(8))
    cosh_ref, sinh_ref, cosa_ref, sina_ref = (next(it) for _ in range(4))
    if with_f:
        x2_ref = next(it)
    h_ref = next(it)
    qa_ref, ka_ref, va_ref, qb_ref, kb_ref, vb_ref, qc_ref, kc_ref, vc_ref, qd_ref, kd_ref, vd_ref = (
        next(it) for _ in range(12))

    xf = x_ref[...]
    if with_f:
        xf = xf + modp_ref[0, 5:6, :] * _load_token_major(f_ref, tm)
        x2_ref[...] = xf
    h = _rms(xf) * gmix_ref[...] * (1.0 + mod_ref[0, 1:2, :]) + mod_ref[0, 0:1, :]
    hb = h.astype(BF16)
    h_ref[...] = hb
    p = jnp.dot(hb, win_ref[...], preferred_element_type=F32)

    def grp(name):
        o, w = _PROJ_OFF[name]
        return p[:, o:o + w]

    cosh, sinh = cosh_ref[...], sinh_ref[...]
    cosa, sina = cosa_ref[...], sina_ref[...]

    cq = (_rms(grp("cq")) * gqa_ref[...]).astype(BF16)
    qa = jnp.dot(cq, wqb_ref[...], preferred_element_type=F32)
    for hd in range(N_HEADS):
        sl = slice(hd * LANES, (hd + 1) * LANES)
        qa_ref[:, sl] = (_rope(qa[:, sl], cosa, sina, 8) * scale_a).astype(BF16)
    ckv = (_rms(grp("ckv")) * gkva_ref[...]).astype(BF16)
    kva = jnp.dot(ckv, wkvb_ref[...], preferred_element_type=F32)
    kr = _rope(grp("kr"), cosa, sina, 8)
    for hd in range(N_HEADS):
        sl = slice(hd * LANES, (hd + 1) * LANES)
        ka_ref[:, sl] = (kva[:, sl] + kr).astype(BF16)
    va_ref[...] = kva[:, N_HEADS * LANES:].astype(BF16)

    qb = grp("qb")
    for j in range(2):
        sl = slice(j * LANES, (j + 1) * LANES)
        qb_ref[:, sl] = _rope(qb[:, sl], cosh, sinh, 16).astype(BF16)
    kb_ref[...] = _rope(grp("kb"), cosh, sinh, 16).astype(BF16)
    vb_ref[...] = grp("vb").astype(BF16)

    qc_ref[...] = grp("qc").astype(BF16)
    kc_ref[...] = grp("kc").astype(BF16)
    vc_ref[...] = grp("vc").astype(BF16)

    qd = grp("qd")
    for j in range(2):
        sl = slice(j * LANES, (j + 1) * LANES)
        qd_ref[:, sl] = _rope(_pair_norm(qd[:, sl], gqd_ref[...]), cosh, sinh, 16).astype(BF16)
    kd_ref[...] = _rope(_pair_norm(grp("kd"), gkd_ref[...]), cosh, sinh, 16).astype(BF16)
    vd_ref[...] = grp("vd").astype(BF16)


def _proj_call(lay, x, f, modp, mod, layer, w, tabs):
    T, D, tm = lay["T"], lay["D"], lay["tm"]
    nct, tpb, B = lay["nct"], lay["tpb"], lay["B"]
    with_f = f is not None
    n_tiles = T // tm

    def mod_row(i):
        return jnp.where(i < nct, B, (i - nct) // tpb)

    def tab_blk(i):
        return jnp.where(i < nct, tpb, (i - nct) % tpb)

    row = lambda i: (i, 0)
    const = lambda i: (0, 0)
    in_specs = [pl.BlockSpec((tm, D), row)]
    args = [x]
    if with_f:
        in_specs += [pl.BlockSpec((tm * SUBLANES, LANES), row),
                     pl.BlockSpec((1, 6, D), lambda i: ((layer - 1) * lay["mod_rows"] + mod_row(i), 0, 0))]
        args += [f, modp]
    in_specs += [pl.BlockSpec((1, 6, D), lambda i: (layer * lay["mod_rows"] + mod_row(i), 0, 0))]
    args += [mod]
    for name in ("g_mix", "w_in", "g_q_a", "w_q_b", "g_kv_a", "w_kv_b", "g_q_d", "g_k_d"):
        a = w[name]
        in_specs.append(pl.BlockSpec(a.shape, const))
        args.append(a)
    for tname in ("cos_h", "sin_h", "cos_a", "sin_a"):
        in_specs.append(pl.BlockSpec((tm, LANES), lambda i: (tab_blk(i), 0)))
        args.append(tabs[tname])

    widths = [("h", D), ("qa", 512), ("ka", 512), ("va", 256), ("qb", 256), ("kb", 128), ("vb", 128),
              ("qc", 256), ("kc", 256), ("vc", 256), ("qd", 256), ("kd", 128), ("vd", 128)]
    out_shape, out_specs = [], []
    if with_f:
        out_shape.append(jax.ShapeDtypeStruct((T, D), F32))
        out_specs.append(pl.BlockSpec((tm, D), row))
    for _, wd in widths:
        out_shape.append(jax.ShapeDtypeStruct((T, wd), BF16))
        out_specs.append(pl.BlockSpec((tm, wd), row))

    outs = pl.pallas_call(
        functools.partial(_proj_kernel, with_f=with_f, tm=tm,
                          scale_a=float((A_NOPE + A_ROPE) ** -0.5 * LOG2E)),
        out_shape=out_shape, grid=(n_tiles,), in_specs=in_specs, out_specs=out_specs,
        compiler_params=_cparams(("parallel",)),
        name="proj_in",
    )(*args)
    outs = list(outs)
    x2 = outs.pop(0) if with_f else x
    names = [n for n, _ in widths]
    return x2, dict(zip(names, outs))


_NT = (((1,), (1,)), ((), ()))


def _head_plan(kind):
    if kind == "A":
        return tuple((r, None, r, r // 2) for r in range(N_HEADS))
    if kind == "C":
        return tuple((r // 2, r % 2, r // 2, r // 2) for r in range(N_HEADS))
    return tuple((r // 2, r % 2, 0, 0) for r in range(N_HEADS))


def _head_query(q_ref, plan_r):
    qt, half, _, _ = plan_r
    src = q_ref[:, qt * LANES:(qt + 1) * LANES]
    if half is None:
        return src
    lane = _lane_iota(src.shape)
    keep = (lane < HEAD_DIM) if half == 0 else (lane >= HEAD_DIM)
    return jnp.where(keep, src, jnp.zeros_like(src))


def _score_chunks(q, k_blocks, bias_blocks):
    chunks = []
    for kb, bb in zip(k_blocks, bias_blocks):
        s = lax.dot_general(q, kb, _NT, preferred_element_type=F32)
        if bb is not None:
            s = s + bb
        chunks += [s[:, c * LANES:(c + 1) * LANES] for c in range(s.shape[1] // LANES)]
    return chunks


def _row_max(chunks):
    m = functools.reduce(jnp.maximum, chunks)
    return jnp.broadcast_to(jnp.max(m, axis=-1, keepdims=True), m.shape)


def _weighted_values(p_chunks, v_blocks, half):
    pv, idx = None, 0
    for vb in v_blocks:
        n = vb.shape[0] // LANES
        p = jnp.concatenate(p_chunks[idx:idx + n], axis=1).astype(BF16)
        idx += n
        lane = _lane_iota(vb.shape)
        own = (lane < HEAD_DIM) if half == 0 else (lane >= HEAD_DIM)
        d = jnp.dot(p, jnp.where(own, vb, jnp.ones_like(vb)), preferred_element_type=F32)
        pv = d if pv is None else pv + d
    return pv


def _softmax_once(q, k_blocks, v_blocks, bias_blocks, sink, half):
    chunks = _score_chunks(q, k_blocks, bias_blocks)
    m = _row_max(chunks)
    if sink is not None:
        m = jnp.maximum(m, sink)
    p = [jnp.exp2(c - m) for c in chunks]
    pv = _weighted_values(p, v_blocks, half)
    l = pltpu.roll(pv, HEAD_DIM, 1)
    if sink is not None:
        l = l + jnp.exp2(sink - m)
    return pv / l


def _store_heads(o_ref, outs):
    lane = _lane_iota(outs[0].shape)
    for g in range(2):
        o_ref[:, g * LANES:(g + 1) * LANES] = jnp.where(
            lane < HEAD_DIM, outs[2 * g], outs[2 * g + 1]).astype(o_ref.dtype)


def _attn_band_kernel(*refs, kind, lb, n_kv):
    tq = ROW_BLK
    it = iter(refs)
    q_ref = next(it)
    k_refs = [next(it) for _ in range(n_kv)]
    v_refs = [next(it) for _ in range(n_kv)]
    bias_ref = next(it) if n_kv > 1 else None
    sink_ref = next(it) if kind == "B" else None
    if n_kv == 1:
        next(it)
    o_ref = next(it)
    t = pl.program_id(0) % lb
    var = jnp.where(t == 0, 0, jnp.where(t == lb - 1, 2, 1))
    outs = []
    for r, plan_r in enumerate(_head_plan(kind)):
        _, _, kt, vt = plan_r
        ksl = slice(kt * LANES, (kt + 1) * LANES)
        vsl = slice(vt * LANES, (vt + 1) * LANES)
        bias = [None]
        for j in range(n_kv - 1):
            if kind == "C":
                bias.append(bias_ref[var, j, r * tq:(r + 1) * tq, :])
            else:
                bias.append(bias_ref[var, j])
        sink = sink_ref[_GQA_PERM[r]] * LOG2E if kind == "B" else None
        outs.append(_softmax_once(_head_query(q_ref, plan_r), [k[:, ksl] for k in k_refs],
                                  [v[:, vsl] for v in v_refs], bias, sink, r % 2))
    _store_heads(o_ref, outs)


def _attn_dense_kernel(q_ref, kc_ref, vc_ref, kl_ref, vl_ref, o_ref, qs_ref, m_ref, acc_ref,
                       *, kind, n_steps):
    tq = ROW_BLK
    s = pl.program_id(1)
    plan = _head_plan(kind)

    def update(first):
        for r, (_, _, kt, vt) in enumerate(plan):
            rows = slice(r * tq, (r + 1) * tq)
            ksl = slice(kt * LANES, (kt + 1) * LANES)
            vsl = slice(vt * LANES, (vt + 1) * LANES)
            if first:
                q = _head_query(q_ref, plan[r])
                qs_ref[rows, :] = q
                k_blocks = [kc_ref[:, ksl], kl_ref[:, ksl]]
                v_blocks = [vc_ref[:, vsl], vl_ref[:, vsl]]
            else:
                q = qs_ref[rows, :]
                k_blocks = [kl_ref[:, ksl]]
                v_blocks = [vl_ref[:, vsl]]
            chunks = _score_chunks(q, k_blocks, [None] * len(k_blocks))
            m_cur = _row_max(chunks)
            if first:
                m_new = m_cur
            else:
                m_prev = m_ref[rows, :]
                m_new = jnp.maximum(m_prev, m_cur)
                alpha = jnp.exp2(m_prev - m_new)
            p = [jnp.exp2(c - m_new) for c in chunks]
            pv = _weighted_values(p, v_blocks, r % 2)
            if first:
                acc_ref[rows, :] = pv
            else:
                acc_ref[rows, :] = alpha * acc_ref[rows, :] + pv
            m_ref[rows, :] = m_new

    @pl.when(s == 0)
    def _():
        update(True)

    @pl.when(s > 0)
    def _():
        update(False)

    @pl.when(s == n_steps - 1)
    def _():
        outs = []
        for r in range(N_HEADS):
            rows = slice(r * tq, (r + 1) * tq)
            acc = acc_ref[rows, :]
            outs.append(acc / pltpu.roll(acc, HEAD_DIM, 1))
        _store_heads(o_ref, outs)


def _attn_latent_call(lay, kind, q, k, v, *, sink=None, bias=None):
    T, B, ncb, lb, S = lay["T"], lay["B"], lay["ncb"], lay["lb"], lay["S"]
    tq = ROW_BLK
    qw, kw, vw = q.shape[1], k.shape[1], v.shape[1]
    out_shape = jax.ShapeDtypeStruct((T, BRANCH_W), BF16)
    if kind in ("A", "D"):
        tk = lay["tk_dense"]
        n_steps = S // tk
        lat0 = ncb * ROW_BLK // tk
        lat_blk = lambda i, s: (lat0 + (i // lb) * n_steps + s, 0)
        return pl.pallas_call(
            functools.partial(_attn_dense_kernel, kind=kind, n_steps=n_steps),
            out_shape=out_shape,
            grid=(B * lb, n_steps),
            in_specs=[pl.BlockSpec((tq, qw), lambda i, s: (ncb + i, 0)),
                      pl.BlockSpec((ROW_BLK, kw), lambda i, s: (i // lb, 0)),
                      pl.BlockSpec((ROW_BLK, vw), lambda i, s: (i // lb, 0)),
                      pl.BlockSpec((tk, kw), lat_blk),
                      pl.BlockSpec((tk, vw), lat_blk)],
            out_specs=pl.BlockSpec((tq, BRANCH_W), lambda i, s: (ncb + i, 0)),
            scratch_shapes=[pltpu.VMEM((N_HEADS * tq, LANES), BF16),
                            pltpu.VMEM((N_HEADS * tq, LANES), F32),
                            pltpu.VMEM((N_HEADS * tq, LANES), F32)],
            compiler_params=_cparams(("parallel", "arbitrary")),
            name="attn_" + kind,
        )(q, k, v, k, v)

    def nb(i, d):
        return (ncb + (i // lb) * lb + jnp.clip(i % lb + d, 0, lb - 1), 0)

    kv_maps = [lambda i: (i // lb, 0), lambda i: nb(i, -1), lambda i: nb(i, 0), lambda i: nb(i, 1)]
    in_specs = [pl.BlockSpec((tq, qw), lambda i: (ncb + i, 0))]
    in_specs += [pl.BlockSpec((ROW_BLK, kw), m) for m in kv_maps]
    in_specs += [pl.BlockSpec((ROW_BLK, vw), m) for m in kv_maps]
    in_specs.append(pl.BlockSpec(bias.shape, lambda i: (0,) * bias.ndim))
    args = [q, k, k, k, k, v, v, v, v, bias]
    if kind == "B":
        in_specs.append(pl.BlockSpec(memory_space=pltpu.SMEM))
        args.append(sink)
    return pl.pallas_call(
        functools.partial(_attn_band_kernel, kind=kind, lb=lb, n_kv=4),
        out_shape=out_shape,
        grid=(B * lb,),
        in_specs=in_specs,
        out_specs=pl.BlockSpec((tq, BRANCH_W), lambda i: (ncb + i, 0)),
        compiler_params=_cparams(("parallel",)),
        name="attn_" + kind,
    )(*args)


def _attn_context_call(lay, kind, q, k, v, o, *, sink=None):
    ncb = lay["ncb"]
    qw, kw, vw = q.shape[1], k.shape[1], v.shape[1]
    blk = lambda i: (i, 0)
    in_specs = [pl.BlockSpec((ROW_BLK, qw), blk), pl.BlockSpec((ROW_BLK, kw), blk),
                pl.BlockSpec((ROW_BLK, vw), blk)]
    args = [q, k, v]
    if kind == "B":
        in_specs.append(pl.BlockSpec(memory_space=pltpu.SMEM))
        args.append(sink)
    in_specs.append(pl.BlockSpec(memory_space=pl.ANY))
    args.append(o)
    return pl.pallas_call(
        functools.partial(_attn_band_kernel, kind=kind, lb=1, n_kv=1),
        out_shape=jax.ShapeDtypeStruct(o.shape, o.dtype),
        grid=(ncb,),
        in_specs=in_specs,
        out_specs=pl.BlockSpec((ROW_BLK, BRANCH_W), blk),
        input_output_aliases={len(args) - 1: 0},
        compiler_params=_cparams(("parallel",)),
        name="attn_ctx_" + kind,
    )(*args)


def _window_bias(lb):
    a = np.arange(ROW_BLK)
    out = np.full((3, 3, ROW_BLK, ROW_BLK), NEG_INF, np.float32)
    for vi, t_rep in enumerate((0, 1, lb - 1)):
        for di, delta in enumerate((-1, 0, 1)):
            kt = t_rep + delta
            if not 0 <= kt < lb:
                continue
            qpos = t_rep * ROW_BLK + a
            kpos = kt * ROW_BLK + a
            ok = np.abs(qpos[:, None] - kpos[None, :]) <= WINDOW
            out[vi, di] = np.where(ok, 0.0, NEG_INF)
    return jnp.asarray(out)


def _neighbourhood_bias(rpb, rows_total):
    lb = rows_total * GRID_W // ROW_BLK
    rpt = ROW_BLK // GRID_W
    kh = min(NA_KH, rows_total)
    a = np.arange(ROW_BLK)
    q_sub, q_col = a // GRID_W, a % GRID_W
    out = []
    for t_rep in (0, 1, lb - 1):
        per_delta = []
        for delta in (-1, 0, 1):
            kt = t_rep + delta
            q_row = t_rep * rpt + q_sub
            k_row = kt * rpt + q_sub
            k_col = q_col
            r_start = np.clip(q_row - kh // 2, 0, rows_total - kh)
            row_ok = (k_row[None] >= r_start[:, None]) & (k_row[None] < r_start[:, None] + kh)
            c_start = np.clip(q_col - NA_KW // 2, 0, GRID_W - NA_KW)
            col_ok = (k_col[None] >= c_start[:, None]) & (k_col[None] < c_start[:, None] + NA_KW)
            ok = row_ok & col_ok & (0 <= kt < lb)
            dr = np.clip(k_row[None] - q_row[:, None], -(NA_KH - 1), NA_KH - 1) + NA_KH - 1
            dc = np.clip(k_col[None] - q_col[:, None], -(NA_KW - 1), NA_KW - 1) + NA_KW - 1
            vals = rpb[:, dr, dc].astype(F32) * LOG2E
            per_delta.append(jnp.where(jnp.asarray(ok)[None], vals, NEG_INF).reshape(-1, ROW_BLK))
        out.append(jnp.stack(per_delta))
    return jnp.stack(out)


def _merge_kernel(h_ref, oa_ref, ob_ref, oc_ref, od_ref, x_ref, mod_ref, gffn_ref, wg_ref, bg_ref,
                  wb_ref, wout_ref, wr_ref, br_ref, x1_ref, h2_ref, route_ref, *, tm):
    hb = h_ref[...]
    y = None
    for n, o_ref in enumerate((oa_ref, ob_ref, oc_ref, od_ref)):
        gate = _sigmoid(jnp.dot(hb, wg_ref[n], preferred_element_type=F32) + bg_ref[n])
        t = gate * jnp.dot(o_ref[...], wb_ref[n], preferred_element_type=F32)
        y = t if y is None else y + t
    z = jnp.dot(y.astype(BF16), wout_ref[...], preferred_element_type=F32)
    x1 = x_ref[...] + mod_ref[0, 2:3, :] * z
    x1_ref[...] = x1
    h2 = _rms(x1) * gffn_ref[...] * (1.0 + mod_ref[0, 4:5, :]) + mod_ref[0, 3:4, :]
    for c in range(SUBLANES):
        h2_ref[pl.ds(c, tm, stride=SUBLANES), :] = h2[:, c * LANES:(c + 1) * LANES]

    h2_hi = h2.astype(BF16)
    h2_lo = (h2 - h2_hi.astype(F32)).astype(BF16)
    logit = (jnp.dot(h2_hi, wr_ref[0], preferred_element_type=F32)
             + jnp.dot(h2_lo, wr_ref[0], preferred_element_type=F32)
             + jnp.dot(h2_hi, wr_ref[1], preferred_element_type=F32)) + br_ref[...]
    lane = _lane_iota(logit.shape)
    big = jnp.int32(1 << 20)
    is_g = (lane >= N_EXPERTS) & (lane < N_EXPERTS + N_GROUPS)
    gl = jnp.where(is_g, logit, NEG_INF)
    gmax = jnp.max(gl, axis=-1, keepdims=True)
    gsel = jnp.min(jnp.where(gl == gmax, lane - N_EXPERTS, big), axis=-1, keepdims=True)
    gw = 1.0 / jnp.sum(jnp.where(is_g, jnp.exp(gl - gmax), 0.0), axis=-1, keepdims=True)
    in_grp = (lane < N_EXPERTS) & ((lane // EXPERTS_PER_GROUP) == gsel)
    el = jnp.where(in_grp, logit, NEG_INF)
    v1 = jnp.max(el, axis=-1, keepdims=True)
    i1 = jnp.min(jnp.where(el == v1, lane, big), axis=-1, keepdims=True)
    el2 = jnp.where(lane == i1, NEG_INF, el)
    v2 = jnp.max(el2, axis=-1, keepdims=True)
    i2 = jnp.min(jnp.where(el2 == v2, lane, big), axis=-1, keepdims=True)
    e21 = jnp.exp(v2 - v1)
    w1 = gw / (1.0 + e21)
    w2 = gw * e21 / (1.0 + e21)
    route_ref[...] = jnp.where(lane == 0, i1.astype(F32),
                               jnp.where(lane == 1, i2.astype(F32),
                                         jnp.where(lane == 2, w1, jnp.where(lane == 3, w2, 0.0))))


def _merge_call(lay, layer, h, o, x, mod, w, *, with_ctx):
    T, D, tm = lay["T"], lay["D"], lay["tm"]
    nct, tpb, B = lay["nct"], lay["tpb"], lay["B"]
    off = 0 if with_ctx else nct
    n_tiles = T // tm - off

    def mod_row(i):
        return jnp.where(i < nct, B, (i - nct) // tpb)

    row = lambda i: (i + off, 0)
    const2 = lambda i: (0, 0)
    const3 = lambda i: (0, 0, 0)
    in_specs = [pl.BlockSpec((tm, D), row)]
    in_specs += [pl.BlockSpec((tm, BRANCH_W), row)] * 4
    in_specs += [pl.BlockSpec((tm, D), row),
                 pl.BlockSpec((1, 6, D), lambda i: (layer * lay["mod_rows"] + mod_row(i + off), 0, 0)),
                 pl.BlockSpec((1, D), const2),
                 pl.BlockSpec(w["w_gate"].shape, const3),
                 pl.BlockSpec(w["b_gate"].shape, const3),
                 pl.BlockSpec(w["w_branch"].shape, const3),
                 pl.BlockSpec((D, D), const2),
                 pl.BlockSpec((2, D, LANES), const3),
                 pl.BlockSpec((1, LANES), const2)]
    out_shape = [jax.ShapeDtypeStruct((T, D), F32),
                 jax.ShapeDtypeStruct((T * SUBLANES, LANES), F32),
                 jax.ShapeDtypeStruct((T, LANES), F32)]
    out_specs = [pl.BlockSpec((tm, D), row),
                 pl.BlockSpec((tm * SUBLANES, LANES), row),
                 pl.BlockSpec((tm, LANES), row)]
    return pl.pallas_call(
        functools.partial(_merge_kernel, tm=tm),
        out_shape=out_shape, grid=(n_tiles,), in_specs=in_specs, out_specs=out_specs,
        compiler_params=_cparams(("parallel",)),
        name="merge",
    )(h, o["A"], o["B"], o["C"], o["D"], x, mod, w["g_ffn"], w["w_gate"], w["b_gate"],
      w["w_branch"], w["w_out"], w["w_route"], w["b_route"])


def _moe_kernel(tab_ref, tok_ref, h2_ref, sw_ref, w1_ref, w3_ref, w2_ref, f_ref,
                xg_ref, yb_ref, *, blk, slots, tile_off):
    ti = pl.program_id(0)
    e = pl.program_id(1)

    @pl.when(e == 0)
    def _():
        f_ref[...] = jnp.zeros_like(f_ref)
        xg_ref[...] = jnp.zeros_like(xg_ref)

    base = ((ti + tile_off) * N_EXPERTS + e) * 2
    seg0 = tab_ref[base]
    n_pad = tab_ref[base + 1]
    tok_base = (ti + tile_off) * slots
    n_blocks = (n_pad + blk - 1) // blk

    def block_body(b, carry):
        off = pl.multiple_of(seg0 + b * blk, SUBLANES)
        n_grp = jnp.minimum(n_pad - b * blk, blk) // SUBLANES

        def gather(gi, c):
            for u in range(SUBLANES):
                r = gi * SUBLANES + u
                tok = tok_ref[tok_base + off + r]
                src = pl.multiple_of(tok * SUBLANES, SUBLANES)
                dst = pl.multiple_of(r * SUBLANES, SUBLANES)
                xg_ref[pl.ds(dst, SUBLANES), :] = h2_ref[pl.ds(src, SUBLANES), :]
            return c

        lax.fori_loop(0, n_grp, gather, 0)
        xb = _load_token_major(xg_ref, blk).astype(BF16)
        a = jnp.dot(xb, w1_ref[0], preferred_element_type=F32)
        g = jnp.dot(xb, w3_ref[0], preferred_element_type=F32)
        hid = (a * _sigmoid(a) * g).astype(BF16)
        y = jnp.dot(hid, w2_ref[0], preferred_element_type=F32)
        y = y * sw_ref[0, pl.ds(off, blk), :]
        for c in range(SUBLANES):
            yb_ref[pl.ds(c, blk, stride=SUBLANES), :] = y[:, c * LANES:(c + 1) * LANES]

        def scatter(gi, c):
            for u in range(SUBLANES):
                r = gi * SUBLANES + u
                tok = tok_ref[tok_base + off + r]
                dst = pl.multiple_of(tok * SUBLANES, SUBLANES)
                src = pl.multiple_of(r * SUBLANES, SUBLANES)
                f_ref[pl.ds(dst, SUBLANES), :] += yb_ref[pl.ds(src, SUBLANES), :]
            return c

        lax.fori_loop(0, n_grp, scatter, 0)
        return carry

    lax.fori_loop(0, n_blocks, block_body, 0)


def _moe_call(lay, route, h2, w, *, with_ctx):
    T, D, tt = lay["T"], lay["D"], lay["tt"]
    blk = lay["moe_blk"]
    n_tiles_all = T // tt
    tile_off = 0 if with_ctx else lay["ncb"] * ROW_BLK // tt
    n_tiles = n_tiles_all - tile_off
    n_assign = 2 * tt
    n_fill = N_EXPERTS * SUBLANES
    slots = n_assign + n_fill + blk

    eid = route[:, 0:2].astype(jnp.int32).reshape(n_tiles_all, n_assign)
    wts = route[:, 2:4].reshape(n_tiles_all, n_assign)
    tok = jnp.broadcast_to(jnp.arange(n_assign, dtype=jnp.int32)[None, :] // 2, eid.shape)
    ex = jnp.arange(N_EXPERTS, dtype=jnp.int32)
    counts = jnp.sum(eid[:, :, None] == ex[None, None, :], axis=1, dtype=jnp.int32)
    n_dummy = (-counts) % SUBLANES
    fill_key = jnp.where(jnp.arange(SUBLANES, dtype=jnp.int32)[None, None, :] < n_dummy[:, :, None],
                         ex[None, :, None], N_EXPERTS).reshape(n_tiles_all, n_fill)
    zeros_i = jnp.zeros((n_tiles_all, n_fill), jnp.int32)
    _, slot_tok, slot_w = lax.sort(
        (jnp.concatenate([eid, fill_key], axis=1), jnp.concatenate([tok, zeros_i], axis=1),
         jnp.concatenate([wts, zeros_i.astype(F32)], axis=1)),
        dimension=1, is_stable=True, num_keys=1)
    slot_tok = jnp.pad(slot_tok, ((0, 0), (0, blk)))
    slot_w = jnp.pad(slot_w, ((0, 0), (0, blk)))
    padded = counts + n_dummy
    seg0 = jnp.cumsum(padded, axis=1) - padded
    tab = jnp.stack([seg0, padded], axis=-1).reshape(-1).astype(jnp.int32)

    return pl.pallas_call(
        functools.partial(_moe_kernel, blk=blk, slots=slots, tile_off=tile_off),
        out_shape=jax.ShapeDtypeStruct((T * SUBLANES, LANES), F32),
        grid_spec=pltpu.PrefetchScalarGridSpec(
            num_scalar_prefetch=2,
            grid=(n_tiles, N_EXPERTS),
            in_specs=[
                pl.BlockSpec((tt * SUBLANES, LANES), lambda t, e, *_: (t + tile_off, 0)),
                pl.BlockSpec((1, slots, 1), lambda t, e, *_: (t + tile_off, 0, 0)),
                pl.BlockSpec((1, D, EXPERT_FF), lambda t, e, *_: (e, 0, 0)),
                pl.BlockSpec((1, D, EXPERT_FF), lambda t, e, *_: (e, 0, 0)),
                pl.BlockSpec((1, EXPERT_FF, D), lambda t, e, *_: (e, 0, 0)),
            ],
            out_specs=pl.BlockSpec((tt * SUBLANES, LANES), lambda t, e, *_: (t + tile_off, 0)),
            scratch_shapes=[pltpu.VMEM((blk * SUBLANES, LANES), F32),
                            pltpu.VMEM((blk * SUBLANES, LANES), F32)],
        ),
        compiler_params=_cparams(("parallel", "arbitrary")),
        name="moe_experts",
    )(tab, slot_tok.reshape(-1), h2, slot_w.reshape(n_tiles_all, slots, 1),
      w["w_ff1"], w["w_ff3"], w["w_ff2"])


def _final_kernel(x_ref, f_ref, mod_ref, g_ref, o_ref, *, tm):
    xf = x_ref[...] + mod_ref[0, 5:6, :] * _load_token_major(f_ref, tm)
    o_ref[...] = _rms(xf) * g_ref[...]


def _final_call(lay, layer, x1, f, mod, g_final):
    T, D, tm = lay["T"], lay["D"], lay["tm"]
    nct, tpb = lay["nct"], lay["tpb"]
    n_lat = T // tm - nct
    return pl.pallas_call(
        functools.partial(_final_kernel, tm=tm),
        out_shape=jax.ShapeDtypeStruct((n_lat * tm, D), F32),
        grid=(n_lat,),
        in_specs=[pl.BlockSpec((tm, D), lambda i: (i + nct, 0)),
                  pl.BlockSpec((tm * SUBLANES, LANES), lambda i: (i + nct, 0)),
                  pl.BlockSpec((1, 6, D), lambda i: (layer * lay["mod_rows"] + i // tpb, 0, 0)),
                  pl.BlockSpec((1, D), lambda i: (0, 0))],
        out_specs=pl.BlockSpec((tm, D), lambda i: (i, 0)),
        compiler_params=_cparams(("parallel",)),
        name="final_norm",
    )(x1, f, mod, g_final)


def _select_cols(wm, segs, scale=None):
    parts = []
    for k, (start, width) in enumerate(segs):
        if start is None:
            parts.append(jnp.zeros((wm.shape[0], width), wm.dtype))
        else:
            blk = wm[:, start:start + width]
            parts.append(blk if scale is None or scale[k] is None else blk * scale[k])
    return jnp.concatenate(parts, axis=1)


def _prep_layer(l, p):
    a_cols = A_Q_RANK + A_KV_RANK + A_ROPE
    b_off = a_cols
    c_off = b_off + 512
    d_off = c_off + 768
    qk_scale = HEAD_DIM ** -0.5 * LOG2E
    gqa_q = lambda off: [(off + hh * HEAD_DIM, HEAD_DIM) for hh in _GQA_PERM]
    segs = ([(0, 256), (256, 128), (None, 64), (384, 32), (None, 32)]
            + gqa_q(b_off) + [(b_off + 256, 128), (b_off + 384, 128)]
            + [(c_off, 256), (c_off + 256, 256), (c_off + 512, 256)]
            + gqa_q(d_off) + [(d_off + 256, 128), (d_off + 384, 128)])
    scale = [None] * len(segs)
    for k in (5, 6, 7, 8, 11):
        scale[k] = qk_scale
    w_in = _select_cols(p["w_in"][l], segs, scale).astype(BF16)
    assert w_in.shape[1] == PROJ_COLS

    hq = A_NOPE + A_ROPE
    segs_q = []
    for hh in range(N_HEADS):
        segs_q += [(hh * hq, hq), (None, LANES - hq)]
    w_q_b = _select_cols(p["w_q_b"][l], segs_q).astype(BF16)
    hk = A_NOPE + HEAD_DIM
    segs_k = []
    for hh in range(N_HEADS):
        segs_k += [(hh * hk, A_NOPE), (None, LANES - A_NOPE)]
    segs_k += [(hh * hk + A_NOPE, HEAD_DIM) for hh in range(N_HEADS)]
    w_kv_b = _select_cols(p["w_kv_b"][l], segs_k).astype(BF16)

    wb = p["w_branch"][l]
    perm_rows = lambda m: jnp.concatenate([m[hh * HEAD_DIM:(hh + 1) * HEAD_DIM] for hh in _GQA_PERM], axis=0)
    w_branch = jnp.stack([wb[0], perm_rows(wb[1]), wb[2], perm_rows(wb[3])]).astype(BF16)

    d = p["w_in"].shape[1]
    w_route = jnp.zeros((d, LANES), F32)
    w_route = w_route.at[:, :N_EXPERTS].set(p["w_router"][l]).at[:, N_EXPERTS:N_EXPERTS + N_GROUPS].set(p["w_group"][l])
    b_route = jnp.zeros((1, LANES), F32)
    b_route = b_route.at[0, :N_EXPERTS].set(p["b_router"][l]).at[0, N_EXPERTS:N_EXPERTS + N_GROUPS].set(p["b_group"][l])
    return {
        "g_mix": p["g_norm_mix"][l][None, :],
        "w_in": w_in,
        "g_q_a": p["g_q_a"][l][None, :],
        "w_q_b": w_q_b,
        "g_kv_a": p["g_kv_a"][l][None, :],
        "w_kv_b": w_kv_b,
        "g_q_d": (jnp.tile(p["g_q_d"][l], 2) * qk_scale)[None, :],
        "g_k_d": jnp.tile(p["g_k_d"][l], 2)[None, :],
        "sink": p["sink_b"][l],
        "rpb": p["rpb_c"][l],
        "w_gate": p["w_gate"][l].astype(BF16),
        "b_gate": p["b_gate"][l][:, None, :],
        "w_branch": w_branch,
        "w_out": p["w_out"][l].astype(BF16),
        "g_ffn": p["g_norm_ffn"][l][None, :],
        "w_route": jnp.stack([w_route.astype(BF16),
                              (w_route - w_route.astype(BF16).astype(F32)).astype(BF16)]),
        "b_route": b_route,
        "w_ff1": p["w_ff1"][l].astype(BF16),
        "w_ff3": p["w_ff3"][l].astype(BF16),
        "w_ff2": p["w_ff2"][l].astype(BF16),
    }


def _rope_tables(S, tm):
    t = jnp.arange(S, dtype=jnp.int32)
    rows = (t // GRID_W).astype(F32)
    cols = (t % GRID_W).astype(F32)

    def cs(rot):
        half = rot // 2
        inv = ROPE_THETA ** (-jnp.arange(0, half, 2, dtype=F32) / half)
        ar_, ac_ = rows[:, None] * inv, cols[:, None] * inv
        cos = jnp.concatenate([jnp.cos(ar_), jnp.cos(ar_), jnp.cos(ac_), jnp.cos(ac_)], axis=-1)
        sin = jnp.concatenate([-jnp.sin(ar_), jnp.sin(ar_), -jnp.sin(ac_), jnp.sin(ac_)], axis=-1)
        return cos, sin

    cos64, sin64 = cs(HEAD_DIM)
    cos32, sin32 = cs(A_ROPE)
    ones = lambda n: jnp.ones((S, n), F32)
    zeros = lambda n: jnp.zeros((S, n), F32)
    tabs = {
        "cos_h": jnp.concatenate([cos64, cos64], axis=-1),
        "sin_h": jnp.concatenate([sin64, sin64], axis=-1),
        "cos_a": jnp.concatenate([ones(A_NOPE), cos32, ones(LANES - A_NOPE - A_ROPE)], axis=-1),
        "sin_a": jnp.concatenate([zeros(A_NOPE), sin32, zeros(LANES - A_NOPE - A_ROPE)], axis=-1),
    }
    ident = {"cos_h": 1.0, "sin_h": 0.0, "cos_a": 1.0, "sin_a": 0.0}
    return {k: jnp.concatenate([v, jnp.full((tm, LANES), ident[k], F32)], axis=0) for k, v in tabs.items()}


def _layout(B, S, n_ctx, D):
    assert n_ctx == ROW_BLK and S % 1024 == 0 and S // GRID_W >= 3 * (ROW_BLK // GRID_W)
    T = B * (n_ctx + S)
    tm = 512 if (B * n_ctx) % 512 == 0 else 256
    tk_dense = 1024 if (B * n_ctx) % 1024 == 0 else (512 if (B * n_ctx) % 512 == 0 else 256)
    tt = 2048 if (B * n_ctx) % 2048 == 0 else B * n_ctx
    assert S % tt == 0
    return {
        "B": B, "S": S, "D": D, "T": T, "tm": tm,
        "ncb": B * n_ctx // ROW_BLK,
        "lb": S // ROW_BLK,
        "nct": B * n_ctx // tm,
        "tpb": S // tm,
        "tk_dense": tk_dense,
        "tt": tt, "moe_blk": 256,
        "mod_rows": 16,
    }


def kernel(x, c, ctx, c_ctx, w_mod, b_mod, g_norm_mix, w_in, g_q_a, w_q_b, g_kv_a, w_kv_b, sink_b, rpb_c,
           g_q_d, g_k_d, w_gate, b_gate, w_branch, w_out, g_norm_ffn, w_group, b_group, w_router, b_router,
           w_ff1, w_ff3, w_ff2, g_final):
    B, S, D = x.shape
    n_ctx = ctx.shape[1]
    depth = w_mod.shape[0]
    lay = _layout(B, S, n_ctx, D)
    params = dict(w_in=w_in, g_norm_mix=g_norm_mix, g_q_a=g_q_a, w_q_b=w_q_b, g_kv_a=g_kv_a, w_kv_b=w_kv_b,
                  sink_b=sink_b, rpb_c=rpb_c, g_q_d=g_q_d, g_k_d=g_k_d, w_gate=w_gate, b_gate=b_gate,
                  w_branch=w_branch, w_out=w_out, g_norm_ffn=g_norm_ffn, w_group=w_group, b_group=b_group,
                  w_router=w_router, b_router=b_router, w_ff1=w_ff1, w_ff3=w_ff3, w_ff2=w_ff2)

    c_all = jnp.zeros((lay["mod_rows"], D), F32).at[:B].set(c).at[B].set(c_ctx)
    mod = _modulation(c_all, w_mod, b_mod).reshape(depth * lay["mod_rows"], 6, D)
    tabs = _rope_tables(S, lay["tm"])
    win_bias = _window_bias(lay["lb"])
    xf = jnp.concatenate([ctx.reshape(B * n_ctx, D), x.reshape(B * S, D)], axis=0)

    f = None
    for l in range(depth):
        with_ctx = l < depth - 1
        w = _prep_layer(l, params)
        xf, pr = _proj_call(lay, xf, f, mod, mod, l, w, tabs)
        o = {}
        for kind, kl in (("A", "a"), ("B", "b"), ("C", "c"), ("D", "d")):
            q, k, v = pr["q" + kl], pr["k" + kl], pr["v" + kl]
            sink = w["sink"] if kind == "B" else None
            bias = {"B": win_bias, "C": _neighbourhood_bias(w["rpb"], S // GRID_W)}.get(kind)
            o[kind] = _attn_latent_call(lay, kind, q, k, v, sink=sink, bias=bias)
            if with_ctx:
                o[kind] = _attn_context_call(lay, kind, q, k, v, o[kind], sink=sink)
        xf, h2, route = _merge_call(lay, l, pr["h"], o, xf, mod, w, with_ctx=with_ctx)
        f = _moe_call(lay, route, h2, w, with_ctx=with_ctx)
    out = _final_call(lay, depth - 1, xf, f, mod, g_final[None, :])
    return out.reshape(B, S, D)
```

```python
import functools

import numpy as np
import jax
import jax.numpy as jnp
from jax import lax
from jax.experimental import pallas as pl
from jax.experimental.pallas import tpu as pltpu

F32 = jnp.float32
BF16 = jnp.bfloat16
HIGHEST = lax.Precision.HIGHEST

GRID_W = 64
ROPE_THETA = 10000.0
EPS = 1e-6
NEG_INF = -1e30
LOG2E = 1.4426950408889634
HEAD_DIM = 64
N_HEADS = 4
BRANCH_W = 256
A_Q_RANK = 256
A_KV_RANK = 128
A_NOPE = 64
A_ROPE = 32
NA_KH = 8
NA_KW = 16
WINDOW = 128
N_GROUPS = 4
EXPERTS_PER_GROUP = 8
N_EXPERTS = 32
EXPERT_FF = 256

LANES = 128
SUBLANES = 8
ROW_BLK = 256
VMEM_LIMIT = 56 * 1024 * 1024

_PROJ_GROUPS = (("cq", 256), ("ckv", 128), ("kr", 128), ("qb", 256), ("kb", 128), ("vb", 128),
                ("qc", 256), ("kc", 256), ("vc", 256), ("qd", 256), ("kd", 128), ("vd", 128))
_PROJ_OFF = {}
_o = 0
for _n, _w in _PROJ_GROUPS:
    _PROJ_OFF[_n] = (_o, _w)
    _o += _w
PROJ_COLS = _o
_GQA_PERM = (0, 2, 1, 3)


def _cparams(sem):
    return pltpu.CompilerParams(dimension_semantics=sem, vmem_limit_bytes=VMEM_LIMIT)


def _lane_iota(shape):
    return lax.broadcasted_iota(jnp.int32, shape, len(shape) - 1)


def _sigmoid(x):
    return 1.0 / (1.0 + jnp.exp(-x))


def _mod_kernel(c_ref, w_ref, b_ref, o_ref):
    cf = c_ref[...]
    s = cf * _sigmoid(cf)
    o_ref[0] = jnp.dot(s, w_ref[0], precision=HIGHEST, preferred_element_type=F32) + b_ref[0]


def _modulation(c_all, w_mod, b_mod):
    n_layers, d, n_out = w_mod.shape
    rows = c_all.shape[0]
    tn = 1536
    return pl.pallas_call(
        _mod_kernel,
        out_shape=jax.ShapeDtypeStruct((n_layers, rows, n_out), F32),
        grid=(n_layers, n_out // tn),
        in_specs=[pl.BlockSpec((rows, d), lambda l, j: (0, 0)),
                  pl.BlockSpec((1, d, tn), lambda l, j: (l, 0, j)),
                  pl.BlockSpec((1, 1, tn), lambda l, j: (l, 0, j))],
        out_specs=pl.BlockSpec((1, rows, tn), lambda l, j: (l, 0, j)),
        compiler_params=_cparams(("arbitrary", "arbitrary")),
        name="modulation",
    )(c_all, w_mod, b_mod.reshape(n_layers, 1, n_out))


def _rms(x):
    return x * lax.rsqrt(jnp.mean(x * x, axis=-1, keepdims=True) + EPS)


def _swap_blocks(x, blk):
    lane = _lane_iota(x.shape)
    up = pltpu.roll(x, LANES - blk, 1)
    dn = pltpu.roll(x, blk, 1)
    return jnp.where((lane // blk) % 2 == 0, up, dn)


def _rope(x, cos, sin, blk):
    return x * cos + _swap_blocks(x, blk) * sin


def _pair_norm(x, g):
    lo = _lane_iota(x.shape) < HEAD_DIM
    sq = x * x
    s_lo = jnp.sum(jnp.where(lo, sq, 0.0), axis=-1, keepdims=True)
    s_hi = jnp.sum(jnp.where(lo, 0.0, sq), axis=-1, keepdims=True)
    ms = jnp.where(lo, s_lo, s_hi) * (1.0 / HEAD_DIM)
    return x * lax.rsqrt(ms + EPS) * g


def _load_token_major(ref, rows):
    return jnp.concatenate(
        [ref[pl.ds(c, rows, stride=SUBLANES), :] for c in range(SUBLANES)], axis=-1)


def _proj_kernel(*refs, with_f, tm, scale_a):
    it = iter(refs)
    x_ref = next(it)
    if with_f:
        f_ref = next(it)
        modp_ref = next(it)
    mod_ref = next(it)
    gmix_ref, win_ref, gqa_ref, wqb_ref, gkva_ref, wkvb_ref, gqd_ref, gkd_ref = (next(it) for _ in range(8))
    cosh_ref, sinh_ref, cosa_ref, sina_ref = (next(it) for _ in range(4))
    if with_f:
        x2_ref = next(it)
    h_ref = next(it)
    qa_ref, ka_ref, va_ref, qb_ref, kb_ref, vb_ref, qc_ref, kc_ref, vc_ref, qd_ref, kd_ref, vd_ref = (
        next(it) for _ in range(12))

    xf = x_ref[...]
    if with_f:
        xf = xf + modp_ref[0, 5:6, :] * _load_token_major(f_ref, tm)
        x2_ref[...] = xf
    h = _rms(xf) * gmix_ref[...] * (1.0 + mod_ref[0, 1:2, :]) + mod_ref[0, 0:1, :]
    hb = h.astype(BF16)
    h_ref[...] = hb
    p = jnp.dot(hb, win_ref[...], preferred_element_type=F32)

    def grp(name):
        o, w = _PROJ_OFF[name]
        return p[:, o:o + w]

    cosh, sinh = cosh_ref[...], sinh_ref[...]
    cosa, sina = cosa_ref[...], sina_ref[...]

    cq = (_rms(grp("cq")) * gqa_ref[...]).astype(BF16)
    qa = jnp.dot(cq, wqb_ref[...], preferred_element_type=F32)
    for hd in range(N_HEADS):
        sl = slice(hd * LANES, (hd + 1) * LANES)
        qa_ref[:, sl] = (_rope(qa[:, sl], cosa, sina, 8) * scale_a).astype(BF16)
    ckv = (_rms(grp("ckv")) * gkva_ref[...]).astype(BF16)
    kva = jnp.dot(ckv, wkvb_ref[...], preferred_element_type=F32)
    kr = _rope(grp("kr"), cosa, sina, 8)
    for hd in range(N_HEADS):
        sl = slice(hd * LANES, (hd + 1) * LANES)
        ka_ref[:, sl] = (kva[:, sl] + kr).astype(BF16)
    va_ref[...] = kva[:, N_HEADS * LANES:].astype(BF16)

    qb = grp("qb")
    for j in range(2):
        sl = slice(j * LANES, (j + 1) * LANES)
        qb_ref[:, sl] = _rope(qb[:, sl], cosh, sinh, 16).astype(BF16)
    kb_ref[...] = _rope(grp("kb"), cosh, sinh, 16).astype(BF16)
    vb_ref[...] = grp("vb").astype(BF16)

    qc_ref[...] = grp("qc").astype(BF16)
    kc_ref[...] = grp("kc").astype(BF16)
    vc_ref[...] = grp("vc").astype(BF16)

    qd = grp("qd")
    for j in range(2):
        sl = slice(j * LANES, (j + 1) * LANES)
        qd_ref[:, sl] = _rope(_pair_norm(qd[:, sl], gqd_ref[...]), cosh, sinh, 16).astype(BF16)
    kd_ref[...] = _rope(_pair_norm(grp("kd"), gkd_ref[...]), cosh, sinh, 16).astype(BF16)
    vd_ref[...] = grp("vd").astype(BF16)


def _proj_call(lay, x, f, modp, mod, layer, w, tabs):
    T, D, tm = lay["T"], lay["D"], lay["tm"]
    nct, tpb, B = lay["nct"], lay["tpb"], lay["B"]
    with_f = f is not None
    n_tiles = T // tm

    def mod_row(i):
        return jnp.where(i < nct, B, (i - nct) // tpb)

    def tab_blk(i):
        return jnp.where(i < nct, tpb, (i - nct) % tpb)

    row = lambda i: (i, 0)
    const = lambda i: (0, 0)
    in_specs = [pl.BlockSpec((tm, D), row)]
    args = [x]
    if with_f:
        in_specs += [pl.BlockSpec((tm * SUBLANES, LANES), row),
                     pl.BlockSpec((1, 6, D), lambda i: ((layer - 1) * lay["mod_rows"] + mod_row(i), 0, 0))]
        args += [f, modp]
    in_specs += [pl.BlockSpec((1, 6, D), lambda i: (layer * lay["mod_rows"] + mod_row(i), 0, 0))]
    args += [mod]
    for name in ("g_mix", "w_in", "g_q_a", "w_q_b", "g_kv_a", "w_kv_b", "g_q_d", "g_k_d"):
        a = w[name]
        in_specs.append(pl.BlockSpec(a.shape, const))
        args.append(a)
    for tname in ("cos_h", "sin_h", "cos_a", "sin_a"):
        in_specs.append(pl.BlockSpec((tm, LANES), lambda i: (tab_blk(i), 0)))
        args.append(tabs[tname])

    widths = [("h", D), ("qa", 512), ("ka", 512), ("va", 256), ("qb", 256), ("kb", 128), ("vb", 128),
              ("qc", 256), ("kc", 256), ("vc", 256), ("qd", 256), ("kd", 128), ("vd", 128)]
    out_shape, out_specs = [], []
    if with_f:
        out_shape.append(jax.ShapeDtypeStruct((T, D), F32))
        out_specs.append(pl.BlockSpec((tm, D), row))
    for _, wd in widths:
        out_shape.append(jax.ShapeDtypeStruct((T, wd), BF16))
        out_specs.append(pl.BlockSpec((tm, wd), row))

    outs = pl.pallas_call(
        functools.partial(_proj_kernel, with_f=with_f, tm=tm,
                          scale_a=float((A_NOPE + A_ROPE) ** -0.5 * LOG2E)),
        out_shape=out_shape, grid=(n_tiles,), in_specs=in_specs, out_specs=out_specs,
        compiler_params=_cparams(("parallel",)),
        name="proj_in",
    )(*args)
    outs = list(outs)
    x2 = outs.pop(0) if with_f else x
    names = [n for n, _ in widths]
    return x2, dict(zip(names, outs))


_NT = (((1,), (1,)), ((), ()))


def _head_plan(kind):
    if kind == "A":
        return tuple((r, None, r, r // 2) for r in range(N_HEADS))
    if kind == "C":
        return tuple((r // 2, r % 2, r // 2, r // 2) for r in range(N_HEADS))
    return tuple((r // 2, r % 2, 0, 0) for r in range(N_HEADS))


def _head_query(q_ref, plan_r):
    qt, half, _, _ = plan_r
    src = q_ref[:, qt * LANES:(qt + 1) * LANES]
    if half is None:
        return src
    lane = _lane_iota(src.shape)
    keep = (lane < HEAD_DIM) if half == 0 else (lane >= HEAD_DIM)
    return jnp.where(keep, src, jnp.zeros_like(src))


def _score_chunks(q, k_blocks, bias_blocks):
    chunks = []
    for kb, bb in zip(k_blocks, bias_blocks):
        s = lax.dot_general(q, kb, _NT, preferred_element_type=F32)
        if bb is not None:
            s = s + bb
        chunks += [s[:, c * LANES:(c + 1) * LANES] for c in range(s.shape[1] // LANES)]
    return chunks


def _row_max(chunks):
    m = functools.reduce(jnp.maximum, chunks)
    return jnp.broadcast_to(jnp.max(m, axis=-1, keepdims=True), m.shape)


def _weighted_values(p_chunks, v_blocks, half):
    pv, idx = None, 0
    for vb in v_blocks:
        n = vb.shape[0] // LANES
        p = jnp.concatenate(p_chunks[idx:idx + n], axis=1).astype(BF16)
        idx += n
        lane = _lane_iota(vb.shape)
        own = (lane < HEAD_DIM) if half == 0 else (lane >= HEAD_DIM)
        d = jnp.dot(p, jnp.where(own, vb, jnp.ones_like(vb)), preferred_element_type=F32)
        pv = d if pv is None else pv + d
    return pv


def _softmax_once(q, k_blocks, v_blocks, bias_blocks, sink, half):
    chunks = _score_chunks(q, k_blocks, bias_blocks)
    m = _row_max(chunks)
    if sink is not None:
        m = jnp.maximum(m, sink)
    p = [jnp.exp2(c - m) for c in chunks]
    pv = _weighted_values(p, v_blocks, half)
    l = pltpu.roll(pv, HEAD_DIM, 1)
    if sink is not None:
        l = l + jnp.exp2(sink - m)
    return pv / l


def _store_heads(o_ref, outs):
    lane = _lane_iota(outs[0].shape)
    for g in range(2):
        o_ref[:, g * LANES:(g + 1) * LANES] = jnp.where(
            lane < HEAD_DIM, outs[2 * g], outs[2 * g + 1]).astype(o_ref.dtype)


def _attn_band_kernel(*refs, kind, lb, n_kv):
    tq = ROW_BLK
    it = iter(refs)
    q_ref = next(it)
    k_refs = [next(it) for _ in range(n_kv)]
    v_refs = [next(it) for _ in range(n_kv)]
    bias_ref = next(it) if n_kv > 1 else None
    sink_ref = next(it) if kind == "B" else None
    if n_kv == 1:
        next(it)
    o_ref = next(it)
    t = pl.program_id(0) % lb
    var = jnp.where(t == 0, 0, jnp.where(t == lb - 1, 2, 1))
    outs = []
    for r, plan_r in enumerate(_head_plan(kind)):
        _, _, kt, vt = plan_r
        ksl = slice(kt * LANES, (kt + 1) * LANES)
        vsl = slice(vt * LANES, (vt + 1) * LANES)
        bias = [None]
        for j in range(n_kv - 1):
            if kind == "C":
                bias.append(bias_ref[var, j, r * tq:(r + 1) * tq, :])
            else:
                bias.append(bias_ref[var, j])
        sink = sink_ref[_GQA_PERM[r]] * LOG2E if kind == "B" else None
        outs.append(_softmax_once(_head_query(q_ref, plan_r), [k[:, ksl] for k in k_refs],
                                  [v[:, vsl] for v in v_refs], bias, sink, r % 2))
    _store_heads(o_ref, outs)


def _attn_dense_kernel(q_ref, kc_ref, vc_ref, kl_ref, vl_ref, o_ref, qs_ref, m_ref, acc_ref,
                       *, kind, n_steps, tq):
    s = pl.program_id(1)
    plan = _head_plan(kind)

    def update(first):
        for r, (_, _, kt, vt) in enumerate(plan):
            rows = slice(r * tq, (r + 1) * tq)
            ksl = slice(kt * LANES, (kt + 1) * LANES)
            vsl = slice(vt * LANES, (vt + 1) * LANES)
            if first:
                q = _head_query(q_ref, plan[r])
                qs_ref[rows, :] = q
                k_blocks = [kc_ref[:, ksl], kl_ref[:, ksl]]
                v_blocks = [vc_ref[:, vsl], vl_ref[:, vsl]]
            else:
                q = qs_ref[rows, :]
                k_blocks = [kl_ref[:, ksl]]
                v_blocks = [vl_ref[:, vsl]]
            chunks = _score_chunks(q, k_blocks, [None] * len(k_blocks))
            m_cur = _row_max(chunks)
            if first:
                m_new = m_cur
            else:
                m_prev = m_ref[rows, :]
                m_new = jnp.maximum(m_prev, m_cur)
                alpha = jnp.exp2(m_prev - m_new)
            p = [jnp.exp2(c - m_new) for c in chunks]
            pv = _weighted_values(p, v_blocks, r % 2)
            if first:
                acc_ref[rows, :] = pv
            else:
                acc_ref[rows, :] = alpha * acc_ref[rows, :] + pv
            m_ref[rows, :] = m_new

    @pl.when(s == 0)
    def _():
        update(True)

    @pl.when(s > 0)
    def _():
        update(False)

    @pl.when(s == n_steps - 1)
    def _():
        outs = []
        for r in range(N_HEADS):
            rows = slice(r * tq, (r + 1) * tq)
            acc = acc_ref[rows, :]
            outs.append(acc / pltpu.roll(acc, HEAD_DIM, 1))
        _store_heads(o_ref, outs)


def _attn_latent_call(lay, kind, q, k, v, *, sink=None, bias=None):
    T, B, ncb, lb, S = lay["T"], lay["B"], lay["ncb"], lay["lb"], lay["S"]
    tq = ROW_BLK
    qw, kw, vw = q.shape[1], k.shape[1], v.shape[1]
    out_shape = jax.ShapeDtypeStruct((T, BRANCH_W), BF16)
    if kind in ("A", "D"):
        tk, tq = lay["tk_dense"], lay["tq_dense"]
        n_steps = S // tk
        lat0 = ncb * ROW_BLK // tk
        q0 = ncb * ROW_BLK // tq
        qpb = S // tq
        lat_blk = lambda i, s: (lat0 + (i // qpb) * n_steps + s, 0)
        return pl.pallas_call(
            functools.partial(_attn_dense_kernel, kind=kind, n_steps=n_steps, tq=tq),
            out_shape=out_shape,
            grid=(B * qpb, n_steps),
            in_specs=[pl.BlockSpec((tq, qw), lambda i, s: (q0 + i, 0)),
                      pl.BlockSpec((ROW_BLK, kw), lambda i, s: (i // qpb, 0)),
                      pl.BlockSpec((ROW_BLK, vw), lambda i, s: (i // qpb, 0)),
                      pl.BlockSpec((tk, kw), lat_blk),
                      pl.BlockSpec((tk, vw), lat_blk)],
            out_specs=pl.BlockSpec((tq, BRANCH_W), lambda i, s: (q0 + i, 0)),
            scratch_shapes=[pltpu.VMEM((N_HEADS * tq, LANES), BF16),
                            pltpu.VMEM((N_HEADS * tq, LANES), F32),
                            pltpu.VMEM((N_HEADS * tq, LANES), F32)],
            compiler_params=_cparams(("parallel", "arbitrary")),
            name="attn_" + kind,
        )(q, k, v, k, v)

    def nb(i, d):
        return (ncb + (i // lb) * lb + jnp.clip(i % lb + d, 0, lb - 1), 0)

    kv_maps = [lambda i: (i // lb, 0), lambda i: nb(i, -1), lambda i: nb(i, 0), lambda i: nb(i, 1)]
    in_specs = [pl.BlockSpec((tq, qw), lambda i: (ncb + i, 0))]
    in_specs += [pl.BlockSpec((ROW_BLK, kw), m) for m in kv_maps]
    in_specs += [pl.BlockSpec((ROW_BLK, vw), m) for m in kv_maps]
    in_specs.append(pl.BlockSpec(bias.shape, lambda i: (0,) * bias.ndim))
    args = [q, k, k, k, k, v, v, v, v, bias]
    if kind == "B":
        in_specs.append(pl.BlockSpec(memory_space=pltpu.SMEM))
        args.append(sink)
    return pl.pallas_call(
        functools.partial(_attn_band_kernel, kind=kind, lb=lb, n_kv=4),
        out_shape=out_shape,
        grid=(B * lb,),
        in_specs=in_specs,
        out_specs=pl.BlockSpec((tq, BRANCH_W), lambda i: (ncb + i, 0)),
        compiler_params=_cparams(("parallel",)),
        name="attn_" + kind,
    )(*args)


def _attn_context_call(lay, kind, q, k, v, o, *, sink=None):
    ncb = lay["ncb"]
    qw, kw, vw = q.shape[1], k.shape[1], v.shape[1]
    blk = lambda i: (i, 0)
    in_specs = [pl.BlockSpec((ROW_BLK, qw), blk), pl.BlockSpec((ROW_BLK, kw), blk),
                pl.BlockSpec((ROW_BLK, vw), blk)]
    args = [q, k, v]
    if kind == "B":
        in_specs.append(pl.BlockSpec(memory_space=pltpu.SMEM))
        args.append(sink)
    in_specs.append(pl.BlockSpec(memory_space=pl.ANY))
    args.append(o)
    return pl.pallas_call(
        functools.partial(_attn_band_kernel, kind=kind, lb=1, n_kv=1),
        out_shape=jax.ShapeDtypeStruct(o.shape, o.dtype),
        grid=(ncb,),
        in_specs=in_specs,
        out_specs=pl.BlockSpec((ROW_BLK, BRANCH_W), blk),
        input_output_aliases={len(args) - 1: 0},
        compiler_params=_cparams(("parallel",)),
        name="attn_ctx_" + kind,
    )(*args)


def _window_bias(lb):
    a = np.arange(ROW_BLK)
    out = np.full((3, 3, ROW_BLK, ROW_BLK), NEG_INF, np.float32)
    for vi, t_rep in enumerate((0, 1, lb - 1)):
        for di, delta in enumerate((-1, 0, 1)):
            kt = t_rep + delta
            if not 0 <= kt < lb:
                continue
            qpos = t_rep * ROW_BLK + a
            kpos = kt * ROW_BLK + a
            ok = np.abs(qpos[:, None] - kpos[None, :]) <= WINDOW
            out[vi, di] = np.where(ok, 0.0, NEG_INF)
    return jnp.asarray(out)


def _neighbourhood_bias(rpb, rows_total):
    lb = rows_total * GRID_W // ROW_BLK
    rpt = ROW_BLK // GRID_W
    kh = min(NA_KH, rows_total)
    a = np.arange(ROW_BLK)
    q_sub, q_col = a // GRID_W, a % GRID_W
    n_dr, n_dc = 2 * NA_KH - 1, 2 * NA_KW - 1
    col = np.arange(GRID_W)
    dc = np.clip(col[None, :] - col[:, None], -(NA_KW - 1), NA_KW - 1) + NA_KW - 1
    hot_c = (dc[:, :, None] == np.arange(n_dc)).astype(np.float32)
    by_col = jnp.einsum("huv,cdv->hucd", rpb.astype(F32) * LOG2E, jnp.asarray(hot_c),
                        precision=HIGHEST)
    sub = np.arange(rpt)
    vals = []
    for delta in (-1, 0, 1):
        dr = np.clip(delta * rpt + sub[None, :] - sub[:, None], -(NA_KH - 1), NA_KH - 1) + NA_KH - 1
        hot_r = (dr[:, :, None] == np.arange(n_dr)).astype(np.float32)
        v = jnp.einsum("abu,hucd->hacbd", jnp.asarray(hot_r), by_col, precision=HIGHEST)
        vals.append(v.reshape(N_HEADS, ROW_BLK, ROW_BLK))
    out = []
    for t_rep in (0, 1, lb - 1):
        per_delta = []
        for di, delta in enumerate((-1, 0, 1)):
            kt = t_rep + delta
            q_row = t_rep * rpt + q_sub
            k_row = kt * rpt + q_sub
            k_col = q_col
            r_start = np.clip(q_row - kh // 2, 0, rows_total - kh)
            row_ok = (k_row[None] >= r_start[:, None]) & (k_row[None] < r_start[:, None] + kh)
            c_start = np.clip(q_col - NA_KW // 2, 0, GRID_W - NA_KW)
            col_ok = (k_col[None] >= c_start[:, None]) & (k_col[None] < c_start[:, None] + NA_KW)
            ok = row_ok & col_ok & (0 <= kt < lb)
            per_delta.append(jnp.where(jnp.asarray(ok)[None], vals[di], NEG_INF).reshape(-1, ROW_BLK))
        out.append(jnp.stack(per_delta))
    return jnp.stack(out)


def _merge_kernel(h_ref, oa_ref, ob_ref, oc_ref, od_ref, x_ref, mod_ref, gffn_ref, wg_ref, bg_ref,
                  wb_ref, wout_ref, wr_ref, br_ref, x1_ref, h2_ref, route_ref, *, tm):
    hb = h_ref[...]
    y = None
    for n, o_ref in enumerate((oa_ref, ob_ref, oc_ref, od_ref)):
        gate = _sigmoid(jnp.dot(hb, wg_ref[n], preferred_element_type=F32) + bg_ref[n])
        t = gate * jnp.dot(o_ref[...], wb_ref[n], preferred_element_type=F32)
        y = t if y is None else y + t
    z = jnp.dot(y.astype(BF16), wout_ref[...], preferred_element_type=F32)
    x1 = x_ref[...] + mod_ref[0, 2:3, :] * z
    x1_ref[...] = x1
    h2 = _rms(x1) * gffn_ref[...] * (1.0 + mod_ref[0, 4:5, :]) + mod_ref[0, 3:4, :]
    for c in range(SUBLANES):
        h2_ref[pl.ds(c, tm, stride=SUBLANES), :] = h2[:, c * LANES:(c + 1) * LANES]

    h2_hi = h2.astype(BF16)
    h2_lo = (h2 - h2_hi.astype(F32)).astype(BF16)
    logit = (jnp.dot(h2_hi, wr_ref[0], preferred_element_type=F32)
             + jnp.dot(h2_lo, wr_ref[0], preferred_element_type=F32)
             + jnp.dot(h2_hi, wr_ref[1], preferred_element_type=F32)) + br_ref[...]
    lane = _lane_iota(logit.shape)
    big = jnp.int32(1 << 20)
    is_g = (lane >= N_EXPERTS) & (lane < N_EXPERTS + N_GROUPS)
    gl = jnp.where(is_g, logit, NEG_INF)
    gmax = jnp.max(gl, axis=-1, keepdims=True)
    gsel = jnp.min(jnp.where(gl == gmax, lane - N_EXPERTS, big), axis=-1, keepdims=True)
    gw = 1.0 / jnp.sum(jnp.where(is_g, jnp.exp(gl - gmax), 0.0), axis=-1, keepdims=True)
    in_grp = (lane < N_EXPERTS) & ((lane // EXPERTS_PER_GROUP) == gsel)
    el = jnp.where(in_grp, logit, NEG_INF)
    v1 = jnp.max(el, axis=-1, keepdims=True)
    i1 = jnp.min(jnp.where(el == v1, lane, big), axis=-1, keepdims=True)
    el2 = jnp.where(lane == i1, NEG_INF, el)
    v2 = jnp.max(el2, axis=-1, keepdims=True)
    i2 = jnp.min(jnp.where(el2 == v2, lane, big), axis=-1, keepdims=True)
    e21 = jnp.exp(v2 - v1)
    w1 = gw / (1.0 + e21)
    w2 = gw * e21 / (1.0 + e21)
    route_ref[...] = jnp.where(lane == 0, i1.astype(F32),
                               jnp.where(lane == 1, i2.astype(F32),
                                         jnp.where(lane == 2, w1, jnp.where(lane == 3, w2, 0.0))))


def _merge_call(lay, layer, h, o, x, mod, w, *, with_ctx):
    T, D, tm = lay["T"], lay["D"], lay["tm"]
    nct, tpb, B = lay["nct"], lay["tpb"], lay["B"]
    off = 0 if with_ctx else nct
    n_tiles = T // tm - off

    def mod_row(i):
        return jnp.where(i < nct, B, (i - nct) // tpb)

    row = lambda i: (i + off, 0)
    const2 = lambda i: (0, 0)
    const3 = lambda i: (0, 0, 0)
    in_specs = [pl.BlockSpec((tm, D), row)]
    in_specs += [pl.BlockSpec((tm, BRANCH_W), row)] * 4
    in_specs += [pl.BlockSpec((tm, D), row),
                 pl.BlockSpec((1, 6, D), lambda i: (layer * lay["mod_rows"] + mod_row(i + off), 0, 0)),
                 pl.BlockSpec((1, D), const2),
                 pl.BlockSpec(w["w_gate"].shape, const3),
                 pl.BlockSpec(w["b_gate"].shape, const3),
                 pl.BlockSpec(w["w_branch"].shape, const3),
                 pl.BlockSpec((D, D), const2),
                 pl.BlockSpec((2, D, LANES), const3),
                 pl.BlockSpec((1, LANES), const2)]
    out_shape = [jax.ShapeDtypeStruct((T, D), F32),
                 jax.ShapeDtypeStruct((T * SUBLANES, LANES), F32),
                 jax.ShapeDtypeStruct((T, LANES), F32)]
    out_specs = [pl.BlockSpec((tm, D), row),
                 pl.BlockSpec((tm * SUBLANES, LANES), row),
                 pl.BlockSpec((tm, LANES), row)]
    return pl.pallas_call(
        functools.partial(_merge_kernel, tm=tm),
        out_shape=out_shape, grid=(n_tiles,), in_specs=in_specs, out_specs=out_specs,
        compiler_params=_cparams(("parallel",)),
        name="merge",
    )(h, o["A"], o["B"], o["C"], o["D"], x, mod, w["g_ffn"], w["w_gate"], w["b_gate"],
      w["w_branch"], w["w_out"], w["w_route"], w["b_route"])


def _moe_kernel(tab_ref, tok_ref, h2_ref, sw_ref, w1_ref, w3_ref, w2_ref, f_ref,
                xg_ref, yb_ref, *, blk, slots, tile_off):
    ti = pl.program_id(0)
    e = pl.program_id(1)

    @pl.when(e == 0)
    def _():
        f_ref[...] = jnp.zeros_like(f_ref)
        xg_ref[...] = jnp.zeros_like(xg_ref)

    base = ((ti + tile_off) * N_EXPERTS + e) * 2
    seg0 = tab_ref[base]
    n_pad = tab_ref[base + 1]
    tok_base = (ti + tile_off) * slots
    n_blocks = (n_pad + blk - 1) // blk

    def block_body(b, carry):
        off = pl.multiple_of(seg0 + b * blk, SUBLANES)
        n_grp = jnp.minimum(n_pad - b * blk, blk) // SUBLANES

        def gather(gi, c):
            for u in range(SUBLANES):
                r = gi * SUBLANES + u
                tok = tok_ref[tok_base + off + r]
                src = pl.multiple_of(tok * SUBLANES, SUBLANES)
                dst = pl.multiple_of(r * SUBLANES, SUBLANES)
                xg_ref[pl.ds(dst, SUBLANES), :] = h2_ref[pl.ds(src, SUBLANES), :]
            return c

        lax.fori_loop(0, n_grp, gather, 0)
        xb = _load_token_major(xg_ref, blk).astype(BF16)
        a = jnp.dot(xb, w1_ref[0], preferred_element_type=F32)
        g = jnp.dot(xb, w3_ref[0], preferred_element_type=F32)
        hid = (a * _sigmoid(a) * g).astype(BF16)
        y = jnp.dot(hid, w2_ref[0], preferred_element_type=F32)
        y = y * sw_ref[0, pl.ds(off, blk), :]
        for c in range(SUBLANES):
            yb_ref[pl.ds(c, blk, stride=SUBLANES), :] = y[:, c * LANES:(c + 1) * LANES]

        def scatter(gi, c):
            for u in range(SUBLANES):
                r = gi * SUBLANES + u
                tok = tok_ref[tok_base + off + r]
                dst = pl.multiple_of(tok * SUBLANES, SUBLANES)
                src = pl.multiple_of(r * SUBLANES, SUBLANES)
                f_ref[pl.ds(dst, SUBLANES), :] += yb_ref[pl.ds(src, SUBLANES), :]
            return c

        lax.fori_loop(0, n_grp, scatter, 0)
        return carry

    lax.fori_loop(0, n_blocks, block_body, 0)


def _moe_call(lay, route, h2, w, *, with_ctx):
    T, D, tt = lay["T"], lay["D"], lay["tt"]
    blk = lay["moe_blk"]
    n_tiles_all = T // tt
    tile_off = 0 if with_ctx else lay["ncb"] * ROW_BLK // tt
    n_tiles = n_tiles_all - tile_off
    n_assign = 2 * tt
    n_fill = N_EXPERTS * SUBLANES
    slots = n_assign + n_fill + blk

    eid = route[:, 0:2].astype(jnp.int32).reshape(n_tiles_all, n_assign)
    wts = route[:, 2:4].reshape(n_tiles_all, n_assign)
    tok = jnp.broadcast_to(jnp.arange(n_assign, dtype=jnp.int32)[None, :] // 2, eid.shape)
    ex = jnp.arange(N_EXPERTS, dtype=jnp.int32)
    counts = jnp.sum(eid[:, :, None] == ex[None, None, :], axis=1, dtype=jnp.int32)
    n_dummy = (-counts) % SUBLANES
    fill_key = jnp.where(jnp.arange(SUBLANES, dtype=jnp.int32)[None, None, :] < n_dummy[:, :, None],
                         ex[None, :, None], N_EXPERTS).reshape(n_tiles_all, n_fill)
    zeros_i = jnp.zeros((n_tiles_all, n_fill), jnp.int32)
    _, slot_tok, slot_w = lax.sort(
        (jnp.concatenate([eid, fill_key], axis=1), jnp.concatenate([tok, zeros_i], axis=1),
         jnp.concatenate([wts, zeros_i.astype(F32)], axis=1)),
        dimension=1, is_stable=True, num_keys=1)
    slot_tok = jnp.pad(slot_tok, ((0, 0), (0, blk)))
    slot_w = jnp.pad(slot_w, ((0, 0), (0, blk)))
    padded = counts + n_dummy
    seg0 = jnp.cumsum(padded, axis=1) - padded
    tab = jnp.stack([seg0, padded], axis=-1).reshape(-1).astype(jnp.int32)

    return pl.pallas_call(
        functools.partial(_moe_kernel, blk=blk, slots=slots, tile_off=tile_off),
        out_shape=jax.ShapeDtypeStruct((T * SUBLANES, LANES), F32),
        grid_spec=pltpu.PrefetchScalarGridSpec(
            num_scalar_prefetch=2,
            grid=(n_tiles, N_EXPERTS),
            in_specs=[
                pl.BlockSpec((tt * SUBLANES, LANES), lambda t, e, *_: (t + tile_off, 0)),
                pl.BlockSpec((1, slots, 1), lambda t, e, *_: (t + tile_off, 0, 0)),
                pl.BlockSpec((1, D, EXPERT_FF), lambda t, e, *_: (e, 0, 0)),
                pl.BlockSpec((1, D, EXPERT_FF), lambda t, e, *_: (e, 0, 0)),
                pl.BlockSpec((1, EXPERT_FF, D), lambda t, e, *_: (e, 0, 0)),
            ],
            out_specs=pl.BlockSpec((tt * SUBLANES, LANES), lambda t, e, *_: (t + tile_off, 0)),
            scratch_shapes=[pltpu.VMEM((blk * SUBLANES, LANES), F32),
                            pltpu.VMEM((blk * SUBLANES, LANES), F32)],
        ),
        compiler_params=_cparams(("parallel", "arbitrary")),
        name="moe_experts",
    )(tab, slot_tok.reshape(-1), h2, slot_w.reshape(n_tiles_all, slots, 1),
      w["w_ff1"], w["w_ff3"], w["w_ff2"])


def _final_kernel(x_ref, f_ref, mod_ref, g_ref, o_ref, *, tm):
    xf = x_ref[...] + mod_ref[0, 5:6, :] * _load_token_major(f_ref, tm)
    o_ref[...] = _rms(xf) * g_ref[...]


def _final_call(lay, layer, x1, f, mod, g_final):
    T, D, tm = lay["T"], lay["D"], lay["tm"]
    nct, tpb = lay["nct"], lay["tpb"]
    n_lat = T // tm - nct
    return pl.pallas_call(
        functools.partial(_final_kernel, tm=tm),
        out_shape=jax.ShapeDtypeStruct((n_lat * tm, D), F32),
        grid=(n_lat,),
        in_specs=[pl.BlockSpec((tm, D), lambda i: (i + nct, 0)),
                  pl.BlockSpec((tm * SUBLANES, LANES), lambda i: (i + nct, 0)),
                  pl.BlockSpec((1, 6, D), lambda i: (layer * lay["mod_rows"] + i // tpb, 0, 0)),
                  pl.BlockSpec((1, D), lambda i: (0, 0))],
        out_specs=pl.BlockSpec((tm, D), lambda i: (i, 0)),
        compiler_params=_cparams(("parallel",)),
        name="final_norm",
    )(x1, f, mod, g_final)


def _select_cols(wm, segs, scale=None):
    parts = []
    for k, (start, width) in enumerate(segs):
        if start is None:
            parts.append(jnp.zeros((wm.shape[0], width), wm.dtype))
        else:
            blk = wm[:, start:start + width]
            parts.append(blk if scale is None or scale[k] is None else blk * scale[k])
    return jnp.concatenate(parts, axis=1)


def _prep_layer(l, p):
    a_cols = A_Q_RANK + A_KV_RANK + A_ROPE
    b_off = a_cols
    c_off = b_off + 512
    d_off = c_off + 768
    qk_scale = HEAD_DIM ** -0.5 * LOG2E
    gqa_q = lambda off: [(off + hh * HEAD_DIM, HEAD_DIM) for hh in _GQA_PERM]
    segs = ([(0, 256), (256, 128), (None, 64), (384, 32), (None, 32)]
            + gqa_q(b_off) + [(b_off + 256, 128), (b_off + 384, 128)]
            + [(c_off, 256), (c_off + 256, 256), (c_off + 512, 256)]
            + gqa_q(d_off) + [(d_off + 256, 128), (d_off + 384, 128)])
    scale = [None] * len(segs)
    for k in (5, 6, 7, 8, 11):
        scale[k] = qk_scale
    w_in = _select_cols(p["w_in"][l], segs, scale).astype(BF16)
    assert w_in.shape[1] == PROJ_COLS

    hq = A_NOPE + A_ROPE
    segs_q = []
    for hh in range(N_HEADS):
        segs_q += [(hh * hq, hq), (None, LANES - hq)]
    w_q_b = _select_cols(p["w_q_b"][l], segs_q).astype(BF16)
    hk = A_NOPE + HEAD_DIM
    segs_k = []
    for hh in range(N_HEADS):
        segs_k += [(hh * hk, A_NOPE), (None, LANES - A_NOPE)]
    segs_k += [(hh * hk + A_NOPE, HEAD_DIM) for hh in range(N_HEADS)]
    w_kv_b = _select_cols(p["w_kv_b"][l], segs_k).astype(BF16)

    wb = p["w_branch"][l]
    perm_rows = lambda m: jnp.concatenate([m[hh * HEAD_DIM:(hh + 1) * HEAD_DIM] for hh in _GQA_PERM], axis=0)
    w_branch = jnp.stack([wb[0], perm_rows(wb[1]), wb[2], perm_rows(wb[3])]).astype(BF16)

    d = p["w_in"].shape[1]
    w_route = jnp.zeros((d, LANES), F32)
    w_route = w_route.at[:, :N_EXPERTS].set(p["w_router"][l]).at[:, N_EXPERTS:N_EXPERTS + N_GROUPS].set(p["w_group"][l])
    b_route = jnp.zeros((1, LANES), F32)
    b_route = b_route.at[0, :N_EXPERTS].set(p["b_router"][l]).at[0, N_EXPERTS:N_EXPERTS + N_GROUPS].set(p["b_group"][l])
    return {
        "g_mix": p["g_norm_mix"][l][None, :],
        "w_in": w_in,
        "g_q_a": p["g_q_a"][l][None, :],
        "w_q_b": w_q_b,
        "g_kv_a": p["g_kv_a"][l][None, :],
        "w_kv_b": w_kv_b,
        "g_q_d": (jnp.tile(p["g_q_d"][l], 2) * qk_scale)[None, :],
        "g_k_d": jnp.tile(p["g_k_d"][l], 2)[None, :],
        "sink": p["sink_b"][l],
        "rpb": p["rpb_c"][l],
        "w_gate": p["w_gate"][l].astype(BF16),
        "b_gate": p["b_gate"][l][:, None, :],
        "w_branch": w_branch,
        "w_out": p["w_out"][l].astype(BF16),
        "g_ffn": p["g_norm_ffn"][l][None, :],
        "w_route": jnp.stack([w_route.astype(BF16),
                              (w_route - w_route.astype(BF16).astype(F32)).astype(BF16)]),
        "b_route": b_route,
        "w_ff1": p["w_ff1"][l].astype(BF16),
        "w_ff3": p["w_ff3"][l].astype(BF16),
        "w_ff2": p["w_ff2"][l].astype(BF16),
    }


def _rope_tables(S, tm):
    t = jnp.arange(S, dtype=jnp.int32)
    rows = (t // GRID_W).astype(F32)
    cols = (t % GRID_W).astype(F32)

    def cs(rot):
        half = rot // 2
        inv = ROPE_THETA ** (-jnp.arange(0, half, 2, dtype=F32) / half)
        ar_, ac_ = rows[:, None] * inv, cols[:, None] * inv
        cos = jnp.concatenate([jnp.cos(ar_), jnp.cos(ar_), jnp.cos(ac_), jnp.cos(ac_)], axis=-1)
        sin = jnp.concatenate([-jnp.sin(ar_), jnp.sin(ar_), -jnp.sin(ac_), jnp.sin(ac_)], axis=-1)
        return cos, sin

    cos64, sin64 = cs(HEAD_DIM)
    cos32, sin32 = cs(A_ROPE)
    ones = lambda n: jnp.ones((S, n), F32)
    zeros = lambda n: jnp.zeros((S, n), F32)
    tabs = {
        "cos_h": jnp.concatenate([cos64, cos64], axis=-1),
        "sin_h": jnp.concatenate([sin64, sin64], axis=-1),
        "cos_a": jnp.concatenate([ones(A_NOPE), cos32, ones(LANES - A_NOPE - A_ROPE)], axis=-1),
        "sin_a": jnp.concatenate([zeros(A_NOPE), sin32, zeros(LANES - A_NOPE - A_ROPE)], axis=-1),
    }
    ident = {"cos_h": 1.0, "sin_h": 0.0, "cos_a": 1.0, "sin_a": 0.0}
    return {k: jnp.concatenate([v, jnp.full((tm, LANES), ident[k], F32)], axis=0) for k, v in tabs.items()}


def _layout(B, S, n_ctx, D):
    assert n_ctx == ROW_BLK and S % 1024 == 0 and S // GRID_W >= 3 * (ROW_BLK // GRID_W)
    T = B * (n_ctx + S)
    tm = 512 if (B * n_ctx) % 512 == 0 else 256
    tk_dense = 1024 if (B * n_ctx) % 1024 == 0 else (512 if (B * n_ctx) % 512 == 0 else 256)
    tt = 2048 if (B * n_ctx) % 2048 == 0 else B * n_ctx
    assert S % tt == 0
    return {
        "B": B, "S": S, "D": D, "T": T, "tm": tm,
        "ncb": B * n_ctx // ROW_BLK,
        "lb": S // ROW_BLK,
        "nct": B * n_ctx // tm,
        "tpb": S // tm,
        "tk_dense": tk_dense,
        "tq_dense": tm,
        "tt": tt, "moe_blk": 256,
        "mod_rows": 16,
    }


def kernel(x, c, ctx, c_ctx, w_mod, b_mod, g_norm_mix, w_in, g_q_a, w_q_b, g_kv_a, w_kv_b, sink_b, rpb_c,
           g_q_d, g_k_d, w_gate, b_gate, w_branch, w_out, g_norm_ffn, w_group, b_group, w_router, b_router,
           w_ff1, w_ff3, w_ff2, g_final):
    B, S, D = x.shape
    n_ctx = ctx.shape[1]
    depth = w_mod.shape[0]
    lay = _layout(B, S, n_ctx, D)
    params = dict(w_in=w_in, g_norm_mix=g_norm_mix, g_q_a=g_q_a, w_q_b=w_q_b, g_kv_a=g_kv_a, w_kv_b=w_kv_b,
                  sink_b=sink_b, rpb_c=rpb_c, g_q_d=g_q_d, g_k_d=g_k_d, w_gate=w_gate, b_gate=b_gate,
                  w_branch=w_branch, w_out=w_out, g_norm_ffn=g_norm_ffn, w_group=w_group, b_group=b_group,
                  w_router=w_router, b_router=b_router, w_ff1=w_ff1, w_ff3=w_ff3, w_ff2=w_ff2)

    c_all = jnp.zeros((lay["mod_rows"], D), F32).at[:B].set(c).at[B].set(c_ctx)
    mod = _modulation(c_all, w_mod, b_mod).reshape(depth * lay["mod_rows"], 6, D)
    tabs = _rope_tables(S, lay["tm"])
    win_bias = _window_bias(lay["lb"])
    xf = jnp.concatenate([ctx.reshape(B * n_ctx, D), x.reshape(B * S, D)], axis=0)

    f = None
    for l in range(depth):
        with_ctx = l < depth - 1
        w = _prep_layer(l, params)
        xf, pr = _proj_call(lay, xf, f, mod, mod, l, w, tabs)
        o = {}
        for kind, kl in (("A", "a"), ("B", "b"), ("C", "c"), ("D", "d")):
            q, k, v = pr["q" + kl], pr["k" + kl], pr["v" + kl]
            sink = w["sink"] if kind == "B" else None
            bias = {"B": win_bias, "C": _neighbourhood_bias(w["rpb"], S // GRID_W)}.get(kind)
            o[kind] = _attn_latent_call(lay, kind, q, k, v, sink=sink, bias=bias)
            if with_ctx:
                o[kind] = _attn_context_call(lay, kind, q, k, v, o[kind], sink=sink)
        xf, h2, route = _merge_call(lay, l, pr["h"], o, xf, mod, w, with_ctx=with_ctx)
        f = _moe_call(lay, route, h2, w, with_ctx=with_ctx)
    out = _final_call(lay, depth - 1, xf, f, mod, g_final[None, :])
    return out.reshape(B, S, D)
```

```python
import functools

import numpy as np
import jax
import jax.numpy as jnp
from jax import lax
from jax.experimental import pallas as pl
from jax.experimental.pallas import tpu as pltpu

F32 = jnp.float32
BF16 = jnp.bfloat16
HIGHEST = lax.Precision.HIGHEST

GRID_W = 64
ROPE_THETA = 10000.0
EPS = 1e-6
NEG_INF = -1e30
LOG2E = 1.4426950408889634
HEAD_DIM = 64
N_HEADS = 4
BRANCH_W = 256
A_Q_RANK = 256
A_KV_RANK = 128
A_NOPE = 64
A_ROPE = 32
NA_KH = 8
NA_KW = 16
WINDOW = 128
N_GROUPS = 4
EXPERTS_PER_GROUP = 8
N_EXPERTS = 32
EXPERT_FF = 256

LANES = 128
SUBLANES = 8
ROW_BLK = 256
MERGE_TN = 256
VMEM_LIMIT = 56 * 1024 * 1024

_PROJ_GROUPS = (("cq", 256), ("ckv", 128), ("kr", 128), ("qb", 256), ("kb", 128), ("vb", 128),
                ("qc", 256), ("kc", 256), ("vc", 256), ("qd", 256), ("kd", 128), ("vd", 128))
_PROJ_OFF = {}
_o = 0
for _n, _w in _PROJ_GROUPS:
    _PROJ_OFF[_n] = (_o, _w)
    _o += _w
PROJ_COLS = _o
_GQA_PERM = (0, 2, 1, 3)


def _cparams(sem):
    return pltpu.CompilerParams(dimension_semantics=sem, vmem_limit_bytes=VMEM_LIMIT)


def _lane_iota(shape):
    return lax.broadcasted_iota(jnp.int32, shape, len(shape) - 1)


def _sigmoid(x):
    return 1.0 / (1.0 + jnp.exp(-x))


def _mod_kernel(c_ref, w_ref, b_ref, o_ref):
    cf = c_ref[...]
    s = cf * _sigmoid(cf)
    o_ref[0] = jnp.dot(s, w_ref[0], precision=HIGHEST, preferred_element_type=F32) + b_ref[0]


def _modulation(c_all, w_mod, b_mod):
    n_layers, d, n_out = w_mod.shape
    rows = c_all.shape[0]
    tn = 1536
    return pl.pallas_call(
        _mod_kernel,
        out_shape=jax.ShapeDtypeStruct((n_layers, rows, n_out), F32),
        grid=(n_layers, n_out // tn),
        in_specs=[pl.BlockSpec((rows, d), lambda l, j: (0, 0)),
                  pl.BlockSpec((1, d, tn), lambda l, j: (l, 0, j)),
                  pl.BlockSpec((1, 1, tn), lambda l, j: (l, 0, j))],
        out_specs=pl.BlockSpec((1, rows, tn), lambda l, j: (l, 0, j)),
        compiler_params=_cparams(("arbitrary", "arbitrary")),
        name="modulation",
    )(c_all, w_mod, b_mod.reshape(n_layers, 1, n_out))


def _rms(x):
    return x * lax.rsqrt(jnp.mean(x * x, axis=-1, keepdims=True) + EPS)


def _swap_blocks(x, blk):
    lane = _lane_iota(x.shape)
    up = pltpu.roll(x, LANES - blk, 1)
    dn = pltpu.roll(x, blk, 1)
    return jnp.where((lane // blk) % 2 == 0, up, dn)


def _rope(x, cos, sin, blk):
    return x * cos + _swap_blocks(x, blk) * sin


def _pair_norm(x, g):
    lo = _lane_iota(x.shape) < HEAD_DIM
    sq = x * x
    s_lo = jnp.sum(jnp.where(lo, sq, 0.0), axis=-1, keepdims=True)
    s_hi = jnp.sum(jnp.where(lo, 0.0, sq), axis=-1, keepdims=True)
    ms = jnp.where(lo, s_lo, s_hi) * (1.0 / HEAD_DIM)
    return x * lax.rsqrt(ms + EPS) * g


def _load_token_major(ref, rows):
    return jnp.concatenate(
        [ref[pl.ds(c, rows, stride=SUBLANES), :] for c in range(SUBLANES)], axis=-1)


def _proj_kernel(*refs, with_f, tm, scale_a):
    it = iter(refs)
    x_ref = next(it)
    if with_f:
        f_ref = next(it)
        modp_ref = next(it)
    mod_ref = next(it)
    gmix_ref, win_ref, gqa_ref, wqb_ref, gkva_ref, wkvb_ref, gqd_ref, gkd_ref = (next(it) for _ in range(8))
    cosh_ref, sinh_ref, cosa_ref, sina_ref = (next(it) for _ in range(4))
    if with_f:
        x2_ref = next(it)
    h_ref = next(it)
    qa_ref, ka_ref, va_ref, qb_ref, kb_ref, vb_ref, qc_ref, kc_ref, vc_ref, qd_ref, kd_ref, vd_ref = (
        next(it) for _ in range(12))

    xf = x_ref[...]
    if with_f:
        xf = xf + modp_ref[0, 5:6, :] * _load_token_major(f_ref, tm)
        x2_ref[...] = xf
    h = _rms(xf) * gmix_ref[...] * (1.0 + mod_ref[0, 1:2, :]) + mod_ref[0, 0:1, :]
    hb = h.astype(BF16)
    h_ref[...] = hb
    p = jnp.dot(hb, win_ref[...], preferred_element_type=F32)

    def grp(name):
        o, w = _PROJ_OFF[name]
        return p[:, o:o + w]

    cosh, sinh = cosh_ref[...], sinh_ref[...]
    cosa, sina = cosa_ref[...], sina_ref[...]

    cq = (_rms(grp("cq")) * gqa_ref[...]).astype(BF16)
    qa = jnp.dot(cq, wqb_ref[...], preferred_element_type=F32)
    for hd in range(N_HEADS):
        sl = slice(hd * LANES, (hd + 1) * LANES)
        qa_ref[:, sl] = (_rope(qa[:, sl], cosa, sina, 8) * scale_a).astype(BF16)
    ckv = (_rms(grp("ckv")) * gkva_ref[...]).astype(BF16)
    kva = jnp.dot(ckv, wkvb_ref[...], preferred_element_type=F32)
    kr = _rope(grp("kr"), cosa, sina, 8)
    for hd in range(N_HEADS):
        sl = slice(hd * LANES, (hd + 1) * LANES)
        ka_ref[:, sl] = (kva[:, sl] + kr).astype(BF16)
    va_ref[...] = kva[:, N_HEADS * LANES:].astype(BF16)

    qb = grp("qb")
    for j in range(2):
        sl = slice(j * LANES, (j + 1) * LANES)
        qb_ref[:, sl] = _rope(qb[:, sl], cosh, sinh, 16).astype(BF16)
    kb_ref[...] = _rope(grp("kb"), cosh, sinh, 16).astype(BF16)
    vb_ref[...] = grp("vb").astype(BF16)

    qc_ref[...] = grp("qc").astype(BF16)
    kc_ref[...] = grp("kc").astype(BF16)
    vc_ref[...] = grp("vc").astype(BF16)

    qd = grp("qd")
    for j in range(2):
        sl = slice(j * LANES, (j + 1) * LANES)
        qd_ref[:, sl] = _rope(_pair_norm(qd[:, sl], gqd_ref[...]), cosh, sinh, 16).astype(BF16)
    kd_ref[...] = _rope(_pair_norm(grp("kd"), gkd_ref[...]), cosh, sinh, 16).astype(BF16)
    vd_ref[...] = grp("vd").astype(BF16)


def _proj_call(lay, x, f, modp, mod, layer, w, tabs):
    T, D, tm = lay["T"], lay["D"], lay["tm"]
    nct, tpb, B = lay["nct"], lay["tpb"], lay["B"]
    with_f = f is not None
    n_tiles = T // tm

    def mod_row(i):
        return jnp.where(i < nct, B, (i - nct) // tpb)

    def tab_blk(i):
        return jnp.where(i < nct, tpb, (i - nct) % tpb)

    row = lambda i: (i, 0)
    const = lambda i: (0, 0)
    in_specs = [pl.BlockSpec((tm, D), row)]
    args = [x]
    if with_f:
        in_specs += [pl.BlockSpec((tm * SUBLANES, LANES), row),
                     pl.BlockSpec((1, 6, D), lambda i: ((layer - 1) * lay["mod_rows"] + mod_row(i), 0, 0))]
        args += [f, modp]
    in_specs += [pl.BlockSpec((1, 6, D), lambda i: (layer * lay["mod_rows"] + mod_row(i), 0, 0))]
    args += [mod]
    for name in ("g_mix", "w_in", "g_q_a", "w_q_b", "g_kv_a", "w_kv_b", "g_q_d", "g_k_d"):
        a = w[name]
        in_specs.append(pl.BlockSpec(a.shape, const))
        args.append(a)
    for tname in ("cos_h", "sin_h", "cos_a", "sin_a"):
        in_specs.append(pl.BlockSpec((tm, LANES), lambda i: (tab_blk(i), 0)))
        args.append(tabs[tname])

    widths = [("h", D), ("qa", 512), ("ka", 512), ("va", 256), ("qb", 256), ("kb", 128), ("vb", 128),
              ("qc", 256), ("kc", 256), ("vc", 256), ("qd", 256), ("kd", 128), ("vd", 128)]
    out_shape, out_specs = [], []
    if with_f:
        out_shape.append(jax.ShapeDtypeStruct((T, D), F32))
        out_specs.append(pl.BlockSpec((tm, D), row))
    for _, wd in widths:
        out_shape.append(jax.ShapeDtypeStruct((T, wd), BF16))
        out_specs.append(pl.BlockSpec((tm, wd), row))

    outs = pl.pallas_call(
        functools.partial(_proj_kernel, with_f=with_f, tm=tm,
                          scale_a=float((A_NOPE + A_ROPE) ** -0.5 * LOG2E)),
        out_shape=out_shape, grid=(n_tiles,), in_specs=in_specs, out_specs=out_specs,
        compiler_params=_cparams(("parallel",)),
        name="proj_in",
    )(*args)
    outs = list(outs)
    x2 = outs.pop(0) if with_f else x
    names = [n for n, _ in widths]
    return x2, dict(zip(names, outs))


_NT = (((1,), (1,)), ((), ()))


def _head_plan(kind):
    if kind == "A":
        return tuple((r, None, r, r // 2) for r in range(N_HEADS))
    if kind == "C":
        return tuple((r // 2, r % 2, r // 2, r // 2) for r in range(N_HEADS))
    return tuple((r // 2, r % 2, 0, 0) for r in range(N_HEADS))


def _head_query(q_ref, plan_r):
    qt, half, _, _ = plan_r
    src = q_ref[:, qt * LANES:(qt + 1) * LANES]
    if half is None:
        return src
    lane = _lane_iota(src.shape)
    keep = (lane < HEAD_DIM) if half == 0 else (lane >= HEAD_DIM)
    return jnp.where(keep, src, jnp.zeros_like(src))


def _score_chunks(q, k_blocks, bias_blocks):
    chunks = []
    for kb, bb in zip(k_blocks, bias_blocks):
        s = lax.dot_general(q, kb, _NT, preferred_element_type=F32)
        if bb is not None:
            s = s + bb
        chunks += [s[:, c * LANES:(c + 1) * LANES] for c in range(s.shape[1] // LANES)]
    return chunks


def _row_max(chunks):
    m = functools.reduce(jnp.maximum, chunks)
    return jnp.broadcast_to(jnp.max(m, axis=-1, keepdims=True), m.shape)


def _weighted_values(p_chunks, v_blocks, half):
    pv, idx = None, 0
    for vb in v_blocks:
        n = vb.shape[0] // LANES
        p = jnp.concatenate(p_chunks[idx:idx + n], axis=1).astype(BF16)
        idx += n
        lane = _lane_iota(vb.shape)
        own = (lane < HEAD_DIM) if half == 0 else (lane >= HEAD_DIM)
        d = jnp.dot(p, jnp.where(own, vb, jnp.ones_like(vb)), preferred_element_type=F32)
        pv = d if pv is None else pv + d
    return pv


def _softmax_once(q, k_blocks, v_blocks, bias_blocks, sink, half):
    chunks = _score_chunks(q, k_blocks, bias_blocks)
    m = _row_max(chunks)
    if sink is not None:
        m = jnp.maximum(m, sink)
    p = [jnp.exp2(c - m) for c in chunks]
    pv = _weighted_values(p, v_blocks, half)
    l = pltpu.roll(pv, HEAD_DIM, 1)
    if sink is not None:
        l = l + jnp.exp2(sink - m)
    return pv / l


def _store_heads(o_ref, outs):
    lane = _lane_iota(outs[0].shape)
    for g in range(2):
        o_ref[:, g * LANES:(g + 1) * LANES] = jnp.where(
            lane < HEAD_DIM, outs[2 * g], outs[2 * g + 1]).astype(o_ref.dtype)


def _attn_band_kernel(*refs, kind, lb, n_kv):
    tq = ROW_BLK
    it = iter(refs)
    q_ref = next(it)
    k_refs = [next(it) for _ in range(n_kv)]
    v_refs = [next(it) for _ in range(n_kv)]
    bias_ref = next(it) if n_kv > 1 else None
    sink_ref = next(it) if kind == "B" else None
    if n_kv == 1:
        next(it)
    o_ref = next(it)
    t = pl.program_id(0) % lb
    var = jnp.where(t == 0, 0, jnp.where(t == lb - 1, 2, 1))
    outs = []
    for r, plan_r in enumerate(_head_plan(kind)):
        _, _, kt, vt = plan_r
        ksl = slice(kt * LANES, (kt + 1) * LANES)
        vsl = slice(vt * LANES, (vt + 1) * LANES)
        bias = [None]
        for j in range(n_kv - 1):
            if kind == "C":
                bias.append(bias_ref[var, j, r * tq:(r + 1) * tq, :])
            else:
                bias.append(bias_ref[var, j])
        sink = sink_ref[_GQA_PERM[r]] * LOG2E if kind == "B" else None
        outs.append(_softmax_once(_head_query(q_ref, plan_r), [k[:, ksl] for k in k_refs],
                                  [v[:, vsl] for v in v_refs], bias, sink, r % 2))
    _store_heads(o_ref, outs)


def _attn_dense_kernel(q_ref, kc_ref, vc_ref, kl_ref, vl_ref, o_ref, qs_ref, m_ref, acc_ref,
                       *, kind, n_steps, tq):
    s = pl.program_id(1)
    plan = _head_plan(kind)

    def update(first):
        for r, (_, _, kt, vt) in enumerate(plan):
            rows = slice(r * tq, (r + 1) * tq)
            ksl = slice(kt * LANES, (kt + 1) * LANES)
            vsl = slice(vt * LANES, (vt + 1) * LANES)
            if first:
                q = _head_query(q_ref, plan[r])
                qs_ref[rows, :] = q
                k_blocks = [kc_ref[:, ksl], kl_ref[:, ksl]]
                v_blocks = [vc_ref[:, vsl], vl_ref[:, vsl]]
            else:
                q = qs_ref[rows, :]
                k_blocks = [kl_ref[:, ksl]]
                v_blocks = [vl_ref[:, vsl]]
            chunks = _score_chunks(q, k_blocks, [None] * len(k_blocks))
            m_cur = _row_max(chunks)
            if first:
                m_new = m_cur
            else:
                m_prev = m_ref[rows, :]
                m_new = jnp.maximum(m_prev, m_cur)
                alpha = jnp.exp2(m_prev - m_new)
            p = [jnp.exp2(c - m_new) for c in chunks]
            pv = _weighted_values(p, v_blocks, r % 2)
            if first:
                acc_ref[rows, :] = pv
            else:
                acc_ref[rows, :] = alpha * acc_ref[rows, :] + pv
            m_ref[rows, :] = m_new

    @pl.when(s == 0)
    def _():
        update(True)

    @pl.when(s > 0)
    def _():
        update(False)

    @pl.when(s == n_steps - 1)
    def _():
        outs = []
        for r in range(N_HEADS):
            rows = slice(r * tq, (r + 1) * tq)
            acc = acc_ref[rows, :]
            outs.append(acc / pltpu.roll(acc, HEAD_DIM, 1))
        _store_heads(o_ref, outs)


def _attn_latent_call(lay, kind, q, k, v, *, sink=None, bias=None):
    T, B, ncb, lb, S = lay["T"], lay["B"], lay["ncb"], lay["lb"], lay["S"]
    tq = ROW_BLK
    qw, kw, vw = q.shape[1], k.shape[1], v.shape[1]
    out_shape = jax.ShapeDtypeStruct((T, BRANCH_W), BF16)
    if kind in ("A", "D"):
        tk, tq = lay["tk_dense"], lay["tq_dense"]
        n_steps = S // tk
        lat0 = ncb * ROW_BLK // tk
        q0 = ncb * ROW_BLK // tq
        qpb = S // tq
        lat_blk = lambda i, s: (lat0 + (i // qpb) * n_steps + s, 0)
        return pl.pallas_call(
            functools.partial(_attn_dense_kernel, kind=kind, n_steps=n_steps, tq=tq),
            out_shape=out_shape,
            grid=(B * qpb, n_steps),
            in_specs=[pl.BlockSpec((tq, qw), lambda i, s: (q0 + i, 0)),
                      pl.BlockSpec((ROW_BLK, kw), lambda i, s: (i // qpb, 0)),
                      pl.BlockSpec((ROW_BLK, vw), lambda i, s: (i // qpb, 0)),
                      pl.BlockSpec((tk, kw), lat_blk),
                      pl.BlockSpec((tk, vw), lat_blk)],
            out_specs=pl.BlockSpec((tq, BRANCH_W), lambda i, s: (q0 + i, 0)),
            scratch_shapes=[pltpu.VMEM((N_HEADS * tq, LANES), BF16),
                            pltpu.VMEM((N_HEADS * tq, LANES), F32),
                            pltpu.VMEM((N_HEADS * tq, LANES), F32)],
            compiler_params=_cparams(("parallel", "arbitrary")),
            name="attn_" + kind,
        )(q, k, v, k, v)

    def nb(i, d):
        return (ncb + (i // lb) * lb + jnp.clip(i % lb + d, 0, lb - 1), 0)

    kv_maps = [lambda i: (i // lb, 0), lambda i: nb(i, -1), lambda i: nb(i, 0), lambda i: nb(i, 1)]
    in_specs = [pl.BlockSpec((tq, qw), lambda i: (ncb + i, 0))]
    in_specs += [pl.BlockSpec((ROW_BLK, kw), m) for m in kv_maps]
    in_specs += [pl.BlockSpec((ROW_BLK, vw), m) for m in kv_maps]
    in_specs.append(pl.BlockSpec(bias.shape, lambda i: (0,) * bias.ndim))
    args = [q, k, k, k, k, v, v, v, v, bias]
    if kind == "B":
        in_specs.append(pl.BlockSpec(memory_space=pltpu.SMEM))
        args.append(sink)
    return pl.pallas_call(
        functools.partial(_attn_band_kernel, kind=kind, lb=lb, n_kv=4),
        out_shape=out_shape,
        grid=(B * lb,),
        in_specs=in_specs,
        out_specs=pl.BlockSpec((tq, BRANCH_W), lambda i: (ncb + i, 0)),
        compiler_params=_cparams(("parallel",)),
        name="attn_" + kind,
    )(*args)


def _attn_context_call(lay, kind, q, k, v, o, *, sink=None):
    ncb = lay["ncb"]
    qw, kw, vw = q.shape[1], k.shape[1], v.shape[1]
    blk = lambda i: (i, 0)
    in_specs = [pl.BlockSpec((ROW_BLK, qw), blk), pl.BlockSpec((ROW_BLK, kw), blk),
                pl.BlockSpec((ROW_BLK, vw), blk)]
    args = [q, k, v]
    if kind == "B":
        in_specs.append(pl.BlockSpec(memory_space=pltpu.SMEM))
        args.append(sink)
    in_specs.append(pl.BlockSpec(memory_space=pl.ANY))
    args.append(o)
    return pl.pallas_call(
        functools.partial(_attn_band_kernel, kind=kind, lb=1, n_kv=1),
        out_shape=jax.ShapeDtypeStruct(o.shape, o.dtype),
        grid=(ncb,),
        in_specs=in_specs,
        out_specs=pl.BlockSpec((ROW_BLK, BRANCH_W), blk),
        input_output_aliases={len(args) - 1: 0},
        compiler_params=_cparams(("parallel",)),
        name="attn_ctx_" + kind,
    )(*args)


def _window_bias(lb):
    a = np.arange(ROW_BLK)
    out = np.full((3, 3, ROW_BLK, ROW_BLK), NEG_INF, np.float32)
    for vi, t_rep in enumerate((0, 1, lb - 1)):
        for di, delta in enumerate((-1, 0, 1)):
            kt = t_rep + delta
            if not 0 <= kt < lb:
                continue
            qpos = t_rep * ROW_BLK + a
            kpos = kt * ROW_BLK + a
            ok = np.abs(qpos[:, None] - kpos[None, :]) <= WINDOW
            out[vi, di] = np.where(ok, 0.0, NEG_INF)
    return jnp.asarray(out)


def _neighbourhood_bias(rpb, rows_total):
    lb = rows_total * GRID_W // ROW_BLK
    rpt = ROW_BLK // GRID_W
    kh = min(NA_KH, rows_total)
    a = np.arange(ROW_BLK)
    q_sub, q_col = a // GRID_W, a % GRID_W
    n_dr, n_dc = 2 * NA_KH - 1, 2 * NA_KW - 1
    col = np.arange(GRID_W)
    dc = np.clip(col[None, :] - col[:, None], -(NA_KW - 1), NA_KW - 1) + NA_KW - 1
    hot_c = (dc[:, :, None] == np.arange(n_dc)).astype(np.float32)
    by_col = jnp.einsum("huv,cdv->hucd", rpb.astype(F32) * LOG2E, jnp.asarray(hot_c),
                        precision=HIGHEST)
    sub = np.arange(rpt)
    vals = []
    for delta in (-1, 0, 1):
        dr = np.clip(delta * rpt + sub[None, :] - sub[:, None], -(NA_KH - 1), NA_KH - 1) + NA_KH - 1
        hot_r = (dr[:, :, None] == np.arange(n_dr)).astype(np.float32)
        v = jnp.einsum("abu,hucd->hacbd", jnp.asarray(hot_r), by_col, precision=HIGHEST)
        vals.append(v.reshape(N_HEADS, ROW_BLK, ROW_BLK))
    out = []
    for t_rep in (0, 1, lb - 1):
        per_delta = []
        for di, delta in enumerate((-1, 0, 1)):
            kt = t_rep + delta
            q_row = t_rep * rpt + q_sub
            k_row = kt * rpt + q_sub
            k_col = q_col
            r_start = np.clip(q_row - kh // 2, 0, rows_total - kh)
            row_ok = (k_row[None] >= r_start[:, None]) & (k_row[None] < r_start[:, None] + kh)
            c_start = np.clip(q_col - NA_KW // 2, 0, GRID_W - NA_KW)
            col_ok = (k_col[None] >= c_start[:, None]) & (k_col[None] < c_start[:, None] + NA_KW)
            ok = row_ok & col_ok & (0 <= kt < lb)
            per_delta.append(jnp.where(jnp.asarray(ok)[None], vals[di], NEG_INF).reshape(-1, ROW_BLK))
        out.append(jnp.stack(per_delta))
    return jnp.stack(out)


def _merge_kernel(h_ref, oa_ref, ob_ref, oc_ref, od_ref, x_ref, mod_ref, gffn_ref, wg_ref, bg_ref,
                  wb_ref, wout_ref, wr_ref, br_ref, x1_ref, h2_ref, route_ref, y_ref, *, tm):
    hb = h_ref[...]
    d_model = hb.shape[1]
    for t in range(d_model // MERGE_TN):
        cs = slice(t * MERGE_TN, (t + 1) * MERGE_TN)
        y = None
        for n, o_ref in enumerate((oa_ref, ob_ref, oc_ref, od_ref)):
            gate = _sigmoid(jnp.dot(hb, wg_ref[n, :, cs], preferred_element_type=F32) + bg_ref[n, :, cs])
            u = gate * jnp.dot(o_ref[...], wb_ref[n, :, cs], preferred_element_type=F32)
            y = u if y is None else y + u
        y_ref[:, cs] = y.astype(BF16)
    z = jnp.dot(y_ref[...], wout_ref[...], preferred_element_type=F32)
    x1 = x_ref[...] + mod_ref[0, 2:3, :] * z
    x1_ref[...] = x1
    h2 = _rms(x1) * gffn_ref[...] * (1.0 + mod_ref[0, 4:5, :]) + mod_ref[0, 3:4, :]
    for c in range(SUBLANES):
        h2_ref[pl.ds(c, tm, stride=SUBLANES), :] = h2[:, c * LANES:(c + 1) * LANES]

    h2_hi = h2.astype(BF16)
    h2_lo = (h2 - h2_hi.astype(F32)).astype(BF16)
    logit = (jnp.dot(h2_hi, wr_ref[0], preferred_element_type=F32)
             + jnp.dot(h2_lo, wr_ref[0], preferred_element_type=F32)
             + jnp.dot(h2_hi, wr_ref[1], preferred_element_type=F32)) + br_ref[...]
    lane = _lane_iota(logit.shape)
    big = jnp.int32(1 << 20)
    is_g = (lane >= N_EXPERTS) & (lane < N_EXPERTS + N_GROUPS)
    gl = jnp.where(is_g, logit, NEG_INF)
    gmax = jnp.max(gl, axis=-1, keepdims=True)
    gsel = jnp.min(jnp.where(gl == gmax, lane - N_EXPERTS, big), axis=-1, keepdims=True)
    gw = 1.0 / jnp.sum(jnp.where(is_g, jnp.exp(gl - gmax), 0.0), axis=-1, keepdims=True)
    in_grp = (lane < N_EXPERTS) & ((lane // EXPERTS_PER_GROUP) == gsel)
    el = jnp.where(in_grp, logit, NEG_INF)
    v1 = jnp.max(el, axis=-1, keepdims=True)
    i1 = jnp.min(jnp.where(el == v1, lane, big), axis=-1, keepdims=True)
    el2 = jnp.where(lane == i1, NEG_INF, el)
    v2 = jnp.max(el2, axis=-1, keepdims=True)
    i2 = jnp.min(jnp.where(el2 == v2, lane, big), axis=-1, keepdims=True)
    e21 = jnp.exp(v2 - v1)
    w1 = gw / (1.0 + e21)
    w2 = gw * e21 / (1.0 + e21)
    route_ref[...] = jnp.where(lane == 0, i1.astype(F32),
                               jnp.where(lane == 1, i2.astype(F32),
                                         jnp.where(lane == 2, w1, jnp.where(lane == 3, w2, 0.0))))


def _merge_call(lay, layer, h, o, x, mod, w, *, with_ctx):
    T, D, tm = lay["T"], lay["D"], lay["tm"]
    nct, tpb, B = lay["nct"], lay["tpb"], lay["B"]
    off = 0 if with_ctx else nct
    n_tiles = T // tm - off

    def mod_row(i):
        return jnp.where(i < nct, B, (i - nct) // tpb)

    row = lambda i: (i + off, 0)
    const2 = lambda i: (0, 0)
    const3 = lambda i: (0, 0, 0)
    in_specs = [pl.BlockSpec((tm, D), row)]
    in_specs += [pl.BlockSpec((tm, BRANCH_W), row)] * 4
    in_specs += [pl.BlockSpec((tm, D), row),
                 pl.BlockSpec((1, 6, D), lambda i: (layer * lay["mod_rows"] + mod_row(i + off), 0, 0)),
                 pl.BlockSpec((1, D), const2),
                 pl.BlockSpec(w["w_gate"].shape, const3),
                 pl.BlockSpec(w["b_gate"].shape, const3),
                 pl.BlockSpec(w["w_branch"].shape, const3),
                 pl.BlockSpec((D, D), const2),
                 pl.BlockSpec((2, D, LANES), const3),
                 pl.BlockSpec((1, LANES), const2)]
    out_shape = [jax.ShapeDtypeStruct((T, D), F32),
                 jax.ShapeDtypeStruct((T * SUBLANES, LANES), F32),
                 jax.ShapeDtypeStruct((T, LANES), F32)]
    out_specs = [pl.BlockSpec((tm, D), row),
                 pl.BlockSpec((tm * SUBLANES, LANES), row),
                 pl.BlockSpec((tm, LANES), row)]
    return pl.pallas_call(
        functools.partial(_merge_kernel, tm=tm),
        out_shape=out_shape, grid=(n_tiles,), in_specs=in_specs, out_specs=out_specs,
        scratch_shapes=[pltpu.VMEM((tm, D), BF16)],
        compiler_params=_cparams(("parallel",)),
        name="merge",
    )(h, o["A"], o["B"], o["C"], o["D"], x, mod, w["g_ffn"], w["w_gate"], w["b_gate"],
      w["w_branch"], w["w_out"], w["w_route"], w["b_route"])


def _moe_kernel(tab_ref, tok_ref, h2_ref, sw_ref, w1_ref, w3_ref, w2_ref, f_ref,
                xg_ref, y_ref, st_ref, *, blk, slots, tile_off):
    ti = pl.program_id(0)
    e = pl.program_id(1)

    @pl.when(e == 0)
    def _():
        f_ref[...] = jnp.zeros_like(f_ref)
        xg_ref[...] = jnp.zeros_like(xg_ref)

    base = ((ti + tile_off) * N_EXPERTS + e) * 2
    seg0 = tab_ref[base]
    n_pad = tab_ref[base + 1]
    tok_base = (ti + tile_off) * slots
    n_blocks = (n_pad + blk - 1) // blk

    def block_body(b, carry):
        off = pl.multiple_of(seg0 + b * blk, SUBLANES)
        n_grp = jnp.minimum(n_pad - b * blk, blk) // SUBLANES

        def gather(gi, c):
            for u in range(SUBLANES):
                r = gi * SUBLANES + u
                tok = tok_ref[tok_base + off + r]
                src = pl.multiple_of(tok * SUBLANES, SUBLANES)
                dst = pl.multiple_of(r * SUBLANES, SUBLANES)
                xg_ref[pl.ds(dst, SUBLANES), :] = h2_ref[pl.ds(src, SUBLANES), :]
            return c

        lax.fori_loop(0, n_grp, gather, 0)
        xb = _load_token_major(xg_ref, blk).astype(BF16)
        a = jnp.dot(xb, w1_ref[0], preferred_element_type=F32)
        g = jnp.dot(xb, w3_ref[0], preferred_element_type=F32)
        hid = (a * _sigmoid(a) * g).astype(BF16)
        y = jnp.dot(hid, w2_ref[0], preferred_element_type=F32)
        y_ref[...] = y * sw_ref[0, pl.ds(off, blk), :]

        def scatter(gi, c):
            r0 = pl.multiple_of(gi * SUBLANES, SUBLANES)
            for ch in range(SUBLANES):
                st_ref[pl.ds(ch, SUBLANES, stride=SUBLANES), :] = y_ref[pl.ds(r0, SUBLANES),
                                                                        ch * LANES:(ch + 1) * LANES]
            dsts, vals = [], []
            for u in range(SUBLANES):
                tok = tok_ref[tok_base + off + r0 + u]
                dsts.append(pl.multiple_of(tok * SUBLANES, SUBLANES))
                vals.append(f_ref[pl.ds(dsts[u], SUBLANES), :] + st_ref[u * SUBLANES:(u + 1) * SUBLANES, :])
            for u in reversed(range(SUBLANES)):
                f_ref[pl.ds(dsts[u], SUBLANES), :] = vals[u]
            return c

        lax.fori_loop(0, n_grp, scatter, 0)
        return carry

    lax.fori_loop(0, n_blocks, block_body, 0)


def _moe_call(lay, route, h2, w, *, with_ctx):
    T, D, tt = lay["T"], lay["D"], lay["tt"]
    blk = lay["moe_blk"]
    n_tiles_all = T // tt
    tile_off = 0 if with_ctx else lay["ncb"] * ROW_BLK // tt
    n_tiles = n_tiles_all - tile_off
    n_assign = 2 * tt
    n_fill = N_EXPERTS * SUBLANES
    slots = n_assign + n_fill + blk

    eid = route[:, 0:2].astype(jnp.int32).reshape(n_tiles_all, n_assign)
    wts = route[:, 2:4].reshape(n_tiles_all, n_assign)
    tok = jnp.broadcast_to(jnp.arange(n_assign, dtype=jnp.int32)[None, :] // 2, eid.shape)
    ex = jnp.arange(N_EXPERTS, dtype=jnp.int32)
    counts = jnp.sum(eid[:, :, None] == ex[None, None, :], axis=1, dtype=jnp.int32)
    n_dummy = (-counts) % SUBLANES
    fill_key = jnp.where(jnp.arange(SUBLANES, dtype=jnp.int32)[None, None, :] < n_dummy[:, :, None],
                         ex[None, :, None], N_EXPERTS).reshape(n_tiles_all, n_fill)
    zeros_i = jnp.zeros((n_tiles_all, n_fill), jnp.int32)
    _, slot_tok, slot_w = lax.sort(
        (jnp.concatenate([eid, fill_key], axis=1), jnp.concatenate([tok, zeros_i], axis=1),
         jnp.concatenate([wts, zeros_i.astype(F32)], axis=1)),
        dimension=1, is_stable=True, num_keys=1)
    slot_tok = jnp.pad(slot_tok, ((0, 0), (0, blk)))
    slot_w = jnp.pad(slot_w, ((0, 0), (0, blk)))
    padded = counts + n_dummy
    seg0 = jnp.cumsum(padded, axis=1) - padded
    tab = jnp.stack([seg0, padded], axis=-1).reshape(-1).astype(jnp.int32)

    return pl.pallas_call(
        functools.partial(_moe_kernel, blk=blk, slots=slots, tile_off=tile_off),
        out_shape=jax.ShapeDtypeStruct((T * SUBLANES, LANES), F32),
        grid_spec=pltpu.PrefetchScalarGridSpec(
            num_scalar_prefetch=2,
            grid=(n_tiles, N_EXPERTS),
            in_specs=[
                pl.BlockSpec((tt * SUBLANES, LANES), lambda t, e, *_: (t + tile_off, 0)),
                pl.BlockSpec((1, slots, 1), lambda t, e, *_: (t + tile_off, 0, 0)),
                pl.BlockSpec((1, D, EXPERT_FF), lambda t, e, *_: (e, 0, 0)),
                pl.BlockSpec((1, D, EXPERT_FF), lambda t, e, *_: (e, 0, 0)),
                pl.BlockSpec((1, EXPERT_FF, D), lambda t, e, *_: (e, 0, 0)),
            ],
            out_specs=pl.BlockSpec((tt * SUBLANES, LANES), lambda t, e, *_: (t + tile_off, 0)),
            scratch_shapes=[pltpu.VMEM((blk * SUBLANES, LANES), F32),
                            pltpu.VMEM((blk, D), F32),
                            pltpu.VMEM((SUBLANES * SUBLANES, LANES), F32)],
        ),
        compiler_params=_cparams(("parallel", "arbitrary")),
        name="moe_experts",
    )(tab, slot_tok.reshape(-1), h2, slot_w.reshape(n_tiles_all, slots, 1),
      w["w_ff1"], w["w_ff3"], w["w_ff2"])


def _final_kernel(x_ref, f_ref, mod_ref, g_ref, o_ref, *, tm):
    xf = x_ref[...] + mod_ref[0, 5:6, :] * _load_token_major(f_ref, tm)
    o_ref[...] = _rms(xf) * g_ref[...]


def _final_call(lay, layer, x1, f, mod, g_final):
    T, D, tm = lay["T"], lay["D"], lay["tm"]
    nct, tpb = lay["nct"], lay["tpb"]
    n_lat = T // tm - nct
    return pl.pallas_call(
        functools.partial(_final_kernel, tm=tm),
        out_shape=jax.ShapeDtypeStruct((n_lat * tm, D), F32),
        grid=(n_lat,),
        in_specs=[pl.BlockSpec((tm, D), lambda i: (i + nct, 0)),
                  pl.BlockSpec((tm * SUBLANES, LANES), lambda i: (i + nct, 0)),
                  pl.BlockSpec((1, 6, D), lambda i: (layer * lay["mod_rows"] + i // tpb, 0, 0)),
                  pl.BlockSpec((1, D), lambda i: (0, 0))],
        out_specs=pl.BlockSpec((tm, D), lambda i: (i, 0)),
        compiler_params=_cparams(("parallel",)),
        name="final_norm",
    )(x1, f, mod, g_final)


def _select_cols(wm, segs, scale=None):
    parts = []
    for k, (start, width) in enumerate(segs):
        if start is None:
            parts.append(jnp.zeros((wm.shape[0], width), wm.dtype))
        else:
            blk = wm[:, start:start + width]
            parts.append(blk if scale is None or scale[k] is None else blk * scale[k])
    return jnp.concatenate(parts, axis=1)


def _prep_layer(l, p):
    a_cols = A_Q_RANK + A_KV_RANK + A_ROPE
    b_off = a_cols
    c_off = b_off + 512
    d_off = c_off + 768
    qk_scale = HEAD_DIM ** -0.5 * LOG2E
    gqa_q = lambda off: [(off + hh * HEAD_DIM, HEAD_DIM) for hh in _GQA_PERM]
    segs = ([(0, 256), (256, 128), (None, 64), (384, 32), (None, 32)]
            + gqa_q(b_off) + [(b_off + 256, 128), (b_off + 384, 128)]
            + [(c_off, 256), (c_off + 256, 256), (c_off + 512, 256)]
            + gqa_q(d_off) + [(d_off + 256, 128), (d_off + 384, 128)])
    scale = [None] * len(segs)
    for k in (5, 6, 7, 8, 11):
        scale[k] = qk_scale
    w_in = _select_cols(p["w_in"][l], segs, scale).astype(BF16)
    assert w_in.shape[1] == PROJ_COLS

    hq = A_NOPE + A_ROPE
    segs_q = []
    for hh in range(N_HEADS):
        segs_q += [(hh * hq, hq), (None, LANES - hq)]
    w_q_b = _select_cols(p["w_q_b"][l], segs_q).astype(BF16)
    hk = A_NOPE + HEAD_DIM
    segs_k = []
    for hh in range(N_HEADS):
        segs_k += [(hh * hk, A_NOPE), (None, LANES - A_NOPE)]
    segs_k += [(hh * hk + A_NOPE, HEAD_DIM) for hh in range(N_HEADS)]
    w_kv_b = _select_cols(p["w_kv_b"][l], segs_k).astype(BF16)

    wb = p["w_branch"][l]
    perm_rows = lambda m: jnp.concatenate([m[hh * HEAD_DIM:(hh + 1) * HEAD_DIM] for hh in _GQA_PERM], axis=0)
    w_branch = jnp.stack([wb[0], perm_rows(wb[1]), wb[2], perm_rows(wb[3])]).astype(BF16)

    d = p["w_in"].shape[1]
    w_route = jnp.zeros((d, LANES), F32)
    w_route = w_route.at[:, :N_EXPERTS].set(p["w_router"][l]).at[:, N_EXPERTS:N_EXPERTS + N_GROUPS].set(p["w_group"][l])
    b_route = jnp.zeros((1, LANES), F32)
    b_route = b_route.at[0, :N_EXPERTS].set(p["b_router"][l]).at[0, N_EXPERTS:N_EXPERTS + N_GROUPS].set(p["b_group"][l])
    return {
        "g_mix": p["g_norm_mix"][l][None, :],
        "w_in": w_in,
        "g_q_a": p["g_q_a"][l][None, :],
        "w_q_b": w_q_b,
        "g_kv_a": p["g_kv_a"][l][None, :],
        "w_kv_b": w_kv_b,
        "g_q_d": (jnp.tile(p["g_q_d"][l], 2) * qk_scale)[None, :],
        "g_k_d": jnp.tile(p["g_k_d"][l], 2)[None, :],
        "sink": p["sink_b"][l],
        "rpb": p["rpb_c"][l],
        "w_gate": p["w_gate"][l].astype(BF16),
        "b_gate": p["b_gate"][l][:, None, :],
        "w_branch": w_branch,
        "w_out": p["w_out"][l].astype(BF16),
        "g_ffn": p["g_norm_ffn"][l][None, :],
        "w_route": jnp.stack([w_route.astype(BF16),
                              (w_route - w_route.astype(BF16).astype(F32)).astype(BF16)]),
        "b_route": b_route,
        "w_ff1": p["w_ff1"][l].astype(BF16),
        "w_ff3": p["w_ff3"][l].astype(BF16),
        "w_ff2": p["w_ff2"][l].astype(BF16),
    }


def _rope_tables(S, tm):
    t = jnp.arange(S, dtype=jnp.int32)
    rows = (t // GRID_W).astype(F32)
    cols = (t % GRID_W).astype(F32)

    def cs(rot):
        half = rot // 2
        inv = ROPE_THETA ** (-jnp.arange(0, half, 2, dtype=F32) / half)
        ar_, ac_ = rows[:, None] * inv, cols[:, None] * inv
        cos = jnp.concatenate([jnp.cos(ar_), jnp.cos(ar_), jnp.cos(ac_), jnp.cos(ac_)], axis=-1)
        sin = jnp.concatenate([-jnp.sin(ar_), jnp.sin(ar_), -jnp.sin(ac_), jnp.sin(ac_)], axis=-1)
        return cos, sin

    cos64, sin64 = cs(HEAD_DIM)
    cos32, sin32 = cs(A_ROPE)
    ones = lambda n: jnp.ones((S, n), F32)
    zeros = lambda n: jnp.zeros((S, n), F32)
    tabs = {
        "cos_h": jnp.concatenate([cos64, cos64], axis=-1),
        "sin_h": jnp.concatenate([sin64, sin64], axis=-1),
        "cos_a": jnp.concatenate([ones(A_NOPE), cos32, ones(LANES - A_NOPE - A_ROPE)], axis=-1),
        "sin_a": jnp.concatenate([zeros(A_NOPE), sin32, zeros(LANES - A_NOPE - A_ROPE)], axis=-1),
    }
    ident = {"cos_h": 1.0, "sin_h": 0.0, "cos_a": 1.0, "sin_a": 0.0}
    return {k: jnp.concatenate([v, jnp.full((tm, LANES), ident[k], F32)], axis=0) for k, v in tabs.items()}


def _layout(B, S, n_ctx, D):
    assert n_ctx == ROW_BLK and S % 1024 == 0 and S // GRID_W >= 3 * (ROW_BLK // GRID_W)
    T = B * (n_ctx + S)
    tm = 512 if (B * n_ctx) % 512 == 0 else 256
    tk_dense = 1024 if (B * n_ctx) % 1024 == 0 else (512 if (B * n_ctx) % 512 == 0 else 256)
    tt = 2048 if (B * n_ctx) % 2048 == 0 else B * n_ctx
    assert S % tt == 0
    return {
        "B": B, "S": S, "D": D, "T": T, "tm": tm,
        "ncb": B * n_ctx // ROW_BLK,
        "lb": S // ROW_BLK,
        "nct": B * n_ctx // tm,
        "tpb": S // tm,
        "tk_dense": tk_dense,
        "tq_dense": tm,
        "tt": tt, "moe_blk": 256,
        "mod_rows": 16,
    }


def kernel(x, c, ctx, c_ctx, w_mod, b_mod, g_norm_mix, w_in, g_q_a, w_q_b, g_kv_a, w_kv_b, sink_b, rpb_c,
           g_q_d, g_k_d, w_gate, b_gate, w_branch, w_out, g_norm_ffn, w_group, b_group, w_router, b_router,
           w_ff1, w_ff3, w_ff2, g_final):
    B, S, D = x.shape
    n_ctx = ctx.shape[1]
    depth = w_mod.shape[0]
    lay = _layout(B, S, n_ctx, D)
    params = dict(w_in=w_in, g_norm_mix=g_norm_mix, g_q_a=g_q_a, w_q_b=w_q_b, g_kv_a=g_kv_a, w_kv_b=w_kv_b,
                  sink_b=sink_b, rpb_c=rpb_c, g_q_d=g_q_d, g_k_d=g_k_d, w_gate=w_gate, b_gate=b_gate,
                  w_branch=w_branch, w_out=w_out, g_norm_ffn=g_norm_ffn, w_group=w_group, b_group=b_group,
                  w_router=w_router, b_router=b_router, w_ff1=w_ff1, w_ff3=w_ff3, w_ff2=w_ff2)

    c_all = jnp.zeros((lay["mod_rows"], D), F32).at[:B].set(c).at[B].set(c_ctx)
    mod = _modulation(c_all, w_mod, b_mod).reshape(depth * lay["mod_rows"], 6, D)
    tabs = _rope_tables(S, lay["tm"])
    win_bias = _window_bias(lay["lb"])
    xf = jnp.concatenate([ctx.reshape(B * n_ctx, D), x.reshape(B * S, D)], axis=0)

    f = None
    for l in range(depth):
        with_ctx = l < depth - 1
        w = _prep_layer(l, params)
        xf, pr = _proj_call(lay, xf, f, mod, mod, l, w, tabs)
        o = {}
        for kind, kl in (("A", "a"), ("B", "b"), ("C", "c"), ("D", "d")):
            q, k, v = pr["q" + kl], pr["k" + kl], pr["v" + kl]
            sink = w["sink"] if kind == "B" else None
            bias = {"B": win_bias, "C": _neighbourhood_bias(w["rpb"], S // GRID_W)}.get(kind)
            o[kind] = _attn_latent_call(lay, kind, q, k, v, sink=sink, bias=bias)
            if with_ctx:
                o[kind] = _attn_context_call(lay, kind, q, k, v, o[kind], sink=sink)
        xf, h2, route = _merge_call(lay, l, pr["h"], o, xf, mod, w, with_ctx=with_ctx)
        f = _moe_call(lay, route, h2, w, with_ctx=with_ctx)
    out = _final_call(lay, depth - 1, xf, f, mod, g_final[None, :])
    return out.reshape(B, S, D)
```

```python
import functools

import numpy as np
import jax
import jax.numpy as jnp
from jax import lax
from jax.experimental import pallas as pl
from jax.experimental.pallas import tpu as pltpu

F32 = jnp.float32
BF16 = jnp.bfloat16
HIGHEST = lax.Precision.HIGHEST

GRID_W = 64
ROPE_THETA = 10000.0
EPS = 1e-6
NEG_INF = -1e30
LOG2E = 1.4426950408889634
HEAD_DIM = 64
N_HEADS = 4
BRANCH_W = 256
A_Q_RANK = 256
A_KV_RANK = 128
A_NOPE = 64
A_ROPE = 32
NA_KH = 8
NA_KW = 16
WINDOW = 128
N_GROUPS = 4
EXPERTS_PER_GROUP = 8
N_EXPERTS = 32
EXPERT_FF = 256

LANES = 128
SUBLANES = 8
ROW_BLK = 256
MERGE_TN = 256
VMEM_LIMIT = 56 * 1024 * 1024

_PROJ_GROUPS = (("cq", 256), ("ckv", 128), ("kr", 128), ("qb", 256), ("kb", 128), ("vb", 128),
                ("qc", 256), ("kc", 256), ("vc", 256), ("qd", 256), ("kd", 128), ("vd", 128))
_PROJ_OFF = {}
_o = 0
for _n, _w in _PROJ_GROUPS:
    _PROJ_OFF[_n] = (_o, _w)
    _o += _w
PROJ_COLS = _o
_GQA_PERM = (0, 2, 1, 3)


def _cparams(sem):
    return pltpu.CompilerParams(dimension_semantics=sem, vmem_limit_bytes=VMEM_LIMIT)


def _lane_iota(shape):
    return lax.broadcasted_iota(jnp.int32, shape, len(shape) - 1)


def _sigmoid(x):
    return 1.0 / (1.0 + jnp.exp(-x))


def _mod_kernel(c_ref, w_ref, b_ref, o_ref):
    cf = c_ref[...]
    s = cf * _sigmoid(cf)
    o_ref[0] = jnp.dot(s, w_ref[0], precision=HIGHEST, preferred_element_type=F32) + b_ref[0]


def _modulation(c_all, w_mod, b_mod):
    n_layers, d, n_out = w_mod.shape
    rows = c_all.shape[0]
    tn = 1536
    return pl.pallas_call(
        _mod_kernel,
        out_shape=jax.ShapeDtypeStruct((n_layers, rows, n_out), F32),
        grid=(n_layers, n_out // tn),
        in_specs=[pl.BlockSpec((rows, d), lambda l, j: (0, 0)),
                  pl.BlockSpec((1, d, tn), lambda l, j: (l, 0, j)),
                  pl.BlockSpec((1, 1, tn), lambda l, j: (l, 0, j))],
        out_specs=pl.BlockSpec((1, rows, tn), lambda l, j: (l, 0, j)),
        compiler_params=_cparams(("arbitrary", "arbitrary")),
        name="modulation",
    )(c_all, w_mod, b_mod.reshape(n_layers, 1, n_out))


def _rms(x):
    return x * lax.rsqrt(jnp.mean(x * x, axis=-1, keepdims=True) + EPS)


def _swap_blocks(x, blk):
    lane = _lane_iota(x.shape)
    up = pltpu.roll(x, LANES - blk, 1)
    dn = pltpu.roll(x, blk, 1)
    return jnp.where((lane // blk) % 2 == 0, up, dn)


def _rope(x, cos, sin, blk):
    return x * cos + _swap_blocks(x, blk) * sin


def _pair_norm(x, g):
    lo = _lane_iota(x.shape) < HEAD_DIM
    sq = x * x
    s_lo = jnp.sum(jnp.where(lo, sq, 0.0), axis=-1, keepdims=True)
    s_hi = jnp.sum(jnp.where(lo, 0.0, sq), axis=-1, keepdims=True)
    ms = jnp.where(lo, s_lo, s_hi) * (1.0 / HEAD_DIM)
    return x * lax.rsqrt(ms + EPS) * g


def _load_token_major(ref, rows):
    return jnp.concatenate(
        [ref[pl.ds(c, rows, stride=SUBLANES), :] for c in range(SUBLANES)], axis=-1)


def _proj_kernel(*refs, with_f, tm, scale_a):
    it = iter(refs)
    x_ref = next(it)
    if with_f:
        f_ref = next(it)
        modp_ref = next(it)
    mod_ref = next(it)
    gmix_ref, win_ref, gqa_ref, wqb_ref, gkva_ref, wkvb_ref, gqd_ref, gkd_ref = (next(it) for _ in range(8))
    cosh_ref, sinh_ref, cosa_ref, sina_ref = (next(it) for _ in range(4))
    if with_f:
        x2_ref = next(it)
    h_ref = next(it)
    qa_ref, ka_ref, va_ref, qb_ref, kb_ref, vb_ref, qc_ref, kc_ref, vc_ref, qd_ref, kd_ref, vd_ref = (
        next(it) for _ in range(12))

    xf = x_ref[...]
    if with_f:
        xf = xf + modp_ref[0, 5:6, :] * _load_token_major(f_ref, tm)
        x2_ref[...] = xf
    h = _rms(xf) * gmix_ref[...] * (1.0 + mod_ref[0, 1:2, :]) + mod_ref[0, 0:1, :]
    hb = h.astype(BF16)
    h_ref[...] = hb
    p = jnp.dot(hb, win_ref[...], preferred_element_type=F32)

    def grp(name):
        o, w = _PROJ_OFF[name]
        return p[:, o:o + w]

    cosh, sinh = cosh_ref[...], sinh_ref[...]
    cosa, sina = cosa_ref[...], sina_ref[...]

    cq = (_rms(grp("cq")) * gqa_ref[...]).astype(BF16)
    qa = jnp.dot(cq, wqb_ref[...], preferred_element_type=F32)
    for hd in range(N_HEADS):
        sl = slice(hd * LANES, (hd + 1) * LANES)
        qa_ref[:, sl] = (_rope(qa[:, sl], cosa, sina, 8) * scale_a).astype(BF16)
    ckv = (_rms(grp("ckv")) * gkva_ref[...]).astype(BF16)
    kva = jnp.dot(ckv, wkvb_ref[...], preferred_element_type=F32)
    kr = _rope(grp("kr"), cosa, sina, 8)
    for hd in range(N_HEADS):
        sl = slice(hd * LANES, (hd + 1) * LANES)
        ka_ref[:, sl] = (kva[:, sl] + kr).astype(BF16)
    va_ref[...] = kva[:, N_HEADS * LANES:].astype(BF16)

    qb = grp("qb")
    for j in range(2):
        sl = slice(j * LANES, (j + 1) * LANES)
        qb_ref[:, sl] = _rope(qb[:, sl], cosh, sinh, 16).astype(BF16)
    kb_ref[...] = _rope(grp("kb"), cosh, sinh, 16).astype(BF16)
    vb_ref[...] = grp("vb").astype(BF16)

    qc_ref[...] = grp("qc").astype(BF16)
    kc_ref[...] = grp("kc").astype(BF16)
    vc_ref[...] = grp("vc").astype(BF16)

    qd = grp("qd")
    for j in range(2):
        sl = slice(j * LANES, (j + 1) * LANES)
        qd_ref[:, sl] = _rope(_pair_norm(qd[:, sl], gqd_ref[...]), cosh, sinh, 16).astype(BF16)
    kd_ref[...] = _rope(_pair_norm(grp("kd"), gkd_ref[...]), cosh, sinh, 16).astype(BF16)
    vd_ref[...] = grp("vd").astype(BF16)


def _proj_call(lay, x, f, modp, mod, layer, w, tabs):
    T, D, tm = lay["T"], lay["D"], lay["tm"]
    nct, tpb, B = lay["nct"], lay["tpb"], lay["B"]
    with_f = f is not None
    n_tiles = T // tm

    def mod_row(i):
        return jnp.where(i < nct, B, (i - nct) // tpb)

    def tab_blk(i):
        return jnp.where(i < nct, tpb, (i - nct) % tpb)

    row = lambda i: (i, 0)
    const = lambda i: (0, 0)
    in_specs = [pl.BlockSpec((tm, D), row)]
    args = [x]
    if with_f:
        in_specs += [pl.BlockSpec((tm * SUBLANES, LANES), row),
                     pl.BlockSpec((1, 6, D), lambda i: ((layer - 1) * lay["mod_rows"] + mod_row(i), 0, 0))]
        args += [f, modp]
    in_specs += [pl.BlockSpec((1, 6, D), lambda i: (layer * lay["mod_rows"] + mod_row(i), 0, 0))]
    args += [mod]
    for name in ("g_mix", "w_in", "g_q_a", "w_q_b", "g_kv_a", "w_kv_b", "g_q_d", "g_k_d"):
        a = w[name]
        in_specs.append(pl.BlockSpec(a.shape, const))
        args.append(a)
    for tname in ("cos_h", "sin_h", "cos_a", "sin_a"):
        in_specs.append(pl.BlockSpec((tm, LANES), lambda i: (tab_blk(i), 0)))
        args.append(tabs[tname])

    widths = [("h", D), ("qa", 512), ("ka", 512), ("va", 256), ("qb", 256), ("kb", 128), ("vb", 128),
              ("qc", 256), ("kc", 256), ("vc", 256), ("qd", 256), ("kd", 128), ("vd", 128)]
    out_shape, out_specs = [], []
    if with_f:
        out_shape.append(jax.ShapeDtypeStruct((T, D), F32))
        out_specs.append(pl.BlockSpec((tm, D), row))
    for _, wd in widths:
        out_shape.append(jax.ShapeDtypeStruct((T, wd), BF16))
        out_specs.append(pl.BlockSpec((tm, wd), row))

    outs = pl.pallas_call(
        functools.partial(_proj_kernel, with_f=with_f, tm=tm,
                          scale_a=float((A_NOPE + A_ROPE) ** -0.5 * LOG2E)),
        out_shape=out_shape, grid=(n_tiles,), in_specs=in_specs, out_specs=out_specs,
        compiler_params=_cparams(("parallel",)),
        name="proj_in",
    )(*args)
    outs = list(outs)
    x2 = outs.pop(0) if with_f else x
    names = [n for n, _ in widths]
    return x2, dict(zip(names, outs))


_NT = (((1,), (1,)), ((), ()))


def _head_plan(kind):
    if kind == "A":
        return tuple((r, None, r, r // 2) for r in range(N_HEADS))
    if kind == "C":
        return tuple((r // 2, r % 2, r // 2, r // 2) for r in range(N_HEADS))
    return tuple((r // 2, r % 2, 0, 0) for r in range(N_HEADS))


def _head_query(q_ref, plan_r):
    qt, half, _, _ = plan_r
    src = q_ref[:, qt * LANES:(qt + 1) * LANES]
    if half is None:
        return src
    lane = _lane_iota(src.shape)
    keep = (lane < HEAD_DIM) if half == 0 else (lane >= HEAD_DIM)
    return jnp.where(keep, src, jnp.zeros_like(src))


def _score_chunks(q, k_blocks, bias_blocks):
    chunks = []
    for kb, bb in zip(k_blocks, bias_blocks):
        s = lax.dot_general(q, kb, _NT, preferred_element_type=F32)
        if bb is not None:
            s = s + bb
        chunks += [s[:, c * LANES:(c + 1) * LANES] for c in range(s.shape[1] // LANES)]
    return chunks


def _row_max(chunks):
    m = functools.reduce(jnp.maximum, chunks)
    return jnp.broadcast_to(jnp.max(m, axis=-1, keepdims=True), m.shape)


def _weighted_values(p_chunks, v_blocks, half):
    pv, idx = None, 0
    for vb in v_blocks:
        n = vb.shape[0] // LANES
        p = jnp.concatenate(p_chunks[idx:idx + n], axis=1).astype(BF16)
        idx += n
        lane = _lane_iota(vb.shape)
        own = (lane < HEAD_DIM) if half == 0 else (lane >= HEAD_DIM)
        d = jnp.dot(p, jnp.where(own, vb, jnp.ones_like(vb)), preferred_element_type=F32)
        pv = d if pv is None else pv + d
    return pv


def _softmax_once(q, k_blocks, v_blocks, bias_blocks, sink, half):
    chunks = _score_chunks(q, k_blocks, bias_blocks)
    m = _row_max(chunks)
    if sink is not None:
        m = jnp.maximum(m, sink)
    p = [jnp.exp2(c - m) for c in chunks]
    pv = _weighted_values(p, v_blocks, half)
    l = pltpu.roll(pv, HEAD_DIM, 1)
    if sink is not None:
        l = l + jnp.exp2(sink - m)
    return pv / l


def _store_heads(o_ref, outs):
    lane = _lane_iota(outs[0].shape)
    for g in range(2):
        o_ref[:, g * LANES:(g + 1) * LANES] = jnp.where(
            lane < HEAD_DIM, outs[2 * g], outs[2 * g + 1]).astype(o_ref.dtype)


def _attn_band_kernel(*refs, kind, lb, n_kv, tq):
    it = iter(refs)
    q_ref = next(it)
    k_refs = [next(it) for _ in range(n_kv)]
    v_refs = [next(it) for _ in range(n_kv)]
    bias_ref = next(it) if n_kv > 1 else None
    sink_ref = next(it) if kind == "B" else None
    if n_kv == 1:
        next(it)
    o_ref = next(it)
    t = pl.program_id(0) % lb
    var = jnp.where(t == 0, 0, jnp.where(t == lb - 1, 2, 1))
    outs = []
    for r, plan_r in enumerate(_head_plan(kind)):
        _, _, kt, vt = plan_r
        ksl = slice(kt * LANES, (kt + 1) * LANES)
        vsl = slice(vt * LANES, (vt + 1) * LANES)
        bias = [None]
        for j in range(n_kv - 1):
            if kind == "C":
                bias.append(bias_ref[var, j, r * tq:(r + 1) * tq, :])
            else:
                bias.append(bias_ref[var, j])
        sink = sink_ref[_GQA_PERM[r]] * LOG2E if kind == "B" else None
        outs.append(_softmax_once(_head_query(q_ref, plan_r), [k[:, ksl] for k in k_refs],
                                  [v[:, vsl] for v in v_refs], bias, sink, r % 2))
    _store_heads(o_ref, outs)


def _attn_dense_kernel(q_ref, kc_ref, vc_ref, kl_ref, vl_ref, o_ref, qs_ref, m_ref, acc_ref,
                       *, kind, n_steps, tq):
    s = pl.program_id(1)
    plan = _head_plan(kind)

    def update(first):
        for r, (_, _, kt, vt) in enumerate(plan):
            rows = slice(r * tq, (r + 1) * tq)
            ksl = slice(kt * LANES, (kt + 1) * LANES)
            vsl = slice(vt * LANES, (vt + 1) * LANES)
            if first:
                q = _head_query(q_ref, plan[r])
                qs_ref[rows, :] = q
                k_blocks = [kc_ref[:, ksl], kl_ref[:, ksl]]
                v_blocks = [vc_ref[:, vsl], vl_ref[:, vsl]]
            else:
                q = qs_ref[rows, :]
                k_blocks = [kl_ref[:, ksl]]
                v_blocks = [vl_ref[:, vsl]]
            chunks = _score_chunks(q, k_blocks, [None] * len(k_blocks))
            m_cur = _row_max(chunks)
            if first:
                m_new = m_cur
            else:
                m_prev = m_ref[rows, :]
                m_new = jnp.maximum(m_prev, m_cur)
                alpha = jnp.exp2(m_prev - m_new)
            p = [jnp.exp2(c - m_new) for c in chunks]
            pv = _weighted_values(p, v_blocks, r % 2)
            if first:
                acc_ref[rows, :] = pv
            else:
                acc_ref[rows, :] = alpha * acc_ref[rows, :] + pv
            m_ref[rows, :] = m_new

    @pl.when(s == 0)
    def _():
        update(True)

    @pl.when(s > 0)
    def _():
        update(False)

    @pl.when(s == n_steps - 1)
    def _():
        outs = []
        for r in range(N_HEADS):
            rows = slice(r * tq, (r + 1) * tq)
            acc = acc_ref[rows, :]
            outs.append(acc / pltpu.roll(acc, HEAD_DIM, 1))
        _store_heads(o_ref, outs)


def _attn_latent_call(lay, kind, q, k, v, *, sink=None, bias=None):
    T, B, ncb, lb, S = lay["T"], lay["B"], lay["ncb"], lay["lb"], lay["S"]
    tq = ROW_BLK
    qw, kw, vw = q.shape[1], k.shape[1], v.shape[1]
    out_shape = jax.ShapeDtypeStruct((T, BRANCH_W), BF16)
    if kind in ("A", "D"):
        tk, tq = lay["tk_dense"], lay["tq_dense"]
        n_steps = S // tk
        lat0 = ncb * ROW_BLK // tk
        q0 = ncb * ROW_BLK // tq
        qpb = S // tq
        lat_blk = lambda i, s: (lat0 + (i // qpb) * n_steps + s, 0)
        return pl.pallas_call(
            functools.partial(_attn_dense_kernel, kind=kind, n_steps=n_steps, tq=tq),
            out_shape=out_shape,
            grid=(B * qpb, n_steps),
            in_specs=[pl.BlockSpec((tq, qw), lambda i, s: (q0 + i, 0)),
                      pl.BlockSpec((ROW_BLK, kw), lambda i, s: (i // qpb, 0)),
                      pl.BlockSpec((ROW_BLK, vw), lambda i, s: (i // qpb, 0)),
                      pl.BlockSpec((tk, kw), lat_blk),
                      pl.BlockSpec((tk, vw), lat_blk)],
            out_specs=pl.BlockSpec((tq, BRANCH_W), lambda i, s: (q0 + i, 0)),
            scratch_shapes=[pltpu.VMEM((N_HEADS * tq, LANES), BF16),
                            pltpu.VMEM((N_HEADS * tq, LANES), F32),
                            pltpu.VMEM((N_HEADS * tq, LANES), F32)],
            compiler_params=_cparams(("parallel", "arbitrary")),
            name="attn_" + kind,
        )(q, k, v, k, v)

    tq = lay["tq_band"]
    qpb = S // tq
    q0 = ncb * ROW_BLK // tq
    bpq = tq // ROW_BLK

    def nb(i, d):
        return (ncb + (i // qpb) * lb + jnp.clip((i % qpb) * bpq + d, 0, lb - 1), 0)

    kv_maps = [lambda i: (i // qpb, 0)] + [functools.partial(nb, d=d) for d in _band_offsets(tq)]
    n_kv = len(kv_maps)
    in_specs = [pl.BlockSpec((tq, qw), lambda i: (q0 + i, 0))]
    in_specs += [pl.BlockSpec((ROW_BLK, kw), m) for m in kv_maps]
    in_specs += [pl.BlockSpec((ROW_BLK, vw), m) for m in kv_maps]
    in_specs.append(pl.BlockSpec(bias.shape, lambda i: (0,) * bias.ndim, pipeline_mode=pl.Buffered(1)))
    args = [q] + [k] * n_kv + [v] * n_kv + [bias]
    if kind == "B":
        in_specs.append(pl.BlockSpec(memory_space=pltpu.SMEM))
        args.append(sink)
    return pl.pallas_call(
        functools.partial(_attn_band_kernel, kind=kind, lb=qpb, n_kv=n_kv, tq=tq),
        out_shape=out_shape,
        grid=(B * qpb,),
        in_specs=in_specs,
        out_specs=pl.BlockSpec((tq, BRANCH_W), lambda i: (q0 + i, 0)),
        compiler_params=_cparams(("parallel",)),
        name="attn_" + kind,
    )(*args)


def _attn_context_call(lay, kind, q, k, v, o, *, sink=None):
    ncb = lay["ncb"]
    qw, kw, vw = q.shape[1], k.shape[1], v.shape[1]
    blk = lambda i: (i, 0)
    in_specs = [pl.BlockSpec((ROW_BLK, qw), blk), pl.BlockSpec((ROW_BLK, kw), blk),
                pl.BlockSpec((ROW_BLK, vw), blk)]
    args = [q, k, v]
    if kind == "B":
        in_specs.append(pl.BlockSpec(memory_space=pltpu.SMEM))
        args.append(sink)
    in_specs.append(pl.BlockSpec(memory_space=pl.ANY))
    args.append(o)
    return pl.pallas_call(
        functools.partial(_attn_band_kernel, kind=kind, lb=1, n_kv=1, tq=ROW_BLK),
        out_shape=jax.ShapeDtypeStruct(o.shape, o.dtype),
        grid=(ncb,),
        in_specs=in_specs,
        out_specs=pl.BlockSpec((ROW_BLK, BRANCH_W), blk),
        input_output_aliases={len(args) - 1: 0},
        compiler_params=_cparams(("parallel",)),
        name="attn_ctx_" + kind,
    )(*args)


def _band_offsets(tq):
    return tuple(range(-1, tq // ROW_BLK + 1))


def _window_bias(S, tq):
    lb, qpb, bpq = S // ROW_BLK, S // tq, tq // ROW_BLK
    offs = _band_offsets(tq)
    qa, ka = np.arange(tq), np.arange(ROW_BLK)
    out = np.full((3, len(offs), tq, ROW_BLK), NEG_INF, np.float32)
    for vi, t_rep in enumerate((0, 1, qpb - 1)):
        for di, d in enumerate(offs):
            kt = t_rep * bpq + d
            if not 0 <= kt < lb:
                continue
            qpos = t_rep * tq + qa
            kpos = kt * ROW_BLK + ka
            ok = np.abs(qpos[:, None] - kpos[None, :]) <= WINDOW
            out[vi, di] = np.where(ok, 0.0, NEG_INF)
    return jnp.asarray(out)


def _neighbourhood_bias(rpb, rows_total, tq):
    lb = rows_total * GRID_W // ROW_BLK
    qpb, bpq = rows_total * GRID_W // tq, tq // ROW_BLK
    rpt = ROW_BLK // GRID_W
    rpq = tq // GRID_W
    kh = min(NA_KH, rows_total)
    qa, ka = np.arange(tq), np.arange(ROW_BLK)
    q_sub, q_col = qa // GRID_W, qa % GRID_W
    k_sub, k_col = ka // GRID_W, ka % GRID_W
    n_dr, n_dc = 2 * NA_KH - 1, 2 * NA_KW - 1
    col = np.arange(GRID_W)
    dc = np.clip(col[None, :] - col[:, None], -(NA_KW - 1), NA_KW - 1) + NA_KW - 1
    hot_c = (dc[:, :, None] == np.arange(n_dc)).astype(np.float32)
    by_col = jnp.einsum("huv,cdv->hucd", rpb.astype(F32) * LOG2E, jnp.asarray(hot_c),
                        precision=HIGHEST)
    offs = _band_offsets(tq)
    vals = []
    for d in offs:
        dr = np.clip(d * rpt + np.arange(rpt)[None, :] - np.arange(rpq)[:, None],
                     -(NA_KH - 1), NA_KH - 1) + NA_KH - 1
        hot_r = (dr[:, :, None] == np.arange(n_dr)).astype(np.float32)
        v = jnp.einsum("abu,hucd->hacbd", jnp.asarray(hot_r), by_col, precision=HIGHEST)
        vals.append(v.reshape(N_HEADS, tq, ROW_BLK))
    out = []
    for t_rep in (0, 1, qpb - 1):
        per_block = []
        for di, d in enumerate(offs):
            kt = t_rep * bpq + d
            q_row = t_rep * rpq + q_sub
            k_row = kt * rpt + k_sub
            r_start = np.clip(q_row - kh // 2, 0, rows_total - kh)
            row_ok = (k_row[None] >= r_start[:, None]) & (k_row[None] < r_start[:, None] + kh)
            c_start = np.clip(q_col - NA_KW // 2, 0, GRID_W - NA_KW)
            col_ok = (k_col[None] >= c_start[:, None]) & (k_col[None] < c_start[:, None] + NA_KW)
            ok = row_ok & col_ok & (0 <= kt < lb)
            per_block.append(jnp.where(jnp.asarray(ok)[None], vals[di], NEG_INF).reshape(-1, ROW_BLK))
        out.append(jnp.stack(per_block))
    return jnp.stack(out)


def _merge_kernel(h_ref, oa_ref, ob_ref, oc_ref, od_ref, x_ref, mod_ref, gffn_ref, wg_ref, bg_ref,
                  wb_ref, wout_ref, wr_ref, br_ref, x1_ref, h2_ref, route_ref, y_ref, *, tm):
    hb = h_ref[...]
    d_model = hb.shape[1]
    for t in range(d_model // MERGE_TN):
        cs = slice(t * MERGE_TN, (t + 1) * MERGE_TN)
        y = None
        for n, o_ref in enumerate((oa_ref, ob_ref, oc_ref, od_ref)):
            gate = _sigmoid(jnp.dot(hb, wg_ref[n, :, cs], preferred_element_type=F32) + bg_ref[n, :, cs])
            u = gate * jnp.dot(o_ref[...], wb_ref[n, :, cs], preferred_element_type=F32)
            y = u if y is None else y + u
        y_ref[:, cs] = y.astype(BF16)
    z = jnp.dot(y_ref[...], wout_ref[...], preferred_element_type=F32)
    x1 = x_ref[...] + mod_ref[0, 2:3, :] * z
    x1_ref[...] = x1
    h2 = _rms(x1) * gffn_ref[...] * (1.0 + mod_ref[0, 4:5, :]) + mod_ref[0, 3:4, :]
    for c in range(SUBLANES):
        h2_ref[pl.ds(c, tm, stride=SUBLANES), :] = h2[:, c * LANES:(c + 1) * LANES]

    h2_hi = h2.astype(BF16)
    h2_lo = (h2 - h2_hi.astype(F32)).astype(BF16)
    logit = (jnp.dot(h2_hi, wr_ref[0], preferred_element_type=F32)
             + jnp.dot(h2_lo, wr_ref[0], preferred_element_type=F32)
             + jnp.dot(h2_hi, wr_ref[1], preferred_element_type=F32)) + br_ref[...]
    lane = _lane_iota(logit.shape)
    big = jnp.int32(1 << 20)
    is_g = (lane >= N_EXPERTS) & (lane < N_EXPERTS + N_GROUPS)
    gl = jnp.where(is_g, logit, NEG_INF)
    gmax = jnp.max(gl, axis=-1, keepdims=True)
    gsel = jnp.min(jnp.where(gl == gmax, lane - N_EXPERTS, big), axis=-1, keepdims=True)
    gw = 1.0 / jnp.sum(jnp.where(is_g, jnp.exp(gl - gmax), 0.0), axis=-1, keepdims=True)
    in_grp = (lane < N_EXPERTS) & ((lane // EXPERTS_PER_GROUP) == gsel)
    el = jnp.where(in_grp, logit, NEG_INF)
    v1 = jnp.max(el, axis=-1, keepdims=True)
    i1 = jnp.min(jnp.where(el == v1, lane, big), axis=-1, keepdims=True)
    el2 = jnp.where(lane == i1, NEG_INF, el)
    v2 = jnp.max(el2, axis=-1, keepdims=True)
    i2 = jnp.min(jnp.where(el2 == v2, lane, big), axis=-1, keepdims=True)
    e21 = jnp.exp(v2 - v1)
    w1 = gw / (1.0 + e21)
    w2 = gw * e21 / (1.0 + e21)
    route_ref[...] = jnp.where(lane == 0, i1.astype(F32),
                               jnp.where(lane == 1, i2.astype(F32),
                                         jnp.where(lane == 2, w1, jnp.where(lane == 3, w2, 0.0))))


def _merge_call(lay, layer, h, o, x, mod, w, *, with_ctx):
    T, D, tm = lay["T"], lay["D"], lay["tm"]
    nct, tpb, B = lay["nct"], lay["tpb"], lay["B"]
    off = 0 if with_ctx else nct
    n_tiles = T // tm - off

    def mod_row(i):
        return jnp.where(i < nct, B, (i - nct) // tpb)

    row = lambda i: (i + off, 0)
    const2 = lambda i: (0, 0)
    const3 = lambda i: (0, 0, 0)
    in_specs = [pl.BlockSpec((tm, D), row)]
    in_specs += [pl.BlockSpec((tm, BRANCH_W), row)] * 4
    in_specs += [pl.BlockSpec((tm, D), row),
                 pl.BlockSpec((1, 6, D), lambda i: (layer * lay["mod_rows"] + mod_row(i + off), 0, 0)),
                 pl.BlockSpec((1, D), const2),
                 pl.BlockSpec(w["w_gate"].shape, const3),
                 pl.BlockSpec(w["b_gate"].shape, const3),
                 pl.BlockSpec(w["w_branch"].shape, const3),
                 pl.BlockSpec((D, D), const2),
                 pl.BlockSpec((2, D, LANES), const3),
                 pl.BlockSpec((1, LANES), const2)]
    out_shape = [jax.ShapeDtypeStruct((T, D), F32),
                 jax.ShapeDtypeStruct((T * SUBLANES, LANES), F32),
                 jax.ShapeDtypeStruct((T, LANES), F32)]
    out_specs = [pl.BlockSpec((tm, D), row),
                 pl.BlockSpec((tm * SUBLANES, LANES), row),
                 pl.BlockSpec((tm, LANES), row)]
    return pl.pallas_call(
        functools.partial(_merge_kernel, tm=tm),
        out_shape=out_shape, grid=(n_tiles,), in_specs=in_specs, out_specs=out_specs,
        scratch_shapes=[pltpu.VMEM((tm, D), BF16)],
        compiler_params=_cparams(("parallel",)),
        name="merge",
    )(h, o["A"], o["B"], o["C"], o["D"], x, mod, w["g_ffn"], w["w_gate"], w["b_gate"],
      w["w_branch"], w["w_out"], w["w_route"], w["b_route"])


def _moe_kernel(tab_ref, tok_ref, h2_ref, sw_ref, w1_ref, w3_ref, w2_ref, f_ref,
                xg_ref, y_ref, st_ref, *, blk, slots, tile_off):
    ti = pl.program_id(0)
    e = pl.program_id(1)

    @pl.when(e == 0)
    def _():
        f_ref[...] = jnp.zeros_like(f_ref)
        xg_ref[...] = jnp.zeros_like(xg_ref)

    base = ((ti + tile_off) * N_EXPERTS + e) * 2
    seg0 = tab_ref[base]
    n_pad = tab_ref[base + 1]
    tok_base = (ti + tile_off) * slots
    n_blocks = (n_pad + blk - 1) // blk

    def block_body(b, carry):
        off = pl.multiple_of(seg0 + b * blk, SUBLANES)
        n_grp = jnp.minimum(n_pad - b * blk, blk) // SUBLANES

        def gather(gi, c):
            for u in range(SUBLANES):
                r = gi * SUBLANES + u
                tok = tok_ref[tok_base + off + r]
                src = pl.multiple_of(tok * SUBLANES, SUBLANES)
                dst = pl.multiple_of(r * SUBLANES, SUBLANES)
                xg_ref[pl.ds(dst, SUBLANES), :] = h2_ref[pl.ds(src, SUBLANES), :]
            return c

        lax.fori_loop(0, n_grp, gather, 0)
        xb = _load_token_major(xg_ref, blk).astype(BF16)
        a = jnp.dot(xb, w1_ref[0], preferred_element_type=F32)
        g = jnp.dot(xb, w3_ref[0], preferred_element_type=F32)
        hid = (a * _sigmoid(a) * g).astype(BF16)
        y = jnp.dot(hid, w2_ref[0], preferred_element_type=F32)
        y_ref[...] = y * sw_ref[0, pl.ds(off, blk), :]

        def scatter(gi, c):
            r0 = pl.multiple_of(gi * SUBLANES, SUBLANES)
            for ch in range(SUBLANES):
                st_ref[pl.ds(ch, SUBLANES, stride=SUBLANES), :] = y_ref[pl.ds(r0, SUBLANES),
                                                                        ch * LANES:(ch + 1) * LANES]
            dsts, vals = [], []
            for u in range(SUBLANES):
                tok = tok_ref[tok_base + off + r0 + u]
                dsts.append(pl.multiple_of(tok * SUBLANES, SUBLANES))
                vals.append(f_ref[pl.ds(dsts[u], SUBLANES), :] + st_ref[u * SUBLANES:(u + 1) * SUBLANES, :])
            for u in reversed(range(SUBLANES)):
                f_ref[pl.ds(dsts[u], SUBLANES), :] = vals[u]
            return c

        lax.fori_loop(0, n_grp, scatter, 0)
        return carry

    lax.fori_loop(0, n_blocks, block_body, 0)


def _moe_call(lay, route, h2, w, *, with_ctx):
    T, D, tt = lay["T"], lay["D"], lay["tt"]
    blk = lay["moe_blk"]
    n_tiles_all = T // tt
    tile_off = 0 if with_ctx else lay["ncb"] * ROW_BLK // tt
    n_tiles = n_tiles_all - tile_off
    n_assign = 2 * tt
    n_fill = N_EXPERTS * SUBLANES
    slots = n_assign + n_fill + blk

    eid = route[:, 0:2].astype(jnp.int32).reshape(n_tiles_all, n_assign)
    wts = route[:, 2:4].reshape(n_tiles_all, n_assign)
    tok = jnp.broadcast_to(jnp.arange(n_assign, dtype=jnp.int32)[None, :] // 2, eid.shape)
    ex = jnp.arange(N_EXPERTS, dtype=jnp.int32)
    counts = jnp.sum(eid[:, :, None] == ex[None, None, :], axis=1, dtype=jnp.int32)
    n_dummy = (-counts) % SUBLANES
    fill_key = jnp.where(jnp.arange(SUBLANES, dtype=jnp.int32)[None, None, :] < n_dummy[:, :, None],
                         ex[None, :, None], N_EXPERTS).reshape(n_tiles_all, n_fill)
    zeros_i = jnp.zeros((n_tiles_all, n_fill), jnp.int32)
    _, slot_tok, slot_w = lax.sort(
        (jnp.concatenate([eid, fill_key], axis=1), jnp.concatenate([tok, zeros_i], axis=1),
         jnp.concatenate([wts, zeros_i.astype(F32)], axis=1)),
        dimension=1, is_stable=True, num_keys=1)
    slot_tok = jnp.pad(slot_tok, ((0, 0), (0, blk)))
    slot_w = jnp.pad(slot_w, ((0, 0), (0, blk)))
    padded = counts + n_dummy
    seg0 = jnp.cumsum(padded, axis=1) - padded
    tab = jnp.stack([seg0, padded], axis=-1).reshape(-1).astype(jnp.int32)

    return pl.pallas_call(
        functools.partial(_moe_kernel, blk=blk, slots=slots, tile_off=tile_off),
        out_shape=jax.ShapeDtypeStruct((T * SUBLANES, LANES), F32),
        grid_spec=pltpu.PrefetchScalarGridSpec(
            num_scalar_prefetch=2,
            grid=(n_tiles, N_EXPERTS),
            in_specs=[
                pl.BlockSpec((tt * SUBLANES, LANES), lambda t, e, *_: (t + tile_off, 0)),
                pl.BlockSpec((1, slots, 1), lambda t, e, *_: (t + tile_off, 0, 0)),
                pl.BlockSpec((1, D, EXPERT_FF), lambda t, e, *_: (e, 0, 0)),
                pl.BlockSpec((1, D, EXPERT_FF), lambda t, e, *_: (e, 0, 0)),
                pl.BlockSpec((1, EXPERT_FF, D), lambda t, e, *_: (e, 0, 0)),
            ],
            out_specs=pl.BlockSpec((tt * SUBLANES, LANES), lambda t, e, *_: (t + tile_off, 0)),
            scratch_shapes=[pltpu.VMEM((blk * SUBLANES, LANES), F32),
                            pltpu.VMEM((blk, D), F32),
                            pltpu.VMEM((SUBLANES * SUBLANES, LANES), F32)],
        ),
        compiler_params=_cparams(("parallel", "arbitrary")),
        name="moe_experts",
    )(tab, slot_tok.reshape(-1), h2, slot_w.reshape(n_tiles_all, slots, 1),
      w["w_ff1"], w["w_ff3"], w["w_ff2"])


def _final_kernel(x_ref, f_ref, mod_ref, g_ref, o_ref, *, tm):
    xf = x_ref[...] + mod_ref[0, 5:6, :] * _load_token_major(f_ref, tm)
    o_ref[...] = _rms(xf) * g_ref[...]


def _final_call(lay, layer, x1, f, mod, g_final):
    T, D, tm = lay["T"], lay["D"], lay["tm"]
    nct, tpb = lay["nct"], lay["tpb"]
    n_lat = T // tm - nct
    return pl.pallas_call(
        functools.partial(_final_kernel, tm=tm),
        out_shape=jax.ShapeDtypeStruct((n_lat * tm, D), F32),
        grid=(n_lat,),
        in_specs=[pl.BlockSpec((tm, D), lambda i: (i + nct, 0)),
                  pl.BlockSpec((tm * SUBLANES, LANES), lambda i: (i + nct, 0)),
                  pl.BlockSpec((1, 6, D), lambda i: (layer * lay["mod_rows"] + i // tpb, 0, 0)),
                  pl.BlockSpec((1, D), lambda i: (0, 0))],
        out_specs=pl.BlockSpec((tm, D), lambda i: (i, 0)),
        compiler_params=_cparams(("parallel",)),
        name="final_norm",
    )(x1, f, mod, g_final)


def _select_cols(wm, segs, scale=None):
    parts = []
    for k, (start, width) in enumerate(segs):
        if start is None:
            parts.append(jnp.zeros((wm.shape[0], width), wm.dtype))
        else:
            blk = wm[:, start:start + width]
            parts.append(blk if scale is None or scale[k] is None else blk * scale[k])
    return jnp.concatenate(parts, axis=1)


def _prep_layer(l, p):
    a_cols = A_Q_RANK + A_KV_RANK + A_ROPE
    b_off = a_cols
    c_off = b_off + 512
    d_off = c_off + 768
    qk_scale = HEAD_DIM ** -0.5 * LOG2E
    gqa_q = lambda off: [(off + hh * HEAD_DIM, HEAD_DIM) for hh in _GQA_PERM]
    segs = ([(0, 256), (256, 128), (None, 64), (384, 32), (None, 32)]
            + gqa_q(b_off) + [(b_off + 256, 128), (b_off + 384, 128)]
            + [(c_off, 256), (c_off + 256, 256), (c_off + 512, 256)]
            + gqa_q(d_off) + [(d_off + 256, 128), (d_off + 384, 128)])
    scale = [None] * len(segs)
    for k in (5, 6, 7, 8, 11):
        scale[k] = qk_scale
    w_in = _select_cols(p["w_in"][l], segs, scale).astype(BF16)
    assert w_in.shape[1] == PROJ_COLS

    hq = A_NOPE + A_ROPE
    segs_q = []
    for hh in range(N_HEADS):
        segs_q += [(hh * hq, hq), (None, LANES - hq)]
    w_q_b = _select_cols(p["w_q_b"][l], segs_q).astype(BF16)
    hk = A_NOPE + HEAD_DIM
    segs_k = []
    for hh in range(N_HEADS):
        segs_k += [(hh * hk, A_NOPE), (None, LANES - A_NOPE)]
    segs_k += [(hh * hk + A_NOPE, HEAD_DIM) for hh in range(N_HEADS)]
    w_kv_b = _select_cols(p["w_kv_b"][l], segs_k).astype(BF16)

    wb = p["w_branch"][l]
    perm_rows = lambda m: jnp.concatenate([m[hh * HEAD_DIM:(hh + 1) * HEAD_DIM] for hh in _GQA_PERM], axis=0)
    w_branch = jnp.stack([wb[0], perm_rows(wb[1]), wb[2], perm_rows(wb[3])]).astype(BF16)

    d = p["w_in"].shape[1]
    w_route = jnp.zeros((d, LANES), F32)
    w_route = w_route.at[:, :N_EXPERTS].set(p["w_router"][l]).at[:, N_EXPERTS:N_EXPERTS + N_GROUPS].set(p["w_group"][l])
    b_route = jnp.zeros((1, LANES), F32)
    b_route = b_route.at[0, :N_EXPERTS].set(p["b_router"][l]).at[0, N_EXPERTS:N_EXPERTS + N_GROUPS].set(p["b_group"][l])
    return {
        "g_mix": p["g_norm_mix"][l][None, :],
        "w_in": w_in,
        "g_q_a": p["g_q_a"][l][None, :],
        "w_q_b": w_q_b,
        "g_kv_a": p["g_kv_a"][l][None, :],
        "w_kv_b": w_kv_b,
        "g_q_d": (jnp.tile(p["g_q_d"][l], 2) * qk_scale)[None, :],
        "g_k_d": jnp.tile(p["g_k_d"][l], 2)[None, :],
        "sink": p["sink_b"][l],
        "rpb": p["rpb_c"][l],
        "w_gate": p["w_gate"][l].astype(BF16),
        "b_gate": p["b_gate"][l][:, None, :],
        "w_branch": w_branch,
        "w_out": p["w_out"][l].astype(BF16),
        "g_ffn": p["g_norm_ffn"][l][None, :],
        "w_route": jnp.stack([w_route.astype(BF16),
                              (w_route - w_route.astype(BF16).astype(F32)).astype(BF16)]),
        "b_route": b_route,
        "w_ff1": p["w_ff1"][l].astype(BF16),
        "w_ff3": p["w_ff3"][l].astype(BF16),
        "w_ff2": p["w_ff2"][l].astype(BF16),
    }


def _rope_tables(S, tm):
    t = jnp.arange(S, dtype=jnp.int32)
    rows = (t // GRID_W).astype(F32)
    cols = (t % GRID_W).astype(F32)

    def cs(rot):
        half = rot // 2
        inv = ROPE_THETA ** (-jnp.arange(0, half, 2, dtype=F32) / half)
        ar_, ac_ = rows[:, None] * inv, cols[:, None] * inv
        cos = jnp.concatenate([jnp.cos(ar_), jnp.cos(ar_), jnp.cos(ac_), jnp.cos(ac_)], axis=-1)
        sin = jnp.concatenate([-jnp.sin(ar_), jnp.sin(ar_), -jnp.sin(ac_), jnp.sin(ac_)], axis=-1)
        return cos, sin

    cos64, sin64 = cs(HEAD_DIM)
    cos32, sin32 = cs(A_ROPE)
    ones = lambda n: jnp.ones((S, n), F32)
    zeros = lambda n: jnp.zeros((S, n), F32)
    tabs = {
        "cos_h": jnp.concatenate([cos64, cos64], axis=-1),
        "sin_h": jnp.concatenate([sin64, sin64], axis=-1),
        "cos_a": jnp.concatenate([ones(A_NOPE), cos32, ones(LANES - A_NOPE - A_ROPE)], axis=-1),
        "sin_a": jnp.concatenate([zeros(A_NOPE), sin32, zeros(LANES - A_NOPE - A_ROPE)], axis=-1),
    }
    ident = {"cos_h": 1.0, "sin_h": 0.0, "cos_a": 1.0, "sin_a": 0.0}
    return {k: jnp.concatenate([v, jnp.full((tm, LANES), ident[k], F32)], axis=0) for k, v in tabs.items()}


def _layout(B, S, n_ctx, D):
    assert n_ctx == ROW_BLK and S % 1024 == 0 and S // GRID_W >= 3 * (ROW_BLK // GRID_W)
    T = B * (n_ctx + S)
    tm = 512 if (B * n_ctx) % 512 == 0 else 256
    tk_dense = 1024 if (B * n_ctx) % 1024 == 0 else (512 if (B * n_ctx) % 512 == 0 else 256)
    tt = 2048 if (B * n_ctx) % 2048 == 0 else B * n_ctx
    assert S % tt == 0
    return {
        "B": B, "S": S, "D": D, "T": T, "tm": tm,
        "ncb": B * n_ctx // ROW_BLK,
        "lb": S // ROW_BLK,
        "nct": B * n_ctx // tm,
        "tpb": S // tm,
        "tk_dense": tk_dense,
        "tq_dense": tk_dense,
        "tq_band": tm,
        "tt": tt, "moe_blk": 256,
        "mod_rows": 16,
    }


def kernel(x, c, ctx, c_ctx, w_mod, b_mod, g_norm_mix, w_in, g_q_a, w_q_b, g_kv_a, w_kv_b, sink_b, rpb_c,
           g_q_d, g_k_d, w_gate, b_gate, w_branch, w_out, g_norm_ffn, w_group, b_group, w_router, b_router,
           w_ff1, w_ff3, w_ff2, g_final):
    B, S, D = x.shape
    n_ctx = ctx.shape[1]
    depth = w_mod.shape[0]
    lay = _layout(B, S, n_ctx, D)
    params = dict(w_in=w_in, g_norm_mix=g_norm_mix, g_q_a=g_q_a, w_q_b=w_q_b, g_kv_a=g_kv_a, w_kv_b=w_kv_b,
                  sink_b=sink_b, rpb_c=rpb_c, g_q_d=g_q_d, g_k_d=g_k_d, w_gate=w_gate, b_gate=b_gate,
                  w_branch=w_branch, w_out=w_out, g_norm_ffn=g_norm_ffn, w_group=w_group, b_group=b_group,
                  w_router=w_router, b_router=b_router, w_ff1=w_ff1, w_ff3=w_ff3, w_ff2=w_ff2)

    c_all = jnp.zeros((lay["mod_rows"], D), F32).at[:B].set(c).at[B].set(c_ctx)
    mod = _modulation(c_all, w_mod, b_mod).reshape(depth * lay["mod_rows"], 6, D)
    tabs = _rope_tables(S, lay["tm"])
    win_bias = _window_bias(S, lay["tq_band"])
    xf = jnp.concatenate([ctx.reshape(B * n_ctx, D), x.reshape(B * S, D)], axis=0)

    f = None
    for l in range(depth):
        with_ctx = l < depth - 1
        w = _prep_layer(l, params)
        xf, pr = _proj_call(lay, xf, f, mod, mod, l, w, tabs)
        o = {}
        for kind, kl in (("A", "a"), ("B", "b"), ("C", "c"), ("D", "d")):
            q, k, v = pr["q" + kl], pr["k" + kl], pr["v" + kl]
            sink = w["sink"] if kind == "B" else None
            bias = win_bias if kind == "B" else None
            if kind == "C":
                bias = _neighbourhood_bias(w["rpb"], S // GRID_W, lay["tq_band"])
            o[kind] = _attn_latent_call(lay, kind, q, k, v, sink=sink, bias=bias)
            if with_ctx:
                o[kind] = _attn_context_call(lay, kind, q, k, v, o[kind], sink=sink)
        xf, h2, route = _merge_call(lay, l, pr["h"], o, xf, mod, w, with_ctx=with_ctx)
        f = _moe_call(lay, route, h2, w, with_ctx=with_ctx)
    out = _final_call(lay, depth - 1, xf, f, mod, g_final[None, :])
    return out.reshape(B, S, D)
```

```python
import functools

import numpy as np
import jax
import jax.numpy as jnp
from jax import lax
from jax.experimental import pallas as pl
from jax.experimental.pallas import tpu as pltpu

F32 = jnp.float32
BF16 = jnp.bfloat16
HIGHEST = lax.Precision.HIGHEST

GRID_W = 64
ROPE_THETA = 10000.0
EPS = 1e-6
NEG_INF = -1e30
LOG2E = 1.4426950408889634
HEAD_DIM = 64
N_HEADS = 4
BRANCH_W = 256
A_Q_RANK = 256
A_KV_RANK = 128
A_NOPE = 64
A_ROPE = 32
NA_KH = 8
NA_KW = 16
WINDOW = 128
N_GROUPS = 4
EXPERTS_PER_GROUP = 8
N_EXPERTS = 32
EXPERT_FF = 256

LANES = 128
SUBLANES = 8
ROW_BLK = 256
MERGE_TN = 256
MOE_EXPERTS_PER_STEP = 2
VMEM_LIMIT = 56 * 1024 * 1024

_PROJ_GROUPS = (("cq", 256), ("ckv", 128), ("kr", 128), ("qb", 256), ("kb", 128), ("vb", 128),
                ("qc", 256), ("kc", 256), ("vc", 256), ("qd", 256), ("kd", 128), ("vd", 128))
_PROJ_OFF = {}
_o = 0
for _n, _w in _PROJ_GROUPS:
    _PROJ_OFF[_n] = (_o, _w)
    _o += _w
PROJ_COLS = _o
_GQA_PERM = (0, 2, 1, 3)


def _cparams(sem):
    return pltpu.CompilerParams(dimension_semantics=sem, vmem_limit_bytes=VMEM_LIMIT)


def _lane_iota(shape):
    return lax.broadcasted_iota(jnp.int32, shape, len(shape) - 1)


def _sigmoid(x):
    return 1.0 / (1.0 + jnp.exp(-x))


def _mod_kernel(c_ref, w_ref, b_ref, o_ref):
    cf = c_ref[...]
    s = cf * _sigmoid(cf)
    o_ref[0] = jnp.dot(s, w_ref[0], precision=HIGHEST, preferred_element_type=F32) + b_ref[0]


def _modulation(c_all, w_mod, b_mod):
    n_layers, d, n_out = w_mod.shape
    rows = c_all.shape[0]
    tn = 1536
    return pl.pallas_call(
        _mod_kernel,
        out_shape=jax.ShapeDtypeStruct((n_layers, rows, n_out), F32),
        grid=(n_layers, n_out // tn),
        in_specs=[pl.BlockSpec((rows, d), lambda l, j: (0, 0)),
                  pl.BlockSpec((1, d, tn), lambda l, j: (l, 0, j)),
                  pl.BlockSpec((1, 1, tn), lambda l, j: (l, 0, j))],
        out_specs=pl.BlockSpec((1, rows, tn), lambda l, j: (l, 0, j)),
        compiler_params=_cparams(("arbitrary", "arbitrary")),
        name="modulation",
    )(c_all, w_mod, b_mod.reshape(n_layers, 1, n_out))


def _rms(x):
    return x * lax.rsqrt(jnp.mean(x * x, axis=-1, keepdims=True) + EPS)


def _swap_blocks(x, blk):
    lane = _lane_iota(x.shape)
    up = pltpu.roll(x, LANES - blk, 1)
    dn = pltpu.roll(x, blk, 1)
    return jnp.where((lane // blk) % 2 == 0, up, dn)


def _rope(x, cos, sin, blk):
    return x * cos + _swap_blocks(x, blk) * sin


def _pair_norm(x, g):
    lo = _lane_iota(x.shape) < HEAD_DIM
    sq = x * x
    s_lo = jnp.sum(jnp.where(lo, sq, 0.0), axis=-1, keepdims=True)
    s_hi = jnp.sum(jnp.where(lo, 0.0, sq), axis=-1, keepdims=True)
    ms = jnp.where(lo, s_lo, s_hi) * (1.0 / HEAD_DIM)
    return x * lax.rsqrt(ms + EPS) * g


def _load_token_major(ref, rows):
    return jnp.concatenate(
        [ref[pl.ds(c, rows, stride=SUBLANES), :] for c in range(SUBLANES)], axis=-1)


def _proj_kernel(*refs, with_f, tm, scale_a):
    it = iter(refs)
    x_ref = next(it)
    if with_f:
        f_ref = next(it)
        modp_ref = next(it)
    mod_ref = next(it)
    gmix_ref, win_ref, gqa_ref, wqb_ref, gkva_ref, wkvb_ref, gqd_ref, gkd_ref = (next(it) for _ in range(8))
    cosh_ref, sinh_ref, cosa_ref, sina_ref = (next(it) for _ in range(4))
    if with_f:
        x2_ref = next(it)
    h_ref = next(it)
    qa_ref, ka_ref, va_ref, qb_ref, kb_ref, vb_ref, qc_ref, kc_ref, vc_ref, qd_ref, kd_ref, vd_ref = (
        next(it) for _ in range(12))

    xf = x_ref[...]
    if with_f:
        xf = xf + modp_ref[0, 5:6, :] * _load_token_major(f_ref, tm)
        x2_ref[...] = xf
    h = _rms(xf) * gmix_ref[...] * (1.0 + mod_ref[0, 1:2, :]) + mod_ref[0, 0:1, :]
    hb = h.astype(BF16)
    h_ref[...] = hb
    p = jnp.dot(hb, win_ref[...], preferred_element_type=F32)

    def grp(name):
        o, w = _PROJ_OFF[name]
        return p[:, o:o + w]

    cosh, sinh = cosh_ref[...], sinh_ref[...]
    cosa, sina = cosa_ref[...], sina_ref[...]

    cq = (_rms(grp("cq")) * gqa_ref[...]).astype(BF16)
    qa = jnp.dot(cq, wqb_ref[...], preferred_element_type=F32)
    for hd in range(N_HEADS):
        sl = slice(hd * LANES, (hd + 1) * LANES)
        qa_ref[:, sl] = (_rope(qa[:, sl], cosa, sina, 8) * scale_a).astype(BF16)
    ckv = (_rms(grp("ckv")) * gkva_ref[...]).astype(BF16)
    kva = jnp.dot(ckv, wkvb_ref[...], preferred_element_type=F32)
    kr = _rope(grp("kr"), cosa, sina, 8)
    for hd in range(N_HEADS):
        sl = slice(hd * LANES, (hd + 1) * LANES)
        ka_ref[:, sl] = (kva[:, sl] + kr).astype(BF16)
    va_ref[...] = kva[:, N_HEADS * LANES:].astype(BF16)

    qb = grp("qb")
    for j in range(2):
        sl = slice(j * LANES, (j + 1) * LANES)
        qb_ref[:, sl] = _rope(qb[:, sl], cosh, sinh, 16).astype(BF16)
    kb_ref[...] = _rope(grp("kb"), cosh, sinh, 16).astype(BF16)
    vb_ref[...] = grp("vb").astype(BF16)

    qc_ref[...] = grp("qc").astype(BF16)
    kc_ref[...] = grp("kc").astype(BF16)
    vc_ref[...] = grp("vc").astype(BF16)

    qd = grp("qd")
    for j in range(2):
        sl = slice(j * LANES, (j + 1) * LANES)
        qd_ref[:, sl] = _rope(_pair_norm(qd[:, sl], gqd_ref[...]), cosh, sinh, 16).astype(BF16)
    kd_ref[...] = _rope(_pair_norm(grp("kd"), gkd_ref[...]), cosh, sinh, 16).astype(BF16)
    vd_ref[...] = grp("vd").astype(BF16)


def _proj_call(lay, x, f, modp, mod, layer, w, tabs):
    T, D, tm = lay["T"], lay["D"], lay["tm"]
    nct, tpb, B = lay["nct"], lay["tpb"], lay["B"]
    with_f = f is not None
    n_tiles = T // tm

    def mod_row(i):
        return jnp.where(i < nct, B, (i - nct) // tpb)

    def tab_blk(i):
        return jnp.where(i < nct, tpb, (i - nct) % tpb)

    row = lambda i: (i, 0)
    const = lambda i: (0, 0)
    in_specs = [pl.BlockSpec((tm, D), row)]
    args = [x]
    if with_f:
        in_specs += [pl.BlockSpec((tm * SUBLANES, LANES), row),
                     pl.BlockSpec((1, 6, D), lambda i: ((layer - 1) * lay["mod_rows"] + mod_row(i), 0, 0))]
        args += [f, modp]
    in_specs += [pl.BlockSpec((1, 6, D), lambda i: (layer * lay["mod_rows"] + mod_row(i), 0, 0))]
    args += [mod]
    for name in ("g_mix", "w_in", "g_q_a", "w_q_b", "g_kv_a", "w_kv_b", "g_q_d", "g_k_d"):
        a = w[name]
        in_specs.append(pl.BlockSpec(a.shape, const))
        args.append(a)
    for tname in ("cos_h", "sin_h", "cos_a", "sin_a"):
        in_specs.append(pl.BlockSpec((tm, LANES), lambda i: (tab_blk(i), 0)))
        args.append(tabs[tname])

    widths = [("h", D), ("qa", 512), ("ka", 512), ("va", 256), ("qb", 256), ("kb", 128), ("vb", 128),
              ("qc", 256), ("kc", 256), ("vc", 256), ("qd", 256), ("kd", 128), ("vd", 128)]
    out_shape, out_specs = [], []
    if with_f:
        out_shape.append(jax.ShapeDtypeStruct((T, D), F32))
        out_specs.append(pl.BlockSpec((tm, D), row))
    for _, wd in widths:
        out_shape.append(jax.ShapeDtypeStruct((T, wd), BF16))
        out_specs.append(pl.BlockSpec((tm, wd), row))

    outs = pl.pallas_call(
        functools.partial(_proj_kernel, with_f=with_f, tm=tm,
                          scale_a=float((A_NOPE + A_ROPE) ** -0.5 * LOG2E)),
        out_shape=out_shape, grid=(n_tiles,), in_specs=in_specs, out_specs=out_specs,
        compiler_params=_cparams(("parallel",)),
        name="proj_in",
    )(*args)
    outs = list(outs)
    x2 = outs.pop(0) if with_f else x
    names = [n for n, _ in widths]
    return x2, dict(zip(names, outs))


_NT = (((1,), (1,)), ((), ()))


def _head_plan(kind):
    if kind == "A":
        return tuple((r, None, r, r // 2) for r in range(N_HEADS))
    if kind == "C":
        return tuple((r // 2, r % 2, r // 2, r // 2) for r in range(N_HEADS))
    return tuple((r // 2, r % 2, 0, 0) for r in range(N_HEADS))


def _head_query(q_ref, plan_r):
    qt, half, _, _ = plan_r
    src = q_ref[:, qt * LANES:(qt + 1) * LANES]
    if half is None:
        return src
    lane = _lane_iota(src.shape)
    keep = (lane < HEAD_DIM) if half == 0 else (lane >= HEAD_DIM)
    return jnp.where(keep, src, jnp.zeros_like(src))


def _score_chunks(q, k_blocks, bias_blocks):
    chunks = []
    for kb, bb in zip(k_blocks, bias_blocks):
        s = lax.dot_general(q, kb, _NT, preferred_element_type=F32)
        if bb is not None:
            s = s + bb
        chunks += [s[:, c * LANES:(c + 1) * LANES] for c in range(s.shape[1] // LANES)]
    return chunks


def _row_max(chunks):
    m = functools.reduce(jnp.maximum, chunks)
    return jnp.broadcast_to(jnp.max(m, axis=-1, keepdims=True), m.shape)


def _weighted_values(p_chunks, v_blocks, half):
    pv, idx = None, 0
    for vb in v_blocks:
        n = vb.shape[0] // LANES
        p = jnp.concatenate(p_chunks[idx:idx + n], axis=1).astype(BF16)
        idx += n
        lane = _lane_iota(vb.shape)
        own = (lane < HEAD_DIM) if half == 0 else (lane >= HEAD_DIM)
        d = jnp.dot(p, jnp.where(own, vb, jnp.ones_like(vb)), preferred_element_type=F32)
        pv = d if pv is None else pv + d
    return pv


def _softmax_once(q, k_blocks, v_blocks, bias_blocks, sink, half):
    chunks = _score_chunks(q, k_blocks, bias_blocks)
    m = _row_max(chunks)
    if sink is not None:
        m = jnp.maximum(m, sink)
    p = [jnp.exp2(c - m) for c in chunks]
    pv = _weighted_values(p, v_blocks, half)
    l = pltpu.roll(pv, HEAD_DIM, 1)
    if sink is not None:
        l = l + jnp.exp2(sink - m)
    return pv / l


def _store_heads(o_ref, outs):
    lane = _lane_iota(outs[0].shape)
    for g in range(2):
        o_ref[:, g * LANES:(g + 1) * LANES] = jnp.where(
            lane < HEAD_DIM, outs[2 * g], outs[2 * g + 1]).astype(o_ref.dtype)


def _attn_band_kernel(*refs, kind, lb, n_kv, tq):
    it = iter(refs)
    q_ref = next(it)
    k_refs = [next(it) for _ in range(n_kv)]
    v_refs = [next(it) for _ in range(n_kv)]
    bias_ref = next(it) if n_kv > 1 else None
    sink_ref = next(it) if kind == "B" else None
    if n_kv == 1:
        next(it)
    o_ref = next(it)
    t = pl.program_id(0) % lb
    var = jnp.where(t == 0, 0, jnp.where(t == lb - 1, 2, 1))
    outs = []
    for r, plan_r in enumerate(_head_plan(kind)):
        _, _, kt, vt = plan_r
        ksl = slice(kt * LANES, (kt + 1) * LANES)
        vsl = slice(vt * LANES, (vt + 1) * LANES)
        bias = [None]
        for j in range(n_kv - 1):
            if kind == "C":
                bias.append(bias_ref[var, j, r * tq:(r + 1) * tq, :])
            else:
                bias.append(bias_ref[var, j])
        sink = sink_ref[_GQA_PERM[r]] * LOG2E if kind == "B" else None
        outs.append(_softmax_once(_head_query(q_ref, plan_r), [k[:, ksl] for k in k_refs],
                                  [v[:, vsl] for v in v_refs], bias, sink, r % 2))
    _store_heads(o_ref, outs)


def _attn_dense_kernel(q_ref, kc_ref, vc_ref, kl_ref, vl_ref, o_ref, qs_ref, m_ref, acc_ref,
                       *, kind, n_steps, tq):
    s = pl.program_id(1)
    plan = _head_plan(kind)

    def update(first):
        for r, (_, _, kt, vt) in enumerate(plan):
            rows = slice(r * tq, (r + 1) * tq)
            ksl = slice(kt * LANES, (kt + 1) * LANES)
            vsl = slice(vt * LANES, (vt + 1) * LANES)
            if first:
                q = _head_query(q_ref, plan[r])
                qs_ref[rows, :] = q
                k_blocks = [kc_ref[:, ksl], kl_ref[:, ksl]]
                v_blocks = [vc_ref[:, vsl], vl_ref[:, vsl]]
            else:
                q = qs_ref[rows, :]
                k_blocks = [kl_ref[:, ksl]]
                v_blocks = [vl_ref[:, vsl]]
            chunks = _score_chunks(q, k_blocks, [None] * len(k_blocks))
            m_cur = _row_max(chunks)
            if first:
                m_new = m_cur
            else:
                m_prev = m_ref[rows, :]
                m_new = jnp.maximum(m_prev, m_cur)
                alpha = jnp.exp2(m_prev - m_new)
            p = [jnp.exp2(c - m_new) for c in chunks]
            pv = _weighted_values(p, v_blocks, r % 2)
            if first:
                acc_ref[rows, :] = pv
            else:
                acc_ref[rows, :] = alpha * acc_ref[rows, :] + pv
            m_ref[rows, :] = m_new

    @pl.when(s == 0)
    def _():
        update(True)

    @pl.when(s > 0)
    def _():
        update(False)

    @pl.when(s == n_steps - 1)
    def _():
        outs = []
        for r in range(N_HEADS):
            rows = slice(r * tq, (r + 1) * tq)
            acc = acc_ref[rows, :]
            outs.append(acc / pltpu.roll(acc, HEAD_DIM, 1))
        _store_heads(o_ref, outs)


def _attn_latent_call(lay, kind, q, k, v, *, sink=None, bias=None):
    T, B, ncb, lb, S = lay["T"], lay["B"], lay["ncb"], lay["lb"], lay["S"]
    tq = ROW_BLK
    qw, kw, vw = q.shape[1], k.shape[1], v.shape[1]
    out_shape = jax.ShapeDtypeStruct((T, BRANCH_W), BF16)
    if kind in ("A", "D"):
        tk, tq = lay["tk_dense"], lay["tq_dense"]
        n_steps = S // tk
        lat0 = ncb * ROW_BLK // tk
        q0 = ncb * ROW_BLK // tq
        qpb = S // tq
        lat_blk = lambda i, s: (lat0 + (i // qpb) * n_steps + s, 0)
        return pl.pallas_call(
            functools.partial(_attn_dense_kernel, kind=kind, n_steps=n_steps, tq=tq),
            out_shape=out_shape,
            grid=(B * qpb, n_steps),
            in_specs=[pl.BlockSpec((tq, qw), lambda i, s: (q0 + i, 0)),
                      pl.BlockSpec((ROW_BLK, kw), lambda i, s: (i // qpb, 0)),
                      pl.BlockSpec((ROW_BLK, vw), lambda i, s: (i // qpb, 0)),
                      pl.BlockSpec((tk, kw), lat_blk),
                      pl.BlockSpec((tk, vw), lat_blk)],
            out_specs=pl.BlockSpec((tq, BRANCH_W), lambda i, s: (q0 + i, 0)),
            scratch_shapes=[pltpu.VMEM((N_HEADS * tq, LANES), BF16),
                            pltpu.VMEM((N_HEADS * tq, LANES), F32),
                            pltpu.VMEM((N_HEADS * tq, LANES), F32)],
            compiler_params=_cparams(("parallel", "arbitrary")),
            name="attn_" + kind,
        )(q, k, v, k, v)

    tq = lay["tq_band"]
    qpb = S // tq
    q0 = ncb * ROW_BLK // tq
    bpq = tq // ROW_BLK

    def nb(i, d):
        return (ncb + (i // qpb) * lb + jnp.clip((i % qpb) * bpq + d, 0, lb - 1), 0)

    kv_maps = [lambda i: (i // qpb, 0)] + [functools.partial(nb, d=d) for d in _band_offsets(tq)]
    n_kv = len(kv_maps)
    in_specs = [pl.BlockSpec((tq, qw), lambda i: (q0 + i, 0))]
    in_specs += [pl.BlockSpec((ROW_BLK, kw), m) for m in kv_maps]
    in_specs += [pl.BlockSpec((ROW_BLK, vw), m) for m in kv_maps]
    in_specs.append(pl.BlockSpec(bias.shape, lambda i: (0,) * bias.ndim, pipeline_mode=pl.Buffered(1)))
    args = [q] + [k] * n_kv + [v] * n_kv + [bias]
    if kind == "B":
        in_specs.append(pl.BlockSpec(memory_space=pltpu.SMEM))
        args.append(sink)
    return pl.pallas_call(
        functools.partial(_attn_band_kernel, kind=kind, lb=qpb, n_kv=n_kv, tq=tq),
        out_shape=out_shape,
        grid=(B * qpb,),
        in_specs=in_specs,
        out_specs=pl.BlockSpec((tq, BRANCH_W), lambda i: (q0 + i, 0)),
        compiler_params=_cparams(("parallel",)),
        name="attn_" + kind,
    )(*args)


def _attn_context_call(lay, kind, q, k, v, o, *, sink=None):
    ncb = lay["ncb"]
    qw, kw, vw = q.shape[1], k.shape[1], v.shape[1]
    blk = lambda i: (i, 0)
    in_specs = [pl.BlockSpec((ROW_BLK, qw), blk), pl.BlockSpec((ROW_BLK, kw), blk),
                pl.BlockSpec((ROW_BLK, vw), blk)]
    args = [q, k, v]
    if kind == "B":
        in_specs.append(pl.BlockSpec(memory_space=pltpu.SMEM))
        args.append(sink)
    in_specs.append(pl.BlockSpec(memory_space=pl.ANY))
    args.append(o)
    return pl.pallas_call(
        functools.partial(_attn_band_kernel, kind=kind, lb=1, n_kv=1, tq=ROW_BLK),
        out_shape=jax.ShapeDtypeStruct(o.shape, o.dtype),
        grid=(ncb,),
        in_specs=in_specs,
        out_specs=pl.BlockSpec((ROW_BLK, BRANCH_W), blk),
        input_output_aliases={len(args) - 1: 0},
        compiler_params=_cparams(("parallel",)),
        name="attn_ctx_" + kind,
    )(*args)


def _band_offsets(tq):
    return tuple(range(-1, tq // ROW_BLK + 1))


def _window_bias(S, tq):
    lb, qpb, bpq = S // ROW_BLK, S // tq, tq // ROW_BLK
    offs = _band_offsets(tq)
    qa, ka = np.arange(tq), np.arange(ROW_BLK)
    out = np.full((3, len(offs), tq, ROW_BLK), NEG_INF, np.float32)
    for vi, t_rep in enumerate((0, 1, qpb - 1)):
        for di, d in enumerate(offs):
            kt = t_rep * bpq + d
            if not 0 <= kt < lb:
                continue
            qpos = t_rep * tq + qa
            kpos = kt * ROW_BLK + ka
            ok = np.abs(qpos[:, None] - kpos[None, :]) <= WINDOW
            out[vi, di] = np.where(ok, 0.0, NEG_INF)
    return jnp.asarray(out)


def _neighbourhood_bias(rpb, rows_total, tq):
    lb = rows_total * GRID_W // ROW_BLK
    qpb, bpq = rows_total * GRID_W // tq, tq // ROW_BLK
    rpt = ROW_BLK // GRID_W
    rpq = tq // GRID_W
    kh = min(NA_KH, rows_total)
    qa, ka = np.arange(tq), np.arange(ROW_BLK)
    q_sub, q_col = qa // GRID_W, qa % GRID_W
    k_sub, k_col = ka // GRID_W, ka % GRID_W
    n_dr, n_dc = 2 * NA_KH - 1, 2 * NA_KW - 1
    col = np.arange(GRID_W)
    dc = np.clip(col[None, :] - col[:, None], -(NA_KW - 1), NA_KW - 1) + NA_KW - 1
    hot_c = (dc[:, :, None] == np.arange(n_dc)).astype(np.float32)
    by_col = jnp.einsum("huv,cdv->hucd", rpb.astype(F32) * LOG2E, jnp.asarray(hot_c),
                        precision=HIGHEST)
    offs = _band_offsets(tq)
    vals = []
    for d in offs:
        dr = np.clip(d * rpt + np.arange(rpt)[None, :] - np.arange(rpq)[:, None],
                     -(NA_KH - 1), NA_KH - 1) + NA_KH - 1
        hot_r = (dr[:, :, None] == np.arange(n_dr)).astype(np.float32)
        v = jnp.einsum("abu,hucd->hacbd", jnp.asarray(hot_r), by_col, precision=HIGHEST)
        vals.append(v.reshape(N_HEADS, tq, ROW_BLK))
    out = []
    for t_rep in (0, 1, qpb - 1):
        per_block = []
        for di, d in enumerate(offs):
            kt = t_rep * bpq + d
            q_row = t_rep * rpq + q_sub
            k_row = kt * rpt + k_sub
            r_start = np.clip(q_row - kh // 2, 0, rows_total - kh)
            row_ok = (k_row[None] >= r_start[:, None]) & (k_row[None] < r_start[:, None] + kh)
            c_start = np.clip(q_col - NA_KW // 2, 0, GRID_W - NA_KW)
            col_ok = (k_col[None] >= c_start[:, None]) & (k_col[None] < c_start[:, None] + NA_KW)
            ok = row_ok & col_ok & (0 <= kt < lb)
            per_block.append(jnp.where(jnp.asarray(ok)[None], vals[di], NEG_INF).reshape(-1, ROW_BLK))
        out.append(jnp.stack(per_block))
    return jnp.stack(out)


def _merge_kernel(h_ref, oa_ref, ob_ref, oc_ref, od_ref, x_ref, mod_ref, gffn_ref, wg_ref, bg_ref,
                  wb_ref, wout_ref, wr_ref, br_ref, x1_ref, h2_ref, route_ref, y_ref, *, tm):
    hb = h_ref[...]
    d_model = hb.shape[1]
    for t in range(d_model // MERGE_TN):
        cs = slice(t * MERGE_TN, (t + 1) * MERGE_TN)
        y = None
        for n, o_ref in enumerate((oa_ref, ob_ref, oc_ref, od_ref)):
            gate = _sigmoid(jnp.dot(hb, wg_ref[n, :, cs], preferred_element_type=F32) + bg_ref[n, :, cs])
            u = gate * jnp.dot(o_ref[...], wb_ref[n, :, cs], preferred_element_type=F32)
            y = u if y is None else y + u
        y_ref[:, cs] = y.astype(BF16)
    z = jnp.dot(y_ref[...], wout_ref[...], preferred_element_type=F32)
    x1 = x_ref[...] + mod_ref[0, 2:3, :] * z
    x1_ref[...] = x1
    h2 = _rms(x1) * gffn_ref[...] * (1.0 + mod_ref[0, 4:5, :]) + mod_ref[0, 3:4, :]
    for c in range(SUBLANES):
        h2_ref[pl.ds(c, tm, stride=SUBLANES), :] = h2[:, c * LANES:(c + 1) * LANES]

    h2_hi = h2.astype(BF16)
    h2_lo = (h2 - h2_hi.astype(F32)).astype(BF16)
    logit = (jnp.dot(h2_hi, wr_ref[0], preferred_element_type=F32)
             + jnp.dot(h2_lo, wr_ref[0], preferred_element_type=F32)
             + jnp.dot(h2_hi, wr_ref[1], preferred_element_type=F32)) + br_ref[...]
    lane = _lane_iota(logit.shape)
    big = jnp.int32(1 << 20)
    is_g = (lane >= N_EXPERTS) & (lane < N_EXPERTS + N_GROUPS)
    gl = jnp.where(is_g, logit, NEG_INF)
    gmax = jnp.max(gl, axis=-1, keepdims=True)
    gsel = jnp.min(jnp.where(gl == gmax, lane - N_EXPERTS, big), axis=-1, keepdims=True)
    gw = 1.0 / jnp.sum(jnp.where(is_g, jnp.exp(gl - gmax), 0.0), axis=-1, keepdims=True)
    in_grp = (lane < N_EXPERTS) & ((lane // EXPERTS_PER_GROUP) == gsel)
    el = jnp.where(in_grp, logit, NEG_INF)
    v1 = jnp.max(el, axis=-1, keepdims=True)
    i1 = jnp.min(jnp.where(el == v1, lane, big), axis=-1, keepdims=True)
    el2 = jnp.where(lane == i1, NEG_INF, el)
    v2 = jnp.max(el2, axis=-1, keepdims=True)
    i2 = jnp.min(jnp.where(el2 == v2, lane, big), axis=-1, keepdims=True)
    e21 = jnp.exp(v2 - v1)
    w1 = gw / (1.0 + e21)
    w2 = gw * e21 / (1.0 + e21)
    route_ref[...] = jnp.where(lane == 0, i1.astype(F32),
                               jnp.where(lane == 1, i2.astype(F32),
                                         jnp.where(lane == 2, w1, jnp.where(lane == 3, w2, 0.0))))


def _merge_call(lay, layer, h, o, x, mod, w, *, with_ctx):
    T, D, tm = lay["T"], lay["D"], lay["tm"]
    nct, tpb, B = lay["nct"], lay["tpb"], lay["B"]
    off = 0 if with_ctx else nct
    n_tiles = T // tm - off

    def mod_row(i):
        return jnp.where(i < nct, B, (i - nct) // tpb)

    row = lambda i: (i + off, 0)
    const2 = lambda i: (0, 0)
    const3 = lambda i: (0, 0, 0)
    in_specs = [pl.BlockSpec((tm, D), row)]
    in_specs += [pl.BlockSpec((tm, BRANCH_W), row)] * 4
    in_specs += [pl.BlockSpec((tm, D), row),
                 pl.BlockSpec((1, 6, D), lambda i: (layer * lay["mod_rows"] + mod_row(i + off), 0, 0)),
                 pl.BlockSpec((1, D), const2),
                 pl.BlockSpec(w["w_gate"].shape, const3),
                 pl.BlockSpec(w["b_gate"].shape, const3),
                 pl.BlockSpec(w["w_branch"].shape, const3),
                 pl.BlockSpec((D, D), const2),
                 pl.BlockSpec((2, D, LANES), const3),
                 pl.BlockSpec((1, LANES), const2)]
    out_shape = [jax.ShapeDtypeStruct((T, D), F32),
                 jax.ShapeDtypeStruct((T * SUBLANES, LANES), F32),
                 jax.ShapeDtypeStruct((T, LANES), F32)]
    out_specs = [pl.BlockSpec((tm, D), row),
                 pl.BlockSpec((tm * SUBLANES, LANES), row),
                 pl.BlockSpec((tm, LANES), row)]
    return pl.pallas_call(
        functools.partial(_merge_kernel, tm=tm),
        out_shape=out_shape, grid=(n_tiles,), in_specs=in_specs, out_specs=out_specs,
        scratch_shapes=[pltpu.VMEM((tm, D), BF16)],
        compiler_params=_cparams(("parallel",)),
        name="merge",
    )(h, o["A"], o["B"], o["C"], o["D"], x, mod, w["g_ffn"], w["w_gate"], w["b_gate"],
      w["w_branch"], w["w_out"], w["w_route"], w["b_route"])


def _moe_kernel(tab_ref, tok_ref, h2_ref, sw_ref, w1_ref, w3_ref, w2_ref, f_ref,
                xg_ref, y_ref, st_ref, *, blk, slots, tile_off):
    ti = pl.program_id(0)
    step = pl.program_id(1)

    @pl.when(step == 0)
    def _():
        f_ref[...] = jnp.zeros_like(f_ref)
        xg_ref[...] = jnp.zeros_like(xg_ref)

    tok_base = (ti + tile_off) * slots
    for j in range(MOE_EXPERTS_PER_STEP):
        base = ((ti + tile_off) * N_EXPERTS + step * MOE_EXPERTS_PER_STEP + j) * 2
        _moe_expert(tab_ref[base], tab_ref[base + 1], tok_base, tok_ref, h2_ref, sw_ref,
                    w1_ref.at[j], w3_ref.at[j], w2_ref.at[j], f_ref, xg_ref, y_ref, st_ref, blk)


def _moe_expert(seg0, n_pad, tok_base, tok_ref, h2_ref, sw_ref, w1_ref, w3_ref, w2_ref, f_ref,
                xg_ref, y_ref, st_ref, blk):
    n_blocks = (n_pad + blk - 1) // blk

    def block_body(b, carry):
        off = pl.multiple_of(seg0 + b * blk, SUBLANES)
        n_grp = jnp.minimum(n_pad - b * blk, blk) // SUBLANES

        def gather(gi, c):
            for u in range(SUBLANES):
                r = gi * SUBLANES + u
                tok = tok_ref[tok_base + off + r]
                src = pl.multiple_of(tok * SUBLANES, SUBLANES)
                dst = pl.multiple_of(r * SUBLANES, SUBLANES)
                xg_ref[pl.ds(dst, SUBLANES), :] = h2_ref[pl.ds(src, SUBLANES), :]
            return c

        lax.fori_loop(0, n_grp, gather, 0)
        xb = _load_token_major(xg_ref, blk).astype(BF16)
        a = jnp.dot(xb, w1_ref[...], preferred_element_type=F32)
        g = jnp.dot(xb, w3_ref[...], preferred_element_type=F32)
        hid = (a * _sigmoid(a) * g).astype(BF16)
        y = jnp.dot(hid, w2_ref[...], preferred_element_type=F32)
        y_ref[...] = y * sw_ref[0, pl.ds(off, blk), :]

        def scatter(gi, c):
            r0 = pl.multiple_of(gi * SUBLANES, SUBLANES)
            for ch in range(SUBLANES):
                st_ref[pl.ds(ch, SUBLANES, stride=SUBLANES), :] = y_ref[pl.ds(r0, SUBLANES),
                                                                        ch * LANES:(ch + 1) * LANES]
            dsts, vals = [], []
            for u in range(SUBLANES):
                tok = tok_ref[tok_base + off + r0 + u]
                dsts.append(pl.multiple_of(tok * SUBLANES, SUBLANES))
                vals.append(f_ref[pl.ds(dsts[u], SUBLANES), :] + st_ref[u * SUBLANES:(u + 1) * SUBLANES, :])
            for u in reversed(range(SUBLANES)):
                f_ref[pl.ds(dsts[u], SUBLANES), :] = vals[u]
            return c

        lax.fori_loop(0, n_grp, scatter, 0)
        return carry

    lax.fori_loop(0, n_blocks, block_body, 0)


def _moe_call(lay, route, h2, w, *, with_ctx):
    T, D, tt = lay["T"], lay["D"], lay["tt"]
    blk = lay["moe_blk"]
    n_tiles_all = T // tt
    tile_off = 0 if with_ctx else lay["ncb"] * ROW_BLK // tt
    n_tiles = n_tiles_all - tile_off
    n_assign = 2 * tt
    n_fill = N_EXPERTS * SUBLANES
    slots = n_assign + n_fill + blk

    eid = route[:, 0:2].astype(jnp.int32).reshape(n_tiles_all, n_assign)
    wts = route[:, 2:4].reshape(n_tiles_all, n_assign)
    tok = jnp.broadcast_to(jnp.arange(n_assign, dtype=jnp.int32)[None, :] // 2, eid.shape)
    ex = jnp.arange(N_EXPERTS, dtype=jnp.int32)
    counts = jnp.sum(eid[:, :, None] == ex[None, None, :], axis=1, dtype=jnp.int32)
    n_dummy = (-counts) % SUBLANES
    fill_key = jnp.where(jnp.arange(SUBLANES, dtype=jnp.int32)[None, None, :] < n_dummy[:, :, None],
                         ex[None, :, None], N_EXPERTS).reshape(n_tiles_all, n_fill)
    zeros_i = jnp.zeros((n_tiles_all, n_fill), jnp.int32)
    _, slot_tok, slot_w = lax.sort(
        (jnp.concatenate([eid, fill_key], axis=1), jnp.concatenate([tok, zeros_i], axis=1),
         jnp.concatenate([wts, zeros_i.astype(F32)], axis=1)),
        dimension=1, is_stable=True, num_keys=1)
    slot_tok = jnp.pad(slot_tok, ((0, 0), (0, blk)))
    slot_w = jnp.pad(slot_w, ((0, 0), (0, blk)))
    padded = counts + n_dummy
    seg0 = jnp.cumsum(padded, axis=1) - padded
    tab = jnp.stack([seg0, padded], axis=-1).reshape(-1).astype(jnp.int32)

    return pl.pallas_call(
        functools.partial(_moe_kernel, blk=blk, slots=slots, tile_off=tile_off),
        out_shape=jax.ShapeDtypeStruct((T * SUBLANES, LANES), F32),
        grid_spec=pltpu.PrefetchScalarGridSpec(
            num_scalar_prefetch=2,
            grid=(n_tiles, N_EXPERTS // MOE_EXPERTS_PER_STEP),
            in_specs=[
                pl.BlockSpec((tt * SUBLANES, LANES), lambda t, e, *_: (t + tile_off, 0)),
                pl.BlockSpec((1, slots, 1), lambda t, e, *_: (t + tile_off, 0, 0)),
                pl.BlockSpec((MOE_EXPERTS_PER_STEP, D, EXPERT_FF), lambda t, e, *_: (e, 0, 0)),
                pl.BlockSpec((MOE_EXPERTS_PER_STEP, D, EXPERT_FF), lambda t, e, *_: (e, 0, 0)),
                pl.BlockSpec((MOE_EXPERTS_PER_STEP, EXPERT_FF, D), lambda t, e, *_: (e, 0, 0)),
            ],
            out_specs=pl.BlockSpec((tt * SUBLANES, LANES), lambda t, e, *_: (t + tile_off, 0)),
            scratch_shapes=[pltpu.VMEM((blk * SUBLANES, LANES), F32),
                            pltpu.VMEM((blk, D), F32),
                            pltpu.VMEM((SUBLANES * SUBLANES, LANES), F32)],
        ),
        compiler_params=_cparams(("parallel", "arbitrary")),
        name="moe_experts",
    )(tab, slot_tok.reshape(-1), h2, slot_w.reshape(n_tiles_all, slots, 1),
      w["w_ff1"], w["w_ff3"], w["w_ff2"])


def _final_kernel(x_ref, f_ref, mod_ref, g_ref, o_ref, *, tm):
    xf = x_ref[...] + mod_ref[0, 5:6, :] * _load_token_major(f_ref, tm)
    o_ref[...] = _rms(xf) * g_ref[...]


def _final_call(lay, layer, x1, f, mod, g_final):
    T, D, tm = lay["T"], lay["D"], lay["tm"]
    nct, tpb = lay["nct"], lay["tpb"]
    n_lat = T // tm - nct
    return pl.pallas_call(
        functools.partial(_final_kernel, tm=tm),
        out_shape=jax.ShapeDtypeStruct((n_lat * tm, D), F32),
        grid=(n_lat,),
        in_specs=[pl.BlockSpec((tm, D), lambda i: (i + nct, 0)),
                  pl.BlockSpec((tm * SUBLANES, LANES), lambda i: (i + nct, 0)),
                  pl.BlockSpec((1, 6, D), lambda i: (layer * lay["mod_rows"] + i // tpb, 0, 0)),
                  pl.BlockSpec((1, D), lambda i: (0, 0))],
        out_specs=pl.BlockSpec((tm, D), lambda i: (i, 0)),
        compiler_params=_cparams(("parallel",)),
        name="final_norm",
    )(x1, f, mod, g_final)


def _select_cols(wm, segs, scale=None):
    parts = []
    for k, (start, width) in enumerate(segs):
        if start is None:
            parts.append(jnp.zeros((wm.shape[0], width), wm.dtype))
        else:
            blk = wm[:, start:start + width]
            parts.append(blk if scale is None or scale[k] is None else blk * scale[k])
    return jnp.concatenate(parts, axis=1)


def _prep_layer(l, p):
    a_cols = A_Q_RANK + A_KV_RANK + A_ROPE
    b_off = a_cols
    c_off = b_off + 512
    d_off = c_off + 768
    qk_scale = HEAD_DIM ** -0.5 * LOG2E
    gqa_q = lambda off: [(off + hh * HEAD_DIM, HEAD_DIM) for hh in _GQA_PERM]
    segs = ([(0, 256), (256, 128), (None, 64), (384, 32), (None, 32)]
            + gqa_q(b_off) + [(b_off + 256, 128), (b_off + 384, 128)]
            + [(c_off, 256), (c_off + 256, 256), (c_off + 512, 256)]
            + gqa_q(d_off) + [(d_off + 256, 128), (d_off + 384, 128)])
    scale = [None] * len(segs)
    for k in (5, 6, 7, 8, 11):
        scale[k] = qk_scale
    w_in = _select_cols(p["w_in"][l], segs, scale).astype(BF16)
    assert w_in.shape[1] == PROJ_COLS

    hq = A_NOPE + A_ROPE
    segs_q = []
    for hh in range(N_HEADS):
        segs_q += [(hh * hq, hq), (None, LANES - hq)]
    w_q_b = _select_cols(p["w_q_b"][l], segs_q).astype(BF16)
    hk = A_NOPE + HEAD_DIM
    segs_k = []
    for hh in range(N_HEADS):
        segs_k += [(hh * hk, A_NOPE), (None, LANES - A_NOPE)]
    segs_k += [(hh * hk + A_NOPE, HEAD_DIM) for hh in range(N_HEADS)]
    w_kv_b = _select_cols(p["w_kv_b"][l], segs_k).astype(BF16)

    wb = p["w_branch"][l]
    perm_rows = lambda m: jnp.concatenate([m[hh * HEAD_DIM:(hh + 1) * HEAD_DIM] for hh in _GQA_PERM], axis=0)
    w_branch = jnp.stack([wb[0], perm_rows(wb[1]), wb[2], perm_rows(wb[3])]).astype(BF16)

    d = p["w_in"].shape[1]
    w_route = jnp.zeros((d, LANES), F32)
    w_route = w_route.at[:, :N_EXPERTS].set(p["w_router"][l]).at[:, N_EXPERTS:N_EXPERTS + N_GROUPS].set(p["w_group"][l])
    b_route = jnp.zeros((1, LANES), F32)
    b_route = b_route.at[0, :N_EXPERTS].set(p["b_router"][l]).at[0, N_EXPERTS:N_EXPERTS + N_GROUPS].set(p["b_group"][l])
    return {
        "g_mix": p["g_norm_mix"][l][None, :],
        "w_in": w_in,
        "g_q_a": p["g_q_a"][l][None, :],
        "w_q_b": w_q_b,
        "g_kv_a": p["g_kv_a"][l][None, :],
        "w_kv_b": w_kv_b,
        "g_q_d": (jnp.tile(p["g_q_d"][l], 2) * qk_scale)[None, :],
        "g_k_d": jnp.tile(p["g_k_d"][l], 2)[None, :],
        "sink": p["sink_b"][l],
        "rpb": p["rpb_c"][l],
        "w_gate": p["w_gate"][l].astype(BF16),
        "b_gate": p["b_gate"][l][:, None, :],
        "w_branch": w_branch,
        "w_out": p["w_out"][l].astype(BF16),
        "g_ffn": p["g_norm_ffn"][l][None, :],
        "w_route": jnp.stack([w_route.astype(BF16),
                              (w_route - w_route.astype(BF16).astype(F32)).astype(BF16)]),
        "b_route": b_route,
        "w_ff1": p["w_ff1"][l].astype(BF16),
        "w_ff3": p["w_ff3"][l].astype(BF16),
        "w_ff2": p["w_ff2"][l].astype(BF16),
    }


def _rope_tables(S, tm):
    t = np.arange(S)
    rows = (t // GRID_W).astype(np.float32)
    cols = (t % GRID_W).astype(np.float32)

    def cs(rot):
        half = rot // 2
        inv = np.float32(ROPE_THETA) ** (-np.arange(0, half, 2, dtype=np.float32) / np.float32(half))
        ar_, ac_ = rows[:, None] * inv, cols[:, None] * inv
        cos = np.concatenate([np.cos(ar_), np.cos(ar_), np.cos(ac_), np.cos(ac_)], axis=-1)
        sin = np.concatenate([-np.sin(ar_), np.sin(ar_), -np.sin(ac_), np.sin(ac_)], axis=-1)
        return cos.astype(np.float32), sin.astype(np.float32)

    cos64, sin64 = cs(HEAD_DIM)
    cos32, sin32 = cs(A_ROPE)
    ones = lambda n: np.ones((S, n), np.float32)
    zeros = lambda n: np.zeros((S, n), np.float32)
    tabs = {
        "cos_h": np.concatenate([cos64, cos64], axis=-1),
        "sin_h": np.concatenate([sin64, sin64], axis=-1),
        "cos_a": np.concatenate([ones(A_NOPE), cos32, ones(LANES - A_NOPE - A_ROPE)], axis=-1),
        "sin_a": np.concatenate([zeros(A_NOPE), sin32, zeros(LANES - A_NOPE - A_ROPE)], axis=-1),
    }
    ident = {"cos_h": 1.0, "sin_h": 0.0, "cos_a": 1.0, "sin_a": 0.0}
    return {k: jnp.asarray(np.concatenate([v, np.full((tm, LANES), ident[k], np.float32)], axis=0))
            for k, v in tabs.items()}


def _layout(B, S, n_ctx, D):
    assert n_ctx == ROW_BLK and S % 1024 == 0 and S // GRID_W >= 3 * (ROW_BLK // GRID_W)
    T = B * (n_ctx + S)
    tm = 512 if (B * n_ctx) % 512 == 0 else 256
    tk_dense = 1024 if (B * n_ctx) % 1024 == 0 else (512 if (B * n_ctx) % 512 == 0 else 256)
    tt = 2048 if (B * n_ctx) % 2048 == 0 else B * n_ctx
    assert S % tt == 0
    return {
        "B": B, "S": S, "D": D, "T": T, "tm": tm,
        "ncb": B * n_ctx // ROW_BLK,
        "lb": S // ROW_BLK,
        "nct": B * n_ctx // tm,
        "tpb": S // tm,
        "tk_dense": tk_dense,
        "tq_dense": tk_dense,
        "tq_band": tm,
        "tt": tt, "moe_blk": 160,
        "mod_rows": 16,
    }


def kernel(x, c, ctx, c_ctx, w_mod, b_mod, g_norm_mix, w_in, g_q_a, w_q_b, g_kv_a, w_kv_b, sink_b, rpb_c,
           g_q_d, g_k_d, w_gate, b_gate, w_branch, w_out, g_norm_ffn, w_group, b_group, w_router, b_router,
           w_ff1, w_ff3, w_ff2, g_final):
    B, S, D = x.shape
    n_ctx = ctx.shape[1]
    depth = w_mod.shape[0]
    lay = _layout(B, S, n_ctx, D)
    params = dict(w_in=w_in, g_norm_mix=g_norm_mix, g_q_a=g_q_a, w_q_b=w_q_b, g_kv_a=g_kv_a, w_kv_b=w_kv_b,
                  sink_b=sink_b, rpb_c=rpb_c, g_q_d=g_q_d, g_k_d=g_k_d, w_gate=w_gate, b_gate=b_gate,
                  w_branch=w_branch, w_out=w_out, g_norm_ffn=g_norm_ffn, w_group=w_group, b_group=b_group,
                  w_router=w_router, b_router=b_router, w_ff1=w_ff1, w_ff3=w_ff3, w_ff2=w_ff2)

    c_all = jnp.zeros((lay["mod_rows"], D), F32).at[:B].set(c).at[B].set(c_ctx)
    mod = _modulation(c_all, w_mod, b_mod).reshape(depth * lay["mod_rows"], 6, D)
    tabs = _rope_tables(S, lay["tm"])
    win_bias = _window_bias(S, lay["tq_band"])
    xf = jnp.concatenate([ctx.reshape(B * n_ctx, D), x.reshape(B * S, D)], axis=0)

    f = None
    for l in range(depth):
        with_ctx = l < depth - 1
        w = _prep_layer(l, params)
        xf, pr = _proj_call(lay, xf, f, mod, mod, l, w, tabs)
        o = {}
        for kind, kl in (("A", "a"), ("B", "b"), ("C", "c"), ("D", "d")):
            q, k, v = pr["q" + kl], pr["k" + kl], pr["v" + kl]
            sink = w["sink"] if kind == "B" else None
            bias = win_bias if kind == "B" else None
            if kind == "C":
                bias = _neighbourhood_bias(w["rpb"], S // GRID_W, lay["tq_band"])
            o[kind] = _attn_latent_call(lay, kind, q, k, v, sink=sink, bias=bias)
            if with_ctx:
                o[kind] = _attn_context_call(lay, kind, q, k, v, o[kind], sink=sink)
        xf, h2, route = _merge_call(lay, l, pr["h"], o, xf, mod, w, with_ctx=with_ctx)
        f = _moe_call(lay, route, h2, w, with_ctx=with_ctx)
    out = _final_call(lay, depth - 1, xf, f, mod, g_final[None, :])
    return out.reshape(B, S, D)
```

```python
import functools

import numpy as np
import jax
import jax.numpy as jnp
from jax import lax
from jax.experimental import pallas as pl
from jax.experimental.pallas import tpu as pltpu

F32 = jnp.float32
BF16 = jnp.bfloat16
HIGHEST = lax.Precision.HIGHEST

GRID_W = 64
ROPE_THETA = 10000.0
EPS = 1e-6
NEG_INF = -1e30
LOG2E = 1.4426950408889634
HEAD_DIM = 64
N_HEADS = 4
BRANCH_W = 256
A_Q_RANK = 256
A_KV_RANK = 128
A_NOPE = 64
A_ROPE = 32
NA_KH = 8
NA_KW = 16
WINDOW = 128
N_GROUPS = 4
EXPERTS_PER_GROUP = 8
N_EXPERTS = 32
EXPERT_FF = 256

LANES = 128
SUBLANES = 8
ROW_BLK = 256
MERGE_TN = 256
MOE_EXPERTS_PER_STEP = 2
ACC_ONES = 16
ACC_ROWS = HEAD_DIM + ACC_ONES
VMEM_LIMIT = 56 * 1024 * 1024

_PROJ_GROUPS = (("cq", 256), ("ckv", 128), ("kr", 128), ("qb", 256), ("kb", 128), ("vb", 128),
                ("qc", 256), ("kc", 256), ("vc", 256), ("qd", 256), ("kd", 128), ("vd", 128))
_PROJ_OFF = {}
_o = 0
for _n, _w in _PROJ_GROUPS:
    _PROJ_OFF[_n] = (_o, _w)
    _o += _w
PROJ_COLS = _o
_GQA_PERM = (0, 2, 1, 3)


def _cparams(sem):
    return pltpu.CompilerParams(dimension_semantics=sem, vmem_limit_bytes=VMEM_LIMIT)


def _lane_iota(shape):
    return lax.broadcasted_iota(jnp.int32, shape, len(shape) - 1)


def _sigmoid(x):
    return 1.0 / (1.0 + jnp.exp(-x))


def _mod_kernel(c_ref, w_ref, b_ref, o_ref):
    cf = c_ref[...]
    s = cf * _sigmoid(cf)
    o_ref[0] = jnp.dot(s, w_ref[0], precision=HIGHEST, preferred_element_type=F32) + b_ref[0]


def _modulation(c_all, w_mod, b_mod):
    n_layers, d, n_out = w_mod.shape
    rows = c_all.shape[0]
    tn = 1536
    return pl.pallas_call(
        _mod_kernel,
        out_shape=jax.ShapeDtypeStruct((n_layers, rows, n_out), F32),
        grid=(n_layers, n_out // tn),
        in_specs=[pl.BlockSpec((rows, d), lambda l, j: (0, 0)),
                  pl.BlockSpec((1, d, tn), lambda l, j: (l, 0, j)),
                  pl.BlockSpec((1, 1, tn), lambda l, j: (l, 0, j))],
        out_specs=pl.BlockSpec((1, rows, tn), lambda l, j: (l, 0, j)),
        compiler_params=_cparams(("arbitrary", "arbitrary")),
        name="modulation",
    )(c_all, w_mod, b_mod.reshape(n_layers, 1, n_out))


def _rms(x):
    return x * lax.rsqrt(jnp.mean(x * x, axis=-1, keepdims=True) + EPS)


def _swap_blocks(x, blk):
    lane = _lane_iota(x.shape)
    up = pltpu.roll(x, LANES - blk, 1)
    dn = pltpu.roll(x, blk, 1)
    return jnp.where((lane // blk) % 2 == 0, up, dn)


def _rope(x, cos, sin, blk):
    return x * cos + _swap_blocks(x, blk) * sin


def _pair_norm(x, g):
    lo = _lane_iota(x.shape) < HEAD_DIM
    sq = x * x
    s_lo = jnp.sum(jnp.where(lo, sq, 0.0), axis=-1, keepdims=True)
    s_hi = jnp.sum(jnp.where(lo, 0.0, sq), axis=-1, keepdims=True)
    ms = jnp.where(lo, s_lo, s_hi) * (1.0 / HEAD_DIM)
    return x * lax.rsqrt(ms + EPS) * g


def _load_token_major(ref, rows):
    return jnp.concatenate(
        [ref[pl.ds(c, rows, stride=SUBLANES), :] for c in range(SUBLANES)], axis=-1)


def _proj_kernel(*refs, with_f, tm, scale_a):
    it = iter(refs)
    x_ref = next(it)
    if with_f:
        f_ref = next(it)
        modp_ref = next(it)
    mod_ref = next(it)
    gmix_ref, win_ref, gqa_ref, wqb_ref, gkva_ref, wkvb_ref, gqd_ref, gkd_ref = (next(it) for _ in range(8))
    cosh_ref, sinh_ref, cosa_ref, sina_ref = (next(it) for _ in range(4))
    if with_f:
        x2_ref = next(it)
    h_ref = next(it)
    qa_ref, ka_ref, va_ref, qb_ref, kb_ref, vb_ref, qc_ref, kc_ref, vc_ref, qd_ref, kd_ref, vd_ref = (
        next(it) for _ in range(12))
    vta_ref, vtd_ref = next(it), next(it)

    xf = x_ref[...]
    if with_f:
        xf = xf + modp_ref[0, 5:6, :] * _load_token_major(f_ref, tm)
        x2_ref[...] = xf
    h = _rms(xf) * gmix_ref[...] * (1.0 + mod_ref[0, 1:2, :]) + mod_ref[0, 0:1, :]
    hb = h.astype(BF16)
    h_ref[...] = hb
    p = jnp.dot(hb, win_ref[...], preferred_element_type=F32)

    def grp(name):
        o, w = _PROJ_OFF[name]
        return p[:, o:o + w]

    cosh, sinh = cosh_ref[...], sinh_ref[...]
    cosa, sina = cosa_ref[...], sina_ref[...]

    cq = (_rms(grp("cq")) * gqa_ref[...]).astype(BF16)
    qa = jnp.dot(cq, wqb_ref[...], preferred_element_type=F32)
    for hd in range(N_HEADS):
        sl = slice(hd * LANES, (hd + 1) * LANES)
        qa_ref[:, sl] = (_rope(qa[:, sl], cosa, sina, 8) * scale_a).astype(BF16)
    ckv = (_rms(grp("ckv")) * gkva_ref[...]).astype(BF16)
    kva = jnp.dot(ckv, wkvb_ref[...], preferred_element_type=F32)
    kr = _rope(grp("kr"), cosa, sina, 8)
    for hd in range(N_HEADS):
        sl = slice(hd * LANES, (hd + 1) * LANES)
        ka_ref[:, sl] = (kva[:, sl] + kr).astype(BF16)
    va = kva[:, N_HEADS * LANES:]
    va_ref[...] = va.astype(BF16)
    for j in range(2):
        vta_ref[j * LANES:(j + 1) * LANES, :] = va[:, j * LANES:(j + 1) * LANES].T.astype(BF16)

    qb = grp("qb")
    for j in range(2):
        sl = slice(j * LANES, (j + 1) * LANES)
        qb_ref[:, sl] = _rope(qb[:, sl], cosh, sinh, 16).astype(BF16)
    kb_ref[...] = _rope(grp("kb"), cosh, sinh, 16).astype(BF16)
    vb_ref[...] = grp("vb").astype(BF16)

    qc_ref[...] = grp("qc").astype(BF16)
    kc_ref[...] = grp("kc").astype(BF16)
    vc_ref[...] = grp("vc").astype(BF16)

    qd = grp("qd")
    for j in range(2):
        sl = slice(j * LANES, (j + 1) * LANES)
        qd_ref[:, sl] = _rope(_pair_norm(qd[:, sl], gqd_ref[...]), cosh, sinh, 16).astype(BF16)
    kd_ref[...] = _rope(_pair_norm(grp("kd"), gkd_ref[...]), cosh, sinh, 16).astype(BF16)
    vd = grp("vd")
    vd_ref[...] = vd.astype(BF16)
    vtd_ref[...] = vd.T.astype(BF16)


def _proj_call(lay, x, f, modp, mod, layer, w, tabs):
    T, D, tm = lay["T"], lay["D"], lay["tm"]
    nct, tpb, B = lay["nct"], lay["tpb"], lay["B"]
    with_f = f is not None
    n_tiles = T // tm

    def mod_row(i):
        return jnp.where(i < nct, B, (i - nct) // tpb)

    def tab_blk(i):
        return jnp.where(i < nct, tpb, (i - nct) % tpb)

    row = lambda i: (i, 0)
    const = lambda i: (0, 0)
    in_specs = [pl.BlockSpec((tm, D), row)]
    args = [x]
    if with_f:
        in_specs += [pl.BlockSpec((tm * SUBLANES, LANES), row),
                     pl.BlockSpec((1, 6, D), lambda i: ((layer - 1) * lay["mod_rows"] + mod_row(i), 0, 0))]
        args += [f, modp]
    in_specs += [pl.BlockSpec((1, 6, D), lambda i: (layer * lay["mod_rows"] + mod_row(i), 0, 0))]
    args += [mod]
    for name in ("g_mix", "w_in", "g_q_a", "w_q_b", "g_kv_a", "w_kv_b", "g_q_d", "g_k_d"):
        a = w[name]
        in_specs.append(pl.BlockSpec(a.shape, const))
        args.append(a)
    for tname in ("cos_h", "sin_h", "cos_a", "sin_a"):
        in_specs.append(pl.BlockSpec((tm, LANES), lambda i: (tab_blk(i), 0)))
        args.append(tabs[tname])

    widths = [("h", D), ("qa", 512), ("ka", 512), ("va", 256), ("qb", 256), ("kb", 128), ("vb", 128),
              ("qc", 256), ("kc", 256), ("vc", 256), ("qd", 256), ("kd", 128), ("vd", 128)]
    out_shape, out_specs = [], []
    if with_f:
        out_shape.append(jax.ShapeDtypeStruct((T, D), F32))
        out_specs.append(pl.BlockSpec((tm, D), row))
    for _, wd in widths:
        out_shape.append(jax.ShapeDtypeStruct((T, wd), BF16))
        out_specs.append(pl.BlockSpec((tm, wd), row))
    transposed = [("vta", 256), ("vtd", 128)]
    for _, wd in transposed:
        out_shape.append(jax.ShapeDtypeStruct((wd, T), BF16))
        out_specs.append(pl.BlockSpec((wd, tm), lambda i: (0, i)))

    outs = pl.pallas_call(
        functools.partial(_proj_kernel, with_f=with_f, tm=tm,
                          scale_a=float((A_NOPE + A_ROPE) ** -0.5 * LOG2E)),
        out_shape=out_shape, grid=(n_tiles,), in_specs=in_specs, out_specs=out_specs,
        compiler_params=_cparams(("parallel",)),
        name="proj_in",
    )(*args)
    outs = list(outs)
    x2 = outs.pop(0) if with_f else x
    names = [n for n, _ in widths] + [n for n, _ in transposed]
    return x2, dict(zip(names, outs))


_NT = (((1,), (1,)), ((), ()))


def _head_plan(kind):
    if kind == "A":
        return tuple((r, None, r, r // 2) for r in range(N_HEADS))
    if kind == "C":
        return tuple((r // 2, r % 2, r // 2, r // 2) for r in range(N_HEADS))
    return tuple((r // 2, r % 2, 0, 0) for r in range(N_HEADS))


def _head_query(q_ref, plan_r):
    qt, half, _, _ = plan_r
    src = q_ref[:, qt * LANES:(qt + 1) * LANES]
    if half is None:
        return src
    lane = _lane_iota(src.shape)
    keep = (lane < HEAD_DIM) if half == 0 else (lane >= HEAD_DIM)
    return jnp.where(keep, src, jnp.zeros_like(src))


def _score_chunks(q, k_blocks, bias_blocks):
    chunks = []
    for kb, bb in zip(k_blocks, bias_blocks):
        s = lax.dot_general(q, kb, _NT, preferred_element_type=F32)
        if bb is not None:
            s = s + bb
        chunks += [s[:, c * LANES:(c + 1) * LANES] for c in range(s.shape[1] // LANES)]
    return chunks


def _row_max(chunks):
    m = functools.reduce(jnp.maximum, chunks)
    return jnp.broadcast_to(jnp.max(m, axis=-1, keepdims=True), m.shape)


def _weighted_values(p_chunks, v_blocks, half):
    pv, idx = None, 0
    for vb in v_blocks:
        n = vb.shape[0] // LANES
        p = jnp.concatenate(p_chunks[idx:idx + n], axis=1).astype(BF16)
        idx += n
        lane = _lane_iota(vb.shape)
        own = (lane < HEAD_DIM) if half == 0 else (lane >= HEAD_DIM)
        d = jnp.dot(p, jnp.where(own, vb, jnp.ones_like(vb)), preferred_element_type=F32)
        pv = d if pv is None else pv + d
    return pv


def _softmax_once(q, k_blocks, v_blocks, bias_blocks, sink, half):
    chunks = _score_chunks(q, k_blocks, bias_blocks)
    m = _row_max(chunks)
    if sink is not None:
        m = jnp.maximum(m, sink)
    p = [jnp.exp2(c - m) for c in chunks]
    pv = _weighted_values(p, v_blocks, half)
    l = pltpu.roll(pv, HEAD_DIM, 1)
    if sink is not None:
        l = l + jnp.exp2(sink - m)
    return pv / l


def _store_heads(o_ref, outs):
    lane = _lane_iota(outs[0].shape)
    for g in range(2):
        o_ref[:, g * LANES:(g + 1) * LANES] = jnp.where(
            lane < HEAD_DIM, outs[2 * g], outs[2 * g + 1]).astype(o_ref.dtype)


def _attn_band_kernel(*refs, kind, lb, n_kv, tq):
    it = iter(refs)
    q_ref = next(it)
    k_refs = [next(it) for _ in range(n_kv)]
    v_refs = [next(it) for _ in range(n_kv)]
    bias_ref = next(it) if n_kv > 1 else None
    sink_ref = next(it) if kind == "B" else None
    if n_kv == 1:
        next(it)
    o_ref = next(it)
    t = pl.program_id(0) % lb
    var = jnp.where(t == 0, 0, jnp.where(t == lb - 1, 2, 1))
    outs = []
    for r, plan_r in enumerate(_head_plan(kind)):
        _, _, kt, vt = plan_r
        ksl = slice(kt * LANES, (kt + 1) * LANES)
        vsl = slice(vt * LANES, (vt + 1) * LANES)
        bias = [None]
        for j in range(n_kv - 1):
            if kind == "C":
                bias.append(bias_ref[var, j, r * tq:(r + 1) * tq, :])
            else:
                bias.append(bias_ref[var, j])
        sink = sink_ref[_GQA_PERM[r]] * LOG2E if kind == "B" else None
        outs.append(_softmax_once(_head_query(q_ref, plan_r), [k[:, ksl] for k in k_refs],
                                  [v[:, vsl] for v in v_refs], bias, sink, r % 2))
    _store_heads(o_ref, outs)


def _attn_dense_kernel(q_ref, kc_ref, vtc_ref, kl_ref, vtl_ref, o_ref, qs_ref, m_ref, acc_ref,
                       *, kind, n_steps, tq):
    s = pl.program_id(1)
    plan = _head_plan(kind)

    def update(first):
        for r, (_, _, kt, vt) in enumerate(plan):
            rows = slice(r * tq, (r + 1) * tq)
            arows = slice(r * ACC_ROWS, (r + 1) * ACC_ROWS)
            mrow = slice(r * SUBLANES, r * SUBLANES + 1)
            ksl = slice(kt * LANES, (kt + 1) * LANES)
            v0 = vt * LANES + (r % 2) * HEAD_DIM
            if first:
                q = _head_query(q_ref, plan[r])
                qs_ref[rows, :] = q
                blocks = [(kc_ref, vtc_ref), (kl_ref, vtl_ref)]
            else:
                q = qs_ref[rows, :]
                blocks = [(kl_ref, vtl_ref)]
            st = [lax.dot_general(kr[:, ksl], q, _NT, preferred_element_type=F32) for kr, _ in blocks]
            m_cur = functools.reduce(jnp.maximum, [jnp.max(x, axis=0, keepdims=True) for x in st])
            if first:
                m_new = m_cur
            else:
                m_prev = m_ref[mrow, :]
                m_new = jnp.maximum(m_prev, m_cur)
            pv = None
            for x, (_, vtr) in zip(st, blocks):
                p = jnp.exp2(x - m_new).astype(BF16)
                vt_aug = jnp.concatenate(
                    [vtr[v0:v0 + HEAD_DIM, :], jnp.ones((ACC_ONES, x.shape[0]), BF16)], axis=0)
                d = jnp.dot(vt_aug, p, preferred_element_type=F32)
                pv = d if pv is None else pv + d
            if first:
                acc_ref[arows, :] = pv
            else:
                acc_ref[arows, :] = jnp.exp2(m_prev - m_new) * acc_ref[arows, :] + pv
            m_ref[mrow, :] = m_new

    @pl.when(s == 0)
    def _():
        update(True)

    @pl.when(s > 0)
    def _():
        update(False)

    @pl.when(s == n_steps - 1)
    def _():
        for g in range(2):
            halves = []
            for r in (2 * g, 2 * g + 1):
                acc = acc_ref[r * ACC_ROWS:(r + 1) * ACC_ROWS, :]
                halves.append(acc[:HEAD_DIM, :] / acc[HEAD_DIM:HEAD_DIM + 1, :])
            pair = jnp.concatenate(halves, axis=0)
            o_ref[:, g * LANES:(g + 1) * LANES] = pair.T.astype(o_ref.dtype)


def _attn_latent_call(lay, kind, q, k, v, *, vt=None, sink=None, bias=None):
    T, B, ncb, lb, S = lay["T"], lay["B"], lay["ncb"], lay["lb"], lay["S"]
    tq = ROW_BLK
    qw, kw, vw = q.shape[1], k.shape[1], v.shape[1]
    out_shape = jax.ShapeDtypeStruct((T, BRANCH_W), BF16)
    if kind in ("A", "D"):
        tk, tq = lay["tk_dense"], lay["tq_dense"]
        n_steps = S // tk
        lat0 = ncb * ROW_BLK // tk
        q0 = ncb * ROW_BLK // tq
        qpb = S // tq
        lat_blk = lambda i, s: (lat0 + (i // qpb) * n_steps + s, 0)
        lat_blk_t = lambda i, s: (0, lat0 + (i // qpb) * n_steps + s)
        return pl.pallas_call(
            functools.partial(_attn_dense_kernel, kind=kind, n_steps=n_steps, tq=tq),
            out_shape=out_shape,
            grid=(B * qpb, n_steps),
            in_specs=[pl.BlockSpec((tq, qw), lambda i, s: (q0 + i, 0)),
                      pl.BlockSpec((ROW_BLK, kw), lambda i, s: (i // qpb, 0)),
                      pl.BlockSpec((vw, ROW_BLK), lambda i, s: (0, i // qpb)),
                      pl.BlockSpec((tk, kw), lat_blk),
                      pl.BlockSpec((vw, tk), lat_blk_t)],
            out_specs=pl.BlockSpec((tq, BRANCH_W), lambda i, s: (q0 + i, 0)),
            scratch_shapes=[pltpu.VMEM((N_HEADS * tq, LANES), BF16),
                            pltpu.VMEM((N_HEADS * SUBLANES, tq), F32),
                            pltpu.VMEM((N_HEADS * ACC_ROWS, tq), F32)],
            compiler_params=_cparams(("parallel", "arbitrary")),
            name="attn_" + kind,
        )(q, k, vt, k, vt)

    tq = lay["tq_band"]
    qpb = S // tq
    q0 = ncb * ROW_BLK // tq
    bpq = tq // ROW_BLK

    def nb(i, d):
        return (ncb + (i // qpb) * lb + jnp.clip((i % qpb) * bpq + d, 0, lb - 1), 0)

    kv_maps = [lambda i: (i // qpb, 0)] + [functools.partial(nb, d=d) for d in _band_offsets(tq)]
    n_kv = len(kv_maps)
    in_specs = [pl.BlockSpec((tq, qw), lambda i: (q0 + i, 0))]
    in_specs += [pl.BlockSpec((ROW_BLK, kw), m) for m in kv_maps]
    in_specs += [pl.BlockSpec((ROW_BLK, vw), m) for m in kv_maps]
    in_specs.append(pl.BlockSpec(bias.shape, lambda i: (0,) * bias.ndim, pipeline_mode=pl.Buffered(1)))
    args = [q] + [k] * n_kv + [v] * n_kv + [bias]
    if kind == "B":
        in_specs.append(pl.BlockSpec(memory_space=pltpu.SMEM))
        args.append(sink)
    return pl.pallas_call(
        functools.partial(_attn_band_kernel, kind=kind, lb=qpb, n_kv=n_kv, tq=tq),
        out_shape=out_shape,
        grid=(B * qpb,),
        in_specs=in_specs,
        out_specs=pl.BlockSpec((tq, BRANCH_W), lambda i: (q0 + i, 0)),
        compiler_params=_cparams(("parallel",)),
        name="attn_" + kind,
    )(*args)


def _attn_context_call(lay, kind, q, k, v, o, *, sink=None):
    ncb = lay["ncb"]
    qw, kw, vw = q.shape[1], k.shape[1], v.shape[1]
    blk = lambda i: (i, 0)
    in_specs = [pl.BlockSpec((ROW_BLK, qw), blk), pl.BlockSpec((ROW_BLK, kw), blk),
                pl.BlockSpec((ROW_BLK, vw), blk)]
    args = [q, k, v]
    if kind == "B":
        in_specs.append(pl.BlockSpec(memory_space=pltpu.SMEM))
        args.append(sink)
    in_specs.append(pl.BlockSpec(memory_space=pl.ANY))
    args.append(o)
    return pl.pallas_call(
        functools.partial(_attn_band_kernel, kind=kind, lb=1, n_kv=1, tq=ROW_BLK),
        out_shape=jax.ShapeDtypeStruct(o.shape, o.dtype),
        grid=(ncb,),
        in_specs=in_specs,
        out_specs=pl.BlockSpec((ROW_BLK, BRANCH_W), blk),
        input_output_aliases={len(args) - 1: 0},
        compiler_params=_cparams(("parallel",)),
        name="attn_ctx_" + kind,
    )(*args)


def _band_offsets(tq):
    return tuple(range(-1, tq // ROW_BLK + 1))


def _window_bias(S, tq):
    lb, qpb, bpq = S // ROW_BLK, S // tq, tq // ROW_BLK
    offs = _band_offsets(tq)
    qa, ka = np.arange(tq), np.arange(ROW_BLK)
    out = np.full((3, len(offs), tq, ROW_BLK), NEG_INF, np.float32)
    for vi, t_rep in enumerate((0, 1, qpb - 1)):
        for di, d in enumerate(offs):
            kt = t_rep * bpq + d
            if not 0 <= kt < lb:
                continue
            qpos = t_rep * tq + qa
            kpos = kt * ROW_BLK + ka
            ok = np.abs(qpos[:, None] - kpos[None, :]) <= WINDOW
            out[vi, di] = np.where(ok, 0.0, NEG_INF)
    return jnp.asarray(out)


def _neighbourhood_bias(rpb, rows_total, tq):
    lb = rows_total * GRID_W // ROW_BLK
    qpb, bpq = rows_total * GRID_W // tq, tq // ROW_BLK
    rpt = ROW_BLK // GRID_W
    rpq = tq // GRID_W
    kh = min(NA_KH, rows_total)
    qa, ka = np.arange(tq), np.arange(ROW_BLK)
    q_sub, q_col = qa // GRID_W, qa % GRID_W
    k_sub, k_col = ka // GRID_W, ka % GRID_W
    n_dr, n_dc = 2 * NA_KH - 1, 2 * NA_KW - 1
    col = np.arange(GRID_W)
    dc = np.clip(col[None, :] - col[:, None], -(NA_KW - 1), NA_KW - 1) + NA_KW - 1
    hot_c = (dc[:, :, None] == np.arange(n_dc)).astype(np.float32)
    by_col = jnp.einsum("huv,cdv->hucd", rpb.astype(F32) * LOG2E, jnp.asarray(hot_c),
                        precision=HIGHEST)
    offs = _band_offsets(tq)
    vals = []
    for d in offs:
        dr = np.clip(d * rpt + np.arange(rpt)[None, :] - np.arange(rpq)[:, None],
                     -(NA_KH - 1), NA_KH - 1) + NA_KH - 1
        hot_r = (dr[:, :, None] == np.arange(n_dr)).astype(np.float32)
        v = jnp.einsum("abu,hucd->hacbd", jnp.asarray(hot_r), by_col, precision=HIGHEST)
        vals.append(v.reshape(N_HEADS, tq, ROW_BLK))
    out = []
    for t_rep in (0, 1, qpb - 1):
        per_block = []
        for di, d in enumerate(offs):
            kt = t_rep * bpq + d
            q_row = t_rep * rpq + q_sub
            k_row = kt * rpt + k_sub
            r_start = np.clip(q_row - kh // 2, 0, rows_total - kh)
            row_ok = (k_row[None] >= r_start[:, None]) & (k_row[None] < r_start[:, None] + kh)
            c_start = np.clip(q_col - NA_KW // 2, 0, GRID_W - NA_KW)
            col_ok = (k_col[None] >= c_start[:, None]) & (k_col[None] < c_start[:, None] + NA_KW)
            ok = row_ok & col_ok & (0 <= kt < lb)
            per_block.append(jnp.where(jnp.asarray(ok)[None], vals[di], NEG_INF).reshape(-1, ROW_BLK))
        out.append(jnp.stack(per_block))
    return jnp.stack(out)


def _merge_kernel(h_ref, oa_ref, ob_ref, oc_ref, od_ref, x_ref, mod_ref, gffn_ref, wg_ref, bg_ref,
                  wb_ref, wout_ref, wr_ref, br_ref, x1_ref, h2_ref, route_ref, y_ref, *, tm):
    hb = h_ref[...]
    d_model = hb.shape[1]
    for t in range(d_model // MERGE_TN):
        cs = slice(t * MERGE_TN, (t + 1) * MERGE_TN)
        y = None
        for n, o_ref in enumerate((oa_ref, ob_ref, oc_ref, od_ref)):
            gate = _sigmoid(jnp.dot(hb, wg_ref[n, :, cs], preferred_element_type=F32) + bg_ref[n, :, cs])
            u = gate * jnp.dot(o_ref[...], wb_ref[n, :, cs], preferred_element_type=F32)
            y = u if y is None else y + u
        y_ref[:, cs] = y.astype(BF16)
    z = jnp.dot(y_ref[...], wout_ref[...], preferred_element_type=F32)
    x1 = x_ref[...] + mod_ref[0, 2:3, :] * z
    x1_ref[...] = x1
    h2 = _rms(x1) * gffn_ref[...] * (1.0 + mod_ref[0, 4:5, :]) + mod_ref[0, 3:4, :]
    for c in range(SUBLANES):
        h2_ref[pl.ds(c, tm, stride=SUBLANES), :] = h2[:, c * LANES:(c + 1) * LANES]

    h2_hi = h2.astype(BF16)
    h2_lo = (h2 - h2_hi.astype(F32)).astype(BF16)
    logit = (jnp.dot(h2_hi, wr_ref[0], preferred_element_type=F32)
             + jnp.dot(h2_lo, wr_ref[0], preferred_element_type=F32)
             + jnp.dot(h2_hi, wr_ref[1], preferred_element_type=F32)) + br_ref[...]
    lane = _lane_iota(logit.shape)
    big = jnp.int32(1 << 20)
    is_g = (lane >= N_EXPERTS) & (lane < N_EXPERTS + N_GROUPS)
    gl = jnp.where(is_g, logit, NEG_INF)
    gmax = jnp.max(gl, axis=-1, keepdims=True)
    gsel = jnp.min(jnp.where(gl == gmax, lane - N_EXPERTS, big), axis=-1, keepdims=True)
    gw = 1.0 / jnp.sum(jnp.where(is_g, jnp.exp(gl - gmax), 0.0), axis=-1, keepdims=True)
    in_grp = (lane < N_EXPERTS) & ((lane // EXPERTS_PER_GROUP) == gsel)
    el = jnp.where(in_grp, logit, NEG_INF)
    v1 = jnp.max(el, axis=-1, keepdims=True)
    i1 = jnp.min(jnp.where(el == v1, lane, big), axis=-1, keepdims=True)
    el2 = jnp.where(lane == i1, NEG_INF, el)
    v2 = jnp.max(el2, axis=-1, keepdims=True)
    i2 = jnp.min(jnp.where(el2 == v2, lane, big), axis=-1, keepdims=True)
    e21 = jnp.exp(v2 - v1)
    w1 = gw / (1.0 + e21)
    w2 = gw * e21 / (1.0 + e21)
    route_ref[...] = jnp.where(lane == 0, i1.astype(F32),
                               jnp.where(lane == 1, i2.astype(F32),
                                         jnp.where(lane == 2, w1, jnp.where(lane == 3, w2, 0.0))))


def _merge_call(lay, layer, h, o, x, mod, w, *, with_ctx):
    T, D, tm = lay["T"], lay["D"], lay["tm"]
    nct, tpb, B = lay["nct"], lay["tpb"], lay["B"]
    off = 0 if with_ctx else nct
    n_tiles = T // tm - off

    def mod_row(i):
        return jnp.where(i < nct, B, (i - nct) // tpb)

    row = lambda i: (i + off, 0)
    const2 = lambda i: (0, 0)
    const3 = lambda i: (0, 0, 0)
    in_specs = [pl.BlockSpec((tm, D), row)]
    in_specs += [pl.BlockSpec((tm, BRANCH_W), row)] * 4
    in_specs += [pl.BlockSpec((tm, D), row),
                 pl.BlockSpec((1, 6, D), lambda i: (layer * lay["mod_rows"] + mod_row(i + off), 0, 0)),
                 pl.BlockSpec((1, D), const2),
                 pl.BlockSpec(w["w_gate"].shape, const3),
                 pl.BlockSpec(w["b_gate"].shape, const3),
                 pl.BlockSpec(w["w_branch"].shape, const3),
                 pl.BlockSpec((D, D), const2),
                 pl.BlockSpec((2, D, LANES), const3),
                 pl.BlockSpec((1, LANES), const2)]
    out_shape = [jax.ShapeDtypeStruct((T, D), F32),
                 jax.ShapeDtypeStruct((T * SUBLANES, LANES), F32),
                 jax.ShapeDtypeStruct((T, LANES), F32)]
    out_specs = [pl.BlockSpec((tm, D), row),
                 pl.BlockSpec((tm * SUBLANES, LANES), row),
                 pl.BlockSpec((tm, LANES), row)]
    return pl.pallas_call(
        functools.partial(_merge_kernel, tm=tm),
        out_shape=out_shape, grid=(n_tiles,), in_specs=in_specs, out_specs=out_specs,
        scratch_shapes=[pltpu.VMEM((tm, D), BF16)],
        compiler_params=_cparams(("parallel",)),
        name="merge",
    )(h, o["A"], o["B"], o["C"], o["D"], x, mod, w["g_ffn"], w["w_gate"], w["b_gate"],
      w["w_branch"], w["w_out"], w["w_route"], w["b_route"])


def _moe_kernel(tab_ref, tok_ref, h2_ref, sw_ref, w1_ref, w3_ref, w2_ref, f_ref,
                xg_ref, y_ref, st_ref, *, blk, slots, tile_off):
    ti = pl.program_id(0)
    step = pl.program_id(1)

    @pl.when(step == 0)
    def _():
        f_ref[...] = jnp.zeros_like(f_ref)
        xg_ref[...] = jnp.zeros_like(xg_ref)

    tok_base = (ti + tile_off) * slots
    for j in range(MOE_EXPERTS_PER_STEP):
        base = ((ti + tile_off) * N_EXPERTS + step * MOE_EXPERTS_PER_STEP + j) * 2
        _moe_expert(tab_ref[base], tab_ref[base + 1], tok_base, tok_ref, h2_ref, sw_ref,
                    w1_ref.at[j], w3_ref.at[j], w2_ref.at[j], f_ref, xg_ref, y_ref, st_ref, blk)


def _moe_expert(seg0, n_pad, tok_base, tok_ref, h2_ref, sw_ref, w1_ref, w3_ref, w2_ref, f_ref,
                xg_ref, y_ref, st_ref, blk):
    n_blocks = (n_pad + blk - 1) // blk

    def block_body(b, carry):
        off = pl.multiple_of(seg0 + b * blk, SUBLANES)
        n_grp = jnp.minimum(n_pad - b * blk, blk) // SUBLANES

        def gather(gi, c):
            for u in range(SUBLANES):
                r = gi * SUBLANES + u
                tok = tok_ref[tok_base + off + r]
                src = pl.multiple_of(tok * SUBLANES, SUBLANES)
                dst = pl.multiple_of(r * SUBLANES, SUBLANES)
                xg_ref[pl.ds(dst, SUBLANES), :] = h2_ref[pl.ds(src, SUBLANES), :]
            return c

        lax.fori_loop(0, n_grp, gather, 0)
        xb = _load_token_major(xg_ref, blk).astype(BF16)
        a = jnp.dot(xb, w1_ref[...], preferred_element_type=F32)
        g = jnp.dot(xb, w3_ref[...], preferred_element_type=F32)
        hid = (a * _sigmoid(a) * g).astype(BF16)
        y = jnp.dot(hid, w2_ref[...], preferred_element_type=F32)
        y_ref[...] = y * sw_ref[0, pl.ds(off, blk), :]

        def scatter(gi, c):
            r0 = pl.multiple_of(gi * SUBLANES, SUBLANES)
            for ch in range(SUBLANES):
                st_ref[pl.ds(ch, SUBLANES, stride=SUBLANES), :] = y_ref[pl.ds(r0, SUBLANES),
                                                                        ch * LANES:(ch + 1) * LANES]
            dsts, vals = [], []
            for u in range(SUBLANES):
                tok = tok_ref[tok_base + off + r0 + u]
                dsts.append(pl.multiple_of(tok * SUBLANES, SUBLANES))
                vals.append(f_ref[pl.ds(dsts[u], SUBLANES), :] + st_ref[u * SUBLANES:(u + 1) * SUBLANES, :])
            for u in reversed(range(SUBLANES)):
                f_ref[pl.ds(dsts[u], SUBLANES), :] = vals[u]
            return c

        lax.fori_loop(0, n_grp, scatter, 0)
        return carry

    lax.fori_loop(0, n_blocks, block_body, 0)


def _moe_call(lay, route, h2, w, *, with_ctx):
    T, D, tt = lay["T"], lay["D"], lay["tt"]
    blk = lay["moe_blk"]
    n_tiles_all = T // tt
    tile_off = 0 if with_ctx else lay["ncb"] * ROW_BLK // tt
    n_tiles = n_tiles_all - tile_off
    n_assign = 2 * tt
    n_fill = N_EXPERTS * SUBLANES
    slots = n_assign + n_fill + blk

    eid = route[:, 0:2].astype(jnp.int32).reshape(n_tiles_all, n_assign)
    wts = route[:, 2:4].reshape(n_tiles_all, n_assign)
    tok = jnp.broadcast_to(jnp.arange(n_assign, dtype=jnp.int32)[None, :] // 2, eid.shape)
    ex = jnp.arange(N_EXPERTS, dtype=jnp.int32)
    counts = jnp.sum(eid[:, :, None] == ex[None, None, :], axis=1, dtype=jnp.int32)
    n_dummy = (-counts) % SUBLANES
    fill_key = jnp.where(jnp.arange(SUBLANES, dtype=jnp.int32)[None, None, :] < n_dummy[:, :, None],
                         ex[None, :, None], N_EXPERTS).reshape(n_tiles_all, n_fill)
    zeros_i = jnp.zeros((n_tiles_all, n_fill), jnp.int32)
    _, slot_tok, slot_w = lax.sort(
        (jnp.concatenate([eid, fill_key], axis=1), jnp.concatenate([tok, zeros_i], axis=1),
         jnp.concatenate([wts, zeros_i.astype(F32)], axis=1)),
        dimension=1, is_stable=True, num_keys=1)
    slot_tok = jnp.pad(slot_tok, ((0, 0), (0, blk)))
    slot_w = jnp.pad(slot_w, ((0, 0), (0, blk)))
    padded = counts + n_dummy
    seg0 = jnp.cumsum(padded, axis=1) - padded
    tab = jnp.stack([seg0, padded], axis=-1).reshape(-1).astype(jnp.int32)

    return pl.pallas_call(
        functools.partial(_moe_kernel, blk=blk, slots=slots, tile_off=tile_off),
        out_shape=jax.ShapeDtypeStruct((T * SUBLANES, LANES), F32),
        grid_spec=pltpu.PrefetchScalarGridSpec(
            num_scalar_prefetch=2,
            grid=(n_tiles, N_EXPERTS // MOE_EXPERTS_PER_STEP),
            in_specs=[
                pl.BlockSpec((tt * SUBLANES, LANES), lambda t, e, *_: (t + tile_off, 0)),
                pl.BlockSpec((1, slots, 1), lambda t, e, *_: (t + tile_off, 0, 0)),
                pl.BlockSpec((MOE_EXPERTS_PER_STEP, D, EXPERT_FF), lambda t, e, *_: (e, 0, 0)),
                pl.BlockSpec((MOE_EXPERTS_PER_STEP, D, EXPERT_FF), lambda t, e, *_: (e, 0, 0)),
                pl.BlockSpec((MOE_EXPERTS_PER_STEP, EXPERT_FF, D), lambda t, e, *_: (e, 0, 0)),
            ],
            out_specs=pl.BlockSpec((tt * SUBLANES, LANES), lambda t, e, *_: (t + tile_off, 0)),
            scratch_shapes=[pltpu.VMEM((blk * SUBLANES, LANES), F32),
                            pltpu.VMEM((blk, D), F32),
                            pltpu.VMEM((SUBLANES * SUBLANES, LANES), F32)],
        ),
        compiler_params=_cparams(("parallel", "arbitrary")),
        name="moe_experts",
    )(tab, slot_tok.reshape(-1), h2, slot_w.reshape(n_tiles_all, slots, 1),
      w["w_ff1"], w["w_ff3"], w["w_ff2"])


def _final_kernel(x_ref, f_ref, mod_ref, g_ref, o_ref, *, tm):
    xf = x_ref[...] + mod_ref[0, 5:6, :] * _load_token_major(f_ref, tm)
    o_ref[...] = _rms(xf) * g_ref[...]


def _final_call(lay, layer, x1, f, mod, g_final):
    T, D, tm = lay["T"], lay["D"], lay["tm"]
    nct, tpb = lay["nct"], lay["tpb"]
    n_lat = T // tm - nct
    return pl.pallas_call(
        functools.partial(_final_kernel, tm=tm),
        out_shape=jax.ShapeDtypeStruct((n_lat * tm, D), F32),
        grid=(n_lat,),
        in_specs=[pl.BlockSpec((tm, D), lambda i: (i + nct, 0)),
                  pl.BlockSpec((tm * SUBLANES, LANES), lambda i: (i + nct, 0)),
                  pl.BlockSpec((1, 6, D), lambda i: (layer * lay["mod_rows"] + i // tpb, 0, 0)),
                  pl.BlockSpec((1, D), lambda i: (0, 0))],
        out_specs=pl.BlockSpec((tm, D), lambda i: (i, 0)),
        compiler_params=_cparams(("parallel",)),
        name="final_norm",
    )(x1, f, mod, g_final)


def _select_cols(wm, segs, scale=None):
    parts = []
    for k, (start, width) in enumerate(segs):
        if start is None:
            parts.append(jnp.zeros((wm.shape[0], width), wm.dtype))
        else:
            blk = wm[:, start:start + width]
            parts.append(blk if scale is None or scale[k] is None else blk * scale[k])
    return jnp.concatenate(parts, axis=1)


def _prep_layer(l, p):
    a_cols = A_Q_RANK + A_KV_RANK + A_ROPE
    b_off = a_cols
    c_off = b_off + 512
    d_off = c_off + 768
    qk_scale = HEAD_DIM ** -0.5 * LOG2E
    gqa_q = lambda off: [(off + hh * HEAD_DIM, HEAD_DIM) for hh in _GQA_PERM]
    segs = ([(0, 256), (256, 128), (None, 64), (384, 32), (None, 32)]
            + gqa_q(b_off) + [(b_off + 256, 128), (b_off + 384, 128)]
            + [(c_off, 256), (c_off + 256, 256), (c_off + 512, 256)]
            + gqa_q(d_off) + [(d_off + 256, 128), (d_off + 384, 128)])
    scale = [None] * len(segs)
    for k in (5, 6, 7, 8, 11):
        scale[k] = qk_scale
    w_in = _select_cols(p["w_in"][l], segs, scale).astype(BF16)
    assert w_in.shape[1] == PROJ_COLS

    hq = A_NOPE + A_ROPE
    segs_q = []
    for hh in range(N_HEADS):
        segs_q += [(hh * hq, hq), (None, LANES - hq)]
    w_q_b = _select_cols(p["w_q_b"][l], segs_q).astype(BF16)
    hk = A_NOPE + HEAD_DIM
    segs_k = []
    for hh in range(N_HEADS):
        segs_k += [(hh * hk, A_NOPE), (None, LANES - A_NOPE)]
    segs_k += [(hh * hk + A_NOPE, HEAD_DIM) for hh in range(N_HEADS)]
    w_kv_b = _select_cols(p["w_kv_b"][l], segs_k).astype(BF16)

    wb = p["w_branch"][l]
    perm_rows = lambda m: jnp.concatenate([m[hh * HEAD_DIM:(hh + 1) * HEAD_DIM] for hh in _GQA_PERM], axis=0)
    w_branch = jnp.stack([wb[0], perm_rows(wb[1]), wb[2], perm_rows(wb[3])]).astype(BF16)

    d = p["w_in"].shape[1]
    w_route = jnp.zeros((d, LANES), F32)
    w_route = w_route.at[:, :N_EXPERTS].set(p["w_router"][l]).at[:, N_EXPERTS:N_EXPERTS + N_GROUPS].set(p["w_group"][l])
    b_route = jnp.zeros((1, LANES), F32)
    b_route = b_route.at[0, :N_EXPERTS].set(p["b_router"][l]).at[0, N_EXPERTS:N_EXPERTS + N_GROUPS].set(p["b_group"][l])
    return {
        "g_mix": p["g_norm_mix"][l][None, :],
        "w_in": w_in,
        "g_q_a": p["g_q_a"][l][None, :],
        "w_q_b": w_q_b,
        "g_kv_a": p["g_kv_a"][l][None, :],
        "w_kv_b": w_kv_b,
        "g_q_d": (jnp.tile(p["g_q_d"][l], 2) * qk_scale)[None, :],
        "g_k_d": jnp.tile(p["g_k_d"][l], 2)[None, :],
        "sink": p["sink_b"][l],
        "rpb": p["rpb_c"][l],
        "w_gate": p["w_gate"][l].astype(BF16),
        "b_gate": p["b_gate"][l][:, None, :],
        "w_branch": w_branch,
        "w_out": p["w_out"][l].astype(BF16),
        "g_ffn": p["g_norm_ffn"][l][None, :],
        "w_route": jnp.stack([w_route.astype(BF16),
                              (w_route - w_route.astype(BF16).astype(F32)).astype(BF16)]),
        "b_route": b_route,
        "w_ff1": p["w_ff1"][l].astype(BF16),
        "w_ff3": p["w_ff3"][l].astype(BF16),
        "w_ff2": p["w_ff2"][l].astype(BF16),
    }


def _rope_tables(S, tm):
    t = np.arange(S)
    rows = (t // GRID_W).astype(np.float32)
    cols = (t % GRID_W).astype(np.float32)

    def cs(rot):
        half = rot // 2
        inv = np.float32(ROPE_THETA) ** (-np.arange(0, half, 2, dtype=np.float32) / np.float32(half))
        ar_, ac_ = rows[:, None] * inv, cols[:, None] * inv
        cos = np.concatenate([np.cos(ar_), np.cos(ar_), np.cos(ac_), np.cos(ac_)], axis=-1)
        sin = np.concatenate([-np.sin(ar_), np.sin(ar_), -np.sin(ac_), np.sin(ac_)], axis=-1)
        return cos.astype(np.float32), sin.astype(np.float32)

    cos64, sin64 = cs(HEAD_DIM)
    cos32, sin32 = cs(A_ROPE)
    ones = lambda n: np.ones((S, n), np.float32)
    zeros = lambda n: np.zeros((S, n), np.float32)
    tabs = {
        "cos_h": np.concatenate([cos64, cos64], axis=-1),
        "sin_h": np.concatenate([sin64, sin64], axis=-1),
        "cos_a": np.concatenate([ones(A_NOPE), cos32, ones(LANES - A_NOPE - A_ROPE)], axis=-1),
        "sin_a": np.concatenate([zeros(A_NOPE), sin32, zeros(LANES - A_NOPE - A_ROPE)], axis=-1),
    }
    ident = {"cos_h": 1.0, "sin_h": 0.0, "cos_a": 1.0, "sin_a": 0.0}
    return {k: jnp.asarray(np.concatenate([v, np.full((tm, LANES), ident[k], np.float32)], axis=0))
            for k, v in tabs.items()}


def _layout(B, S, n_ctx, D):
    assert n_ctx == ROW_BLK and S % 1024 == 0 and S // GRID_W >= 3 * (ROW_BLK // GRID_W)
    T = B * (n_ctx + S)
    tm = 512 if (B * n_ctx) % 512 == 0 else 256
    tk_dense = 1024 if (B * n_ctx) % 1024 == 0 else (512 if (B * n_ctx) % 512 == 0 else 256)
    tt = 2048 if (B * n_ctx) % 2048 == 0 else B * n_ctx
    assert S % tt == 0
    return {
        "B": B, "S": S, "D": D, "T": T, "tm": tm,
        "ncb": B * n_ctx // ROW_BLK,
        "lb": S // ROW_BLK,
        "nct": B * n_ctx // tm,
        "tpb": S // tm,
        "tk_dense": tk_dense,
        "tq_dense": tk_dense,
        "tq_band": tm,
        "tt": tt, "moe_blk": 160,
        "mod_rows": 16,
    }


def kernel(x, c, ctx, c_ctx, w_mod, b_mod, g_norm_mix, w_in, g_q_a, w_q_b, g_kv_a, w_kv_b, sink_b, rpb_c,
           g_q_d, g_k_d, w_gate, b_gate, w_branch, w_out, g_norm_ffn, w_group, b_group, w_router, b_router,
           w_ff1, w_ff3, w_ff2, g_final):
    B, S, D = x.shape
    n_ctx = ctx.shape[1]
    depth = w_mod.shape[0]
    lay = _layout(B, S, n_ctx, D)
    params = dict(w_in=w_in, g_norm_mix=g_norm_mix, g_q_a=g_q_a, w_q_b=w_q_b, g_kv_a=g_kv_a, w_kv_b=w_kv_b,
                  sink_b=sink_b, rpb_c=rpb_c, g_q_d=g_q_d, g_k_d=g_k_d, w_gate=w_gate, b_gate=b_gate,
                  w_branch=w_branch, w_out=w_out, g_norm_ffn=g_norm_ffn, w_group=w_group, b_group=b_group,
                  w_router=w_router, b_router=b_router, w_ff1=w_ff1, w_ff3=w_ff3, w_ff2=w_ff2)

    c_all = jnp.zeros((lay["mod_rows"], D), F32).at[:B].set(c).at[B].set(c_ctx)
    mod = _modulation(c_all, w_mod, b_mod).reshape(depth * lay["mod_rows"], 6, D)
    tabs = _rope_tables(S, lay["tm"])
    win_bias = _window_bias(S, lay["tq_band"])
    xf = jnp.concatenate([ctx.reshape(B * n_ctx, D), x.reshape(B * S, D)], axis=0)

    f = None
    for l in range(depth):
        with_ctx = l < depth - 1
        w = _prep_layer(l, params)
        xf, pr = _proj_call(lay, xf, f, mod, mod, l, w, tabs)
        o = {}
        for kind, kl in (("A", "a"), ("B", "b"), ("C", "c"), ("D", "d")):
            q, k, v = pr["q" + kl], pr["k" + kl], pr["v" + kl]
            sink = w["sink"] if kind == "B" else None
            bias = win_bias if kind == "B" else None
            if kind == "C":
                bias = _neighbourhood_bias(w["rpb"], S // GRID_W, lay["tq_band"])
            o[kind] = _attn_latent_call(lay, kind, q, k, v, vt=pr.get("vt" + kl), sink=sink, bias=bias)
            if with_ctx:
                o[kind] = _attn_context_call(lay, kind, q, k, v, o[kind], sink=sink)
        xf, h2, route = _merge_call(lay, l, pr["h"], o, xf, mod, w, with_ctx=with_ctx)
        f = _moe_call(lay, route, h2, w, with_ctx=with_ctx)
    out = _final_call(lay, depth - 1, xf, f, mod, g_final[None, :])
    return out.reshape(B, S, D)
```

```python
import functools

import numpy as np
import jax
import jax.numpy as jnp
from jax import lax
from jax.experimental import pallas as pl
from jax.experimental.pallas import tpu as pltpu

F32 = jnp.float32
BF16 = jnp.bfloat16
HIGHEST = lax.Precision.HIGHEST

GRID_W = 64
ROPE_THETA = 10000.0
EPS = 1e-6
NEG_INF = -1e30
LOG2E = 1.4426950408889634
HEAD_DIM = 64
N_HEADS = 4
BRANCH_W = 256
A_Q_RANK = 256
A_KV_RANK = 128
A_NOPE = 64
A_ROPE = 32
NA_KH = 8
NA_KW = 16
WINDOW = 128
N_GROUPS = 4
EXPERTS_PER_GROUP = 8
N_EXPERTS = 32
EXPERT_FF = 256

LANES = 128
SUBLANES = 8
ROW_BLK = 256
MERGE_TN = 256
MOE_EXPERTS_PER_STEP = 2
VMEM_LIMIT = 56 * 1024 * 1024

_PROJ_GROUPS = (("cq", 256), ("ckv", 128), ("kr", 128), ("qb", 256), ("kb", 128), ("vb", 128),
                ("qc", 256), ("kc", 256), ("vc", 256), ("qd", 256), ("kd", 128), ("vd", 128))
_PROJ_OFF = {}
_o = 0
for _n, _w in _PROJ_GROUPS:
    _PROJ_OFF[_n] = (_o, _w)
    _o += _w
PROJ_COLS = _o
_GQA_PERM = (0, 2, 1, 3)


def _cparams(sem):
    return pltpu.CompilerParams(dimension_semantics=sem, vmem_limit_bytes=VMEM_LIMIT)


def _lane_iota(shape):
    return lax.broadcasted_iota(jnp.int32, shape, len(shape) - 1)


def _sigmoid(x):
    return 1.0 / (1.0 + jnp.exp(-x))


def _mod_kernel(c_ref, w_ref, b_ref, o_ref):
    cf = c_ref[...]
    s = cf * _sigmoid(cf)
    o_ref[0] = jnp.dot(s, w_ref[0], precision=HIGHEST, preferred_element_type=F32) + b_ref[0]


def _modulation(c_all, w_mod, b_mod):
    n_layers, d, n_out = w_mod.shape
    rows = c_all.shape[0]
    tn = 1536
    return pl.pallas_call(
        _mod_kernel,
        out_shape=jax.ShapeDtypeStruct((n_layers, rows, n_out), F32),
        grid=(n_layers, n_out // tn),
        in_specs=[pl.BlockSpec((rows, d), lambda l, j: (0, 0)),
                  pl.BlockSpec((1, d, tn), lambda l, j: (l, 0, j)),
                  pl.BlockSpec((1, 1, tn), lambda l, j: (l, 0, j))],
        out_specs=pl.BlockSpec((1, rows, tn), lambda l, j: (l, 0, j)),
        compiler_params=_cparams(("arbitrary", "arbitrary")),
        name="modulation",
    )(c_all, w_mod, b_mod.reshape(n_layers, 1, n_out))


def _rms(x):
    return x * lax.rsqrt(jnp.mean(x * x, axis=-1, keepdims=True) + EPS)


def _swap_blocks(x, blk):
    lane = _lane_iota(x.shape)
    up = pltpu.roll(x, LANES - blk, 1)
    dn = pltpu.roll(x, blk, 1)
    return jnp.where((lane // blk) % 2 == 0, up, dn)


def _rope(x, cos, sin, blk):
    return x * cos + _swap_blocks(x, blk) * sin


def _pair_norm(x, g):
    lo = _lane_iota(x.shape) < HEAD_DIM
    sq = x * x
    s_lo = jnp.sum(jnp.where(lo, sq, 0.0), axis=-1, keepdims=True)
    s_hi = jnp.sum(jnp.where(lo, 0.0, sq), axis=-1, keepdims=True)
    ms = jnp.where(lo, s_lo, s_hi) * (1.0 / HEAD_DIM)
    return x * lax.rsqrt(ms + EPS) * g


def _load_token_major(ref, rows):
    return jnp.concatenate(
        [ref[pl.ds(c, rows, stride=SUBLANES), :] for c in range(SUBLANES)], axis=-1)


def _tile_rows(xc_ref, xl_ref, is_ctx):
    return jnp.where(is_ctx, xc_ref[...], xl_ref[...])


def _proj_kernel(*refs, with_f, tm, scale_a, nct):
    it = iter(refs)
    if with_f:
        x_ref = next(it)
        f_ref = next(it)
        modp_ref = next(it)
    else:
        xc_ref, xl_ref = next(it), next(it)
    mod_ref = next(it)
    gmix_ref, win_ref, gqa_ref, wqb_ref, gkva_ref, wkvb_ref, gqd_ref, gkd_ref = (next(it) for _ in range(8))
    cosh_ref, sinh_ref, cosa_ref, sina_ref = (next(it) for _ in range(4))
    if with_f:
        x2_ref = next(it)
    h_ref = next(it)
    qa_ref, ka_ref, va_ref, qb_ref, kb_ref, vb_ref, qc_ref, kc_ref, vc_ref, qd_ref, kd_ref, vd_ref = (
        next(it) for _ in range(12))

    if with_f:
        xf = x_ref[...] + modp_ref[0, 5:6, :] * _load_token_major(f_ref, tm)
        x2_ref[...] = xf
    else:
        xf = _tile_rows(xc_ref, xl_ref, pl.program_id(0) < nct)
    h = _rms(xf) * gmix_ref[...] * (1.0 + mod_ref[0, 1:2, :]) + mod_ref[0, 0:1, :]
    hb = h.astype(BF16)
    h_ref[...] = hb
    def grp(name):
        o, w = _PROJ_OFF[name]
        return jnp.dot(hb, win_ref[:, o:o + w], preferred_element_type=F32)

    cosh, sinh = cosh_ref[...], sinh_ref[...]
    cosa, sina = cosa_ref[...], sina_ref[...]

    cq = (_rms(grp("cq")) * gqa_ref[...]).astype(BF16)
    qa = jnp.dot(cq, wqb_ref[...], preferred_element_type=F32)
    for hd in range(N_HEADS):
        sl = slice(hd * LANES, (hd + 1) * LANES)
        qa_ref[:, sl] = (_rope(qa[:, sl], cosa, sina, 8) * scale_a).astype(BF16)
    ckv = (_rms(grp("ckv")) * gkva_ref[...]).astype(BF16)
    kva = jnp.dot(ckv, wkvb_ref[...], preferred_element_type=F32)
    kr = _rope(grp("kr"), cosa, sina, 8)
    for hd in range(N_HEADS):
        sl = slice(hd * LANES, (hd + 1) * LANES)
        ka_ref[:, sl] = (kva[:, sl] + kr).astype(BF16)
    va_ref[...] = kva[:, N_HEADS * LANES:].astype(BF16)

    qb = grp("qb")
    for j in range(2):
        sl = slice(j * LANES, (j + 1) * LANES)
        qb_ref[:, sl] = _rope(qb[:, sl], cosh, sinh, 16).astype(BF16)
    kb_ref[...] = _rope(grp("kb"), cosh, sinh, 16).astype(BF16)
    vb_ref[...] = grp("vb").astype(BF16)

    qc_ref[...] = grp("qc").astype(BF16)
    kc_ref[...] = grp("kc").astype(BF16)
    vc_ref[...] = grp("vc").astype(BF16)

    qd = grp("qd")
    for j in range(2):
        sl = slice(j * LANES, (j + 1) * LANES)
        qd_ref[:, sl] = _rope(_pair_norm(qd[:, sl], gqd_ref[...]), cosh, sinh, 16).astype(BF16)
    kd_ref[...] = _rope(_pair_norm(grp("kd"), gkd_ref[...]), cosh, sinh, 16).astype(BF16)
    vd_ref[...] = grp("vd").astype(BF16)


def _split_row_specs(lay, off=0):
    tm, D, nct = lay["tm"], lay["D"], lay["nct"]
    return [pl.BlockSpec((tm, D), lambda i: (jnp.minimum(i + off, nct - 1), 0)),
            pl.BlockSpec((tm, D), lambda i: (jnp.maximum(i + off - nct, 0), 0))]


def _proj_call(lay, x, f, modp, mod, layer, w, tabs):
    T, D, tm = lay["T"], lay["D"], lay["tm"]
    nct, tpb, B = lay["nct"], lay["tpb"], lay["B"]
    with_f = f is not None
    n_tiles = T // tm

    def mod_row(i):
        return jnp.where(i < nct, B, (i - nct) // tpb)

    def tab_blk(i):
        return jnp.where(i < nct, tpb, (i - nct) % tpb)

    row = lambda i: (i, 0)
    const = lambda i: (0, 0)
    if with_f:
        in_specs = [pl.BlockSpec((tm, D), row),
                    pl.BlockSpec((tm * SUBLANES, LANES), row),
                    pl.BlockSpec((1, 6, D), lambda i: ((layer - 1) * lay["mod_rows"] + mod_row(i), 0, 0))]
        args = [x, f, modp]
    else:
        in_specs = _split_row_specs(lay)
        args = list(x)
    in_specs += [pl.BlockSpec((1, 6, D), lambda i: (layer * lay["mod_rows"] + mod_row(i), 0, 0))]
    args += [mod]
    for name in ("g_mix", "w_in", "g_q_a", "w_q_b", "g_kv_a", "w_kv_b", "g_q_d", "g_k_d"):
        a = w[name]
        in_specs.append(pl.BlockSpec(a.shape, const))
        args.append(a)
    for tname in ("cos_h", "sin_h", "cos_a", "sin_a"):
        in_specs.append(pl.BlockSpec((tm, LANES), lambda i: (tab_blk(i), 0)))
        args.append(tabs[tname])

    widths = [("h", D), ("qa", 512), ("ka", 512), ("va", 256), ("qb", 256), ("kb", 128), ("vb", 128),
              ("qc", 256), ("kc", 256), ("vc", 256), ("qd", 256), ("kd", 128), ("vd", 128)]
    out_shape, out_specs = [], []
    if with_f:
        out_shape.append(jax.ShapeDtypeStruct((T, D), F32))
        out_specs.append(pl.BlockSpec((tm, D), row))
    for _, wd in widths:
        out_shape.append(jax.ShapeDtypeStruct((T, wd), BF16))
        out_specs.append(pl.BlockSpec((tm, wd), row))

    outs = pl.pallas_call(
        functools.partial(_proj_kernel, with_f=with_f, tm=tm, nct=nct,
                          scale_a=float((A_NOPE + A_ROPE) ** -0.5 * LOG2E)),
        out_shape=out_shape, grid=(n_tiles,), in_specs=in_specs, out_specs=out_specs,
        compiler_params=_cparams(("parallel",)),
        name="proj_in",
    )(*args)
    outs = list(outs)
    x2 = outs.pop(0) if with_f else x
    names = [n for n, _ in widths]
    return x2, dict(zip(names, outs))


_NT = (((1,), (1,)), ((), ()))


def _head_plan(kind):
    if kind == "A":
        return tuple((r, None, r, r // 2) for r in range(N_HEADS))
    if kind == "C":
        return tuple((r // 2, r % 2, r // 2, r // 2) for r in range(N_HEADS))
    return tuple((r // 2, r % 2, 0, 0) for r in range(N_HEADS))


def _head_query(q_ref, plan_r):
    qt, half, _, _ = plan_r
    src = q_ref[:, qt * LANES:(qt + 1) * LANES]
    if half is None:
        return src
    lane = _lane_iota(src.shape)
    keep = (lane < HEAD_DIM) if half == 0 else (lane >= HEAD_DIM)
    return jnp.where(keep, src, jnp.zeros_like(src))


def _score_chunks(q, k_blocks, bias_blocks):
    chunks = []
    for kb, bb in zip(k_blocks, bias_blocks):
        s = lax.dot_general(q, kb, _NT, preferred_element_type=F32)
        if bb is not None:
            s = s + bb
        chunks += [s[:, c * LANES:(c + 1) * LANES] for c in range(s.shape[1] // LANES)]
    return chunks


def _row_max(chunks):
    m = functools.reduce(jnp.maximum, chunks)
    return jnp.broadcast_to(jnp.max(m, axis=-1, keepdims=True), m.shape)


def _weighted_values(p_chunks, v_blocks, half):
    pv, idx = None, 0
    for vb in v_blocks:
        n = vb.shape[0] // LANES
        p = jnp.concatenate(p_chunks[idx:idx + n], axis=1).astype(BF16)
        idx += n
        lane = _lane_iota(vb.shape)
        own = (lane < HEAD_DIM) if half == 0 else (lane >= HEAD_DIM)
        d = jnp.dot(p, jnp.where(own, vb, jnp.ones_like(vb)), preferred_element_type=F32)
        pv = d if pv is None else pv + d
    return pv


def _softmax_once(q, k_blocks, v_blocks, bias_blocks, sink, half):
    chunks = _score_chunks(q, k_blocks, bias_blocks)
    m = _row_max(chunks)
    if sink is not None:
        m = jnp.maximum(m, sink)
    p = [jnp.exp2(c - m) for c in chunks]
    pv = _weighted_values(p, v_blocks, half)
    l = pltpu.roll(pv, HEAD_DIM, 1)
    if sink is not None:
        l = l + jnp.exp2(sink - m)
    return pv / l


def _store_heads(o_ref, outs):
    lane = _lane_iota(outs[0].shape)
    for g in range(2):
        o_ref[:, g * LANES:(g + 1) * LANES] = jnp.where(
            lane < HEAD_DIM, outs[2 * g], outs[2 * g + 1]).astype(o_ref.dtype)


def _attn_band_kernel(*refs, kind, lb, n_kv, tq):
    it = iter(refs)
    q_ref = next(it)
    k_refs = [next(it) for _ in range(n_kv)]
    v_refs = [next(it) for _ in range(n_kv)]
    bias_ref = next(it) if n_kv > 1 else None
    sink_ref = next(it) if kind == "B" else None
    if n_kv == 1:
        next(it)
    o_ref = next(it)
    t = pl.program_id(0) % lb
    var = jnp.where(t == 0, 0, jnp.where(t == lb - 1, 2, 1))
    outs = []
    for r, plan_r in enumerate(_head_plan(kind)):
        _, _, kt, vt = plan_r
        ksl = slice(kt * LANES, (kt + 1) * LANES)
        vsl = slice(vt * LANES, (vt + 1) * LANES)
        bias = [None]
        for j in range(n_kv - 1):
            if kind == "C":
                bias.append(bias_ref[var, j, r * tq:(r + 1) * tq, :])
            else:
                bias.append(bias_ref[var, j])
        sink = sink_ref[_GQA_PERM[r]] * LOG2E if kind == "B" else None
        outs.append(_softmax_once(_head_query(q_ref, plan_r), [k[:, ksl] for k in k_refs],
                                  [v[:, vsl] for v in v_refs], bias, sink, r % 2))
    _store_heads(o_ref, outs)


def _attn_dense_kernel(q_ref, kc_ref, vc_ref, kl_ref, vl_ref, o_ref, qs_ref, m_ref, acc_ref,
                       *, kind, n_steps, tq):
    s = pl.program_id(1)
    plan = _head_plan(kind)

    def update(first):
        for r, (_, _, kt, vt) in enumerate(plan):
            rows = slice(r * tq, (r + 1) * tq)
            ksl = slice(kt * LANES, (kt + 1) * LANES)
            vsl = slice(vt * LANES, (vt + 1) * LANES)
            if first:
                q = _head_query(q_ref, plan[r])
                qs_ref[rows, :] = q
                k_blocks = [kc_ref[:, ksl], kl_ref[:, ksl]]
                v_blocks = [vc_ref[:, vsl], vl_ref[:, vsl]]
            else:
                q = qs_ref[rows, :]
                k_blocks = [kl_ref[:, ksl]]
                v_blocks = [vl_ref[:, vsl]]
            chunks = _score_chunks(q, k_blocks, [None] * len(k_blocks))
            m_cur = _row_max(chunks)
            if first:
                m_new = m_cur
            else:
                m_prev = m_ref[rows, :]
                m_new = jnp.maximum(m_prev, m_cur)
                alpha = jnp.exp2(m_prev - m_new)
            p = [jnp.exp2(c - m_new) for c in chunks]
            pv = _weighted_values(p, v_blocks, r % 2)
            if first:
                acc_ref[rows, :] = pv
            else:
                acc_ref[rows, :] = alpha * acc_ref[rows, :] + pv
            m_ref[rows, :] = m_new

    @pl.when(s == 0)
    def _():
        update(True)

    @pl.when(s > 0)
    def _():
        update(False)

    @pl.when(s == n_steps - 1)
    def _():
        outs = []
        for r in range(N_HEADS):
            rows = slice(r * tq, (r + 1) * tq)
            acc = acc_ref[rows, :]
            outs.append(acc / pltpu.roll(acc, HEAD_DIM, 1))
        _store_heads(o_ref, outs)


def _attn_latent_call(lay, kind, q, k, v, *, sink=None, bias=None):
    T, B, ncb, lb, S = lay["T"], lay["B"], lay["ncb"], lay["lb"], lay["S"]
    tq = ROW_BLK
    qw, kw, vw = q.shape[1], k.shape[1], v.shape[1]
    out_shape = jax.ShapeDtypeStruct((T, BRANCH_W), BF16)
    if kind in ("A", "D"):
        tk, tq = lay["tk_dense"], lay["tq_dense"]
        n_steps = S // tk
        lat0 = ncb * ROW_BLK // tk
        q0 = ncb * ROW_BLK // tq
        qpb = S // tq
        lat_blk = lambda i, s: (lat0 + (i // qpb) * n_steps + s, 0)
        return pl.pallas_call(
            functools.partial(_attn_dense_kernel, kind=kind, n_steps=n_steps, tq=tq),
            out_shape=out_shape,
            grid=(B * qpb, n_steps),
            in_specs=[pl.BlockSpec((tq, qw), lambda i, s: (q0 + i, 0)),
                      pl.BlockSpec((ROW_BLK, kw), lambda i, s: (i // qpb, 0)),
                      pl.BlockSpec((ROW_BLK, vw), lambda i, s: (i // qpb, 0)),
                      pl.BlockSpec((tk, kw), lat_blk),
                      pl.BlockSpec((tk, vw), lat_blk)],
            out_specs=pl.BlockSpec((tq, BRANCH_W), lambda i, s: (q0 + i, 0)),
            scratch_shapes=[pltpu.VMEM((N_HEADS * tq, LANES), BF16),
                            pltpu.VMEM((N_HEADS * tq, LANES), F32),
                            pltpu.VMEM((N_HEADS * tq, LANES), F32)],
            compiler_params=_cparams(("parallel", "arbitrary")),
            name="attn_" + kind,
        )(q, k, v, k, v)

    tq = lay["tq_band"]
    qpb = S // tq
    q0 = ncb * ROW_BLK // tq
    bpq = tq // ROW_BLK

    def nb(i, d):
        return (ncb + (i // qpb) * lb + jnp.clip((i % qpb) * bpq + d, 0, lb - 1), 0)

    kv_maps = [lambda i: (i // qpb, 0)] + [functools.partial(nb, d=d) for d in _band_offsets(tq)]
    n_kv = len(kv_maps)
    in_specs = [pl.BlockSpec((tq, qw), lambda i: (q0 + i, 0))]
    in_specs += [pl.BlockSpec((ROW_BLK, kw), m) for m in kv_maps]
    in_specs += [pl.BlockSpec((ROW_BLK, vw), m) for m in kv_maps]
    in_specs.append(pl.BlockSpec(bias.shape, lambda i: (0,) * bias.ndim, pipeline_mode=pl.Buffered(1)))
    args = [q] + [k] * n_kv + [v] * n_kv + [bias]
    if kind == "B":
        in_specs.append(pl.BlockSpec(memory_space=pltpu.SMEM))
        args.append(sink)
    return pl.pallas_call(
        functools.partial(_attn_band_kernel, kind=kind, lb=qpb, n_kv=n_kv, tq=tq),
        out_shape=out_shape,
        grid=(B * qpb,),
        in_specs=in_specs,
        out_specs=pl.BlockSpec((tq, BRANCH_W), lambda i: (q0 + i, 0)),
        compiler_params=_cparams(("parallel",)),
        name="attn_" + kind,
    )(*args)


def _attn_context_call(lay, kind, q, k, v, o, *, sink=None):
    ncb = lay["ncb"]
    qw, kw, vw = q.shape[1], k.shape[1], v.shape[1]
    blk = lambda i: (i, 0)
    in_specs = [pl.BlockSpec((ROW_BLK, qw), blk), pl.BlockSpec((ROW_BLK, kw), blk),
                pl.BlockSpec((ROW_BLK, vw), blk)]
    args = [q, k, v]
    if kind == "B":
        in_specs.append(pl.BlockSpec(memory_space=pltpu.SMEM))
        args.append(sink)
    in_specs.append(pl.BlockSpec(memory_space=pl.ANY))
    args.append(o)
    return pl.pallas_call(
        functools.partial(_attn_band_kernel, kind=kind, lb=1, n_kv=1, tq=ROW_BLK),
        out_shape=jax.ShapeDtypeStruct(o.shape, o.dtype),
        grid=(ncb,),
        in_specs=in_specs,
        out_specs=pl.BlockSpec((ROW_BLK, BRANCH_W), blk),
        input_output_aliases={len(args) - 1: 0},
        compiler_params=_cparams(("parallel",)),
        name="attn_ctx_" + kind,
    )(*args)


def _band_offsets(tq):
    return tuple(range(-1, tq // ROW_BLK + 1))


def _window_bias(S, tq):
    lb, qpb, bpq = S // ROW_BLK, S // tq, tq // ROW_BLK
    offs = _band_offsets(tq)
    qa, ka = np.arange(tq), np.arange(ROW_BLK)
    out = np.full((3, len(offs), tq, ROW_BLK), NEG_INF, np.float32)
    for vi, t_rep in enumerate((0, 1, qpb - 1)):
        for di, d in enumerate(offs):
            kt = t_rep * bpq + d
            if not 0 <= kt < lb:
                continue
            qpos = t_rep * tq + qa
            kpos = kt * ROW_BLK + ka
            ok = np.abs(qpos[:, None] - kpos[None, :]) <= WINDOW
            out[vi, di] = np.where(ok, 0.0, NEG_INF)
    return jnp.asarray(out)


def _neighbourhood_bias(rpb, rows_total, tq):
    lb = rows_total * GRID_W // ROW_BLK
    qpb, bpq = rows_total * GRID_W // tq, tq // ROW_BLK
    rpt = ROW_BLK // GRID_W
    rpq = tq // GRID_W
    kh = min(NA_KH, rows_total)
    qa, ka = np.arange(tq), np.arange(ROW_BLK)
    q_sub, q_col = qa // GRID_W, qa % GRID_W
    k_sub, k_col = ka // GRID_W, ka % GRID_W
    n_dr, n_dc = 2 * NA_KH - 1, 2 * NA_KW - 1
    col = np.arange(GRID_W)
    dc = np.clip(col[None, :] - col[:, None], -(NA_KW - 1), NA_KW - 1) + NA_KW - 1
    hot_c = (dc[:, :, None] == np.arange(n_dc)).astype(np.float32)
    by_col = jnp.einsum("huv,cdv->hucd", rpb.astype(F32) * LOG2E, jnp.asarray(hot_c),
                        precision=HIGHEST)
    offs = _band_offsets(tq)
    vals = []
    for d in offs:
        dr = np.clip(d * rpt + np.arange(rpt)[None, :] - np.arange(rpq)[:, None],
                     -(NA_KH - 1), NA_KH - 1) + NA_KH - 1
        hot_r = (dr[:, :, None] == np.arange(n_dr)).astype(np.float32)
        v = jnp.einsum("abu,hucd->hacbd", jnp.asarray(hot_r), by_col, precision=HIGHEST)
        vals.append(v.reshape(N_HEADS, tq, ROW_BLK))
    out = []
    for t_rep in (0, 1, qpb - 1):
        per_block = []
        for di, d in enumerate(offs):
            kt = t_rep * bpq + d
            q_row = t_rep * rpq + q_sub
            k_row = kt * rpt + k_sub
            r_start = np.clip(q_row - kh // 2, 0, rows_total - kh)
            row_ok = (k_row[None] >= r_start[:, None]) & (k_row[None] < r_start[:, None] + kh)
            c_start = np.clip(q_col - NA_KW // 2, 0, GRID_W - NA_KW)
            col_ok = (k_col[None] >= c_start[:, None]) & (k_col[None] < c_start[:, None] + NA_KW)
            ok = row_ok & col_ok & (0 <= kt < lb)
            per_block.append(jnp.where(jnp.asarray(ok)[None], vals[di], NEG_INF).reshape(-1, ROW_BLK))
        out.append(jnp.stack(per_block))
    return jnp.stack(out)


def _merge_kernel(*refs, tm, split_x, n_ctx_tiles):
    it = iter(refs)
    h_ref, oa_ref, ob_ref, oc_ref, od_ref = (next(it) for _ in range(5))
    if split_x:
        xc_ref, xl_ref = next(it), next(it)
    else:
        x_ref = next(it)
    (mod_ref, gffn_ref, wg_ref, bg_ref, wb_ref, wout_ref, wr_ref, br_ref,
     x1_ref, h2_ref, route_ref, y_ref) = (next(it) for _ in range(12))
    hb = h_ref[...]
    d_model = hb.shape[1]
    for t in range(d_model // MERGE_TN):
        cs = slice(t * MERGE_TN, (t + 1) * MERGE_TN)
        y = None
        for n, o_ref in enumerate((oa_ref, ob_ref, oc_ref, od_ref)):
            gate = _sigmoid(jnp.dot(hb, wg_ref[n, :, cs], preferred_element_type=F32) + bg_ref[n, :, cs])
            u = gate * jnp.dot(o_ref[...], wb_ref[n, :, cs], preferred_element_type=F32)
            y = u if y is None else y + u
        y_ref[:, cs] = y.astype(BF16)
    z = jnp.dot(y_ref[...], wout_ref[...], preferred_element_type=F32)
    x_in = _tile_rows(xc_ref, xl_ref, pl.program_id(0) < n_ctx_tiles) if split_x else x_ref[...]
    x1 = x_in + mod_ref[0, 2:3, :] * z
    x1_ref[...] = x1
    h2 = _rms(x1) * gffn_ref[...] * (1.0 + mod_ref[0, 4:5, :]) + mod_ref[0, 3:4, :]
    for c in range(SUBLANES):
        h2_ref[pl.ds(c, tm, stride=SUBLANES), :] = h2[:, c * LANES:(c + 1) * LANES]

    h2_hi = h2.astype(BF16)
    h2_lo = (h2 - h2_hi.astype(F32)).astype(BF16)
    logit = (jnp.dot(h2_hi, wr_ref[0], preferred_element_type=F32)
             + jnp.dot(h2_lo, wr_ref[0], preferred_element_type=F32)
             + jnp.dot(h2_hi, wr_ref[1], preferred_element_type=F32)) + br_ref[...]
    lane = _lane_iota(logit.shape)
    big = jnp.int32(1 << 20)
    is_g = (lane >= N_EXPERTS) & (lane < N_EXPERTS + N_GROUPS)
    gl = jnp.where(is_g, logit, NEG_INF)
    gmax = jnp.max(gl, axis=-1, keepdims=True)
    gsel = jnp.min(jnp.where(gl == gmax, lane - N_EXPERTS, big), axis=-1, keepdims=True)
    gw = 1.0 / jnp.sum(jnp.where(is_g, jnp.exp(gl - gmax), 0.0), axis=-1, keepdims=True)
    in_grp = (lane < N_EXPERTS) & ((lane // EXPERTS_PER_GROUP) == gsel)
    el = jnp.where(in_grp, logit, NEG_INF)
    v1 = jnp.max(el, axis=-1, keepdims=True)
    i1 = jnp.min(jnp.where(el == v1, lane, big), axis=-1, keepdims=True)
    el2 = jnp.where(lane == i1, NEG_INF, el)
    v2 = jnp.max(el2, axis=-1, keepdims=True)
    i2 = jnp.min(jnp.where(el2 == v2, lane, big), axis=-1, keepdims=True)
    e21 = jnp.exp(v2 - v1)
    w1 = gw / (1.0 + e21)
    w2 = gw * e21 / (1.0 + e21)
    route_ref[...] = jnp.where(lane == 0, i1.astype(F32),
                               jnp.where(lane == 1, i2.astype(F32),
                                         jnp.where(lane == 2, w1, jnp.where(lane == 3, w2, 0.0))))


def _merge_call(lay, layer, h, o, x, mod, w, *, with_ctx):
    T, D, tm = lay["T"], lay["D"], lay["tm"]
    nct, tpb, B = lay["nct"], lay["tpb"], lay["B"]
    off = 0 if with_ctx else nct
    n_tiles = T // tm - off

    def mod_row(i):
        return jnp.where(i < nct, B, (i - nct) // tpb)

    row = lambda i: (i + off, 0)
    const2 = lambda i: (0, 0)
    const3 = lambda i: (0, 0, 0)
    split_x = isinstance(x, (tuple, list))
    in_specs = [pl.BlockSpec((tm, D), row)]
    in_specs += [pl.BlockSpec((tm, BRANCH_W), row)] * 4
    in_specs += _split_row_specs(lay, off) if split_x else [pl.BlockSpec((tm, D), row)]
    in_specs += [pl.BlockSpec((1, 6, D), lambda i: (layer * lay["mod_rows"] + mod_row(i + off), 0, 0)),
                 pl.BlockSpec((1, D), const2),
                 pl.BlockSpec(w["w_gate"].shape, const3),
                 pl.BlockSpec(w["b_gate"].shape, const3),
                 pl.BlockSpec(w["w_branch"].shape, const3),
                 pl.BlockSpec((D, D), const2),
                 pl.BlockSpec((2, D, LANES), const3),
                 pl.BlockSpec((1, LANES), const2)]
    out_shape = [jax.ShapeDtypeStruct((T, D), F32),
                 jax.ShapeDtypeStruct((T * SUBLANES, LANES), F32),
                 jax.ShapeDtypeStruct((T, LANES), F32)]
    out_specs = [pl.BlockSpec((tm, D), row),
                 pl.BlockSpec((tm * SUBLANES, LANES), row),
                 pl.BlockSpec((tm, LANES), row)]
    return pl.pallas_call(
        functools.partial(_merge_kernel, tm=tm, split_x=split_x, n_ctx_tiles=nct - off),
        out_shape=out_shape, grid=(n_tiles,), in_specs=in_specs, out_specs=out_specs,
        scratch_shapes=[pltpu.VMEM((tm, D), BF16)],
        compiler_params=_cparams(("parallel",)),
        name="merge",
    )(h, o["A"], o["B"], o["C"], o["D"], *(x if split_x else (x,)), mod, w["g_ffn"], w["w_gate"],
      w["b_gate"], w["w_branch"], w["w_out"], w["w_route"], w["b_route"])


def _moe_kernel(tab_ref, tok_ref, h2_ref, sw_ref, w1_ref, w3_ref, w2_ref, f_ref,
                xg_ref, y_ref, st_ref, *, blk, slots, tile_off):
    ti = pl.program_id(0)
    step = pl.program_id(1)

    @pl.when(step == 0)
    def _():
        f_ref[...] = jnp.zeros_like(f_ref)
        xg_ref[...] = jnp.zeros_like(xg_ref)

    tok_base = (ti + tile_off) * slots
    for j in range(MOE_EXPERTS_PER_STEP):
        base = ((ti + tile_off) * N_EXPERTS + step * MOE_EXPERTS_PER_STEP + j) * 2
        _moe_expert(tab_ref[base], tab_ref[base + 1], tok_base, tok_ref, h2_ref, sw_ref,
                    w1_ref.at[j], w3_ref.at[j], w2_ref.at[j], f_ref, xg_ref, y_ref, st_ref, blk)


def _moe_expert(seg0, n_pad, tok_base, tok_ref, h2_ref, sw_ref, w1_ref, w3_ref, w2_ref, f_ref,
                xg_ref, y_ref, st_ref, blk):
    n_blocks = (n_pad + blk - 1) // blk

    def block_body(b, carry):
        off = pl.multiple_of(seg0 + b * blk, SUBLANES)
        n_grp = jnp.minimum(n_pad - b * blk, blk) // SUBLANES

        def gather(gi, c):
            for u in range(SUBLANES):
                r = gi * SUBLANES + u
                tok = tok_ref[tok_base + off + r]
                src = pl.multiple_of(tok * SUBLANES, SUBLANES)
                dst = pl.multiple_of(r * SUBLANES, SUBLANES)
                xg_ref[pl.ds(dst, SUBLANES), :] = h2_ref[pl.ds(src, SUBLANES), :]
            return c

        lax.fori_loop(0, n_grp, gather, 0)
        xb = _load_token_major(xg_ref, blk).astype(BF16)
        a = jnp.dot(xb, w1_ref[...], preferred_element_type=F32)
        g = jnp.dot(xb, w3_ref[...], preferred_element_type=F32)
        hid = (a * _sigmoid(a) * g).astype(BF16)
        y = jnp.dot(hid, w2_ref[...], preferred_element_type=F32)
        y_ref[...] = y * sw_ref[0, pl.ds(off, blk), :]

        def scatter(gi, c):
            r0 = pl.multiple_of(gi * SUBLANES, SUBLANES)
            for ch in range(SUBLANES):
                st_ref[pl.ds(ch, SUBLANES, stride=SUBLANES), :] = y_ref[pl.ds(r0, SUBLANES),
                                                                        ch * LANES:(ch + 1) * LANES]
            dsts, vals = [], []
            for u in range(SUBLANES):
                tok = tok_ref[tok_base + off + r0 + u]
                dsts.append(pl.multiple_of(tok * SUBLANES, SUBLANES))
                vals.append(f_ref[pl.ds(dsts[u], SUBLANES), :] + st_ref[u * SUBLANES:(u + 1) * SUBLANES, :])
            for u in reversed(range(SUBLANES)):
                f_ref[pl.ds(dsts[u], SUBLANES), :] = vals[u]
            return c

        lax.fori_loop(0, n_grp, scatter, 0)
        return carry

    lax.fori_loop(0, n_blocks, block_body, 0)


def _moe_call(lay, route, h2, w, *, with_ctx):
    T, D, tt = lay["T"], lay["D"], lay["tt"]
    blk = lay["moe_blk"]
    n_tiles_all = T // tt
    tile_off = 0 if with_ctx else lay["ncb"] * ROW_BLK // tt
    n_tiles = n_tiles_all - tile_off
    n_assign = 2 * tt
    n_fill = N_EXPERTS * SUBLANES
    slots = n_assign + n_fill + blk

    eid = route[:, 0:2].astype(jnp.int32).reshape(n_tiles_all, n_assign)
    wts = route[:, 2:4].reshape(n_tiles_all, n_assign)
    tok = jnp.broadcast_to(jnp.arange(n_assign, dtype=jnp.int32)[None, :] // 2, eid.shape)
    ex = jnp.arange(N_EXPERTS, dtype=jnp.int32)
    counts = jnp.sum(eid[:, :, None] == ex[None, None, :], axis=1, dtype=jnp.int32)
    n_dummy = (-counts) % SUBLANES
    fill_key = jnp.where(jnp.arange(SUBLANES, dtype=jnp.int32)[None, None, :] < n_dummy[:, :, None],
                         ex[None, :, None], N_EXPERTS).reshape(n_tiles_all, n_fill)
    zeros_i = jnp.zeros((n_tiles_all, n_fill), jnp.int32)
    _, slot_tok, slot_w = lax.sort(
        (jnp.concatenate([eid, fill_key], axis=1), jnp.concatenate([tok, zeros_i], axis=1),
         jnp.concatenate([wts, zeros_i.astype(F32)], axis=1)),
        dimension=1, is_stable=True, num_keys=1)
    slot_tok = jnp.pad(slot_tok, ((0, 0), (0, blk)))
    slot_w = jnp.pad(slot_w, ((0, 0), (0, blk)))
    padded = counts + n_dummy
    seg0 = jnp.cumsum(padded, axis=1) - padded
    tab = jnp.stack([seg0, padded], axis=-1).reshape(-1).astype(jnp.int32)

    return pl.pallas_call(
        functools.partial(_moe_kernel, blk=blk, slots=slots, tile_off=tile_off),
        out_shape=jax.ShapeDtypeStruct((T * SUBLANES, LANES), F32),
        grid_spec=pltpu.PrefetchScalarGridSpec(
            num_scalar_prefetch=2,
            grid=(n_tiles, N_EXPERTS // MOE_EXPERTS_PER_STEP),
            in_specs=[
                pl.BlockSpec((tt * SUBLANES, LANES), lambda t, e, *_: (t + tile_off, 0)),
                pl.BlockSpec((1, slots, 1), lambda t, e, *_: (t + tile_off, 0, 0)),
                pl.BlockSpec((MOE_EXPERTS_PER_STEP, D, EXPERT_FF), lambda t, e, *_: (e, 0, 0)),
                pl.BlockSpec((MOE_EXPERTS_PER_STEP, D, EXPERT_FF), lambda t, e, *_: (e, 0, 0)),
                pl.BlockSpec((MOE_EXPERTS_PER_STEP, EXPERT_FF, D), lambda t, e, *_: (e, 0, 0)),
            ],
            out_specs=pl.BlockSpec((tt * SUBLANES, LANES), lambda t, e, *_: (t + tile_off, 0)),
            scratch_shapes=[pltpu.VMEM((blk * SUBLANES, LANES), F32),
                            pltpu.VMEM((blk, D), F32),
                            pltpu.VMEM((SUBLANES * SUBLANES, LANES), F32)],
        ),
        compiler_params=_cparams(("parallel", "arbitrary")),
        name="moe_experts",
    )(tab, slot_tok.reshape(-1), h2, slot_w.reshape(n_tiles_all, slots, 1),
      w["w_ff1"], w["w_ff3"], w["w_ff2"])


def _final_kernel(x_ref, f_ref, mod_ref, g_ref, o_ref, *, tm):
    xf = x_ref[...] + mod_ref[0, 5:6, :] * _load_token_major(f_ref, tm)
    o_ref[...] = _rms(xf) * g_ref[...]


def _final_call(lay, layer, x1, f, mod, g_final):
    T, D, tm = lay["T"], lay["D"], lay["tm"]
    nct, tpb = lay["nct"], lay["tpb"]
    n_lat = T // tm - nct
    return pl.pallas_call(
        functools.partial(_final_kernel, tm=tm),
        out_shape=jax.ShapeDtypeStruct((n_lat * tm, D), F32),
        grid=(n_lat,),
        in_specs=[pl.BlockSpec((tm, D), lambda i: (i + nct, 0)),
                  pl.BlockSpec((tm * SUBLANES, LANES), lambda i: (i + nct, 0)),
                  pl.BlockSpec((1, 6, D), lambda i: (layer * lay["mod_rows"] + i // tpb, 0, 0)),
                  pl.BlockSpec((1, D), lambda i: (0, 0))],
        out_specs=pl.BlockSpec((tm, D), lambda i: (i, 0)),
        compiler_params=_cparams(("parallel",)),
        name="final_norm",
    )(x1, f, mod, g_final)


def _select_cols(wm, segs, scale=None):
    parts = []
    for k, (start, width) in enumerate(segs):
        if start is None:
            parts.append(jnp.zeros((wm.shape[0], width), wm.dtype))
        else:
            blk = wm[:, start:start + width]
            parts.append(blk if scale is None or scale[k] is None else blk * scale[k])
    return jnp.concatenate(parts, axis=1)


def _prep_layer(l, p):
    a_cols = A_Q_RANK + A_KV_RANK + A_ROPE
    b_off = a_cols
    c_off = b_off + 512
    d_off = c_off + 768
    qk_scale = HEAD_DIM ** -0.5 * LOG2E
    gqa_q = lambda off: [(off + hh * HEAD_DIM, HEAD_DIM) for hh in _GQA_PERM]
    segs = ([(0, 256), (256, 128), (None, 64), (384, 32), (None, 32)]
            + gqa_q(b_off) + [(b_off + 256, 128), (b_off + 384, 128)]
            + [(c_off, 256), (c_off + 256, 256), (c_off + 512, 256)]
            + gqa_q(d_off) + [(d_off + 256, 128), (d_off + 384, 128)])
    scale = [None] * len(segs)
    for k in (5, 6, 7, 8, 11):
        scale[k] = qk_scale
    w_in = _select_cols(p["w_in"][l], segs, scale).astype(BF16)
    assert w_in.shape[1] == PROJ_COLS

    hq = A_NOPE + A_ROPE
    segs_q = []
    for hh in range(N_HEADS):
        segs_q += [(hh * hq, hq), (None, LANES - hq)]
    w_q_b = _select_cols(p["w_q_b"][l], segs_q).astype(BF16)
    hk = A_NOPE + HEAD_DIM
    segs_k = []
    for hh in range(N_HEADS):
        segs_k += [(hh * hk, A_NOPE), (None, LANES - A_NOPE)]
    segs_k += [(hh * hk + A_NOPE, HEAD_DIM) for hh in range(N_HEADS)]
    w_kv_b = _select_cols(p["w_kv_b"][l], segs_k).astype(BF16)

    wb = p["w_branch"][l]
    perm_rows = lambda m: jnp.concatenate([m[hh * HEAD_DIM:(hh + 1) * HEAD_DIM] for hh in _GQA_PERM], axis=0)
    w_branch = jnp.stack([wb[0], perm_rows(wb[1]), wb[2], perm_rows(wb[3])]).astype(BF16)

    d = p["w_in"].shape[1]
    w_route = jnp.zeros((d, LANES), F32)
    w_route = w_route.at[:, :N_EXPERTS].set(p["w_router"][l]).at[:, N_EXPERTS:N_EXPERTS + N_GROUPS].set(p["w_group"][l])
    b_route = jnp.zeros((1, LANES), F32)
    b_route = b_route.at[0, :N_EXPERTS].set(p["b_router"][l]).at[0, N_EXPERTS:N_EXPERTS + N_GROUPS].set(p["b_group"][l])
    return {
        "g_mix": p["g_norm_mix"][l][None, :],
        "w_in": w_in,
        "g_q_a": p["g_q_a"][l][None, :],
        "w_q_b": w_q_b,
        "g_kv_a": p["g_kv_a"][l][None, :],
        "w_kv_b": w_kv_b,
        "g_q_d": (jnp.tile(p["g_q_d"][l], 2) * qk_scale)[None, :],
        "g_k_d": jnp.tile(p["g_k_d"][l], 2)[None, :],
        "sink": p["sink_b"][l],
        "rpb": p["rpb_c"][l],
        "w_gate": p["w_gate"][l].astype(BF16),
        "b_gate": p["b_gate"][l][:, None, :],
        "w_branch": w_branch,
        "w_out": p["w_out"][l].astype(BF16),
        "g_ffn": p["g_norm_ffn"][l][None, :],
        "w_route": jnp.stack([w_route.astype(BF16),
                              (w_route - w_route.astype(BF16).astype(F32)).astype(BF16)]),
        "b_route": b_route,
        "w_ff1": p["w_ff1"][l].astype(BF16),
        "w_ff3": p["w_ff3"][l].astype(BF16),
        "w_ff2": p["w_ff2"][l].astype(BF16),
    }


def _rope_tables(S, tm):
    t = np.arange(S)
    rows = (t // GRID_W).astype(np.float32)
    cols = (t % GRID_W).astype(np.float32)

    def cs(rot):
        half = rot // 2
        inv = np.float32(ROPE_THETA) ** (-np.arange(0, half, 2, dtype=np.float32) / np.float32(half))
        ar_, ac_ = rows[:, None] * inv, cols[:, None] * inv
        cos = np.concatenate([np.cos(ar_), np.cos(ar_), np.cos(ac_), np.cos(ac_)], axis=-1)
        sin = np.concatenate([-np.sin(ar_), np.sin(ar_), -np.sin(ac_), np.sin(ac_)], axis=-1)
        return cos.astype(np.float32), sin.astype(np.float32)

    cos64, sin64 = cs(HEAD_DIM)
    cos32, sin32 = cs(A_ROPE)
    ones = lambda n: np.ones((S, n), np.float32)
    zeros = lambda n: np.zeros((S, n), np.float32)
    tabs = {
        "cos_h": np.concatenate([cos64, cos64], axis=-1),
        "sin_h": np.concatenate([sin64, sin64], axis=-1),
        "cos_a": np.concatenate([ones(A_NOPE), cos32, ones(LANES - A_NOPE - A_ROPE)], axis=-1),
        "sin_a": np.concatenate([zeros(A_NOPE), sin32, zeros(LANES - A_NOPE - A_ROPE)], axis=-1),
    }
    ident = {"cos_h": 1.0, "sin_h": 0.0, "cos_a": 1.0, "sin_a": 0.0}
    return {k: jnp.asarray(np.concatenate([v, np.full((tm, LANES), ident[k], np.float32)], axis=0))
            for k, v in tabs.items()}


def _layout(B, S, n_ctx, D):
    assert n_ctx == ROW_BLK and S % 1024 == 0 and S // GRID_W >= 3 * (ROW_BLK // GRID_W)
    T = B * (n_ctx + S)
    tm = 512 if (B * n_ctx) % 512 == 0 else 256
    tk_dense = 1024 if (B * n_ctx) % 1024 == 0 else (512 if (B * n_ctx) % 512 == 0 else 256)
    tt = 2048 if (B * n_ctx) % 2048 == 0 else B * n_ctx
    assert S % tt == 0
    return {
        "B": B, "S": S, "D": D, "T": T, "tm": tm,
        "ncb": B * n_ctx // ROW_BLK,
        "lb": S // ROW_BLK,
        "nct": B * n_ctx // tm,
        "tpb": S // tm,
        "tk_dense": tk_dense,
        "tq_dense": tk_dense,
        "tq_band": tm,
        "tt": tt, "moe_blk": 160,
        "mod_rows": 16,
    }


def kernel(x, c, ctx, c_ctx, w_mod, b_mod, g_norm_mix, w_in, g_q_a, w_q_b, g_kv_a, w_kv_b, sink_b, rpb_c,
           g_q_d, g_k_d, w_gate, b_gate, w_branch, w_out, g_norm_ffn, w_group, b_group, w_router, b_router,
           w_ff1, w_ff3, w_ff2, g_final):
    B, S, D = x.shape
    n_ctx = ctx.shape[1]
    depth = w_mod.shape[0]
    lay = _layout(B, S, n_ctx, D)
    params = dict(w_in=w_in, g_norm_mix=g_norm_mix, g_q_a=g_q_a, w_q_b=w_q_b, g_kv_a=g_kv_a, w_kv_b=w_kv_b,
                  sink_b=sink_b, rpb_c=rpb_c, g_q_d=g_q_d, g_k_d=g_k_d, w_gate=w_gate, b_gate=b_gate,
                  w_branch=w_branch, w_out=w_out, g_norm_ffn=g_norm_ffn, w_group=w_group, b_group=b_group,
                  w_router=w_router, b_router=b_router, w_ff1=w_ff1, w_ff3=w_ff3, w_ff2=w_ff2)

    c_all = jnp.zeros((lay["mod_rows"], D), F32).at[:B].set(c).at[B].set(c_ctx)
    mod = _modulation(c_all, w_mod, b_mod).reshape(depth * lay["mod_rows"], 6, D)
    tabs = _rope_tables(S, lay["tm"])
    win_bias = _window_bias(S, lay["tq_band"])
    xf = (ctx.reshape(B * n_ctx, D), x.reshape(B * S, D))

    f = None
    for l in range(depth):
        with_ctx = l < depth - 1
        w = _prep_layer(l, params)
        xf, pr = _proj_call(lay, xf, f, mod, mod, l, w, tabs)
        o = {}
        for kind, kl in (("A", "a"), ("B", "b"), ("C", "c"), ("D", "d")):
            q, k, v = pr["q" + kl], pr["k" + kl], pr["v" + kl]
            sink = w["sink"] if kind == "B" else None
            bias = win_bias if kind == "B" else None
            if kind == "C":
                bias = _neighbourhood_bias(w["rpb"], S // GRID_W, lay["tq_band"])
            o[kind] = _attn_latent_call(lay, kind, q, k, v, sink=sink, bias=bias)
            if with_ctx:
                o[kind] = _attn_context_call(lay, kind, q, k, v, o[kind], sink=sink)
        xf, h2, route = _merge_call(lay, l, pr["h"], o, xf, mod, w, with_ctx=with_ctx)
        f = _moe_call(lay, route, h2, w, with_ctx=with_ctx)
    out = _final_call(lay, depth - 1, xf, f, mod, g_final[None, :])
    return out.reshape(B, S, D)
```

```python
import functools

import numpy as np
import jax
import jax.numpy as jnp
from jax import lax
from jax.experimental import pallas as pl
from jax.experimental.pallas import tpu as pltpu

F32 = jnp.float32
BF16 = jnp.bfloat16
HIGHEST = lax.Precision.HIGHEST

GRID_W = 64
ROPE_THETA = 10000.0
EPS = 1e-6
NEG_INF = -1e30
LOG2E = 1.4426950408889634
HEAD_DIM = 64
N_HEADS = 4
BRANCH_W = 256
A_Q_RANK = 256
A_KV_RANK = 128
A_NOPE = 64
A_ROPE = 32
NA_KH = 8
NA_KW = 16
WINDOW = 128
N_GROUPS = 4
EXPERTS_PER_GROUP = 8
N_EXPERTS = 32
EXPERT_FF = 256

LANES = 128
SUBLANES = 8
ROW_BLK = 256
MERGE_TN = 256
MOE_EXPERTS_PER_STEP = 2
VMEM_LIMIT = 56 * 1024 * 1024

_PROJ_GROUPS = (("cq", 256), ("ckv", 128), ("kr", 128), ("qb", 256), ("kb", 128), ("vb", 128),
                ("qc", 256), ("kc", 256), ("vc", 256), ("qd", 256), ("kd", 128), ("vd", 128))
_PROJ_OFF = {}
_o = 0
for _n, _w in _PROJ_GROUPS:
    _PROJ_OFF[_n] = (_o, _w)
    _o += _w
PROJ_COLS = _o
_GQA_PERM = (0, 2, 1, 3)


def _cparams(sem):
    return pltpu.CompilerParams(dimension_semantics=sem, vmem_limit_bytes=VMEM_LIMIT)


def _lane_iota(shape):
    return lax.broadcasted_iota(jnp.int32, shape, len(shape) - 1)


def _sigmoid(x):
    return 1.0 / (1.0 + jnp.exp(-x))


def _mod_kernel(c_ref, w_ref, b_ref, o_ref):
    cf = c_ref[...]
    s = cf * _sigmoid(cf)
    o_ref[0] = jnp.dot(s, w_ref[0], precision=HIGHEST, preferred_element_type=F32) + b_ref[0]


def _modulation(c_all, w_mod, b_mod):
    n_layers, d, n_out = w_mod.shape
    rows = c_all.shape[0]
    tn = 1536
    return pl.pallas_call(
        _mod_kernel,
        out_shape=jax.ShapeDtypeStruct((n_layers, rows, n_out), F32),
        grid=(n_layers, n_out // tn),
        in_specs=[pl.BlockSpec((rows, d), lambda l, j: (0, 0)),
                  pl.BlockSpec((1, d, tn), lambda l, j: (l, 0, j)),
                  pl.BlockSpec((1, 1, tn), lambda l, j: (l, 0, j))],
        out_specs=pl.BlockSpec((1, rows, tn), lambda l, j: (l, 0, j)),
        compiler_params=_cparams(("arbitrary", "arbitrary")),
        name="modulation",
    )(c_all, w_mod, b_mod.reshape(n_layers, 1, n_out))


def _rms(x):
    return x * lax.rsqrt(jnp.mean(x * x, axis=-1, keepdims=True) + EPS)


def _swap_blocks(x, blk):
    lane = _lane_iota(x.shape)
    up = pltpu.roll(x, LANES - blk, 1)
    dn = pltpu.roll(x, blk, 1)
    return jnp.where((lane // blk) % 2 == 0, up, dn)


def _rope(x, cos, sin, blk):
    return x * cos + _swap_blocks(x, blk) * sin


def _pair_norm(x, g):
    lo = _lane_iota(x.shape) < HEAD_DIM
    sq = x * x
    s_lo = jnp.sum(jnp.where(lo, sq, 0.0), axis=-1, keepdims=True)
    s_hi = jnp.sum(jnp.where(lo, 0.0, sq), axis=-1, keepdims=True)
    ms = jnp.where(lo, s_lo, s_hi) * (1.0 / HEAD_DIM)
    return x * lax.rsqrt(ms + EPS) * g


def _load_token_major(ref, rows):
    return jnp.concatenate(
        [ref[pl.ds(c, rows, stride=SUBLANES), :] for c in range(SUBLANES)], axis=-1)


def _tile_rows(xc_ref, xl_ref, is_ctx):
    return jnp.where(is_ctx, xc_ref[...], xl_ref[...])


def _proj_kernel(*refs, with_f, tm, scale_a, nct):
    it = iter(refs)
    if with_f:
        x_ref = next(it)
        f_ref = next(it)
        modp_ref = next(it)
    else:
        xc_ref, xl_ref = next(it), next(it)
    mod_ref = next(it)
    gmix_ref, win_ref, gqa_ref, wqb_ref, gkva_ref, wkvb_ref, gqd_ref, gkd_ref = (next(it) for _ in range(8))
    cosh_ref, sinh_ref, cosa_ref, sina_ref = (next(it) for _ in range(4))
    if with_f:
        x2_ref = next(it)
    h_ref = next(it)
    qa_ref, ka_ref, va_ref, qb_ref, kb_ref, vb_ref, qc_ref, kc_ref, vc_ref, qd_ref, kd_ref, vd_ref = (
        next(it) for _ in range(12))

    if with_f:
        xf = x_ref[...] + modp_ref[0, 5:6, :] * _load_token_major(f_ref, tm)
        x2_ref[...] = xf
    else:
        xf = _tile_rows(xc_ref, xl_ref, pl.program_id(0) < nct)
    h = _rms(xf) * gmix_ref[...] * (1.0 + mod_ref[0, 1:2, :]) + mod_ref[0, 0:1, :]
    hb = h.astype(BF16)
    h_ref[...] = hb
    p = jnp.dot(hb, win_ref[...], preferred_element_type=F32)

    def grp(name):
        o, w = _PROJ_OFF[name]
        return p[:, o:o + w]

    cosh, sinh = cosh_ref[...], sinh_ref[...]
    cosa, sina = cosa_ref[...], sina_ref[...]

    cq = (_rms(grp("cq")) * gqa_ref[...]).astype(BF16)
    qa = jnp.dot(cq, wqb_ref[...], preferred_element_type=F32)
    for hd in range(N_HEADS):
        sl = slice(hd * LANES, (hd + 1) * LANES)
        qa_ref[:, sl] = (_rope(qa[:, sl], cosa, sina, 8) * scale_a).astype(BF16)
    ckv = (_rms(grp("ckv")) * gkva_ref[...]).astype(BF16)
    kva = jnp.dot(ckv, wkvb_ref[...], preferred_element_type=F32)
    kr = _rope(grp("kr"), cosa, sina, 8)
    for hd in range(N_HEADS):
        sl = slice(hd * LANES, (hd + 1) * LANES)
        ka_ref[:, sl] = (kva[:, sl] + kr).astype(BF16)
    va_ref[...] = kva[:, N_HEADS * LANES:].astype(BF16)

    qb = grp("qb")
    for j in range(2):
        sl = slice(j * LANES, (j + 1) * LANES)
        qb_ref[:, sl] = _rope(qb[:, sl], cosh, sinh, 16).astype(BF16)
    kb_ref[...] = _rope(grp("kb"), cosh, sinh, 16).astype(BF16)
    vb_ref[...] = grp("vb").astype(BF16)

    qc_ref[...] = grp("qc").astype(BF16)
    kc_ref[...] = grp("kc").astype(BF16)
    vc_ref[...] = grp("vc").astype(BF16)

    qd = grp("qd")
    for j in range(2):
        sl = slice(j * LANES, (j + 1) * LANES)
        qd_ref[:, sl] = _rope(_pair_norm(qd[:, sl], gqd_ref[...]), cosh, sinh, 16).astype(BF16)
    kd_ref[...] = _rope(_pair_norm(grp("kd"), gkd_ref[...]), cosh, sinh, 16).astype(BF16)
    vd_ref[...] = grp("vd").astype(BF16)


def _split_row_specs(lay, width, off=0):
    tm, nct = lay["tm"], lay["nct"]
    return [pl.BlockSpec((tm, width), lambda i: (jnp.minimum(i + off, nct - 1), 0)),
            pl.BlockSpec((tm, width), lambda i: (jnp.maximum(i + off - nct, 0), 0))]


def _proj_call(lay, x, f, modp, mod, layer, w, tabs):
    T, D, tm = lay["T"], lay["D"], lay["tm"]
    nct, tpb, B = lay["nct"], lay["tpb"], lay["B"]
    with_f = f is not None
    n_tiles = T // tm

    def mod_row(i):
        return jnp.where(i < nct, B, (i - nct) // tpb)

    def tab_blk(i):
        return jnp.where(i < nct, tpb, (i - nct) % tpb)

    row = lambda i: (i, 0)
    const = lambda i: (0, 0)
    if with_f:
        in_specs = [pl.BlockSpec((tm, D), row),
                    pl.BlockSpec((tm * SUBLANES, LANES), row),
                    pl.BlockSpec((1, 6, D), lambda i: ((layer - 1) * lay["mod_rows"] + mod_row(i), 0, 0))]
        args = [x, f, modp]
    else:
        in_specs = _split_row_specs(lay, D)
        args = list(x)
    in_specs += [pl.BlockSpec((1, 6, D), lambda i: (layer * lay["mod_rows"] + mod_row(i), 0, 0))]
    args += [mod]
    for name in ("g_mix", "w_in", "g_q_a", "w_q_b", "g_kv_a", "w_kv_b", "g_q_d", "g_k_d"):
        a = w[name]
        in_specs.append(pl.BlockSpec(a.shape, const))
        args.append(a)
    for tname in ("cos_h", "sin_h", "cos_a", "sin_a"):
        in_specs.append(pl.BlockSpec((tm, LANES), lambda i: (tab_blk(i), 0)))
        args.append(tabs[tname])

    widths = [("h", D), ("qa", 512), ("ka", 512), ("va", 256), ("qb", 256), ("kb", 128), ("vb", 128),
              ("qc", 256), ("kc", 256), ("vc", 256), ("qd", 256), ("kd", 128), ("vd", 128)]
    out_shape, out_specs = [], []
    if with_f:
        out_shape.append(jax.ShapeDtypeStruct((T, D), F32))
        out_specs.append(pl.BlockSpec((tm, D), row))
    for _, wd in widths:
        out_shape.append(jax.ShapeDtypeStruct((T, wd), BF16))
        out_specs.append(pl.BlockSpec((tm, wd), row))

    outs = pl.pallas_call(
        functools.partial(_proj_kernel, with_f=with_f, tm=tm, nct=nct,
                          scale_a=float((A_NOPE + A_ROPE) ** -0.5 * LOG2E)),
        out_shape=out_shape, grid=(n_tiles,), in_specs=in_specs, out_specs=out_specs,
        compiler_params=_cparams(("parallel",)),
        name="proj_in",
    )(*args)
    outs = list(outs)
    x2 = outs.pop(0) if with_f else x
    names = [n for n, _ in widths]
    return x2, dict(zip(names, outs))


_NT = (((1,), (1,)), ((), ()))


def _head_plan(kind):
    if kind == "A":
        return tuple((r, None, r, r // 2) for r in range(N_HEADS))
    if kind == "C":
        return tuple((r // 2, r % 2, r // 2, r // 2) for r in range(N_HEADS))
    return tuple((r // 2, r % 2, 0, 0) for r in range(N_HEADS))


def _head_query(q_ref, plan_r):
    qt, half, _, _ = plan_r
    src = q_ref[:, qt * LANES:(qt + 1) * LANES]
    if half is None:
        return src
    lane = _lane_iota(src.shape)
    keep = (lane < HEAD_DIM) if half == 0 else (lane >= HEAD_DIM)
    return jnp.where(keep, src, jnp.zeros_like(src))


def _score_chunks(q, k_blocks, bias_blocks):
    chunks = []
    for kb, bb in zip(k_blocks, bias_blocks):
        s = lax.dot_general(q, kb, _NT, preferred_element_type=F32)
        if bb is not None:
            s = s + bb
        chunks += [s[:, c * LANES:(c + 1) * LANES] for c in range(s.shape[1] // LANES)]
    return chunks


def _row_max(chunks):
    m = functools.reduce(jnp.maximum, chunks)
    return jnp.broadcast_to(jnp.max(m, axis=-1, keepdims=True), m.shape)


def _weighted_values(p_chunks, v_blocks, half):
    pv, idx = None, 0
    for vb in v_blocks:
        n = vb.shape[0] // LANES
        p = jnp.concatenate(p_chunks[idx:idx + n], axis=1).astype(BF16)
        idx += n
        lane = _lane_iota(vb.shape)
        own = (lane < HEAD_DIM) if half == 0 else (lane >= HEAD_DIM)
        d = jnp.dot(p, jnp.where(own, vb, jnp.ones_like(vb)), preferred_element_type=F32)
        pv = d if pv is None else pv + d
    return pv


def _softmax_once(q, k_blocks, v_blocks, bias_blocks, sink, half):
    chunks = _score_chunks(q, k_blocks, bias_blocks)
    m = _row_max(chunks)
    if sink is not None:
        m = jnp.maximum(m, sink)
    p = [jnp.exp2(c - m) for c in chunks]
    pv = _weighted_values(p, v_blocks, half)
    l = pltpu.roll(pv, HEAD_DIM, 1)
    if sink is not None:
        l = l + jnp.exp2(sink - m)
    return pv / l


def _store_heads(o_ref, outs):
    lane = _lane_iota(outs[0].shape)
    for g in range(2):
        o_ref[:, g * LANES:(g + 1) * LANES] = jnp.where(
            lane < HEAD_DIM, outs[2 * g], outs[2 * g + 1]).astype(o_ref.dtype)


def _attn_band_kernel(*refs, kind, lb, n_kv, tq):
    it = iter(refs)
    q_ref = next(it)
    k_refs = [next(it) for _ in range(n_kv)]
    v_refs = [next(it) for _ in range(n_kv)]
    bias_ref = next(it) if n_kv > 1 else None
    sink_ref = next(it) if kind == "B" else None
    o_ref = next(it)
    t = pl.program_id(0) % lb
    var = jnp.where(t == 0, 0, jnp.where(t == lb - 1, 2, 1))
    outs = []
    for r, plan_r in enumerate(_head_plan(kind)):
        _, _, kt, vt = plan_r
        ksl = slice(kt * LANES, (kt + 1) * LANES)
        vsl = slice(vt * LANES, (vt + 1) * LANES)
        bias = [None]
        for j in range(n_kv - 1):
            if kind == "C":
                bias.append(bias_ref[var, j, r * tq:(r + 1) * tq, :])
            else:
                bias.append(bias_ref[var, j])
        sink = sink_ref[_GQA_PERM[r]] * LOG2E if kind == "B" else None
        outs.append(_softmax_once(_head_query(q_ref, plan_r), [k[:, ksl] for k in k_refs],
                                  [v[:, vsl] for v in v_refs], bias, sink, r % 2))
    _store_heads(o_ref, outs)


def _attn_dense_kernel(q_ref, kc_ref, vc_ref, kl_ref, vl_ref, o_ref, qs_ref, m_ref, acc_ref,
                       *, kind, n_steps, tq):
    s = pl.program_id(1)
    plan = _head_plan(kind)

    def update(first):
        for r, (_, _, kt, vt) in enumerate(plan):
            rows = slice(r * tq, (r + 1) * tq)
            ksl = slice(kt * LANES, (kt + 1) * LANES)
            vsl = slice(vt * LANES, (vt + 1) * LANES)
            if first:
                q = _head_query(q_ref, plan[r])
                qs_ref[rows, :] = q
                k_blocks = [kc_ref[:, ksl], kl_ref[:, ksl]]
                v_blocks = [vc_ref[:, vsl], vl_ref[:, vsl]]
            else:
                q = qs_ref[rows, :]
                k_blocks = [kl_ref[:, ksl]]
                v_blocks = [vl_ref[:, vsl]]
            chunks = _score_chunks(q, k_blocks, [None] * len(k_blocks))
            m_cur = _row_max(chunks)
            if first:
                m_new = m_cur
            else:
                m_prev = m_ref[rows, :]
                m_new = jnp.maximum(m_prev, m_cur)
                alpha = jnp.exp2(m_prev - m_new)
            p = [jnp.exp2(c - m_new) for c in chunks]
            pv = _weighted_values(p, v_blocks, r % 2)
            if first:
                acc_ref[rows, :] = pv
            else:
                acc_ref[rows, :] = alpha * acc_ref[rows, :] + pv
            m_ref[rows, :] = m_new

    @pl.when(s == 0)
    def _():
        update(True)

    @pl.when(s > 0)
    def _():
        update(False)

    @pl.when(s == n_steps - 1)
    def _():
        outs = []
        for r in range(N_HEADS):
            rows = slice(r * tq, (r + 1) * tq)
            acc = acc_ref[rows, :]
            outs.append(acc / pltpu.roll(acc, HEAD_DIM, 1))
        _store_heads(o_ref, outs)


def _attn_latent_call(lay, kind, q, k, v, *, sink=None, bias=None):
    T, B, ncb, lb, S = lay["T"], lay["B"], lay["ncb"], lay["lb"], lay["S"]
    tq = ROW_BLK
    qw, kw, vw = q.shape[1], k.shape[1], v.shape[1]
    out_shape = jax.ShapeDtypeStruct((B * S, BRANCH_W), BF16)
    if kind in ("A", "D"):
        tk, tq = lay["tk_dense"], lay["tq_dense"]
        n_steps = S // tk
        lat0 = ncb * ROW_BLK // tk
        q0 = ncb * ROW_BLK // tq
        qpb = S // tq
        lat_blk = lambda i, s: (lat0 + (i // qpb) * n_steps + s, 0)
        return pl.pallas_call(
            functools.partial(_attn_dense_kernel, kind=kind, n_steps=n_steps, tq=tq),
            out_shape=out_shape,
            grid=(B * qpb, n_steps),
            in_specs=[pl.BlockSpec((tq, qw), lambda i, s: (q0 + i, 0)),
                      pl.BlockSpec((ROW_BLK, kw), lambda i, s: (i // qpb, 0)),
                      pl.BlockSpec((ROW_BLK, vw), lambda i, s: (i // qpb, 0)),
                      pl.BlockSpec((tk, kw), lat_blk),
                      pl.BlockSpec((tk, vw), lat_blk)],
            out_specs=pl.BlockSpec((tq, BRANCH_W), lambda i, s: (i, 0)),
            scratch_shapes=[pltpu.VMEM((N_HEADS * tq, LANES), BF16),
                            pltpu.VMEM((N_HEADS * tq, LANES), F32),
                            pltpu.VMEM((N_HEADS * tq, LANES), F32)],
            compiler_params=_cparams(("parallel", "arbitrary")),
            name="attn_" + kind,
        )(q, k, v, k, v)

    tq = lay["tq_band"]
    qpb = S // tq
    q0 = ncb * ROW_BLK // tq
    bpq = tq // ROW_BLK

    def nb(i, d):
        return (ncb + (i // qpb) * lb + jnp.clip((i % qpb) * bpq + d, 0, lb - 1), 0)

    kv_maps = [lambda i: (i // qpb, 0)] + [functools.partial(nb, d=d) for d in _band_offsets(tq)]
    n_kv = len(kv_maps)
    in_specs = [pl.BlockSpec((tq, qw), lambda i: (q0 + i, 0))]
    in_specs += [pl.BlockSpec((ROW_BLK, kw), m) for m in kv_maps]
    in_specs += [pl.BlockSpec((ROW_BLK, vw), m) for m in kv_maps]
    in_specs.append(pl.BlockSpec(bias.shape, lambda i: (0,) * bias.ndim, pipeline_mode=pl.Buffered(1)))
    args = [q] + [k] * n_kv + [v] * n_kv + [bias]
    if kind == "B":
        in_specs.append(pl.BlockSpec(memory_space=pltpu.SMEM))
        args.append(sink)
    return pl.pallas_call(
        functools.partial(_attn_band_kernel, kind=kind, lb=qpb, n_kv=n_kv, tq=tq),
        out_shape=out_shape,
        grid=(B * qpb,),
        in_specs=in_specs,
        out_specs=pl.BlockSpec((tq, BRANCH_W), lambda i: (i, 0)),
        compiler_params=_cparams(("parallel",)),
        name="attn_" + kind,
    )(*args)


def _attn_context_call(lay, kind, q, k, v, *, sink=None):
    ncb = lay["ncb"]
    qw, kw, vw = q.shape[1], k.shape[1], v.shape[1]
    blk = lambda i: (i, 0)
    in_specs = [pl.BlockSpec((ROW_BLK, qw), blk), pl.BlockSpec((ROW_BLK, kw), blk),
                pl.BlockSpec((ROW_BLK, vw), blk)]
    args = [q, k, v]
    if kind == "B":
        in_specs.append(pl.BlockSpec(memory_space=pltpu.SMEM))
        args.append(sink)
    return pl.pallas_call(
        functools.partial(_attn_band_kernel, kind=kind, lb=1, n_kv=1, tq=ROW_BLK),
        out_shape=jax.ShapeDtypeStruct((ncb * ROW_BLK, BRANCH_W), BF16),
        grid=(ncb,),
        in_specs=in_specs,
        out_specs=pl.BlockSpec((ROW_BLK, BRANCH_W), blk),
        compiler_params=_cparams(("parallel",)),
        name="attn_ctx_" + kind,
    )(*args)


def _band_offsets(tq):
    return tuple(range(-1, tq // ROW_BLK + 1))


def _window_bias(S, tq):
    lb, qpb, bpq = S // ROW_BLK, S // tq, tq // ROW_BLK
    offs = _band_offsets(tq)
    qa, ka = np.arange(tq), np.arange(ROW_BLK)
    out = np.full((3, len(offs), tq, ROW_BLK), NEG_INF, np.float32)
    for vi, t_rep in enumerate((0, 1, qpb - 1)):
        for di, d in enumerate(offs):
            kt = t_rep * bpq + d
            if not 0 <= kt < lb:
                continue
            qpos = t_rep * tq + qa
            kpos = kt * ROW_BLK + ka
            ok = np.abs(qpos[:, None] - kpos[None, :]) <= WINDOW
            out[vi, di] = np.where(ok, 0.0, NEG_INF)
    return jnp.asarray(out)


def _neighbourhood_bias(rpb, rows_total, tq):
    lb = rows_total * GRID_W // ROW_BLK
    qpb, bpq = rows_total * GRID_W // tq, tq // ROW_BLK
    rpt = ROW_BLK // GRID_W
    rpq = tq // GRID_W
    kh = min(NA_KH, rows_total)
    qa, ka = np.arange(tq), np.arange(ROW_BLK)
    q_sub, q_col = qa // GRID_W, qa % GRID_W
    k_sub, k_col = ka // GRID_W, ka % GRID_W
    n_dr, n_dc = 2 * NA_KH - 1, 2 * NA_KW - 1
    col = np.arange(GRID_W)
    dc = np.clip(col[None, :] - col[:, None], -(NA_KW - 1), NA_KW - 1) + NA_KW - 1
    hot_c = (dc[:, :, None] == np.arange(n_dc)).astype(np.float32)
    by_col = jnp.einsum("huv,cdv->hucd", rpb.astype(F32) * LOG2E, jnp.asarray(hot_c),
                        precision=HIGHEST)
    offs = _band_offsets(tq)
    vals = []
    for d in offs:
        dr = np.clip(d * rpt + np.arange(rpt)[None, :] - np.arange(rpq)[:, None],
                     -(NA_KH - 1), NA_KH - 1) + NA_KH - 1
        hot_r = (dr[:, :, None] == np.arange(n_dr)).astype(np.float32)
        v = jnp.einsum("abu,hucd->hacbd", jnp.asarray(hot_r), by_col, precision=HIGHEST)
        vals.append(v.reshape(N_HEADS, tq, ROW_BLK))
    out = []
    for t_rep in (0, 1, qpb - 1):
        per_block = []
        for di, d in enumerate(offs):
            kt = t_rep * bpq + d
            q_row = t_rep * rpq + q_sub
            k_row = kt * rpt + k_sub
            r_start = np.clip(q_row - kh // 2, 0, rows_total - kh)
            row_ok = (k_row[None] >= r_start[:, None]) & (k_row[None] < r_start[:, None] + kh)
            c_start = np.clip(q_col - NA_KW // 2, 0, GRID_W - NA_KW)
            col_ok = (k_col[None] >= c_start[:, None]) & (k_col[None] < c_start[:, None] + NA_KW)
            ok = row_ok & col_ok & (0 <= kt < lb)
            per_block.append(jnp.where(jnp.asarray(ok)[None], vals[di], NEG_INF).reshape(-1, ROW_BLK))
        out.append(jnp.stack(per_block))
    return jnp.stack(out)


def _merge_kernel(*refs, tm, split_x, n_ctx_tiles):
    it = iter(refs)
    h_ref = next(it)
    is_ctx = pl.program_id(0) < n_ctx_tiles
    if n_ctx_tiles > 0:
        o_refs = [(next(it), next(it)) for _ in range(4)]
        o_vals = [_tile_rows(oc, ol, is_ctx) for oc, ol in o_refs]
    else:
        o_vals = [next(it)[...] for _ in range(4)]
    if split_x:
        xc_ref, xl_ref = next(it), next(it)
    else:
        x_ref = next(it)
    (mod_ref, gffn_ref, wg_ref, bg_ref, wb_ref, wout_ref, wr_ref, br_ref,
     x1_ref, h2_ref, route_ref, y_ref) = (next(it) for _ in range(12))
    hb = h_ref[...]
    d_model = hb.shape[1]
    for t in range(d_model // MERGE_TN):
        cs = slice(t * MERGE_TN, (t + 1) * MERGE_TN)
        y = None
        for n, o_n in enumerate(o_vals):
            gate = _sigmoid(jnp.dot(hb, wg_ref[n, :, cs], preferred_element_type=F32) + bg_ref[n, :, cs])
            u = gate * jnp.dot(o_n, wb_ref[n, :, cs], preferred_element_type=F32)
            y = u if y is None else y + u
        y_ref[:, cs] = y.astype(BF16)
    z = jnp.dot(y_ref[...], wout_ref[...], preferred_element_type=F32)
    x_in = _tile_rows(xc_ref, xl_ref, is_ctx) if split_x else x_ref[...]
    x1 = x_in + mod_ref[0, 2:3, :] * z
    x1_ref[...] = x1
    h2 = _rms(x1) * gffn_ref[...] * (1.0 + mod_ref[0, 4:5, :]) + mod_ref[0, 3:4, :]
    for c in range(SUBLANES):
        h2_ref[pl.ds(c, tm, stride=SUBLANES), :] = h2[:, c * LANES:(c + 1) * LANES]

    h2_hi = h2.astype(BF16)
    h2_lo = (h2 - h2_hi.astype(F32)).astype(BF16)
    logit = (jnp.dot(h2_hi, wr_ref[0], preferred_element_type=F32)
             + jnp.dot(h2_lo, wr_ref[0], preferred_element_type=F32)
             + jnp.dot(h2_hi, wr_ref[1], preferred_element_type=F32)) + br_ref[...]
    lane = _lane_iota(logit.shape)
    big = jnp.int32(1 << 20)
    is_g = (lane >= N_EXPERTS) & (lane < N_EXPERTS + N_GROUPS)
    gl = jnp.where(is_g, logit, NEG_INF)
    gmax = jnp.max(gl, axis=-1, keepdims=True)
    gsel = jnp.min(jnp.where(gl == gmax, lane - N_EXPERTS, big), axis=-1, keepdims=True)
    gw = 1.0 / jnp.sum(jnp.where(is_g, jnp.exp(gl - gmax), 0.0), axis=-1, keepdims=True)
    in_grp = (lane < N_EXPERTS) & ((lane // EXPERTS_PER_GROUP) == gsel)
    el = jnp.where(in_grp, logit, NEG_INF)
    v1 = jnp.max(el, axis=-1, keepdims=True)
    i1 = jnp.min(jnp.where(el == v1, lane, big), axis=-1, keepdims=True)
    el2 = jnp.where(lane == i1, NEG_INF, el)
    v2 = jnp.max(el2, axis=-1, keepdims=True)
    i2 = jnp.min(jnp.where(el2 == v2, lane, big), axis=-1, keepdims=True)
    e21 = jnp.exp(v2 - v1)
    w1 = gw / (1.0 + e21)
    w2 = gw * e21 / (1.0 + e21)
    route_ref[...] = jnp.where(lane == 0, i1.astype(F32),
                               jnp.where(lane == 1, i2.astype(F32),
                                         jnp.where(lane == 2, w1, jnp.where(lane == 3, w2, 0.0))))


def _merge_call(lay, layer, h, o, x, mod, w, *, with_ctx):
    T, D, tm = lay["T"], lay["D"], lay["tm"]
    nct, tpb, B = lay["nct"], lay["tpb"], lay["B"]
    off = 0 if with_ctx else nct
    n_tiles = T // tm - off

    def mod_row(i):
        return jnp.where(i < nct, B, (i - nct) // tpb)

    row = lambda i: (i + off, 0)
    const2 = lambda i: (0, 0)
    const3 = lambda i: (0, 0, 0)
    split_x = isinstance(x, (tuple, list))
    in_specs = [pl.BlockSpec((tm, D), row)]
    o_args = []
    for kind in ("A", "B", "C", "D"):
        o_ctx, o_lat = o[kind]
        if with_ctx:
            in_specs += _split_row_specs(lay, BRANCH_W)
            o_args += [o_ctx, o_lat]
        else:
            in_specs.append(pl.BlockSpec((tm, BRANCH_W), lambda i: (i, 0)))
            o_args.append(o_lat)
    in_specs += _split_row_specs(lay, D, off) if split_x else [pl.BlockSpec((tm, D), row)]
    in_specs += [pl.BlockSpec((1, 6, D), lambda i: (layer * lay["mod_rows"] + mod_row(i + off), 0, 0)),
                 pl.BlockSpec((1, D), const2),
                 pl.BlockSpec(w["w_gate"].shape, const3),
                 pl.BlockSpec(w["b_gate"].shape, const3),
                 pl.BlockSpec(w["w_branch"].shape, const3),
                 pl.BlockSpec((D, D), const2),
                 pl.BlockSpec((2, D, LANES), const3),
                 pl.BlockSpec((1, LANES), const2)]
    out_shape = [jax.ShapeDtypeStruct((T, D), F32),
                 jax.ShapeDtypeStruct((T * SUBLANES, LANES), F32),
                 jax.ShapeDtypeStruct((T, LANES), F32)]
    out_specs = [pl.BlockSpec((tm, D), row),
                 pl.BlockSpec((tm * SUBLANES, LANES), row),
                 pl.BlockSpec((tm, LANES), row)]
    return pl.pallas_call(
        functools.partial(_merge_kernel, tm=tm, split_x=split_x, n_ctx_tiles=nct - off),
        out_shape=out_shape, grid=(n_tiles,), in_specs=in_specs, out_specs=out_specs,
        scratch_shapes=[pltpu.VMEM((tm, D), BF16)],
        compiler_params=_cparams(("parallel",)),
        name="merge",
    )(h, *o_args, *(x if split_x else (x,)), mod, w["g_ffn"], w["w_gate"],
      w["b_gate"], w["w_branch"], w["w_out"], w["w_route"], w["b_route"])


def _moe_kernel(tab_ref, tok_ref, h2_ref, sw_ref, w1_ref, w3_ref, w2_ref, f_ref,
                xg_ref, y_ref, st_ref, *, blk, slots, tile_off):
    ti = pl.program_id(0)
    step = pl.program_id(1)

    @pl.when(step == 0)
    def _():
        f_ref[...] = jnp.zeros_like(f_ref)
        xg_ref[...] = jnp.zeros_like(xg_ref)

    tok_base = (ti + tile_off) * slots
    for j in range(MOE_EXPERTS_PER_STEP):
        base = ((ti + tile_off) * N_EXPERTS + step * MOE_EXPERTS_PER_STEP + j) * 2
        _moe_expert(tab_ref[base], tab_ref[base + 1], tok_base, tok_ref, h2_ref, sw_ref,
                    w1_ref.at[j], w3_ref.at[j], w2_ref.at[j], f_ref, xg_ref, y_ref, st_ref, blk)


def _moe_expert(seg0, n_pad, tok_base, tok_ref, h2_ref, sw_ref, w1_ref, w3_ref, w2_ref, f_ref,
                xg_ref, y_ref, st_ref, blk):
    n_blocks = (n_pad + blk - 1) // blk

    def block_body(b, carry):
        off = pl.multiple_of(seg0 + b * blk, SUBLANES)
        n_grp = jnp.minimum(n_pad - b * blk, blk) // SUBLANES

        def gather(gi, c):
            for u in range(SUBLANES):
                r = gi * SUBLANES + u
                tok = tok_ref[tok_base + off + r]
                src = pl.multiple_of(tok * SUBLANES, SUBLANES)
                dst = pl.multiple_of(r * SUBLANES, SUBLANES)
                xg_ref[pl.ds(dst, SUBLANES), :] = h2_ref[pl.ds(src, SUBLANES), :]
            return c

        lax.fori_loop(0, n_grp, gather, 0)
        xb = _load_token_major(xg_ref, blk).astype(BF16)
        a = jnp.dot(xb, w1_ref[...], preferred_element_type=F32)
        g = jnp.dot(xb, w3_ref[...], preferred_element_type=F32)
        hid = (a * _sigmoid(a) * g).astype(BF16)
        y = jnp.dot(hid, w2_ref[...], preferred_element_type=F32)
        y_ref[...] = y * sw_ref[0, pl.ds(off, blk), :]

        def scatter(gi, c):
            r0 = pl.multiple_of(gi * SUBLANES, SUBLANES)
            for ch in range(SUBLANES):
                st_ref[pl.ds(ch, SUBLANES, stride=SUBLANES), :] = y_ref[pl.ds(r0, SUBLANES),
                                                                        ch * LANES:(ch + 1) * LANES]
            dsts, vals = [], []
            for u in range(SUBLANES):
                tok = tok_ref[tok_base + off + r0 + u]
                dsts.append(pl.multiple_of(tok * SUBLANES, SUBLANES))
                vals.append(f_ref[pl.ds(dsts[u], SUBLANES), :] + st_ref[u * SUBLANES:(u + 1) * SUBLANES, :])
            for u in reversed(range(SUBLANES)):
                f_ref[pl.ds(dsts[u], SUBLANES), :] = vals[u]
            return c

        lax.fori_loop(0, n_grp, scatter, 0)
        return carry

    lax.fori_loop(0, n_blocks, block_body, 0)


def _moe_call(lay, route, h2, w, *, with_ctx):
    T, D, tt = lay["T"], lay["D"], lay["tt"]
    blk = lay["moe_blk"]
    n_tiles_all = T // tt
    tile_off = 0 if with_ctx else lay["ncb"] * ROW_BLK // tt
    n_tiles = n_tiles_all - tile_off
    n_assign = 2 * tt
    n_fill = N_EXPERTS * SUBLANES
    slots = n_assign + n_fill + blk

    eid = route[:, 0:2].astype(jnp.int32).reshape(n_tiles_all, n_assign)
    wts = route[:, 2:4].reshape(n_tiles_all, n_assign)
    tok = jnp.broadcast_to(jnp.arange(n_assign, dtype=jnp.int32)[None, :] // 2, eid.shape)
    ex = jnp.arange(N_EXPERTS, dtype=jnp.int32)
    counts = jnp.sum(eid[:, :, None] == ex[None, None, :], axis=1, dtype=jnp.int32)
    n_dummy = (-counts) % SUBLANES
    fill_key = jnp.where(jnp.arange(SUBLANES, dtype=jnp.int32)[None, None, :] < n_dummy[:, :, None],
                         ex[None, :, None], N_EXPERTS).reshape(n_tiles_all, n_fill)
    zeros_i = jnp.zeros((n_tiles_all, n_fill), jnp.int32)
    _, slot_tok, slot_w = lax.sort(
        (jnp.concatenate([eid, fill_key], axis=1), jnp.concatenate([tok, zeros_i], axis=1),
         jnp.concatenate([wts, zeros_i.astype(F32)], axis=1)),
        dimension=1, is_stable=True, num_keys=1)
    slot_tok = jnp.pad(slot_tok, ((0, 0), (0, blk)))
    slot_w = jnp.pad(slot_w, ((0, 0), (0, blk)))
    padded = counts + n_dummy
    seg0 = jnp.cumsum(padded, axis=1) - padded
    tab = jnp.stack([seg0, padded], axis=-1).reshape(-1).astype(jnp.int32)

    return pl.pallas_call(
        functools.partial(_moe_kernel, blk=blk, slots=slots, tile_off=tile_off),
        out_shape=jax.ShapeDtypeStruct((T * SUBLANES, LANES), F32),
        grid_spec=pltpu.PrefetchScalarGridSpec(
            num_scalar_prefetch=2,
            grid=(n_tiles, N_EXPERTS // MOE_EXPERTS_PER_STEP),
            in_specs=[
                pl.BlockSpec((tt * SUBLANES, LANES), lambda t, e, *_: (t + tile_off, 0)),
                pl.BlockSpec((1, slots, 1), lambda t, e, *_: (t + tile_off, 0, 0)),
                pl.BlockSpec((MOE_EXPERTS_PER_STEP, D, EXPERT_FF), lambda t, e, *_: (e, 0, 0)),
                pl.BlockSpec((MOE_EXPERTS_PER_STEP, D, EXPERT_FF), lambda t, e, *_: (e, 0, 0)),
                pl.BlockSpec((MOE_EXPERTS_PER_STEP, EXPERT_FF, D), lambda t, e, *_: (e, 0, 0)),
            ],
            out_specs=pl.BlockSpec((tt * SUBLANES, LANES), lambda t, e, *_: (t + tile_off, 0)),
            scratch_shapes=[pltpu.VMEM((blk * SUBLANES, LANES), F32),
                            pltpu.VMEM((blk, D), F32),
                            pltpu.VMEM((SUBLANES * SUBLANES, LANES), F32)],
        ),
        compiler_params=_cparams(("parallel", "arbitrary")),
        name="moe_experts",
    )(tab, slot_tok.reshape(-1), h2, slot_w.reshape(n_tiles_all, slots, 1),
      w["w_ff1"], w["w_ff3"], w["w_ff2"])


def _final_kernel(x_ref, f_ref, mod_ref, g_ref, o_ref, *, tm):
    xf = x_ref[...] + mod_ref[0, 5:6, :] * _load_token_major(f_ref, tm)
    o_ref[...] = _rms(xf) * g_ref[...]


def _final_call(lay, layer, x1, f, mod, g_final):
    T, D, tm = lay["T"], lay["D"], lay["tm"]
    nct, tpb = lay["nct"], lay["tpb"]
    n_lat = T // tm - nct
    return pl.pallas_call(
        functools.partial(_final_kernel, tm=tm),
        out_shape=jax.ShapeDtypeStruct((n_lat * tm, D), F32),
        grid=(n_lat,),
        in_specs=[pl.BlockSpec((tm, D), lambda i: (i + nct, 0)),
                  pl.BlockSpec((tm * SUBLANES, LANES), lambda i: (i + nct, 0)),
                  pl.BlockSpec((1, 6, D), lambda i: (layer * lay["mod_rows"] + i // tpb, 0, 0)),
                  pl.BlockSpec((1, D), lambda i: (0, 0))],
        out_specs=pl.BlockSpec((tm, D), lambda i: (i, 0)),
        compiler_params=_cparams(("parallel",)),
        name="final_norm",
    )(x1, f, mod, g_final)


def _select_cols(wm, segs, scale=None):
    parts = []
    for k, (start, width) in enumerate(segs):
        if start is None:
            parts.append(jnp.zeros((wm.shape[0], width), wm.dtype))
        else:
            blk = wm[:, start:start + width]
            parts.append(blk if scale is None or scale[k] is None else blk * scale[k])
    return jnp.concatenate(parts, axis=1)


def _prep_layer(l, p):
    a_cols = A_Q_RANK + A_KV_RANK + A_ROPE
    b_off = a_cols
    c_off = b_off + 512
    d_off = c_off + 768
    qk_scale = HEAD_DIM ** -0.5 * LOG2E
    gqa_q = lambda off: [(off + hh * HEAD_DIM, HEAD_DIM) for hh in _GQA_PERM]
    segs = ([(0, 256), (256, 128), (None, 64), (384, 32), (None, 32)]
            + gqa_q(b_off) + [(b_off + 256, 128), (b_off + 384, 128)]
            + [(c_off, 256), (c_off + 256, 256), (c_off + 512, 256)]
            + gqa_q(d_off) + [(d_off + 256, 128), (d_off + 384, 128)])
    scale = [None] * len(segs)
    for k in (5, 6, 7, 8, 11):
        scale[k] = qk_scale
    w_in = _select_cols(p["w_in"][l], segs, scale).astype(BF16)
    assert w_in.shape[1] == PROJ_COLS

    hq = A_NOPE + A_ROPE
    segs_q = []
    for hh in range(N_HEADS):
        segs_q += [(hh * hq, hq), (None, LANES - hq)]
    w_q_b = _select_cols(p["w_q_b"][l], segs_q).astype(BF16)
    hk = A_NOPE + HEAD_DIM
    segs_k = []
    for hh in range(N_HEADS):
        segs_k += [(hh * hk, A_NOPE), (None, LANES - A_NOPE)]
    segs_k += [(hh * hk + A_NOPE, HEAD_DIM) for hh in range(N_HEADS)]
    w_kv_b = _select_cols(p["w_kv_b"][l], segs_k).astype(BF16)

    wb = p["w_branch"][l]
    perm_rows = lambda m: jnp.concatenate([m[hh * HEAD_DIM:(hh + 1) * HEAD_DIM] for hh in _GQA_PERM], axis=0)
    w_branch = jnp.stack([wb[0], perm_rows(wb[1]), wb[2], perm_rows(wb[3])]).astype(BF16)

    d = p["w_in"].shape[1]
    w_route = jnp.zeros((d, LANES), F32)
    w_route = w_route.at[:, :N_EXPERTS].set(p["w_router"][l]).at[:, N_EXPERTS:N_EXPERTS + N_GROUPS].set(p["w_group"][l])
    b_route = jnp.zeros((1, LANES), F32)
    b_route = b_route.at[0, :N_EXPERTS].set(p["b_router"][l]).at[0, N_EXPERTS:N_EXPERTS + N_GROUPS].set(p["b_group"][l])
    return {
        "g_mix": p["g_norm_mix"][l][None, :],
        "w_in": w_in,
        "g_q_a": p["g_q_a"][l][None, :],
        "w_q_b": w_q_b,
        "g_kv_a": p["g_kv_a"][l][None, :],
        "w_kv_b": w_kv_b,
        "g_q_d": (jnp.tile(p["g_q_d"][l], 2) * qk_scale)[None, :],
        "g_k_d": jnp.tile(p["g_k_d"][l], 2)[None, :],
        "sink": p["sink_b"][l],
        "rpb": p["rpb_c"][l],
        "w_gate": p["w_gate"][l].astype(BF16),
        "b_gate": p["b_gate"][l][:, None, :],
        "w_branch": w_branch,
        "w_out": p["w_out"][l].astype(BF16),
        "g_ffn": p["g_norm_ffn"][l][None, :],
        "w_route": jnp.stack([w_route.astype(BF16),
                              (w_route - w_route.astype(BF16).astype(F32)).astype(BF16)]),
        "b_route": b_route,
        "w_ff1": p["w_ff1"][l].astype(BF16),
        "w_ff3": p["w_ff3"][l].astype(BF16),
        "w_ff2": p["w_ff2"][l].astype(BF16),
    }


def _rope_tables(S, tm):
    t = np.arange(S)
    rows = (t // GRID_W).astype(np.float32)
    cols = (t % GRID_W).astype(np.float32)

    def cs(rot):
        half = rot // 2
        inv = np.float32(ROPE_THETA) ** (-np.arange(0, half, 2, dtype=np.float32) / np.float32(half))
        ar_, ac_ = rows[:, None] * inv, cols[:, None] * inv
        cos = np.concatenate([np.cos(ar_), np.cos(ar_), np.cos(ac_), np.cos(ac_)], axis=-1)
        sin = np.concatenate([-np.sin(ar_), np.sin(ar_), -np.sin(ac_), np.sin(ac_)], axis=-1)
        return cos.astype(np.float32), sin.astype(np.float32)

    cos64, sin64 = cs(HEAD_DIM)
    cos32, sin32 = cs(A_ROPE)
    ones = lambda n: np.ones((S, n), np.float32)
    zeros = lambda n: np.zeros((S, n), np.float32)
    tabs = {
        "cos_h": np.concatenate([cos64, cos64], axis=-1),
        "sin_h": np.concatenate([sin64, sin64], axis=-1),
        "cos_a": np.concatenate([ones(A_NOPE), cos32, ones(LANES - A_NOPE - A_ROPE)], axis=-1),
        "sin_a": np.concatenate([zeros(A_NOPE), sin32, zeros(LANES - A_NOPE - A_ROPE)], axis=-1),
    }
    ident = {"cos_h": 1.0, "sin_h": 0.0, "cos_a": 1.0, "sin_a": 0.0}
    return {k: jnp.asarray(np.concatenate([v, np.full((tm, LANES), ident[k], np.float32)], axis=0))
            for k, v in tabs.items()}


def _layout(B, S, n_ctx, D):
    assert n_ctx == ROW_BLK and S % 1024 == 0 and S // GRID_W >= 3 * (ROW_BLK // GRID_W)
    T = B * (n_ctx + S)
    tm = 512 if (B * n_ctx) % 512 == 0 else 256
    tk_dense = 1024 if (B * n_ctx) % 1024 == 0 else (512 if (B * n_ctx) % 512 == 0 else 256)
    tt = 2048 if (B * n_ctx) % 2048 == 0 else B * n_ctx
    assert S % tt == 0
    return {
        "B": B, "S": S, "D": D, "T": T, "tm": tm,
        "ncb": B * n_ctx // ROW_BLK,
        "lb": S // ROW_BLK,
        "nct": B * n_ctx // tm,
        "tpb": S // tm,
        "tk_dense": tk_dense,
        "tq_dense": tk_dense,
        "tq_band": tm,
        "tt": tt, "moe_blk": 160,
        "mod_rows": 16,
    }


def kernel(x, c, ctx, c_ctx, w_mod, b_mod, g_norm_mix, w_in, g_q_a, w_q_b, g_kv_a, w_kv_b, sink_b, rpb_c,
           g_q_d, g_k_d, w_gate, b_gate, w_branch, w_out, g_norm_ffn, w_group, b_group, w_router, b_router,
           w_ff1, w_ff3, w_ff2, g_final):
    B, S, D = x.shape
    n_ctx = ctx.shape[1]
    depth = w_mod.shape[0]
    lay = _layout(B, S, n_ctx, D)
    params = dict(w_in=w_in, g_norm_mix=g_norm_mix, g_q_a=g_q_a, w_q_b=w_q_b, g_kv_a=g_kv_a, w_kv_b=w_kv_b,
                  sink_b=sink_b, rpb_c=rpb_c, g_q_d=g_q_d, g_k_d=g_k_d, w_gate=w_gate, b_gate=b_gate,
                  w_branch=w_branch, w_out=w_out, g_norm_ffn=g_norm_ffn, w_group=w_group, b_group=b_group,
                  w_router=w_router, b_router=b_router, w_ff1=w_ff1, w_ff3=w_ff3, w_ff2=w_ff2)

    c_all = jnp.zeros((lay["mod_rows"], D), F32).at[:B].set(c).at[B].set(c_ctx)
    mod = _modulation(c_all, w_mod, b_mod).reshape(depth * lay["mod_rows"], 6, D)
    tabs = _rope_tables(S, lay["tm"])
    win_bias = _window_bias(S, lay["tq_band"])
    xf = (ctx.reshape(B * n_ctx, D), x.reshape(B * S, D))

    f = None
    for l in range(depth):
        with_ctx = l < depth - 1
        w = _prep_layer(l, params)
        xf, pr = _proj_call(lay, xf, f, mod, mod, l, w, tabs)
        o = {}
        for kind, kl in (("A", "a"), ("B", "b"), ("C", "c"), ("D", "d")):
            q, k, v = pr["q" + kl], pr["k" + kl], pr["v" + kl]
            sink = w["sink"] if kind == "B" else None
            bias = win_bias if kind == "B" else None
            if kind == "C":
                bias = _neighbourhood_bias(w["rpb"], S // GRID_W, lay["tq_band"])
            o_ctx = _attn_context_call(lay, kind, q, k, v, sink=sink) if with_ctx else None
            o[kind] = (o_ctx, _attn_latent_call(lay, kind, q, k, v, sink=sink, bias=bias))
        xf, h2, route = _merge_call(lay, l, pr["h"], o, xf, mod, w, with_ctx=with_ctx)
        f = _moe_call(lay, route, h2, w, with_ctx=with_ctx)
    out = _final_call(lay, depth - 1, xf, f, mod, g_final[None, :])
    return out.reshape(B, S, D)
```

```python
import functools

import numpy as np
import jax
import jax.numpy as jnp
from jax import lax
from jax.experimental import pallas as pl
from jax.experimental.pallas import tpu as pltpu

F32 = jnp.float32
BF16 = jnp.bfloat16
HIGHEST = lax.Precision.HIGHEST

GRID_W = 64
ROPE_THETA = 10000.0
EPS = 1e-6
NEG_INF = -1e30
LOG2E = 1.4426950408889634
HEAD_DIM = 64
N_HEADS = 4
BRANCH_W = 256
A_Q_RANK = 256
A_KV_RANK = 128
A_NOPE = 64
A_ROPE = 32
NA_KH = 8
NA_KW = 16
WINDOW = 128
N_GROUPS = 4
EXPERTS_PER_GROUP = 8
N_EXPERTS = 32
EXPERT_FF = 256

LANES = 128
SUBLANES = 8
ROW_BLK = 256
MERGE_TN = 256
MOE_EXPERTS_PER_STEP = 2
MOE_GROUP = 16
VMEM_LIMIT = 56 * 1024 * 1024

_PROJ_GROUPS = (("cq", 256), ("ckv", 128), ("kr", 128), ("qb", 256), ("kb", 128), ("vb", 128),
                ("qc", 256), ("kc", 256), ("vc", 256), ("qd", 256), ("kd", 128), ("vd", 128))
_PROJ_OFF = {}
_o = 0
for _n, _w in _PROJ_GROUPS:
    _PROJ_OFF[_n] = (_o, _w)
    _o += _w
PROJ_COLS = _o
_GQA_PERM = (0, 2, 1, 3)


def _cparams(sem):
    return pltpu.CompilerParams(dimension_semantics=sem, vmem_limit_bytes=VMEM_LIMIT)


def _lane_iota(shape):
    return lax.broadcasted_iota(jnp.int32, shape, len(shape) - 1)


def _sigmoid(x):
    return 1.0 / (1.0 + jnp.exp(-x))


def _mod_kernel(c_ref, w_ref, b_ref, o_ref):
    cf = c_ref[...]
    s = cf * _sigmoid(cf)
    o_ref[0] = jnp.dot(s, w_ref[0], precision=HIGHEST, preferred_element_type=F32) + b_ref[0]


def _modulation(c_all, w_mod, b_mod):
    n_layers, d, n_out = w_mod.shape
    rows = c_all.shape[0]
    tn = 1536
    return pl.pallas_call(
        _mod_kernel,
        out_shape=jax.ShapeDtypeStruct((n_layers, rows, n_out), F32),
        grid=(n_layers, n_out // tn),
        in_specs=[pl.BlockSpec((rows, d), lambda l, j: (0, 0)),
                  pl.BlockSpec((1, d, tn), lambda l, j: (l, 0, j)),
                  pl.BlockSpec((1, 1, tn), lambda l, j: (l, 0, j))],
        out_specs=pl.BlockSpec((1, rows, tn), lambda l, j: (l, 0, j)),
        compiler_params=_cparams(("arbitrary", "arbitrary")),
        name="modulation",
    )(c_all, w_mod, b_mod.reshape(n_layers, 1, n_out))


def _rms(x):
    return x * lax.rsqrt(jnp.mean(x * x, axis=-1, keepdims=True) + EPS)


def _swap_blocks(x, blk):
    lane = _lane_iota(x.shape)
    up = pltpu.roll(x, LANES - blk, 1)
    dn = pltpu.roll(x, blk, 1)
    return jnp.where((lane // blk) % 2 == 0, up, dn)


def _rope(x, cos, sin, blk):
    return x * cos + _swap_blocks(x, blk) * sin


def _pair_norm(x, g):
    lo = _lane_iota(x.shape) < HEAD_DIM
    sq = x * x
    s_lo = jnp.sum(jnp.where(lo, sq, 0.0), axis=-1, keepdims=True)
    s_hi = jnp.sum(jnp.where(lo, 0.0, sq), axis=-1, keepdims=True)
    ms = jnp.where(lo, s_lo, s_hi) * (1.0 / HEAD_DIM)
    return x * lax.rsqrt(ms + EPS) * g


def _load_token_major(ref, rows):
    return jnp.concatenate(
        [ref[pl.ds(c, rows, stride=SUBLANES), :] for c in range(SUBLANES)], axis=-1)


def _tile_rows(xc_ref, xl_ref, is_ctx):
    return jnp.where(is_ctx, xc_ref[...], xl_ref[...])


def _proj_kernel(*refs, with_f, tm, scale_a, nct):
    it = iter(refs)
    if with_f:
        x_ref = next(it)
        f_ref = next(it)
        modp_ref = next(it)
    else:
        xc_ref, xl_ref = next(it), next(it)
    mod_ref = next(it)
    gmix_ref, win_ref, gqa_ref, wqb_ref, gkva_ref, wkvb_ref, gqd_ref, gkd_ref = (next(it) for _ in range(8))
    cosh_ref, sinh_ref, cosa_ref, sina_ref = (next(it) for _ in range(4))
    if with_f:
        x2_ref = next(it)
    h_ref = next(it)
    qa_ref, ka_ref, va_ref, qb_ref, kb_ref, vb_ref, qc_ref, kc_ref, vc_ref, qd_ref, kd_ref, vd_ref = (
        next(it) for _ in range(12))

    if with_f:
        xf = x_ref[...] + modp_ref[0, 5:6, :] * _load_token_major(f_ref, tm)
        x2_ref[...] = xf
    else:
        xf = _tile_rows(xc_ref, xl_ref, pl.program_id(0) < nct)
    h = _rms(xf) * gmix_ref[...] * (1.0 + mod_ref[0, 1:2, :]) + mod_ref[0, 0:1, :]
    hb = h.astype(BF16)
    h_ref[...] = hb
    p = jnp.dot(hb, win_ref[...], preferred_element_type=F32)

    def grp(name):
        o, w = _PROJ_OFF[name]
        return p[:, o:o + w]

    cosh, sinh = cosh_ref[...], sinh_ref[...]
    cosa, sina = cosa_ref[...], sina_ref[...]

    cq = (_rms(grp("cq")) * gqa_ref[...]).astype(BF16)
    qa = jnp.dot(cq, wqb_ref[...], preferred_element_type=F32)
    for hd in range(N_HEADS):
        sl = slice(hd * LANES, (hd + 1) * LANES)
        qa_ref[:, sl] = (_rope(qa[:, sl], cosa, sina, 8) * scale_a).astype(BF16)
    ckv = (_rms(grp("ckv")) * gkva_ref[...]).astype(BF16)
    kva = jnp.dot(ckv, wkvb_ref[...], preferred_element_type=F32)
    kr = _rope(grp("kr"), cosa, sina, 8)
    for hd in range(N_HEADS):
        sl = slice(hd * LANES, (hd + 1) * LANES)
        ka_ref[:, sl] = (kva[:, sl] + kr).astype(BF16)
    va_ref[...] = kva[:, N_HEADS * LANES:].astype(BF16)

    qb = grp("qb")
    for j in range(2):
        sl = slice(j * LANES, (j + 1) * LANES)
        qb_ref[:, sl] = _rope(qb[:, sl], cosh, sinh, 16).astype(BF16)
    kb_ref[...] = _rope(grp("kb"), cosh, sinh, 16).astype(BF16)
    vb_ref[...] = grp("vb").astype(BF16)

    qc_ref[...] = grp("qc").astype(BF16)
    kc_ref[...] = grp("kc").astype(BF16)
    vc_ref[...] = grp("vc").astype(BF16)

    qd = grp("qd")
    for j in range(2):
        sl = slice(j * LANES, (j + 1) * LANES)
        qd_ref[:, sl] = _rope(_pair_norm(qd[:, sl], gqd_ref[...]), cosh, sinh, 16).astype(BF16)
    kd_ref[...] = _rope(_pair_norm(grp("kd"), gkd_ref[...]), cosh, sinh, 16).astype(BF16)
    vd_ref[...] = grp("vd").astype(BF16)


def _split_row_specs(lay, width, off=0):
    tm, nct = lay["tm"], lay["nct"]
    return [pl.BlockSpec((tm, width), lambda i: (jnp.minimum(i + off, nct - 1), 0)),
            pl.BlockSpec((tm, width), lambda i: (jnp.maximum(i + off - nct, 0), 0))]


def _proj_call(lay, x, f, modp, mod, layer, w, tabs):
    T, D, tm = lay["T"], lay["D"], lay["tm"]
    nct, tpb, B = lay["nct"], lay["tpb"], lay["B"]
    with_f = f is not None
    n_tiles = T // tm

    def mod_row(i):
        return jnp.where(i < nct, B, (i - nct) // tpb)

    def tab_blk(i):
        return jnp.where(i < nct, tpb, (i - nct) % tpb)

    row = lambda i: (i, 0)
    const = lambda i: (0, 0)
    if with_f:
        in_specs = [pl.BlockSpec((tm, D), row),
                    pl.BlockSpec((tm * SUBLANES, LANES), row),
                    pl.BlockSpec((1, 6, D), lambda i: ((layer - 1) * lay["mod_rows"] + mod_row(i), 0, 0))]
        args = [x, f, modp]
    else:
        in_specs = _split_row_specs(lay, D)
        args = list(x)
    in_specs += [pl.BlockSpec((1, 6, D), lambda i: (layer * lay["mod_rows"] + mod_row(i), 0, 0))]
    args += [mod]
    for name in ("g_mix", "w_in", "g_q_a", "w_q_b", "g_kv_a", "w_kv_b", "g_q_d", "g_k_d"):
        a = w[name]
        in_specs.append(pl.BlockSpec(a.shape, const))
        args.append(a)
    for tname in ("cos_h", "sin_h", "cos_a", "sin_a"):
        in_specs.append(pl.BlockSpec((tm, LANES), lambda i: (tab_blk(i), 0)))
        args.append(tabs[tname])

    widths = [("h", D), ("qa", 512), ("ka", 512), ("va", 256), ("qb", 256), ("kb", 128), ("vb", 128),
              ("qc", 256), ("kc", 256), ("vc", 256), ("qd", 256), ("kd", 128), ("vd", 128)]
    out_shape, out_specs = [], []
    if with_f:
        out_shape.append(jax.ShapeDtypeStruct((T, D), F32))
        out_specs.append(pl.BlockSpec((tm, D), row))
    for _, wd in widths:
        out_shape.append(jax.ShapeDtypeStruct((T, wd), BF16))
        out_specs.append(pl.BlockSpec((tm, wd), row))

    outs = pl.pallas_call(
        functools.partial(_proj_kernel, with_f=with_f, tm=tm, nct=nct,
                          scale_a=float((A_NOPE + A_ROPE) ** -0.5 * LOG2E)),
        out_shape=out_shape, grid=(n_tiles,), in_specs=in_specs, out_specs=out_specs,
        compiler_params=_cparams(("parallel",)),
        name="proj_in",
    )(*args)
    outs = list(outs)
    x2 = outs.pop(0) if with_f else x
    names = [n for n, _ in widths]
    return x2, dict(zip(names, outs))


_NT = (((1,), (1,)), ((), ()))


def _head_plan(kind):
    if kind == "A":
        return tuple((r, None, r, r // 2) for r in range(N_HEADS))
    if kind == "C":
        return tuple((r // 2, r % 2, r // 2, r // 2) for r in range(N_HEADS))
    return tuple((r // 2, r % 2, 0, 0) for r in range(N_HEADS))


def _head_query(q_ref, plan_r):
    qt, half, _, _ = plan_r
    src = q_ref[:, qt * LANES:(qt + 1) * LANES]
    if half is None:
        return src
    lane = _lane_iota(src.shape)
    keep = (lane < HEAD_DIM) if half == 0 else (lane >= HEAD_DIM)
    return jnp.where(keep, src, jnp.zeros_like(src))


def _score_chunks(q, k_blocks, bias_blocks):
    chunks = []
    for kb, bb in zip(k_blocks, bias_blocks):
        s = lax.dot_general(q, kb, _NT, preferred_element_type=F32)
        if bb is not None:
            s = s + bb
        chunks += [s[:, c * LANES:(c + 1) * LANES] for c in range(s.shape[1] // LANES)]
    return chunks


def _row_max(chunks):
    m = functools.reduce(jnp.maximum, chunks)
    return jnp.broadcast_to(jnp.max(m, axis=-1, keepdims=True), m.shape)


def _weighted_values(p_chunks, v_blocks, half):
    pv, idx = None, 0
    for vb in v_blocks:
        n = vb.shape[0] // LANES
        p = jnp.concatenate(p_chunks[idx:idx + n], axis=1).astype(BF16)
        idx += n
        lane = _lane_iota(vb.shape)
        own = (lane < HEAD_DIM) if half == 0 else (lane >= HEAD_DIM)
        d = jnp.dot(p, jnp.where(own, vb, jnp.ones_like(vb)), preferred_element_type=F32)
        pv = d if pv is None else pv + d
    return pv


def _softmax_once(q, k_blocks, v_blocks, bias_blocks, sink, half):
    chunks = _score_chunks(q, k_blocks, bias_blocks)
    m = _row_max(chunks)
    if sink is not None:
        m = jnp.maximum(m, sink)
    p = [jnp.exp2(c - m) for c in chunks]
    pv = _weighted_values(p, v_blocks, half)
    l = pltpu.roll(pv, HEAD_DIM, 1)
    if sink is not None:
        l = l + jnp.exp2(sink - m)
    return pv / l


def _store_heads(o_ref, outs):
    lane = _lane_iota(outs[0].shape)
    for g in range(2):
        o_ref[:, g * LANES:(g + 1) * LANES] = jnp.where(
            lane < HEAD_DIM, outs[2 * g], outs[2 * g + 1]).astype(o_ref.dtype)


def _attn_band_kernel(*refs, kind, lb, n_kv, tq):
    it = iter(refs)
    q_ref = next(it)
    k_refs = [next(it) for _ in range(n_kv)]
    v_refs = [next(it) for _ in range(n_kv)]
    bias_ref = next(it) if n_kv > 1 else None
    sink_ref = next(it) if kind == "B" else None
    o_ref = next(it)
    t = pl.program_id(0) % lb
    var = jnp.where(t == 0, 0, jnp.where(t == lb - 1, 2, 1))
    outs = []
    for r, plan_r in enumerate(_head_plan(kind)):
        _, _, kt, vt = plan_r
        ksl = slice(kt * LANES, (kt + 1) * LANES)
        vsl = slice(vt * LANES, (vt + 1) * LANES)
        bias = [None]
        for j in range(n_kv - 1):
            if kind == "C":
                bias.append(bias_ref[var, j, r * tq:(r + 1) * tq, :])
            else:
                bias.append(bias_ref[var, j])
        sink = sink_ref[_GQA_PERM[r]] * LOG2E if kind == "B" else None
        outs.append(_softmax_once(_head_query(q_ref, plan_r), [k[:, ksl] for k in k_refs],
                                  [v[:, vsl] for v in v_refs], bias, sink, r % 2))
    _store_heads(o_ref, outs)


def _attn_dense_kernel(q_ref, kc_ref, vc_ref, kl_ref, vl_ref, o_ref, qs_ref, m_ref, acc_ref,
                       *, kind, n_steps, tq):
    s = pl.program_id(1)
    plan = _head_plan(kind)

    def update(first):
        for r, (_, _, kt, vt) in enumerate(plan):
            rows = slice(r * tq, (r + 1) * tq)
            ksl = slice(kt * LANES, (kt + 1) * LANES)
            vsl = slice(vt * LANES, (vt + 1) * LANES)
            if first:
                q = _head_query(q_ref, plan[r])
                qs_ref[rows, :] = q
                k_blocks = [kc_ref[:, ksl], kl_ref[:, ksl]]
                v_blocks = [vc_ref[:, vsl], vl_ref[:, vsl]]
            else:
                q = qs_ref[rows, :]
                k_blocks = [kl_ref[:, ksl]]
                v_blocks = [vl_ref[:, vsl]]
            chunks = _score_chunks(q, k_blocks, [None] * len(k_blocks))
            m_cur = _row_max(chunks)
            if first:
                m_new = m_cur
            else:
                m_prev = m_ref[rows, :]
                m_new = jnp.maximum(m_prev, m_cur)
                alpha = jnp.exp2(m_prev - m_new)
            p = [jnp.exp2(c - m_new) for c in chunks]
            pv = _weighted_values(p, v_blocks, r % 2)
            if first:
                acc_ref[rows, :] = pv
            else:
                acc_ref[rows, :] = alpha * acc_ref[rows, :] + pv
            m_ref[rows, :] = m_new

    @pl.when(s == 0)
    def _():
        update(True)

    @pl.when(s > 0)
    def _():
        update(False)

    @pl.when(s == n_steps - 1)
    def _():
        outs = []
        for r in range(N_HEADS):
            rows = slice(r * tq, (r + 1) * tq)
            acc = acc_ref[rows, :]
            outs.append(acc / pltpu.roll(acc, HEAD_DIM, 1))
        _store_heads(o_ref, outs)


def _attn_latent_call(lay, kind, q, k, v, *, sink=None, bias=None):
    T, B, ncb, lb, S = lay["T"], lay["B"], lay["ncb"], lay["lb"], lay["S"]
    tq = ROW_BLK
    qw, kw, vw = q.shape[1], k.shape[1], v.shape[1]
    out_shape = jax.ShapeDtypeStruct((B * S, BRANCH_W), BF16)
    if kind in ("A", "D"):
        tk, tq = lay["tk_dense"], lay["tq_dense"]
        n_steps = S // tk
        lat0 = ncb * ROW_BLK // tk
        q0 = ncb * ROW_BLK // tq
        qpb = S // tq
        lat_blk = lambda i, s: (lat0 + (i // qpb) * n_steps + s, 0)
        return pl.pallas_call(
            functools.partial(_attn_dense_kernel, kind=kind, n_steps=n_steps, tq=tq),
            out_shape=out_shape,
            grid=(B * qpb, n_steps),
            in_specs=[pl.BlockSpec((tq, qw), lambda i, s: (q0 + i, 0)),
                      pl.BlockSpec((ROW_BLK, kw), lambda i, s: (i // qpb, 0)),
                      pl.BlockSpec((ROW_BLK, vw), lambda i, s: (i // qpb, 0)),
                      pl.BlockSpec((tk, kw), lat_blk),
                      pl.BlockSpec((tk, vw), lat_blk)],
            out_specs=pl.BlockSpec((tq, BRANCH_W), lambda i, s: (i, 0)),
            scratch_shapes=[pltpu.VMEM((N_HEADS * tq, LANES), BF16),
                            pltpu.VMEM((N_HEADS * tq, LANES), F32),
                            pltpu.VMEM((N_HEADS * tq, LANES), F32)],
            compiler_params=_cparams(("parallel", "arbitrary")),
            name="attn_" + kind,
        )(q, k, v, k, v)

    tq = lay["tq_band"]
    qpb = S // tq
    q0 = ncb * ROW_BLK // tq
    bpq = tq // ROW_BLK

    def nb(i, d):
        return (ncb + (i // qpb) * lb + jnp.clip((i % qpb) * bpq + d, 0, lb - 1), 0)

    kv_maps = [lambda i: (i // qpb, 0)] + [functools.partial(nb, d=d) for d in _band_offsets(tq)]
    n_kv = len(kv_maps)
    in_specs = [pl.BlockSpec((tq, qw), lambda i: (q0 + i, 0))]
    in_specs += [pl.BlockSpec((ROW_BLK, kw), m) for m in kv_maps]
    in_specs += [pl.BlockSpec((ROW_BLK, vw), m) for m in kv_maps]
    in_specs.append(pl.BlockSpec(bias.shape, lambda i: (0,) * bias.ndim, pipeline_mode=pl.Buffered(1)))
    args = [q] + [k] * n_kv + [v] * n_kv + [bias]
    if kind == "B":
        in_specs.append(pl.BlockSpec(memory_space=pltpu.SMEM))
        args.append(sink)
    return pl.pallas_call(
        functools.partial(_attn_band_kernel, kind=kind, lb=qpb, n_kv=n_kv, tq=tq),
        out_shape=out_shape,
        grid=(B * qpb,),
        in_specs=in_specs,
        out_specs=pl.BlockSpec((tq, BRANCH_W), lambda i: (i, 0)),
        compiler_params=_cparams(("parallel",)),
        name="attn_" + kind,
    )(*args)


def _attn_context_call(lay, kind, q, k, v, *, sink=None):
    ncb = lay["ncb"]
    qw, kw, vw = q.shape[1], k.shape[1], v.shape[1]
    blk = lambda i: (i, 0)
    in_specs = [pl.BlockSpec((ROW_BLK, qw), blk), pl.BlockSpec((ROW_BLK, kw), blk),
                pl.BlockSpec((ROW_BLK, vw), blk)]
    args = [q, k, v]
    if kind == "B":
        in_specs.append(pl.BlockSpec(memory_space=pltpu.SMEM))
        args.append(sink)
    return pl.pallas_call(
        functools.partial(_attn_band_kernel, kind=kind, lb=1, n_kv=1, tq=ROW_BLK),
        out_shape=jax.ShapeDtypeStruct((ncb * ROW_BLK, BRANCH_W), BF16),
        grid=(ncb,),
        in_specs=in_specs,
        out_specs=pl.BlockSpec((ROW_BLK, BRANCH_W), blk),
        compiler_params=_cparams(("parallel",)),
        name="attn_ctx_" + kind,
    )(*args)


def _band_offsets(tq):
    return tuple(range(-1, tq // ROW_BLK + 1))


def _window_bias(S, tq):
    lb, qpb, bpq = S // ROW_BLK, S // tq, tq // ROW_BLK
    offs = _band_offsets(tq)
    qa, ka = np.arange(tq), np.arange(ROW_BLK)
    out = np.full((3, len(offs), tq, ROW_BLK), NEG_INF, np.float32)
    for vi, t_rep in enumerate((0, 1, qpb - 1)):
        for di, d in enumerate(offs):
            kt = t_rep * bpq + d
            if not 0 <= kt < lb:
                continue
            qpos = t_rep * tq + qa
            kpos = kt * ROW_BLK + ka
            ok = np.abs(qpos[:, None] - kpos[None, :]) <= WINDOW
            out[vi, di] = np.where(ok, 0.0, NEG_INF)
    return jnp.asarray(out)


def _neighbourhood_bias(rpb, rows_total, tq):
    lb = rows_total * GRID_W // ROW_BLK
    qpb, bpq = rows_total * GRID_W // tq, tq // ROW_BLK
    rpt = ROW_BLK // GRID_W
    rpq = tq // GRID_W
    kh = min(NA_KH, rows_total)
    qa, ka = np.arange(tq), np.arange(ROW_BLK)
    q_sub, q_col = qa // GRID_W, qa % GRID_W
    k_sub, k_col = ka // GRID_W, ka % GRID_W
    n_dr, n_dc = 2 * NA_KH - 1, 2 * NA_KW - 1
    col = np.arange(GRID_W)
    dc = np.clip(col[None, :] - col[:, None], -(NA_KW - 1), NA_KW - 1) + NA_KW - 1
    hot_c = (dc[:, :, None] == np.arange(n_dc)).astype(np.float32)
    by_col = jnp.einsum("huv,cdv->hucd", rpb.astype(F32) * LOG2E, jnp.asarray(hot_c),
                        precision=HIGHEST)
    offs = _band_offsets(tq)
    vals = []
    for d in offs:
        dr = np.clip(d * rpt + np.arange(rpt)[None, :] - np.arange(rpq)[:, None],
                     -(NA_KH - 1), NA_KH - 1) + NA_KH - 1
        hot_r = (dr[:, :, None] == np.arange(n_dr)).astype(np.float32)
        v = jnp.einsum("abu,hucd->hacbd", jnp.asarray(hot_r), by_col, precision=HIGHEST)
        vals.append(v.reshape(N_HEADS, tq, ROW_BLK))
    out = []
    for t_rep in (0, 1, qpb - 1):
        per_block = []
        for di, d in enumerate(offs):
            kt = t_rep * bpq + d
            q_row = t_rep * rpq + q_sub
            k_row = kt * rpt + k_sub
            r_start = np.clip(q_row - kh // 2, 0, rows_total - kh)
            row_ok = (k_row[None] >= r_start[:, None]) & (k_row[None] < r_start[:, None] + kh)
            c_start = np.clip(q_col - NA_KW // 2, 0, GRID_W - NA_KW)
            col_ok = (k_col[None] >= c_start[:, None]) & (k_col[None] < c_start[:, None] + NA_KW)
            ok = row_ok & col_ok & (0 <= kt < lb)
            per_block.append(jnp.where(jnp.asarray(ok)[None], vals[di], NEG_INF).reshape(-1, ROW_BLK))
        out.append(jnp.stack(per_block))
    return jnp.stack(out)


def _merge_kernel(*refs, tm, split_x, n_ctx_tiles):
    it = iter(refs)
    h_ref = next(it)
    is_ctx = pl.program_id(0) < n_ctx_tiles
    if n_ctx_tiles > 0:
        o_refs = [(next(it), next(it)) for _ in range(4)]
        o_vals = [_tile_rows(oc, ol, is_ctx) for oc, ol in o_refs]
    else:
        o_vals = [next(it)[...] for _ in range(4)]
    if split_x:
        xc_ref, xl_ref = next(it), next(it)
    else:
        x_ref = next(it)
    (mod_ref, gffn_ref, wg_ref, bg_ref, wb_ref, wout_ref, wr_ref, br_ref,
     x1_ref, h2_ref, route_ref, y_ref) = (next(it) for _ in range(12))
    hb = h_ref[...]
    d_model = hb.shape[1]
    for t in range(d_model // MERGE_TN):
        cs = slice(t * MERGE_TN, (t + 1) * MERGE_TN)
        y = None
        for n, o_n in enumerate(o_vals):
            gate = _sigmoid(jnp.dot(hb, wg_ref[n, :, cs], preferred_element_type=F32) + bg_ref[n, :, cs])
            u = gate * jnp.dot(o_n, wb_ref[n, :, cs], preferred_element_type=F32)
            y = u if y is None else y + u
        y_ref[:, cs] = y.astype(BF16)
    z = jnp.dot(y_ref[...], wout_ref[...], preferred_element_type=F32)
    x_in = _tile_rows(xc_ref, xl_ref, is_ctx) if split_x else x_ref[...]
    x1 = x_in + mod_ref[0, 2:3, :] * z
    x1_ref[...] = x1
    h2 = _rms(x1) * gffn_ref[...] * (1.0 + mod_ref[0, 4:5, :]) + mod_ref[0, 3:4, :]
    for c in range(SUBLANES):
        h2_ref[pl.ds(c, tm, stride=SUBLANES), :] = h2[:, c * LANES:(c + 1) * LANES]

    h2_hi = h2.astype(BF16)
    h2_lo = (h2 - h2_hi.astype(F32)).astype(BF16)
    logit = (jnp.dot(h2_hi, wr_ref[0], preferred_element_type=F32)
             + jnp.dot(h2_lo, wr_ref[0], preferred_element_type=F32)
             + jnp.dot(h2_hi, wr_ref[1], preferred_element_type=F32)) + br_ref[...]
    lane = _lane_iota(logit.shape)
    big = jnp.int32(1 << 20)
    is_g = (lane >= N_EXPERTS) & (lane < N_EXPERTS + N_GROUPS)
    gl = jnp.where(is_g, logit, NEG_INF)
    gmax = jnp.max(gl, axis=-1, keepdims=True)
    gsel = jnp.min(jnp.where(gl == gmax, lane - N_EXPERTS, big), axis=-1, keepdims=True)
    gw = 1.0 / jnp.sum(jnp.where(is_g, jnp.exp(gl - gmax), 0.0), axis=-1, keepdims=True)
    in_grp = (lane < N_EXPERTS) & ((lane // EXPERTS_PER_GROUP) == gsel)
    el = jnp.where(in_grp, logit, NEG_INF)
    v1 = jnp.max(el, axis=-1, keepdims=True)
    i1 = jnp.min(jnp.where(el == v1, lane, big), axis=-1, keepdims=True)
    el2 = jnp.where(lane == i1, NEG_INF, el)
    v2 = jnp.max(el2, axis=-1, keepdims=True)
    i2 = jnp.min(jnp.where(el2 == v2, lane, big), axis=-1, keepdims=True)
    e21 = jnp.exp(v2 - v1)
    w1 = gw / (1.0 + e21)
    w2 = gw * e21 / (1.0 + e21)
    route_ref[...] = jnp.where(lane == 0, i1.astype(F32),
                               jnp.where(lane == 1, i2.astype(F32),
                                         jnp.where(lane == 2, w1, jnp.where(lane == 3, w2, 0.0))))


def _merge_call(lay, layer, h, o, x, mod, w, *, with_ctx):
    T, D, tm = lay["T"], lay["D"], lay["tm"]
    nct, tpb, B = lay["nct"], lay["tpb"], lay["B"]
    off = 0 if with_ctx else nct
    n_tiles = T // tm - off

    def mod_row(i):
        return jnp.where(i < nct, B, (i - nct) // tpb)

    row = lambda i: (i + off, 0)
    const2 = lambda i: (0, 0)
    const3 = lambda i: (0, 0, 0)
    split_x = isinstance(x, (tuple, list))
    in_specs = [pl.BlockSpec((tm, D), row)]
    o_args = []
    for kind in ("A", "B", "C", "D"):
        o_ctx, o_lat = o[kind]
        if with_ctx:
            in_specs += _split_row_specs(lay, BRANCH_W)
            o_args += [o_ctx, o_lat]
        else:
            in_specs.append(pl.BlockSpec((tm, BRANCH_W), lambda i: (i, 0)))
            o_args.append(o_lat)
    in_specs += _split_row_specs(lay, D, off) if split_x else [pl.BlockSpec((tm, D), row)]
    in_specs += [pl.BlockSpec((1, 6, D), lambda i: (layer * lay["mod_rows"] + mod_row(i + off), 0, 0)),
                 pl.BlockSpec((1, D), const2),
                 pl.BlockSpec(w["w_gate"].shape, const3),
                 pl.BlockSpec(w["b_gate"].shape, const3),
                 pl.BlockSpec(w["w_branch"].shape, const3),
                 pl.BlockSpec((D, D), const2),
                 pl.BlockSpec((2, D, LANES), const3),
                 pl.BlockSpec((1, LANES), const2)]
    out_shape = [jax.ShapeDtypeStruct((T, D), F32),
                 jax.ShapeDtypeStruct((T * SUBLANES, LANES), F32),
                 jax.ShapeDtypeStruct((T, LANES), F32)]
    out_specs = [pl.BlockSpec((tm, D), row),
                 pl.BlockSpec((tm * SUBLANES, LANES), row),
                 pl.BlockSpec((tm, LANES), row)]
    return pl.pallas_call(
        functools.partial(_merge_kernel, tm=tm, split_x=split_x, n_ctx_tiles=nct - off),
        out_shape=out_shape, grid=(n_tiles,), in_specs=in_specs, out_specs=out_specs,
        scratch_shapes=[pltpu.VMEM((tm, D), BF16)],
        compiler_params=_cparams(("parallel",)),
        name="merge",
    )(h, *o_args, *(x if split_x else (x,)), mod, w["g_ffn"], w["w_gate"],
      w["b_gate"], w["w_branch"], w["w_out"], w["w_route"], w["b_route"])


def _moe_kernel(tab_ref, tok_ref, h2_ref, sw_ref, w1_ref, w3_ref, w2_ref, f_ref,
                xg_ref, y_ref, st_ref, *, blk, slots, tile_off):
    ti = pl.program_id(0)
    step = pl.program_id(1)

    @pl.when(step == 0)
    def _():
        f_ref[...] = jnp.zeros_like(f_ref)
        xg_ref[...] = jnp.zeros_like(xg_ref)

    tok_base = (ti + tile_off) * slots
    for j in range(MOE_EXPERTS_PER_STEP):
        base = ((ti + tile_off) * N_EXPERTS + step * MOE_EXPERTS_PER_STEP + j) * 2
        _moe_expert(tab_ref[base], tab_ref[base + 1], tok_base, tok_ref, h2_ref, sw_ref,
                    w1_ref.at[j], w3_ref.at[j], w2_ref.at[j], f_ref, xg_ref, y_ref, st_ref, blk)


def _moe_expert(seg0, n_pad, tok_base, tok_ref, h2_ref, sw_ref, w1_ref, w3_ref, w2_ref, f_ref,
                xg_ref, y_ref, st_ref, blk):
    n_blocks = (n_pad + blk - 1) // blk

    def block_body(b, carry):
        off = pl.multiple_of(seg0 + b * blk, MOE_GROUP)
        n_grp = jnp.minimum(n_pad - b * blk, blk) // MOE_GROUP
        slot0 = tok_base + off

        def gather(gi, c):
            r0 = gi * MOE_GROUP
            srcs = [pl.multiple_of(tok_ref[slot0 + r0 + u], SUBLANES) for u in range(MOE_GROUP)]
            for u in range(MOE_GROUP):
                dst = pl.multiple_of((r0 + u) * SUBLANES, SUBLANES)
                xg_ref[pl.ds(dst, SUBLANES), :] = h2_ref[pl.ds(srcs[u], SUBLANES), :]
            return c

        lax.fori_loop(0, n_grp, gather, 0)
        xb = _load_token_major(xg_ref, blk).astype(BF16)
        a = jnp.dot(xb, w1_ref[...], preferred_element_type=F32)
        g = jnp.dot(xb, w3_ref[...], preferred_element_type=F32)
        hid = (a * _sigmoid(a) * g).astype(BF16)
        y = jnp.dot(hid, w2_ref[...], preferred_element_type=F32)
        y_ref[...] = y * sw_ref[0, pl.ds(off, blk), :]

        def scatter(gi, c):
            r0 = pl.multiple_of(gi * MOE_GROUP, MOE_GROUP)
            dsts = [pl.multiple_of(tok_ref[slot0 + r0 + u], SUBLANES) for u in range(MOE_GROUP)]
            for ch in range(SUBLANES):
                st_ref[pl.ds(ch, MOE_GROUP, stride=SUBLANES), :] = y_ref[pl.ds(r0, MOE_GROUP),
                                                                         ch * LANES:(ch + 1) * LANES]
            vals = [f_ref[pl.ds(dsts[u], SUBLANES), :] + st_ref[u * SUBLANES:(u + 1) * SUBLANES, :]
                    for u in range(MOE_GROUP)]
            for u in reversed(range(MOE_GROUP)):
                f_ref[pl.ds(dsts[u], SUBLANES), :] = vals[u]
            return c

        lax.fori_loop(0, n_grp, scatter, 0)
        return carry

    lax.fori_loop(0, n_blocks, block_body, 0)


def _moe_call(lay, route, h2, w, *, with_ctx):
    T, D, tt = lay["T"], lay["D"], lay["tt"]
    blk = lay["moe_blk"]
    n_tiles_all = T // tt
    tile_off = 0 if with_ctx else lay["ncb"] * ROW_BLK // tt
    n_tiles = n_tiles_all - tile_off
    n_assign = 2 * tt
    n_fill = N_EXPERTS * MOE_GROUP
    slots = n_assign + n_fill + blk

    eid = route[:, 0:2].astype(jnp.int32).reshape(n_tiles_all, n_assign)
    wts = route[:, 2:4].reshape(n_tiles_all, n_assign)
    tok = jnp.broadcast_to(jnp.arange(n_assign, dtype=jnp.int32)[None, :] // 2 * SUBLANES, eid.shape)
    ex = jnp.arange(N_EXPERTS, dtype=jnp.int32)
    counts = jnp.sum(eid[:, :, None] == ex[None, None, :], axis=1, dtype=jnp.int32)
    n_dummy = (-counts) % MOE_GROUP
    fill_key = jnp.where(jnp.arange(MOE_GROUP, dtype=jnp.int32)[None, None, :] < n_dummy[:, :, None],
                         ex[None, :, None], N_EXPERTS).reshape(n_tiles_all, n_fill)
    zeros_i = jnp.zeros((n_tiles_all, n_fill), jnp.int32)
    _, slot_tok, slot_w = lax.sort(
        (jnp.concatenate([eid, fill_key], axis=1), jnp.concatenate([tok, zeros_i], axis=1),
         jnp.concatenate([wts, zeros_i.astype(F32)], axis=1)),
        dimension=1, is_stable=True, num_keys=1)
    slot_tok = jnp.pad(slot_tok, ((0, 0), (0, blk)))
    slot_w = jnp.pad(slot_w, ((0, 0), (0, blk)))
    padded = counts + n_dummy
    seg0 = jnp.cumsum(padded, axis=1) - padded
    tab = jnp.stack([seg0, padded], axis=-1).reshape(-1).astype(jnp.int32)

    return pl.pallas_call(
        functools.partial(_moe_kernel, blk=blk, slots=slots, tile_off=tile_off),
        out_shape=jax.ShapeDtypeStruct((T * SUBLANES, LANES), F32),
        grid_spec=pltpu.PrefetchScalarGridSpec(
            num_scalar_prefetch=2,
            grid=(n_tiles, N_EXPERTS // MOE_EXPERTS_PER_STEP),
            in_specs=[
                pl.BlockSpec((tt * SUBLANES, LANES), lambda t, e, *_: (t + tile_off, 0)),
                pl.BlockSpec((1, slots, 1), lambda t, e, *_: (t + tile_off, 0, 0)),
                pl.BlockSpec((MOE_EXPERTS_PER_STEP, D, EXPERT_FF), lambda t, e, *_: (e, 0, 0)),
                pl.BlockSpec((MOE_EXPERTS_PER_STEP, D, EXPERT_FF), lambda t, e, *_: (e, 0, 0)),
                pl.BlockSpec((MOE_EXPERTS_PER_STEP, EXPERT_FF, D), lambda t, e, *_: (e, 0, 0)),
            ],
            out_specs=pl.BlockSpec((tt * SUBLANES, LANES), lambda t, e, *_: (t + tile_off, 0)),
            scratch_shapes=[pltpu.VMEM((blk * SUBLANES, LANES), F32),
                            pltpu.VMEM((blk, D), F32),
                            pltpu.VMEM((MOE_GROUP * SUBLANES, LANES), F32)],
        ),
        compiler_params=_cparams(("parallel", "arbitrary")),
        name="moe_experts",
    )(tab, slot_tok.reshape(-1), h2, slot_w.reshape(n_tiles_all, slots, 1),
      w["w_ff1"], w["w_ff3"], w["w_ff2"])


def _final_kernel(x_ref, f_ref, mod_ref, g_ref, o_ref, *, tm):
    xf = x_ref[...] + mod_ref[0, 5:6, :] * _load_token_major(f_ref, tm)
    o_ref[...] = _rms(xf) * g_ref[...]


def _final_call(lay, layer, x1, f, mod, g_final):
    T, D, tm = lay["T"], lay["D"], lay["tm"]
    nct, tpb = lay["nct"], lay["tpb"]
    n_lat = T // tm - nct
    return pl.pallas_call(
        functools.partial(_final_kernel, tm=tm),
        out_shape=jax.ShapeDtypeStruct((n_lat * tm, D), F32),
        grid=(n_lat,),
        in_specs=[pl.BlockSpec((tm, D), lambda i: (i + nct, 0)),
                  pl.BlockSpec((tm * SUBLANES, LANES), lambda i: (i + nct, 0)),
                  pl.BlockSpec((1, 6, D), lambda i: (layer * lay["mod_rows"] + i // tpb, 0, 0)),
                  pl.BlockSpec((1, D), lambda i: (0, 0))],
        out_specs=pl.BlockSpec((tm, D), lambda i: (i, 0)),
        compiler_params=_cparams(("parallel",)),
        name="final_norm",
    )(x1, f, mod, g_final)


def _select_cols(wm, segs, scale=None):
    parts = []
    for k, (start, width) in enumerate(segs):
        if start is None:
            parts.append(jnp.zeros((wm.shape[0], width), wm.dtype))
        else:
            blk = wm[:, start:start + width]
            parts.append(blk if scale is None or scale[k] is None else blk * scale[k])
    return jnp.concatenate(parts, axis=1)


def _prep_layer(l, p):
    a_cols = A_Q_RANK + A_KV_RANK + A_ROPE
    b_off = a_cols
    c_off = b_off + 512
    d_off = c_off + 768
    qk_scale = HEAD_DIM ** -0.5 * LOG2E
    gqa_q = lambda off: [(off + hh * HEAD_DIM, HEAD_DIM) for hh in _GQA_PERM]
    segs = ([(0, 256), (256, 128), (None, 64), (384, 32), (None, 32)]
            + gqa_q(b_off) + [(b_off + 256, 128), (b_off + 384, 128)]
            + [(c_off, 256), (c_off + 256, 256), (c_off + 512, 256)]
            + gqa_q(d_off) + [(d_off + 256, 128), (d_off + 384, 128)])
    scale = [None] * len(segs)
    for k in (5, 6, 7, 8, 11):
        scale[k] = qk_scale
    w_in = _select_cols(p["w_in"][l], segs, scale).astype(BF16)
    assert w_in.shape[1] == PROJ_COLS

    hq = A_NOPE + A_ROPE
    segs_q = []
    for hh in range(N_HEADS):
        segs_q += [(hh * hq, hq), (None, LANES - hq)]
    w_q_b = _select_cols(p["w_q_b"][l], segs_q).astype(BF16)
    hk = A_NOPE + HEAD_DIM
    segs_k = []
    for hh in range(N_HEADS):
        segs_k += [(hh * hk, A_NOPE), (None, LANES - A_NOPE)]
    segs_k += [(hh * hk + A_NOPE, HEAD_DIM) for hh in range(N_HEADS)]
    w_kv_b = _select_cols(p["w_kv_b"][l], segs_k).astype(BF16)

    wb = p["w_branch"][l]
    perm_rows = lambda m: jnp.concatenate([m[hh * HEAD_DIM:(hh + 1) * HEAD_DIM] for hh in _GQA_PERM], axis=0)
    w_branch = jnp.stack([wb[0], perm_rows(wb[1]), wb[2], perm_rows(wb[3])]).astype(BF16)

    d = p["w_in"].shape[1]
    w_route = jnp.zeros((d, LANES), F32)
    w_route = w_route.at[:, :N_EXPERTS].set(p["w_router"][l]).at[:, N_EXPERTS:N_EXPERTS + N_GROUPS].set(p["w_group"][l])
    b_route = jnp.zeros((1, LANES), F32)
    b_route = b_route.at[0, :N_EXPERTS].set(p["b_router"][l]).at[0, N_EXPERTS:N_EXPERTS + N_GROUPS].set(p["b_group"][l])
    return {
        "g_mix": p["g_norm_mix"][l][None, :],
        "w_in": w_in,
        "g_q_a": p["g_q_a"][l][None, :],
        "w_q_b": w_q_b,
        "g_kv_a": p["g_kv_a"][l][None, :],
        "w_kv_b": w_kv_b,
        "g_q_d": (jnp.tile(p["g_q_d"][l], 2) * qk_scale)[None, :],
        "g_k_d": jnp.tile(p["g_k_d"][l], 2)[None, :],
        "sink": p["sink_b"][l],
        "rpb": p["rpb_c"][l],
        "w_gate": p["w_gate"][l].astype(BF16),
        "b_gate": p["b_gate"][l][:, None, :],
        "w_branch": w_branch,
        "w_out": p["w_out"][l].astype(BF16),
        "g_ffn": p["g_norm_ffn"][l][None, :],
        "w_route": jnp.stack([w_route.astype(BF16),
                              (w_route - w_route.astype(BF16).astype(F32)).astype(BF16)]),
        "b_route": b_route,
        "w_ff1": p["w_ff1"][l].astype(BF16),
        "w_ff3": p["w_ff3"][l].astype(BF16),
        "w_ff2": p["w_ff2"][l].astype(BF16),
    }


def _rope_tables(S, tm):
    t = np.arange(S)
    rows = (t // GRID_W).astype(np.float32)
    cols = (t % GRID_W).astype(np.float32)

    def cs(rot):
        half = rot // 2
        inv = np.float32(ROPE_THETA) ** (-np.arange(0, half, 2, dtype=np.float32) / np.float32(half))
        ar_, ac_ = rows[:, None] * inv, cols[:, None] * inv
        cos = np.concatenate([np.cos(ar_), np.cos(ar_), np.cos(ac_), np.cos(ac_)], axis=-1)
        sin = np.concatenate([-np.sin(ar_), np.sin(ar_), -np.sin(ac_), np.sin(ac_)], axis=-1)
        return cos.astype(np.float32), sin.astype(np.float32)

    cos64, sin64 = cs(HEAD_DIM)
    cos32, sin32 = cs(A_ROPE)
    ones = lambda n: np.ones((S, n), np.float32)
    zeros = lambda n: np.zeros((S, n), np.float32)
    tabs = {
        "cos_h": np.concatenate([cos64, cos64], axis=-1),
        "sin_h": np.concatenate([sin64, sin64], axis=-1),
        "cos_a": np.concatenate([ones(A_NOPE), cos32, ones(LANES - A_NOPE - A_ROPE)], axis=-1),
        "sin_a": np.concatenate([zeros(A_NOPE), sin32, zeros(LANES - A_NOPE - A_ROPE)], axis=-1),
    }
    ident = {"cos_h": 1.0, "sin_h": 0.0, "cos_a": 1.0, "sin_a": 0.0}
    return {k: jnp.asarray(np.concatenate([v, np.full((tm, LANES), ident[k], np.float32)], axis=0))
            for k, v in tabs.items()}


def _layout(B, S, n_ctx, D):
    assert n_ctx == ROW_BLK and S % 1024 == 0 and S // GRID_W >= 3 * (ROW_BLK // GRID_W)
    T = B * (n_ctx + S)
    tm = 512 if (B * n_ctx) % 512 == 0 else 256
    tk_dense = 1024 if (B * n_ctx) % 1024 == 0 else (512 if (B * n_ctx) % 512 == 0 else 256)
    tt = 2048 if (B * n_ctx) % 2048 == 0 else B * n_ctx
    assert S % tt == 0
    return {
        "B": B, "S": S, "D": D, "T": T, "tm": tm,
        "ncb": B * n_ctx // ROW_BLK,
        "lb": S // ROW_BLK,
        "nct": B * n_ctx // tm,
        "tpb": S // tm,
        "tk_dense": tk_dense,
        "tq_dense": tk_dense,
        "tq_band": tm,
        "tt": tt, "moe_blk": 160,
        "mod_rows": 16,
    }


def kernel(x, c, ctx, c_ctx, w_mod, b_mod, g_norm_mix, w_in, g_q_a, w_q_b, g_kv_a, w_kv_b, sink_b, rpb_c,
           g_q_d, g_k_d, w_gate, b_gate, w_branch, w_out, g_norm_ffn, w_group, b_group, w_router, b_router,
           w_ff1, w_ff3, w_ff2, g_final):
    B, S, D = x.shape
    n_ctx = ctx.shape[1]
    depth = w_mod.shape[0]
    lay = _layout(B, S, n_ctx, D)
    params = dict(w_in=w_in, g_norm_mix=g_norm_mix, g_q_a=g_q_a, w_q_b=w_q_b, g_kv_a=g_kv_a, w_kv_b=w_kv_b,
                  sink_b=sink_b, rpb_c=rpb_c, g_q_d=g_q_d, g_k_d=g_k_d, w_gate=w_gate, b_gate=b_gate,
                  w_branch=w_branch, w_out=w_out, g_norm_ffn=g_norm_ffn, w_group=w_group, b_group=b_group,
                  w_router=w_router, b_router=b_router, w_ff1=w_ff1, w_ff3=w_ff3, w_ff2=w_ff2)

    c_all = jnp.zeros((lay["mod_rows"], D), F32).at[:B].set(c).at[B].set(c_ctx)
    mod = _modulation(c_all, w_mod, b_mod).reshape(depth * lay["mod_rows"], 6, D)
    tabs = _rope_tables(S, lay["tm"])
    win_bias = _window_bias(S, lay["tq_band"])
    xf = (ctx.reshape(B * n_ctx, D), x.reshape(B * S, D))

    f = None
    for l in range(depth):
        with_ctx = l < depth - 1
        w = _prep_layer(l, params)
        xf, pr = _proj_call(lay, xf, f, mod, mod, l, w, tabs)
        o = {}
        for kind, kl in (("A", "a"), ("B", "b"), ("C", "c"), ("D", "d")):
            q, k, v = pr["q" + kl], pr["k" + kl], pr["v" + kl]
            sink = w["sink"] if kind == "B" else None
            bias = win_bias if kind == "B" else None
            if kind == "C":
                bias = _neighbourhood_bias(w["rpb"], S // GRID_W, lay["tq_band"])
            o_ctx = _attn_context_call(lay, kind, q, k, v, sink=sink) if with_ctx else None
            o[kind] = (o_ctx, _attn_latent_call(lay, kind, q, k, v, sink=sink, bias=bias))
        xf, h2, route = _merge_call(lay, l, pr["h"], o, xf, mod, w, with_ctx=with_ctx)
        f = _moe_call(lay, route, h2, w, with_ctx=with_ctx)
    out = _final_call(lay, depth - 1, xf, f, mod, g_final[None, :])
    return out.reshape(B, S, D)
```

```python
import functools

import numpy as np
import jax
import jax.numpy as jnp
from jax import lax
from jax.experimental import pallas as pl
from jax.experimental.pallas import tpu as pltpu

F32 = jnp.float32
BF16 = jnp.bfloat16
HIGHEST = lax.Precision.HIGHEST

GRID_W = 64
ROPE_THETA = 10000.0
EPS = 1e-6
NEG_INF = -1e30
LOG2E = 1.4426950408889634
HEAD_DIM = 64
N_HEADS = 4
BRANCH_W = 256
A_Q_RANK = 256
A_KV_RANK = 128
A_NOPE = 64
A_ROPE = 32
NA_KH = 8
NA_KW = 16
WINDOW = 128
N_GROUPS = 4
EXPERTS_PER_GROUP = 8
N_EXPERTS = 32
EXPERT_FF = 256

LANES = 128
SUBLANES = 8
ROW_BLK = 256
MERGE_TN = 256
MOE_EXPERTS_PER_STEP = 2
MOE_GROUP = 16
VMEM_LIMIT = 56 * 1024 * 1024

_PROJ_GROUPS = (("cq", 256), ("ckv", 128), ("kr", 128), ("qb", 256), ("kb", 128), ("vb", 128),
                ("qc", 256), ("kc", 256), ("vc", 256), ("qd", 256), ("kd", 128), ("vd", 128))
_PROJ_OFF = {}
_o = 0
for _n, _w in _PROJ_GROUPS:
    _PROJ_OFF[_n] = (_o, _w)
    _o += _w
PROJ_COLS = _o
_GQA_PERM = (0, 2, 1, 3)


def _cparams(sem):
    return pltpu.CompilerParams(dimension_semantics=sem, vmem_limit_bytes=VMEM_LIMIT)


def _lane_iota(shape):
    return lax.broadcasted_iota(jnp.int32, shape, len(shape) - 1)


def _sigmoid(x):
    return 1.0 / (1.0 + jnp.exp(-x))


def _mod_kernel(c_ref, w_ref, b_ref, o_ref):
    cf = c_ref[...]
    s = cf * _sigmoid(cf)
    o_ref[0] = jnp.dot(s, w_ref[0], precision=HIGHEST, preferred_element_type=F32) + b_ref[0]


def _modulation(c_all, w_mod, b_mod):
    n_layers, d, n_out = w_mod.shape
    rows = c_all.shape[0]
    tn = 1536
    return pl.pallas_call(
        _mod_kernel,
        out_shape=jax.ShapeDtypeStruct((n_layers, rows, n_out), F32),
        grid=(n_layers, n_out // tn),
        in_specs=[pl.BlockSpec((rows, d), lambda l, j: (0, 0)),
                  pl.BlockSpec((1, d, tn), lambda l, j: (l, 0, j)),
                  pl.BlockSpec((1, 1, tn), lambda l, j: (l, 0, j))],
        out_specs=pl.BlockSpec((1, rows, tn), lambda l, j: (l, 0, j)),
        compiler_params=_cparams(("arbitrary", "arbitrary")),
        name="modulation",
    )(c_all, w_mod, b_mod.reshape(n_layers, 1, n_out))


def _rms(x):
    return x * lax.rsqrt(jnp.mean(x * x, axis=-1, keepdims=True) + EPS)


def _swap_blocks(x, blk):
    lane = _lane_iota(x.shape)
    up = pltpu.roll(x, LANES - blk, 1)
    dn = pltpu.roll(x, blk, 1)
    return jnp.where((lane // blk) % 2 == 0, up, dn)


def _rope(x, cos, sin, blk):
    return x * cos + _swap_blocks(x, blk) * sin


def _pair_norm(x, g):
    lo = _lane_iota(x.shape) < HEAD_DIM
    sq = x * x
    s_lo = jnp.sum(jnp.where(lo, sq, 0.0), axis=-1, keepdims=True)
    s_hi = jnp.sum(jnp.where(lo, 0.0, sq), axis=-1, keepdims=True)
    ms = jnp.where(lo, s_lo, s_hi) * (1.0 / HEAD_DIM)
    return x * lax.rsqrt(ms + EPS) * g


def _load_token_major(ref, rows):
    return jnp.concatenate(
        [ref[pl.ds(c, rows, stride=SUBLANES), :] for c in range(SUBLANES)], axis=-1)


def _tile_rows(xc_ref, xl_ref, is_ctx):
    return jnp.where(is_ctx, xc_ref[...], xl_ref[...])


def _proj_kernel(*refs, with_f, tm, scale_a, nct):
    it = iter(refs)
    if with_f:
        x_ref = next(it)
        f_ref = next(it)
        modp_ref = next(it)
    else:
        xc_ref, xl_ref = next(it), next(it)
    mod_ref = next(it)
    gmix_ref, win_ref, gqa_ref, wqb_ref, gkva_ref, wkvb_ref, gqd_ref, gkd_ref = (next(it) for _ in range(8))
    cosh_ref, sinh_ref, cosa_ref, sina_ref = (next(it) for _ in range(4))
    if with_f:
        x2_ref = next(it)
    h_ref = next(it)
    qa_ref, ka_ref, va_ref, qb_ref, kb_ref, vb_ref, qc_ref, kc_ref, vc_ref, qd_ref, kd_ref, vd_ref = (
        next(it) for _ in range(12))

    if with_f:
        xf = x_ref[...] + modp_ref[0, 5:6, :] * _load_token_major(f_ref, tm)
        x2_ref[...] = xf
    else:
        xf = _tile_rows(xc_ref, xl_ref, pl.program_id(0) < nct)
    h = _rms(xf) * gmix_ref[...] * (1.0 + mod_ref[0, 1:2, :]) + mod_ref[0, 0:1, :]
    hb = h.astype(BF16)
    h_ref[...] = hb
    p = jnp.dot(hb, win_ref[...], preferred_element_type=F32)

    def grp(name):
        o, w = _PROJ_OFF[name]
        return p[:, o:o + w]

    cosh, sinh = cosh_ref[...], sinh_ref[...]
    cosa, sina = cosa_ref[...], sina_ref[...]

    cq = (_rms(grp("cq")) * gqa_ref[...]).astype(BF16)
    qa = jnp.dot(cq, wqb_ref[...], preferred_element_type=F32)
    for hd in range(N_HEADS):
        sl = slice(hd * LANES, (hd + 1) * LANES)
        qa_ref[:, sl] = (_rope(qa[:, sl], cosa, sina, 8) * scale_a).astype(BF16)
    ckv = (_rms(grp("ckv")) * gkva_ref[...]).astype(BF16)
    kva = jnp.dot(ckv, wkvb_ref[...], preferred_element_type=F32)
    kr = _rope(grp("kr"), cosa, sina, 8)
    for hd in range(N_HEADS):
        sl = slice(hd * LANES, (hd + 1) * LANES)
        ka_ref[:, sl] = (kva[:, sl] + kr).astype(BF16)
    va_ref[...] = kva[:, N_HEADS * LANES:].astype(BF16)

    qb = grp("qb")
    for j in range(2):
        sl = slice(j * LANES, (j + 1) * LANES)
        qb_ref[:, sl] = _rope(qb[:, sl], cosh, sinh, 16).astype(BF16)
    kb_ref[...] = _rope(grp("kb"), cosh, sinh, 16).astype(BF16)
    vb_ref[...] = grp("vb").astype(BF16)

    qc_ref[...] = grp("qc").astype(BF16)
    kc_ref[...] = grp("kc").astype(BF16)
    vc_ref[...] = grp("vc").astype(BF16)

    qd = grp("qd")
    for j in range(2):
        sl = slice(j * LANES, (j + 1) * LANES)
        qd_ref[:, sl] = _rope(_pair_norm(qd[:, sl], gqd_ref[...]), cosh, sinh, 16).astype(BF16)
    kd_ref[...] = _rope(_pair_norm(grp("kd"), gkd_ref[...]), cosh, sinh, 16).astype(BF16)
    vd_ref[...] = grp("vd").astype(BF16)


def _split_row_specs(lay, width, off=0):
    tm, nct = lay["tm"], lay["nct"]
    return [pl.BlockSpec((tm, width), lambda i: (jnp.minimum(i + off, nct - 1), 0)),
            pl.BlockSpec((tm, width), lambda i: (jnp.maximum(i + off - nct, 0), 0))]


def _proj_call(lay, x, f, modp, mod, layer, w, tabs):
    T, D, tm = lay["T"], lay["D"], lay["tm"]
    nct, tpb, B = lay["nct"], lay["tpb"], lay["B"]
    with_f = f is not None
    n_tiles = T // tm

    def mod_row(i):
        return jnp.where(i < nct, B, (i - nct) // tpb)

    def tab_blk(i):
        return jnp.where(i < nct, tpb, (i - nct) % tpb)

    row = lambda i: (i, 0)
    const = lambda i: (0, 0)
    if with_f:
        in_specs = [pl.BlockSpec((tm, D), row),
                    pl.BlockSpec((tm * SUBLANES, LANES), row),
                    pl.BlockSpec((1, 6, D), lambda i: ((layer - 1) * lay["mod_rows"] + mod_row(i), 0, 0))]
        args = [x, f, modp]
    else:
        in_specs = _split_row_specs(lay, D)
        args = list(x)
    in_specs += [pl.BlockSpec((1, 6, D), lambda i: (layer * lay["mod_rows"] + mod_row(i), 0, 0))]
    args += [mod]
    for name in ("g_mix", "w_in", "g_q_a", "w_q_b", "g_kv_a", "w_kv_b", "g_q_d", "g_k_d"):
        a = w[name]
        in_specs.append(pl.BlockSpec(a.shape, const))
        args.append(a)
    for tname in ("cos_h", "sin_h", "cos_a", "sin_a"):
        in_specs.append(pl.BlockSpec((tm, LANES), lambda i: (tab_blk(i), 0)))
        args.append(tabs[tname])

    widths = [("h", D), ("qa", 512), ("ka", 512), ("va", 256), ("qb", 256), ("kb", 128), ("vb", 128),
              ("qc", 256), ("kc", 256), ("vc", 256), ("qd", 256), ("kd", 128), ("vd", 128)]
    out_shape, out_specs = [], []
    if with_f:
        out_shape.append(jax.ShapeDtypeStruct((T, D), F32))
        out_specs.append(pl.BlockSpec((tm, D), row))
    for _, wd in widths:
        out_shape.append(jax.ShapeDtypeStruct((T, wd), BF16))
        out_specs.append(pl.BlockSpec((tm, wd), row))

    outs = pl.pallas_call(
        functools.partial(_proj_kernel, with_f=with_f, tm=tm, nct=nct,
                          scale_a=float((A_NOPE + A_ROPE) ** -0.5 * LOG2E)),
        out_shape=out_shape, grid=(n_tiles,), in_specs=in_specs, out_specs=out_specs,
        compiler_params=_cparams(("parallel",)),
        name="proj_in",
    )(*args)
    outs = list(outs)
    x2 = outs.pop(0) if with_f else x
    names = [n for n, _ in widths]
    return x2, dict(zip(names, outs))


_NT = (((1,), (1,)), ((), ()))


def _head_plan(kind):
    if kind == "A":
        return tuple((r, None, r, r // 2) for r in range(N_HEADS))
    if kind == "C":
        return tuple((r // 2, r % 2, r // 2, r // 2) for r in range(N_HEADS))
    return tuple((r // 2, r % 2, 0, 0) for r in range(N_HEADS))


def _head_query(q_ref, plan_r):
    qt, half, _, _ = plan_r
    src = q_ref[:, qt * LANES:(qt + 1) * LANES]
    if half is None:
        return src
    lane = _lane_iota(src.shape)
    keep = (lane < HEAD_DIM) if half == 0 else (lane >= HEAD_DIM)
    return jnp.where(keep, src, jnp.zeros_like(src))


def _score_chunks(q, k_blocks, bias_blocks):
    chunks = []
    for kb, bb in zip(k_blocks, bias_blocks):
        s = lax.dot_general(q, kb, _NT, preferred_element_type=F32)
        if bb is not None:
            s = s + bb
        chunks += [s[:, c * LANES:(c + 1) * LANES] for c in range(s.shape[1] // LANES)]
    return chunks


def _row_max(chunks):
    m = functools.reduce(jnp.maximum, chunks)
    return jnp.broadcast_to(jnp.max(m, axis=-1, keepdims=True), m.shape)


def _weighted_values(p_chunks, v_blocks, half):
    pv, idx = None, 0
    for vb in v_blocks:
        n = vb.shape[0] // LANES
        p = jnp.concatenate(p_chunks[idx:idx + n], axis=1).astype(BF16)
        idx += n
        lane = _lane_iota(vb.shape)
        own = (lane < HEAD_DIM) if half == 0 else (lane >= HEAD_DIM)
        d = jnp.dot(p, jnp.where(own, vb, jnp.ones_like(vb)), preferred_element_type=F32)
        pv = d if pv is None else pv + d
    return pv


def _softmax_once(s, v_aug, bias_blocks, sink):
    chunks = []
    for c in range(s.shape[1] // LANES):
        sc = s[:, c * LANES:(c + 1) * LANES]
        bb = bias_blocks[c * LANES // ROW_BLK]
        if bb is not None:
            lo = c * LANES % ROW_BLK
            sc = sc + bb[:, lo:lo + LANES]
        chunks.append(sc)
    m = _row_max(chunks)
    if sink is not None:
        m = jnp.maximum(m, sink)
    p = jnp.concatenate([jnp.exp2(c - m) for c in chunks], axis=1).astype(BF16)
    pv = jnp.dot(p, v_aug, preferred_element_type=F32)
    l = pltpu.roll(pv, HEAD_DIM, 1)
    if sink is not None:
        l = l + jnp.exp2(sink - m)
    return pv / l


def _store_heads(o_ref, outs):
    lane = _lane_iota(outs[0].shape)
    for g in range(2):
        o_ref[:, g * LANES:(g + 1) * LANES] = jnp.where(
            lane < HEAD_DIM, outs[2 * g], outs[2 * g + 1]).astype(o_ref.dtype)


def _attn_band_kernel(*refs, kind, lb, n_kv, tq):
    it = iter(refs)
    q_ref = next(it)
    k_refs = [next(it) for _ in range(n_kv)]
    v_refs = [next(it) for _ in range(n_kv)]
    bias_ref = next(it) if n_kv > 1 else None
    sink_ref = next(it) if kind == "B" else None
    o_ref = next(it)
    t = pl.program_id(0) % lb
    var = jnp.where(t == 0, 0, jnp.where(t == lb - 1, 2, 1))
    plan = _head_plan(kind)
    scores = {}
    for kt in sorted({p[2] for p in plan}):
        heads = [r for r in range(N_HEADS) if plan[r][2] == kt]
        k_all = jnp.concatenate([k[:, kt * LANES:(kt + 1) * LANES] for k in k_refs], axis=0)
        q_all = jnp.concatenate([_head_query(q_ref, plan[r]) for r in heads], axis=0)
        s_all = lax.dot_general(q_all, k_all, _NT, preferred_element_type=F32)
        for j, r in enumerate(heads):
            scores[r] = s_all[j * tq:(j + 1) * tq, :]
    outs = []
    v_aug = {}
    for r, plan_r in enumerate(plan):
        _, _, kt, vt = plan_r
        if (vt, r % 2) not in v_aug:
            vv = jnp.concatenate([v[:, vt * LANES:(vt + 1) * LANES] for v in v_refs], axis=0)
            lane = _lane_iota(vv.shape)
            own = (lane < HEAD_DIM) if r % 2 == 0 else (lane >= HEAD_DIM)
            v_aug[(vt, r % 2)] = jnp.where(own, vv, jnp.ones_like(vv))
        bias = [None]
        for j in range(n_kv - 1):
            if kind == "C":
                bias.append(bias_ref[var, j, r * tq:(r + 1) * tq, :])
            else:
                bias.append(bias_ref[var, j])
        sink = sink_ref[_GQA_PERM[r]] * LOG2E if kind == "B" else None
        outs.append(_softmax_once(scores[r], v_aug[(vt, r % 2)], bias, sink))
    _store_heads(o_ref, outs)


def _attn_dense_kernel(q_ref, kc_ref, vc_ref, kl_ref, vl_ref, o_ref, qs_ref, m_ref, acc_ref,
                       *, kind, n_steps, tq):
    s = pl.program_id(1)
    plan = _head_plan(kind)

    def update(first):
        if first:
            for r in range(N_HEADS):
                qs_ref[r * tq:(r + 1) * tq, :] = _head_query(q_ref, plan[r])
        scores = {}
        for kt in sorted({p[2] for p in plan}):
            heads = [r for r in range(N_HEADS) if plan[r][2] == kt]
            ksl = slice(kt * LANES, (kt + 1) * LANES)
            k_blocks = [kc_ref[:, ksl], kl_ref[:, ksl]] if first else [kl_ref[:, ksl]]
            q_all = qs_ref[heads[0] * tq:(heads[-1] + 1) * tq, :]
            chunks = _score_chunks(q_all, k_blocks, [None] * len(k_blocks))
            for j, r in enumerate(heads):
                scores[r] = [c[j * tq:(j + 1) * tq, :] for c in chunks]
        for r, (_, _, kt, vt) in enumerate(plan):
            rows = slice(r * tq, (r + 1) * tq)
            vsl = slice(vt * LANES, (vt + 1) * LANES)
            v_blocks = [vc_ref[:, vsl], vl_ref[:, vsl]] if first else [vl_ref[:, vsl]]
            chunks = scores[r]
            m_cur = _row_max(chunks)
            if first:
                m_new = m_cur
            else:
                m_prev = m_ref[rows, :]
                m_new = jnp.maximum(m_prev, m_cur)
                alpha = jnp.exp2(m_prev - m_new)
            p = [jnp.exp2(c - m_new) for c in chunks]
            pv = _weighted_values(p, v_blocks, r % 2)
            if first:
                acc_ref[rows, :] = pv
            else:
                acc_ref[rows, :] = alpha * acc_ref[rows, :] + pv
            m_ref[rows, :] = m_new

    @pl.when(s == 0)
    def _():
        update(True)

    @pl.when(s > 0)
    def _():
        update(False)

    @pl.when(s == n_steps - 1)
    def _():
        outs = []
        for r in range(N_HEADS):
            rows = slice(r * tq, (r + 1) * tq)
            acc = acc_ref[rows, :]
            outs.append(acc / pltpu.roll(acc, HEAD_DIM, 1))
        _store_heads(o_ref, outs)


def _attn_latent_call(lay, kind, q, k, v, *, sink=None, bias=None):
    T, B, ncb, lb, S = lay["T"], lay["B"], lay["ncb"], lay["lb"], lay["S"]
    tq = ROW_BLK
    qw, kw, vw = q.shape[1], k.shape[1], v.shape[1]
    out_shape = jax.ShapeDtypeStruct((B * S, BRANCH_W), BF16)
    if kind in ("A", "D"):
        tk, tq = lay["tk_dense"], lay["tq_dense"]
        n_steps = S // tk
        lat0 = ncb * ROW_BLK // tk
        q0 = ncb * ROW_BLK // tq
        qpb = S // tq
        lat_blk = lambda i, s: (lat0 + (i // qpb) * n_steps + s, 0)
        return pl.pallas_call(
            functools.partial(_attn_dense_kernel, kind=kind, n_steps=n_steps, tq=tq),
            out_shape=out_shape,
            grid=(B * qpb, n_steps),
            in_specs=[pl.BlockSpec((tq, qw), lambda i, s: (q0 + i, 0)),
                      pl.BlockSpec((ROW_BLK, kw), lambda i, s: (i // qpb, 0)),
                      pl.BlockSpec((ROW_BLK, vw), lambda i, s: (i // qpb, 0)),
                      pl.BlockSpec((tk, kw), lat_blk),
                      pl.BlockSpec((tk, vw), lat_blk)],
            out_specs=pl.BlockSpec((tq, BRANCH_W), lambda i, s: (i, 0)),
            scratch_shapes=[pltpu.VMEM((N_HEADS * tq, LANES), BF16),
                            pltpu.VMEM((N_HEADS * tq, LANES), F32),
                            pltpu.VMEM((N_HEADS * tq, LANES), F32)],
            compiler_params=_cparams(("parallel", "arbitrary")),
            name="attn_" + kind,
        )(q, k, v, k, v)

    tq = lay["tq_band"]
    qpb = S // tq
    q0 = ncb * ROW_BLK // tq
    bpq = tq // ROW_BLK

    def nb(i, d):
        return (ncb + (i // qpb) * lb + jnp.clip((i % qpb) * bpq + d, 0, lb - 1), 0)

    kv_maps = [lambda i: (i // qpb, 0)] + [functools.partial(nb, d=d) for d in _band_offsets(tq)]
    n_kv = len(kv_maps)
    in_specs = [pl.BlockSpec((tq, qw), lambda i: (q0 + i, 0))]
    in_specs += [pl.BlockSpec((ROW_BLK, kw), m) for m in kv_maps]
    in_specs += [pl.BlockSpec((ROW_BLK, vw), m) for m in kv_maps]
    in_specs.append(pl.BlockSpec(bias.shape, lambda i: (0,) * bias.ndim, pipeline_mode=pl.Buffered(1)))
    args = [q] + [k] * n_kv + [v] * n_kv + [bias]
    if kind == "B":
        in_specs.append(pl.BlockSpec(memory_space=pltpu.SMEM))
        args.append(sink)
    return pl.pallas_call(
        functools.partial(_attn_band_kernel, kind=kind, lb=qpb, n_kv=n_kv, tq=tq),
        out_shape=out_shape,
        grid=(B * qpb,),
        in_specs=in_specs,
        out_specs=pl.BlockSpec((tq, BRANCH_W), lambda i: (i, 0)),
        compiler_params=_cparams(("parallel",)),
        name="attn_" + kind,
    )(*args)


def _attn_context_call(lay, kind, q, k, v, *, sink=None):
    ncb = lay["ncb"]
    qw, kw, vw = q.shape[1], k.shape[1], v.shape[1]
    blk = lambda i: (i, 0)
    in_specs = [pl.BlockSpec((ROW_BLK, qw), blk), pl.BlockSpec((ROW_BLK, kw), blk),
                pl.BlockSpec((ROW_BLK, vw), blk)]
    args = [q, k, v]
    if kind == "B":
        in_specs.append(pl.BlockSpec(memory_space=pltpu.SMEM))
        args.append(sink)
    return pl.pallas_call(
        functools.partial(_attn_band_kernel, kind=kind, lb=1, n_kv=1, tq=ROW_BLK),
        out_shape=jax.ShapeDtypeStruct((ncb * ROW_BLK, BRANCH_W), BF16),
        grid=(ncb,),
        in_specs=in_specs,
        out_specs=pl.BlockSpec((ROW_BLK, BRANCH_W), blk),
        compiler_params=_cparams(("parallel",)),
        name="attn_ctx_" + kind,
    )(*args)


def _band_offsets(tq):
    return tuple(range(-1, tq // ROW_BLK + 1))


def _window_bias(S, tq):
    lb, qpb, bpq = S // ROW_BLK, S // tq, tq // ROW_BLK
    offs = _band_offsets(tq)
    qa, ka = np.arange(tq), np.arange(ROW_BLK)
    out = np.full((3, len(offs), tq, ROW_BLK), NEG_INF, np.float32)
    for vi, t_rep in enumerate((0, 1, qpb - 1)):
        for di, d in enumerate(offs):
            kt = t_rep * bpq + d
            if not 0 <= kt < lb:
                continue
            qpos = t_rep * tq + qa
            kpos = kt * ROW_BLK + ka
            ok = np.abs(qpos[:, None] - kpos[None, :]) <= WINDOW
            out[vi, di] = np.where(ok, 0.0, NEG_INF)
    return jnp.asarray(out)


def _neighbourhood_bias(rpb, rows_total, tq):
    lb = rows_total * GRID_W // ROW_BLK
    qpb, bpq = rows_total * GRID_W // tq, tq // ROW_BLK
    rpt = ROW_BLK // GRID_W
    rpq = tq // GRID_W
    kh = min(NA_KH, rows_total)
    qa, ka = np.arange(tq), np.arange(ROW_BLK)
    q_sub, q_col = qa // GRID_W, qa % GRID_W
    k_sub, k_col = ka // GRID_W, ka % GRID_W
    n_dr, n_dc = 2 * NA_KH - 1, 2 * NA_KW - 1
    col = np.arange(GRID_W)
    dc = np.clip(col[None, :] - col[:, None], -(NA_KW - 1), NA_KW - 1) + NA_KW - 1
    hot_c = (dc[:, :, None] == np.arange(n_dc)).astype(np.float32)
    by_col = jnp.einsum("huv,cdv->hucd", rpb.astype(F32) * LOG2E, jnp.asarray(hot_c),
                        precision=HIGHEST)
    offs = _band_offsets(tq)
    vals = []
    for d in offs:
        dr = np.clip(d * rpt + np.arange(rpt)[None, :] - np.arange(rpq)[:, None],
                     -(NA_KH - 1), NA_KH - 1) + NA_KH - 1
        hot_r = (dr[:, :, None] == np.arange(n_dr)).astype(np.float32)
        v = jnp.einsum("abu,hucd->hacbd", jnp.asarray(hot_r), by_col, precision=HIGHEST)
        vals.append(v.reshape(N_HEADS, tq, ROW_BLK))
    out = []
    for t_rep in (0, 1, qpb - 1):
        per_block = []
        for di, d in enumerate(offs):
            kt = t_rep * bpq + d
            q_row = t_rep * rpq + q_sub
            k_row = kt * rpt + k_sub
            r_start = np.clip(q_row - kh // 2, 0, rows_total - kh)
            row_ok = (k_row[None] >= r_start[:, None]) & (k_row[None] < r_start[:, None] + kh)
            c_start = np.clip(q_col - NA_KW // 2, 0, GRID_W - NA_KW)
            col_ok = (k_col[None] >= c_start[:, None]) & (k_col[None] < c_start[:, None] + NA_KW)
            ok = row_ok & col_ok & (0 <= kt < lb)
            per_block.append(jnp.where(jnp.asarray(ok)[None], vals[di], NEG_INF).reshape(-1, ROW_BLK))
        out.append(jnp.stack(per_block))
    return jnp.stack(out)


def _merge_kernel(*refs, tm, split_x, n_ctx_tiles):
    it = iter(refs)
    h_ref = next(it)
    is_ctx = pl.program_id(0) < n_ctx_tiles
    if n_ctx_tiles > 0:
        o_refs = [(next(it), next(it)) for _ in range(4)]
        o_vals = [_tile_rows(oc, ol, is_ctx) for oc, ol in o_refs]
    else:
        o_vals = [next(it)[...] for _ in range(4)]
    if split_x:
        xc_ref, xl_ref = next(it), next(it)
    else:
        x_ref = next(it)
    (mod_ref, gffn_ref, wg_ref, bg_ref, wb_ref, wout_ref, wr_ref, br_ref,
     x1_ref, h2_ref, route_ref, y_ref) = (next(it) for _ in range(12))
    hb = h_ref[...]
    d_model = hb.shape[1]
    for t in range(d_model // MERGE_TN):
        cs = slice(t * MERGE_TN, (t + 1) * MERGE_TN)
        y = None
        for n, o_n in enumerate(o_vals):
            gate = _sigmoid(jnp.dot(hb, wg_ref[n, :, cs], preferred_element_type=F32) + bg_ref[n, :, cs])
            u = gate * jnp.dot(o_n, wb_ref[n, :, cs], preferred_element_type=F32)
            y = u if y is None else y + u
        y_ref[:, cs] = y.astype(BF16)
    z = jnp.dot(y_ref[...], wout_ref[...], preferred_element_type=F32)
    x_in = _tile_rows(xc_ref, xl_ref, is_ctx) if split_x else x_ref[...]
    x1 = x_in + mod_ref[0, 2:3, :] * z
    x1_ref[...] = x1
    h2 = _rms(x1) * gffn_ref[...] * (1.0 + mod_ref[0, 4:5, :]) + mod_ref[0, 3:4, :]
    for c in range(SUBLANES):
        h2_ref[pl.ds(c, tm, stride=SUBLANES), :] = h2[:, c * LANES:(c + 1) * LANES]

    h2_hi = h2.astype(BF16)
    h2_lo = (h2 - h2_hi.astype(F32)).astype(BF16)
    logit = (jnp.dot(h2_hi, wr_ref[0], preferred_element_type=F32)
             + jnp.dot(h2_lo, wr_ref[0], preferred_element_type=F32)
             + jnp.dot(h2_hi, wr_ref[1], preferred_element_type=F32)) + br_ref[...]
    lane = _lane_iota(logit.shape)
    big = jnp.int32(1 << 20)
    is_g = (lane >= N_EXPERTS) & (lane < N_EXPERTS + N_GROUPS)
    gl = jnp.where(is_g, logit, NEG_INF)
    gmax = jnp.max(gl, axis=-1, keepdims=True)
    gsel = jnp.min(jnp.where(gl == gmax, lane - N_EXPERTS, big), axis=-1, keepdims=True)
    gw = 1.0 / jnp.sum(jnp.where(is_g, jnp.exp(gl - gmax), 0.0), axis=-1, keepdims=True)
    in_grp = (lane < N_EXPERTS) & ((lane // EXPERTS_PER_GROUP) == gsel)
    el = jnp.where(in_grp, logit, NEG_INF)
    v1 = jnp.max(el, axis=-1, keepdims=True)
    i1 = jnp.min(jnp.where(el == v1, lane, big), axis=-1, keepdims=True)
    el2 = jnp.where(lane == i1, NEG_INF, el)
    v2 = jnp.max(el2, axis=-1, keepdims=True)
    i2 = jnp.min(jnp.where(el2 == v2, lane, big), axis=-1, keepdims=True)
    e21 = jnp.exp(v2 - v1)
    w1 = gw / (1.0 + e21)
    w2 = gw * e21 / (1.0 + e21)
    route_ref[...] = jnp.where(lane == 0, i1.astype(F32),
                               jnp.where(lane == 1, i2.astype(F32),
                                         jnp.where(lane == 2, w1, jnp.where(lane == 3, w2, 0.0))))


def _merge_call(lay, layer, h, o, x, mod, w, *, with_ctx):
    T, D, tm = lay["T"], lay["D"], lay["tm"]
    nct, tpb, B = lay["nct"], lay["tpb"], lay["B"]
    off = 0 if with_ctx else nct
    n_tiles = T // tm - off

    def mod_row(i):
        return jnp.where(i < nct, B, (i - nct) // tpb)

    row = lambda i: (i + off, 0)
    const2 = lambda i: (0, 0)
    const3 = lambda i: (0, 0, 0)
    split_x = isinstance(x, (tuple, list))
    in_specs = [pl.BlockSpec((tm, D), row)]
    o_args = []
    for kind in ("A", "B", "C", "D"):
        o_ctx, o_lat = o[kind]
        if with_ctx:
            in_specs += _split_row_specs(lay, BRANCH_W)
            o_args += [o_ctx, o_lat]
        else:
            in_specs.append(pl.BlockSpec((tm, BRANCH_W), lambda i: (i, 0)))
            o_args.append(o_lat)
    in_specs += _split_row_specs(lay, D, off) if split_x else [pl.BlockSpec((tm, D), row)]
    in_specs += [pl.BlockSpec((1, 6, D), lambda i: (layer * lay["mod_rows"] + mod_row(i + off), 0, 0)),
                 pl.BlockSpec((1, D), const2),
                 pl.BlockSpec(w["w_gate"].shape, const3),
                 pl.BlockSpec(w["b_gate"].shape, const3),
                 pl.BlockSpec(w["w_branch"].shape, const3),
                 pl.BlockSpec((D, D), const2),
                 pl.BlockSpec((2, D, LANES), const3),
                 pl.BlockSpec((1, LANES), const2)]
    out_shape = [jax.ShapeDtypeStruct((T, D), F32),
                 jax.ShapeDtypeStruct((T * SUBLANES, LANES), F32),
                 jax.ShapeDtypeStruct((T, LANES), F32)]
    out_specs = [pl.BlockSpec((tm, D), row),
                 pl.BlockSpec((tm * SUBLANES, LANES), row),
                 pl.BlockSpec((tm, LANES), row)]
    return pl.pallas_call(
        functools.partial(_merge_kernel, tm=tm, split_x=split_x, n_ctx_tiles=nct - off),
        out_shape=out_shape, grid=(n_tiles,), in_specs=in_specs, out_specs=out_specs,
        scratch_shapes=[pltpu.VMEM((tm, D), BF16)],
        compiler_params=_cparams(("parallel",)),
        name="merge",
    )(h, *o_args, *(x if split_x else (x,)), mod, w["g_ffn"], w["w_gate"],
      w["b_gate"], w["w_branch"], w["w_out"], w["w_route"], w["b_route"])


def _moe_kernel(tab_ref, tok_ref, h2_ref, sw_ref, w1_ref, w3_ref, w2_ref, f_ref,
                xg_ref, y_ref, st_ref, *, blk, slots, tile_off):
    ti = pl.program_id(0)
    step = pl.program_id(1)

    @pl.when(step == 0)
    def _():
        f_ref[...] = jnp.zeros_like(f_ref)
        xg_ref[...] = jnp.zeros_like(xg_ref)

    tok_base = (ti + tile_off) * slots
    for j in range(MOE_EXPERTS_PER_STEP):
        base = ((ti + tile_off) * N_EXPERTS + step * MOE_EXPERTS_PER_STEP + j) * 2
        _moe_expert(tab_ref[base], tab_ref[base + 1], tok_base, tok_ref, h2_ref, sw_ref,
                    w1_ref.at[j], w3_ref.at[j], w2_ref.at[j], f_ref, xg_ref, y_ref, st_ref, blk)


def _moe_expert(seg0, n_pad, tok_base, tok_ref, h2_ref, sw_ref, w1_ref, w3_ref, w2_ref, f_ref,
                xg_ref, y_ref, st_ref, blk):
    n_blocks = (n_pad + blk - 1) // blk

    def block_body(b, carry):
        off = pl.multiple_of(seg0 + b * blk, MOE_GROUP)
        n_grp = jnp.minimum(n_pad - b * blk, blk) // MOE_GROUP
        slot0 = tok_base + off

        def gather(gi, c):
            r0 = gi * MOE_GROUP
            srcs = [pl.multiple_of(tok_ref[slot0 + r0 + u], SUBLANES) for u in range(MOE_GROUP)]
            for u in range(MOE_GROUP):
                dst = pl.multiple_of((r0 + u) * SUBLANES, SUBLANES)
                xg_ref[pl.ds(dst, SUBLANES), :] = h2_ref[pl.ds(srcs[u], SUBLANES), :]
            return c

        lax.fori_loop(0, n_grp, gather, 0)
        xb = _load_token_major(xg_ref, blk).astype(BF16)
        a = jnp.dot(xb, w1_ref[...], preferred_element_type=F32)
        g = jnp.dot(xb, w3_ref[...], preferred_element_type=F32)
        hid = (a * _sigmoid(a) * g).astype(BF16)
        y = jnp.dot(hid, w2_ref[...], preferred_element_type=F32)
        y_ref[...] = y * sw_ref[0, pl.ds(off, blk), :]

        def scatter(gi, c):
            r0 = pl.multiple_of(gi * MOE_GROUP, MOE_GROUP)
            dsts = [pl.multiple_of(tok_ref[slot0 + r0 + u], SUBLANES) for u in range(MOE_GROUP)]
            for ch in range(SUBLANES):
                st_ref[pl.ds(ch, MOE_GROUP, stride=SUBLANES), :] = y_ref[pl.ds(r0, MOE_GROUP),
                                                                         ch * LANES:(ch + 1) * LANES]
            vals = [f_ref[pl.ds(dsts[u], SUBLANES), :] + st_ref[u * SUBLANES:(u + 1) * SUBLANES, :]
                    for u in range(MOE_GROUP)]
            for u in reversed(range(MOE_GROUP)):
                f_ref[pl.ds(dsts[u], SUBLANES), :] = vals[u]
            return c

        lax.fori_loop(0, n_grp, scatter, 0)
        return carry

    lax.fori_loop(0, n_blocks, block_body, 0)


def _moe_call(lay, route, h2, w, *, with_ctx):
    T, D, tt = lay["T"], lay["D"], lay["tt"]
    blk = lay["moe_blk"]
    n_tiles_all = T // tt
    tile_off = 0 if with_ctx else lay["ncb"] * ROW_BLK // tt
    n_tiles = n_tiles_all - tile_off
    n_assign = 2 * tt
    n_fill = N_EXPERTS * MOE_GROUP
    slots = n_assign + n_fill + blk

    eid = route[:, 0:2].astype(jnp.int32).reshape(n_tiles_all, n_assign)
    wts = route[:, 2:4].reshape(n_tiles_all, n_assign)
    tok = jnp.broadcast_to(jnp.arange(n_assign, dtype=jnp.int32)[None, :] // 2 * SUBLANES, eid.shape)
    ex = jnp.arange(N_EXPERTS, dtype=jnp.int32)
    counts = jnp.sum(eid[:, :, None] == ex[None, None, :], axis=1, dtype=jnp.int32)
    n_dummy = (-counts) % MOE_GROUP
    fill_key = jnp.where(jnp.arange(MOE_GROUP, dtype=jnp.int32)[None, None, :] < n_dummy[:, :, None],
                         ex[None, :, None], N_EXPERTS).reshape(n_tiles_all, n_fill)
    zeros_i = jnp.zeros((n_tiles_all, n_fill), jnp.int32)
    _, slot_tok, slot_w = lax.sort(
        (jnp.concatenate([eid, fill_key], axis=1), jnp.concatenate([tok, zeros_i], axis=1),
         jnp.concatenate([wts, zeros_i.astype(F32)], axis=1)),
        dimension=1, is_stable=True, num_keys=1)
    slot_tok = jnp.pad(slot_tok, ((0, 0), (0, blk)))
    slot_w = jnp.pad(slot_w, ((0, 0), (0, blk)))
    padded = counts + n_dummy
    seg0 = jnp.cumsum(padded, axis=1) - padded
    tab = jnp.stack([seg0, padded], axis=-1).reshape(-1).astype(jnp.int32)

    return pl.pallas_call(
        functools.partial(_moe_kernel, blk=blk, slots=slots, tile_off=tile_off),
        out_shape=jax.ShapeDtypeStruct((T * SUBLANES, LANES), F32),
        grid_spec=pltpu.PrefetchScalarGridSpec(
            num_scalar_prefetch=2,
            grid=(n_tiles, N_EXPERTS // MOE_EXPERTS_PER_STEP),
            in_specs=[
                pl.BlockSpec((tt * SUBLANES, LANES), lambda t, e, *_: (t + tile_off, 0)),
                pl.BlockSpec((1, slots, 1), lambda t, e, *_: (t + tile_off, 0, 0)),
                pl.BlockSpec((MOE_EXPERTS_PER_STEP, D, EXPERT_FF), lambda t, e, *_: (e, 0, 0)),
                pl.BlockSpec((MOE_EXPERTS_PER_STEP, D, EXPERT_FF), lambda t, e, *_: (e, 0, 0)),
                pl.BlockSpec((MOE_EXPERTS_PER_STEP, EXPERT_FF, D), lambda t, e, *_: (e, 0, 0)),
            ],
            out_specs=pl.BlockSpec((tt * SUBLANES, LANES), lambda t, e, *_: (t + tile_off, 0)),
            scratch_shapes=[pltpu.VMEM((blk * SUBLANES, LANES), F32),
                            pltpu.VMEM((blk, D), F32),
                            pltpu.VMEM((MOE_GROUP * SUBLANES, LANES), F32)],
        ),
        compiler_params=_cparams(("parallel", "arbitrary")),
        name="moe_experts",
    )(tab, slot_tok.reshape(-1), h2, slot_w.reshape(n_tiles_all, slots, 1),
      w["w_ff1"], w["w_ff3"], w["w_ff2"])


def _final_kernel(x_ref, f_ref, mod_ref, g_ref, o_ref, *, tm):
    xf = x_ref[...] + mod_ref[0, 5:6, :] * _load_token_major(f_ref, tm)
    o_ref[...] = _rms(xf) * g_ref[...]


def _final_call(lay, layer, x1, f, mod, g_final):
    T, D, tm = lay["T"], lay["D"], lay["tm"]
    nct, tpb = lay["nct"], lay["tpb"]
    n_lat = T // tm - nct
    return pl.pallas_call(
        functools.partial(_final_kernel, tm=tm),
        out_shape=jax.ShapeDtypeStruct((n_lat * tm, D), F32),
        grid=(n_lat,),
        in_specs=[pl.BlockSpec((tm, D), lambda i: (i + nct, 0)),
                  pl.BlockSpec((tm * SUBLANES, LANES), lambda i: (i + nct, 0)),
                  pl.BlockSpec((1, 6, D), lambda i: (layer * lay["mod_rows"] + i // tpb, 0, 0)),
                  pl.BlockSpec((1, D), lambda i: (0, 0))],
        out_specs=pl.BlockSpec((tm, D), lambda i: (i, 0)),
        compiler_params=_cparams(("parallel",)),
        name="final_norm",
    )(x1, f, mod, g_final)


def _select_cols(wm, segs, scale=None):
    parts = []
    for k, (start, width) in enumerate(segs):
        if start is None:
            parts.append(jnp.zeros((wm.shape[0], width), wm.dtype))
        else:
            blk = wm[:, start:start + width]
            parts.append(blk if scale is None or scale[k] is None else blk * scale[k])
    return jnp.concatenate(parts, axis=1)


def _prep_layer(l, p):
    a_cols = A_Q_RANK + A_KV_RANK + A_ROPE
    b_off = a_cols
    c_off = b_off + 512
    d_off = c_off + 768
    qk_scale = HEAD_DIM ** -0.5 * LOG2E
    gqa_q = lambda off: [(off + hh * HEAD_DIM, HEAD_DIM) for hh in _GQA_PERM]
    segs = ([(0, 256), (256, 128), (None, 64), (384, 32), (None, 32)]
            + gqa_q(b_off) + [(b_off + 256, 128), (b_off + 384, 128)]
            + [(c_off, 256), (c_off + 256, 256), (c_off + 512, 256)]
            + gqa_q(d_off) + [(d_off + 256, 128), (d_off + 384, 128)])
    scale = [None] * len(segs)
    for k in (5, 6, 7, 8, 11):
        scale[k] = qk_scale
    w_in = _select_cols(p["w_in"][l], segs, scale).astype(BF16)
    assert w_in.shape[1] == PROJ_COLS

    hq = A_NOPE + A_ROPE
    segs_q = []
    for hh in range(N_HEADS):
        segs_q += [(hh * hq, hq), (None, LANES - hq)]
    w_q_b = _select_cols(p["w_q_b"][l], segs_q).astype(BF16)
    hk = A_NOPE + HEAD_DIM
    segs_k = []
    for hh in range(N_HEADS):
        segs_k += [(hh * hk, A_NOPE), (None, LANES - A_NOPE)]
    segs_k += [(hh * hk + A_NOPE, HEAD_DIM) for hh in range(N_HEADS)]
    w_kv_b = _select_cols(p["w_kv_b"][l], segs_k).astype(BF16)

    wb = p["w_branch"][l]
    perm_rows = lambda m: jnp.concatenate([m[hh * HEAD_DIM:(hh + 1) * HEAD_DIM] for hh in _GQA_PERM], axis=0)
    w_branch = jnp.stack([wb[0], perm_rows(wb[1]), wb[2], perm_rows(wb[3])]).astype(BF16)

    d = p["w_in"].shape[1]
    w_route = jnp.zeros((d, LANES), F32)
    w_route = w_route.at[:, :N_EXPERTS].set(p["w_router"][l]).at[:, N_EXPERTS:N_EXPERTS + N_GROUPS].set(p["w_group"][l])
    b_route = jnp.zeros((1, LANES), F32)
    b_route = b_route.at[0, :N_EXPERTS].set(p["b_router"][l]).at[0, N_EXPERTS:N_EXPERTS + N_GROUPS].set(p["b_group"][l])
    return {
        "g_mix": p["g_norm_mix"][l][None, :],
        "w_in": w_in,
        "g_q_a": p["g_q_a"][l][None, :],
        "w_q_b": w_q_b,
        "g_kv_a": p["g_kv_a"][l][None, :],
        "w_kv_b": w_kv_b,
        "g_q_d": (jnp.tile(p["g_q_d"][l], 2) * qk_scale)[None, :],
        "g_k_d": jnp.tile(p["g_k_d"][l], 2)[None, :],
        "sink": p["sink_b"][l],
        "rpb": p["rpb_c"][l],
        "w_gate": p["w_gate"][l].astype(BF16),
        "b_gate": p["b_gate"][l][:, None, :],
        "w_branch": w_branch,
        "w_out": p["w_out"][l].astype(BF16),
        "g_ffn": p["g_norm_ffn"][l][None, :],
        "w_route": jnp.stack([w_route.astype(BF16),
                              (w_route - w_route.astype(BF16).astype(F32)).astype(BF16)]),
        "b_route": b_route,
        "w_ff1": p["w_ff1"][l].astype(BF16),
        "w_ff3": p["w_ff3"][l].astype(BF16),
        "w_ff2": p["w_ff2"][l].astype(BF16),
    }


def _rope_tables(S, tm):
    t = np.arange(S)
    rows = (t // GRID_W).astype(np.float32)
    cols = (t % GRID_W).astype(np.float32)

    def cs(rot):
        half = rot // 2
        inv = np.float32(ROPE_THETA) ** (-np.arange(0, half, 2, dtype=np.float32) / np.float32(half))
        ar_, ac_ = rows[:, None] * inv, cols[:, None] * inv
        cos = np.concatenate([np.cos(ar_), np.cos(ar_), np.cos(ac_), np.cos(ac_)], axis=-1)
        sin = np.concatenate([-np.sin(ar_), np.sin(ar_), -np.sin(ac_), np.sin(ac_)], axis=-1)
        return cos.astype(np.float32), sin.astype(np.float32)

    cos64, sin64 = cs(HEAD_DIM)
    cos32, sin32 = cs(A_ROPE)
    ones = lambda n: np.ones((S, n), np.float32)
    zeros = lambda n: np.zeros((S, n), np.float32)
    tabs = {
        "cos_h": np.concatenate([cos64, cos64], axis=-1),
        "sin_h": np.concatenate([sin64, sin64], axis=-1),
        "cos_a": np.concatenate([ones(A_NOPE), cos32, ones(LANES - A_NOPE - A_ROPE)], axis=-1),
        "sin_a": np.concatenate([zeros(A_NOPE), sin32, zeros(LANES - A_NOPE - A_ROPE)], axis=-1),
    }
    ident = {"cos_h": 1.0, "sin_h": 0.0, "cos_a": 1.0, "sin_a": 0.0}
    return {k: jnp.asarray(np.concatenate([v, np.full((tm, LANES), ident[k], np.float32)], axis=0))
            for k, v in tabs.items()}


def _layout(B, S, n_ctx, D):
    assert n_ctx == ROW_BLK and S % 1024 == 0 and S // GRID_W >= 3 * (ROW_BLK // GRID_W)
    T = B * (n_ctx + S)
    tm = 512 if (B * n_ctx) % 512 == 0 else 256
    tk_dense = 1024 if (B * n_ctx) % 1024 == 0 else (512 if (B * n_ctx) % 512 == 0 else 256)
    tt = 2048 if (B * n_ctx) % 2048 == 0 else B * n_ctx
    assert S % tt == 0
    return {
        "B": B, "S": S, "D": D, "T": T, "tm": tm,
        "ncb": B * n_ctx // ROW_BLK,
        "lb": S // ROW_BLK,
        "nct": B * n_ctx // tm,
        "tpb": S // tm,
        "tk_dense": tk_dense,
        "tq_dense": tk_dense,
        "tq_band": tm,
        "tt": tt, "moe_blk": 160,
        "mod_rows": 16,
    }


def kernel(x, c, ctx, c_ctx, w_mod, b_mod, g_norm_mix, w_in, g_q_a, w_q_b, g_kv_a, w_kv_b, sink_b, rpb_c,
           g_q_d, g_k_d, w_gate, b_gate, w_branch, w_out, g_norm_ffn, w_group, b_group, w_router, b_router,
           w_ff1, w_ff3, w_ff2, g_final):
    B, S, D = x.shape
    n_ctx = ctx.shape[1]
    depth = w_mod.shape[0]
    lay = _layout(B, S, n_ctx, D)
    params = dict(w_in=w_in, g_norm_mix=g_norm_mix, g_q_a=g_q_a, w_q_b=w_q_b, g_kv_a=g_kv_a, w_kv_b=w_kv_b,
                  sink_b=sink_b, rpb_c=rpb_c, g_q_d=g_q_d, g_k_d=g_k_d, w_gate=w_gate, b_gate=b_gate,
                  w_branch=w_branch, w_out=w_out, g_norm_ffn=g_norm_ffn, w_group=w_group, b_group=b_group,
                  w_router=w_router, b_router=b_router, w_ff1=w_ff1, w_ff3=w_ff3, w_ff2=w_ff2)

    c_all = jnp.zeros((lay["mod_rows"], D), F32).at[:B].set(c).at[B].set(c_ctx)
    mod = _modulation(c_all, w_mod, b_mod).reshape(depth * lay["mod_rows"], 6, D)
    tabs = _rope_tables(S, lay["tm"])
    win_bias = _window_bias(S, lay["tq_band"])
    xf = (ctx.reshape(B * n_ctx, D), x.reshape(B * S, D))

    f = None
    for l in range(depth):
        with_ctx = l < depth - 1
        w = _prep_layer(l, params)
        xf, pr = _proj_call(lay, xf, f, mod, mod, l, w, tabs)
        o = {}
        for kind, kl in (("A", "a"), ("B", "b"), ("C", "c"), ("D", "d")):
            q, k, v = pr["q" + kl], pr["k" + kl], pr["v" + kl]
            sink = w["sink"] if kind == "B" else None
            bias = win_bias if kind == "B" else None
            if kind == "C":
                bias = _neighbourhood_bias(w["rpb"], S // GRID_W, lay["tq_band"])
            o_ctx = _attn_context_call(lay, kind, q, k, v, sink=sink) if with_ctx else None
            o[kind] = (o_ctx, _attn_latent_call(lay, kind, q, k, v, sink=sink, bias=bias))
        xf, h2, route = _merge_call(lay, l, pr["h"], o, xf, mod, w, with_ctx=with_ctx)
        f = _moe_call(lay, route, h2, w, with_ctx=with_ctx)
    out = _final_call(lay, depth - 1, xf, f, mod, g_final[None, :])
    return out.reshape(B, S, D)
```

```python
import functools

import numpy as np
import jax
import jax.numpy as jnp
from jax import lax
from jax.experimental import pallas as pl
from jax.experimental.pallas import tpu as pltpu

F32 = jnp.float32
BF16 = jnp.bfloat16
HIGHEST = lax.Precision.HIGHEST

GRID_W = 64
ROPE_THETA = 10000.0
EPS = 1e-6
NEG_INF = -1e30
LOG2E = 1.4426950408889634
HEAD_DIM = 64
N_HEADS = 4
BRANCH_W = 256
A_Q_RANK = 256
A_KV_RANK = 128
A_NOPE = 64
A_ROPE = 32
NA_KH = 8
NA_KW = 16
WINDOW = 128
N_GROUPS = 4
EXPERTS_PER_GROUP = 8
N_EXPERTS = 32
EXPERT_FF = 256

LANES = 128
SUBLANES = 8
ROW_BLK = 256
MERGE_TN = 256
MOE_EXPERTS_PER_STEP = 2
MOE_GROUP = 16
VMEM_LIMIT = 56 * 1024 * 1024

_PROJ_GROUPS = (("cq", 256), ("ckv", 128), ("kr", 128), ("qb", 256), ("kb", 128), ("vb", 128),
                ("qc", 256), ("kc", 256), ("vc", 256), ("qd", 256), ("kd", 128), ("vd", 128))
_PROJ_OFF = {}
_o = 0
for _n, _w in _PROJ_GROUPS:
    _PROJ_OFF[_n] = (_o, _w)
    _o += _w
PROJ_COLS = _o
_GQA_PERM = (0, 2, 1, 3)


def _cparams(sem):
    return pltpu.CompilerParams(dimension_semantics=sem, vmem_limit_bytes=VMEM_LIMIT)


def _lane_iota(shape):
    return lax.broadcasted_iota(jnp.int32, shape, len(shape) - 1)


def _sigmoid(x):
    return 1.0 / (1.0 + jnp.exp(-x))


def _mod_kernel(c_ref, w_ref, b_ref, o_ref):
    cf = c_ref[...]
    s = cf * _sigmoid(cf)
    o_ref[0] = jnp.dot(s, w_ref[0], precision=HIGHEST, preferred_element_type=F32) + b_ref[0]


def _modulation(c_all, w_mod, b_mod):
    n_layers, d, n_out = w_mod.shape
    rows = c_all.shape[0]
    tn = 1536
    return pl.pallas_call(
        _mod_kernel,
        out_shape=jax.ShapeDtypeStruct((n_layers, rows, n_out), F32),
        grid=(n_layers, n_out // tn),
        in_specs=[pl.BlockSpec((rows, d), lambda l, j: (0, 0)),
                  pl.BlockSpec((1, d, tn), lambda l, j: (l, 0, j)),
                  pl.BlockSpec((1, 1, tn), lambda l, j: (l, 0, j))],
        out_specs=pl.BlockSpec((1, rows, tn), lambda l, j: (l, 0, j)),
        compiler_params=_cparams(("arbitrary", "arbitrary")),
        name="modulation",
    )(c_all, w_mod, b_mod.reshape(n_layers, 1, n_out))


def _rms(x):
    return x * lax.rsqrt(jnp.mean(x * x, axis=-1, keepdims=True) + EPS)


def _swap_blocks(x, blk):
    lane = _lane_iota(x.shape)
    up = pltpu.roll(x, LANES - blk, 1)
    dn = pltpu.roll(x, blk, 1)
    return jnp.where((lane // blk) % 2 == 0, up, dn)


def _rope(x, cos, sin, blk):
    return x * cos + _swap_blocks(x, blk) * sin


def _pair_norm(x, g):
    lo = _lane_iota(x.shape) < HEAD_DIM
    sq = x * x
    s_lo = jnp.sum(jnp.where(lo, sq, 0.0), axis=-1, keepdims=True)
    s_hi = jnp.sum(jnp.where(lo, 0.0, sq), axis=-1, keepdims=True)
    ms = jnp.where(lo, s_lo, s_hi) * (1.0 / HEAD_DIM)
    return x * lax.rsqrt(ms + EPS) * g


def _load_token_major(ref, rows):
    return jnp.concatenate(
        [ref[pl.ds(c, rows, stride=SUBLANES), :] for c in range(SUBLANES)], axis=-1)


def _tile_rows(xc_ref, xl_ref, is_ctx):
    return jnp.where(is_ctx, xc_ref[...], xl_ref[...])


def _proj_kernel(*refs, with_f, tm, scale_a, nct):
    it = iter(refs)
    if with_f:
        x_ref = next(it)
        f_ref = next(it)
        modp_ref = next(it)
    else:
        xc_ref, xl_ref = next(it), next(it)
    mod_ref = next(it)
    gmix_ref, win_ref, gqa_ref, wqb_ref, gkva_ref, wkvb_ref, gqd_ref, gkd_ref = (next(it) for _ in range(8))
    cosh_ref, sinh_ref, cosa_ref, sina_ref = (next(it) for _ in range(4))
    if with_f:
        x2_ref = next(it)
    h_ref = next(it)
    qa_ref, ka_ref, va_ref, qb_ref, kb_ref, vb_ref, qc_ref, kc_ref, vc_ref, qd_ref, kd_ref, vd_ref = (
        next(it) for _ in range(12))

    if with_f:
        xf = x_ref[...] + modp_ref[0, 5:6, :] * _load_token_major(f_ref, tm)
        x2_ref[...] = xf
    else:
        xf = _tile_rows(xc_ref, xl_ref, pl.program_id(0) < nct)
    h = _rms(xf) * gmix_ref[...] * (1.0 + mod_ref[0, 1:2, :]) + mod_ref[0, 0:1, :]
    hb = h.astype(BF16)
    h_ref[...] = hb
    p = jnp.dot(hb, win_ref[...], preferred_element_type=F32)

    def grp(name):
        o, w = _PROJ_OFF[name]
        return p[:, o:o + w]

    cosh, sinh = cosh_ref[...], sinh_ref[...]
    cosa, sina = cosa_ref[...], sina_ref[...]

    cq = (_rms(grp("cq")) * gqa_ref[...]).astype(BF16)
    qa = jnp.dot(cq, wqb_ref[...], preferred_element_type=F32)
    for hd in range(N_HEADS):
        sl = slice(hd * LANES, (hd + 1) * LANES)
        qa_ref[:, sl] = (_rope(qa[:, sl], cosa, sina, 8) * scale_a).astype(BF16)
    ckv = (_rms(grp("ckv")) * gkva_ref[...]).astype(BF16)
    kva = jnp.dot(ckv, wkvb_ref[...], preferred_element_type=F32)
    kr = _rope(grp("kr"), cosa, sina, 8)
    for hd in range(N_HEADS):
        sl = slice(hd * LANES, (hd + 1) * LANES)
        ka_ref[:, sl] = (kva[:, sl] + kr).astype(BF16)
    va_ref[...] = kva[:, N_HEADS * LANES:].astype(BF16)

    qb = grp("qb")
    for j in range(2):
        sl = slice(j * LANES, (j + 1) * LANES)
        qb_ref[:, sl] = _rope(qb[:, sl], cosh, sinh, 16).astype(BF16)
    kb_ref[...] = _rope(grp("kb"), cosh, sinh, 16).astype(BF16)
    vb_ref[...] = grp("vb").astype(BF16)

    qc_ref[...] = grp("qc").astype(BF16)
    kc_ref[...] = grp("kc").astype(BF16)
    vc_ref[...] = grp("vc").astype(BF16)

    qd = grp("qd")
    for j in range(2):
        sl = slice(j * LANES, (j + 1) * LANES)
        qd_ref[:, sl] = _rope(_pair_norm(qd[:, sl], gqd_ref[...]), cosh, sinh, 16).astype(BF16)
    kd_ref[...] = _rope(_pair_norm(grp("kd"), gkd_ref[...]), cosh, sinh, 16).astype(BF16)
    vd_ref[...] = grp("vd").astype(BF16)


def _split_row_specs(lay, width, off=0):
    tm, nct = lay["tm"], lay["nct"]
    return [pl.BlockSpec((tm, width), lambda i: (jnp.minimum(i + off, nct - 1), 0)),
            pl.BlockSpec((tm, width), lambda i: (jnp.maximum(i + off - nct, 0), 0))]


def _proj_call(lay, x, f, modp, mod, layer, w, tabs):
    T, D, tm = lay["T"], lay["D"], lay["tm"]
    nct, tpb, B = lay["nct"], lay["tpb"], lay["B"]
    with_f = f is not None
    n_tiles = T // tm

    def mod_row(i):
        return jnp.where(i < nct, B, (i - nct) // tpb)

    def tab_blk(i):
        return jnp.where(i < nct, tpb, (i - nct) % tpb)

    row = lambda i: (i, 0)
    const = lambda i: (0, 0)
    if with_f:
        in_specs = [pl.BlockSpec((tm, D), row),
                    pl.BlockSpec((tm * SUBLANES, LANES), row),
                    pl.BlockSpec((1, 6, D), lambda i: ((layer - 1) * lay["mod_rows"] + mod_row(i), 0, 0))]
        args = [x, f, modp]
    else:
        in_specs = _split_row_specs(lay, D)
        args = list(x)
    in_specs += [pl.BlockSpec((1, 6, D), lambda i: (layer * lay["mod_rows"] + mod_row(i), 0, 0))]
    args += [mod]
    for name in ("g_mix", "w_in", "g_q_a", "w_q_b", "g_kv_a", "w_kv_b", "g_q_d", "g_k_d"):
        a = w[name]
        in_specs.append(pl.BlockSpec(a.shape, const))
        args.append(a)
    for tname in ("cos_h", "sin_h", "cos_a", "sin_a"):
        in_specs.append(pl.BlockSpec((tm, LANES), lambda i: (tab_blk(i), 0)))
        args.append(tabs[tname])

    widths = [("h", D), ("qa", 512), ("ka", 512), ("va", 256), ("qb", 256), ("kb", 128), ("vb", 128),
              ("qc", 256), ("kc", 256), ("vc", 256), ("qd", 256), ("kd", 128), ("vd", 128)]
    out_shape, out_specs = [], []
    if with_f:
        out_shape.append(jax.ShapeDtypeStruct((T, D), F32))
        out_specs.append(pl.BlockSpec((tm, D), row))
    for _, wd in widths:
        out_shape.append(jax.ShapeDtypeStruct((T, wd), BF16))
        out_specs.append(pl.BlockSpec((tm, wd), row))

    outs = pl.pallas_call(
        functools.partial(_proj_kernel, with_f=with_f, tm=tm, nct=nct,
                          scale_a=float((A_NOPE + A_ROPE) ** -0.5 * LOG2E)),
        out_shape=out_shape, grid=(n_tiles,), in_specs=in_specs, out_specs=out_specs,
        compiler_params=_cparams(("parallel",)),
        name="proj_in",
    )(*args)
    outs = list(outs)
    x2 = outs.pop(0) if with_f else x
    names = [n for n, _ in widths]
    return x2, dict(zip(names, outs))


_NT = (((1,), (1,)), ((), ()))


def _head_plan(kind):
    if kind == "A":
        return tuple((r, None, r, r // 2) for r in range(N_HEADS))
    if kind == "C":
        return tuple((r // 2, r % 2, r // 2, r // 2) for r in range(N_HEADS))
    return tuple((r // 2, r % 2, 0, 0) for r in range(N_HEADS))


def _head_query(q_ref, plan_r):
    qt, half, _, _ = plan_r
    src = q_ref[:, qt * LANES:(qt + 1) * LANES]
    if half is None:
        return src
    lane = _lane_iota(src.shape)
    keep = (lane < HEAD_DIM) if half == 0 else (lane >= HEAD_DIM)
    return jnp.where(keep, src, jnp.zeros_like(src))


def _score_chunks(q, k_blocks, bias_blocks):
    chunks = []
    for kb, bb in zip(k_blocks, bias_blocks):
        s = lax.dot_general(q, kb, _NT, preferred_element_type=F32)
        if bb is not None:
            s = s + bb
        chunks += [s[:, c * LANES:(c + 1) * LANES] for c in range(s.shape[1] // LANES)]
    return chunks


def _row_max(chunks):
    m = functools.reduce(jnp.maximum, chunks)
    return jnp.broadcast_to(jnp.max(m, axis=-1, keepdims=True), m.shape)


def _weighted_values(p_chunks, v_blocks, half):
    pv, idx = None, 0
    for vb in v_blocks:
        n = vb.shape[0] // LANES
        p = jnp.concatenate(p_chunks[idx:idx + n], axis=1).astype(BF16)
        idx += n
        lane = _lane_iota(vb.shape)
        own = (lane < HEAD_DIM) if half == 0 else (lane >= HEAD_DIM)
        d = jnp.dot(p, jnp.where(own, vb, jnp.ones_like(vb)), preferred_element_type=F32)
        pv = d if pv is None else pv + d
    return pv


def _softmax_once(s, v_aug, bias_blocks, sink):
    chunks = []
    for c in range(s.shape[1] // LANES):
        sc = s[:, c * LANES:(c + 1) * LANES]
        bb = bias_blocks[c * LANES // ROW_BLK]
        if bb is not None:
            lo = c * LANES % ROW_BLK
            sc = sc + bb[:, lo:lo + LANES]
        chunks.append(sc)
    m = _row_max(chunks)
    if sink is not None:
        m = jnp.maximum(m, sink)
    p = jnp.concatenate([jnp.exp2(c - m) for c in chunks], axis=1).astype(BF16)
    pv = jnp.dot(p, v_aug, preferred_element_type=F32)
    l = pltpu.roll(pv, HEAD_DIM, 1)
    if sink is not None:
        l = l + jnp.exp2(sink - m)
    return pv / l


def _store_heads(o_ref, outs):
    lane = _lane_iota(outs[0].shape)
    for g in range(2):
        o_ref[:, g * LANES:(g + 1) * LANES] = jnp.where(
            lane < HEAD_DIM, outs[2 * g], outs[2 * g + 1]).astype(o_ref.dtype)


def _attn_band_kernel(*refs, kind, lb, n_kv, tq):
    it = iter(refs)
    q_ref = next(it)
    k_refs = [next(it) for _ in range(n_kv)]
    v_refs = [next(it) for _ in range(n_kv)]
    bias_ref = next(it) if n_kv > 1 else None
    sink_ref = next(it) if kind == "B" else None
    o_ref = next(it)
    t = pl.program_id(0) % lb
    var = jnp.where(t == 0, 0, jnp.where(t == lb - 1, 2, 1))
    plan = _head_plan(kind)
    scores = {}
    for kt in sorted({p[2] for p in plan}):
        heads = [r for r in range(N_HEADS) if plan[r][2] == kt]
        k_all = jnp.concatenate([k[:, kt * LANES:(kt + 1) * LANES] for k in k_refs], axis=0)
        q_all = jnp.concatenate([_head_query(q_ref, plan[r]) for r in heads], axis=0)
        s_all = lax.dot_general(q_all, k_all, _NT, preferred_element_type=F32)
        for j, r in enumerate(heads):
            scores[r] = s_all[j * tq:(j + 1) * tq, :]
    outs = []
    v_aug = {}
    for r, plan_r in enumerate(plan):
        _, _, kt, vt = plan_r
        if (vt, r % 2) not in v_aug:
            vv = jnp.concatenate([v[:, vt * LANES:(vt + 1) * LANES] for v in v_refs], axis=0)
            lane = _lane_iota(vv.shape)
            own = (lane < HEAD_DIM) if r % 2 == 0 else (lane >= HEAD_DIM)
            v_aug[(vt, r % 2)] = jnp.where(own, vv, jnp.ones_like(vv))
        bias = [None]
        for j in range(n_kv - 1):
            if kind == "C":
                bias.append(bias_ref[var, j, r * tq:(r + 1) * tq, :])
            else:
                bias.append(bias_ref[var, j])
        sink = sink_ref[_GQA_PERM[r]] * LOG2E if kind == "B" else None
        outs.append(_softmax_once(scores[r], v_aug[(vt, r % 2)], bias, sink))
    _store_heads(o_ref, outs)


def _attn_dense_kernel(q_ref, kc_ref, vc_ref, kl_ref, vl_ref, o_ref, qs_ref, m_ref, acc_ref,
                       *, kind, n_steps, tq):
    s = pl.program_id(1)
    plan = _head_plan(kind)

    def update(first):
        if first:
            for r in range(N_HEADS):
                qs_ref[r * tq:(r + 1) * tq, :] = _head_query(q_ref, plan[r])
        scores = {}
        for kt in sorted({p[2] for p in plan}):
            heads = [r for r in range(N_HEADS) if plan[r][2] == kt]
            ksl = slice(kt * LANES, (kt + 1) * LANES)
            k_blocks = [kc_ref[:, ksl], kl_ref[:, ksl]] if first else [kl_ref[:, ksl]]
            q_all = qs_ref[heads[0] * tq:(heads[-1] + 1) * tq, :]
            chunks = _score_chunks(q_all, k_blocks, [None] * len(k_blocks))
            for j, r in enumerate(heads):
                scores[r] = [c[j * tq:(j + 1) * tq, :] for c in chunks]
        for r, (_, _, kt, vt) in enumerate(plan):
            rows = slice(r * tq, (r + 1) * tq)
            vsl = slice(vt * LANES, (vt + 1) * LANES)
            v_blocks = [vc_ref[:, vsl], vl_ref[:, vsl]] if first else [vl_ref[:, vsl]]
            chunks = scores[r]
            m_cur = _row_max(chunks)
            if first:
                m_new = m_cur
            else:
                m_prev = m_ref[rows, :]
                m_new = jnp.maximum(m_prev, m_cur)
                alpha = jnp.exp2(m_prev - m_new)
            p = [jnp.exp2(c - m_new) for c in chunks]
            pv = _weighted_values(p, v_blocks, r % 2)
            if first:
                acc_ref[rows, :] = pv
            else:
                acc_ref[rows, :] = alpha * acc_ref[rows, :] + pv
            m_ref[rows, :] = m_new

    @pl.when(s == 0)
    def _():
        update(True)

    @pl.when(s > 0)
    def _():
        update(False)

    @pl.when(s == n_steps - 1)
    def _():
        outs = []
        for r in range(N_HEADS):
            rows = slice(r * tq, (r + 1) * tq)
            acc = acc_ref[rows, :]
            outs.append(acc / pltpu.roll(acc, HEAD_DIM, 1))
        _store_heads(o_ref, outs)


def _attn_latent_call(lay, kind, q, k, v, *, sink=None, bias=None):
    T, B, ncb, lb, S = lay["T"], lay["B"], lay["ncb"], lay["lb"], lay["S"]
    tq = ROW_BLK
    qw, kw, vw = q.shape[1], k.shape[1], v.shape[1]
    out_shape = jax.ShapeDtypeStruct((B * S, BRANCH_W), BF16)
    if kind in ("A", "D"):
        tk, tq = lay["tk_dense"], lay["tq_dense"]
        n_steps = S // tk
        lat0 = ncb * ROW_BLK // tk
        q0 = ncb * ROW_BLK // tq
        qpb = S // tq
        lat_blk = lambda i, s: (lat0 + (i // qpb) * n_steps + s, 0)
        return pl.pallas_call(
            functools.partial(_attn_dense_kernel, kind=kind, n_steps=n_steps, tq=tq),
            out_shape=out_shape,
            grid=(B * qpb, n_steps),
            in_specs=[pl.BlockSpec((tq, qw), lambda i, s: (q0 + i, 0)),
                      pl.BlockSpec((ROW_BLK, kw), lambda i, s: (i // qpb, 0)),
                      pl.BlockSpec((ROW_BLK, vw), lambda i, s: (i // qpb, 0)),
                      pl.BlockSpec((tk, kw), lat_blk),
                      pl.BlockSpec((tk, vw), lat_blk)],
            out_specs=pl.BlockSpec((tq, BRANCH_W), lambda i, s: (i, 0)),
            scratch_shapes=[pltpu.VMEM((N_HEADS * tq, LANES), BF16),
                            pltpu.VMEM((N_HEADS * tq, LANES), F32),
                            pltpu.VMEM((N_HEADS * tq, LANES), F32)],
            compiler_params=_cparams(("parallel", "arbitrary")),
            name="attn_" + kind,
        )(q, k, v, k, v)

    tq = lay["tq_band"]
    qpb = S // tq
    q0 = ncb * ROW_BLK // tq
    bpq = tq // ROW_BLK

    def nb(i, d):
        return (ncb + (i // qpb) * lb + jnp.clip((i % qpb) * bpq + d, 0, lb - 1), 0)

    kv_maps = [lambda i: (i // qpb, 0)] + [functools.partial(nb, d=d) for d in _band_offsets(tq)]
    n_kv = len(kv_maps)
    in_specs = [pl.BlockSpec((tq, qw), lambda i: (q0 + i, 0))]
    in_specs += [pl.BlockSpec((ROW_BLK, kw), m) for m in kv_maps]
    in_specs += [pl.BlockSpec((ROW_BLK, vw), m) for m in kv_maps]
    in_specs.append(pl.BlockSpec(bias.shape, lambda i: (0,) * bias.ndim, pipeline_mode=pl.Buffered(1)))
    args = [q] + [k] * n_kv + [v] * n_kv + [bias]
    if kind == "B":
        in_specs.append(pl.BlockSpec(memory_space=pltpu.SMEM))
        args.append(sink)
    return pl.pallas_call(
        functools.partial(_attn_band_kernel, kind=kind, lb=qpb, n_kv=n_kv, tq=tq),
        out_shape=out_shape,
        grid=(B * qpb,),
        in_specs=in_specs,
        out_specs=pl.BlockSpec((tq, BRANCH_W), lambda i: (i, 0)),
        compiler_params=_cparams(("parallel",)),
        name="attn_" + kind,
    )(*args)


def _attn_context_call(lay, kind, q, k, v, *, sink=None):
    ncb = lay["ncb"]
    qw, kw, vw = q.shape[1], k.shape[1], v.shape[1]
    blk = lambda i: (i, 0)
    in_specs = [pl.BlockSpec((ROW_BLK, qw), blk), pl.BlockSpec((ROW_BLK, kw), blk),
                pl.BlockSpec((ROW_BLK, vw), blk)]
    args = [q, k, v]
    if kind == "B":
        in_specs.append(pl.BlockSpec(memory_space=pltpu.SMEM))
        args.append(sink)
    return pl.pallas_call(
        functools.partial(_attn_band_kernel, kind=kind, lb=1, n_kv=1, tq=ROW_BLK),
        out_shape=jax.ShapeDtypeStruct((ncb * ROW_BLK, BRANCH_W), BF16),
        grid=(ncb,),
        in_specs=in_specs,
        out_specs=pl.BlockSpec((ROW_BLK, BRANCH_W), blk),
        compiler_params=_cparams(("parallel",)),
        name="attn_ctx_" + kind,
    )(*args)


def _band_offsets(tq):
    return tuple(range(-1, tq // ROW_BLK + 1))


def _window_bias(S, tq):
    lb, qpb, bpq = S // ROW_BLK, S // tq, tq // ROW_BLK
    offs = _band_offsets(tq)
    qa, ka = np.arange(tq), np.arange(ROW_BLK)
    out = np.full((3, len(offs), tq, ROW_BLK), NEG_INF, np.float32)
    for vi, t_rep in enumerate((0, 1, qpb - 1)):
        for di, d in enumerate(offs):
            kt = t_rep * bpq + d
            if not 0 <= kt < lb:
                continue
            qpos = t_rep * tq + qa
            kpos = kt * ROW_BLK + ka
            ok = np.abs(qpos[:, None] - kpos[None, :]) <= WINDOW
            out[vi, di] = np.where(ok, 0.0, NEG_INF)
    return jnp.asarray(out)


def _neighbourhood_bias(rpb, rows_total, tq):
    lb = rows_total * GRID_W // ROW_BLK
    qpb, bpq = rows_total * GRID_W // tq, tq // ROW_BLK
    rpt = ROW_BLK // GRID_W
    rpq = tq // GRID_W
    kh = min(NA_KH, rows_total)
    qa, ka = np.arange(tq), np.arange(ROW_BLK)
    q_sub, q_col = qa // GRID_W, qa % GRID_W
    k_sub, k_col = ka // GRID_W, ka % GRID_W
    n_dr, n_dc = 2 * NA_KH - 1, 2 * NA_KW - 1
    col = np.arange(GRID_W)
    dc = np.clip(col[None, :] - col[:, None], -(NA_KW - 1), NA_KW - 1) + NA_KW - 1
    hot_c = (dc[:, :, None] == np.arange(n_dc)).astype(np.float32)
    by_col = jnp.einsum("huv,cdv->hucd", rpb.astype(F32) * LOG2E, jnp.asarray(hot_c),
                        precision=HIGHEST)
    offs = _band_offsets(tq)
    vals = []
    for d in offs:
        dr = np.clip(d * rpt + np.arange(rpt)[None, :] - np.arange(rpq)[:, None],
                     -(NA_KH - 1), NA_KH - 1) + NA_KH - 1
        hot_r = (dr[:, :, None] == np.arange(n_dr)).astype(np.float32)
        v = jnp.einsum("abu,hucd->hacbd", jnp.asarray(hot_r), by_col, precision=HIGHEST)
        vals.append(v.reshape(N_HEADS, tq, ROW_BLK))
    out = []
    for t_rep in (0, 1, qpb - 1):
        per_block = []
        for di, d in enumerate(offs):
            kt = t_rep * bpq + d
            q_row = t_rep * rpq + q_sub
            k_row = kt * rpt + k_sub
            r_start = np.clip(q_row - kh // 2, 0, rows_total - kh)
            row_ok = (k_row[None] >= r_start[:, None]) & (k_row[None] < r_start[:, None] + kh)
            c_start = np.clip(q_col - NA_KW // 2, 0, GRID_W - NA_KW)
            col_ok = (k_col[None] >= c_start[:, None]) & (k_col[None] < c_start[:, None] + NA_KW)
            ok = row_ok & col_ok & (0 <= kt < lb)
            per_block.append(jnp.where(jnp.asarray(ok)[None], vals[di], NEG_INF).reshape(-1, ROW_BLK))
        out.append(jnp.stack(per_block))
    return jnp.stack(out)


def _merge_kernel(*refs, tm, split_x, n_ctx_tiles):
    it = iter(refs)
    h_ref = next(it)
    is_ctx = pl.program_id(0) < n_ctx_tiles
    if n_ctx_tiles > 0:
        o_refs = [(next(it), next(it)) for _ in range(4)]
        o_vals = [_tile_rows(oc, ol, is_ctx) for oc, ol in o_refs]
    else:
        o_vals = [next(it)[...] for _ in range(4)]
    if split_x:
        xc_ref, xl_ref = next(it), next(it)
    else:
        x_ref = next(it)
    (mod_ref, gffn_ref, wg_ref, bg_ref, wb_ref, wout_ref, wr_ref, br_ref,
     x1_ref, h2_ref, route_ref, y_ref) = (next(it) for _ in range(12))
    hb = h_ref[...]
    d_model = hb.shape[1]
    for t in range(d_model // MERGE_TN):
        cs = slice(t * MERGE_TN, (t + 1) * MERGE_TN)
        y = None
        for n, o_n in enumerate(o_vals):
            gate = _sigmoid(jnp.dot(hb, wg_ref[n, :, cs], preferred_element_type=F32) + bg_ref[n, :, cs])
            u = gate * jnp.dot(o_n, wb_ref[n, :, cs], preferred_element_type=F32)
            y = u if y is None else y + u
        y_ref[:, cs] = y.astype(BF16)
    z = jnp.dot(y_ref[...], wout_ref[...], preferred_element_type=F32)
    x_in = _tile_rows(xc_ref, xl_ref, is_ctx) if split_x else x_ref[...]
    x1 = x_in + mod_ref[0, 2:3, :] * z
    x1_ref[...] = x1
    h2 = _rms(x1) * gffn_ref[...] * (1.0 + mod_ref[0, 4:5, :]) + mod_ref[0, 3:4, :]
    for c in range(SUBLANES):
        h2_ref[pl.ds(c, tm, stride=SUBLANES), :] = h2[:, c * LANES:(c + 1) * LANES]

    h2_hi = h2.astype(BF16)
    h2_lo = (h2 - h2_hi.astype(F32)).astype(BF16)
    logit = (jnp.dot(h2_hi, wr_ref[0], preferred_element_type=F32)
             + jnp.dot(h2_lo, wr_ref[0], preferred_element_type=F32)
             + jnp.dot(h2_hi, wr_ref[1], preferred_element_type=F32)) + br_ref[...]
    lane = _lane_iota(logit.shape)
    big = jnp.int32(1 << 20)
    is_g = (lane >= N_EXPERTS) & (lane < N_EXPERTS + N_GROUPS)
    gl = jnp.where(is_g, logit, NEG_INF)
    gmax = jnp.max(gl, axis=-1, keepdims=True)
    gsel = jnp.min(jnp.where(gl == gmax, lane - N_EXPERTS, big), axis=-1, keepdims=True)
    gw = 1.0 / jnp.sum(jnp.where(is_g, jnp.exp(gl - gmax), 0.0), axis=-1, keepdims=True)
    in_grp = (lane < N_EXPERTS) & ((lane // EXPERTS_PER_GROUP) == gsel)
    el = jnp.where(in_grp, logit, NEG_INF)
    v1 = jnp.max(el, axis=-1, keepdims=True)
    i1 = jnp.min(jnp.where(el == v1, lane, big), axis=-1, keepdims=True)
    el2 = jnp.where(lane == i1, NEG_INF, el)
    v2 = jnp.max(el2, axis=-1, keepdims=True)
    i2 = jnp.min(jnp.where(el2 == v2, lane, big), axis=-1, keepdims=True)
    e21 = jnp.exp(v2 - v1)
    w1 = gw / (1.0 + e21)
    w2 = gw * e21 / (1.0 + e21)
    route_ref[...] = jnp.where(lane == 0, i1.astype(F32),
                               jnp.where(lane == 1, i2.astype(F32),
                                         jnp.where(lane == 2, w1, jnp.where(lane == 3, w2, 0.0))))


def _merge_call(lay, layer, h, o, x, mod, w, *, with_ctx):
    T, D, tm = lay["T"], lay["D"], lay["tm"]
    nct, tpb, B = lay["nct"], lay["tpb"], lay["B"]
    off = 0 if with_ctx else nct
    n_tiles = T // tm - off

    def mod_row(i):
        return jnp.where(i < nct, B, (i - nct) // tpb)

    row = lambda i: (i + off, 0)
    const2 = lambda i: (0, 0)
    const3 = lambda i: (0, 0, 0)
    split_x = isinstance(x, (tuple, list))
    in_specs = [pl.BlockSpec((tm, D), row)]
    o_args = []
    for kind in ("A", "B", "C", "D"):
        o_ctx, o_lat = o[kind]
        if with_ctx:
            in_specs += _split_row_specs(lay, BRANCH_W)
            o_args += [o_ctx, o_lat]
        else:
            in_specs.append(pl.BlockSpec((tm, BRANCH_W), lambda i: (i, 0)))
            o_args.append(o_lat)
    in_specs += _split_row_specs(lay, D, off) if split_x else [pl.BlockSpec((tm, D), row)]
    in_specs += [pl.BlockSpec((1, 6, D), lambda i: (layer * lay["mod_rows"] + mod_row(i + off), 0, 0)),
                 pl.BlockSpec((1, D), const2),
                 pl.BlockSpec(w["w_gate"].shape, const3),
                 pl.BlockSpec(w["b_gate"].shape, const3),
                 pl.BlockSpec(w["w_branch"].shape, const3),
                 pl.BlockSpec((D, D), const2),
                 pl.BlockSpec((2, D, LANES), const3),
                 pl.BlockSpec((1, LANES), const2)]
    out_shape = [jax.ShapeDtypeStruct((T, D), F32),
                 jax.ShapeDtypeStruct((T * SUBLANES, LANES), F32),
                 jax.ShapeDtypeStruct((T, LANES), F32)]
    out_specs = [pl.BlockSpec((tm, D), row),
                 pl.BlockSpec((tm * SUBLANES, LANES), row),
                 pl.BlockSpec((tm, LANES), row)]
    return pl.pallas_call(
        functools.partial(_merge_kernel, tm=tm, split_x=split_x, n_ctx_tiles=nct - off),
        out_shape=out_shape, grid=(n_tiles,), in_specs=in_specs, out_specs=out_specs,
        scratch_shapes=[pltpu.VMEM((tm, D), BF16)],
        compiler_params=_cparams(("parallel",)),
        name="merge",
    )(h, *o_args, *(x if split_x else (x,)), mod, w["g_ffn"], w["w_gate"],
      w["b_gate"], w["w_branch"], w["w_out"], w["w_route"], w["b_route"])


def _moe_kernel(tab_ref, tok_ref, h2_ref, sw_ref, w1_ref, w3_ref, w2_ref, f_ref,
                xg_ref, y_ref, st_ref, *, blk, slots, tile_off):
    ti = pl.program_id(0)
    step = pl.program_id(1)

    @pl.when(step == 0)
    def _():
        f_ref[...] = jnp.zeros_like(f_ref)
        xg_ref[...] = jnp.zeros_like(xg_ref)

    tok_base = (ti + tile_off) * slots
    segs = []
    for j in range(MOE_EXPERTS_PER_STEP):
        base = ((ti + tile_off) * N_EXPERTS + step * MOE_EXPERTS_PER_STEP + j) * 2
        segs.append((tab_ref[base], tab_ref[base + 1]))

    def gather(j, off, n_rows):
        slot0 = tok_base + off

        def body(gi, c):
            r0 = gi * MOE_GROUP
            srcs = [pl.multiple_of(tok_ref[slot0 + r0 + u], SUBLANES) for u in range(MOE_GROUP)]
            for u in range(MOE_GROUP):
                dst = pl.multiple_of((r0 + u) * SUBLANES, SUBLANES)
                xg_ref[j, pl.ds(dst, SUBLANES), :] = h2_ref[pl.ds(srcs[u], SUBLANES), :]
            return c

        lax.fori_loop(0, n_rows // MOE_GROUP, body, 0)

    def ffn(j, off):
        xb = _load_token_major(xg_ref.at[j], blk).astype(BF16)
        a = jnp.dot(xb, w1_ref[j], preferred_element_type=F32)
        g = jnp.dot(xb, w3_ref[j], preferred_element_type=F32)
        hid = (a * _sigmoid(a) * g).astype(BF16)
        y = jnp.dot(hid, w2_ref[j], preferred_element_type=F32)
        y_ref[j] = y * sw_ref[0, pl.ds(off, blk), :]

    def scatter(j, off, n_rows):
        slot0 = tok_base + off

        def body(gi, c):
            r0 = pl.multiple_of(gi * MOE_GROUP, MOE_GROUP)
            dsts = [pl.multiple_of(tok_ref[slot0 + r0 + u], SUBLANES) for u in range(MOE_GROUP)]
            for ch in range(SUBLANES):
                st_ref[pl.ds(ch, MOE_GROUP, stride=SUBLANES), :] = y_ref[j, pl.ds(r0, MOE_GROUP),
                                                                         ch * LANES:(ch + 1) * LANES]
            vals = [f_ref[pl.ds(dsts[u], SUBLANES), :] + st_ref[u * SUBLANES:(u + 1) * SUBLANES, :]
                    for u in range(MOE_GROUP)]
            for u in reversed(range(MOE_GROUP)):
                f_ref[pl.ds(dsts[u], SUBLANES), :] = vals[u]
            return c

        lax.fori_loop(0, n_rows // MOE_GROUP, body, 0)

    for j, (seg0, n_pad) in enumerate(segs):
        gather(j, pl.multiple_of(seg0, MOE_GROUP), jnp.minimum(n_pad, blk))
    for j, (seg0, _) in enumerate(segs):
        ffn(j, pl.multiple_of(seg0, MOE_GROUP))
    for j, (seg0, n_pad) in enumerate(segs):
        scatter(j, pl.multiple_of(seg0, MOE_GROUP), jnp.minimum(n_pad, blk))

    for j, (seg0, n_pad) in enumerate(segs):
        def extra_block(b, carry, j=j, seg0=seg0, n_pad=n_pad):
            off = pl.multiple_of(seg0 + b * blk, MOE_GROUP)
            n_rows = jnp.minimum(n_pad - b * blk, blk)
            gather(j, off, n_rows)
            ffn(j, off)
            scatter(j, off, n_rows)
            return carry

        lax.fori_loop(1, (n_pad + blk - 1) // blk, extra_block, 0)


def _moe_call(lay, route, h2, w, *, with_ctx):
    T, D, tt = lay["T"], lay["D"], lay["tt"]
    blk = lay["moe_blk"]
    n_tiles_all = T // tt
    tile_off = 0 if with_ctx else lay["ncb"] * ROW_BLK // tt
    n_tiles = n_tiles_all - tile_off
    n_assign = 2 * tt
    n_fill = N_EXPERTS * MOE_GROUP
    slots = n_assign + n_fill + blk

    eid = route[:, 0:2].astype(jnp.int32).reshape(n_tiles_all, n_assign)
    wts = route[:, 2:4].reshape(n_tiles_all, n_assign)
    tok = jnp.broadcast_to(jnp.arange(n_assign, dtype=jnp.int32)[None, :] // 2 * SUBLANES, eid.shape)
    ex = jnp.arange(N_EXPERTS, dtype=jnp.int32)
    counts = jnp.sum(eid[:, :, None] == ex[None, None, :], axis=1, dtype=jnp.int32)
    n_dummy = (-counts) % MOE_GROUP
    fill_key = jnp.where(jnp.arange(MOE_GROUP, dtype=jnp.int32)[None, None, :] < n_dummy[:, :, None],
                         ex[None, :, None], N_EXPERTS).reshape(n_tiles_all, n_fill)
    zeros_i = jnp.zeros((n_tiles_all, n_fill), jnp.int32)
    _, slot_tok, slot_w = lax.sort(
        (jnp.concatenate([eid, fill_key], axis=1), jnp.concatenate([tok, zeros_i], axis=1),
         jnp.concatenate([wts, zeros_i.astype(F32)], axis=1)),
        dimension=1, is_stable=True, num_keys=1)
    slot_tok = jnp.pad(slot_tok, ((0, 0), (0, blk)))
    slot_w = jnp.pad(slot_w, ((0, 0), (0, blk)))
    padded = counts + n_dummy
    seg0 = jnp.cumsum(padded, axis=1) - padded
    tab = jnp.stack([seg0, padded], axis=-1).reshape(-1).astype(jnp.int32)

    return pl.pallas_call(
        functools.partial(_moe_kernel, blk=blk, slots=slots, tile_off=tile_off),
        out_shape=jax.ShapeDtypeStruct((T * SUBLANES, LANES), F32),
        grid_spec=pltpu.PrefetchScalarGridSpec(
            num_scalar_prefetch=2,
            grid=(n_tiles, N_EXPERTS // MOE_EXPERTS_PER_STEP),
            in_specs=[
                pl.BlockSpec((tt * SUBLANES, LANES), lambda t, e, *_: (t + tile_off, 0)),
                pl.BlockSpec((1, slots, 1), lambda t, e, *_: (t + tile_off, 0, 0)),
                pl.BlockSpec((MOE_EXPERTS_PER_STEP, D, EXPERT_FF), lambda t, e, *_: (e, 0, 0)),
                pl.BlockSpec((MOE_EXPERTS_PER_STEP, D, EXPERT_FF), lambda t, e, *_: (e, 0, 0)),
                pl.BlockSpec((MOE_EXPERTS_PER_STEP, EXPERT_FF, D), lambda t, e, *_: (e, 0, 0)),
            ],
            out_specs=pl.BlockSpec((tt * SUBLANES, LANES), lambda t, e, *_: (t + tile_off, 0)),
            scratch_shapes=[pltpu.VMEM((MOE_EXPERTS_PER_STEP, blk * SUBLANES, LANES), F32),
                            pltpu.VMEM((MOE_EXPERTS_PER_STEP, blk, D), F32),
                            pltpu.VMEM((MOE_GROUP * SUBLANES, LANES), F32)],
        ),
        compiler_params=_cparams(("parallel", "arbitrary")),
        name="moe_experts",
    )(tab, slot_tok.reshape(-1), h2, slot_w.reshape(n_tiles_all, slots, 1),
      w["w_ff1"], w["w_ff3"], w["w_ff2"])


def _final_kernel(x_ref, f_ref, mod_ref, g_ref, o_ref, *, tm):
    xf = x_ref[...] + mod_ref[0, 5:6, :] * _load_token_major(f_ref, tm)
    o_ref[...] = _rms(xf) * g_ref[...]


def _final_call(lay, layer, x1, f, mod, g_final):
    T, D, tm = lay["T"], lay["D"], lay["tm"]
    nct, tpb = lay["nct"], lay["tpb"]
    n_lat = T // tm - nct
    return pl.pallas_call(
        functools.partial(_final_kernel, tm=tm),
        out_shape=jax.ShapeDtypeStruct((n_lat * tm, D), F32),
        grid=(n_lat,),
        in_specs=[pl.BlockSpec((tm, D), lambda i: (i + nct, 0)),
                  pl.BlockSpec((tm * SUBLANES, LANES), lambda i: (i + nct, 0)),
                  pl.BlockSpec((1, 6, D), lambda i: (layer * lay["mod_rows"] + i // tpb, 0, 0)),
                  pl.BlockSpec((1, D), lambda i: (0, 0))],
        out_specs=pl.BlockSpec((tm, D), lambda i: (i, 0)),
        compiler_params=_cparams(("parallel",)),
        name="final_norm",
    )(x1, f, mod, g_final)


def _select_cols(wm, segs, scale=None):
    parts = []
    for k, (start, width) in enumerate(segs):
        if start is None:
            parts.append(jnp.zeros((wm.shape[0], width), wm.dtype))
        else:
            blk = wm[:, start:start + width]
            parts.append(blk if scale is None or scale[k] is None else blk * scale[k])
    return jnp.concatenate(parts, axis=1)


def _prep_layer(l, p):
    a_cols = A_Q_RANK + A_KV_RANK + A_ROPE
    b_off = a_cols
    c_off = b_off + 512
    d_off = c_off + 768
    qk_scale = HEAD_DIM ** -0.5 * LOG2E
    gqa_q = lambda off: [(off + hh * HEAD_DIM, HEAD_DIM) for hh in _GQA_PERM]
    segs = ([(0, 256), (256, 128), (None, 64), (384, 32), (None, 32)]
            + gqa_q(b_off) + [(b_off + 256, 128), (b_off + 384, 128)]
            + [(c_off, 256), (c_off + 256, 256), (c_off + 512, 256)]
            + gqa_q(d_off) + [(d_off + 256, 128), (d_off + 384, 128)])
    scale = [None] * len(segs)
    for k in (5, 6, 7, 8, 11):
        scale[k] = qk_scale
    w_in = _select_cols(p["w_in"][l], segs, scale).astype(BF16)
    assert w_in.shape[1] == PROJ_COLS

    hq = A_NOPE + A_ROPE
    segs_q = []
    for hh in range(N_HEADS):
        segs_q += [(hh * hq, hq), (None, LANES - hq)]
    w_q_b = _select_cols(p["w_q_b"][l], segs_q).astype(BF16)
    hk = A_NOPE + HEAD_DIM
    segs_k = []
    for hh in range(N_HEADS):
        segs_k += [(hh * hk, A_NOPE), (None, LANES - A_NOPE)]
    segs_k += [(hh * hk + A_NOPE, HEAD_DIM) for hh in range(N_HEADS)]
    w_kv_b = _select_cols(p["w_kv_b"][l], segs_k).astype(BF16)

    wb = p["w_branch"][l]
    perm_rows = lambda m: jnp.concatenate([m[hh * HEAD_DIM:(hh + 1) * HEAD_DIM] for hh in _GQA_PERM], axis=0)
    w_branch = jnp.stack([wb[0], perm_rows(wb[1]), wb[2], perm_rows(wb[3])]).astype(BF16)

    d = p["w_in"].shape[1]
    w_route = jnp.zeros((d, LANES), F32)
    w_route = w_route.at[:, :N_EXPERTS].set(p["w_router"][l]).at[:, N_EXPERTS:N_EXPERTS + N_GROUPS].set(p["w_group"][l])
    b_route = jnp.zeros((1, LANES), F32)
    b_route = b_route.at[0, :N_EXPERTS].set(p["b_router"][l]).at[0, N_EXPERTS:N_EXPERTS + N_GROUPS].set(p["b_group"][l])
    return {
        "g_mix": p["g_norm_mix"][l][None, :],
        "w_in": w_in,
        "g_q_a": p["g_q_a"][l][None, :],
        "w_q_b": w_q_b,
        "g_kv_a": p["g_kv_a"][l][None, :],
        "w_kv_b": w_kv_b,
        "g_q_d": (jnp.tile(p["g_q_d"][l], 2) * qk_scale)[None, :],
        "g_k_d": jnp.tile(p["g_k_d"][l], 2)[None, :],
        "sink": p["sink_b"][l],
        "rpb": p["rpb_c"][l],
        "w_gate": p["w_gate"][l].astype(BF16),
        "b_gate": p["b_gate"][l][:, None, :],
        "w_branch": w_branch,
        "w_out": p["w_out"][l].astype(BF16),
        "g_ffn": p["g_norm_ffn"][l][None, :],
        "w_route": jnp.stack([w_route.astype(BF16),
                              (w_route - w_route.astype(BF16).astype(F32)).astype(BF16)]),
        "b_route": b_route,
        "w_ff1": p["w_ff1"][l].astype(BF16),
        "w_ff3": p["w_ff3"][l].astype(BF16),
        "w_ff2": p["w_ff2"][l].astype(BF16),
    }


def _rope_tables(S, tm):
    t = np.arange(S)
    rows = (t // GRID_W).astype(np.float32)
    cols = (t % GRID_W).astype(np.float32)

    def cs(rot):
        half = rot // 2
        inv = np.float32(ROPE_THETA) ** (-np.arange(0, half, 2, dtype=np.float32) / np.float32(half))
        ar_, ac_ = rows[:, None] * inv, cols[:, None] * inv
        cos = np.concatenate([np.cos(ar_), np.cos(ar_), np.cos(ac_), np.cos(ac_)], axis=-1)
        sin = np.concatenate([-np.sin(ar_), np.sin(ar_), -np.sin(ac_), np.sin(ac_)], axis=-1)
        return cos.astype(np.float32), sin.astype(np.float32)

    cos64, sin64 = cs(HEAD_DIM)
    cos32, sin32 = cs(A_ROPE)
    ones = lambda n: np.ones((S, n), np.float32)
    zeros = lambda n: np.zeros((S, n), np.float32)
    tabs = {
        "cos_h": np.concatenate([cos64, cos64], axis=-1),
        "sin_h": np.concatenate([sin64, sin64], axis=-1),
        "cos_a": np.concatenate([ones(A_NOPE), cos32, ones(LANES - A_NOPE - A_ROPE)], axis=-1),
        "sin_a": np.concatenate([zeros(A_NOPE), sin32, zeros(LANES - A_NOPE - A_ROPE)], axis=-1),
    }
    ident = {"cos_h": 1.0, "sin_h": 0.0, "cos_a": 1.0, "sin_a": 0.0}
    return {k: jnp.asarray(np.concatenate([v, np.full((tm, LANES), ident[k], np.float32)], axis=0))
            for k, v in tabs.items()}


def _layout(B, S, n_ctx, D):
    assert n_ctx == ROW_BLK and S % 1024 == 0 and S // GRID_W >= 3 * (ROW_BLK // GRID_W)
    T = B * (n_ctx + S)
    tm = 512 if (B * n_ctx) % 512 == 0 else 256
    tk_dense = 1024 if (B * n_ctx) % 1024 == 0 else (512 if (B * n_ctx) % 512 == 0 else 256)
    tt = 2048 if (B * n_ctx) % 2048 == 0 else B * n_ctx
    assert S % tt == 0
    return {
        "B": B, "S": S, "D": D, "T": T, "tm": tm,
        "ncb": B * n_ctx // ROW_BLK,
        "lb": S // ROW_BLK,
        "nct": B * n_ctx // tm,
        "tpb": S // tm,
        "tk_dense": tk_dense,
        "tq_dense": tk_dense,
        "tq_band": tm,
        "tt": tt, "moe_blk": 160,
        "mod_rows": 16,
    }


def kernel(x, c, ctx, c_ctx, w_mod, b_mod, g_norm_mix, w_in, g_q_a, w_q_b, g_kv_a, w_kv_b, sink_b, rpb_c,
           g_q_d, g_k_d, w_gate, b_gate, w_branch, w_out, g_norm_ffn, w_group, b_group, w_router, b_router,
           w_ff1, w_ff3, w_ff2, g_final):
    B, S, D = x.shape
    n_ctx = ctx.shape[1]
    depth = w_mod.shape[0]
    lay = _layout(B, S, n_ctx, D)
    params = dict(w_in=w_in, g_norm_mix=g_norm_mix, g_q_a=g_q_a, w_q_b=w_q_b, g_kv_a=g_kv_a, w_kv_b=w_kv_b,
                  sink_b=sink_b, rpb_c=rpb_c, g_q_d=g_q_d, g_k_d=g_k_d, w_gate=w_gate, b_gate=b_gate,
                  w_branch=w_branch, w_out=w_out, g_norm_ffn=g_norm_ffn, w_group=w_group, b_group=b_group,
                  w_router=w_router, b_router=b_router, w_ff1=w_ff1, w_ff3=w_ff3, w_ff2=w_ff2)

    c_all = jnp.zeros((lay["mod_rows"], D), F32).at[:B].set(c).at[B].set(c_ctx)
    mod = _modulation(c_all, w_mod, b_mod).reshape(depth * lay["mod_rows"], 6, D)
    tabs = _rope_tables(S, lay["tm"])
    win_bias = _window_bias(S, lay["tq_band"])
    xf = (ctx.reshape(B * n_ctx, D), x.reshape(B * S, D))

    f = None
    for l in range(depth):
        with_ctx = l < depth - 1
        w = _prep_layer(l, params)
        xf, pr = _proj_call(lay, xf, f, mod, mod, l, w, tabs)
        o = {}
        for kind, kl in (("A", "a"), ("B", "b"), ("C", "c"), ("D", "d")):
            q, k, v = pr["q" + kl], pr["k" + kl], pr["v" + kl]
            sink = w["sink"] if kind == "B" else None
            bias = win_bias if kind == "B" else None
            if kind == "C":
                bias = _neighbourhood_bias(w["rpb"], S // GRID_W, lay["tq_band"])
            o_ctx = _attn_context_call(lay, kind, q, k, v, sink=sink) if with_ctx else None
            o[kind] = (o_ctx, _attn_latent_call(lay, kind, q, k, v, sink=sink, bias=bias))
        xf, h2, route = _merge_call(lay, l, pr["h"], o, xf, mod, w, with_ctx=with_ctx)
        f = _moe_call(lay, route, h2, w, with_ctx=with_ctx)
    out = _final_call(lay, depth - 1, xf, f, mod, g_final[None, :])
    return out.reshape(B, S, D)
```

```python
import functools

import numpy as np
import jax
import jax.numpy as jnp
from jax import lax
from jax.experimental import pallas as pl
from jax.experimental.pallas import tpu as pltpu

F32 = jnp.float32
BF16 = jnp.bfloat16
HIGHEST = lax.Precision.HIGHEST

GRID_W = 64
ROPE_THETA = 10000.0
EPS = 1e-6
NEG_INF = -1e30
LOG2E = 1.4426950408889634
HEAD_DIM = 64
N_HEADS = 4
BRANCH_W = 256
A_Q_RANK = 256
A_KV_RANK = 128
A_NOPE = 64
A_ROPE = 32
NA_KH = 8
NA_KW = 16
WINDOW = 128
N_GROUPS = 4
EXPERTS_PER_GROUP = 8
N_EXPERTS = 32
EXPERT_FF = 256

LANES = 128
SUBLANES = 8
ROW_BLK = 256
MERGE_TN = 256
MOE_EXPERTS_PER_STEP = 4
MOE_GROUP = 16
VMEM_LIMIT = 56 * 1024 * 1024

_PROJ_GROUPS = (("cq", 256), ("ckv", 128), ("kr", 128), ("qb", 256), ("kb", 128), ("vb", 128),
                ("qc", 256), ("kc", 256), ("vc", 256), ("qd", 256), ("kd", 128), ("vd", 128))
_PROJ_OFF = {}
_o = 0
for _n, _w in _PROJ_GROUPS:
    _PROJ_OFF[_n] = (_o, _w)
    _o += _w
PROJ_COLS = _o
_GQA_PERM = (0, 2, 1, 3)


def _cparams(sem):
    return pltpu.CompilerParams(dimension_semantics=sem, vmem_limit_bytes=VMEM_LIMIT)


def _lane_iota(shape):
    return lax.broadcasted_iota(jnp.int32, shape, len(shape) - 1)


def _sigmoid(x):
    return 1.0 / (1.0 + jnp.exp(-x))


def _mod_kernel(c_ref, w_ref, b_ref, o_ref):
    cf = c_ref[...]
    s = cf * _sigmoid(cf)
    o_ref[0] = jnp.dot(s, w_ref[0], precision=HIGHEST, preferred_element_type=F32) + b_ref[0]


def _modulation(c_all, w_mod, b_mod):
    n_layers, d, n_out = w_mod.shape
    rows = c_all.shape[0]
    tn = 1536
    return pl.pallas_call(
        _mod_kernel,
        out_shape=jax.ShapeDtypeStruct((n_layers, rows, n_out), F32),
        grid=(n_layers, n_out // tn),
        in_specs=[pl.BlockSpec((rows, d), lambda l, j: (0, 0)),
                  pl.BlockSpec((1, d, tn), lambda l, j: (l, 0, j)),
                  pl.BlockSpec((1, 1, tn), lambda l, j: (l, 0, j))],
        out_specs=pl.BlockSpec((1, rows, tn), lambda l, j: (l, 0, j)),
        compiler_params=_cparams(("arbitrary", "arbitrary")),
        name="modulation",
    )(c_all, w_mod, b_mod.reshape(n_layers, 1, n_out))


def _rms(x):
    return x * lax.rsqrt(jnp.mean(x * x, axis=-1, keepdims=True) + EPS)


def _swap_blocks(x, blk):
    lane = _lane_iota(x.shape)
    up = pltpu.roll(x, LANES - blk, 1)
    dn = pltpu.roll(x, blk, 1)
    return jnp.where((lane // blk) % 2 == 0, up, dn)


def _rope(x, cos, sin, blk):
    return x * cos + _swap_blocks(x, blk) * sin


def _pair_norm(x, g):
    lo = _lane_iota(x.shape) < HEAD_DIM
    sq = x * x
    s_lo = jnp.sum(jnp.where(lo, sq, 0.0), axis=-1, keepdims=True)
    s_hi = jnp.sum(jnp.where(lo, 0.0, sq), axis=-1, keepdims=True)
    ms = jnp.where(lo, s_lo, s_hi) * (1.0 / HEAD_DIM)
    return x * lax.rsqrt(ms + EPS) * g


def _load_token_major(ref, rows):
    return jnp.concatenate(
        [ref[pl.ds(c, rows, stride=SUBLANES), :] for c in range(SUBLANES)], axis=-1)


def _tile_rows(xc_ref, xl_ref, is_ctx):
    return jnp.where(is_ctx, xc_ref[...], xl_ref[...])


def _proj_kernel(*refs, with_f, tm, scale_a, nct):
    it = iter(refs)
    if with_f:
        x_ref = next(it)
        f_ref = next(it)
        modp_ref = next(it)
    else:
        xc_ref, xl_ref = next(it), next(it)
    mod_ref = next(it)
    gmix_ref, win_ref, gqa_ref, wqb_ref, gkva_ref, wkvb_ref, gqd_ref, gkd_ref = (next(it) for _ in range(8))
    cosh_ref, sinh_ref, cosa_ref, sina_ref = (next(it) for _ in range(4))
    if with_f:
        x2_ref = next(it)
    h_ref = next(it)
    qa_ref, ka_ref, va_ref, qb_ref, kb_ref, vb_ref, qc_ref, kc_ref, vc_ref, qd_ref, kd_ref, vd_ref = (
        next(it) for _ in range(12))

    if with_f:
        xf = x_ref[...] + modp_ref[0, 5:6, :] * _load_token_major(f_ref, tm)
        x2_ref[...] = xf
    else:
        xf = _tile_rows(xc_ref, xl_ref, pl.program_id(0) < nct)
    h = _rms(xf) * gmix_ref[...] * (1.0 + mod_ref[0, 1:2, :]) + mod_ref[0, 0:1, :]
    hb = h.astype(BF16)
    h_ref[...] = hb
    p = jnp.dot(hb, win_ref[...], preferred_element_type=F32)

    def grp(name):
        o, w = _PROJ_OFF[name]
        return p[:, o:o + w]

    cosh, sinh = cosh_ref[...], sinh_ref[...]
    cosa, sina = cosa_ref[...], sina_ref[...]

    cq = (_rms(grp("cq")) * gqa_ref[...]).astype(BF16)
    qa = jnp.dot(cq, wqb_ref[...], preferred_element_type=F32)
    for hd in range(N_HEADS):
        sl = slice(hd * LANES, (hd + 1) * LANES)
        qa_ref[:, sl] = (_rope(qa[:, sl], cosa, sina, 8) * scale_a).astype(BF16)
    ckv = (_rms(grp("ckv")) * gkva_ref[...]).astype(BF16)
    kva = jnp.dot(ckv, wkvb_ref[...], preferred_element_type=F32)
    kr = _rope(grp("kr"), cosa, sina, 8)
    for hd in range(N_HEADS):
        sl = slice(hd * LANES, (hd + 1) * LANES)
        ka_ref[:, sl] = (kva[:, sl] + kr).astype(BF16)
    va_ref[...] = kva[:, N_HEADS * LANES:].astype(BF16)

    qb = grp("qb")
    for j in range(2):
        sl = slice(j * LANES, (j + 1) * LANES)
        qb_ref[:, sl] = _rope(qb[:, sl], cosh, sinh, 16).astype(BF16)
    kb_ref[...] = _rope(grp("kb"), cosh, sinh, 16).astype(BF16)
    vb_ref[...] = grp("vb").astype(BF16)

    qc_ref[...] = grp("qc").astype(BF16)
    kc_ref[...] = grp("kc").astype(BF16)
    vc_ref[...] = grp("vc").astype(BF16)

    qd = grp("qd")
    for j in range(2):
        sl = slice(j * LANES, (j + 1) * LANES)
        qd_ref[:, sl] = _rope(_pair_norm(qd[:, sl], gqd_ref[...]), cosh, sinh, 16).astype(BF16)
    kd_ref[...] = _rope(_pair_norm(grp("kd"), gkd_ref[...]), cosh, sinh, 16).astype(BF16)
    vd_ref[...] = grp("vd").astype(BF16)


def _split_row_specs(lay, width, off=0):
    tm, nct = lay["tm"], lay["nct"]
    return [pl.BlockSpec((tm, width), lambda i: (jnp.minimum(i + off, nct - 1), 0)),
            pl.BlockSpec((tm, width), lambda i: (jnp.maximum(i + off - nct, 0), 0))]


def _proj_call(lay, x, f, modp, mod, layer, w, tabs):
    T, D, tm = lay["T"], lay["D"], lay["tm"]
    nct, tpb, B = lay["nct"], lay["tpb"], lay["B"]
    with_f = f is not None
    n_tiles = T // tm

    def mod_row(i):
        return jnp.where(i < nct, B, (i - nct) // tpb)

    def tab_blk(i):
        return jnp.where(i < nct, tpb, (i - nct) % tpb)

    row = lambda i: (i, 0)
    const = lambda i: (0, 0)
    if with_f:
        in_specs = [pl.BlockSpec((tm, D), row),
                    pl.BlockSpec((tm * SUBLANES, LANES), row),
                    pl.BlockSpec((1, 6, D), lambda i: ((layer - 1) * lay["mod_rows"] + mod_row(i), 0, 0))]
        args = [x, f, modp]
    else:
        in_specs = _split_row_specs(lay, D)
        args = list(x)
    in_specs += [pl.BlockSpec((1, 6, D), lambda i: (layer * lay["mod_rows"] + mod_row(i), 0, 0))]
    args += [mod]
    for name in ("g_mix", "w_in", "g_q_a", "w_q_b", "g_kv_a", "w_kv_b", "g_q_d", "g_k_d"):
        a = w[name]
        in_specs.append(pl.BlockSpec(a.shape, const))
        args.append(a)
    for tname in ("cos_h", "sin_h", "cos_a", "sin_a"):
        in_specs.append(pl.BlockSpec((tm, LANES), lambda i: (tab_blk(i), 0)))
        args.append(tabs[tname])

    widths = [("h", D), ("qa", 512), ("ka", 512), ("va", 256), ("qb", 256), ("kb", 128), ("vb", 128),
              ("qc", 256), ("kc", 256), ("vc", 256), ("qd", 256), ("kd", 128), ("vd", 128)]
    out_shape, out_specs = [], []
    if with_f:
        out_shape.append(jax.ShapeDtypeStruct((T, D), F32))
        out_specs.append(pl.BlockSpec((tm, D), row))
    for _, wd in widths:
        out_shape.append(jax.ShapeDtypeStruct((T, wd), BF16))
        out_specs.append(pl.BlockSpec((tm, wd), row))

    outs = pl.pallas_call(
        functools.partial(_proj_kernel, with_f=with_f, tm=tm, nct=nct,
                          scale_a=float((A_NOPE + A_ROPE) ** -0.5 * LOG2E)),
        out_shape=out_shape, grid=(n_tiles,), in_specs=in_specs, out_specs=out_specs,
        compiler_params=_cparams(("parallel",)),
        name="proj_in",
    )(*args)
    outs = list(outs)
    x2 = outs.pop(0) if with_f else x
    names = [n for n, _ in widths]
    return x2, dict(zip(names, outs))


_NT = (((1,), (1,)), ((), ()))


def _head_plan(kind):
    if kind == "A":
        return tuple((r, None, r, r // 2) for r in range(N_HEADS))
    if kind == "C":
        return tuple((r // 2, r % 2, r // 2, r // 2) for r in range(N_HEADS))
    return tuple((r // 2, r % 2, 0, 0) for r in range(N_HEADS))


def _head_query(q_ref, plan_r):
    qt, half, _, _ = plan_r
    src = q_ref[:, qt * LANES:(qt + 1) * LANES]
    if half is None:
        return src
    lane = _lane_iota(src.shape)
    keep = (lane < HEAD_DIM) if half == 0 else (lane >= HEAD_DIM)
    return jnp.where(keep, src, jnp.zeros_like(src))


def _score_chunks(q, k_blocks, bias_blocks):
    chunks = []
    for kb, bb in zip(k_blocks, bias_blocks):
        s = lax.dot_general(q, kb, _NT, preferred_element_type=F32)
        if bb is not None:
            s = s + bb
        chunks += [s[:, c * LANES:(c + 1) * LANES] for c in range(s.shape[1] // LANES)]
    return chunks


def _row_max(chunks):
    m = functools.reduce(jnp.maximum, chunks)
    return jnp.broadcast_to(jnp.max(m, axis=-1, keepdims=True), m.shape)


def _weighted_values(p_chunks, v_blocks, half):
    pv, idx = None, 0
    for vb in v_blocks:
        n = vb.shape[0] // LANES
        p = jnp.concatenate(p_chunks[idx:idx + n], axis=1).astype(BF16)
        idx += n
        lane = _lane_iota(vb.shape)
        own = (lane < HEAD_DIM) if half == 0 else (lane >= HEAD_DIM)
        d = jnp.dot(p, jnp.where(own, vb, jnp.ones_like(vb)), preferred_element_type=F32)
        pv = d if pv is None else pv + d
    return pv


def _softmax_once(s, v_aug, bias_blocks, sink):
    chunks = []
    for c in range(s.shape[1] // LANES):
        sc = s[:, c * LANES:(c + 1) * LANES]
        bb = bias_blocks[c * LANES // ROW_BLK]
        if bb is not None:
            lo = c * LANES % ROW_BLK
            sc = sc + bb[:, lo:lo + LANES]
        chunks.append(sc)
    m = _row_max(chunks)
    if sink is not None:
        m = jnp.maximum(m, sink)
    p = jnp.concatenate([jnp.exp2(c - m) for c in chunks], axis=1).astype(BF16)
    pv = jnp.dot(p, v_aug, preferred_element_type=F32)
    l = pltpu.roll(pv, HEAD_DIM, 1)
    if sink is not None:
        l = l + jnp.exp2(sink - m)
    return pv / l


def _store_heads(o_ref, outs):
    lane = _lane_iota(outs[0].shape)
    for g in range(2):
        o_ref[:, g * LANES:(g + 1) * LANES] = jnp.where(
            lane < HEAD_DIM, outs[2 * g], outs[2 * g + 1]).astype(o_ref.dtype)


def _attn_band_kernel(*refs, kind, lb, n_kv, tq):
    it = iter(refs)
    q_ref = next(it)
    k_refs = [next(it) for _ in range(n_kv)]
    v_refs = [next(it) for _ in range(n_kv)]
    bias_ref = next(it) if n_kv > 1 else None
    sink_ref = next(it) if kind == "B" else None
    o_ref = next(it)
    t = pl.program_id(0) % lb
    var = jnp.where(t == 0, 0, jnp.where(t == lb - 1, 2, 1))
    plan = _head_plan(kind)
    scores = {}
    for kt in sorted({p[2] for p in plan}):
        heads = [r for r in range(N_HEADS) if plan[r][2] == kt]
        k_all = jnp.concatenate([k[:, kt * LANES:(kt + 1) * LANES] for k in k_refs], axis=0)
        q_all = jnp.concatenate([_head_query(q_ref, plan[r]) for r in heads], axis=0)
        s_all = lax.dot_general(q_all, k_all, _NT, preferred_element_type=F32)
        for j, r in enumerate(heads):
            scores[r] = s_all[j * tq:(j + 1) * tq, :]
    outs = []
    v_aug = {}
    for r, plan_r in enumerate(plan):
        _, _, kt, vt = plan_r
        if (vt, r % 2) not in v_aug:
            vv = jnp.concatenate([v[:, vt * LANES:(vt + 1) * LANES] for v in v_refs], axis=0)
            lane = _lane_iota(vv.shape)
            own = (lane < HEAD_DIM) if r % 2 == 0 else (lane >= HEAD_DIM)
            v_aug[(vt, r % 2)] = jnp.where(own, vv, jnp.ones_like(vv))
        bias = [None]
        for j in range(n_kv - 1):
            if kind == "C":
                bias.append(bias_ref[var, j, r * tq:(r + 1) * tq, :])
            else:
                bias.append(bias_ref[var, j])
        sink = sink_ref[_GQA_PERM[r]] * LOG2E if kind == "B" else None
        outs.append(_softmax_once(scores[r], v_aug[(vt, r % 2)], bias, sink))
    _store_heads(o_ref, outs)


def _attn_dense_kernel(q_ref, kc_ref, vc_ref, kl_ref, vl_ref, o_ref, qs_ref, m_ref, acc_ref,
                       *, kind, n_steps, tq):
    s = pl.program_id(1)
    plan = _head_plan(kind)

    def update(first):
        if first:
            for r in range(N_HEADS):
                qs_ref[r * tq:(r + 1) * tq, :] = _head_query(q_ref, plan[r])
        scores = {}
        for kt in sorted({p[2] for p in plan}):
            heads = [r for r in range(N_HEADS) if plan[r][2] == kt]
            ksl = slice(kt * LANES, (kt + 1) * LANES)
            k_blocks = [kc_ref[:, ksl], kl_ref[:, ksl]] if first else [kl_ref[:, ksl]]
            q_all = qs_ref[heads[0] * tq:(heads[-1] + 1) * tq, :]
            chunks = _score_chunks(q_all, k_blocks, [None] * len(k_blocks))
            for j, r in enumerate(heads):
                scores[r] = [c[j * tq:(j + 1) * tq, :] for c in chunks]
        for r, (_, _, kt, vt) in enumerate(plan):
            rows = slice(r * tq, (r + 1) * tq)
            vsl = slice(vt * LANES, (vt + 1) * LANES)
            v_blocks = [vc_ref[:, vsl], vl_ref[:, vsl]] if first else [vl_ref[:, vsl]]
            chunks = scores[r]
            m_cur = _row_max(chunks)
            if first:
                m_new = m_cur
            else:
                m_prev = m_ref[rows, :]
                m_new = jnp.maximum(m_prev, m_cur)
                alpha = jnp.exp2(m_prev - m_new)
            p = [jnp.exp2(c - m_new) for c in chunks]
            pv = _weighted_values(p, v_blocks, r % 2)
            if first:
                acc_ref[rows, :] = pv
            else:
                acc_ref[rows, :] = alpha * acc_ref[rows, :] + pv
            m_ref[rows, :] = m_new

    @pl.when(s == 0)
    def _():
        update(True)

    @pl.when(s > 0)
    def _():
        update(False)

    @pl.when(s == n_steps - 1)
    def _():
        outs = []
        for r in range(N_HEADS):
            rows = slice(r * tq, (r + 1) * tq)
            acc = acc_ref[rows, :]
            outs.append(acc / pltpu.roll(acc, HEAD_DIM, 1))
        _store_heads(o_ref, outs)


def _attn_latent_call(lay, kind, q, k, v, *, sink=None, bias=None):
    T, B, ncb, lb, S = lay["T"], lay["B"], lay["ncb"], lay["lb"], lay["S"]
    tq = ROW_BLK
    qw, kw, vw = q.shape[1], k.shape[1], v.shape[1]
    out_shape = jax.ShapeDtypeStruct((B * S, BRANCH_W), BF16)
    if kind in ("A", "D"):
        tk, tq = lay["tk_dense"], lay["tq_dense"]
        n_steps = S // tk
        lat0 = ncb * ROW_BLK // tk
        q0 = ncb * ROW_BLK // tq
        qpb = S // tq
        lat_blk = lambda i, s: (lat0 + (i // qpb) * n_steps + s, 0)
        return pl.pallas_call(
            functools.partial(_attn_dense_kernel, kind=kind, n_steps=n_steps, tq=tq),
            out_shape=out_shape,
            grid=(B * qpb, n_steps),
            in_specs=[pl.BlockSpec((tq, qw), lambda i, s: (q0 + i, 0)),
                      pl.BlockSpec((ROW_BLK, kw), lambda i, s: (i // qpb, 0)),
                      pl.BlockSpec((ROW_BLK, vw), lambda i, s: (i // qpb, 0)),
                      pl.BlockSpec((tk, kw), lat_blk),
                      pl.BlockSpec((tk, vw), lat_blk)],
            out_specs=pl.BlockSpec((tq, BRANCH_W), lambda i, s: (i, 0)),
            scratch_shapes=[pltpu.VMEM((N_HEADS * tq, LANES), BF16),
                            pltpu.VMEM((N_HEADS * tq, LANES), F32),
                            pltpu.VMEM((N_HEADS * tq, LANES), F32)],
            compiler_params=_cparams(("parallel", "arbitrary")),
            name="attn_" + kind,
        )(q, k, v, k, v)

    tq = lay["tq_band"]
    qpb = S // tq
    q0 = ncb * ROW_BLK // tq
    bpq = tq // ROW_BLK

    def nb(i, d):
        return (ncb + (i // qpb) * lb + jnp.clip((i % qpb) * bpq + d, 0, lb - 1), 0)

    kv_maps = [lambda i: (i // qpb, 0)] + [functools.partial(nb, d=d) for d in _band_offsets(tq)]
    n_kv = len(kv_maps)
    in_specs = [pl.BlockSpec((tq, qw), lambda i: (q0 + i, 0))]
    in_specs += [pl.BlockSpec((ROW_BLK, kw), m) for m in kv_maps]
    in_specs += [pl.BlockSpec((ROW_BLK, vw), m) for m in kv_maps]
    in_specs.append(pl.BlockSpec(bias.shape, lambda i: (0,) * bias.ndim, pipeline_mode=pl.Buffered(1)))
    args = [q] + [k] * n_kv + [v] * n_kv + [bias]
    if kind == "B":
        in_specs.append(pl.BlockSpec(memory_space=pltpu.SMEM))
        args.append(sink)
    return pl.pallas_call(
        functools.partial(_attn_band_kernel, kind=kind, lb=qpb, n_kv=n_kv, tq=tq),
        out_shape=out_shape,
        grid=(B * qpb,),
        in_specs=in_specs,
        out_specs=pl.BlockSpec((tq, BRANCH_W), lambda i: (i, 0)),
        compiler_params=_cparams(("parallel",)),
        name="attn_" + kind,
    )(*args)


def _attn_context_call(lay, kind, q, k, v, *, sink=None):
    ncb = lay["ncb"]
    qw, kw, vw = q.shape[1], k.shape[1], v.shape[1]
    blk = lambda i: (i, 0)
    in_specs = [pl.BlockSpec((ROW_BLK, qw), blk), pl.BlockSpec((ROW_BLK, kw), blk),
                pl.BlockSpec((ROW_BLK, vw), blk)]
    args = [q, k, v]
    if kind == "B":
        in_specs.append(pl.BlockSpec(memory_space=pltpu.SMEM))
        args.append(sink)
    return pl.pallas_call(
        functools.partial(_attn_band_kernel, kind=kind, lb=1, n_kv=1, tq=ROW_BLK),
        out_shape=jax.ShapeDtypeStruct((ncb * ROW_BLK, BRANCH_W), BF16),
        grid=(ncb,),
        in_specs=in_specs,
        out_specs=pl.BlockSpec((ROW_BLK, BRANCH_W), blk),
        compiler_params=_cparams(("parallel",)),
        name="attn_ctx_" + kind,
    )(*args)


def _band_offsets(tq):
    return tuple(range(-1, tq // ROW_BLK + 1))


def _window_bias(S, tq):
    lb, qpb, bpq = S // ROW_BLK, S // tq, tq // ROW_BLK
    offs = _band_offsets(tq)
    qa, ka = np.arange(tq), np.arange(ROW_BLK)
    out = np.full((3, len(offs), tq, ROW_BLK), NEG_INF, np.float32)
    for vi, t_rep in enumerate((0, 1, qpb - 1)):
        for di, d in enumerate(offs):
            kt = t_rep * bpq + d
            if not 0 <= kt < lb:
                continue
            qpos = t_rep * tq + qa
            kpos = kt * ROW_BLK + ka
            ok = np.abs(qpos[:, None] - kpos[None, :]) <= WINDOW
            out[vi, di] = np.where(ok, 0.0, NEG_INF)
    return jnp.asarray(out)


def _neighbourhood_bias(rpb, rows_total, tq):
    lb = rows_total * GRID_W // ROW_BLK
    qpb, bpq = rows_total * GRID_W // tq, tq // ROW_BLK
    rpt = ROW_BLK // GRID_W
    rpq = tq // GRID_W
    kh = min(NA_KH, rows_total)
    qa, ka = np.arange(tq), np.arange(ROW_BLK)
    q_sub, q_col = qa // GRID_W, qa % GRID_W
    k_sub, k_col = ka // GRID_W, ka % GRID_W
    n_dr, n_dc = 2 * NA_KH - 1, 2 * NA_KW - 1
    col = np.arange(GRID_W)
    dc = np.clip(col[None, :] - col[:, None], -(NA_KW - 1), NA_KW - 1) + NA_KW - 1
    hot_c = (dc[:, :, None] == np.arange(n_dc)).astype(np.float32)
    by_col = jnp.einsum("huv,cdv->hucd", rpb.astype(F32) * LOG2E, jnp.asarray(hot_c),
                        precision=HIGHEST)
    offs = _band_offsets(tq)
    vals = []
    for d in offs:
        dr = np.clip(d * rpt + np.arange(rpt)[None, :] - np.arange(rpq)[:, None],
                     -(NA_KH - 1), NA_KH - 1) + NA_KH - 1
        hot_r = (dr[:, :, None] == np.arange(n_dr)).astype(np.float32)
        v = jnp.einsum("abu,hucd->hacbd", jnp.asarray(hot_r), by_col, precision=HIGHEST)
        vals.append(v.reshape(N_HEADS, tq, ROW_BLK))
    out = []
    for t_rep in (0, 1, qpb - 1):
        per_block = []
        for di, d in enumerate(offs):
            kt = t_rep * bpq + d
            q_row = t_rep * rpq + q_sub
            k_row = kt * rpt + k_sub
            r_start = np.clip(q_row - kh // 2, 0, rows_total - kh)
            row_ok = (k_row[None] >= r_start[:, None]) & (k_row[None] < r_start[:, None] + kh)
            c_start = np.clip(q_col - NA_KW // 2, 0, GRID_W - NA_KW)
            col_ok = (k_col[None] >= c_start[:, None]) & (k_col[None] < c_start[:, None] + NA_KW)
            ok = row_ok & col_ok & (0 <= kt < lb)
            per_block.append(jnp.where(jnp.asarray(ok)[None], vals[di], NEG_INF).reshape(-1, ROW_BLK))
        out.append(jnp.stack(per_block))
    return jnp.stack(out)


def _merge_kernel(*refs, tm, split_x, n_ctx_tiles):
    it = iter(refs)
    h_ref = next(it)
    is_ctx = pl.program_id(0) < n_ctx_tiles
    if n_ctx_tiles > 0:
        o_refs = [(next(it), next(it)) for _ in range(4)]
        o_vals = [_tile_rows(oc, ol, is_ctx) for oc, ol in o_refs]
    else:
        o_vals = [next(it)[...] for _ in range(4)]
    if split_x:
        xc_ref, xl_ref = next(it), next(it)
    else:
        x_ref = next(it)
    (mod_ref, gffn_ref, wg_ref, bg_ref, wb_ref, wout_ref, wr_ref, br_ref,
     x1_ref, h2_ref, route_ref, y_ref) = (next(it) for _ in range(12))
    hb = h_ref[...]
    d_model = hb.shape[1]
    for t in range(d_model // MERGE_TN):
        cs = slice(t * MERGE_TN, (t + 1) * MERGE_TN)
        y = None
        for n, o_n in enumerate(o_vals):
            gate = _sigmoid(jnp.dot(hb, wg_ref[n, :, cs], preferred_element_type=F32) + bg_ref[n, :, cs])
            u = gate * jnp.dot(o_n, wb_ref[n, :, cs], preferred_element_type=F32)
            y = u if y is None else y + u
        y_ref[:, cs] = y.astype(BF16)
    z = jnp.dot(y_ref[...], wout_ref[...], preferred_element_type=F32)
    x_in = _tile_rows(xc_ref, xl_ref, is_ctx) if split_x else x_ref[...]
    x1 = x_in + mod_ref[0, 2:3, :] * z
    x1_ref[...] = x1
    h2 = _rms(x1) * gffn_ref[...] * (1.0 + mod_ref[0, 4:5, :]) + mod_ref[0, 3:4, :]
    for c in range(SUBLANES):
        h2_ref[pl.ds(c, tm, stride=SUBLANES), :] = h2[:, c * LANES:(c + 1) * LANES]

    h2_hi = h2.astype(BF16)
    h2_lo = (h2 - h2_hi.astype(F32)).astype(BF16)
    logit = (jnp.dot(h2_hi, wr_ref[0], preferred_element_type=F32)
             + jnp.dot(h2_lo, wr_ref[0], preferred_element_type=F32)
             + jnp.dot(h2_hi, wr_ref[1], preferred_element_type=F32)) + br_ref[...]
    lane = _lane_iota(logit.shape)
    big = jnp.int32(1 << 20)
    is_g = (lane >= N_EXPERTS) & (lane < N_EXPERTS + N_GROUPS)
    gl = jnp.where(is_g, logit, NEG_INF)
    gmax = jnp.max(gl, axis=-1, keepdims=True)
    gsel = jnp.min(jnp.where(gl == gmax, lane - N_EXPERTS, big), axis=-1, keepdims=True)
    gw = 1.0 / jnp.sum(jnp.where(is_g, jnp.exp(gl - gmax), 0.0), axis=-1, keepdims=True)
    in_grp = (lane < N_EXPERTS) & ((lane // EXPERTS_PER_GROUP) == gsel)
    el = jnp.where(in_grp, logit, NEG_INF)
    v1 = jnp.max(el, axis=-1, keepdims=True)
    i1 = jnp.min(jnp.where(el == v1, lane, big), axis=-1, keepdims=True)
    el2 = jnp.where(lane == i1, NEG_INF, el)
    v2 = jnp.max(el2, axis=-1, keepdims=True)
    i2 = jnp.min(jnp.where(el2 == v2, lane, big), axis=-1, keepdims=True)
    e21 = jnp.exp(v2 - v1)
    w1 = gw / (1.0 + e21)
    w2 = gw * e21 / (1.0 + e21)
    route_ref[...] = jnp.where(lane == 0, i1.astype(F32),
                               jnp.where(lane == 1, i2.astype(F32),
                                         jnp.where(lane == 2, w1, jnp.where(lane == 3, w2, 0.0))))


def _merge_call(lay, layer, h, o, x, mod, w, *, with_ctx):
    T, D, tm = lay["T"], lay["D"], lay["tm"]
    nct, tpb, B = lay["nct"], lay["tpb"], lay["B"]
    off = 0 if with_ctx else nct
    n_tiles = T // tm - off

    def mod_row(i):
        return jnp.where(i < nct, B, (i - nct) // tpb)

    row = lambda i: (i + off, 0)
    const2 = lambda i: (0, 0)
    const3 = lambda i: (0, 0, 0)
    split_x = isinstance(x, (tuple, list))
    in_specs = [pl.BlockSpec((tm, D), row)]
    o_args = []
    for kind in ("A", "B", "C", "D"):
        o_ctx, o_lat = o[kind]
        if with_ctx:
            in_specs += _split_row_specs(lay, BRANCH_W)
            o_args += [o_ctx, o_lat]
        else:
            in_specs.append(pl.BlockSpec((tm, BRANCH_W), lambda i: (i, 0)))
            o_args.append(o_lat)
    in_specs += _split_row_specs(lay, D, off) if split_x else [pl.BlockSpec((tm, D), row)]
    in_specs += [pl.BlockSpec((1, 6, D), lambda i: (layer * lay["mod_rows"] + mod_row(i + off), 0, 0)),
                 pl.BlockSpec((1, D), const2),
                 pl.BlockSpec(w["w_gate"].shape, const3),
                 pl.BlockSpec(w["b_gate"].shape, const3),
                 pl.BlockSpec(w["w_branch"].shape, const3),
                 pl.BlockSpec((D, D), const2),
                 pl.BlockSpec((2, D, LANES), const3),
                 pl.BlockSpec((1, LANES), const2)]
    out_shape = [jax.ShapeDtypeStruct((T, D), F32),
                 jax.ShapeDtypeStruct((T * SUBLANES, LANES), F32),
                 jax.ShapeDtypeStruct((T, LANES), F32)]
    out_specs = [pl.BlockSpec((tm, D), row),
                 pl.BlockSpec((tm * SUBLANES, LANES), row),
                 pl.BlockSpec((tm, LANES), row)]
    return pl.pallas_call(
        functools.partial(_merge_kernel, tm=tm, split_x=split_x, n_ctx_tiles=nct - off),
        out_shape=out_shape, grid=(n_tiles,), in_specs=in_specs, out_specs=out_specs,
        scratch_shapes=[pltpu.VMEM((tm, D), BF16)],
        compiler_params=_cparams(("parallel",)),
        name="merge",
    )(h, *o_args, *(x if split_x else (x,)), mod, w["g_ffn"], w["w_gate"],
      w["b_gate"], w["w_branch"], w["w_out"], w["w_route"], w["b_route"])


def _moe_kernel(tab_ref, tok_ref, h2_ref, sw_ref, w1_ref, w3_ref, w2_ref, f_ref,
                xg_ref, y_ref, st_ref, *, blk, slots, tile_off):
    ti = pl.program_id(0)
    step = pl.program_id(1)

    @pl.when(step == 0)
    def _():
        f_ref[...] = jnp.zeros_like(f_ref)
        xg_ref[...] = jnp.zeros_like(xg_ref)

    tok_base = (ti + tile_off) * slots
    segs = []
    for j in range(MOE_EXPERTS_PER_STEP):
        base = ((ti + tile_off) * N_EXPERTS + step * MOE_EXPERTS_PER_STEP + j) * 2
        segs.append((tab_ref[base], tab_ref[base + 1]))

    def gather(j, off, n_rows):
        slot0 = tok_base + off

        def body(gi, c):
            r0 = gi * MOE_GROUP
            srcs = [pl.multiple_of(tok_ref[slot0 + r0 + u], SUBLANES) for u in range(MOE_GROUP)]
            for u in range(MOE_GROUP):
                dst = pl.multiple_of((r0 + u) * SUBLANES, SUBLANES)
                xg_ref[j, pl.ds(dst, SUBLANES), :] = h2_ref[pl.ds(srcs[u], SUBLANES), :]
            return c

        lax.fori_loop(0, n_rows // MOE_GROUP, body, 0)

    def ffn(j, off):
        xb = _load_token_major(xg_ref.at[j], blk).astype(BF16)
        a = jnp.dot(xb, w1_ref[j], preferred_element_type=F32)
        g = jnp.dot(xb, w3_ref[j], preferred_element_type=F32)
        hid = (a * _sigmoid(a) * g).astype(BF16)
        y = jnp.dot(hid, w2_ref[j], preferred_element_type=F32)
        y_ref[j] = y * sw_ref[0, pl.ds(off, blk), :]

    def scatter(j, off, n_rows):
        slot0 = tok_base + off

        def body(gi, c):
            r0 = pl.multiple_of(gi * MOE_GROUP, MOE_GROUP)
            dsts = [pl.multiple_of(tok_ref[slot0 + r0 + u], SUBLANES) for u in range(MOE_GROUP)]
            for ch in range(SUBLANES):
                st_ref[pl.ds(ch, MOE_GROUP, stride=SUBLANES), :] = y_ref[j, pl.ds(r0, MOE_GROUP),
                                                                         ch * LANES:(ch + 1) * LANES]
            vals = [f_ref[pl.ds(dsts[u], SUBLANES), :] + st_ref[u * SUBLANES:(u + 1) * SUBLANES, :]
                    for u in range(MOE_GROUP)]
            for u in reversed(range(MOE_GROUP)):
                f_ref[pl.ds(dsts[u], SUBLANES), :] = vals[u]
            return c

        lax.fori_loop(0, n_rows // MOE_GROUP, body, 0)

    for j, (seg0, n_pad) in enumerate(segs):
        gather(j, pl.multiple_of(seg0, MOE_GROUP), jnp.minimum(n_pad, blk))
    for j, (seg0, _) in enumerate(segs):
        ffn(j, pl.multiple_of(seg0, MOE_GROUP))
    for j, (seg0, n_pad) in enumerate(segs):
        scatter(j, pl.multiple_of(seg0, MOE_GROUP), jnp.minimum(n_pad, blk))

    for j, (seg0, n_pad) in enumerate(segs):
        def extra_block(b, carry, j=j, seg0=seg0, n_pad=n_pad):
            off = pl.multiple_of(seg0 + b * blk, MOE_GROUP)
            n_rows = jnp.minimum(n_pad - b * blk, blk)
            gather(j, off, n_rows)
            ffn(j, off)
            scatter(j, off, n_rows)
            return carry

        lax.fori_loop(1, (n_pad + blk - 1) // blk, extra_block, 0)


def _moe_call(lay, route, h2, w, *, with_ctx):
    T, D, tt = lay["T"], lay["D"], lay["tt"]
    blk = lay["moe_blk"]
    n_tiles_all = T // tt
    tile_off = 0 if with_ctx else lay["ncb"] * ROW_BLK // tt
    n_tiles = n_tiles_all - tile_off
    n_assign = 2 * tt
    n_fill = N_EXPERTS * MOE_GROUP
    slots = n_assign + n_fill + blk

    eid = route[:, 0:2].astype(jnp.int32).reshape(n_tiles_all, n_assign)
    wts = route[:, 2:4].reshape(n_tiles_all, n_assign)
    tok = jnp.broadcast_to(jnp.arange(n_assign, dtype=jnp.int32)[None, :] // 2 * SUBLANES, eid.shape)
    ex = jnp.arange(N_EXPERTS, dtype=jnp.int32)
    counts = jnp.sum(eid[:, :, None] == ex[None, None, :], axis=1, dtype=jnp.int32)
    n_dummy = (-counts) % MOE_GROUP
    fill_key = jnp.where(jnp.arange(MOE_GROUP, dtype=jnp.int32)[None, None, :] < n_dummy[:, :, None],
                         ex[None, :, None], N_EXPERTS).reshape(n_tiles_all, n_fill)
    zeros_i = jnp.zeros((n_tiles_all, n_fill), jnp.int32)
    _, slot_tok, slot_w = lax.sort(
        (jnp.concatenate([eid, fill_key], axis=1), jnp.concatenate([tok, zeros_i], axis=1),
         jnp.concatenate([wts, zeros_i.astype(F32)], axis=1)),
        dimension=1, is_stable=True, num_keys=1)
    slot_tok = jnp.pad(slot_tok, ((0, 0), (0, blk)))
    slot_w = jnp.pad(slot_w, ((0, 0), (0, blk)))
    padded = counts + n_dummy
    seg0 = jnp.cumsum(padded, axis=1) - padded
    tab = jnp.stack([seg0, padded], axis=-1).reshape(-1).astype(jnp.int32)

    return pl.pallas_call(
        functools.partial(_moe_kernel, blk=blk, slots=slots, tile_off=tile_off),
        out_shape=jax.ShapeDtypeStruct((T * SUBLANES, LANES), F32),
        grid_spec=pltpu.PrefetchScalarGridSpec(
            num_scalar_prefetch=2,
            grid=(n_tiles, N_EXPERTS // MOE_EXPERTS_PER_STEP),
            in_specs=[
                pl.BlockSpec((tt * SUBLANES, LANES), lambda t, e, *_: (t + tile_off, 0)),
                pl.BlockSpec((1, slots, 1), lambda t, e, *_: (t + tile_off, 0, 0)),
                pl.BlockSpec((MOE_EXPERTS_PER_STEP, D, EXPERT_FF), lambda t, e, *_: (e, 0, 0)),
                pl.BlockSpec((MOE_EXPERTS_PER_STEP, D, EXPERT_FF), lambda t, e, *_: (e, 0, 0)),
                pl.BlockSpec((MOE_EXPERTS_PER_STEP, EXPERT_FF, D), lambda t, e, *_: (e, 0, 0)),
            ],
            out_specs=pl.BlockSpec((tt * SUBLANES, LANES), lambda t, e, *_: (t + tile_off, 0)),
            scratch_shapes=[pltpu.VMEM((MOE_EXPERTS_PER_STEP, blk * SUBLANES, LANES), F32),
                            pltpu.VMEM((MOE_EXPERTS_PER_STEP, blk, D), F32),
                            pltpu.VMEM((MOE_GROUP * SUBLANES, LANES), F32)],
        ),
        compiler_params=_cparams(("parallel", "arbitrary")),
        name="moe_experts",
    )(tab, slot_tok.reshape(-1), h2, slot_w.reshape(n_tiles_all, slots, 1),
      w["w_ff1"], w["w_ff3"], w["w_ff2"])


def _final_kernel(x_ref, f_ref, mod_ref, g_ref, o_ref, *, tm):
    xf = x_ref[...] + mod_ref[0, 5:6, :] * _load_token_major(f_ref, tm)
    o_ref[...] = _rms(xf) * g_ref[...]


def _final_call(lay, layer, x1, f, mod, g_final):
    T, D, tm = lay["T"], lay["D"], lay["tm"]
    nct, tpb = lay["nct"], lay["tpb"]
    n_lat = T // tm - nct
    return pl.pallas_call(
        functools.partial(_final_kernel, tm=tm),
        out_shape=jax.ShapeDtypeStruct((n_lat * tm, D), F32),
        grid=(n_lat,),
        in_specs=[pl.BlockSpec((tm, D), lambda i: (i + nct, 0)),
                  pl.BlockSpec((tm * SUBLANES, LANES), lambda i: (i + nct, 0)),
                  pl.BlockSpec((1, 6, D), lambda i: (layer * lay["mod_rows"] + i // tpb, 0, 0)),
                  pl.BlockSpec((1, D), lambda i: (0, 0))],
        out_specs=pl.BlockSpec((tm, D), lambda i: (i, 0)),
        compiler_params=_cparams(("parallel",)),
        name="final_norm",
    )(x1, f, mod, g_final)


def _select_cols(wm, segs, scale=None):
    parts = []
    for k, (start, width) in enumerate(segs):
        if start is None:
            parts.append(jnp.zeros((wm.shape[0], width), wm.dtype))
        else:
            blk = wm[:, start:start + width]
            parts.append(blk if scale is None or scale[k] is None else blk * scale[k])
    return jnp.concatenate(parts, axis=1)


def _prep_layer(l, p):
    a_cols = A_Q_RANK + A_KV_RANK + A_ROPE
    b_off = a_cols
    c_off = b_off + 512
    d_off = c_off + 768
    qk_scale = HEAD_DIM ** -0.5 * LOG2E
    gqa_q = lambda off: [(off + hh * HEAD_DIM, HEAD_DIM) for hh in _GQA_PERM]
    segs = ([(0, 256), (256, 128), (None, 64), (384, 32), (None, 32)]
            + gqa_q(b_off) + [(b_off + 256, 128), (b_off + 384, 128)]
            + [(c_off, 256), (c_off + 256, 256), (c_off + 512, 256)]
            + gqa_q(d_off) + [(d_off + 256, 128), (d_off + 384, 128)])
    scale = [None] * len(segs)
    for k in (5, 6, 7, 8, 11):
        scale[k] = qk_scale
    w_in = _select_cols(p["w_in"][l], segs, scale).astype(BF16)
    assert w_in.shape[1] == PROJ_COLS

    hq = A_NOPE + A_ROPE
    segs_q = []
    for hh in range(N_HEADS):
        segs_q += [(hh * hq, hq), (None, LANES - hq)]
    w_q_b = _select_cols(p["w_q_b"][l], segs_q).astype(BF16)
    hk = A_NOPE + HEAD_DIM
    segs_k = []
    for hh in range(N_HEADS):
        segs_k += [(hh * hk, A_NOPE), (None, LANES - A_NOPE)]
    segs_k += [(hh * hk + A_NOPE, HEAD_DIM) for hh in range(N_HEADS)]
    w_kv_b = _select_cols(p["w_kv_b"][l], segs_k).astype(BF16)

    wb = p["w_branch"][l]
    perm_rows = lambda m: jnp.concatenate([m[hh * HEAD_DIM:(hh + 1) * HEAD_DIM] for hh in _GQA_PERM], axis=0)
    w_branch = jnp.stack([wb[0], perm_rows(wb[1]), wb[2], perm_rows(wb[3])]).astype(BF16)

    d = p["w_in"].shape[1]
    w_route = jnp.zeros((d, LANES), F32)
    w_route = w_route.at[:, :N_EXPERTS].set(p["w_router"][l]).at[:, N_EXPERTS:N_EXPERTS + N_GROUPS].set(p["w_group"][l])
    b_route = jnp.zeros((1, LANES), F32)
    b_route = b_route.at[0, :N_EXPERTS].set(p["b_router"][l]).at[0, N_EXPERTS:N_EXPERTS + N_GROUPS].set(p["b_group"][l])
    return {
        "g_mix": p["g_norm_mix"][l][None, :],
        "w_in": w_in,
        "g_q_a": p["g_q_a"][l][None, :],
        "w_q_b": w_q_b,
        "g_kv_a": p["g_kv_a"][l][None, :],
        "w_kv_b": w_kv_b,
        "g_q_d": (jnp.tile(p["g_q_d"][l], 2) * qk_scale)[None, :],
        "g_k_d": jnp.tile(p["g_k_d"][l], 2)[None, :],
        "sink": p["sink_b"][l],
        "rpb": p["rpb_c"][l],
        "w_gate": p["w_gate"][l].astype(BF16),
        "b_gate": p["b_gate"][l][:, None, :],
        "w_branch": w_branch,
        "w_out": p["w_out"][l].astype(BF16),
        "g_ffn": p["g_norm_ffn"][l][None, :],
        "w_route": jnp.stack([w_route.astype(BF16),
                              (w_route - w_route.astype(BF16).astype(F32)).astype(BF16)]),
        "b_route": b_route,
        "w_ff1": p["w_ff1"][l].astype(BF16),
        "w_ff3": p["w_ff3"][l].astype(BF16),
        "w_ff2": p["w_ff2"][l].astype(BF16),
    }


def _rope_tables(S, tm):
    t = np.arange(S)
    rows = (t // GRID_W).astype(np.float32)
    cols = (t % GRID_W).astype(np.float32)

    def cs(rot):
        half = rot // 2
        inv = np.float32(ROPE_THETA) ** (-np.arange(0, half, 2, dtype=np.float32) / np.float32(half))
        ar_, ac_ = rows[:, None] * inv, cols[:, None] * inv
        cos = np.concatenate([np.cos(ar_), np.cos(ar_), np.cos(ac_), np.cos(ac_)], axis=-1)
        sin = np.concatenate([-np.sin(ar_), np.sin(ar_), -np.sin(ac_), np.sin(ac_)], axis=-1)
        return cos.astype(np.float32), sin.astype(np.float32)

    cos64, sin64 = cs(HEAD_DIM)
    cos32, sin32 = cs(A_ROPE)
    ones = lambda n: np.ones((S, n), np.float32)
    zeros = lambda n: np.zeros((S, n), np.float32)
    tabs = {
        "cos_h": np.concatenate([cos64, cos64], axis=-1),
        "sin_h": np.concatenate([sin64, sin64], axis=-1),
        "cos_a": np.concatenate([ones(A_NOPE), cos32, ones(LANES - A_NOPE - A_ROPE)], axis=-1),
        "sin_a": np.concatenate([zeros(A_NOPE), sin32, zeros(LANES - A_NOPE - A_ROPE)], axis=-1),
    }
    ident = {"cos_h": 1.0, "sin_h": 0.0, "cos_a": 1.0, "sin_a": 0.0}
    return {k: jnp.asarray(np.concatenate([v, np.full((tm, LANES), ident[k], np.float32)], axis=0))
            for k, v in tabs.items()}


def _layout(B, S, n_ctx, D):
    assert n_ctx == ROW_BLK and S % 1024 == 0 and S // GRID_W >= 3 * (ROW_BLK // GRID_W)
    T = B * (n_ctx + S)
    tm = 512 if (B * n_ctx) % 512 == 0 else 256
    tk_dense = 1024 if (B * n_ctx) % 1024 == 0 else (512 if (B * n_ctx) % 512 == 0 else 256)
    tt = 2048 if (B * n_ctx) % 2048 == 0 else B * n_ctx
    assert S % tt == 0
    return {
        "B": B, "S": S, "D": D, "T": T, "tm": tm,
        "ncb": B * n_ctx // ROW_BLK,
        "lb": S // ROW_BLK,
        "nct": B * n_ctx // tm,
        "tpb": S // tm,
        "tk_dense": tk_dense,
        "tq_dense": tk_dense,
        "tq_band": tm,
        "tt": tt, "moe_blk": 160,
        "mod_rows": 16,
    }


def kernel(x, c, ctx, c_ctx, w_mod, b_mod, g_norm_mix, w_in, g_q_a, w_q_b, g_kv_a, w_kv_b, sink_b, rpb_c,
           g_q_d, g_k_d, w_gate, b_gate, w_branch, w_out, g_norm_ffn, w_group, b_group, w_router, b_router,
           w_ff1, w_ff3, w_ff2, g_final):
    B, S, D = x.shape
    n_ctx = ctx.shape[1]
    depth = w_mod.shape[0]
    lay = _layout(B, S, n_ctx, D)
    params = dict(w_in=w_in, g_norm_mix=g_norm_mix, g_q_a=g_q_a, w_q_b=w_q_b, g_kv_a=g_kv_a, w_kv_b=w_kv_b,
                  sink_b=sink_b, rpb_c=rpb_c, g_q_d=g_q_d, g_k_d=g_k_d, w_gate=w_gate, b_gate=b_gate,
                  w_branch=w_branch, w_out=w_out, g_norm_ffn=g_norm_ffn, w_group=w_group, b_group=b_group,
                  w_router=w_router, b_router=b_router, w_ff1=w_ff1, w_ff3=w_ff3, w_ff2=w_ff2)

    c_all = jnp.zeros((lay["mod_rows"], D), F32).at[:B].set(c).at[B].set(c_ctx)
    mod = _modulation(c_all, w_mod, b_mod).reshape(depth * lay["mod_rows"], 6, D)
    tabs = _rope_tables(S, lay["tm"])
    win_bias = _window_bias(S, lay["tq_band"])
    xf = (ctx.reshape(B * n_ctx, D), x.reshape(B * S, D))

    f = None
    for l in range(depth):
        with_ctx = l < depth - 1
        w = _prep_layer(l, params)
        xf, pr = _proj_call(lay, xf, f, mod, mod, l, w, tabs)
        o = {}
        for kind, kl in (("A", "a"), ("B", "b"), ("C", "c"), ("D", "d")):
            q, k, v = pr["q" + kl], pr["k" + kl], pr["v" + kl]
            sink = w["sink"] if kind == "B" else None
            bias = win_bias if kind == "B" else None
            if kind == "C":
                bias = _neighbourhood_bias(w["rpb"], S // GRID_W, lay["tq_band"])
            o_ctx = _attn_context_call(lay, kind, q, k, v, sink=sink) if with_ctx else None
            o[kind] = (o_ctx, _attn_latent_call(lay, kind, q, k, v, sink=sink, bias=bias))
        xf, h2, route = _merge_call(lay, l, pr["h"], o, xf, mod, w, with_ctx=with_ctx)
        f = _moe_call(lay, route, h2, w, with_ctx=with_ctx)
    out = _final_call(lay, depth - 1, xf, f, mod, g_final[None, :])
    return out.reshape(B, S, D)
```

```python
import functools

import numpy as np
import jax
import jax.numpy as jnp
from jax import lax
from jax.experimental import pallas as pl
from jax.experimental.pallas import tpu as pltpu

F32 = jnp.float32
BF16 = jnp.bfloat16
HIGHEST = lax.Precision.HIGHEST

GRID_W = 64
ROPE_THETA = 10000.0
EPS = 1e-6
NEG_INF = -1e30
LOG2E = 1.4426950408889634
HEAD_DIM = 64
N_HEADS = 4
BRANCH_W = 256
A_Q_RANK = 256
A_KV_RANK = 128
A_NOPE = 64
A_ROPE = 32
NA_KH = 8
NA_KW = 16
WINDOW = 128
N_GROUPS = 4
EXPERTS_PER_GROUP = 8
N_EXPERTS = 32
EXPERT_FF = 256

LANES = 128
SUBLANES = 8
ROW_BLK = 256
MERGE_TN = 256
MOE_EXPERTS_PER_STEP = 4
MOE_GROUP = 16
VMEM_LIMIT = 56 * 1024 * 1024

_PROJ_GROUPS = (("cq", 256), ("ckv", 128), ("kr", 128), ("qb", 256), ("kb", 128), ("vb", 128),
                ("qc", 256), ("kc", 256), ("vc", 256), ("qd", 256), ("kd", 128), ("vd", 128))
_PROJ_OFF = {}
_o = 0
for _n, _w in _PROJ_GROUPS:
    _PROJ_OFF[_n] = (_o, _w)
    _o += _w
PROJ_COLS = _o
_GQA_PERM = (0, 2, 1, 3)


def _cparams(sem):
    return pltpu.CompilerParams(dimension_semantics=sem, vmem_limit_bytes=VMEM_LIMIT)


def _lane_iota(shape):
    return lax.broadcasted_iota(jnp.int32, shape, len(shape) - 1)


def _sigmoid(x):
    return 1.0 / (1.0 + jnp.exp(-x))


def _mod_kernel(c_ref, w_ref, b_ref, o_ref):
    cf = c_ref[...]
    s = cf * _sigmoid(cf)
    o_ref[0] = jnp.dot(s, w_ref[0], precision=HIGHEST, preferred_element_type=F32) + b_ref[0]


def _modulation(c_all, w_mod, b_mod):
    n_layers, d, n_out = w_mod.shape
    rows = c_all.shape[0]
    tn = 1536
    return pl.pallas_call(
        _mod_kernel,
        out_shape=jax.ShapeDtypeStruct((n_layers, rows, n_out), F32),
        grid=(n_layers, n_out // tn),
        in_specs=[pl.BlockSpec((rows, d), lambda l, j: (0, 0)),
                  pl.BlockSpec((1, d, tn), lambda l, j: (l, 0, j)),
                  pl.BlockSpec((1, 1, tn), lambda l, j: (l, 0, j))],
        out_specs=pl.BlockSpec((1, rows, tn), lambda l, j: (l, 0, j)),
        compiler_params=_cparams(("arbitrary", "arbitrary")),
        name="modulation",
    )(c_all, w_mod, b_mod.reshape(n_layers, 1, n_out))


def _rms(x):
    return x * lax.rsqrt(jnp.mean(x * x, axis=-1, keepdims=True) + EPS)


def _swap_blocks(x, blk):
    lane = _lane_iota(x.shape)
    up = pltpu.roll(x, LANES - blk, 1)
    dn = pltpu.roll(x, blk, 1)
    return jnp.where((lane // blk) % 2 == 0, up, dn)


def _rope(x, cos, sin, blk):
    return x * cos + _swap_blocks(x, blk) * sin


def _pair_norm(x, g):
    lo = _lane_iota(x.shape) < HEAD_DIM
    sq = x * x
    s_lo = jnp.sum(jnp.where(lo, sq, 0.0), axis=-1, keepdims=True)
    s_hi = jnp.sum(jnp.where(lo, 0.0, sq), axis=-1, keepdims=True)
    ms = jnp.where(lo, s_lo, s_hi) * (1.0 / HEAD_DIM)
    return x * lax.rsqrt(ms + EPS) * g


def _load_token_major(ref, rows):
    return jnp.concatenate(
        [ref[pl.ds(c, rows, stride=SUBLANES), :] for c in range(SUBLANES)], axis=-1)


def _tile_rows(xc_ref, xl_ref, is_ctx):
    return jnp.where(is_ctx, xc_ref[...], xl_ref[...])


def _proj_kernel(*refs, with_f, tm, scale_a, nct):
    it = iter(refs)
    if with_f:
        x_ref = next(it)
        f_ref = next(it)
        modp_ref = next(it)
    else:
        xc_ref, xl_ref = next(it), next(it)
    mod_ref = next(it)
    gmix_ref, win_ref, gqa_ref, wqb_ref, gkva_ref, wkvb_ref, gqd_ref, gkd_ref = (next(it) for _ in range(8))
    cosh_ref, sinh_ref, cosa_ref, sina_ref = (next(it) for _ in range(4))
    if with_f:
        x2_ref = next(it)
    h_ref = next(it)
    qa_ref, ka_ref, va_ref, qb_ref, kb_ref, vb_ref, qc_ref, kc_ref, vc_ref, qd_ref, kd_ref, vd_ref = (
        next(it) for _ in range(12))

    if with_f:
        xf = x_ref[...] + modp_ref[0, 5:6, :] * _load_token_major(f_ref, tm)
        x2_ref[...] = xf
    else:
        xf = _tile_rows(xc_ref, xl_ref, pl.program_id(0) < nct)
    h = _rms(xf) * gmix_ref[...] * (1.0 + mod_ref[0, 1:2, :]) + mod_ref[0, 0:1, :]
    hb = h.astype(BF16)
    h_ref[...] = hb
    p = jnp.dot(hb, win_ref[...], preferred_element_type=F32)

    def grp(name):
        o, w = _PROJ_OFF[name]
        return p[:, o:o + w]

    cosh, sinh = cosh_ref[...], sinh_ref[...]
    cosa, sina = cosa_ref[...], sina_ref[...]

    cq = (_rms(grp("cq")) * gqa_ref[...]).astype(BF16)
    qa = jnp.dot(cq, wqb_ref[...], preferred_element_type=F32)
    for hd in range(N_HEADS):
        sl = slice(hd * LANES, (hd + 1) * LANES)
        qa_ref[:, sl] = (_rope(qa[:, sl], cosa, sina, 8) * scale_a).astype(BF16)
    ckv = (_rms(grp("ckv")) * gkva_ref[...]).astype(BF16)
    kva = jnp.dot(ckv, wkvb_ref[...], preferred_element_type=F32)
    kr = _rope(grp("kr"), cosa, sina, 8)
    for hd in range(N_HEADS):
        sl = slice(hd * LANES, (hd + 1) * LANES)
        ka_ref[:, sl] = (kva[:, sl] + kr).astype(BF16)
    va_ref[...] = kva[:, N_HEADS * LANES:].astype(BF16)

    qb = grp("qb")
    for j in range(2):
        sl = slice(j * LANES, (j + 1) * LANES)
        qb_ref[:, sl] = _rope(qb[:, sl], cosh, sinh, 16).astype(BF16)
    kb_ref[...] = _rope(grp("kb"), cosh, sinh, 16).astype(BF16)
    vb_ref[...] = grp("vb").astype(BF16)

    qc_ref[...] = grp("qc").astype(BF16)
    kc_ref[...] = grp("kc").astype(BF16)
    vc_ref[...] = grp("vc").astype(BF16)

    qd = grp("qd")
    for j in range(2):
        sl = slice(j * LANES, (j + 1) * LANES)
        qd_ref[:, sl] = _rope(_pair_norm(qd[:, sl], gqd_ref[...]), cosh, sinh, 16).astype(BF16)
    kd_ref[...] = _rope(_pair_norm(grp("kd"), gkd_ref[...]), cosh, sinh, 16).astype(BF16)
    vd_ref[...] = grp("vd").astype(BF16)


def _split_row_specs(lay, width, off=0):
    tm, nct = lay["tm"], lay["nct"]
    return [pl.BlockSpec((tm, width), lambda i: (jnp.minimum(i + off, nct - 1), 0)),
            pl.BlockSpec((tm, width), lambda i: (jnp.maximum(i + off - nct, 0), 0))]


def _proj_call(lay, x, f, modp, mod, layer, w, tabs):
    T, D, tm = lay["T"], lay["D"], lay["tm"]
    nct, tpb, B = lay["nct"], lay["tpb"], lay["B"]
    with_f = f is not None
    n_tiles = T // tm

    def mod_row(i):
        return jnp.where(i < nct, B, (i - nct) // tpb)

    def tab_blk(i):
        return jnp.where(i < nct, tpb, (i - nct) % tpb)

    row = lambda i: (i, 0)
    const = lambda i: (0, 0)
    if with_f:
        in_specs = [pl.BlockSpec((tm, D), row),
                    pl.BlockSpec((tm * SUBLANES, LANES), row),
                    pl.BlockSpec((1, 6, D), lambda i: ((layer - 1) * lay["mod_rows"] + mod_row(i), 0, 0))]
        args = [x, f, modp]
    else:
        in_specs = _split_row_specs(lay, D)
        args = list(x)
    in_specs += [pl.BlockSpec((1, 6, D), lambda i: (layer * lay["mod_rows"] + mod_row(i), 0, 0))]
    args += [mod]
    for name in ("g_mix", "w_in", "g_q_a", "w_q_b", "g_kv_a", "w_kv_b", "g_q_d", "g_k_d"):
        a = w[name]
        in_specs.append(pl.BlockSpec(a.shape, const))
        args.append(a)
    for tname in ("cos_h", "sin_h", "cos_a", "sin_a"):
        in_specs.append(pl.BlockSpec((tm, LANES), lambda i: (tab_blk(i), 0)))
        args.append(tabs[tname])

    widths = [("h", D), ("qa", 512), ("ka", 512), ("va", 256), ("qb", 256), ("kb", 128), ("vb", 128),
              ("qc", 256), ("kc", 256), ("vc", 256), ("qd", 256), ("kd", 128), ("vd", 128)]
    out_shape, out_specs = [], []
    if with_f:
        out_shape.append(jax.ShapeDtypeStruct((T, D), F32))
        out_specs.append(pl.BlockSpec((tm, D), row))
    for _, wd in widths:
        out_shape.append(jax.ShapeDtypeStruct((T, wd), BF16))
        out_specs.append(pl.BlockSpec((tm, wd), row))

    outs = pl.pallas_call(
        functools.partial(_proj_kernel, with_f=with_f, tm=tm, nct=nct,
                          scale_a=float((A_NOPE + A_ROPE) ** -0.5 * LOG2E)),
        out_shape=out_shape, grid=(n_tiles,), in_specs=in_specs, out_specs=out_specs,
        compiler_params=_cparams(("parallel",)),
        name="proj_in",
    )(*args)
    outs = list(outs)
    x2 = outs.pop(0) if with_f else x
    names = [n for n, _ in widths]
    return x2, dict(zip(names, outs))


_NT = (((1,), (1,)), ((), ()))


def _head_plan(kind):
    if kind == "A":
        return tuple((r, None, r, r // 2) for r in range(N_HEADS))
    if kind == "C":
        return tuple((r // 2, r % 2, r // 2, r // 2) for r in range(N_HEADS))
    return tuple((r // 2, r % 2, 0, 0) for r in range(N_HEADS))


def _head_query(q_ref, plan_r):
    qt, half, _, _ = plan_r
    src = q_ref[:, qt * LANES:(qt + 1) * LANES]
    if half is None:
        return src
    lane = _lane_iota(src.shape)
    keep = (lane < HEAD_DIM) if half == 0 else (lane >= HEAD_DIM)
    return jnp.where(keep, src, jnp.zeros_like(src))


def _score_chunks(q, k_blocks, bias_blocks):
    chunks = []
    for kb, bb in zip(k_blocks, bias_blocks):
        s = lax.dot_general(q, kb, _NT, preferred_element_type=F32)
        if bb is not None:
            s = s + bb
        chunks += [s[:, c * LANES:(c + 1) * LANES] for c in range(s.shape[1] // LANES)]
    return chunks


def _row_max(chunks):
    m = functools.reduce(jnp.maximum, chunks)
    return jnp.broadcast_to(jnp.max(m, axis=-1, keepdims=True), m.shape)


def _weighted_values(p_chunks, v_blocks, half):
    pv, idx = None, 0
    for vb in v_blocks:
        n = vb.shape[0] // LANES
        p = jnp.concatenate(p_chunks[idx:idx + n], axis=1).astype(BF16)
        idx += n
        lane = _lane_iota(vb.shape)
        own = (lane < HEAD_DIM) if half == 0 else (lane >= HEAD_DIM)
        d = jnp.dot(p, jnp.where(own, vb, jnp.ones_like(vb)), preferred_element_type=F32)
        pv = d if pv is None else pv + d
    return pv


def _softmax_once(s, v_aug, bias_blocks, sink):
    chunks = []
    for c in range(s.shape[1] // LANES):
        sc = s[:, c * LANES:(c + 1) * LANES]
        bb = bias_blocks[c * LANES // ROW_BLK]
        if bb is not None:
            lo = c * LANES % ROW_BLK
            sc = sc + bb[:, lo:lo + LANES]
        chunks.append(sc)
    m = _row_max(chunks)
    if sink is not None:
        m = jnp.maximum(m, sink)
    p = jnp.concatenate([jnp.exp2(c - m) for c in chunks], axis=1).astype(BF16)
    pv = jnp.dot(p, v_aug, preferred_element_type=F32)
    l = pltpu.roll(pv, HEAD_DIM, 1)
    if sink is not None:
        l = l + jnp.exp2(sink - m)
    return pv / l


def _store_heads(o_ref, outs):
    lane = _lane_iota(outs[0].shape)
    for g in range(2):
        o_ref[:, g * LANES:(g + 1) * LANES] = jnp.where(
            lane < HEAD_DIM, outs[2 * g], outs[2 * g + 1]).astype(o_ref.dtype)


def _attn_band_kernel(*refs, kind, lb, n_kv, tq):
    it = iter(refs)
    q_ref = next(it)
    k_refs = [next(it) for _ in range(n_kv)]
    v_refs = [next(it) for _ in range(n_kv)]
    bias_ref = next(it) if n_kv > 1 else None
    sink_ref = next(it) if kind == "B" else None
    o_ref = next(it)
    t = pl.program_id(0) % lb
    var = jnp.where(t == 0, 0, jnp.where(t == lb - 1, 2, 1))
    plan = _head_plan(kind)
    scores = {}
    for kt in sorted({p[2] for p in plan}):
        heads = [r for r in range(N_HEADS) if plan[r][2] == kt]
        k_all = jnp.concatenate([k[:, kt * LANES:(kt + 1) * LANES] for k in k_refs], axis=0)
        q_all = jnp.concatenate([_head_query(q_ref, plan[r]) for r in heads], axis=0)
        s_all = lax.dot_general(q_all, k_all, _NT, preferred_element_type=F32)
        for j, r in enumerate(heads):
            scores[r] = s_all[j * tq:(j + 1) * tq, :]
    outs = []
    v_aug = {}
    for r, plan_r in enumerate(plan):
        _, _, kt, vt = plan_r
        if (vt, r % 2) not in v_aug:
            vv = jnp.concatenate([v[:, vt * LANES:(vt + 1) * LANES] for v in v_refs], axis=0)
            lane = _lane_iota(vv.shape)
            own = (lane < HEAD_DIM) if r % 2 == 0 else (lane >= HEAD_DIM)
            v_aug[(vt, r % 2)] = jnp.where(own, vv, jnp.ones_like(vv))
        bias = [None]
        for j in range(n_kv - 1):
            if kind == "C":
                bias.append(bias_ref[var, j, r * tq:(r + 1) * tq, :])
            else:
                bias.append(bias_ref[var, j])
        sink = sink_ref[_GQA_PERM[r]] * LOG2E if kind == "B" else None
        outs.append(_softmax_once(scores[r], v_aug[(vt, r % 2)], bias, sink))
    _store_heads(o_ref, outs)


def _attn_dense_kernel(q_ref, kc_ref, vc_ref, kl_ref, vl_ref, o_ref, qs_ref, m_ref, acc_ref,
                       *, kind, n_steps, tq):
    s = pl.program_id(1)
    plan = _head_plan(kind)

    def update(first):
        if first:
            for r in range(N_HEADS):
                qs_ref[r * tq:(r + 1) * tq, :] = _head_query(q_ref, plan[r])
        scores = {}
        for kt in sorted({p[2] for p in plan}):
            heads = [r for r in range(N_HEADS) if plan[r][2] == kt]
            ksl = slice(kt * LANES, (kt + 1) * LANES)
            k_blocks = [kc_ref[:, ksl], kl_ref[:, ksl]] if first else [kl_ref[:, ksl]]
            q_all = qs_ref[heads[0] * tq:(heads[-1] + 1) * tq, :]
            chunks = _score_chunks(q_all, k_blocks, [None] * len(k_blocks))
            for j, r in enumerate(heads):
                scores[r] = [c[j * tq:(j + 1) * tq, :] for c in chunks]
        for r, (_, _, kt, vt) in enumerate(plan):
            rows = slice(r * tq, (r + 1) * tq)
            vsl = slice(vt * LANES, (vt + 1) * LANES)
            v_blocks = [vc_ref[:, vsl], vl_ref[:, vsl]] if first else [vl_ref[:, vsl]]
            chunks = scores[r]
            m_cur = _row_max(chunks)
            if first:
                m_new = m_cur
            else:
                m_prev = m_ref[rows, :]
                m_new = jnp.maximum(m_prev, m_cur)
                alpha = jnp.exp2(m_prev - m_new)
            p = [jnp.exp2(c - m_new) for c in chunks]
            pv = _weighted_values(p, v_blocks, r % 2)
            if first:
                acc_ref[rows, :] = pv
            else:
                acc_ref[rows, :] = alpha * acc_ref[rows, :] + pv
            m_ref[rows, :] = m_new

    @pl.when(s == 0)
    def _():
        update(True)

    @pl.when(s > 0)
    def _():
        update(False)

    @pl.when(s == n_steps - 1)
    def _():
        outs = []
        for r in range(N_HEADS):
            rows = slice(r * tq, (r + 1) * tq)
            acc = acc_ref[rows, :]
            outs.append(acc / pltpu.roll(acc, HEAD_DIM, 1))
        _store_heads(o_ref, outs)


def _attn_latent_call(lay, kind, q, k, v, *, sink=None, bias=None):
    T, B, ncb, lb, S = lay["T"], lay["B"], lay["ncb"], lay["lb"], lay["S"]
    tq = ROW_BLK
    qw, kw, vw = q.shape[1], k.shape[1], v.shape[1]
    out_shape = jax.ShapeDtypeStruct((B * S, BRANCH_W), BF16)
    if kind in ("A", "D"):
        tk, tq = lay["tk_dense"], lay["tq_dense"]
        n_steps = S // tk
        lat0 = ncb * ROW_BLK // tk
        q0 = ncb * ROW_BLK // tq
        qpb = S // tq
        lat_blk = lambda i, s: (lat0 + (i // qpb) * n_steps + s, 0)
        return pl.pallas_call(
            functools.partial(_attn_dense_kernel, kind=kind, n_steps=n_steps, tq=tq),
            out_shape=out_shape,
            grid=(B * qpb, n_steps),
            in_specs=[pl.BlockSpec((tq, qw), lambda i, s: (q0 + i, 0)),
                      pl.BlockSpec((ROW_BLK, kw), lambda i, s: (i // qpb, 0)),
                      pl.BlockSpec((ROW_BLK, vw), lambda i, s: (i // qpb, 0)),
                      pl.BlockSpec((tk, kw), lat_blk),
                      pl.BlockSpec((tk, vw), lat_blk)],
            out_specs=pl.BlockSpec((tq, BRANCH_W), lambda i, s: (i, 0)),
            scratch_shapes=[pltpu.VMEM((N_HEADS * tq, LANES), BF16),
                            pltpu.VMEM((N_HEADS * tq, LANES), F32),
                            pltpu.VMEM((N_HEADS * tq, LANES), F32)],
            compiler_params=_cparams(("parallel", "arbitrary")),
            name="attn_" + kind,
        )(q, k, v, k, v)

    tq = lay["tq_band"]
    qpb = S // tq
    q0 = ncb * ROW_BLK // tq
    bpq = tq // ROW_BLK

    def nb(i, d):
        return (ncb + (i // qpb) * lb + jnp.clip((i % qpb) * bpq + d, 0, lb - 1), 0)

    kv_maps = [lambda i: (i // qpb, 0)] + [functools.partial(nb, d=d) for d in _band_offsets(tq)]
    n_kv = len(kv_maps)
    in_specs = [pl.BlockSpec((tq, qw), lambda i: (q0 + i, 0))]
    in_specs += [pl.BlockSpec((ROW_BLK, kw), m) for m in kv_maps]
    in_specs += [pl.BlockSpec((ROW_BLK, vw), m) for m in kv_maps]
    in_specs.append(pl.BlockSpec(bias.shape, lambda i: (0,) * bias.ndim, pipeline_mode=pl.Buffered(1)))
    args = [q] + [k] * n_kv + [v] * n_kv + [bias]
    if kind == "B":
        in_specs.append(pl.BlockSpec(memory_space=pltpu.SMEM))
        args.append(sink)
    return pl.pallas_call(
        functools.partial(_attn_band_kernel, kind=kind, lb=qpb, n_kv=n_kv, tq=tq),
        out_shape=out_shape,
        grid=(B * qpb,),
        in_specs=in_specs,
        out_specs=pl.BlockSpec((tq, BRANCH_W), lambda i: (i, 0)),
        compiler_params=_cparams(("parallel",)),
        name="attn_" + kind,
    )(*args)


def _attn_context_call(lay, kind, q, k, v, *, sink=None):
    ncb = lay["ncb"]
    qw, kw, vw = q.shape[1], k.shape[1], v.shape[1]
    blk = lambda i: (i, 0)
    in_specs = [pl.BlockSpec((ROW_BLK, qw), blk), pl.BlockSpec((ROW_BLK, kw), blk),
                pl.BlockSpec((ROW_BLK, vw), blk)]
    args = [q, k, v]
    if kind == "B":
        in_specs.append(pl.BlockSpec(memory_space=pltpu.SMEM))
        args.append(sink)
    return pl.pallas_call(
        functools.partial(_attn_band_kernel, kind=kind, lb=1, n_kv=1, tq=ROW_BLK),
        out_shape=jax.ShapeDtypeStruct((ncb * ROW_BLK, BRANCH_W), BF16),
        grid=(ncb,),
        in_specs=in_specs,
        out_specs=pl.BlockSpec((ROW_BLK, BRANCH_W), blk),
        compiler_params=_cparams(("parallel",)),
        name="attn_ctx_" + kind,
    )(*args)


def _band_offsets(tq):
    return tuple(range(-1, tq // ROW_BLK + 1))


def _window_bias(S, tq):
    lb, qpb, bpq = S // ROW_BLK, S // tq, tq // ROW_BLK
    offs = _band_offsets(tq)
    qa, ka = np.arange(tq), np.arange(ROW_BLK)
    out = np.full((3, len(offs), tq, ROW_BLK), NEG_INF, np.float32)
    for vi, t_rep in enumerate((0, 1, qpb - 1)):
        for di, d in enumerate(offs):
            kt = t_rep * bpq + d
            if not 0 <= kt < lb:
                continue
            qpos = t_rep * tq + qa
            kpos = kt * ROW_BLK + ka
            ok = np.abs(qpos[:, None] - kpos[None, :]) <= WINDOW
            out[vi, di] = np.where(ok, 0.0, NEG_INF)
    return jnp.asarray(out)


def _neighbourhood_bias(rpb, rows_total, tq):
    lb = rows_total * GRID_W // ROW_BLK
    qpb, bpq = rows_total * GRID_W // tq, tq // ROW_BLK
    rpt = ROW_BLK // GRID_W
    rpq = tq // GRID_W
    kh = min(NA_KH, rows_total)
    qa, ka = np.arange(tq), np.arange(ROW_BLK)
    q_sub, q_col = qa // GRID_W, qa % GRID_W
    k_sub, k_col = ka // GRID_W, ka % GRID_W
    n_dr, n_dc = 2 * NA_KH - 1, 2 * NA_KW - 1
    col = np.arange(GRID_W)
    dc = np.clip(col[None, :] - col[:, None], -(NA_KW - 1), NA_KW - 1) + NA_KW - 1
    hot_c = (dc[:, :, None] == np.arange(n_dc)).astype(np.float32)
    by_col = jnp.einsum("huv,cdv->hucd", rpb.astype(F32) * LOG2E, jnp.asarray(hot_c),
                        precision=HIGHEST)
    offs = _band_offsets(tq)
    vals = []
    for d in offs:
        dr = np.clip(d * rpt + np.arange(rpt)[None, :] - np.arange(rpq)[:, None],
                     -(NA_KH - 1), NA_KH - 1) + NA_KH - 1
        hot_r = (dr[:, :, None] == np.arange(n_dr)).astype(np.float32)
        v = jnp.einsum("abu,hucd->hacbd", jnp.asarray(hot_r), by_col, precision=HIGHEST)
        vals.append(v.reshape(N_HEADS, tq, ROW_BLK))
    out = []
    for t_rep in (0, 1, qpb - 1):
        per_block = []
        for di, d in enumerate(offs):
            kt = t_rep * bpq + d
            q_row = t_rep * rpq + q_sub
            k_row = kt * rpt + k_sub
            r_start = np.clip(q_row - kh // 2, 0, rows_total - kh)
            row_ok = (k_row[None] >= r_start[:, None]) & (k_row[None] < r_start[:, None] + kh)
            c_start = np.clip(q_col - NA_KW // 2, 0, GRID_W - NA_KW)
            col_ok = (k_col[None] >= c_start[:, None]) & (k_col[None] < c_start[:, None] + NA_KW)
            ok = row_ok & col_ok & (0 <= kt < lb)
            per_block.append(jnp.where(jnp.asarray(ok)[None], vals[di], NEG_INF).reshape(-1, ROW_BLK))
        out.append(jnp.stack(per_block))
    return jnp.stack(out)


def _merge_kernel(*refs, tm, split_x, n_ctx_tiles):
    it = iter(refs)
    h_ref = next(it)
    is_ctx = pl.program_id(0) < n_ctx_tiles
    if n_ctx_tiles > 0:
        o_refs = [(next(it), next(it)) for _ in range(4)]
        o_vals = [_tile_rows(oc, ol, is_ctx) for oc, ol in o_refs]
    else:
        o_vals = [next(it)[...] for _ in range(4)]
    if split_x:
        xc_ref, xl_ref = next(it), next(it)
    else:
        x_ref = next(it)
    (mod_ref, gffn_ref, wg_ref, bg_ref, wb_ref, wout_ref, wr_ref, br_ref,
     x1_ref, h2_ref, route_ref, y_ref) = (next(it) for _ in range(12))
    hb = h_ref[...]
    d_model = hb.shape[1]
    for t in range(d_model // MERGE_TN):
        cs = slice(t * MERGE_TN, (t + 1) * MERGE_TN)
        y = None
        for n, o_n in enumerate(o_vals):
            gate = _sigmoid(jnp.dot(hb, wg_ref[n, :, cs], preferred_element_type=F32) + bg_ref[n, :, cs])
            u = gate * jnp.dot(o_n, wb_ref[n, :, cs], preferred_element_type=F32)
            y = u if y is None else y + u
        y_ref[:, cs] = y.astype(BF16)
    z = jnp.dot(y_ref[...], wout_ref[...], preferred_element_type=F32)
    x_in = _tile_rows(xc_ref, xl_ref, is_ctx) if split_x else x_ref[...]
    x1 = x_in + mod_ref[0, 2:3, :] * z
    x1_ref[...] = x1
    h2 = _rms(x1) * gffn_ref[...] * (1.0 + mod_ref[0, 4:5, :]) + mod_ref[0, 3:4, :]
    for c in range(SUBLANES):
        h2_ref[pl.ds(c, tm, stride=SUBLANES), :] = h2[:, c * LANES:(c + 1) * LANES]

    h2_hi = h2.astype(BF16)
    h2_lo = (h2 - h2_hi.astype(F32)).astype(BF16)
    logit = (jnp.dot(h2_hi, wr_ref[0], preferred_element_type=F32)
             + jnp.dot(h2_lo, wr_ref[0], preferred_element_type=F32)
             + jnp.dot(h2_hi, wr_ref[1], preferred_element_type=F32)) + br_ref[...]
    lane = _lane_iota(logit.shape)
    big = jnp.int32(1 << 20)
    is_g = (lane >= N_EXPERTS) & (lane < N_EXPERTS + N_GROUPS)
    gl = jnp.where(is_g, logit, NEG_INF)
    gmax = jnp.max(gl, axis=-1, keepdims=True)
    gsel = jnp.min(jnp.where(gl == gmax, lane - N_EXPERTS, big), axis=-1, keepdims=True)
    gw = 1.0 / jnp.sum(jnp.where(is_g, jnp.exp(gl - gmax), 0.0), axis=-1, keepdims=True)
    in_grp = (lane < N_EXPERTS) & ((lane // EXPERTS_PER_GROUP) == gsel)
    el = jnp.where(in_grp, logit, NEG_INF)
    v1 = jnp.max(el, axis=-1, keepdims=True)
    i1 = jnp.min(jnp.where(el == v1, lane, big), axis=-1, keepdims=True)
    el2 = jnp.where(lane == i1, NEG_INF, el)
    v2 = jnp.max(el2, axis=-1, keepdims=True)
    i2 = jnp.min(jnp.where(el2 == v2, lane, big), axis=-1, keepdims=True)
    e21 = jnp.exp(v2 - v1)
    w1 = gw / (1.0 + e21)
    w2 = gw * e21 / (1.0 + e21)
    route_ref[...] = jnp.where(lane == 0, i1.astype(F32),
                               jnp.where(lane == 1, i2.astype(F32),
                                         jnp.where(lane == 2, w1, jnp.where(lane == 3, w2, 0.0))))


def _merge_call(lay, layer, h, o, x, mod, w, *, with_ctx):
    T, D, tm = lay["T"], lay["D"], lay["tm"]
    nct, tpb, B = lay["nct"], lay["tpb"], lay["B"]
    off = 0 if with_ctx else nct
    n_tiles = T // tm - off

    def mod_row(i):
        return jnp.where(i < nct, B, (i - nct) // tpb)

    row = lambda i: (i + off, 0)
    const2 = lambda i: (0, 0)
    const3 = lambda i: (0, 0, 0)
    split_x = isinstance(x, (tuple, list))
    in_specs = [pl.BlockSpec((tm, D), row)]
    o_args = []
    for kind in ("A", "B", "C", "D"):
        o_ctx, o_lat = o[kind]
        if with_ctx:
            in_specs += _split_row_specs(lay, BRANCH_W)
            o_args += [o_ctx, o_lat]
        else:
            in_specs.append(pl.BlockSpec((tm, BRANCH_W), lambda i: (i, 0)))
            o_args.append(o_lat)
    in_specs += _split_row_specs(lay, D, off) if split_x else [pl.BlockSpec((tm, D), row)]
    in_specs += [pl.BlockSpec((1, 6, D), lambda i: (layer * lay["mod_rows"] + mod_row(i + off), 0, 0)),
                 pl.BlockSpec((1, D), const2),
                 pl.BlockSpec(w["w_gate"].shape, const3),
                 pl.BlockSpec(w["b_gate"].shape, const3),
                 pl.BlockSpec(w["w_branch"].shape, const3),
                 pl.BlockSpec((D, D), const2),
                 pl.BlockSpec((2, D, LANES), const3),
                 pl.BlockSpec((1, LANES), const2)]
    out_shape = [jax.ShapeDtypeStruct((T, D), F32),
                 jax.ShapeDtypeStruct((T * SUBLANES, LANES), F32),
                 jax.ShapeDtypeStruct((T, LANES), F32)]
    out_specs = [pl.BlockSpec((tm, D), row),
                 pl.BlockSpec((tm * SUBLANES, LANES), row),
                 pl.BlockSpec((tm, LANES), row)]
    return pl.pallas_call(
        functools.partial(_merge_kernel, tm=tm, split_x=split_x, n_ctx_tiles=nct - off),
        out_shape=out_shape, grid=(n_tiles,), in_specs=in_specs, out_specs=out_specs,
        scratch_shapes=[pltpu.VMEM((tm, D), BF16)],
        compiler_params=_cparams(("parallel",)),
        name="merge",
    )(h, *o_args, *(x if split_x else (x,)), mod, w["g_ffn"], w["w_gate"],
      w["b_gate"], w["w_branch"], w["w_out"], w["w_route"], w["b_route"])


def _moe_kernel(tab_ref, tok_ref, h2_ref, sw_ref, w1_ref, w3_ref, w2_ref, f_ref,
                xg_ref, y_ref, st_ref, *, blk, slots, tile_off):
    ti = pl.program_id(0)
    step = pl.program_id(1)

    @pl.when(step == 0)
    def _():
        f_ref[...] = jnp.zeros_like(f_ref)
        xg_ref[...] = jnp.zeros_like(xg_ref)

    tok_base = (ti + tile_off) * slots
    segs = []
    for j in range(MOE_EXPERTS_PER_STEP):
        base = ((ti + tile_off) * N_EXPERTS + step * MOE_EXPERTS_PER_STEP + j) * 2
        segs.append((tab_ref[base], tab_ref[base + 1]))

    def gather(j, off, n_rows):
        slot0 = tok_base + off

        def body(gi, c):
            r0 = gi * MOE_GROUP
            srcs = [pl.multiple_of(tok_ref[slot0 + r0 + u], SUBLANES) for u in range(MOE_GROUP)]
            for u in range(MOE_GROUP):
                dst = pl.multiple_of((r0 + u) * SUBLANES, SUBLANES)
                xg_ref[j, pl.ds(dst, SUBLANES), :] = h2_ref[pl.ds(srcs[u], SUBLANES), :]
            return c

        lax.fori_loop(0, n_rows // MOE_GROUP, body, 0)

    def ffn(j, off):
        xb = _load_token_major(xg_ref.at[j], blk).astype(BF16)
        a = jnp.dot(xb, w1_ref[j], preferred_element_type=F32)
        g = jnp.dot(xb, w3_ref[j], preferred_element_type=F32)
        hid = (a * _sigmoid(a) * g).astype(BF16)
        y = jnp.dot(hid, w2_ref[j], preferred_element_type=F32)
        y_ref[j] = y * sw_ref[0, pl.ds(off, blk), :]

    def scatter(j, off, n_rows):
        slot0 = tok_base + off

        def body(gi, c):
            r0 = pl.multiple_of(gi * MOE_GROUP, MOE_GROUP)
            dsts = [pl.multiple_of(tok_ref[slot0 + r0 + u], SUBLANES) for u in range(MOE_GROUP)]
            for ch in range(SUBLANES):
                st_ref[pl.ds(ch, MOE_GROUP, stride=SUBLANES), :] = y_ref[j, pl.ds(r0, MOE_GROUP),
                                                                         ch * LANES:(ch + 1) * LANES]
            vals = [f_ref[pl.ds(dsts[u], SUBLANES), :] + st_ref[u * SUBLANES:(u + 1) * SUBLANES, :]
                    for u in range(MOE_GROUP)]
            for u in reversed(range(MOE_GROUP)):
                f_ref[pl.ds(dsts[u], SUBLANES), :] = vals[u]
            return c

        lax.fori_loop(0, n_rows // MOE_GROUP, body, 0)

    for j, (seg0, n_pad) in enumerate(segs):
        gather(j, pl.multiple_of(seg0, MOE_GROUP), jnp.minimum(n_pad, blk))
    for j, (seg0, _) in enumerate(segs):
        ffn(j, pl.multiple_of(seg0, MOE_GROUP))
    for j, (seg0, n_pad) in enumerate(segs):
        scatter(j, pl.multiple_of(seg0, MOE_GROUP), jnp.minimum(n_pad, blk))

    for j, (seg0, n_pad) in enumerate(segs):
        def extra_block(b, carry, j=j, seg0=seg0, n_pad=n_pad):
            off = pl.multiple_of(seg0 + b * blk, MOE_GROUP)
            n_rows = jnp.minimum(n_pad - b * blk, blk)
            gather(j, off, n_rows)
            ffn(j, off)
            scatter(j, off, n_rows)
            return carry

        lax.fori_loop(1, (n_pad + blk - 1) // blk, extra_block, 0)


def _moe_call(lay, route, h2, w, *, with_ctx):
    T, D, tt = lay["T"], lay["D"], lay["tt"]
    blk = lay["moe_blk"]
    n_tiles_all = T // tt
    tile_off = 0 if with_ctx else lay["ncb"] * ROW_BLK // tt
    n_tiles = n_tiles_all - tile_off
    n_assign = 2 * tt
    n_fill = N_EXPERTS * MOE_GROUP
    slots = n_assign + n_fill + blk

    eid = route[:, 0:2].astype(jnp.int32).reshape(n_tiles_all, n_assign)
    wts = route[:, 2:4].reshape(n_tiles_all, n_assign)
    tok = jnp.broadcast_to(jnp.arange(n_assign, dtype=jnp.int32)[None, :] // 2 * SUBLANES, eid.shape)
    ex = jnp.arange(N_EXPERTS, dtype=jnp.int32)
    counts = jnp.sum(eid[:, :, None] == ex[None, None, :], axis=1, dtype=jnp.int32)
    n_dummy = (-counts) % MOE_GROUP
    fill_key = jnp.where(jnp.arange(MOE_GROUP, dtype=jnp.int32)[None, None, :] < n_dummy[:, :, None],
                         ex[None, :, None], N_EXPERTS).reshape(n_tiles_all, n_fill)
    zeros_i = jnp.zeros((n_tiles_all, n_fill), jnp.int32)
    _, slot_tok, slot_w = lax.sort(
        (jnp.concatenate([eid, fill_key], axis=1), jnp.concatenate([tok, zeros_i], axis=1),
         jnp.concatenate([wts, zeros_i.astype(F32)], axis=1)),
        dimension=1, is_stable=True, num_keys=1)
    slot_tok = jnp.pad(slot_tok, ((0, 0), (0, blk)))
    slot_w = jnp.pad(slot_w, ((0, 0), (0, blk)))
    padded = counts + n_dummy
    seg0 = jnp.cumsum(padded, axis=1) - padded
    tab = jnp.stack([seg0, padded], axis=-1).reshape(-1).astype(jnp.int32)

    return pl.pallas_call(
        functools.partial(_moe_kernel, blk=blk, slots=slots, tile_off=tile_off),
        out_shape=jax.ShapeDtypeStruct((T * SUBLANES, LANES), F32),
        grid_spec=pltpu.PrefetchScalarGridSpec(
            num_scalar_prefetch=2,
            grid=(n_tiles, N_EXPERTS // MOE_EXPERTS_PER_STEP),
            in_specs=[
                pl.BlockSpec((tt * SUBLANES, LANES), lambda t, e, *_: (t + tile_off, 0)),
                pl.BlockSpec((1, slots, 1), lambda t, e, *_: (t + tile_off, 0, 0)),
                pl.BlockSpec((MOE_EXPERTS_PER_STEP, D, EXPERT_FF), lambda t, e, *_: (e, 0, 0)),
                pl.BlockSpec((MOE_EXPERTS_PER_STEP, D, EXPERT_FF), lambda t, e, *_: (e, 0, 0)),
                pl.BlockSpec((MOE_EXPERTS_PER_STEP, EXPERT_FF, D), lambda t, e, *_: (e, 0, 0)),
            ],
            out_specs=pl.BlockSpec((tt * SUBLANES, LANES), lambda t, e, *_: (t + tile_off, 0)),
            scratch_shapes=[pltpu.VMEM((MOE_EXPERTS_PER_STEP, blk * SUBLANES, LANES), F32),
                            pltpu.VMEM((MOE_EXPERTS_PER_STEP, blk, D), F32),
                            pltpu.VMEM((MOE_GROUP * SUBLANES, LANES), F32)],
        ),
        compiler_params=_cparams(("parallel", "arbitrary")),
        name="moe_experts",
    )(tab, slot_tok.reshape(-1), h2, slot_w.reshape(n_tiles_all, slots, 1),
      w["w_ff1"], w["w_ff3"], w["w_ff2"])


def _final_kernel(x_ref, f_ref, mod_ref, g_ref, o_ref, *, tm):
    xf = x_ref[...] + mod_ref[0, 5:6, :] * _load_token_major(f_ref, tm)
    o_ref[...] = _rms(xf) * g_ref[...]


def _final_call(lay, layer, x1, f, mod, g_final):
    T, D, tm = lay["T"], lay["D"], lay["tm"]
    nct, tpb = lay["nct"], lay["tpb"]
    n_lat = T // tm - nct
    return pl.pallas_call(
        functools.partial(_final_kernel, tm=tm),
        out_shape=jax.ShapeDtypeStruct((n_lat * tm, D), F32),
        grid=(n_lat,),
        in_specs=[pl.BlockSpec((tm, D), lambda i: (i + nct, 0)),
                  pl.BlockSpec((tm * SUBLANES, LANES), lambda i: (i + nct, 0)),
                  pl.BlockSpec((1, 6, D), lambda i: (layer * lay["mod_rows"] + i // tpb, 0, 0)),
                  pl.BlockSpec((1, D), lambda i: (0, 0))],
        out_specs=pl.BlockSpec((tm, D), lambda i: (i, 0)),
        compiler_params=_cparams(("parallel",)),
        name="final_norm",
    )(x1, f, mod, g_final)


def _select_cols(wm, segs, scale=None):
    parts = []
    for k, (start, width) in enumerate(segs):
        if start is None:
            parts.append(jnp.zeros((wm.shape[0], width), wm.dtype))
        else:
            blk = wm[:, start:start + width]
            parts.append(blk if scale is None or scale[k] is None else blk * scale[k])
    return jnp.concatenate(parts, axis=1)


def _prep_layer(l, p):
    a_cols = A_Q_RANK + A_KV_RANK + A_ROPE
    b_off = a_cols
    c_off = b_off + 512
    d_off = c_off + 768
    qk_scale = HEAD_DIM ** -0.5 * LOG2E
    gqa_q = lambda off: [(off + hh * HEAD_DIM, HEAD_DIM) for hh in _GQA_PERM]
    segs = ([(0, 256), (256, 128), (None, 64), (384, 32), (None, 32)]
            + gqa_q(b_off) + [(b_off + 256, 128), (b_off + 384, 128)]
            + [(c_off, 256), (c_off + 256, 256), (c_off + 512, 256)]
            + gqa_q(d_off) + [(d_off + 256, 128), (d_off + 384, 128)])
    scale = [None] * len(segs)
    for k in (5, 6, 7, 8, 11):
        scale[k] = qk_scale
    w_in = _select_cols(p["w_in"][l], segs, scale).astype(BF16)
    assert w_in.shape[1] == PROJ_COLS

    hq = A_NOPE + A_ROPE
    segs_q = []
    for hh in range(N_HEADS):
        segs_q += [(hh * hq, hq), (None, LANES - hq)]
    w_q_b = _select_cols(p["w_q_b"][l], segs_q).astype(BF16)
    hk = A_NOPE + HEAD_DIM
    segs_k = []
    for hh in range(N_HEADS):
        segs_k += [(hh * hk, A_NOPE), (None, LANES - A_NOPE)]
    segs_k += [(hh * hk + A_NOPE, HEAD_DIM) for hh in range(N_HEADS)]
    w_kv_b = _select_cols(p["w_kv_b"][l], segs_k).astype(BF16)

    wb = p["w_branch"][l]
    perm_rows = lambda m: jnp.concatenate([m[hh * HEAD_DIM:(hh + 1) * HEAD_DIM] for hh in _GQA_PERM], axis=0)
    w_branch = jnp.stack([wb[0], perm_rows(wb[1]), wb[2], perm_rows(wb[3])]).astype(BF16)

    d = p["w_in"].shape[1]
    w_route = jnp.zeros((d, LANES), F32)
    w_route = w_route.at[:, :N_EXPERTS].set(p["w_router"][l]).at[:, N_EXPERTS:N_EXPERTS + N_GROUPS].set(p["w_group"][l])
    b_route = jnp.zeros((1, LANES), F32)
    b_route = b_route.at[0, :N_EXPERTS].set(p["b_router"][l]).at[0, N_EXPERTS:N_EXPERTS + N_GROUPS].set(p["b_group"][l])
    return {
        "g_mix": p["g_norm_mix"][l][None, :],
        "w_in": w_in,
        "g_q_a": p["g_q_a"][l][None, :],
        "w_q_b": w_q_b,
        "g_kv_a": p["g_kv_a"][l][None, :],
        "w_kv_b": w_kv_b,
        "g_q_d": (jnp.tile(p["g_q_d"][l], 2) * qk_scale)[None, :],
        "g_k_d": jnp.tile(p["g_k_d"][l], 2)[None, :],
        "sink": p["sink_b"][l],
        "rpb": p["rpb_c"][l],
        "w_gate": p["w_gate"][l].astype(BF16),
        "b_gate": p["b_gate"][l][:, None, :],
        "w_branch": w_branch,
        "w_out": p["w_out"][l].astype(BF16),
        "g_ffn": p["g_norm_ffn"][l][None, :],
        "w_route": jnp.stack([w_route.astype(BF16),
                              (w_route - w_route.astype(BF16).astype(F32)).astype(BF16)]),
        "b_route": b_route,
        "w_ff1": p["w_ff1"][l].astype(BF16),
        "w_ff3": p["w_ff3"][l].astype(BF16),
        "w_ff2": p["w_ff2"][l].astype(BF16),
    }


def _rope_tables(S, tm):
    t = np.arange(S)
    rows = (t // GRID_W).astype(np.float32)
    cols = (t % GRID_W).astype(np.float32)

    def cs(rot):
        half = rot // 2
        inv = np.float32(ROPE_THETA) ** (-np.arange(0, half, 2, dtype=np.float32) / np.float32(half))
        ar_, ac_ = rows[:, None] * inv, cols[:, None] * inv
        cos = np.concatenate([np.cos(ar_), np.cos(ar_), np.cos(ac_), np.cos(ac_)], axis=-1)
        sin = np.concatenate([-np.sin(ar_), np.sin(ar_), -np.sin(ac_), np.sin(ac_)], axis=-1)
        return cos.astype(np.float32), sin.astype(np.float32)

    cos64, sin64 = cs(HEAD_DIM)
    cos32, sin32 = cs(A_ROPE)
    ones = lambda n: np.ones((S, n), np.float32)
    zeros = lambda n: np.zeros((S, n), np.float32)
    tabs = {
        "cos_h": np.concatenate([cos64, cos64], axis=-1),
        "sin_h": np.concatenate([sin64, sin64], axis=-1),
        "cos_a": np.concatenate([ones(A_NOPE), cos32, ones(LANES - A_NOPE - A_ROPE)], axis=-1),
        "sin_a": np.concatenate([zeros(A_NOPE), sin32, zeros(LANES - A_NOPE - A_ROPE)], axis=-1),
    }
    ident = {"cos_h": 1.0, "sin_h": 0.0, "cos_a": 1.0, "sin_a": 0.0}
    return {k: jnp.asarray(np.concatenate([v, np.full((tm, LANES), ident[k], np.float32)], axis=0))
            for k, v in tabs.items()}


def _layout(B, S, n_ctx, D):
    assert n_ctx == ROW_BLK and S % 1024 == 0 and S // GRID_W >= 3 * (ROW_BLK // GRID_W)
    T = B * (n_ctx + S)
    tm = 512 if (B * n_ctx) % 512 == 0 else 256
    def largest_tile(cands):
        return next(t for t in cands if (B * n_ctx) % t == 0 and S % t == 0)

    tq_dense = largest_tile((1024, 512, 256))
    tk_dense = largest_tile((2048, 1024, 512, 256))
    tt = 2048 if (B * n_ctx) % 2048 == 0 else B * n_ctx
    assert S % tt == 0
    return {
        "B": B, "S": S, "D": D, "T": T, "tm": tm,
        "ncb": B * n_ctx // ROW_BLK,
        "lb": S // ROW_BLK,
        "nct": B * n_ctx // tm,
        "tpb": S // tm,
        "tk_dense": tk_dense,
        "tq_dense": tq_dense,
        "tq_band": tm,
        "tt": tt, "moe_blk": 160,
        "mod_rows": 16,
    }


def kernel(x, c, ctx, c_ctx, w_mod, b_mod, g_norm_mix, w_in, g_q_a, w_q_b, g_kv_a, w_kv_b, sink_b, rpb_c,
           g_q_d, g_k_d, w_gate, b_gate, w_branch, w_out, g_norm_ffn, w_group, b_group, w_router, b_router,
           w_ff1, w_ff3, w_ff2, g_final):
    B, S, D = x.shape
    n_ctx = ctx.shape[1]
    depth = w_mod.shape[0]
    lay = _layout(B, S, n_ctx, D)
    params = dict(w_in=w_in, g_norm_mix=g_norm_mix, g_q_a=g_q_a, w_q_b=w_q_b, g_kv_a=g_kv_a, w_kv_b=w_kv_b,
                  sink_b=sink_b, rpb_c=rpb_c, g_q_d=g_q_d, g_k_d=g_k_d, w_gate=w_gate, b_gate=b_gate,
                  w_branch=w_branch, w_out=w_out, g_norm_ffn=g_norm_ffn, w_group=w_group, b_group=b_group,
                  w_router=w_router, b_router=b_router, w_ff1=w_ff1, w_ff3=w_ff3, w_ff2=w_ff2)

    c_all = jnp.zeros((lay["mod_rows"], D), F32).at[:B].set(c).at[B].set(c_ctx)
    mod = _modulation(c_all, w_mod, b_mod).reshape(depth * lay["mod_rows"], 6, D)
    tabs = _rope_tables(S, lay["tm"])
    win_bias = _window_bias(S, lay["tq_band"])
    xf = (ctx.reshape(B * n_ctx, D), x.reshape(B * S, D))

    f = None
    for l in range(depth):
        with_ctx = l < depth - 1
        w = _prep_layer(l, params)
        xf, pr = _proj_call(lay, xf, f, mod, mod, l, w, tabs)
        o = {}
        for kind, kl in (("A", "a"), ("B", "b"), ("C", "c"), ("D", "d")):
            q, k, v = pr["q" + kl], pr["k" + kl], pr["v" + kl]
            sink = w["sink"] if kind == "B" else None
            bias = win_bias if kind == "B" else None
            if kind == "C":
                bias = _neighbourhood_bias(w["rpb"], S // GRID_W, lay["tq_band"])
            o_ctx = _attn_context_call(lay, kind, q, k, v, sink=sink) if with_ctx else None
            o[kind] = (o_ctx, _attn_latent_call(lay, kind, q, k, v, sink=sink, bias=bias))
        xf, h2, route = _merge_call(lay, l, pr["h"], o, xf, mod, w, with_ctx=with_ctx)
        f = _moe_call(lay, route, h2, w, with_ctx=with_ctx)
    out = _final_call(lay, depth - 1, xf, f, mod, g_final[None, :])
    return out.reshape(B, S, D)
```

```python
import functools

import numpy as np
import jax
import jax.numpy as jnp
from jax import lax
from jax.experimental import pallas as pl
from jax.experimental.pallas import tpu as pltpu

F32 = jnp.float32
BF16 = jnp.bfloat16
HIGHEST = lax.Precision.HIGHEST

GRID_W = 64
ROPE_THETA = 10000.0
EPS = 1e-6
NEG_INF = -1e30
LOG2E = 1.4426950408889634
HEAD_DIM = 64
N_HEADS = 4
BRANCH_W = 256
A_Q_RANK = 256
A_KV_RANK = 128
A_NOPE = 64
A_ROPE = 32
NA_KH = 8
NA_KW = 16
WINDOW = 128
N_GROUPS = 4
EXPERTS_PER_GROUP = 8
N_EXPERTS = 32
EXPERT_FF = 256

LANES = 128
SUBLANES = 8
ROW_BLK = 256
MERGE_TN = 256
MOE_EXPERTS_PER_STEP = 4
MOE_GROUP = 16
VMEM_LIMIT = 56 * 1024 * 1024

_PROJ_GROUPS = (("cq", 256), ("ckv", 128), ("kr", 128), ("qb", 256), ("kb", 128), ("vb", 128),
                ("qc", 256), ("kc", 256), ("vc", 256), ("qd", 256), ("kd", 128), ("vd", 128))
_PROJ_OFF = {}
_o = 0
for _n, _w in _PROJ_GROUPS:
    _PROJ_OFF[_n] = (_o, _w)
    _o += _w
PROJ_COLS = _o
_GQA_PERM = (0, 2, 1, 3)


def _cparams(sem):
    return pltpu.CompilerParams(dimension_semantics=sem, vmem_limit_bytes=VMEM_LIMIT)


def _lane_iota(shape):
    return lax.broadcasted_iota(jnp.int32, shape, len(shape) - 1)


def _sigmoid(x):
    return 1.0 / (1.0 + jnp.exp(-x))


def _mod_kernel(c_ref, w_ref, b_ref, o_ref):
    cf = c_ref[...]
    s = cf * _sigmoid(cf)
    o_ref[0] = jnp.dot(s, w_ref[0], precision=HIGHEST, preferred_element_type=F32) + b_ref[0]


def _modulation(c_all, w_mod, b_mod):
    n_layers, d, n_out = w_mod.shape
    rows = c_all.shape[0]
    tn = n_out // 2
    return pl.pallas_call(
        _mod_kernel,
        out_shape=jax.ShapeDtypeStruct((n_layers, rows, n_out), F32),
        grid=(n_layers, n_out // tn),
        in_specs=[pl.BlockSpec((rows, d), lambda l, j: (0, 0)),
                  pl.BlockSpec((1, d, tn), lambda l, j: (l, 0, j)),
                  pl.BlockSpec((1, 1, tn), lambda l, j: (l, 0, j))],
        out_specs=pl.BlockSpec((1, rows, tn), lambda l, j: (l, 0, j)),
        compiler_params=_cparams(("arbitrary", "arbitrary")),
        name="modulation",
    )(c_all, w_mod, b_mod.reshape(n_layers, 1, n_out))


def _rms(x):
    return x * lax.rsqrt(jnp.mean(x * x, axis=-1, keepdims=True) + EPS)


def _swap_blocks(x, blk):
    lane = _lane_iota(x.shape)
    up = pltpu.roll(x, LANES - blk, 1)
    dn = pltpu.roll(x, blk, 1)
    return jnp.where((lane // blk) % 2 == 0, up, dn)


def _rope(x, cos, sin, blk):
    return x * cos + _swap_blocks(x, blk) * sin


def _pair_norm(x, g):
    lo = _lane_iota(x.shape) < HEAD_DIM
    sq = x * x
    s_lo = jnp.sum(jnp.where(lo, sq, 0.0), axis=-1, keepdims=True)
    s_hi = jnp.sum(jnp.where(lo, 0.0, sq), axis=-1, keepdims=True)
    ms = jnp.where(lo, s_lo, s_hi) * (1.0 / HEAD_DIM)
    return x * lax.rsqrt(ms + EPS) * g


def _load_token_major(ref, rows):
    return jnp.concatenate(
        [ref[pl.ds(c, rows, stride=SUBLANES), :] for c in range(SUBLANES)], axis=-1)


def _tile_rows(xc_ref, xl_ref, is_ctx):
    return jnp.where(is_ctx, xc_ref[...], xl_ref[...])


def _proj_kernel(*refs, with_f, tm, scale_a, nct):
    it = iter(refs)
    if with_f:
        x_ref = next(it)
        f_ref = next(it)
        modp_ref = next(it)
    else:
        xc_ref, xl_ref = next(it), next(it)
    mod_ref = next(it)
    gmix_ref, win_ref, gqa_ref, wqb_ref, gkva_ref, wkvb_ref, gqd_ref, gkd_ref = (next(it) for _ in range(8))
    cosh_ref, sinh_ref, cosa_ref, sina_ref = (next(it) for _ in range(4))
    if with_f:
        x2_ref = next(it)
    h_ref = next(it)
    qa_ref, ka_ref, va_ref, qb_ref, kb_ref, vb_ref, qc_ref, kc_ref, vc_ref, qd_ref, kd_ref, vd_ref = (
        next(it) for _ in range(12))

    if with_f:
        xf = x_ref[...] + modp_ref[0, 5:6, :] * _load_token_major(f_ref, tm)
        x2_ref[...] = xf
    else:
        xf = _tile_rows(xc_ref, xl_ref, pl.program_id(0) < nct)
    h = _rms(xf) * gmix_ref[...] * (1.0 + mod_ref[0, 1:2, :]) + mod_ref[0, 0:1, :]
    hb = h.astype(BF16)
    h_ref[...] = hb
    p = jnp.dot(hb, win_ref[...], preferred_element_type=F32)

    def grp(name):
        o, w = _PROJ_OFF[name]
        return p[:, o:o + w]

    cosh, sinh = cosh_ref[...], sinh_ref[...]
    cosa, sina = cosa_ref[...], sina_ref[...]

    cq = (_rms(grp("cq")) * gqa_ref[...]).astype(BF16)
    qa = jnp.dot(cq, wqb_ref[...], preferred_element_type=F32)
    for hd in range(N_HEADS):
        sl = slice(hd * LANES, (hd + 1) * LANES)
        qa_ref[:, sl] = (_rope(qa[:, sl], cosa, sina, 8) * scale_a).astype(BF16)
    ckv = (_rms(grp("ckv")) * gkva_ref[...]).astype(BF16)
    kva = jnp.dot(ckv, wkvb_ref[...], preferred_element_type=F32)
    kr = _rope(grp("kr"), cosa, sina, 8)
    for hd in range(N_HEADS):
        sl = slice(hd * LANES, (hd + 1) * LANES)
        ka_ref[:, sl] = (kva[:, sl] + kr).astype(BF16)
    va_ref[...] = kva[:, N_HEADS * LANES:].astype(BF16)

    qb = grp("qb")
    for j in range(2):
        sl = slice(j * LANES, (j + 1) * LANES)
        qb_ref[:, sl] = _rope(qb[:, sl], cosh, sinh, 16).astype(BF16)
    kb_ref[...] = _rope(grp("kb"), cosh, sinh, 16).astype(BF16)
    vb_ref[...] = grp("vb").astype(BF16)

    qc_ref[...] = grp("qc").astype(BF16)
    kc_ref[...] = grp("kc").astype(BF16)
    vc_ref[...] = grp("vc").astype(BF16)

    qd = grp("qd")
    for j in range(2):
        sl = slice(j * LANES, (j + 1) * LANES)
        qd_ref[:, sl] = _rope(_pair_norm(qd[:, sl], gqd_ref[...]), cosh, sinh, 16).astype(BF16)
    kd_ref[...] = _rope(_pair_norm(grp("kd"), gkd_ref[...]), cosh, sinh, 16).astype(BF16)
    vd_ref[...] = grp("vd").astype(BF16)


def _split_row_specs(lay, width, off=0):
    tm, nct = lay["tm"], lay["nct"]
    return [pl.BlockSpec((tm, width), lambda i: (jnp.minimum(i + off, nct - 1), 0)),
            pl.BlockSpec((tm, width), lambda i: (jnp.maximum(i + off - nct, 0), 0))]


def _proj_call(lay, x, f, modp, mod, layer, w, tabs):
    T, D, tm = lay["T"], lay["D"], lay["tm"]
    nct, tpb, B = lay["nct"], lay["tpb"], lay["B"]
    with_f = f is not None
    n_tiles = T // tm

    def mod_row(i):
        return jnp.where(i < nct, B, (i - nct) // tpb)

    def tab_blk(i):
        return jnp.where(i < nct, tpb, (i - nct) % tpb)

    row = lambda i: (i, 0)
    const = lambda i: (0, 0)
    if with_f:
        in_specs = [pl.BlockSpec((tm, D), row),
                    pl.BlockSpec((tm * SUBLANES, LANES), row),
                    pl.BlockSpec((1, 6, D), lambda i: ((layer - 1) * lay["mod_rows"] + mod_row(i), 0, 0))]
        args = [x, f, modp]
    else:
        in_specs = _split_row_specs(lay, D)
        args = list(x)
    in_specs += [pl.BlockSpec((1, 6, D), lambda i: (layer * lay["mod_rows"] + mod_row(i), 0, 0))]
    args += [mod]
    for name in ("g_mix", "w_in", "g_q_a", "w_q_b", "g_kv_a", "w_kv_b", "g_q_d", "g_k_d"):
        a = w[name]
        in_specs.append(pl.BlockSpec(a.shape, const))
        args.append(a)
    for tname in ("cos_h", "sin_h", "cos_a", "sin_a"):
        in_specs.append(pl.BlockSpec((tm, LANES), lambda i: (tab_blk(i), 0)))
        args.append(tabs[tname])

    widths = [("h", D), ("qa", 512), ("ka", 512), ("va", 256), ("qb", 256), ("kb", 128), ("vb", 128),
              ("qc", 256), ("kc", 256), ("vc", 256), ("qd", 256), ("kd", 128), ("vd", 128)]
    out_shape, out_specs = [], []
    if with_f:
        out_shape.append(jax.ShapeDtypeStruct((T, D), F32))
        out_specs.append(pl.BlockSpec((tm, D), row))
    for _, wd in widths:
        out_shape.append(jax.ShapeDtypeStruct((T, wd), BF16))
        out_specs.append(pl.BlockSpec((tm, wd), row))

    outs = pl.pallas_call(
        functools.partial(_proj_kernel, with_f=with_f, tm=tm, nct=nct,
                          scale_a=float((A_NOPE + A_ROPE) ** -0.5 * LOG2E)),
        out_shape=out_shape, grid=(n_tiles,), in_specs=in_specs, out_specs=out_specs,
        compiler_params=_cparams(("parallel",)),
        name="proj_in",
    )(*args)
    outs = list(outs)
    x2 = outs.pop(0) if with_f else x
    names = [n for n, _ in widths]
    return x2, dict(zip(names, outs))


_NT = (((1,), (1,)), ((), ()))


def _head_plan(kind):
    if kind == "A":
        return tuple((r, None, r, r // 2) for r in range(N_HEADS))
    if kind == "C":
        return tuple((r // 2, r % 2, r // 2, r // 2) for r in range(N_HEADS))
    return tuple((r // 2, r % 2, 0, 0) for r in range(N_HEADS))


def _head_query(q_ref, plan_r):
    qt, half, _, _ = plan_r
    src = q_ref[:, qt * LANES:(qt + 1) * LANES]
    if half is None:
        return src
    lane = _lane_iota(src.shape)
    keep = (lane < HEAD_DIM) if half == 0 else (lane >= HEAD_DIM)
    return jnp.where(keep, src, jnp.zeros_like(src))


def _score_chunks(q, k_blocks, bias_blocks):
    chunks = []
    for kb, bb in zip(k_blocks, bias_blocks):
        s = lax.dot_general(q, kb, _NT, preferred_element_type=F32)
        if bb is not None:
            s = s + bb
        chunks += [s[:, c * LANES:(c + 1) * LANES] for c in range(s.shape[1] // LANES)]
    return chunks


def _row_max(chunks):
    m = functools.reduce(jnp.maximum, chunks)
    return jnp.broadcast_to(jnp.max(m, axis=-1, keepdims=True), m.shape)


def _weighted_values(p_chunks, v_blocks, half):
    pv, idx = None, 0
    for vb in v_blocks:
        n = vb.shape[0] // LANES
        p = jnp.concatenate(p_chunks[idx:idx + n], axis=1).astype(BF16)
        idx += n
        lane = _lane_iota(vb.shape)
        own = (lane < HEAD_DIM) if half == 0 else (lane >= HEAD_DIM)
        d = jnp.dot(p, jnp.where(own, vb, jnp.ones_like(vb)), preferred_element_type=F32)
        pv = d if pv is None else pv + d
    return pv


def _softmax_once(s, v_aug, bias_blocks, sink):
    chunks = []
    for c in range(s.shape[1] // LANES):
        sc = s[:, c * LANES:(c + 1) * LANES]
        bb = bias_blocks[c * LANES // ROW_BLK]
        if bb is not None:
            lo = c * LANES % ROW_BLK
            sc = sc + bb[:, lo:lo + LANES]
        chunks.append(sc)
    m = _row_max(chunks)
    if sink is not None:
        m = jnp.maximum(m, sink)
    p = jnp.concatenate([jnp.exp2(c - m) for c in chunks], axis=1).astype(BF16)
    pv = jnp.dot(p, v_aug, preferred_element_type=F32)
    l = pltpu.roll(pv, HEAD_DIM, 1)
    if sink is not None:
        l = l + jnp.exp2(sink - m)
    return pv / l


def _store_heads(o_ref, outs):
    lane = _lane_iota(outs[0].shape)
    for g in range(2):
        o_ref[:, g * LANES:(g + 1) * LANES] = jnp.where(
            lane < HEAD_DIM, outs[2 * g], outs[2 * g + 1]).astype(o_ref.dtype)


def _attn_band_kernel(*refs, kind, lb, n_kv, tq):
    it = iter(refs)
    q_ref = next(it)
    k_refs = [next(it) for _ in range(n_kv)]
    v_refs = [next(it) for _ in range(n_kv)]
    bias_ref = next(it) if n_kv > 1 else None
    sink_ref = next(it) if kind == "B" else None
    o_ref = next(it)
    t = pl.program_id(0) % lb
    var = jnp.where(t == 0, 0, jnp.where(t == lb - 1, 2, 1))
    plan = _head_plan(kind)
    scores = {}
    for kt in sorted({p[2] for p in plan}):
        heads = [r for r in range(N_HEADS) if plan[r][2] == kt]
        k_all = jnp.concatenate([k[:, kt * LANES:(kt + 1) * LANES] for k in k_refs], axis=0)
        q_all = jnp.concatenate([_head_query(q_ref, plan[r]) for r in heads], axis=0)
        s_all = lax.dot_general(q_all, k_all, _NT, preferred_element_type=F32)
        for j, r in enumerate(heads):
            scores[r] = s_all[j * tq:(j + 1) * tq, :]
    outs = []
    v_aug = {}
    for r, plan_r in enumerate(plan):
        _, _, kt, vt = plan_r
        if (vt, r % 2) not in v_aug:
            vv = jnp.concatenate([v[:, vt * LANES:(vt + 1) * LANES] for v in v_refs], axis=0)
            lane = _lane_iota(vv.shape)
            own = (lane < HEAD_DIM) if r % 2 == 0 else (lane >= HEAD_DIM)
            v_aug[(vt, r % 2)] = jnp.where(own, vv, jnp.ones_like(vv))
        bias = [None]
        for j in range(n_kv - 1):
            if kind == "C":
                bias.append(bias_ref[var, j, r * tq:(r + 1) * tq, :])
            else:
                bias.append(bias_ref[var, j])
        sink = sink_ref[_GQA_PERM[r]] * LOG2E if kind == "B" else None
        outs.append(_softmax_once(scores[r], v_aug[(vt, r % 2)], bias, sink))
    _store_heads(o_ref, outs)


def _attn_dense_kernel(q_ref, kc_ref, vc_ref, kl_ref, vl_ref, o_ref, qs_ref, m_ref, acc_ref,
                       *, kind, n_steps, tq):
    s = pl.program_id(1)
    plan = _head_plan(kind)

    def update(first):
        if first:
            for r in range(N_HEADS):
                qs_ref[r * tq:(r + 1) * tq, :] = _head_query(q_ref, plan[r])
        scores = {}
        for kt in sorted({p[2] for p in plan}):
            heads = [r for r in range(N_HEADS) if plan[r][2] == kt]
            ksl = slice(kt * LANES, (kt + 1) * LANES)
            k_blocks = [kc_ref[:, ksl], kl_ref[:, ksl]] if first else [kl_ref[:, ksl]]
            q_all = qs_ref[heads[0] * tq:(heads[-1] + 1) * tq, :]
            chunks = _score_chunks(q_all, k_blocks, [None] * len(k_blocks))
            for j, r in enumerate(heads):
                scores[r] = [c[j * tq:(j + 1) * tq, :] for c in chunks]
        for r, (_, _, kt, vt) in enumerate(plan):
            rows = slice(r * tq, (r + 1) * tq)
            vsl = slice(vt * LANES, (vt + 1) * LANES)
            v_blocks = [vc_ref[:, vsl], vl_ref[:, vsl]] if first else [vl_ref[:, vsl]]
            chunks = scores[r]
            m_cur = _row_max(chunks)
            if first:
                m_new = m_cur
            else:
                m_prev = m_ref[rows, :]
                m_new = jnp.maximum(m_prev, m_cur)
                alpha = jnp.exp2(m_prev - m_new)
            p = [jnp.exp2(c - m_new) for c in chunks]
            pv = _weighted_values(p, v_blocks, r % 2)
            if first:
                acc_ref[rows, :] = pv
            else:
                acc_ref[rows, :] = alpha * acc_ref[rows, :] + pv
            m_ref[rows, :] = m_new

    @pl.when(s == 0)
    def _():
        update(True)

    @pl.when(s > 0)
    def _():
        update(False)

    @pl.when(s == n_steps - 1)
    def _():
        outs = []
        for r in range(N_HEADS):
            rows = slice(r * tq, (r + 1) * tq)
            acc = acc_ref[rows, :]
            outs.append(acc / pltpu.roll(acc, HEAD_DIM, 1))
        _store_heads(o_ref, outs)


def _attn_latent_call(lay, kind, q, k, v, *, sink=None, bias=None):
    T, B, ncb, lb, S = lay["T"], lay["B"], lay["ncb"], lay["lb"], lay["S"]
    tq = ROW_BLK
    qw, kw, vw = q.shape[1], k.shape[1], v.shape[1]
    out_shape = jax.ShapeDtypeStruct((B * S, BRANCH_W), BF16)
    if kind in ("A", "D"):
        tk, tq = lay["tk_dense"], lay["tq_dense"]
        n_steps = S // tk
        lat0 = ncb * ROW_BLK // tk
        q0 = ncb * ROW_BLK // tq
        qpb = S // tq
        lat_blk = lambda i, s: (lat0 + (i // qpb) * n_steps + s, 0)
        return pl.pallas_call(
            functools.partial(_attn_dense_kernel, kind=kind, n_steps=n_steps, tq=tq),
            out_shape=out_shape,
            grid=(B * qpb, n_steps),
            in_specs=[pl.BlockSpec((tq, qw), lambda i, s: (q0 + i, 0)),
                      pl.BlockSpec((ROW_BLK, kw), lambda i, s: (i // qpb, 0)),
                      pl.BlockSpec((ROW_BLK, vw), lambda i, s: (i // qpb, 0)),
                      pl.BlockSpec((tk, kw), lat_blk),
                      pl.BlockSpec((tk, vw), lat_blk)],
            out_specs=pl.BlockSpec((tq, BRANCH_W), lambda i, s: (i, 0)),
            scratch_shapes=[pltpu.VMEM((N_HEADS * tq, LANES), BF16),
                            pltpu.VMEM((N_HEADS * tq, LANES), F32),
                            pltpu.VMEM((N_HEADS * tq, LANES), F32)],
            compiler_params=_cparams(("parallel", "arbitrary")),
            name="attn_" + kind,
        )(q, k, v, k, v)

    tq = lay["tq_band"]
    qpb = S // tq
    q0 = ncb * ROW_BLK // tq
    bpq = tq // ROW_BLK

    def nb(i, d):
        return (ncb + (i // qpb) * lb + jnp.clip((i % qpb) * bpq + d, 0, lb - 1), 0)

    kv_maps = [lambda i: (i // qpb, 0)] + [functools.partial(nb, d=d) for d in _band_offsets(tq)]
    n_kv = len(kv_maps)
    in_specs = [pl.BlockSpec((tq, qw), lambda i: (q0 + i, 0))]
    in_specs += [pl.BlockSpec((ROW_BLK, kw), m) for m in kv_maps]
    in_specs += [pl.BlockSpec((ROW_BLK, vw), m) for m in kv_maps]
    in_specs.append(pl.BlockSpec(bias.shape, lambda i: (0,) * bias.ndim, pipeline_mode=pl.Buffered(1)))
    args = [q] + [k] * n_kv + [v] * n_kv + [bias]
    if kind == "B":
        in_specs.append(pl.BlockSpec(memory_space=pltpu.SMEM))
        args.append(sink)
    return pl.pallas_call(
        functools.partial(_attn_band_kernel, kind=kind, lb=qpb, n_kv=n_kv, tq=tq),
        out_shape=out_shape,
        grid=(B * qpb,),
        in_specs=in_specs,
        out_specs=pl.BlockSpec((tq, BRANCH_W), lambda i: (i, 0)),
        compiler_params=_cparams(("parallel",)),
        name="attn_" + kind,
    )(*args)


def _attn_context_call(lay, kind, q, k, v, *, sink=None):
    ncb = lay["ncb"]
    qw, kw, vw = q.shape[1], k.shape[1], v.shape[1]
    blk = lambda i: (i, 0)
    in_specs = [pl.BlockSpec((ROW_BLK, qw), blk), pl.BlockSpec((ROW_BLK, kw), blk),
                pl.BlockSpec((ROW_BLK, vw), blk)]
    args = [q, k, v]
    if kind == "B":
        in_specs.append(pl.BlockSpec(memory_space=pltpu.SMEM))
        args.append(sink)
    return pl.pallas_call(
        functools.partial(_attn_band_kernel, kind=kind, lb=1, n_kv=1, tq=ROW_BLK),
        out_shape=jax.ShapeDtypeStruct((ncb * ROW_BLK, BRANCH_W), BF16),
        grid=(ncb,),
        in_specs=in_specs,
        out_specs=pl.BlockSpec((ROW_BLK, BRANCH_W), blk),
        compiler_params=_cparams(("parallel",)),
        name="attn_ctx_" + kind,
    )(*args)


def _band_offsets(tq):
    return tuple(range(-1, tq // ROW_BLK + 1))


def _window_bias(S, tq):
    lb, qpb, bpq = S // ROW_BLK, S // tq, tq // ROW_BLK
    offs = _band_offsets(tq)
    qa, ka = np.arange(tq), np.arange(ROW_BLK)
    out = np.full((3, len(offs), tq, ROW_BLK), NEG_INF, np.float32)
    for vi, t_rep in enumerate((0, 1, qpb - 1)):
        for di, d in enumerate(offs):
            kt = t_rep * bpq + d
            if not 0 <= kt < lb:
                continue
            qpos = t_rep * tq + qa
            kpos = kt * ROW_BLK + ka
            ok = np.abs(qpos[:, None] - kpos[None, :]) <= WINDOW
            out[vi, di] = np.where(ok, 0.0, NEG_INF)
    return jnp.asarray(out)


def _neighbourhood_bias(rpb, rows_total, tq):
    lb = rows_total * GRID_W // ROW_BLK
    qpb, bpq = rows_total * GRID_W // tq, tq // ROW_BLK
    rpt = ROW_BLK // GRID_W
    rpq = tq // GRID_W
    kh = min(NA_KH, rows_total)
    qa, ka = np.arange(tq), np.arange(ROW_BLK)
    q_sub, q_col = qa // GRID_W, qa % GRID_W
    k_sub, k_col = ka // GRID_W, ka % GRID_W
    n_dr, n_dc = 2 * NA_KH - 1, 2 * NA_KW - 1
    col = np.arange(GRID_W)
    dc = np.clip(col[None, :] - col[:, None], -(NA_KW - 1), NA_KW - 1) + NA_KW - 1
    hot_c = (dc[:, :, None] == np.arange(n_dc)).astype(np.float32)
    by_col = jnp.einsum("huv,cdv->hucd", rpb.astype(F32) * LOG2E, jnp.asarray(hot_c),
                        precision=HIGHEST)
    offs = _band_offsets(tq)
    vals = []
    for d in offs:
        dr = np.clip(d * rpt + np.arange(rpt)[None, :] - np.arange(rpq)[:, None],
                     -(NA_KH - 1), NA_KH - 1) + NA_KH - 1
        hot_r = (dr[:, :, None] == np.arange(n_dr)).astype(np.float32)
        v = jnp.einsum("abu,hucd->hacbd", jnp.asarray(hot_r), by_col, precision=HIGHEST)
        vals.append(v.reshape(N_HEADS, tq, ROW_BLK))
    out = []
    for t_rep in (0, 1, qpb - 1):
        per_block = []
        for di, d in enumerate(offs):
            kt = t_rep * bpq + d
            q_row = t_rep * rpq + q_sub
            k_row = kt * rpt + k_sub
            r_start = np.clip(q_row - kh // 2, 0, rows_total - kh)
            row_ok = (k_row[None] >= r_start[:, None]) & (k_row[None] < r_start[:, None] + kh)
            c_start = np.clip(q_col - NA_KW // 2, 0, GRID_W - NA_KW)
            col_ok = (k_col[None] >= c_start[:, None]) & (k_col[None] < c_start[:, None] + NA_KW)
            ok = row_ok & col_ok & (0 <= kt < lb)
            per_block.append(jnp.where(jnp.asarray(ok)[None], vals[di], NEG_INF).reshape(-1, ROW_BLK))
        out.append(jnp.stack(per_block))
    return jnp.stack(out)


def _merge_kernel(*refs, tm, split_x, n_ctx_tiles):
    it = iter(refs)
    h_ref = next(it)
    is_ctx = pl.program_id(0) < n_ctx_tiles
    if n_ctx_tiles > 0:
        o_refs = [(next(it), next(it)) for _ in range(4)]
        o_vals = [_tile_rows(oc, ol, is_ctx) for oc, ol in o_refs]
    else:
        o_vals = [next(it)[...] for _ in range(4)]
    if split_x:
        xc_ref, xl_ref = next(it), next(it)
    else:
        x_ref = next(it)
    (mod_ref, gffn_ref, wg_ref, bg_ref, wb_ref, wout_ref, wr_ref, br_ref,
     x1_ref, h2_ref, route_ref, y_ref) = (next(it) for _ in range(12))
    hb = h_ref[...]
    d_model = hb.shape[1]
    for t in range(d_model // MERGE_TN):
        cs = slice(t * MERGE_TN, (t + 1) * MERGE_TN)
        y = None
        for n, o_n in enumerate(o_vals):
            gate = _sigmoid(jnp.dot(hb, wg_ref[n, :, cs], preferred_element_type=F32) + bg_ref[n, :, cs])
            u = gate * jnp.dot(o_n, wb_ref[n, :, cs], preferred_element_type=F32)
            y = u if y is None else y + u
        y_ref[:, cs] = y.astype(BF16)
    z = jnp.dot(y_ref[...], wout_ref[...], preferred_element_type=F32)
    x_in = _tile_rows(xc_ref, xl_ref, is_ctx) if split_x else x_ref[...]
    x1 = x_in + mod_ref[0, 2:3, :] * z
    x1_ref[...] = x1
    h2 = _rms(x1) * gffn_ref[...] * (1.0 + mod_ref[0, 4:5, :]) + mod_ref[0, 3:4, :]
    for c in range(SUBLANES):
        h2_ref[pl.ds(c, tm, stride=SUBLANES), :] = h2[:, c * LANES:(c + 1) * LANES]

    h2_hi = h2.astype(BF16)
    h2_lo = (h2 - h2_hi.astype(F32)).astype(BF16)
    logit = (jnp.dot(h2_hi, wr_ref[0], preferred_element_type=F32)
             + jnp.dot(h2_lo, wr_ref[0], preferred_element_type=F32)
             + jnp.dot(h2_hi, wr_ref[1], preferred_element_type=F32)) + br_ref[...]
    lane = _lane_iota(logit.shape)
    big = jnp.int32(1 << 20)
    is_g = (lane >= N_EXPERTS) & (lane < N_EXPERTS + N_GROUPS)
    gl = jnp.where(is_g, logit, NEG_INF)
    gmax = jnp.max(gl, axis=-1, keepdims=True)
    gsel = jnp.min(jnp.where(gl == gmax, lane - N_EXPERTS, big), axis=-1, keepdims=True)
    gw = 1.0 / jnp.sum(jnp.where(is_g, jnp.exp(gl - gmax), 0.0), axis=-1, keepdims=True)
    in_grp = (lane < N_EXPERTS) & ((lane // EXPERTS_PER_GROUP) == gsel)
    el = jnp.where(in_grp, logit, NEG_INF)
    v1 = jnp.max(el, axis=-1, keepdims=True)
    i1 = jnp.min(jnp.where(el == v1, lane, big), axis=-1, keepdims=True)
    el2 = jnp.where(lane == i1, NEG_INF, el)
    v2 = jnp.max(el2, axis=-1, keepdims=True)
    i2 = jnp.min(jnp.where(el2 == v2, lane, big), axis=-1, keepdims=True)
    e21 = jnp.exp(v2 - v1)
    w1 = gw / (1.0 + e21)
    w2 = gw * e21 / (1.0 + e21)
    route_ref[...] = jnp.where(lane == 0, i1.astype(F32),
                               jnp.where(lane == 1, i2.astype(F32),
                                         jnp.where(lane == 2, w1, jnp.where(lane == 3, w2, 0.0))))


def _merge_call(lay, layer, h, o, x, mod, w, *, with_ctx):
    T, D, tm = lay["T"], lay["D"], lay["tm"]
    nct, tpb, B = lay["nct"], lay["tpb"], lay["B"]
    off = 0 if with_ctx else nct
    n_tiles = T // tm - off

    def mod_row(i):
        return jnp.where(i < nct, B, (i - nct) // tpb)

    row = lambda i: (i + off, 0)
    const2 = lambda i: (0, 0)
    const3 = lambda i: (0, 0, 0)
    split_x = isinstance(x, (tuple, list))
    in_specs = [pl.BlockSpec((tm, D), row)]
    o_args = []
    for kind in ("A", "B", "C", "D"):
        o_ctx, o_lat = o[kind]
        if with_ctx:
            in_specs += _split_row_specs(lay, BRANCH_W)
            o_args += [o_ctx, o_lat]
        else:
            in_specs.append(pl.BlockSpec((tm, BRANCH_W), lambda i: (i, 0)))
            o_args.append(o_lat)
    in_specs += _split_row_specs(lay, D, off) if split_x else [pl.BlockSpec((tm, D), row)]
    in_specs += [pl.BlockSpec((1, 6, D), lambda i: (layer * lay["mod_rows"] + mod_row(i + off), 0, 0)),
                 pl.BlockSpec((1, D), const2),
                 pl.BlockSpec(w["w_gate"].shape, const3),
                 pl.BlockSpec(w["b_gate"].shape, const3),
                 pl.BlockSpec(w["w_branch"].shape, const3),
                 pl.BlockSpec((D, D), const2),
                 pl.BlockSpec((2, D, LANES), const3),
                 pl.BlockSpec((1, LANES), const2)]
    out_shape = [jax.ShapeDtypeStruct((T, D), F32),
                 jax.ShapeDtypeStruct((T * SUBLANES, LANES), F32),
                 jax.ShapeDtypeStruct((T, LANES), F32)]
    out_specs = [pl.BlockSpec((tm, D), row),
                 pl.BlockSpec((tm * SUBLANES, LANES), row),
                 pl.BlockSpec((tm, LANES), row)]
    return pl.pallas_call(
        functools.partial(_merge_kernel, tm=tm, split_x=split_x, n_ctx_tiles=nct - off),
        out_shape=out_shape, grid=(n_tiles,), in_specs=in_specs, out_specs=out_specs,
        scratch_shapes=[pltpu.VMEM((tm, D), BF16)],
        compiler_params=_cparams(("parallel",)),
        name="merge",
    )(h, *o_args, *(x if split_x else (x,)), mod, w["g_ffn"], w["w_gate"],
      w["b_gate"], w["w_branch"], w["w_out"], w["w_route"], w["b_route"])


def _moe_kernel(tab_ref, tok_ref, h2_ref, sw_ref, w1_ref, w3_ref, w2_ref, f_ref,
                xg_ref, y_ref, st_ref, *, blk, slots, tile_off):
    ti = pl.program_id(0)
    step = pl.program_id(1)

    @pl.when(step == 0)
    def _():
        f_ref[...] = jnp.zeros_like(f_ref)
        xg_ref[...] = jnp.zeros_like(xg_ref)

    tok_base = (ti + tile_off) * slots
    segs = []
    for j in range(MOE_EXPERTS_PER_STEP):
        base = ((ti + tile_off) * N_EXPERTS + step * MOE_EXPERTS_PER_STEP + j) * 2
        segs.append((tab_ref[base], tab_ref[base + 1]))

    def gather(j, off, n_rows):
        slot0 = tok_base + off

        def body(gi, c):
            r0 = gi * MOE_GROUP
            srcs = [pl.multiple_of(tok_ref[slot0 + r0 + u], SUBLANES) for u in range(MOE_GROUP)]
            for u in range(MOE_GROUP):
                dst = pl.multiple_of((r0 + u) * SUBLANES, SUBLANES)
                xg_ref[j, pl.ds(dst, SUBLANES), :] = h2_ref[pl.ds(srcs[u], SUBLANES), :]
            return c

        lax.fori_loop(0, n_rows // MOE_GROUP, body, 0)

    def ffn(j, off):
        xb = _load_token_major(xg_ref.at[j], blk).astype(BF16)
        a = jnp.dot(xb, w1_ref[j], preferred_element_type=F32)
        g = jnp.dot(xb, w3_ref[j], preferred_element_type=F32)
        hid = (a * _sigmoid(a) * g).astype(BF16)
        y = jnp.dot(hid, w2_ref[j], preferred_element_type=F32)
        y_ref[j] = y * sw_ref[0, pl.ds(off, blk), :]

    def scatter(j, off, n_rows):
        slot0 = tok_base + off

        def body(gi, c):
            r0 = pl.multiple_of(gi * MOE_GROUP, MOE_GROUP)
            dsts = [pl.multiple_of(tok_ref[slot0 + r0 + u], SUBLANES) for u in range(MOE_GROUP)]
            for ch in range(SUBLANES):
                st_ref[pl.ds(ch, MOE_GROUP, stride=SUBLANES), :] = y_ref[j, pl.ds(r0, MOE_GROUP),
                                                                         ch * LANES:(ch + 1) * LANES]
            vals = [f_ref[pl.ds(dsts[u], SUBLANES), :] + st_ref[u * SUBLANES:(u + 1) * SUBLANES, :]
                    for u in range(MOE_GROUP)]
            for u in reversed(range(MOE_GROUP)):
                f_ref[pl.ds(dsts[u], SUBLANES), :] = vals[u]
            return c

        lax.fori_loop(0, n_rows // MOE_GROUP, body, 0)

    for j, (seg0, n_pad) in enumerate(segs):
        gather(j, pl.multiple_of(seg0, MOE_GROUP), jnp.minimum(n_pad, blk))
    for j, (seg0, _) in enumerate(segs):
        ffn(j, pl.multiple_of(seg0, MOE_GROUP))
    for j, (seg0, n_pad) in enumerate(segs):
        scatter(j, pl.multiple_of(seg0, MOE_GROUP), jnp.minimum(n_pad, blk))

    for j, (seg0, n_pad) in enumerate(segs):
        def extra_block(b, carry, j=j, seg0=seg0, n_pad=n_pad):
            off = pl.multiple_of(seg0 + b * blk, MOE_GROUP)
            n_rows = jnp.minimum(n_pad - b * blk, blk)
            gather(j, off, n_rows)
            ffn(j, off)
            scatter(j, off, n_rows)
            return carry

        lax.fori_loop(1, (n_pad + blk - 1) // blk, extra_block, 0)


def _moe_call(lay, route, h2, w, *, with_ctx):
    T, D, tt = lay["T"], lay["D"], lay["tt"]
    blk = lay["moe_blk"]
    n_tiles_all = T // tt
    tile_off = 0 if with_ctx else lay["ncb"] * ROW_BLK // tt
    n_tiles = n_tiles_all - tile_off
    n_assign = 2 * tt
    n_fill = N_EXPERTS * MOE_GROUP
    slots = n_assign + n_fill + blk

    eid = route[:, 0:2].astype(jnp.int32).reshape(n_tiles_all, n_assign)
    wts = route[:, 2:4].reshape(n_tiles_all, n_assign)
    tok = jnp.broadcast_to(jnp.arange(n_assign, dtype=jnp.int32)[None, :] // 2 * SUBLANES, eid.shape)
    ex = jnp.arange(N_EXPERTS, dtype=jnp.int32)
    counts = jnp.sum(eid[:, :, None] == ex[None, None, :], axis=1, dtype=jnp.int32)
    n_dummy = (-counts) % MOE_GROUP
    fill_key = jnp.where(jnp.arange(MOE_GROUP, dtype=jnp.int32)[None, None, :] < n_dummy[:, :, None],
                         ex[None, :, None], N_EXPERTS).reshape(n_tiles_all, n_fill)
    zeros_i = jnp.zeros((n_tiles_all, n_fill), jnp.int32)
    _, slot_tok, slot_w = lax.sort(
        (jnp.concatenate([eid, fill_key], axis=1), jnp.concatenate([tok, zeros_i], axis=1),
         jnp.concatenate([wts, zeros_i.astype(F32)], axis=1)),
        dimension=1, is_stable=True, num_keys=1)
    slot_tok = jnp.pad(slot_tok, ((0, 0), (0, blk)))
    slot_w = jnp.pad(slot_w, ((0, 0), (0, blk)))
    padded = counts + n_dummy
    seg0 = jnp.cumsum(padded, axis=1) - padded
    tab = jnp.stack([seg0, padded], axis=-1).reshape(-1).astype(jnp.int32)

    return pl.pallas_call(
        functools.partial(_moe_kernel, blk=blk, slots=slots, tile_off=tile_off),
        out_shape=jax.ShapeDtypeStruct((T * SUBLANES, LANES), F32),
        grid_spec=pltpu.PrefetchScalarGridSpec(
            num_scalar_prefetch=2,
            grid=(n_tiles, N_EXPERTS // MOE_EXPERTS_PER_STEP),
            in_specs=[
                pl.BlockSpec((tt * SUBLANES, LANES), lambda t, e, *_: (t + tile_off, 0)),
                pl.BlockSpec((1, slots, 1), lambda t, e, *_: (t + tile_off, 0, 0)),
                pl.BlockSpec((MOE_EXPERTS_PER_STEP, D, EXPERT_FF), lambda t, e, *_: (e, 0, 0)),
                pl.BlockSpec((MOE_EXPERTS_PER_STEP, D, EXPERT_FF), lambda t, e, *_: (e, 0, 0)),
                pl.BlockSpec((MOE_EXPERTS_PER_STEP, EXPERT_FF, D), lambda t, e, *_: (e, 0, 0)),
            ],
            out_specs=pl.BlockSpec((tt * SUBLANES, LANES), lambda t, e, *_: (t + tile_off, 0)),
            scratch_shapes=[pltpu.VMEM((MOE_EXPERTS_PER_STEP, blk * SUBLANES, LANES), F32),
                            pltpu.VMEM((MOE_EXPERTS_PER_STEP, blk, D), F32),
                            pltpu.VMEM((MOE_GROUP * SUBLANES, LANES), F32)],
        ),
        compiler_params=_cparams(("parallel", "arbitrary")),
        name="moe_experts",
    )(tab, slot_tok.reshape(-1), h2, slot_w.reshape(n_tiles_all, slots, 1),
      w["w_ff1"], w["w_ff3"], w["w_ff2"])


def _final_kernel(x_ref, f_ref, mod_ref, g_ref, o_ref, *, tm):
    xf = x_ref[...] + mod_ref[0, 5:6, :] * _load_token_major(f_ref, tm)
    o_ref[...] = _rms(xf) * g_ref[...]


def _final_call(lay, layer, x1, f, mod, g_final):
    T, D, tm = lay["T"], lay["D"], lay["tm"]
    nct, tpb = lay["nct"], lay["tpb"]
    n_lat = T // tm - nct
    return pl.pallas_call(
        functools.partial(_final_kernel, tm=tm),
        out_shape=jax.ShapeDtypeStruct((n_lat * tm, D), F32),
        grid=(n_lat,),
        in_specs=[pl.BlockSpec((tm, D), lambda i: (i + nct, 0)),
                  pl.BlockSpec((tm * SUBLANES, LANES), lambda i: (i + nct, 0)),
                  pl.BlockSpec((1, 6, D), lambda i: (layer * lay["mod_rows"] + i // tpb, 0, 0)),
                  pl.BlockSpec((1, D), lambda i: (0, 0))],
        out_specs=pl.BlockSpec((tm, D), lambda i: (i, 0)),
        compiler_params=_cparams(("parallel",)),
        name="final_norm",
    )(x1, f, mod, g_final)


def _select_cols(wm, segs, scale=None):
    parts = []
    for k, (start, width) in enumerate(segs):
        if start is None:
            parts.append(jnp.zeros((wm.shape[0], width), wm.dtype))
        else:
            blk = wm[:, start:start + width]
            parts.append(blk if scale is None or scale[k] is None else blk * scale[k])
    return jnp.concatenate(parts, axis=1)


def _prep_layer(l, p):
    a_cols = A_Q_RANK + A_KV_RANK + A_ROPE
    b_off = a_cols
    c_off = b_off + 512
    d_off = c_off + 768
    qk_scale = HEAD_DIM ** -0.5 * LOG2E
    gqa_q = lambda off: [(off + hh * HEAD_DIM, HEAD_DIM) for hh in _GQA_PERM]
    segs = ([(0, 256), (256, 128), (None, 64), (384, 32), (None, 32)]
            + gqa_q(b_off) + [(b_off + 256, 128), (b_off + 384, 128)]
            + [(c_off, 256), (c_off + 256, 256), (c_off + 512, 256)]
            + gqa_q(d_off) + [(d_off + 256, 128), (d_off + 384, 128)])
    scale = [None] * len(segs)
    for k in (5, 6, 7, 8, 11):
        scale[k] = qk_scale
    w_in = _select_cols(p["w_in"][l], segs, scale).astype(BF16)
    assert w_in.shape[1] == PROJ_COLS

    hq = A_NOPE + A_ROPE
    segs_q = []
    for hh in range(N_HEADS):
        segs_q += [(hh * hq, hq), (None, LANES - hq)]
    w_q_b = _select_cols(p["w_q_b"][l], segs_q).astype(BF16)
    hk = A_NOPE + HEAD_DIM
    segs_k = []
    for hh in range(N_HEADS):
        segs_k += [(hh * hk, A_NOPE), (None, LANES - A_NOPE)]
    segs_k += [(hh * hk + A_NOPE, HEAD_DIM) for hh in range(N_HEADS)]
    w_kv_b = _select_cols(p["w_kv_b"][l], segs_k).astype(BF16)

    wb = p["w_branch"][l]
    perm_rows = lambda m: jnp.concatenate([m[hh * HEAD_DIM:(hh + 1) * HEAD_DIM] for hh in _GQA_PERM], axis=0)
    w_branch = jnp.stack([wb[0], perm_rows(wb[1]), wb[2], perm_rows(wb[3])]).astype(BF16)

    d = p["w_in"].shape[1]
    w_route = jnp.zeros((d, LANES), F32)
    w_route = w_route.at[:, :N_EXPERTS].set(p["w_router"][l]).at[:, N_EXPERTS:N_EXPERTS + N_GROUPS].set(p["w_group"][l])
    b_route = jnp.zeros((1, LANES), F32)
    b_route = b_route.at[0, :N_EXPERTS].set(p["b_router"][l]).at[0, N_EXPERTS:N_EXPERTS + N_GROUPS].set(p["b_group"][l])
    return {
        "g_mix": p["g_norm_mix"][l][None, :],
        "w_in": w_in,
        "g_q_a": p["g_q_a"][l][None, :],
        "w_q_b": w_q_b,
        "g_kv_a": p["g_kv_a"][l][None, :],
        "w_kv_b": w_kv_b,
        "g_q_d": (jnp.tile(p["g_q_d"][l], 2) * qk_scale)[None, :],
        "g_k_d": jnp.tile(p["g_k_d"][l], 2)[None, :],
        "sink": p["sink_b"][l],
        "rpb": p["rpb_c"][l],
        "w_gate": p["w_gate"][l].astype(BF16),
        "b_gate": p["b_gate"][l][:, None, :],
        "w_branch": w_branch,
        "w_out": p["w_out"][l].astype(BF16),
        "g_ffn": p["g_norm_ffn"][l][None, :],
        "w_route": jnp.stack([w_route.astype(BF16),
                              (w_route - w_route.astype(BF16).astype(F32)).astype(BF16)]),
        "b_route": b_route,
        "w_ff1": p["w_ff1"][l].astype(BF16),
        "w_ff3": p["w_ff3"][l].astype(BF16),
        "w_ff2": p["w_ff2"][l].astype(BF16),
    }


def _rope_tables(S, tm):
    t = np.arange(S)
    rows = (t // GRID_W).astype(np.float32)
    cols = (t % GRID_W).astype(np.float32)

    def cs(rot):
        half = rot // 2
        inv = np.float32(ROPE_THETA) ** (-np.arange(0, half, 2, dtype=np.float32) / np.float32(half))
        ar_, ac_ = rows[:, None] * inv, cols[:, None] * inv
        cos = np.concatenate([np.cos(ar_), np.cos(ar_), np.cos(ac_), np.cos(ac_)], axis=-1)
        sin = np.concatenate([-np.sin(ar_), np.sin(ar_), -np.sin(ac_), np.sin(ac_)], axis=-1)
        return cos.astype(np.float32), sin.astype(np.float32)

    cos64, sin64 = cs(HEAD_DIM)
    cos32, sin32 = cs(A_ROPE)
    ones = lambda n: np.ones((S, n), np.float32)
    zeros = lambda n: np.zeros((S, n), np.float32)
    tabs = {
        "cos_h": np.concatenate([cos64, cos64], axis=-1),
        "sin_h": np.concatenate([sin64, sin64], axis=-1),
        "cos_a": np.concatenate([ones(A_NOPE), cos32, ones(LANES - A_NOPE - A_ROPE)], axis=-1),
        "sin_a": np.concatenate([zeros(A_NOPE), sin32, zeros(LANES - A_NOPE - A_ROPE)], axis=-1),
    }
    ident = {"cos_h": 1.0, "sin_h": 0.0, "cos_a": 1.0, "sin_a": 0.0}
    return {k: jnp.asarray(np.concatenate([v, np.full((tm, LANES), ident[k], np.float32)], axis=0))
            for k, v in tabs.items()}


def _layout(B, S, n_ctx, D):
    assert n_ctx == ROW_BLK and S % 1024 == 0 and S // GRID_W >= 3 * (ROW_BLK // GRID_W) and B < 16
    T = B * (n_ctx + S)
    tm = 512 if (B * n_ctx) % 512 == 0 else 256
    def largest_tile(cands):
        return next(t for t in cands if (B * n_ctx) % t == 0 and S % t == 0)

    tq_dense = largest_tile((1024, 512, 256))
    tk_dense = largest_tile((2048, 1024, 512, 256))
    tt = 2048 if (B * n_ctx) % 2048 == 0 else B * n_ctx
    assert S % tt == 0
    return {
        "B": B, "S": S, "D": D, "T": T, "tm": tm,
        "ncb": B * n_ctx // ROW_BLK,
        "lb": S // ROW_BLK,
        "nct": B * n_ctx // tm,
        "tpb": S // tm,
        "tk_dense": tk_dense,
        "tq_dense": tq_dense,
        "tq_band": tm,
        "tt": tt,
        "moe_blk": 160,
        "mod_rows": 16,
    }


def kernel(x, c, ctx, c_ctx, w_mod, b_mod, g_norm_mix, w_in, g_q_a, w_q_b, g_kv_a, w_kv_b, sink_b, rpb_c,
           g_q_d, g_k_d, w_gate, b_gate, w_branch, w_out, g_norm_ffn, w_group, b_group, w_router, b_router,
           w_ff1, w_ff3, w_ff2, g_final):
    B, S, D = x.shape
    n_ctx = ctx.shape[1]
    depth = w_mod.shape[0]
    lay = _layout(B, S, n_ctx, D)
    params = dict(w_in=w_in, g_norm_mix=g_norm_mix, g_q_a=g_q_a, w_q_b=w_q_b, g_kv_a=g_kv_a, w_kv_b=w_kv_b,
                  sink_b=sink_b, rpb_c=rpb_c, g_q_d=g_q_d, g_k_d=g_k_d, w_gate=w_gate, b_gate=b_gate,
                  w_branch=w_branch, w_out=w_out, g_norm_ffn=g_norm_ffn, w_group=w_group, b_group=b_group,
                  w_router=w_router, b_router=b_router, w_ff1=w_ff1, w_ff3=w_ff3, w_ff2=w_ff2)

    c_all = jnp.zeros((lay["mod_rows"], D), F32).at[:B].set(c).at[B].set(c_ctx)
    mod = _modulation(c_all, w_mod, b_mod).reshape(depth * lay["mod_rows"], 6, D)
    tabs = _rope_tables(S, lay["tm"])
    win_bias = _window_bias(S, lay["tq_band"])
    xf = (ctx.reshape(B * n_ctx, D), x.reshape(B * S, D))

    f = None
    for l in range(depth):
        with_ctx = l < depth - 1
        w = _prep_layer(l, params)
        xf, pr = _proj_call(lay, xf, f, mod, mod, l, w, tabs)
        o = {}
        for kind, kl in (("A", "a"), ("B", "b"), ("C", "c"), ("D", "d")):
            q, k, v = pr["q" + kl], pr["k" + kl], pr["v" + kl]
            sink = w["sink"] if kind == "B" else None
            bias = win_bias if kind == "B" else None
            if kind == "C":
                bias = _neighbourhood_bias(w["rpb"], S // GRID_W, lay["tq_band"])
            o_ctx = _attn_context_call(lay, kind, q, k, v, sink=sink) if with_ctx else None
            o[kind] = (o_ctx, _attn_latent_call(lay, kind, q, k, v, sink=sink, bias=bias))
        xf, h2, route = _merge_call(lay, l, pr["h"], o, xf, mod, w, with_ctx=with_ctx)
        f = _moe_call(lay, route, h2, w, with_ctx=with_ctx)
    out = _final_call(lay, depth - 1, xf, f, mod, g_final[None, :])
    return out.reshape(B, S, D)
```

```python
import functools

import numpy as np
import jax
import jax.numpy as jnp
from jax import lax
from jax.experimental import pallas as pl
from jax.experimental.pallas import tpu as pltpu

F32 = jnp.float32
BF16 = jnp.bfloat16
HIGHEST = lax.Precision.HIGHEST

GRID_W = 64
ROPE_THETA = 10000.0
EPS = 1e-6
NEG_INF = -1e30
LOG2E = 1.4426950408889634
HEAD_DIM = 64
N_HEADS = 4
BRANCH_W = 256
A_Q_RANK = 256
A_KV_RANK = 128
A_NOPE = 64
A_ROPE = 32
NA_KH = 8
NA_KW = 16
WINDOW = 128
N_GROUPS = 4
EXPERTS_PER_GROUP = 8
N_EXPERTS = 32
EXPERT_FF = 256

LANES = 128
SUBLANES = 8
ROW_BLK = 256
MERGE_TN = 256
MOE_EXPERTS_PER_STEP = 4
MOE_GROUP = 16
VMEM_LIMIT = 56 * 1024 * 1024

_PROJ_GROUPS = (("cq", 256), ("ckv", 128), ("kr", 128), ("qb", 256), ("kb", 128), ("vb", 128),
                ("qc", 256), ("kc", 256), ("vc", 256), ("qd", 256), ("kd", 128), ("vd", 128))
_PROJ_OFF = {}
_o = 0
for _n, _w in _PROJ_GROUPS:
    _PROJ_OFF[_n] = (_o, _w)
    _o += _w
PROJ_COLS = _o
_GQA_PERM = (0, 2, 1, 3)


def _cparams(sem):
    return pltpu.CompilerParams(dimension_semantics=sem, vmem_limit_bytes=VMEM_LIMIT)


def _lane_iota(shape):
    return lax.broadcasted_iota(jnp.int32, shape, len(shape) - 1)


def _sigmoid(x):
    return 1.0 / (1.0 + jnp.exp(-x))


def _mod_kernel(c_ref, w_ref, b_ref, o_ref):
    cf = c_ref[...]
    s = cf * _sigmoid(cf)
    o_ref[0] = jnp.dot(s, w_ref[0], precision=HIGHEST, preferred_element_type=F32) + b_ref[0]


def _modulation(c_all, w_mod, b_mod):
    n_layers, d, n_out = w_mod.shape
    rows = c_all.shape[0]
    tn = n_out // 2
    return pl.pallas_call(
        _mod_kernel,
        out_shape=jax.ShapeDtypeStruct((n_layers, rows, n_out), F32),
        grid=(n_layers, n_out // tn),
        in_specs=[pl.BlockSpec((rows, d), lambda l, j: (0, 0)),
                  pl.BlockSpec((1, d, tn), lambda l, j: (l, 0, j)),
                  pl.BlockSpec((1, 1, tn), lambda l, j: (l, 0, j))],
        out_specs=pl.BlockSpec((1, rows, tn), lambda l, j: (l, 0, j)),
        compiler_params=_cparams(("arbitrary", "arbitrary")),
        name="modulation",
    )(c_all, w_mod, b_mod.reshape(n_layers, 1, n_out))


def _rms(x):
    return x * lax.rsqrt(jnp.mean(x * x, axis=-1, keepdims=True) + EPS)


def _swap_blocks(x, blk):
    lane = _lane_iota(x.shape)
    up = pltpu.roll(x, LANES - blk, 1)
    dn = pltpu.roll(x, blk, 1)
    return jnp.where((lane // blk) % 2 == 0, up, dn)


def _rope(x, cos, sin, blk):
    return x * cos + _swap_blocks(x, blk) * sin


def _pair_norm(x, g):
    lo = _lane_iota(x.shape) < HEAD_DIM
    sq = x * x
    s_lo = jnp.sum(jnp.where(lo, sq, 0.0), axis=-1, keepdims=True)
    s_hi = jnp.sum(jnp.where(lo, 0.0, sq), axis=-1, keepdims=True)
    ms = jnp.where(lo, s_lo, s_hi) * (1.0 / HEAD_DIM)
    return x * lax.rsqrt(ms + EPS) * g


def _load_token_major(ref, rows):
    return jnp.concatenate(
        [ref[pl.ds(c, rows, stride=SUBLANES), :] for c in range(SUBLANES)], axis=-1)


def _tile_rows(xc_ref, xl_ref, is_ctx):
    return jnp.where(is_ctx, xc_ref[...], xl_ref[...])


def _proj_kernel(*refs, with_f, tm, scale_a, nct):
    it = iter(refs)
    if with_f:
        x_ref = next(it)
        f_ref = next(it)
        modp_ref = next(it)
    else:
        xc_ref, xl_ref = next(it), next(it)
    mod_ref = next(it)
    gmix_ref, win_ref, gqa_ref, wqb_ref, gkva_ref, wkvb_ref, gqd_ref, gkd_ref = (next(it) for _ in range(8))
    cosh_ref, sinh_ref, cosa_ref, sina_ref = (next(it) for _ in range(4))
    if with_f:
        x2_ref = next(it)
    h_ref = next(it)
    qa_ref, ka_ref, va_ref, qb_ref, kb_ref, vb_ref, qc_ref, kc_ref, vc_ref, qd_ref, kd_ref, vd_ref = (
        next(it) for _ in range(12))

    if with_f:
        xf = x_ref[...] + modp_ref[0, 5:6, :] * _load_token_major(f_ref, tm)
        x2_ref[...] = xf
    else:
        xf = _tile_rows(xc_ref, xl_ref, pl.program_id(0) < nct)
    h = _rms(xf) * gmix_ref[...] * (1.0 + mod_ref[0, 1:2, :]) + mod_ref[0, 0:1, :]
    hb = h.astype(BF16)
    h_ref[...] = hb
    p = jnp.dot(hb, win_ref[...], preferred_element_type=F32)

    def grp(name):
        o, w = _PROJ_OFF[name]
        return p[:, o:o + w]

    cosh, sinh = cosh_ref[...], sinh_ref[...]
    cosa, sina = cosa_ref[...], sina_ref[...]

    cq = (_rms(grp("cq")) * gqa_ref[...]).astype(BF16)
    qa = jnp.dot(cq, wqb_ref[...], preferred_element_type=F32)
    for hd in range(N_HEADS):
        sl = slice(hd * LANES, (hd + 1) * LANES)
        qa_ref[:, sl] = (_rope(qa[:, sl], cosa, sina, 8) * scale_a).astype(BF16)
    ckv = (_rms(grp("ckv")) * gkva_ref[...]).astype(BF16)
    kva = jnp.dot(ckv, wkvb_ref[...], preferred_element_type=F32)
    kr = _rope(grp("kr"), cosa, sina, 8)
    for hd in range(N_HEADS):
        sl = slice(hd * LANES, (hd + 1) * LANES)
        ka_ref[:, sl] = (kva[:, sl] + kr).astype(BF16)
    va_ref[...] = kva[:, N_HEADS * LANES:].astype(BF16)

    qb = grp("qb")
    for j in range(2):
        sl = slice(j * LANES, (j + 1) * LANES)
        qb_ref[:, sl] = _rope(qb[:, sl], cosh, sinh, 16).astype(BF16)
    kb_ref[...] = _rope(grp("kb"), cosh, sinh, 16).astype(BF16)
    vb_ref[...] = grp("vb").astype(BF16)

    qc_ref[...] = grp("qc").astype(BF16)
    kc_ref[...] = grp("kc").astype(BF16)
    vc_ref[...] = grp("vc").astype(BF16)

    qd = grp("qd")
    for j in range(2):
        sl = slice(j * LANES, (j + 1) * LANES)
        qd_ref[:, sl] = _rope(_pair_norm(qd[:, sl], gqd_ref[...]), cosh, sinh, 16).astype(BF16)
    kd_ref[...] = _rope(_pair_norm(grp("kd"), gkd_ref[...]), cosh, sinh, 16).astype(BF16)
    vd_ref[...] = grp("vd").astype(BF16)


def _split_row_specs(lay, width, off=0):
    tm, nct = lay["tm"], lay["nct"]
    return [pl.BlockSpec((tm, width), lambda i: (jnp.minimum(i + off, nct - 1), 0)),
            pl.BlockSpec((tm, width), lambda i: (jnp.maximum(i + off - nct, 0), 0))]


def _proj_call(lay, x, f, modp, mod, layer, w, tabs):
    T, D, tm = lay["T"], lay["D"], lay["tm"]
    nct, tpb, B = lay["nct"], lay["tpb"], lay["B"]
    with_f = f is not None
    n_tiles = T // tm

    def mod_row(i):
        return jnp.where(i < nct, B, (i - nct) // tpb)

    def tab_blk(i):
        return jnp.where(i < nct, tpb, (i - nct) % tpb)

    row = lambda i: (i, 0)
    const = lambda i: (0, 0)
    if with_f:
        in_specs = [pl.BlockSpec((tm, D), row),
                    pl.BlockSpec((tm * SUBLANES, LANES), row),
                    pl.BlockSpec((1, 6, D), lambda i: ((layer - 1) * lay["mod_rows"] + mod_row(i), 0, 0))]
        args = [x, f, modp]
    else:
        in_specs = _split_row_specs(lay, D)
        args = list(x)
    in_specs += [pl.BlockSpec((1, 6, D), lambda i: (layer * lay["mod_rows"] + mod_row(i), 0, 0))]
    args += [mod]
    for name in ("g_mix", "w_in", "g_q_a", "w_q_b", "g_kv_a", "w_kv_b", "g_q_d", "g_k_d"):
        a = w[name]
        in_specs.append(pl.BlockSpec(a.shape, const))
        args.append(a)
    for tname in ("cos_h", "sin_h", "cos_a", "sin_a"):
        in_specs.append(pl.BlockSpec((tm, LANES), lambda i: (tab_blk(i), 0)))
        args.append(tabs[tname])

    widths = [("h", D), ("qa", 512), ("ka", 512), ("va", 256), ("qb", 256), ("kb", 128), ("vb", 128),
              ("qc", 256), ("kc", 256), ("vc", 256), ("qd", 256), ("kd", 128), ("vd", 128)]
    out_shape, out_specs = [], []
    if with_f:
        out_shape.append(jax.ShapeDtypeStruct((T, D), F32))
        out_specs.append(pl.BlockSpec((tm, D), row))
    for _, wd in widths:
        out_shape.append(jax.ShapeDtypeStruct((T, wd), BF16))
        out_specs.append(pl.BlockSpec((tm, wd), row))

    outs = pl.pallas_call(
        functools.partial(_proj_kernel, with_f=with_f, tm=tm, nct=nct,
                          scale_a=float((A_NOPE + A_ROPE) ** -0.5 * LOG2E)),
        out_shape=out_shape, grid=(n_tiles,), in_specs=in_specs, out_specs=out_specs,
        compiler_params=_cparams(("parallel",)),
        name="proj_in",
    )(*args)
    outs = list(outs)
    x2 = outs.pop(0) if with_f else x
    names = [n for n, _ in widths]
    return x2, dict(zip(names, outs))


_NT = (((1,), (1,)), ((), ()))


def _head_plan(kind):
    if kind == "A":
        return tuple((r, None, r, r // 2) for r in range(N_HEADS))
    if kind == "C":
        return tuple((r // 2, r % 2, r // 2, r // 2) for r in range(N_HEADS))
    return tuple((r // 2, r % 2, 0, 0) for r in range(N_HEADS))


def _head_query(q_ref, plan_r):
    qt, half, _, _ = plan_r
    src = q_ref[:, qt * LANES:(qt + 1) * LANES]
    if half is None:
        return src
    lane = _lane_iota(src.shape)
    keep = (lane < HEAD_DIM) if half == 0 else (lane >= HEAD_DIM)
    return jnp.where(keep, src, jnp.zeros_like(src))


def _score_chunks(q, k_blocks, bias_blocks):
    chunks = []
    for kb, bb in zip(k_blocks, bias_blocks):
        s = lax.dot_general(q, kb, _NT, preferred_element_type=F32)
        if bb is not None:
            s = s + bb
        chunks += [s[:, c * LANES:(c + 1) * LANES] for c in range(s.shape[1] // LANES)]
    return chunks


def _row_max(chunks):
    m = functools.reduce(jnp.maximum, chunks)
    return jnp.broadcast_to(jnp.max(m, axis=-1, keepdims=True), m.shape)


def _weighted_values(p_chunks, v_blocks, half):
    pv, idx = None, 0
    for vb in v_blocks:
        n = vb.shape[0] // LANES
        p = jnp.concatenate(p_chunks[idx:idx + n], axis=1).astype(BF16)
        idx += n
        lane = _lane_iota(vb.shape)
        own = (lane < HEAD_DIM) if half == 0 else (lane >= HEAD_DIM)
        d = jnp.dot(p, jnp.where(own, vb, jnp.ones_like(vb)), preferred_element_type=F32)
        pv = d if pv is None else pv + d
    return pv


def _softmax_once(s, v_aug, bias_blocks, sink):
    chunks = []
    for c in range(s.shape[1] // LANES):
        sc = s[:, c * LANES:(c + 1) * LANES]
        bb = bias_blocks[c * LANES // ROW_BLK]
        if bb is not None:
            lo = c * LANES % ROW_BLK
            sc = sc + bb[:, lo:lo + LANES]
        chunks.append(sc)
    m = _row_max(chunks)
    if sink is not None:
        m = jnp.maximum(m, sink)
    p = jnp.concatenate([jnp.exp2(c - m) for c in chunks], axis=1).astype(BF16)
    pv = jnp.dot(p, v_aug, preferred_element_type=F32)
    l = pltpu.roll(pv, HEAD_DIM, 1)
    if sink is not None:
        l = l + jnp.exp2(sink - m)
    return pv / l


def _store_heads(o_ref, outs):
    lane = _lane_iota(outs[0].shape)
    for g in range(2):
        o_ref[:, g * LANES:(g + 1) * LANES] = jnp.where(
            lane < HEAD_DIM, outs[2 * g], outs[2 * g + 1]).astype(o_ref.dtype)


def _attn_band_kernel(*refs, kind, lb, n_kv, tq):
    it = iter(refs)
    q_ref = next(it)
    k_refs = [next(it) for _ in range(n_kv)]
    v_refs = [next(it) for _ in range(n_kv)]
    bias_ref = next(it) if n_kv > 1 else None
    sink_ref = next(it) if kind == "B" else None
    o_ref = next(it)
    t = pl.program_id(0) % lb
    var = jnp.where(t == 0, 0, jnp.where(t == lb - 1, 2, 1))
    plan = _head_plan(kind)
    scores = {}
    for kt in sorted({p[2] for p in plan}):
        heads = [r for r in range(N_HEADS) if plan[r][2] == kt]
        k_all = jnp.concatenate([k[:, kt * LANES:(kt + 1) * LANES] for k in k_refs], axis=0)
        q_all = jnp.concatenate([_head_query(q_ref, plan[r]) for r in heads], axis=0)
        s_all = lax.dot_general(q_all, k_all, _NT, preferred_element_type=F32)
        for j, r in enumerate(heads):
            scores[r] = s_all[j * tq:(j + 1) * tq, :]
    outs = []
    v_aug = {}
    for r, plan_r in enumerate(plan):
        _, _, kt, vt = plan_r
        if (vt, r % 2) not in v_aug:
            vv = jnp.concatenate([v[:, vt * LANES:(vt + 1) * LANES] for v in v_refs], axis=0)
            lane = _lane_iota(vv.shape)
            own = (lane < HEAD_DIM) if r % 2 == 0 else (lane >= HEAD_DIM)
            v_aug[(vt, r % 2)] = jnp.where(own, vv, jnp.ones_like(vv))
        bias = [None]
        for j in range(n_kv - 1):
            if kind == "C":
                bias.append(bias_ref[var, j, r * tq:(r + 1) * tq, :])
            else:
                bias.append(bias_ref[var, j])
        sink = sink_ref[_GQA_PERM[r]] * LOG2E if kind == "B" else None
        outs.append(_softmax_once(scores[r], v_aug[(vt, r % 2)], bias, sink))
    _store_heads(o_ref, outs)


def _attn_dense_kernel(q_ref, kc_ref, vc_ref, kl_ref, vl_ref, o_ref, qs_ref, m_ref, acc_ref,
                       *, kind, n_steps, tq):
    s = pl.program_id(1)
    plan = _head_plan(kind)

    def update(first):
        if first:
            for r in range(N_HEADS):
                qs_ref[r * tq:(r + 1) * tq, :] = _head_query(q_ref, plan[r])
        scores = {}
        for kt in sorted({p[2] for p in plan}):
            heads = [r for r in range(N_HEADS) if plan[r][2] == kt]
            ksl = slice(kt * LANES, (kt + 1) * LANES)
            k_blocks = [kc_ref[:, ksl], kl_ref[:, ksl]] if first else [kl_ref[:, ksl]]
            q_all = qs_ref[heads[0] * tq:(heads[-1] + 1) * tq, :]
            chunks = _score_chunks(q_all, k_blocks, [None] * len(k_blocks))
            for j, r in enumerate(heads):
                scores[r] = [c[j * tq:(j + 1) * tq, :] for c in chunks]
        for r, (_, _, kt, vt) in enumerate(plan):
            rows = slice(r * tq, (r + 1) * tq)
            vsl = slice(vt * LANES, (vt + 1) * LANES)
            v_blocks = [vc_ref[:, vsl], vl_ref[:, vsl]] if first else [vl_ref[:, vsl]]
            chunks = scores[r]
            m_cur = _row_max(chunks)
            if first:
                m_new = m_cur
            else:
                m_prev = m_ref[rows, :]
                m_new = jnp.maximum(m_prev, m_cur)
                alpha = jnp.exp2(m_prev - m_new)
            p = [jnp.exp2(c - m_new) for c in chunks]
            pv = _weighted_values(p, v_blocks, r % 2)
            if first:
                acc_ref[rows, :] = pv
            else:
                acc_ref[rows, :] = alpha * acc_ref[rows, :] + pv
            m_ref[rows, :] = m_new

    @pl.when(s == 0)
    def _():
        update(True)

    @pl.when(s > 0)
    def _():
        update(False)

    @pl.when(s == n_steps - 1)
    def _():
        outs = []
        for r in range(N_HEADS):
            rows = slice(r * tq, (r + 1) * tq)
            acc = acc_ref[rows, :]
            outs.append(acc / pltpu.roll(acc, HEAD_DIM, 1))
        _store_heads(o_ref, outs)


def _attn_latent_call(lay, kind, q, k, v, *, sink=None, bias=None):
    T, B, ncb, lb, S = lay["T"], lay["B"], lay["ncb"], lay["lb"], lay["S"]
    tq = ROW_BLK
    qw, kw, vw = q.shape[1], k.shape[1], v.shape[1]
    out_shape = jax.ShapeDtypeStruct((B * S, BRANCH_W), BF16)
    if kind in ("A", "D"):
        tk, tq = lay["tk_dense"], lay["tq_dense"]
        n_steps = S // tk
        lat0 = ncb * ROW_BLK // tk
        q0 = ncb * ROW_BLK // tq
        qpb = S // tq
        lat_blk = lambda i, s: (lat0 + (i // qpb) * n_steps + s, 0)
        return pl.pallas_call(
            functools.partial(_attn_dense_kernel, kind=kind, n_steps=n_steps, tq=tq),
            out_shape=out_shape,
            grid=(B * qpb, n_steps),
            in_specs=[pl.BlockSpec((tq, qw), lambda i, s: (q0 + i, 0)),
                      pl.BlockSpec((ROW_BLK, kw), lambda i, s: (i // qpb, 0)),
                      pl.BlockSpec((ROW_BLK, vw), lambda i, s: (i // qpb, 0)),
                      pl.BlockSpec((tk, kw), lat_blk),
                      pl.BlockSpec((tk, vw), lat_blk)],
            out_specs=pl.BlockSpec((tq, BRANCH_W), lambda i, s: (i, 0)),
            scratch_shapes=[pltpu.VMEM((N_HEADS * tq, LANES), BF16),
                            pltpu.VMEM((N_HEADS * tq, LANES), F32),
                            pltpu.VMEM((N_HEADS * tq, LANES), F32)],
            compiler_params=_cparams(("parallel", "arbitrary")),
            name="attn_" + kind,
        )(q, k, v, k, v)

    tq = lay["tq_band"]
    qpb = S // tq
    q0 = ncb * ROW_BLK // tq
    bpq = tq // ROW_BLK

    def nb(i, d):
        return (ncb + (i // qpb) * lb + jnp.clip((i % qpb) * bpq + d, 0, lb - 1), 0)

    kv_maps = [lambda i: (i // qpb, 0)] + [functools.partial(nb, d=d) for d in _band_offsets(tq)]
    n_kv = len(kv_maps)
    in_specs = [pl.BlockSpec((tq, qw), lambda i: (q0 + i, 0))]
    in_specs += [pl.BlockSpec((ROW_BLK, kw), m) for m in kv_maps]
    in_specs += [pl.BlockSpec((ROW_BLK, vw), m) for m in kv_maps]
    in_specs.append(pl.BlockSpec(bias.shape, lambda i: (0,) * bias.ndim, pipeline_mode=pl.Buffered(1)))
    args = [q] + [k] * n_kv + [v] * n_kv + [bias]
    if kind == "B":
        in_specs.append(pl.BlockSpec(memory_space=pltpu.SMEM))
        args.append(sink)
    return pl.pallas_call(
        functools.partial(_attn_band_kernel, kind=kind, lb=qpb, n_kv=n_kv, tq=tq),
        out_shape=out_shape,
        grid=(B * qpb,),
        in_specs=in_specs,
        out_specs=pl.BlockSpec((tq, BRANCH_W), lambda i: (i, 0)),
        compiler_params=_cparams(("parallel",)),
        name="attn_" + kind,
    )(*args)


def _attn_context_call(lay, kind, q, k, v, *, sink=None):
    ncb = lay["ncb"]
    qw, kw, vw = q.shape[1], k.shape[1], v.shape[1]
    blk = lambda i: (i, 0)
    in_specs = [pl.BlockSpec((ROW_BLK, qw), blk), pl.BlockSpec((ROW_BLK, kw), blk),
                pl.BlockSpec((ROW_BLK, vw), blk)]
    args = [q, k, v]
    if kind == "B":
        in_specs.append(pl.BlockSpec(memory_space=pltpu.SMEM))
        args.append(sink)
    return pl.pallas_call(
        functools.partial(_attn_band_kernel, kind=kind, lb=1, n_kv=1, tq=ROW_BLK),
        out_shape=jax.ShapeDtypeStruct((ncb * ROW_BLK, BRANCH_W), BF16),
        grid=(ncb,),
        in_specs=in_specs,
        out_specs=pl.BlockSpec((ROW_BLK, BRANCH_W), blk),
        compiler_params=_cparams(("parallel",)),
        name="attn_ctx_" + kind,
    )(*args)


def _band_offsets(tq):
    return tuple(range(-1, tq // ROW_BLK + 1))


def _window_bias(S, tq):
    lb, qpb, bpq = S // ROW_BLK, S // tq, tq // ROW_BLK
    offs = _band_offsets(tq)
    qa, ka = np.arange(tq), np.arange(ROW_BLK)
    out = np.full((3, len(offs), tq, ROW_BLK), NEG_INF, np.float32)
    for vi, t_rep in enumerate((0, 1, qpb - 1)):
        for di, d in enumerate(offs):
            kt = t_rep * bpq + d
            if not 0 <= kt < lb:
                continue
            qpos = t_rep * tq + qa
            kpos = kt * ROW_BLK + ka
            ok = np.abs(qpos[:, None] - kpos[None, :]) <= WINDOW
            out[vi, di] = np.where(ok, 0.0, NEG_INF)
    return jnp.asarray(out)


def _neighbourhood_bias(rpb, rows_total, tq):
    lb = rows_total * GRID_W // ROW_BLK
    qpb, bpq = rows_total * GRID_W // tq, tq // ROW_BLK
    rpt = ROW_BLK // GRID_W
    rpq = tq // GRID_W
    kh = min(NA_KH, rows_total)
    qa, ka = np.arange(tq), np.arange(ROW_BLK)
    q_sub, q_col = qa // GRID_W, qa % GRID_W
    k_sub, k_col = ka // GRID_W, ka % GRID_W
    n_dr, n_dc = 2 * NA_KH - 1, 2 * NA_KW - 1
    col = np.arange(GRID_W)
    dc = np.clip(col[None, :] - col[:, None], -(NA_KW - 1), NA_KW - 1) + NA_KW - 1
    hot_c = (dc[:, :, None] == np.arange(n_dc)).astype(np.float32)
    by_col = jnp.einsum("huv,cdv->hucd", rpb.astype(F32) * LOG2E, jnp.asarray(hot_c),
                        precision=HIGHEST)
    offs = _band_offsets(tq)
    vals = []
    for d in offs:
        dr = np.clip(d * rpt + np.arange(rpt)[None, :] - np.arange(rpq)[:, None],
                     -(NA_KH - 1), NA_KH - 1) + NA_KH - 1
        hot_r = (dr[:, :, None] == np.arange(n_dr)).astype(np.float32)
        v = jnp.einsum("abu,hucd->hacbd", jnp.asarray(hot_r), by_col, precision=HIGHEST)
        vals.append(v.reshape(N_HEADS, tq, ROW_BLK))
    out = []
    for t_rep in (0, 1, qpb - 1):
        per_block = []
        for di, d in enumerate(offs):
            kt = t_rep * bpq + d
            q_row = t_rep * rpq + q_sub
            k_row = kt * rpt + k_sub
            r_start = np.clip(q_row - kh // 2, 0, rows_total - kh)
            row_ok = (k_row[None] >= r_start[:, None]) & (k_row[None] < r_start[:, None] + kh)
            c_start = np.clip(q_col - NA_KW // 2, 0, GRID_W - NA_KW)
            col_ok = (k_col[None] >= c_start[:, None]) & (k_col[None] < c_start[:, None] + NA_KW)
            ok = row_ok & col_ok & (0 <= kt < lb)
            per_block.append(jnp.where(jnp.asarray(ok)[None], vals[di], NEG_INF).reshape(-1, ROW_BLK))
        out.append(jnp.stack(per_block))
    return jnp.stack(out)


def _merge_kernel(*refs, tm, split_x, n_ctx_tiles):
    it = iter(refs)
    h_ref = next(it)
    is_ctx = pl.program_id(0) < n_ctx_tiles
    if n_ctx_tiles > 0:
        o_refs = [(next(it), next(it)) for _ in range(4)]
        o_vals = [_tile_rows(oc, ol, is_ctx) for oc, ol in o_refs]
    else:
        o_vals = [next(it)[...] for _ in range(4)]
    if split_x:
        xc_ref, xl_ref = next(it), next(it)
    else:
        x_ref = next(it)
    (mod_ref, gffn_ref, wg_ref, bg_ref, wb_ref, wout_ref, wr_ref, br_ref,
     x1_ref, h2_ref, route_ref, y_ref) = (next(it) for _ in range(12))
    hb = h_ref[...]
    d_model = hb.shape[1]
    for t in range(d_model // MERGE_TN):
        cs = slice(t * MERGE_TN, (t + 1) * MERGE_TN)
        y = None
        for n, o_n in enumerate(o_vals):
            gate = _sigmoid(jnp.dot(hb, wg_ref[n, :, cs], preferred_element_type=F32) + bg_ref[n, :, cs])
            u = gate * jnp.dot(o_n, wb_ref[n, :, cs], preferred_element_type=F32)
            y = u if y is None else y + u
        y_ref[:, cs] = y.astype(BF16)
    z = jnp.dot(y_ref[...], wout_ref[...], preferred_element_type=F32)
    x_in = _tile_rows(xc_ref, xl_ref, is_ctx) if split_x else x_ref[...]
    x1 = x_in + mod_ref[0, 2:3, :] * z
    x1_ref[...] = x1
    h2 = _rms(x1) * gffn_ref[...] * (1.0 + mod_ref[0, 4:5, :]) + mod_ref[0, 3:4, :]
    for c in range(SUBLANES):
        h2_ref[pl.ds(c, tm, stride=SUBLANES), :] = h2[:, c * LANES:(c + 1) * LANES]

    h2_hi = h2.astype(BF16)
    h2_lo = (h2 - h2_hi.astype(F32)).astype(BF16)
    logit = (jnp.dot(h2_hi, wr_ref[0], preferred_element_type=F32)
             + jnp.dot(h2_lo, wr_ref[0], preferred_element_type=F32)
             + jnp.dot(h2_hi, wr_ref[1], preferred_element_type=F32)) + br_ref[...]
    lane = _lane_iota(logit.shape)
    big = jnp.int32(1 << 20)
    is_g = (lane >= N_EXPERTS) & (lane < N_EXPERTS + N_GROUPS)
    gl = jnp.where(is_g, logit, NEG_INF)
    gmax = jnp.max(gl, axis=-1, keepdims=True)
    gsel = jnp.min(jnp.where(gl == gmax, lane - N_EXPERTS, big), axis=-1, keepdims=True)
    gw = 1.0 / jnp.sum(jnp.where(is_g, jnp.exp(gl - gmax), 0.0), axis=-1, keepdims=True)
    in_grp = (lane < N_EXPERTS) & ((lane // EXPERTS_PER_GROUP) == gsel)
    el = jnp.where(in_grp, logit, NEG_INF)
    v1 = jnp.max(el, axis=-1, keepdims=True)
    i1 = jnp.min(jnp.where(el == v1, lane, big), axis=-1, keepdims=True)
    el2 = jnp.where(lane == i1, NEG_INF, el)
    v2 = jnp.max(el2, axis=-1, keepdims=True)
    i2 = jnp.min(jnp.where(el2 == v2, lane, big), axis=-1, keepdims=True)
    e21 = jnp.exp(v2 - v1)
    w1 = gw / (1.0 + e21)
    w2 = gw * e21 / (1.0 + e21)
    route_ref[...] = jnp.where(lane == 0, i1.astype(F32),
                               jnp.where(lane == 1, i2.astype(F32),
                                         jnp.where(lane == 2, w1, jnp.where(lane == 3, w2, 0.0))))


def _merge_call(lay, layer, h, o, x, mod, w, *, with_ctx):
    T, D, tm = lay["T"], lay["D"], lay["tm"]
    nct, tpb, B = lay["nct"], lay["tpb"], lay["B"]
    off = 0 if with_ctx else nct
    n_tiles = T // tm - off

    def mod_row(i):
        return jnp.where(i < nct, B, (i - nct) // tpb)

    row = lambda i: (i + off, 0)
    const2 = lambda i: (0, 0)
    const3 = lambda i: (0, 0, 0)
    split_x = isinstance(x, (tuple, list))
    in_specs = [pl.BlockSpec((tm, D), row)]
    o_args = []
    for kind in ("A", "B", "C", "D"):
        o_ctx, o_lat = o[kind]
        if with_ctx:
            in_specs += _split_row_specs(lay, BRANCH_W)
            o_args += [o_ctx, o_lat]
        else:
            in_specs.append(pl.BlockSpec((tm, BRANCH_W), lambda i: (i, 0)))
            o_args.append(o_lat)
    in_specs += _split_row_specs(lay, D, off) if split_x else [pl.BlockSpec((tm, D), row)]
    in_specs += [pl.BlockSpec((1, 6, D), lambda i: (layer * lay["mod_rows"] + mod_row(i + off), 0, 0)),
                 pl.BlockSpec((1, D), const2),
                 pl.BlockSpec(w["w_gate"].shape, const3),
                 pl.BlockSpec(w["b_gate"].shape, const3),
                 pl.BlockSpec(w["w_branch"].shape, const3),
                 pl.BlockSpec((D, D), const2),
                 pl.BlockSpec((2, D, LANES), const3),
                 pl.BlockSpec((1, LANES), const2)]
    out_shape = [jax.ShapeDtypeStruct((T, D), F32),
                 jax.ShapeDtypeStruct((T * SUBLANES, LANES), F32),
                 jax.ShapeDtypeStruct((T, LANES), F32)]
    out_specs = [pl.BlockSpec((tm, D), row),
                 pl.BlockSpec((tm * SUBLANES, LANES), row),
                 pl.BlockSpec((tm, LANES), row)]
    return pl.pallas_call(
        functools.partial(_merge_kernel, tm=tm, split_x=split_x, n_ctx_tiles=nct - off),
        out_shape=out_shape, grid=(n_tiles,), in_specs=in_specs, out_specs=out_specs,
        scratch_shapes=[pltpu.VMEM((tm, D), BF16)],
        compiler_params=_cparams(("parallel",)),
        name="merge",
    )(h, *o_args, *(x if split_x else (x,)), mod, w["g_ffn"], w["w_gate"],
      w["b_gate"], w["w_branch"], w["w_out"], w["w_route"], w["b_route"])


def _moe_kernel(tab_ref, tok_ref, h2_ref, sw_ref, w1_ref, w3_ref, w2_ref, f_ref,
                xg_ref, y_ref, st_ref, *, blk, slots, tile_off):
    ti = pl.program_id(0)
    step = pl.program_id(1)

    @pl.when(step == 0)
    def _():
        f_ref[...] = jnp.zeros_like(f_ref)
        xg_ref[...] = jnp.zeros_like(xg_ref)

    tok_base = (ti + tile_off) * slots
    segs = []
    for j in range(MOE_EXPERTS_PER_STEP):
        base = ((ti + tile_off) * N_EXPERTS + step * MOE_EXPERTS_PER_STEP + j) * 2
        segs.append((tab_ref[base], tab_ref[base + 1]))

    def gather(j, off, n_rows):
        slot0 = tok_base + off

        def body(gi, c):
            r0 = gi * MOE_GROUP
            srcs = [pl.multiple_of(tok_ref[slot0 + r0 + u], SUBLANES) for u in range(MOE_GROUP)]
            for u in range(MOE_GROUP):
                dst = pl.multiple_of((r0 + u) * SUBLANES, SUBLANES)
                xg_ref[j, pl.ds(dst, SUBLANES), :] = h2_ref[pl.ds(srcs[u], SUBLANES), :]
            return c

        lax.fori_loop(0, n_rows // MOE_GROUP, body, 0)

    def ffn(j, off):
        xb = _load_token_major(xg_ref.at[j], blk).astype(BF16)
        a = jnp.dot(xb, w1_ref[j], preferred_element_type=F32)
        g = jnp.dot(xb, w3_ref[j], preferred_element_type=F32)
        hid = (a * _sigmoid(a) * g).astype(BF16)
        y = jnp.dot(hid, w2_ref[j], preferred_element_type=F32)
        y_ref[j] = y * sw_ref[0, pl.ds(off, blk), :]

    def scatter(j, off, n_rows):
        slot0 = tok_base + off

        def body(gi, c):
            r0 = pl.multiple_of(gi * MOE_GROUP, MOE_GROUP)
            dsts = [pl.multiple_of(tok_ref[slot0 + r0 + u], SUBLANES) for u in range(MOE_GROUP)]
            for ch in range(SUBLANES):
                st_ref[pl.ds(ch, MOE_GROUP, stride=SUBLANES), :] = y_ref[j, pl.ds(r0, MOE_GROUP),
                                                                         ch * LANES:(ch + 1) * LANES]
            vals = [f_ref[pl.ds(dsts[u], SUBLANES), :] + st_ref[u * SUBLANES:(u + 1) * SUBLANES, :]
                    for u in range(MOE_GROUP)]
            for u in reversed(range(MOE_GROUP)):
                f_ref[pl.ds(dsts[u], SUBLANES), :] = vals[u]
            return c

        lax.fori_loop(0, n_rows // MOE_GROUP, body, 0)

    for j, (seg0, n_pad) in enumerate(segs):
        gather(j, pl.multiple_of(seg0, MOE_GROUP), jnp.minimum(n_pad, blk))
    for j, (seg0, _) in enumerate(segs):
        ffn(j, pl.multiple_of(seg0, MOE_GROUP))
    for j, (seg0, n_pad) in enumerate(segs):
        scatter(j, pl.multiple_of(seg0, MOE_GROUP), jnp.minimum(n_pad, blk))

    for j, (seg0, n_pad) in enumerate(segs):
        def extra_block(b, carry, j=j, seg0=seg0, n_pad=n_pad):
            off = pl.multiple_of(seg0 + b * blk, MOE_GROUP)
            n_rows = jnp.minimum(n_pad - b * blk, blk)
            gather(j, off, n_rows)
            ffn(j, off)
            scatter(j, off, n_rows)
            return carry

        lax.fori_loop(1, (n_pad + blk - 1) // blk, extra_block, 0)


def _moe_call(lay, route, h2, w, *, with_ctx):
    T, D, tt = lay["T"], lay["D"], lay["tt"]
    blk = lay["moe_blk"]
    n_tiles_all = T // tt
    tile_off = 0 if with_ctx else lay["ncb"] * ROW_BLK // tt
    n_tiles = n_tiles_all - tile_off
    n_assign = 2 * tt
    n_fill = N_EXPERTS * MOE_GROUP
    slots = n_assign + n_fill + blk

    eid = route[:, 0:2].astype(jnp.int32).reshape(n_tiles_all, n_assign)
    wts = route[:, 2:4].reshape(n_tiles_all, n_assign)
    stride = pl.next_power_of_2(n_assign + MOE_GROUP)
    ex = jnp.arange(N_EXPERTS, dtype=jnp.int32)
    counts = jnp.sum(eid[:, :, None] == ex[None, None, :], axis=1, dtype=jnp.int32)
    n_dummy = (-counts) % MOE_GROUP
    fill_i = jnp.arange(MOE_GROUP, dtype=jnp.int32)[None, None, :]
    fill_key = jnp.where(fill_i < n_dummy[:, :, None],
                         ex[None, :, None] * stride + n_assign + fill_i,
                         N_EXPERTS * stride + ex[None, :, None] * MOE_GROUP + fill_i)
    keys = jnp.concatenate([eid * stride + jnp.arange(n_assign, dtype=jnp.int32)[None, :],
                            fill_key.reshape(n_tiles_all, n_fill)], axis=1)
    keys, slot_w = lax.sort(
        (keys, jnp.concatenate([wts, jnp.zeros((n_tiles_all, n_fill), F32)], axis=1)),
        dimension=1, is_stable=False, num_keys=1)
    a_sorted = keys % stride
    is_real = (a_sorted < n_assign) & (keys < N_EXPERTS * stride)
    slot_tok = jnp.where(is_real, a_sorted // 2 * SUBLANES, 0)
    slot_tok = jnp.pad(slot_tok, ((0, 0), (0, blk)))
    slot_w = jnp.pad(slot_w, ((0, 0), (0, blk)))
    padded = counts + n_dummy
    seg0 = jnp.cumsum(padded, axis=1) - padded
    tab = jnp.stack([seg0, padded], axis=-1).reshape(-1).astype(jnp.int32)

    return pl.pallas_call(
        functools.partial(_moe_kernel, blk=blk, slots=slots, tile_off=tile_off),
        out_shape=jax.ShapeDtypeStruct((T * SUBLANES, LANES), F32),
        grid_spec=pltpu.PrefetchScalarGridSpec(
            num_scalar_prefetch=2,
            grid=(n_tiles, N_EXPERTS // MOE_EXPERTS_PER_STEP),
            in_specs=[
                pl.BlockSpec((tt * SUBLANES, LANES), lambda t, e, *_: (t + tile_off, 0)),
                pl.BlockSpec((1, slots, 1), lambda t, e, *_: (t + tile_off, 0, 0)),
                pl.BlockSpec((MOE_EXPERTS_PER_STEP, D, EXPERT_FF), lambda t, e, *_: (e, 0, 0)),
                pl.BlockSpec((MOE_EXPERTS_PER_STEP, D, EXPERT_FF), lambda t, e, *_: (e, 0, 0)),
                pl.BlockSpec((MOE_EXPERTS_PER_STEP, EXPERT_FF, D), lambda t, e, *_: (e, 0, 0)),
            ],
            out_specs=pl.BlockSpec((tt * SUBLANES, LANES), lambda t, e, *_: (t + tile_off, 0)),
            scratch_shapes=[pltpu.VMEM((MOE_EXPERTS_PER_STEP, blk * SUBLANES, LANES), F32),
                            pltpu.VMEM((MOE_EXPERTS_PER_STEP, blk, D), F32),
                            pltpu.VMEM((MOE_GROUP * SUBLANES, LANES), F32)],
        ),
        compiler_params=_cparams(("parallel", "arbitrary")),
        name="moe_experts",
    )(tab, slot_tok.reshape(-1), h2, slot_w.reshape(n_tiles_all, slots, 1),
      w["w_ff1"], w["w_ff3"], w["w_ff2"])


def _final_kernel(x_ref, f_ref, mod_ref, g_ref, o_ref, *, tm):
    xf = x_ref[...] + mod_ref[0, 5:6, :] * _load_token_major(f_ref, tm)
    o_ref[...] = _rms(xf) * g_ref[...]


def _final_call(lay, layer, x1, f, mod, g_final):
    T, D, tm = lay["T"], lay["D"], lay["tm"]
    nct, tpb = lay["nct"], lay["tpb"]
    n_lat = T // tm - nct
    return pl.pallas_call(
        functools.partial(_final_kernel, tm=tm),
        out_shape=jax.ShapeDtypeStruct((n_lat * tm, D), F32),
        grid=(n_lat,),
        in_specs=[pl.BlockSpec((tm, D), lambda i: (i + nct, 0)),
                  pl.BlockSpec((tm * SUBLANES, LANES), lambda i: (i + nct, 0)),
                  pl.BlockSpec((1, 6, D), lambda i: (layer * lay["mod_rows"] + i // tpb, 0, 0)),
                  pl.BlockSpec((1, D), lambda i: (0, 0))],
        out_specs=pl.BlockSpec((tm, D), lambda i: (i, 0)),
        compiler_params=_cparams(("parallel",)),
        name="final_norm",
    )(x1, f, mod, g_final)


def _select_cols(wm, segs, scale=None):
    parts = []
    for k, (start, width) in enumerate(segs):
        if start is None:
            parts.append(jnp.zeros((wm.shape[0], width), wm.dtype))
        else:
            blk = wm[:, start:start + width]
            parts.append(blk if scale is None or scale[k] is None else blk * scale[k])
    return jnp.concatenate(parts, axis=1)


def _prep_layer(l, p):
    a_cols = A_Q_RANK + A_KV_RANK + A_ROPE
    b_off = a_cols
    c_off = b_off + 512
    d_off = c_off + 768
    qk_scale = HEAD_DIM ** -0.5 * LOG2E
    gqa_q = lambda off: [(off + hh * HEAD_DIM, HEAD_DIM) for hh in _GQA_PERM]
    segs = ([(0, 256), (256, 128), (None, 64), (384, 32), (None, 32)]
            + gqa_q(b_off) + [(b_off + 256, 128), (b_off + 384, 128)]
            + [(c_off, 256), (c_off + 256, 256), (c_off + 512, 256)]
            + gqa_q(d_off) + [(d_off + 256, 128), (d_off + 384, 128)])
    scale = [None] * len(segs)
    for k in (5, 6, 7, 8, 11):
        scale[k] = qk_scale
    w_in = _select_cols(p["w_in"][l], segs, scale).astype(BF16)
    assert w_in.shape[1] == PROJ_COLS

    hq = A_NOPE + A_ROPE
    segs_q = []
    for hh in range(N_HEADS):
        segs_q += [(hh * hq, hq), (None, LANES - hq)]
    w_q_b = _select_cols(p["w_q_b"][l], segs_q).astype(BF16)
    hk = A_NOPE + HEAD_DIM
    segs_k = []
    for hh in range(N_HEADS):
        segs_k += [(hh * hk, A_NOPE), (None, LANES - A_NOPE)]
    segs_k += [(hh * hk + A_NOPE, HEAD_DIM) for hh in range(N_HEADS)]
    w_kv_b = _select_cols(p["w_kv_b"][l], segs_k).astype(BF16)

    wb = p["w_branch"][l]
    perm_rows = lambda m: jnp.concatenate([m[hh * HEAD_DIM:(hh + 1) * HEAD_DIM] for hh in _GQA_PERM], axis=0)
    w_branch = jnp.stack([wb[0], perm_rows(wb[1]), wb[2], perm_rows(wb[3])]).astype(BF16)

    d = p["w_in"].shape[1]
    w_route = jnp.zeros((d, LANES), F32)
    w_route = w_route.at[:, :N_EXPERTS].set(p["w_router"][l]).at[:, N_EXPERTS:N_EXPERTS + N_GROUPS].set(p["w_group"][l])
    b_route = jnp.zeros((1, LANES), F32)
    b_route = b_route.at[0, :N_EXPERTS].set(p["b_router"][l]).at[0, N_EXPERTS:N_EXPERTS + N_GROUPS].set(p["b_group"][l])
    return {
        "g_mix": p["g_norm_mix"][l][None, :],
        "w_in": w_in,
        "g_q_a": p["g_q_a"][l][None, :],
        "w_q_b": w_q_b,
        "g_kv_a": p["g_kv_a"][l][None, :],
        "w_kv_b": w_kv_b,
        "g_q_d": (jnp.tile(p["g_q_d"][l], 2) * qk_scale)[None, :],
        "g_k_d": jnp.tile(p["g_k_d"][l], 2)[None, :],
        "sink": p["sink_b"][l],
        "rpb": p["rpb_c"][l],
        "w_gate": p["w_gate"][l].astype(BF16),
        "b_gate": p["b_gate"][l][:, None, :],
        "w_branch": w_branch,
        "w_out": p["w_out"][l].astype(BF16),
        "g_ffn": p["g_norm_ffn"][l][None, :],
        "w_route": jnp.stack([w_route.astype(BF16),
                              (w_route - w_route.astype(BF16).astype(F32)).astype(BF16)]),
        "b_route": b_route,
        "w_ff1": p["w_ff1"][l].astype(BF16),
        "w_ff3": p["w_ff3"][l].astype(BF16),
        "w_ff2": p["w_ff2"][l].astype(BF16),
    }


def _rope_tables(S, tm):
    t = np.arange(S)
    rows = (t // GRID_W).astype(np.float32)
    cols = (t % GRID_W).astype(np.float32)

    def cs(rot):
        half = rot // 2
        inv = np.float32(ROPE_THETA) ** (-np.arange(0, half, 2, dtype=np.float32) / np.float32(half))
        ar_, ac_ = rows[:, None] * inv, cols[:, None] * inv
        cos = np.concatenate([np.cos(ar_), np.cos(ar_), np.cos(ac_), np.cos(ac_)], axis=-1)
        sin = np.concatenate([-np.sin(ar_), np.sin(ar_), -np.sin(ac_), np.sin(ac_)], axis=-1)
        return cos.astype(np.float32), sin.astype(np.float32)

    cos64, sin64 = cs(HEAD_DIM)
    cos32, sin32 = cs(A_ROPE)
    ones = lambda n: np.ones((S, n), np.float32)
    zeros = lambda n: np.zeros((S, n), np.float32)
    tabs = {
        "cos_h": np.concatenate([cos64, cos64], axis=-1),
        "sin_h": np.concatenate([sin64, sin64], axis=-1),
        "cos_a": np.concatenate([ones(A_NOPE), cos32, ones(LANES - A_NOPE - A_ROPE)], axis=-1),
        "sin_a": np.concatenate([zeros(A_NOPE), sin32, zeros(LANES - A_NOPE - A_ROPE)], axis=-1),
    }
    ident = {"cos_h": 1.0, "sin_h": 0.0, "cos_a": 1.0, "sin_a": 0.0}
    return {k: jnp.asarray(np.concatenate([v, np.full((tm, LANES), ident[k], np.float32)], axis=0))
            for k, v in tabs.items()}


def _layout(B, S, n_ctx, D):
    assert n_ctx == ROW_BLK and S % 1024 == 0 and S // GRID_W >= 3 * (ROW_BLK // GRID_W) and B < 16
    T = B * (n_ctx + S)
    tm = 512 if (B * n_ctx) % 512 == 0 else 256
    def largest_tile(cands):
        return next(t for t in cands if (B * n_ctx) % t == 0 and S % t == 0)

    tq_dense = largest_tile((1024, 512, 256))
    tk_dense = largest_tile((2048, 1024, 512, 256))
    tt = 2048 if (B * n_ctx) % 2048 == 0 else B * n_ctx
    assert S % tt == 0
    return {
        "B": B, "S": S, "D": D, "T": T, "tm": tm,
        "ncb": B * n_ctx // ROW_BLK,
        "lb": S // ROW_BLK,
        "nct": B * n_ctx // tm,
        "tpb": S // tm,
        "tk_dense": tk_dense,
        "tq_dense": tq_dense,
        "tq_band": tm,
        "tt": tt,
        "moe_blk": 160,
        "mod_rows": 16,
    }


def kernel(x, c, ctx, c_ctx, w_mod, b_mod, g_norm_mix, w_in, g_q_a, w_q_b, g_kv_a, w_kv_b, sink_b, rpb_c,
           g_q_d, g_k_d, w_gate, b_gate, w_branch, w_out, g_norm_ffn, w_group, b_group, w_router, b_router,
           w_ff1, w_ff3, w_ff2, g_final):
    B, S, D = x.shape
    n_ctx = ctx.shape[1]
    depth = w_mod.shape[0]
    lay = _layout(B, S, n_ctx, D)
    params = dict(w_in=w_in, g_norm_mix=g_norm_mix, g_q_a=g_q_a, w_q_b=w_q_b, g_kv_a=g_kv_a, w_kv_b=w_kv_b,
                  sink_b=sink_b, rpb_c=rpb_c, g_q_d=g_q_d, g_k_d=g_k_d, w_gate=w_gate, b_gate=b_gate,
                  w_branch=w_branch, w_out=w_out, g_norm_ffn=g_norm_ffn, w_group=w_group, b_group=b_group,
                  w_router=w_router, b_router=b_router, w_ff1=w_ff1, w_ff3=w_ff3, w_ff2=w_ff2)

    c_all = jnp.zeros((lay["mod_rows"], D), F32).at[:B].set(c).at[B].set(c_ctx)
    mod = _modulation(c_all, w_mod, b_mod).reshape(depth * lay["mod_rows"], 6, D)
    tabs = _rope_tables(S, lay["tm"])
    win_bias = _window_bias(S, lay["tq_band"])
    xf = (ctx.reshape(B * n_ctx, D), x.reshape(B * S, D))

    f = None
    for l in range(depth):
        with_ctx = l < depth - 1
        w = _prep_layer(l, params)
        xf, pr = _proj_call(lay, xf, f, mod, mod, l, w, tabs)
        o = {}
        for kind, kl in (("A", "a"), ("B", "b"), ("C", "c"), ("D", "d")):
            q, k, v = pr["q" + kl], pr["k" + kl], pr["v" + kl]
            sink = w["sink"] if kind == "B" else None
            bias = win_bias if kind == "B" else None
            if kind == "C":
                bias = _neighbourhood_bias(w["rpb"], S // GRID_W, lay["tq_band"])
            o_ctx = _attn_context_call(lay, kind, q, k, v, sink=sink) if with_ctx else None
            o[kind] = (o_ctx, _attn_latent_call(lay, kind, q, k, v, sink=sink, bias=bias))
        xf, h2, route = _merge_call(lay, l, pr["h"], o, xf, mod, w, with_ctx=with_ctx)
        f = _moe_call(lay, route, h2, w, with_ctx=with_ctx)
    out = _final_call(lay, depth - 1, xf, f, mod, g_final[None, :])
    return out.reshape(B, S, D)
```

```python
import functools

import numpy as np
import jax
import jax.numpy as jnp
from jax import lax
from jax.experimental import pallas as pl
from jax.experimental.pallas import tpu as pltpu

F32 = jnp.float32
BF16 = jnp.bfloat16
HIGHEST = lax.Precision.HIGHEST

GRID_W = 64
ROPE_THETA = 10000.0
EPS = 1e-6
NEG_INF = -1e30
LOG2E = 1.4426950408889634
HEAD_DIM = 64
N_HEADS = 4
BRANCH_W = 256
A_Q_RANK = 256
A_KV_RANK = 128
A_NOPE = 64
A_ROPE = 32
NA_KH = 8
NA_KW = 16
WINDOW = 128
N_GROUPS = 4
EXPERTS_PER_GROUP = 8
N_EXPERTS = 32
EXPERT_FF = 256

LANES = 128
SUBLANES = 8
ROW_BLK = 256
MERGE_TN = 256
MOE_EXPERTS_PER_STEP = 4
MOE_GROUP = 16
VMEM_LIMIT = 56 * 1024 * 1024

_PROJ_GROUPS = (("cq", 256), ("ckv", 128), ("kr", 128), ("qb", 256), ("kb", 128), ("vb", 128),
                ("qc", 256), ("kc", 256), ("vc", 256), ("qd", 256), ("kd", 128), ("vd", 128))
_PROJ_OFF = {}
_o = 0
for _n, _w in _PROJ_GROUPS:
    _PROJ_OFF[_n] = (_o, _w)
    _o += _w
PROJ_COLS = _o
_GQA_PERM = (0, 2, 1, 3)


def _cparams(sem):
    return pltpu.CompilerParams(dimension_semantics=sem, vmem_limit_bytes=VMEM_LIMIT)


def _lane_iota(shape):
    return lax.broadcasted_iota(jnp.int32, shape, len(shape) - 1)


def _sigmoid(x):
    return 1.0 / (1.0 + jnp.exp(-x))


def _mod_kernel(c_ref, w_ref, b_ref, o_ref):
    cf = c_ref[...]
    s = cf * _sigmoid(cf)
    o_ref[0] = jnp.dot(s, w_ref[0], precision=HIGHEST, preferred_element_type=F32) + b_ref[0]


def _modulation(c_all, w_mod, b_mod):
    n_layers, d, n_out = w_mod.shape
    rows = c_all.shape[0]
    tn = n_out // 2
    return pl.pallas_call(
        _mod_kernel,
        out_shape=jax.ShapeDtypeStruct((n_layers, rows, n_out), F32),
        grid=(n_layers, n_out // tn),
        in_specs=[pl.BlockSpec((rows, d), lambda l, j: (0, 0)),
                  pl.BlockSpec((1, d, tn), lambda l, j: (l, 0, j)),
                  pl.BlockSpec((1, 1, tn), lambda l, j: (l, 0, j))],
        out_specs=pl.BlockSpec((1, rows, tn), lambda l, j: (l, 0, j)),
        compiler_params=_cparams(("arbitrary", "arbitrary")),
        name="modulation",
    )(c_all, w_mod, b_mod.reshape(n_layers, 1, n_out))


def _rms(x):
    return x * lax.rsqrt(jnp.mean(x * x, axis=-1, keepdims=True) + EPS)


def _swap_blocks(x, blk):
    lane = _lane_iota(x.shape)
    up = pltpu.roll(x, LANES - blk, 1)
    dn = pltpu.roll(x, blk, 1)
    return jnp.where((lane // blk) % 2 == 0, up, dn)


def _rope(x, cos, sin, blk):
    return x * cos + _swap_blocks(x, blk) * sin


def _pair_norm(x, g):
    lo = _lane_iota(x.shape) < HEAD_DIM
    sq = x * x
    s_lo = jnp.sum(jnp.where(lo, sq, 0.0), axis=-1, keepdims=True)
    s_hi = jnp.sum(jnp.where(lo, 0.0, sq), axis=-1, keepdims=True)
    ms = jnp.where(lo, s_lo, s_hi) * (1.0 / HEAD_DIM)
    return x * lax.rsqrt(ms + EPS) * g


def _load_token_major(ref, rows):
    return jnp.concatenate(
        [ref[pl.ds(c, rows, stride=SUBLANES), :] for c in range(SUBLANES)], axis=-1)


def _tile_rows(xc_ref, xl_ref, is_ctx):
    return jnp.where(is_ctx, xc_ref[...], xl_ref[...])


def _proj_kernel(*refs, with_f, tm, scale_a, nct):
    it = iter(refs)
    if with_f:
        x_ref = next(it)
        f_ref = next(it)
        modp_ref = next(it)
    else:
        xc_ref, xl_ref = next(it), next(it)
    mod_ref = next(it)
    gmix_ref, win_ref, gqa_ref, wqb_ref, gkva_ref, wkvb_ref, gqd_ref, gkd_ref = (next(it) for _ in range(8))
    cosh_ref, sinh_ref, cosa_ref, sina_ref = (next(it) for _ in range(4))
    if with_f:
        x2_ref = next(it)
    h_ref = next(it)
    qa_ref, ka_ref, va_ref, qb_ref, kb_ref, vb_ref, qc_ref, kc_ref, vc_ref, qd_ref, kd_ref, vd_ref = (
        next(it) for _ in range(12))

    if with_f:
        xf = x_ref[...] + modp_ref[0, 5:6, :] * _load_token_major(f_ref, tm)
        x2_ref[...] = xf
    else:
        xf = _tile_rows(xc_ref, xl_ref, pl.program_id(0) < nct)
    h = _rms(xf) * gmix_ref[...] * (1.0 + mod_ref[0, 1:2, :]) + mod_ref[0, 0:1, :]
    hb = h.astype(BF16)
    h_ref[...] = hb
    p = jnp.dot(hb, win_ref[...], preferred_element_type=F32)

    def grp(name):
        o, w = _PROJ_OFF[name]
        return p[:, o:o + w]

    cosh, sinh = cosh_ref[...], sinh_ref[...]
    cosa, sina = cosa_ref[...], sina_ref[...]

    cq = (_rms(grp("cq")) * gqa_ref[...]).astype(BF16)
    qa = jnp.dot(cq, wqb_ref[...], preferred_element_type=F32)
    for hd in range(N_HEADS):
        sl = slice(hd * LANES, (hd + 1) * LANES)
        qa_ref[:, sl] = (_rope(qa[:, sl], cosa, sina, 8) * scale_a).astype(BF16)
    ckv = (_rms(grp("ckv")) * gkva_ref[...]).astype(BF16)
    kva = jnp.dot(ckv, wkvb_ref[...], preferred_element_type=F32)
    kr = _rope(grp("kr"), cosa, sina, 8)
    for hd in range(N_HEADS):
        sl = slice(hd * LANES, (hd + 1) * LANES)
        ka_ref[:, sl] = (kva[:, sl] + kr).astype(BF16)
    va_ref[...] = kva[:, N_HEADS * LANES:].astype(BF16)

    qb = grp("qb")
    for j in range(2):
        sl = slice(j * LANES, (j + 1) * LANES)
        qb_ref[:, sl] = _rope(qb[:, sl], cosh, sinh, 16).astype(BF16)
    kb_ref[...] = _rope(grp("kb"), cosh, sinh, 16).astype(BF16)
    vb_ref[...] = grp("vb").astype(BF16)

    qc_ref[...] = grp("qc").astype(BF16)
    kc_ref[...] = grp("kc").astype(BF16)
    vc_ref[...] = grp("vc").astype(BF16)

    qd = grp("qd")
    for j in range(2):
        sl = slice(j * LANES, (j + 1) * LANES)
        qd_ref[:, sl] = _rope(_pair_norm(qd[:, sl], gqd_ref[...]), cosh, sinh, 16).astype(BF16)
    kd_ref[...] = _rope(_pair_norm(grp("kd"), gkd_ref[...]), cosh, sinh, 16).astype(BF16)
    vd_ref[...] = grp("vd").astype(BF16)


def _split_row_specs(lay, width, off=0):
    tm, nct = lay["tm"], lay["nct"]
    return [pl.BlockSpec((tm, width), lambda i: (jnp.minimum(i + off, nct - 1), 0)),
            pl.BlockSpec((tm, width), lambda i: (jnp.maximum(i + off - nct, 0), 0))]


def _proj_call(lay, x, f, modp, mod, layer, w, tabs):
    T, D, tm = lay["T"], lay["D"], lay["tm"]
    nct, tpb, B = lay["nct"], lay["tpb"], lay["B"]
    with_f = f is not None
    n_tiles = T // tm

    def mod_row(i):
        return jnp.where(i < nct, B, (i - nct) // tpb)

    def tab_blk(i):
        return jnp.where(i < nct, tpb, (i - nct) % tpb)

    row = lambda i: (i, 0)
    const = lambda i: (0, 0)
    if with_f:
        in_specs = [pl.BlockSpec((tm, D), row),
                    pl.BlockSpec((tm * SUBLANES, LANES), row),
                    pl.BlockSpec((1, 6, D), lambda i: ((layer - 1) * lay["mod_rows"] + mod_row(i), 0, 0))]
        args = [x, f, modp]
    else:
        in_specs = _split_row_specs(lay, D)
        args = list(x)
    in_specs += [pl.BlockSpec((1, 6, D), lambda i: (layer * lay["mod_rows"] + mod_row(i), 0, 0))]
    args += [mod]
    for name in ("g_mix", "w_in", "g_q_a", "w_q_b", "g_kv_a", "w_kv_b", "g_q_d", "g_k_d"):
        a = w[name]
        in_specs.append(pl.BlockSpec(a.shape, const))
        args.append(a)
    for tname in ("cos_h", "sin_h", "cos_a", "sin_a"):
        in_specs.append(pl.BlockSpec((tm, LANES), lambda i: (tab_blk(i), 0)))
        args.append(tabs[tname])

    widths = [("h", D), ("qa", 512), ("ka", 512), ("va", 256), ("qb", 256), ("kb", 128), ("vb", 128),
              ("qc", 256), ("kc", 256), ("vc", 256), ("qd", 256), ("kd", 128), ("vd", 128)]
    out_shape, out_specs = [], []
    if with_f:
        out_shape.append(jax.ShapeDtypeStruct((T, D), F32))
        out_specs.append(pl.BlockSpec((tm, D), row))
    for _, wd in widths:
        out_shape.append(jax.ShapeDtypeStruct((T, wd), BF16))
        out_specs.append(pl.BlockSpec((tm, wd), row))

    outs = pl.pallas_call(
        functools.partial(_proj_kernel, with_f=with_f, tm=tm, nct=nct,
                          scale_a=float((A_NOPE + A_ROPE) ** -0.5 * LOG2E)),
        out_shape=out_shape, grid=(n_tiles,), in_specs=in_specs, out_specs=out_specs,
        compiler_params=_cparams(("parallel",)),
        name="proj_in",
    )(*args)
    outs = list(outs)
    x2 = outs.pop(0) if with_f else x
    names = [n for n, _ in widths]
    return x2, dict(zip(names, outs))


_NT = (((1,), (1,)), ((), ()))


def _head_plan(kind):
    if kind == "A":
        return tuple((r, None, r, r // 2) for r in range(N_HEADS))
    if kind == "C":
        return tuple((r // 2, r % 2, r // 2, r // 2) for r in range(N_HEADS))
    return tuple((r // 2, r % 2, 0, 0) for r in range(N_HEADS))


def _head_query(q_ref, plan_r):
    qt, half, _, _ = plan_r
    src = q_ref[:, qt * LANES:(qt + 1) * LANES]
    if half is None:
        return src
    lane = _lane_iota(src.shape)
    keep = (lane < HEAD_DIM) if half == 0 else (lane >= HEAD_DIM)
    return jnp.where(keep, src, jnp.zeros_like(src))


def _score_chunks(q, k_blocks, bias_blocks):
    chunks = []
    for kb, bb in zip(k_blocks, bias_blocks):
        s = lax.dot_general(q, kb, _NT, preferred_element_type=F32)
        if bb is not None:
            s = s + bb
        chunks += [s[:, c * LANES:(c + 1) * LANES] for c in range(s.shape[1] // LANES)]
    return chunks


def _row_max(chunks):
    m = functools.reduce(jnp.maximum, chunks)
    return jnp.broadcast_to(jnp.max(m, axis=-1, keepdims=True), m.shape)


def _weighted_values(p_chunks, v_blocks, half):
    pv, idx = None, 0
    for vb in v_blocks:
        n = vb.shape[0] // LANES
        p = jnp.concatenate(p_chunks[idx:idx + n], axis=1).astype(BF16)
        idx += n
        lane = _lane_iota(vb.shape)
        own = (lane < HEAD_DIM) if half == 0 else (lane >= HEAD_DIM)
        d = jnp.dot(p, jnp.where(own, vb, jnp.ones_like(vb)), preferred_element_type=F32)
        pv = d if pv is None else pv + d
    return pv


def _softmax_once(s, v_aug, bias_blocks, sink):
    chunks = []
    for c in range(s.shape[1] // LANES):
        sc = s[:, c * LANES:(c + 1) * LANES]
        bb = bias_blocks[c * LANES // ROW_BLK]
        if bb is not None:
            lo = c * LANES % ROW_BLK
            sc = sc + bb[:, lo:lo + LANES]
        chunks.append(sc)
    m = _row_max(chunks)
    if sink is not None:
        m = jnp.maximum(m, sink)
    p = jnp.concatenate([jnp.exp2(c - m) for c in chunks], axis=1).astype(BF16)
    pv = jnp.dot(p, v_aug, preferred_element_type=F32)
    l = pltpu.roll(pv, HEAD_DIM, 1)
    if sink is not None:
        l = l + jnp.exp2(sink - m)
    return pv / l


def _store_heads(o_ref, outs):
    lane = _lane_iota(outs[0].shape)
    for g in range(2):
        o_ref[:, g * LANES:(g + 1) * LANES] = jnp.where(
            lane < HEAD_DIM, outs[2 * g], outs[2 * g + 1]).astype(o_ref.dtype)


def _attn_band_kernel(*refs, kind, lb, n_kv, tq):
    it = iter(refs)
    q_ref = next(it)
    k_refs = [next(it) for _ in range(n_kv)]
    v_refs = [next(it) for _ in range(n_kv)]
    bias_ref = next(it) if n_kv > 1 else None
    sink_ref = next(it) if kind == "B" else None
    o_ref = next(it)
    t = pl.program_id(0) % lb
    var = jnp.where(t == 0, 0, jnp.where(t == lb - 1, 2, 1))
    plan = _head_plan(kind)
    scores = {}
    for kt in sorted({p[2] for p in plan}):
        heads = [r for r in range(N_HEADS) if plan[r][2] == kt]
        k_all = jnp.concatenate([k[:, kt * LANES:(kt + 1) * LANES] for k in k_refs], axis=0)
        q_all = jnp.concatenate([_head_query(q_ref, plan[r]) for r in heads], axis=0)
        s_all = lax.dot_general(q_all, k_all, _NT, preferred_element_type=F32)
        for j, r in enumerate(heads):
            scores[r] = s_all[j * tq:(j + 1) * tq, :]
    outs = []
    v_aug = {}
    for r, plan_r in enumerate(plan):
        _, _, kt, vt = plan_r
        if (vt, r % 2) not in v_aug:
            vv = jnp.concatenate([v[:, vt * LANES:(vt + 1) * LANES] for v in v_refs], axis=0)
            lane = _lane_iota(vv.shape)
            own = (lane < HEAD_DIM) if r % 2 == 0 else (lane >= HEAD_DIM)
            v_aug[(vt, r % 2)] = jnp.where(own, vv, jnp.ones_like(vv))
        bias = [None]
        for j in range(n_kv - 1):
            if kind == "C":
                bias.append(bias_ref[var, j, r * tq:(r + 1) * tq, :])
            else:
                bias.append(bias_ref[var, j])
        sink = sink_ref[_GQA_PERM[r]] * LOG2E if kind == "B" else None
        outs.append(_softmax_once(scores[r], v_aug[(vt, r % 2)], bias, sink))
    _store_heads(o_ref, outs)


def _attn_dense_kernel(q_ref, kc_ref, vc_ref, kl_ref, vl_ref, o_ref, qs_ref, m_ref, acc_ref,
                       *, kind, n_steps, tq):
    s = pl.program_id(1)
    plan = _head_plan(kind)

    def update(first):
        if first:
            for r in range(N_HEADS):
                qs_ref[r * tq:(r + 1) * tq, :] = _head_query(q_ref, plan[r])
        scores = {}
        for kt in sorted({p[2] for p in plan}):
            heads = [r for r in range(N_HEADS) if plan[r][2] == kt]
            ksl = slice(kt * LANES, (kt + 1) * LANES)
            k_blocks = [kc_ref[:, ksl], kl_ref[:, ksl]] if first else [kl_ref[:, ksl]]
            q_all = qs_ref[heads[0] * tq:(heads[-1] + 1) * tq, :]
            chunks = _score_chunks(q_all, k_blocks, [None] * len(k_blocks))
            for j, r in enumerate(heads):
                scores[r] = [c[j * tq:(j + 1) * tq, :] for c in chunks]
        for r, (_, _, kt, vt) in enumerate(plan):
            rows = slice(r * tq, (r + 1) * tq)
            vsl = slice(vt * LANES, (vt + 1) * LANES)
            v_blocks = [vc_ref[:, vsl], vl_ref[:, vsl]] if first else [vl_ref[:, vsl]]
            chunks = scores[r]
            m_cur = _row_max(chunks)
            if first:
                m_new = m_cur
            else:
                m_prev = m_ref[rows, :]
                m_new = jnp.maximum(m_prev, m_cur)
                alpha = jnp.exp2(m_prev - m_new)
            p = [jnp.exp2(c - m_new) for c in chunks]
            pv = _weighted_values(p, v_blocks, r % 2)
            if first:
                acc_ref[rows, :] = pv
            else:
                acc_ref[rows, :] = alpha * acc_ref[rows, :] + pv
            m_ref[rows, :] = m_new

    @pl.when(s == 0)
    def _():
        update(True)

    @pl.when(s > 0)
    def _():
        update(False)

    @pl.when(s == n_steps - 1)
    def _():
        outs = []
        for r in range(N_HEADS):
            rows = slice(r * tq, (r + 1) * tq)
            acc = acc_ref[rows, :]
            outs.append(acc / pltpu.roll(acc, HEAD_DIM, 1))
        _store_heads(o_ref, outs)


def _attn_latent_call(lay, kind, q, k, v, *, sink=None, bias=None):
    T, B, ncb, lb, S = lay["T"], lay["B"], lay["ncb"], lay["lb"], lay["S"]
    tq = ROW_BLK
    qw, kw, vw = q.shape[1], k.shape[1], v.shape[1]
    out_shape = jax.ShapeDtypeStruct((B * S, BRANCH_W), BF16)
    if kind in ("A", "D"):
        tk, tq = lay["tk_dense"], lay["tq_dense"]
        n_steps = S // tk
        lat0 = ncb * ROW_BLK // tk
        q0 = ncb * ROW_BLK // tq
        qpb = S // tq
        lat_blk = lambda i, s: (lat0 + (i // qpb) * n_steps + s, 0)
        return pl.pallas_call(
            functools.partial(_attn_dense_kernel, kind=kind, n_steps=n_steps, tq=tq),
            out_shape=out_shape,
            grid=(B * qpb, n_steps),
            in_specs=[pl.BlockSpec((tq, qw), lambda i, s: (q0 + i, 0)),
                      pl.BlockSpec((ROW_BLK, kw), lambda i, s: (i // qpb, 0)),
                      pl.BlockSpec((ROW_BLK, vw), lambda i, s: (i // qpb, 0)),
                      pl.BlockSpec((tk, kw), lat_blk),
                      pl.BlockSpec((tk, vw), lat_blk)],
            out_specs=pl.BlockSpec((tq, BRANCH_W), lambda i, s: (i, 0)),
            scratch_shapes=[pltpu.VMEM((N_HEADS * tq, LANES), BF16),
                            pltpu.VMEM((N_HEADS * tq, LANES), F32),
                            pltpu.VMEM((N_HEADS * tq, LANES), F32)],
            compiler_params=_cparams(("parallel", "arbitrary")),
            name="attn_" + kind,
        )(q, k, v, k, v)

    tq = lay["tq_band"]
    qpb = S // tq
    q0 = ncb * ROW_BLK // tq
    bpq = tq // ROW_BLK

    def nb(i, d):
        return (ncb + (i // qpb) * lb + jnp.clip((i % qpb) * bpq + d, 0, lb - 1), 0)

    kv_maps = [lambda i: (i // qpb, 0)] + [functools.partial(nb, d=d) for d in _band_offsets(tq)]
    n_kv = len(kv_maps)
    in_specs = [pl.BlockSpec((tq, qw), lambda i: (q0 + i, 0))]
    in_specs += [pl.BlockSpec((ROW_BLK, kw), m) for m in kv_maps]
    in_specs += [pl.BlockSpec((ROW_BLK, vw), m) for m in kv_maps]
    in_specs.append(pl.BlockSpec(bias.shape, lambda i: (0,) * bias.ndim, pipeline_mode=pl.Buffered(1)))
    args = [q] + [k] * n_kv + [v] * n_kv + [bias]
    if kind == "B":
        in_specs.append(pl.BlockSpec(memory_space=pltpu.SMEM))
        args.append(sink)
    return pl.pallas_call(
        functools.partial(_attn_band_kernel, kind=kind, lb=qpb, n_kv=n_kv, tq=tq),
        out_shape=out_shape,
        grid=(B * qpb,),
        in_specs=in_specs,
        out_specs=pl.BlockSpec((tq, BRANCH_W), lambda i: (i, 0)),
        compiler_params=_cparams(("parallel",)),
        name="attn_" + kind,
    )(*args)


def _attn_context_call(lay, kind, q, k, v, *, sink=None):
    ncb = lay["ncb"]
    qw, kw, vw = q.shape[1], k.shape[1], v.shape[1]
    blk = lambda i: (i, 0)
    in_specs = [pl.BlockSpec((ROW_BLK, qw), blk), pl.BlockSpec((ROW_BLK, kw), blk),
                pl.BlockSpec((ROW_BLK, vw), blk)]
    args = [q, k, v]
    if kind == "B":
        in_specs.append(pl.BlockSpec(memory_space=pltpu.SMEM))
        args.append(sink)
    return pl.pallas_call(
        functools.partial(_attn_band_kernel, kind=kind, lb=1, n_kv=1, tq=ROW_BLK),
        out_shape=jax.ShapeDtypeStruct((ncb * ROW_BLK, BRANCH_W), BF16),
        grid=(ncb,),
        in_specs=in_specs,
        out_specs=pl.BlockSpec((ROW_BLK, BRANCH_W), blk),
        compiler_params=_cparams(("parallel",)),
        name="attn_ctx_" + kind,
    )(*args)


def _band_offsets(tq):
    return tuple(range(-1, tq // ROW_BLK + 1))


def _window_bias(S, tq):
    lb, qpb, bpq = S // ROW_BLK, S // tq, tq // ROW_BLK
    offs = _band_offsets(tq)
    qa, ka = np.arange(tq), np.arange(ROW_BLK)
    out = np.full((3, len(offs), tq, ROW_BLK), NEG_INF, np.float32)
    for vi, t_rep in enumerate((0, 1, qpb - 1)):
        for di, d in enumerate(offs):
            kt = t_rep * bpq + d
            if not 0 <= kt < lb:
                continue
            qpos = t_rep * tq + qa
            kpos = kt * ROW_BLK + ka
            ok = np.abs(qpos[:, None] - kpos[None, :]) <= WINDOW
            out[vi, di] = np.where(ok, 0.0, NEG_INF)
    return jnp.asarray(out)


def _neighbourhood_bias(rpb, rows_total, tq):
    lb = rows_total * GRID_W // ROW_BLK
    qpb, bpq = rows_total * GRID_W // tq, tq // ROW_BLK
    rpt = ROW_BLK // GRID_W
    rpq = tq // GRID_W
    kh = min(NA_KH, rows_total)
    qa, ka = np.arange(tq), np.arange(ROW_BLK)
    q_sub, q_col = qa // GRID_W, qa % GRID_W
    k_sub, k_col = ka // GRID_W, ka % GRID_W
    n_dr, n_dc = 2 * NA_KH - 1, 2 * NA_KW - 1
    col = np.arange(GRID_W)
    dc = np.clip(col[None, :] - col[:, None], -(NA_KW - 1), NA_KW - 1) + NA_KW - 1
    hot_c = (dc[:, :, None] == np.arange(n_dc)).astype(np.float32)
    by_col = jnp.einsum("huv,cdv->hucd", rpb.astype(F32) * LOG2E, jnp.asarray(hot_c),
                        precision=HIGHEST)
    offs = _band_offsets(tq)
    vals = []
    for d in offs:
        dr = np.clip(d * rpt + np.arange(rpt)[None, :] - np.arange(rpq)[:, None],
                     -(NA_KH - 1), NA_KH - 1) + NA_KH - 1
        hot_r = (dr[:, :, None] == np.arange(n_dr)).astype(np.float32)
        v = jnp.einsum("abu,hucd->hacbd", jnp.asarray(hot_r), by_col, precision=HIGHEST)
        vals.append(v.reshape(N_HEADS, tq, ROW_BLK))
    out = []
    for t_rep in (0, 1, qpb - 1):
        per_block = []
        for di, d in enumerate(offs):
            kt = t_rep * bpq + d
            q_row = t_rep * rpq + q_sub
            k_row = kt * rpt + k_sub
            r_start = np.clip(q_row - kh // 2, 0, rows_total - kh)
            row_ok = (k_row[None] >= r_start[:, None]) & (k_row[None] < r_start[:, None] + kh)
            c_start = np.clip(q_col - NA_KW // 2, 0, GRID_W - NA_KW)
            col_ok = (k_col[None] >= c_start[:, None]) & (k_col[None] < c_start[:, None] + NA_KW)
            ok = row_ok & col_ok & (0 <= kt < lb)
            per_block.append(jnp.where(jnp.asarray(ok)[None], vals[di], NEG_INF).reshape(-1, ROW_BLK))
        out.append(jnp.stack(per_block))
    return jnp.stack(out)


def _merge_kernel(*refs, tm, split_x, n_ctx_tiles):
    it = iter(refs)
    h_ref = next(it)
    is_ctx = pl.program_id(0) < n_ctx_tiles
    if n_ctx_tiles > 0:
        o_refs = [(next(it), next(it)) for _ in range(4)]
        o_vals = [_tile_rows(oc, ol, is_ctx) for oc, ol in o_refs]
    else:
        o_vals = [next(it)[...] for _ in range(4)]
    if split_x:
        xc_ref, xl_ref = next(it), next(it)
    else:
        x_ref = next(it)
    (mod_ref, gffn_ref, wg_ref, bg_ref, wb_ref, wout_ref, wr_ref, br_ref,
     x1_ref, h2_ref, route_ref, cnt_ref, y_ref) = (next(it) for _ in range(13))
    hb = h_ref[...]
    d_model = hb.shape[1]
    for t in range(d_model // MERGE_TN):
        cs = slice(t * MERGE_TN, (t + 1) * MERGE_TN)
        y = None
        for n, o_n in enumerate(o_vals):
            gate = _sigmoid(jnp.dot(hb, wg_ref[n, :, cs], preferred_element_type=F32) + bg_ref[n, :, cs])
            u = gate * jnp.dot(o_n, wb_ref[n, :, cs], preferred_element_type=F32)
            y = u if y is None else y + u
        y_ref[:, cs] = y.astype(BF16)
    z = jnp.dot(y_ref[...], wout_ref[...], preferred_element_type=F32)
    x_in = _tile_rows(xc_ref, xl_ref, is_ctx) if split_x else x_ref[...]
    x1 = x_in + mod_ref[0, 2:3, :] * z
    x1_ref[...] = x1
    h2 = _rms(x1) * gffn_ref[...] * (1.0 + mod_ref[0, 4:5, :]) + mod_ref[0, 3:4, :]
    for c in range(SUBLANES):
        h2_ref[pl.ds(c, tm, stride=SUBLANES), :] = h2[:, c * LANES:(c + 1) * LANES]

    h2_hi = h2.astype(BF16)
    h2_lo = (h2 - h2_hi.astype(F32)).astype(BF16)
    logit = (jnp.dot(h2_hi, wr_ref[0], preferred_element_type=F32)
             + jnp.dot(h2_lo, wr_ref[0], preferred_element_type=F32)
             + jnp.dot(h2_hi, wr_ref[1], preferred_element_type=F32)) + br_ref[...]
    lane = _lane_iota(logit.shape)
    big = jnp.int32(1 << 20)
    is_g = (lane >= N_EXPERTS) & (lane < N_EXPERTS + N_GROUPS)
    gl = jnp.where(is_g, logit, NEG_INF)
    gmax = jnp.max(gl, axis=-1, keepdims=True)
    gsel = jnp.min(jnp.where(gl == gmax, lane - N_EXPERTS, big), axis=-1, keepdims=True)
    gw = 1.0 / jnp.sum(jnp.where(is_g, jnp.exp(gl - gmax), 0.0), axis=-1, keepdims=True)
    in_grp = (lane < N_EXPERTS) & ((lane // EXPERTS_PER_GROUP) == gsel)
    el = jnp.where(in_grp, logit, NEG_INF)
    v1 = jnp.max(el, axis=-1, keepdims=True)
    i1 = jnp.min(jnp.where(el == v1, lane, big), axis=-1, keepdims=True)
    el2 = jnp.where(lane == i1, NEG_INF, el)
    v2 = jnp.max(el2, axis=-1, keepdims=True)
    i2 = jnp.min(jnp.where(el2 == v2, lane, big), axis=-1, keepdims=True)
    e21 = jnp.exp(v2 - v1)
    w1 = gw / (1.0 + e21)
    w2 = gw * e21 / (1.0 + e21)
    route = jnp.where(lane == 0, i1.astype(F32),
                      jnp.where(lane == 1, i2.astype(F32),
                                jnp.where(lane == 2, w1, jnp.where(lane == 3, w2, 0.0))))
    route_ref[...] = route.T[:SUBLANES, :]
    picked = jnp.where((lane == i1) | (lane == i2), 1.0, 0.0)
    cnt_ref[...] = jnp.broadcast_to(jnp.sum(picked, axis=0, keepdims=True), cnt_ref.shape)


def _merge_call(lay, layer, h, o, x, mod, w, *, with_ctx):
    T, D, tm = lay["T"], lay["D"], lay["tm"]
    nct, tpb, B = lay["nct"], lay["tpb"], lay["B"]
    off = 0 if with_ctx else nct
    n_tiles = T // tm - off

    def mod_row(i):
        return jnp.where(i < nct, B, (i - nct) // tpb)

    row = lambda i: (i + off, 0)
    const2 = lambda i: (0, 0)
    const3 = lambda i: (0, 0, 0)
    split_x = isinstance(x, (tuple, list))
    in_specs = [pl.BlockSpec((tm, D), row)]
    o_args = []
    for kind in ("A", "B", "C", "D"):
        o_ctx, o_lat = o[kind]
        if with_ctx:
            in_specs += _split_row_specs(lay, BRANCH_W)
            o_args += [o_ctx, o_lat]
        else:
            in_specs.append(pl.BlockSpec((tm, BRANCH_W), lambda i: (i, 0)))
            o_args.append(o_lat)
    in_specs += _split_row_specs(lay, D, off) if split_x else [pl.BlockSpec((tm, D), row)]
    in_specs += [pl.BlockSpec((1, 6, D), lambda i: (layer * lay["mod_rows"] + mod_row(i + off), 0, 0)),
                 pl.BlockSpec((1, D), const2),
                 pl.BlockSpec(w["w_gate"].shape, const3),
                 pl.BlockSpec(w["b_gate"].shape, const3),
                 pl.BlockSpec(w["w_branch"].shape, const3),
                 pl.BlockSpec((D, D), const2),
                 pl.BlockSpec((2, D, LANES), const3),
                 pl.BlockSpec((1, LANES), const2)]
    out_shape = [jax.ShapeDtypeStruct((T, D), F32),
                 jax.ShapeDtypeStruct((T * SUBLANES, LANES), F32),
                 jax.ShapeDtypeStruct((SUBLANES, T), F32),
                 jax.ShapeDtypeStruct((T // tm * SUBLANES, LANES), F32)]
    out_specs = [pl.BlockSpec((tm, D), row),
                 pl.BlockSpec((tm * SUBLANES, LANES), row),
                 pl.BlockSpec((SUBLANES, tm), lambda i: (0, i + off)),
                 pl.BlockSpec((SUBLANES, LANES), row)]
    return pl.pallas_call(
        functools.partial(_merge_kernel, tm=tm, split_x=split_x, n_ctx_tiles=nct - off),
        out_shape=out_shape, grid=(n_tiles,), in_specs=in_specs, out_specs=out_specs,
        scratch_shapes=[pltpu.VMEM((tm, D), BF16)],
        compiler_params=_cparams(("parallel",)),
        name="merge",
    )(h, *o_args, *(x if split_x else (x,)), mod, w["g_ffn"], w["w_gate"],
      w["b_gate"], w["w_branch"], w["w_out"], w["w_route"], w["b_route"])


def _moe_kernel(tab_ref, tok_ref, h2_ref, sw_ref, w1_ref, w3_ref, w2_ref, f_ref,
                xg_ref, y_ref, st_ref, *, blk, slots, tile_off):
    ti = pl.program_id(0)
    step = pl.program_id(1)

    @pl.when(step == 0)
    def _():
        f_ref[...] = jnp.zeros_like(f_ref)
        xg_ref[...] = jnp.zeros_like(xg_ref)

    tok_base = (ti + tile_off) * slots
    segs = []
    for j in range(MOE_EXPERTS_PER_STEP):
        base = ((ti + tile_off) * N_EXPERTS + step * MOE_EXPERTS_PER_STEP + j) * 2
        segs.append((tab_ref[base], tab_ref[base + 1]))

    def gather(j, off, n_rows):
        slot0 = tok_base + off

        def body(gi, c):
            r0 = gi * MOE_GROUP
            srcs = [pl.multiple_of(tok_ref[slot0 + r0 + u], SUBLANES) for u in range(MOE_GROUP)]
            for u in range(MOE_GROUP):
                dst = pl.multiple_of((r0 + u) * SUBLANES, SUBLANES)
                xg_ref[j, pl.ds(dst, SUBLANES), :] = h2_ref[pl.ds(srcs[u], SUBLANES), :]
            return c

        lax.fori_loop(0, n_rows // MOE_GROUP, body, 0)

    def ffn(j, off):
        xb = _load_token_major(xg_ref.at[j], blk).astype(BF16)
        a = jnp.dot(xb, w1_ref[j], preferred_element_type=F32)
        g = jnp.dot(xb, w3_ref[j], preferred_element_type=F32)
        hid = (a * _sigmoid(a) * g).astype(BF16)
        y = jnp.dot(hid, w2_ref[j], preferred_element_type=F32)
        y_ref[j] = y * sw_ref[0, pl.ds(off, blk), :]

    def scatter(j, off, n_rows):
        slot0 = tok_base + off

        def body(gi, c):
            r0 = pl.multiple_of(gi * MOE_GROUP, MOE_GROUP)
            dsts = [pl.multiple_of(tok_ref[slot0 + r0 + u], SUBLANES) for u in range(MOE_GROUP)]
            for ch in range(SUBLANES):
                st_ref[pl.ds(ch, MOE_GROUP, stride=SUBLANES), :] = y_ref[j, pl.ds(r0, MOE_GROUP),
                                                                         ch * LANES:(ch + 1) * LANES]
            vals = [f_ref[pl.ds(dsts[u], SUBLANES), :] + st_ref[u * SUBLANES:(u + 1) * SUBLANES, :]
                    for u in range(MOE_GROUP)]
            for u in reversed(range(MOE_GROUP)):
                f_ref[pl.ds(dsts[u], SUBLANES), :] = vals[u]
            return c

        lax.fori_loop(0, n_rows // MOE_GROUP, body, 0)

    for j, (seg0, n_pad) in enumerate(segs):
        gather(j, pl.multiple_of(seg0, MOE_GROUP), jnp.minimum(n_pad, blk))
    for j, (seg0, _) in enumerate(segs):
        ffn(j, pl.multiple_of(seg0, MOE_GROUP))
    for j, (seg0, n_pad) in enumerate(segs):
        scatter(j, pl.multiple_of(seg0, MOE_GROUP), jnp.minimum(n_pad, blk))

    for j, (seg0, n_pad) in enumerate(segs):
        def extra_block(b, carry, j=j, seg0=seg0, n_pad=n_pad):
            off = pl.multiple_of(seg0 + b * blk, MOE_GROUP)
            n_rows = jnp.minimum(n_pad - b * blk, blk)
            gather(j, off, n_rows)
            ffn(j, off)
            scatter(j, off, n_rows)
            return carry

        lax.fori_loop(1, (n_pad + blk - 1) // blk, extra_block, 0)


def _moe_call(lay, route, cnt, h2, w, *, with_ctx):
    T, D, tt = lay["T"], lay["D"], lay["tt"]
    blk = lay["moe_blk"]
    n_tiles_all = T // tt
    tile_off = 0 if with_ctx else lay["ncb"] * ROW_BLK // tt
    n_tiles = n_tiles_all - tile_off
    n_assign = 2 * tt
    n_fill = N_EXPERTS * MOE_GROUP
    slots = n_assign + n_fill + blk

    eid = route[0:2, :].T.astype(jnp.int32).reshape(n_tiles_all, n_assign)
    wts = route[2:4, :].T.reshape(n_tiles_all, n_assign)
    stride = pl.next_power_of_2(n_assign + MOE_GROUP)
    ex = jnp.arange(N_EXPERTS, dtype=jnp.int32)
    counts = cnt.reshape(n_tiles_all, -1, SUBLANES, LANES)[:, :, 0, :N_EXPERTS].sum(axis=1).astype(jnp.int32)
    n_dummy = (-counts) % MOE_GROUP
    fill_i = jnp.arange(MOE_GROUP, dtype=jnp.int32)[None, None, :]
    fill_key = jnp.where(fill_i < n_dummy[:, :, None],
                         ex[None, :, None] * stride + n_assign + fill_i,
                         N_EXPERTS * stride + ex[None, :, None] * MOE_GROUP + fill_i)
    keys = jnp.concatenate([eid * stride + jnp.arange(n_assign, dtype=jnp.int32)[None, :],
                            fill_key.reshape(n_tiles_all, n_fill)], axis=1)
    keys, slot_w = lax.sort(
        (keys, jnp.concatenate([wts, jnp.zeros((n_tiles_all, n_fill), F32)], axis=1)),
        dimension=1, is_stable=False, num_keys=1)
    a_sorted = keys % stride
    is_real = (a_sorted < n_assign) & (keys < N_EXPERTS * stride)
    slot_tok = jnp.where(is_real, a_sorted // 2 * SUBLANES, 0)
    slot_tok = jnp.pad(slot_tok, ((0, 0), (0, blk)))
    slot_w = jnp.pad(slot_w, ((0, 0), (0, blk)))
    padded = counts + n_dummy
    seg0 = jnp.cumsum(padded, axis=1) - padded
    tab = jnp.stack([seg0, padded], axis=-1).reshape(-1).astype(jnp.int32)

    return pl.pallas_call(
        functools.partial(_moe_kernel, blk=blk, slots=slots, tile_off=tile_off),
        out_shape=jax.ShapeDtypeStruct((T * SUBLANES, LANES), F32),
        grid_spec=pltpu.PrefetchScalarGridSpec(
            num_scalar_prefetch=2,
            grid=(n_tiles, N_EXPERTS // MOE_EXPERTS_PER_STEP),
            in_specs=[
                pl.BlockSpec((tt * SUBLANES, LANES), lambda t, e, *_: (t + tile_off, 0)),
                pl.BlockSpec((1, slots, 1), lambda t, e, *_: (t + tile_off, 0, 0)),
                pl.BlockSpec((MOE_EXPERTS_PER_STEP, D, EXPERT_FF), lambda t, e, *_: (e, 0, 0)),
                pl.BlockSpec((MOE_EXPERTS_PER_STEP, D, EXPERT_FF), lambda t, e, *_: (e, 0, 0)),
                pl.BlockSpec((MOE_EXPERTS_PER_STEP, EXPERT_FF, D), lambda t, e, *_: (e, 0, 0)),
            ],
            out_specs=pl.BlockSpec((tt * SUBLANES, LANES), lambda t, e, *_: (t + tile_off, 0)),
            scratch_shapes=[pltpu.VMEM((MOE_EXPERTS_PER_STEP, blk * SUBLANES, LANES), F32),
                            pltpu.VMEM((MOE_EXPERTS_PER_STEP, blk, D), F32),
                            pltpu.VMEM((MOE_GROUP * SUBLANES, LANES), F32)],
        ),
        compiler_params=_cparams(("parallel", "arbitrary")),
        name="moe_experts",
    )(tab, slot_tok.reshape(-1), h2, slot_w.reshape(n_tiles_all, slots, 1),
      w["w_ff1"], w["w_ff3"], w["w_ff2"])


def _final_kernel(x_ref, f_ref, mod_ref, g_ref, o_ref, *, tm):
    xf = x_ref[...] + mod_ref[0, 5:6, :] * _load_token_major(f_ref, tm)
    o_ref[...] = _rms(xf) * g_ref[...]


def _final_call(lay, layer, x1, f, mod, g_final):
    T, D, tm = lay["T"], lay["D"], lay["tm"]
    nct, tpb = lay["nct"], lay["tpb"]
    n_lat = T // tm - nct
    return pl.pallas_call(
        functools.partial(_final_kernel, tm=tm),
        out_shape=jax.ShapeDtypeStruct((n_lat * tm, D), F32),
        grid=(n_lat,),
        in_specs=[pl.BlockSpec((tm, D), lambda i: (i + nct, 0)),
                  pl.BlockSpec((tm * SUBLANES, LANES), lambda i: (i + nct, 0)),
                  pl.BlockSpec((1, 6, D), lambda i: (layer * lay["mod_rows"] + i // tpb, 0, 0)),
                  pl.BlockSpec((1, D), lambda i: (0, 0))],
        out_specs=pl.BlockSpec((tm, D), lambda i: (i, 0)),
        compiler_params=_cparams(("parallel",)),
        name="final_norm",
    )(x1, f, mod, g_final)


def _select_cols(wm, segs, scale=None):
    parts = []
    for k, (start, width) in enumerate(segs):
        if start is None:
            parts.append(jnp.zeros((wm.shape[0], width), wm.dtype))
        else:
            blk = wm[:, start:start + width]
            parts.append(blk if scale is None or scale[k] is None else blk * scale[k])
    return jnp.concatenate(parts, axis=1)


def _prep_layer(l, p):
    a_cols = A_Q_RANK + A_KV_RANK + A_ROPE
    b_off = a_cols
    c_off = b_off + 512
    d_off = c_off + 768
    qk_scale = HEAD_DIM ** -0.5 * LOG2E
    gqa_q = lambda off: [(off + hh * HEAD_DIM, HEAD_DIM) for hh in _GQA_PERM]
    segs = ([(0, 256), (256, 128), (None, 64), (384, 32), (None, 32)]
            + gqa_q(b_off) + [(b_off + 256, 128), (b_off + 384, 128)]
            + [(c_off, 256), (c_off + 256, 256), (c_off + 512, 256)]
            + gqa_q(d_off) + [(d_off + 256, 128), (d_off + 384, 128)])
    scale = [None] * len(segs)
    for k in (5, 6, 7, 8, 11):
        scale[k] = qk_scale
    w_in = _select_cols(p["w_in"][l], segs, scale).astype(BF16)
    assert w_in.shape[1] == PROJ_COLS

    hq = A_NOPE + A_ROPE
    segs_q = []
    for hh in range(N_HEADS):
        segs_q += [(hh * hq, hq), (None, LANES - hq)]
    w_q_b = _select_cols(p["w_q_b"][l], segs_q).astype(BF16)
    hk = A_NOPE + HEAD_DIM
    segs_k = []
    for hh in range(N_HEADS):
        segs_k += [(hh * hk, A_NOPE), (None, LANES - A_NOPE)]
    segs_k += [(hh * hk + A_NOPE, HEAD_DIM) for hh in range(N_HEADS)]
    w_kv_b = _select_cols(p["w_kv_b"][l], segs_k).astype(BF16)

    wb = p["w_branch"][l]
    perm_rows = lambda m: jnp.concatenate([m[hh * HEAD_DIM:(hh + 1) * HEAD_DIM] for hh in _GQA_PERM], axis=0)
    w_branch = jnp.stack([wb[0], perm_rows(wb[1]), wb[2], perm_rows(wb[3])]).astype(BF16)

    d = p["w_in"].shape[1]
    w_route = jnp.zeros((d, LANES), F32)
    w_route = w_route.at[:, :N_EXPERTS].set(p["w_router"][l]).at[:, N_EXPERTS:N_EXPERTS + N_GROUPS].set(p["w_group"][l])
    b_route = jnp.zeros((1, LANES), F32)
    b_route = b_route.at[0, :N_EXPERTS].set(p["b_router"][l]).at[0, N_EXPERTS:N_EXPERTS + N_GROUPS].set(p["b_group"][l])
    return {
        "g_mix": p["g_norm_mix"][l][None, :],
        "w_in": w_in,
        "g_q_a": p["g_q_a"][l][None, :],
        "w_q_b": w_q_b,
        "g_kv_a": p["g_kv_a"][l][None, :],
        "w_kv_b": w_kv_b,
        "g_q_d": (jnp.tile(p["g_q_d"][l], 2) * qk_scale)[None, :],
        "g_k_d": jnp.tile(p["g_k_d"][l], 2)[None, :],
        "sink": p["sink_b"][l],
        "rpb": p["rpb_c"][l],
        "w_gate": p["w_gate"][l].astype(BF16),
        "b_gate": p["b_gate"][l][:, None, :],
        "w_branch": w_branch,
        "w_out": p["w_out"][l].astype(BF16),
        "g_ffn": p["g_norm_ffn"][l][None, :],
        "w_route": jnp.stack([w_route.astype(BF16),
                              (w_route - w_route.astype(BF16).astype(F32)).astype(BF16)]),
        "b_route": b_route,
        "w_ff1": p["w_ff1"][l].astype(BF16),
        "w_ff3": p["w_ff3"][l].astype(BF16),
        "w_ff2": p["w_ff2"][l].astype(BF16),
    }


def _rope_tables(S, tm):
    t = np.arange(S)
    rows = (t // GRID_W).astype(np.float32)
    cols = (t % GRID_W).astype(np.float32)

    def cs(rot):
        half = rot // 2
        inv = np.float32(ROPE_THETA) ** (-np.arange(0, half, 2, dtype=np.float32) / np.float32(half))
        ar_, ac_ = rows[:, None] * inv, cols[:, None] * inv
        cos = np.concatenate([np.cos(ar_), np.cos(ar_), np.cos(ac_), np.cos(ac_)], axis=-1)
        sin = np.concatenate([-np.sin(ar_), np.sin(ar_), -np.sin(ac_), np.sin(ac_)], axis=-1)
        return cos.astype(np.float32), sin.astype(np.float32)

    cos64, sin64 = cs(HEAD_DIM)
    cos32, sin32 = cs(A_ROPE)
    ones = lambda n: np.ones((S, n), np.float32)
    zeros = lambda n: np.zeros((S, n), np.float32)
    tabs = {
        "cos_h": np.concatenate([cos64, cos64], axis=-1),
        "sin_h": np.concatenate([sin64, sin64], axis=-1),
        "cos_a": np.concatenate([ones(A_NOPE), cos32, ones(LANES - A_NOPE - A_ROPE)], axis=-1),
        "sin_a": np.concatenate([zeros(A_NOPE), sin32, zeros(LANES - A_NOPE - A_ROPE)], axis=-1),
    }
    ident = {"cos_h": 1.0, "sin_h": 0.0, "cos_a": 1.0, "sin_a": 0.0}
    return {k: jnp.asarray(np.concatenate([v, np.full((tm, LANES), ident[k], np.float32)], axis=0))
            for k, v in tabs.items()}


def _layout(B, S, n_ctx, D):
    assert n_ctx == ROW_BLK and S % 1024 == 0 and S // GRID_W >= 3 * (ROW_BLK // GRID_W) and B < 16
    T = B * (n_ctx + S)
    tm = 512 if (B * n_ctx) % 512 == 0 else 256
    def largest_tile(cands):
        return next(t for t in cands if (B * n_ctx) % t == 0 and S % t == 0)

    tq_dense = largest_tile((1024, 512, 256))
    tk_dense = largest_tile((2048, 1024, 512, 256))
    tt = 2048 if (B * n_ctx) % 2048 == 0 else B * n_ctx
    assert S % tt == 0
    return {
        "B": B, "S": S, "D": D, "T": T, "tm": tm,
        "ncb": B * n_ctx // ROW_BLK,
        "lb": S // ROW_BLK,
        "nct": B * n_ctx // tm,
        "tpb": S // tm,
        "tk_dense": tk_dense,
        "tq_dense": tq_dense,
        "tq_band": tm,
        "tt": tt,
        "moe_blk": 160,
        "mod_rows": 16,
    }


def kernel(x, c, ctx, c_ctx, w_mod, b_mod, g_norm_mix, w_in, g_q_a, w_q_b, g_kv_a, w_kv_b, sink_b, rpb_c,
           g_q_d, g_k_d, w_gate, b_gate, w_branch, w_out, g_norm_ffn, w_group, b_group, w_router, b_router,
           w_ff1, w_ff3, w_ff2, g_final):
    B, S, D = x.shape
    n_ctx = ctx.shape[1]
    depth = w_mod.shape[0]
    lay = _layout(B, S, n_ctx, D)
    params = dict(w_in=w_in, g_norm_mix=g_norm_mix, g_q_a=g_q_a, w_q_b=w_q_b, g_kv_a=g_kv_a, w_kv_b=w_kv_b,
                  sink_b=sink_b, rpb_c=rpb_c, g_q_d=g_q_d, g_k_d=g_k_d, w_gate=w_gate, b_gate=b_gate,
                  w_branch=w_branch, w_out=w_out, g_norm_ffn=g_norm_ffn, w_group=w_group, b_group=b_group,
                  w_router=w_router, b_router=b_router, w_ff1=w_ff1, w_ff3=w_ff3, w_ff2=w_ff2)

    c_all = jnp.zeros((lay["mod_rows"], D), F32).at[:B].set(c).at[B].set(c_ctx)
    mod = _modulation(c_all, w_mod, b_mod).reshape(depth * lay["mod_rows"], 6, D)
    tabs = _rope_tables(S, lay["tm"])
    win_bias = _window_bias(S, lay["tq_band"])
    xf = (ctx.reshape(B * n_ctx, D), x.reshape(B * S, D))

    f = None
    for l in range(depth):
        with_ctx = l < depth - 1
        w = _prep_layer(l, params)
        xf, pr = _proj_call(lay, xf, f, mod, mod, l, w, tabs)
        o = {}
        for kind, kl in (("A", "a"), ("B", "b"), ("C", "c"), ("D", "d")):
            q, k, v = pr["q" + kl], pr["k" + kl], pr["v" + kl]
            sink = w["sink"] if kind == "B" else None
            bias = win_bias if kind == "B" else None
            if kind == "C":
                bias = _neighbourhood_bias(w["rpb"], S // GRID_W, lay["tq_band"])
            o_ctx = _attn_context_call(lay, kind, q, k, v, sink=sink) if with_ctx else None
            o[kind] = (o_ctx, _attn_latent_call(lay, kind, q, k, v, sink=sink, bias=bias))
        xf, h2, route, cnt = _merge_call(lay, l, pr["h"], o, xf, mod, w, with_ctx=with_ctx)
        f = _moe_call(lay, route, cnt, h2, w, with_ctx=with_ctx)
    out = _final_call(lay, depth - 1, xf, f, mod, g_final[None, :])
    return out.reshape(B, S, D)
```

```python
import functools

import numpy as np
import jax
import jax.numpy as jnp
from jax import lax
from jax.experimental import pallas as pl
from jax.experimental.pallas import tpu as pltpu

F32 = jnp.float32
BF16 = jnp.bfloat16
HIGHEST = lax.Precision.HIGHEST

GRID_W = 64
ROPE_THETA = 10000.0
EPS = 1e-6
NEG_INF = -1e30
LOG2E = 1.4426950408889634
HEAD_DIM = 64
N_HEADS = 4
BRANCH_W = 256
A_Q_RANK = 256
A_KV_RANK = 128
A_NOPE = 64
A_ROPE = 32
NA_KH = 8
NA_KW = 16
WINDOW = 128
N_GROUPS = 4
EXPERTS_PER_GROUP = 8
N_EXPERTS = 32
EXPERT_FF = 256

LANES = 128
SUBLANES = 8
ROW_BLK = 256
MERGE_TN = 256
MOE_EXPERTS_PER_STEP = 4
MOE_GROUP = 16
VMEM_LIMIT = 56 * 1024 * 1024

_PROJ_GROUPS = (("cq", 256), ("ckv", 128), ("kr", 128), ("qb", 256), ("kb", 128), ("vb", 128),
                ("qc", 256), ("kc", 256), ("vc", 256), ("qd", 256), ("kd", 128), ("vd", 128))
_PROJ_OFF = {}
_o = 0
for _n, _w in _PROJ_GROUPS:
    _PROJ_OFF[_n] = (_o, _w)
    _o += _w
PROJ_COLS = _o
_GQA_PERM = (0, 2, 1, 3)


def _cparams(sem):
    return pltpu.CompilerParams(dimension_semantics=sem, vmem_limit_bytes=VMEM_LIMIT)


def _lane_iota(shape):
    return lax.broadcasted_iota(jnp.int32, shape, len(shape) - 1)


def _sigmoid(x):
    return 1.0 / (1.0 + jnp.exp(-x))


def _mod_kernel(c_ref, w_ref, b_ref, o_ref):
    cf = c_ref[...]
    s = cf * _sigmoid(cf)
    o_ref[0] = jnp.dot(s, w_ref[0], precision=HIGHEST, preferred_element_type=F32) + b_ref[0]


def _modulation(c_all, w_mod, b_mod):
    n_layers, d, n_out = w_mod.shape
    rows = c_all.shape[0]
    tn = n_out // 2
    return pl.pallas_call(
        _mod_kernel,
        out_shape=jax.ShapeDtypeStruct((n_layers, rows, n_out), F32),
        grid=(n_layers, n_out // tn),
        in_specs=[pl.BlockSpec((rows, d), lambda l, j: (0, 0)),
                  pl.BlockSpec((1, d, tn), lambda l, j: (l, 0, j)),
                  pl.BlockSpec((1, 1, tn), lambda l, j: (l, 0, j))],
        out_specs=pl.BlockSpec((1, rows, tn), lambda l, j: (l, 0, j)),
        compiler_params=_cparams(("arbitrary", "arbitrary")),
        name="modulation",
    )(c_all, w_mod, b_mod.reshape(n_layers, 1, n_out))


def _rms(x):
    return x * lax.rsqrt(jnp.mean(x * x, axis=-1, keepdims=True) + EPS)


def _swap_blocks(x, blk):
    lane = _lane_iota(x.shape)
    up = pltpu.roll(x, LANES - blk, 1)
    dn = pltpu.roll(x, blk, 1)
    return jnp.where((lane // blk) % 2 == 0, up, dn)


def _rope(x, cos, sin, blk):
    return x * cos + _swap_blocks(x, blk) * sin


def _pair_norm(x, g):
    lo = _lane_iota(x.shape) < HEAD_DIM
    sq = x * x
    s_lo = jnp.sum(jnp.where(lo, sq, 0.0), axis=-1, keepdims=True)
    s_hi = jnp.sum(jnp.where(lo, 0.0, sq), axis=-1, keepdims=True)
    ms = jnp.where(lo, s_lo, s_hi) * (1.0 / HEAD_DIM)
    return x * lax.rsqrt(ms + EPS) * g


def _load_token_major(ref, rows):
    return jnp.concatenate(
        [ref[pl.ds(c, rows, stride=SUBLANES), :] for c in range(SUBLANES)], axis=-1)


def _tile_rows(xc_ref, xl_ref, is_ctx):
    return jnp.where(is_ctx, xc_ref[...], xl_ref[...])


def _proj_kernel(*refs, with_f, tm, scale_a, nct):
    it = iter(refs)
    if with_f:
        x_ref = next(it)
        f_ref = next(it)
        modp_ref = next(it)
    else:
        xc_ref, xl_ref = next(it), next(it)
    mod_ref = next(it)
    gmix_ref, win_ref, gqa_ref, wqb_ref, gkva_ref, wkvb_ref, gqd_ref, gkd_ref = (next(it) for _ in range(8))
    cosh_ref, sinh_ref, cosa_ref, sina_ref = (next(it) for _ in range(4))
    if with_f:
        x2_ref = next(it)
    h_ref = next(it)
    qa_ref, ka_ref, va_ref, qb_ref, kb_ref, vb_ref, qc_ref, kc_ref, vc_ref, qd_ref, kd_ref, vd_ref = (
        next(it) for _ in range(12))

    if with_f:
        xf = x_ref[...] + modp_ref[0, 5:6, :] * _load_token_major(f_ref, tm)
        x2_ref[...] = xf
    else:
        xf = _tile_rows(xc_ref, xl_ref, pl.program_id(0) < nct)
    h = _rms(xf) * gmix_ref[...] * (1.0 + mod_ref[0, 1:2, :]) + mod_ref[0, 0:1, :]
    hb = h.astype(BF16)
    h_ref[...] = hb
    p = jnp.dot(hb, win_ref[...], preferred_element_type=F32)

    def grp(name):
        o, w = _PROJ_OFF[name]
        return p[:, o:o + w]

    cosh, sinh = cosh_ref[...], sinh_ref[...]
    cosa, sina = cosa_ref[...], sina_ref[...]

    cq = (_rms(grp("cq")) * gqa_ref[...]).astype(BF16)
    qa = jnp.dot(cq, wqb_ref[...], preferred_element_type=F32)
    for hd in range(N_HEADS):
        sl = slice(hd * LANES, (hd + 1) * LANES)
        qa_ref[:, sl] = (_rope(qa[:, sl], cosa, sina, 8) * scale_a).astype(BF16)
    ckv = (_rms(grp("ckv")) * gkva_ref[...]).astype(BF16)
    kva = jnp.dot(ckv, wkvb_ref[...], preferred_element_type=F32)
    kr = _rope(grp("kr"), cosa, sina, 8)
    for hd in range(N_HEADS):
        sl = slice(hd * LANES, (hd + 1) * LANES)
        ka_ref[:, sl] = (kva[:, sl] + kr).astype(BF16)
    va_ref[...] = kva[:, N_HEADS * LANES:].astype(BF16)

    qb = grp("qb")
    for j in range(2):
        sl = slice(j * LANES, (j + 1) * LANES)
        qb_ref[:, sl] = _rope(qb[:, sl], cosh, sinh, 16).astype(BF16)
    kb_ref[...] = _rope(grp("kb"), cosh, sinh, 16).astype(BF16)
    vb_ref[...] = grp("vb").astype(BF16)

    qc_ref[...] = grp("qc").astype(BF16)
    kc_ref[...] = grp("kc").astype(BF16)
    vc_ref[...] = grp("vc").astype(BF16)

    qd = grp("qd")
    for j in range(2):
        sl = slice(j * LANES, (j + 1) * LANES)
        qd_ref[:, sl] = _rope(_pair_norm(qd[:, sl], gqd_ref[...]), cosh, sinh, 16).astype(BF16)
    kd_ref[...] = _rope(_pair_norm(grp("kd"), gkd_ref[...]), cosh, sinh, 16).astype(BF16)
    vd_ref[...] = grp("vd").astype(BF16)


def _split_row_specs(lay, width, off=0):
    tm, nct = lay["tm"], lay["nct"]
    return [pl.BlockSpec((tm, width), lambda i: (jnp.minimum(i + off, nct - 1), 0)),
            pl.BlockSpec((tm, width), lambda i: (jnp.maximum(i + off - nct, 0), 0))]


def _proj_call(lay, x, f, modp, mod, layer, w, tabs):
    T, D, tm = lay["T"], lay["D"], lay["tm"]
    nct, tpb, B = lay["nct"], lay["tpb"], lay["B"]
    with_f = f is not None
    n_tiles = T // tm

    def mod_row(i):
        return jnp.where(i < nct, B, (i - nct) // tpb)

    def tab_blk(i):
        return jnp.where(i < nct, tpb, (i - nct) % tpb)

    row = lambda i: (i, 0)
    const = lambda i: (0, 0)
    if with_f:
        in_specs = [pl.BlockSpec((tm, D), row),
                    pl.BlockSpec((tm * SUBLANES, LANES), row),
                    pl.BlockSpec((1, 6, D), lambda i: ((layer - 1) * lay["mod_rows"] + mod_row(i), 0, 0))]
        args = [x, f, modp]
    else:
        in_specs = _split_row_specs(lay, D)
        args = list(x)
    in_specs += [pl.BlockSpec((1, 6, D), lambda i: (layer * lay["mod_rows"] + mod_row(i), 0, 0))]
    args += [mod]
    for name in ("g_mix", "w_in", "g_q_a", "w_q_b", "g_kv_a", "w_kv_b", "g_q_d", "g_k_d"):
        a = w[name]
        in_specs.append(pl.BlockSpec(a.shape, const))
        args.append(a)
    for tname in ("cos_h", "sin_h", "cos_a", "sin_a"):
        in_specs.append(pl.BlockSpec((tm, LANES), lambda i: (tab_blk(i), 0)))
        args.append(tabs[tname])

    widths = [("h", D), ("qa", 512), ("ka", 512), ("va", 256), ("qb", 256), ("kb", 128), ("vb", 128),
              ("qc", 256), ("kc", 256), ("vc", 256), ("qd", 256), ("kd", 128), ("vd", 128)]
    out_shape, out_specs = [], []
    if with_f:
        out_shape.append(jax.ShapeDtypeStruct((T, D), F32))
        out_specs.append(pl.BlockSpec((tm, D), row))
    for _, wd in widths:
        out_shape.append(jax.ShapeDtypeStruct((T, wd), BF16))
        out_specs.append(pl.BlockSpec((tm, wd), row))

    outs = pl.pallas_call(
        functools.partial(_proj_kernel, with_f=with_f, tm=tm, nct=nct,
                          scale_a=float((A_NOPE + A_ROPE) ** -0.5 * LOG2E)),
        out_shape=out_shape, grid=(n_tiles,), in_specs=in_specs, out_specs=out_specs,
        compiler_params=_cparams(("parallel",)),
        name="proj_in",
    )(*args)
    outs = list(outs)
    x2 = outs.pop(0) if with_f else x
    names = [n for n, _ in widths]
    return x2, dict(zip(names, outs))


_NT = (((1,), (1,)), ((), ()))


def _head_plan(kind):
    if kind == "A":
        return tuple((r, None, r, r // 2) for r in range(N_HEADS))
    if kind == "C":
        return tuple((r // 2, r % 2, r // 2, r // 2) for r in range(N_HEADS))
    return tuple((r // 2, r % 2, 0, 0) for r in range(N_HEADS))


def _head_query(q_ref, plan_r):
    qt, half, _, _ = plan_r
    src = q_ref[:, qt * LANES:(qt + 1) * LANES]
    if half is None:
        return src
    lane = _lane_iota(src.shape)
    keep = (lane < HEAD_DIM) if half == 0 else (lane >= HEAD_DIM)
    return jnp.where(keep, src, jnp.zeros_like(src))


def _score_chunks(q, k_blocks, bias_blocks):
    chunks = []
    for kb, bb in zip(k_blocks, bias_blocks):
        s = lax.dot_general(q, kb, _NT, preferred_element_type=F32)
        if bb is not None:
            s = s + bb
        chunks += [s[:, c * LANES:(c + 1) * LANES] for c in range(s.shape[1] // LANES)]
    return chunks


def _row_max(chunks):
    m = functools.reduce(jnp.maximum, chunks)
    return jnp.broadcast_to(jnp.max(m, axis=-1, keepdims=True), m.shape)


def _weighted_values(p_chunks, v_blocks, half):
    pv, idx = None, 0
    for vb in v_blocks:
        n = vb.shape[0] // LANES
        p = jnp.concatenate(p_chunks[idx:idx + n], axis=1).astype(BF16)
        idx += n
        lane = _lane_iota(vb.shape)
        own = (lane < HEAD_DIM) if half == 0 else (lane >= HEAD_DIM)
        d = jnp.dot(p, jnp.where(own, vb, jnp.ones_like(vb)), preferred_element_type=F32)
        pv = d if pv is None else pv + d
    return pv


def _softmax_once(s, v_aug, bias_blocks, sink):
    chunks = []
    for c in range(s.shape[1] // LANES):
        sc = s[:, c * LANES:(c + 1) * LANES]
        bb = bias_blocks[c * LANES // ROW_BLK]
        if bb is not None:
            lo = c * LANES % ROW_BLK
            sc = sc + bb[:, lo:lo + LANES]
        chunks.append(sc)
    m = _row_max(chunks)
    if sink is not None:
        m = jnp.maximum(m, sink)
    p = jnp.concatenate([jnp.exp2(c - m) for c in chunks], axis=1).astype(BF16)
    pv = jnp.dot(p, v_aug, preferred_element_type=F32)
    l = pltpu.roll(pv, HEAD_DIM, 1)
    if sink is not None:
        l = l + jnp.exp2(sink - m)
    return pv / l


def _store_heads(o_ref, outs):
    lane = _lane_iota(outs[0].shape)
    for g in range(2):
        o_ref[:, g * LANES:(g + 1) * LANES] = jnp.where(
            lane < HEAD_DIM, outs[2 * g], outs[2 * g + 1]).astype(o_ref.dtype)


def _attn_band_kernel(*refs, kind, lb, n_kv, tq):
    it = iter(refs)
    q_ref = next(it)
    k_refs = [next(it) for _ in range(n_kv)]
    v_refs = [next(it) for _ in range(n_kv)]
    bias_ref = next(it) if n_kv > 1 else None
    sink_ref = next(it) if kind == "B" else None
    o_ref = next(it)
    t = pl.program_id(0) % lb
    var = jnp.where(t == 0, 0, jnp.where(t == lb - 1, 2, 1))
    plan = _head_plan(kind)
    scores = {}
    for kt in sorted({p[2] for p in plan}):
        heads = [r for r in range(N_HEADS) if plan[r][2] == kt]
        k_all = jnp.concatenate([k[:, kt * LANES:(kt + 1) * LANES] for k in k_refs], axis=0)
        q_all = jnp.concatenate([_head_query(q_ref, plan[r]) for r in heads], axis=0)
        s_all = lax.dot_general(q_all, k_all, _NT, preferred_element_type=F32)
        for j, r in enumerate(heads):
            scores[r] = s_all[j * tq:(j + 1) * tq, :]
    outs = []
    v_aug = {}
    for r, plan_r in enumerate(plan):
        _, _, kt, vt = plan_r
        if (vt, r % 2) not in v_aug:
            vv = jnp.concatenate([v[:, vt * LANES:(vt + 1) * LANES] for v in v_refs], axis=0)
            lane = _lane_iota(vv.shape)
            own = (lane < HEAD_DIM) if r % 2 == 0 else (lane >= HEAD_DIM)
            v_aug[(vt, r % 2)] = jnp.where(own, vv, jnp.ones_like(vv))
        bias = [None]
        for j in range(n_kv - 1):
            if kind == "C":
                bias.append(bias_ref[var, j, r * tq:(r + 1) * tq, :])
            else:
                bias.append(bias_ref[var, j])
        sink = sink_ref[_GQA_PERM[r]] * LOG2E if kind == "B" else None
        outs.append(_softmax_once(scores[r], v_aug[(vt, r % 2)], bias, sink))
    _store_heads(o_ref, outs)


def _attn_dense_kernel(q_ref, kc_ref, vc_ref, kl_ref, vl_ref, o_ref, qs_ref, m_ref, acc_ref,
                       *, kind, n_steps, tq):
    s = pl.program_id(1)
    plan = _head_plan(kind)

    def update(first):
        if first:
            for r in range(N_HEADS):
                qs_ref[r * tq:(r + 1) * tq, :] = _head_query(q_ref, plan[r])
        scores = {}
        for kt in sorted({p[2] for p in plan}):
            heads = [r for r in range(N_HEADS) if plan[r][2] == kt]
            ksl = slice(kt * LANES, (kt + 1) * LANES)
            k_blocks = [kc_ref[:, ksl], kl_ref[:, ksl]] if first else [kl_ref[:, ksl]]
            q_all = qs_ref[heads[0] * tq:(heads[-1] + 1) * tq, :]
            chunks = _score_chunks(q_all, k_blocks, [None] * len(k_blocks))
            for j, r in enumerate(heads):
                scores[r] = [c[j * tq:(j + 1) * tq, :] for c in chunks]
        for r, (_, _, kt, vt) in enumerate(plan):
            rows = slice(r * tq, (r + 1) * tq)
            vsl = slice(vt * LANES, (vt + 1) * LANES)
            v_blocks = [vc_ref[:, vsl], vl_ref[:, vsl]] if first else [vl_ref[:, vsl]]
            chunks = scores[r]
            m_cur = _row_max(chunks)
            if first:
                m_new = m_cur
            else:
                m_prev = m_ref[rows, :]
                m_new = jnp.maximum(m_prev, m_cur)
                alpha = jnp.exp2(m_prev - m_new)
            p = [jnp.exp2(c - m_new) for c in chunks]
            pv = _weighted_values(p, v_blocks, r % 2)
            if first:
                acc_ref[rows, :] = pv
            else:
                acc_ref[rows, :] = alpha * acc_ref[rows, :] + pv
            m_ref[rows, :] = m_new

    @pl.when(s == 0)
    def _():
        update(True)

    @pl.when(s > 0)
    def _():
        update(False)

    @pl.when(s == n_steps - 1)
    def _():
        outs = []
        for r in range(N_HEADS):
            rows = slice(r * tq, (r + 1) * tq)
            acc = acc_ref[rows, :]
            outs.append(acc / pltpu.roll(acc, HEAD_DIM, 1))
        _store_heads(o_ref, outs)


def _attn_latent_call(lay, kind, q, k, v, *, sink=None, bias=None):
    T, B, ncb, lb, S = lay["T"], lay["B"], lay["ncb"], lay["lb"], lay["S"]
    tq = ROW_BLK
    qw, kw, vw = q.shape[1], k.shape[1], v.shape[1]
    out_shape = jax.ShapeDtypeStruct((B * S, BRANCH_W), BF16)
    if kind in ("A", "D"):
        tk, tq = lay["tk_dense"], lay["tq_dense"]
        n_steps = S // tk
        lat0 = ncb * ROW_BLK // tk
        q0 = ncb * ROW_BLK // tq
        qpb = S // tq
        lat_blk = lambda i, s: (lat0 + (i // qpb) * n_steps + s, 0)
        return pl.pallas_call(
            functools.partial(_attn_dense_kernel, kind=kind, n_steps=n_steps, tq=tq),
            out_shape=out_shape,
            grid=(B * qpb, n_steps),
            in_specs=[pl.BlockSpec((tq, qw), lambda i, s: (q0 + i, 0)),
                      pl.BlockSpec((ROW_BLK, kw), lambda i, s: (i // qpb, 0)),
                      pl.BlockSpec((ROW_BLK, vw), lambda i, s: (i // qpb, 0)),
                      pl.BlockSpec((tk, kw), lat_blk),
                      pl.BlockSpec((tk, vw), lat_blk)],
            out_specs=pl.BlockSpec((tq, BRANCH_W), lambda i, s: (i, 0)),
            scratch_shapes=[pltpu.VMEM((N_HEADS * tq, LANES), BF16),
                            pltpu.VMEM((N_HEADS * tq, LANES), F32),
                            pltpu.VMEM((N_HEADS * tq, LANES), F32)],
            compiler_params=_cparams(("parallel", "arbitrary")),
            name="attn_" + kind,
        )(q, k, v, k, v)

    tq = lay["tq_band"]
    qpb = S // tq
    q0 = ncb * ROW_BLK // tq
    bpq = tq // ROW_BLK

    def nb(i, d):
        return (ncb + (i // qpb) * lb + jnp.clip((i % qpb) * bpq + d, 0, lb - 1), 0)

    if kind == "B":
        def win(i, j):
            start = _window_block_start(i % qpb, j, tq, S, jnp.clip)
            return (pl.multiple_of(ncb * ROW_BLK + (i // qpb) * S + start, WINDOW), 0)

        blk_shape = lambda width: (pl.Element(ROW_BLK), pl.Element(width))
        kv_maps = [functools.partial(win, j=j) for j in range(_window_blocks(tq))]
    else:
        blk_shape = lambda width: (ROW_BLK, width)
        kv_maps = [functools.partial(nb, d=d) for d in _band_offsets(tq)]
    n_kv = 1 + len(kv_maps)
    ctx_map = lambda i: (i // qpb, 0)
    in_specs = [pl.BlockSpec((tq, qw), lambda i: (q0 + i, 0))]
    in_specs += [pl.BlockSpec((ROW_BLK, kw), ctx_map)] + [pl.BlockSpec(blk_shape(kw), m) for m in kv_maps]
    in_specs += [pl.BlockSpec((ROW_BLK, vw), ctx_map)] + [pl.BlockSpec(blk_shape(vw), m) for m in kv_maps]
    in_specs.append(pl.BlockSpec(bias.shape, lambda i: (0,) * bias.ndim, pipeline_mode=pl.Buffered(1)))
    args = [q] + [k] * n_kv + [v] * n_kv + [bias]
    if kind == "B":
        in_specs.append(pl.BlockSpec(memory_space=pltpu.SMEM))
        args.append(sink)
    return pl.pallas_call(
        functools.partial(_attn_band_kernel, kind=kind, lb=qpb, n_kv=n_kv, tq=tq),
        out_shape=out_shape,
        grid=(B * qpb,),
        in_specs=in_specs,
        out_specs=pl.BlockSpec((tq, BRANCH_W), lambda i: (i, 0)),
        compiler_params=_cparams(("parallel",)),
        name="attn_" + kind,
    )(*args)


def _attn_context_call(lay, kind, q, k, v, *, sink=None):
    ncb = lay["ncb"]
    qw, kw, vw = q.shape[1], k.shape[1], v.shape[1]
    blk = lambda i: (i, 0)
    in_specs = [pl.BlockSpec((ROW_BLK, qw), blk), pl.BlockSpec((ROW_BLK, kw), blk),
                pl.BlockSpec((ROW_BLK, vw), blk)]
    args = [q, k, v]
    if kind == "B":
        in_specs.append(pl.BlockSpec(memory_space=pltpu.SMEM))
        args.append(sink)
    return pl.pallas_call(
        functools.partial(_attn_band_kernel, kind=kind, lb=1, n_kv=1, tq=ROW_BLK),
        out_shape=jax.ShapeDtypeStruct((ncb * ROW_BLK, BRANCH_W), BF16),
        grid=(ncb,),
        in_specs=in_specs,
        out_specs=pl.BlockSpec((ROW_BLK, BRANCH_W), blk),
        compiler_params=_cparams(("parallel",)),
        name="attn_ctx_" + kind,
    )(*args)


def _band_offsets(tq):
    return tuple(range(-1, tq // ROW_BLK + 1))


def _window_blocks(tq):
    return (tq + 2 * WINDOW) // ROW_BLK


def _window_block_start(t, j, tq, S, clip):
    return clip(t * tq - WINDOW + j * ROW_BLK, 0, S - ROW_BLK)


def _window_bias(S, tq):
    qpb = S // tq
    n_blk = _window_blocks(tq)
    qa, ka = np.arange(tq), np.arange(ROW_BLK)
    out = np.full((3, n_blk, tq, ROW_BLK), NEG_INF, np.float32)
    for vi, t_rep in enumerate((0, 1, qpb - 1)):
        qpos = t_rep * tq + qa
        seen = np.zeros(S, bool)
        for j in range(n_blk):
            kpos = int(_window_block_start(t_rep, j, tq, S, np.clip)) + ka
            ok = (np.abs(qpos[:, None] - kpos[None, :]) <= WINDOW) & ~seen[kpos][None, :]
            out[vi, j] = np.where(ok, 0.0, NEG_INF)
            seen[kpos] = True
    return jnp.asarray(out)


def _neighbourhood_bias(rpb, rows_total, tq):
    lb = rows_total * GRID_W // ROW_BLK
    qpb, bpq = rows_total * GRID_W // tq, tq // ROW_BLK
    rpt = ROW_BLK // GRID_W
    rpq = tq // GRID_W
    kh = min(NA_KH, rows_total)
    qa, ka = np.arange(tq), np.arange(ROW_BLK)
    q_sub, q_col = qa // GRID_W, qa % GRID_W
    k_sub, k_col = ka // GRID_W, ka % GRID_W
    n_dr, n_dc = 2 * NA_KH - 1, 2 * NA_KW - 1
    col = np.arange(GRID_W)
    dc = np.clip(col[None, :] - col[:, None], -(NA_KW - 1), NA_KW - 1) + NA_KW - 1
    hot_c = (dc[:, :, None] == np.arange(n_dc)).astype(np.float32)
    by_col = jnp.einsum("huv,cdv->hucd", rpb.astype(F32) * LOG2E, jnp.asarray(hot_c),
                        precision=HIGHEST)
    offs = _band_offsets(tq)
    vals = []
    for d in offs:
        dr = np.clip(d * rpt + np.arange(rpt)[None, :] - np.arange(rpq)[:, None],
                     -(NA_KH - 1), NA_KH - 1) + NA_KH - 1
        hot_r = (dr[:, :, None] == np.arange(n_dr)).astype(np.float32)
        v = jnp.einsum("abu,hucd->hacbd", jnp.asarray(hot_r), by_col, precision=HIGHEST)
        vals.append(v.reshape(N_HEADS, tq, ROW_BLK))
    out = []
    for t_rep in (0, 1, qpb - 1):
        per_block = []
        for di, d in enumerate(offs):
            kt = t_rep * bpq + d
            q_row = t_rep * rpq + q_sub
            k_row = kt * rpt + k_sub
            r_start = np.clip(q_row - kh // 2, 0, rows_total - kh)
            row_ok = (k_row[None] >= r_start[:, None]) & (k_row[None] < r_start[:, None] + kh)
            c_start = np.clip(q_col - NA_KW // 2, 0, GRID_W - NA_KW)
            col_ok = (k_col[None] >= c_start[:, None]) & (k_col[None] < c_start[:, None] + NA_KW)
            ok = row_ok & col_ok & (0 <= kt < lb)
            per_block.append(jnp.where(jnp.asarray(ok)[None], vals[di], NEG_INF).reshape(-1, ROW_BLK))
        out.append(jnp.stack(per_block))
    return jnp.stack(out)


def _merge_kernel(*refs, tm, split_x, n_ctx_tiles):
    it = iter(refs)
    h_ref = next(it)
    is_ctx = pl.program_id(0) < n_ctx_tiles
    if n_ctx_tiles > 0:
        o_refs = [(next(it), next(it)) for _ in range(4)]
        o_vals = [_tile_rows(oc, ol, is_ctx) for oc, ol in o_refs]
    else:
        o_vals = [next(it)[...] for _ in range(4)]
    if split_x:
        xc_ref, xl_ref = next(it), next(it)
    else:
        x_ref = next(it)
    (mod_ref, gffn_ref, wg_ref, bg_ref, wb_ref, wout_ref, wr_ref, br_ref,
     x1_ref, h2_ref, route_ref, y_ref) = (next(it) for _ in range(12))
    hb = h_ref[...]
    d_model = hb.shape[1]
    for t in range(d_model // MERGE_TN):
        cs = slice(t * MERGE_TN, (t + 1) * MERGE_TN)
        y = None
        for n, o_n in enumerate(o_vals):
            gate = _sigmoid(jnp.dot(hb, wg_ref[n, :, cs], preferred_element_type=F32) + bg_ref[n, :, cs])
            u = gate * jnp.dot(o_n, wb_ref[n, :, cs], preferred_element_type=F32)
            y = u if y is None else y + u
        y_ref[:, cs] = y.astype(BF16)
    z = jnp.dot(y_ref[...], wout_ref[...], preferred_element_type=F32)
    x_in = _tile_rows(xc_ref, xl_ref, is_ctx) if split_x else x_ref[...]
    x1 = x_in + mod_ref[0, 2:3, :] * z
    x1_ref[...] = x1
    h2 = _rms(x1) * gffn_ref[...] * (1.0 + mod_ref[0, 4:5, :]) + mod_ref[0, 3:4, :]
    for c in range(SUBLANES):
        h2_ref[pl.ds(c, tm, stride=SUBLANES), :] = h2[:, c * LANES:(c + 1) * LANES]

    h2_hi = h2.astype(BF16)
    h2_lo = (h2 - h2_hi.astype(F32)).astype(BF16)
    logit = (jnp.dot(h2_hi, wr_ref[0], preferred_element_type=F32)
             + jnp.dot(h2_lo, wr_ref[0], preferred_element_type=F32)
             + jnp.dot(h2_hi, wr_ref[1], preferred_element_type=F32)) + br_ref[...]
    lane = _lane_iota(logit.shape)
    big = jnp.int32(1 << 20)
    is_g = (lane >= N_EXPERTS) & (lane < N_EXPERTS + N_GROUPS)
    gl = jnp.where(is_g, logit, NEG_INF)
    gmax = jnp.max(gl, axis=-1, keepdims=True)
    gsel = jnp.min(jnp.where(gl == gmax, lane - N_EXPERTS, big), axis=-1, keepdims=True)
    gw = 1.0 / jnp.sum(jnp.where(is_g, jnp.exp(gl - gmax), 0.0), axis=-1, keepdims=True)
    in_grp = (lane < N_EXPERTS) & ((lane // EXPERTS_PER_GROUP) == gsel)
    el = jnp.where(in_grp, logit, NEG_INF)
    v1 = jnp.max(el, axis=-1, keepdims=True)
    i1 = jnp.min(jnp.where(el == v1, lane, big), axis=-1, keepdims=True)
    el2 = jnp.where(lane == i1, NEG_INF, el)
    v2 = jnp.max(el2, axis=-1, keepdims=True)
    i2 = jnp.min(jnp.where(el2 == v2, lane, big), axis=-1, keepdims=True)
    e21 = jnp.exp(v2 - v1)
    w1 = gw / (1.0 + e21)
    w2 = gw * e21 / (1.0 + e21)
    route_ref[...] = jnp.where(lane == 0, i1.astype(F32),
                               jnp.where(lane == 1, i2.astype(F32),
                                         jnp.where(lane == 2, w1, jnp.where(lane == 3, w2, 0.0))))


def _merge_call(lay, layer, h, o, x, mod, w, *, with_ctx):
    T, D, tm = lay["T"], lay["D"], lay["tm"]
    nct, tpb, B = lay["nct"], lay["tpb"], lay["B"]
    off = 0 if with_ctx else nct
    n_tiles = T // tm - off

    def mod_row(i):
        return jnp.where(i < nct, B, (i - nct) // tpb)

    row = lambda i: (i + off, 0)
    const2 = lambda i: (0, 0)
    const3 = lambda i: (0, 0, 0)
    split_x = isinstance(x, (tuple, list))
    in_specs = [pl.BlockSpec((tm, D), row)]
    o_args = []
    for kind in ("A", "B", "C", "D"):
        o_ctx, o_lat = o[kind]
        if with_ctx:
            in_specs += _split_row_specs(lay, BRANCH_W)
            o_args += [o_ctx, o_lat]
        else:
            in_specs.append(pl.BlockSpec((tm, BRANCH_W), lambda i: (i, 0)))
            o_args.append(o_lat)
    in_specs += _split_row_specs(lay, D, off) if split_x else [pl.BlockSpec((tm, D), row)]
    in_specs += [pl.BlockSpec((1, 6, D), lambda i: (layer * lay["mod_rows"] + mod_row(i + off), 0, 0)),
                 pl.BlockSpec((1, D), const2),
                 pl.BlockSpec(w["w_gate"].shape, const3),
                 pl.BlockSpec(w["b_gate"].shape, const3),
                 pl.BlockSpec(w["w_branch"].shape, const3),
                 pl.BlockSpec((D, D), const2),
                 pl.BlockSpec((2, D, LANES), const3),
                 pl.BlockSpec((1, LANES), const2)]
    out_shape = [jax.ShapeDtypeStruct((T, D), F32),
                 jax.ShapeDtypeStruct((T * SUBLANES, LANES), F32),
                 jax.ShapeDtypeStruct((T, LANES), F32)]
    out_specs = [pl.BlockSpec((tm, D), row),
                 pl.BlockSpec((tm * SUBLANES, LANES), row),
                 pl.BlockSpec((tm, LANES), row)]
    return pl.pallas_call(
        functools.partial(_merge_kernel, tm=tm, split_x=split_x, n_ctx_tiles=nct - off),
        out_shape=out_shape, grid=(n_tiles,), in_specs=in_specs, out_specs=out_specs,
        scratch_shapes=[pltpu.VMEM((tm, D), BF16)],
        compiler_params=_cparams(("parallel",)),
        name="merge",
    )(h, *o_args, *(x if split_x else (x,)), mod, w["g_ffn"], w["w_gate"],
      w["b_gate"], w["w_branch"], w["w_out"], w["w_route"], w["b_route"])


def _moe_kernel(tab_ref, tok_ref, h2_ref, sw_ref, w1_ref, w3_ref, w2_ref, f_ref,
                xg_ref, y_ref, st_ref, *, blk, slots, tile_off):
    ti = pl.program_id(0)
    step = pl.program_id(1)

    @pl.when(step == 0)
    def _():
        f_ref[...] = jnp.zeros_like(f_ref)
        xg_ref[...] = jnp.zeros_like(xg_ref)

    tok_base = (ti + tile_off) * slots
    segs = []
    for j in range(MOE_EXPERTS_PER_STEP):
        base = ((ti + tile_off) * N_EXPERTS + step * MOE_EXPERTS_PER_STEP + j) * 2
        segs.append((tab_ref[base], tab_ref[base + 1]))

    def gather(j, off, n_rows):
        slot0 = tok_base + off

        def body(gi, c):
            r0 = gi * MOE_GROUP
            srcs = [pl.multiple_of(tok_ref[slot0 + r0 + u], SUBLANES) for u in range(MOE_GROUP)]
            for u in range(MOE_GROUP):
                dst = pl.multiple_of((r0 + u) * SUBLANES, SUBLANES)
                xg_ref[j, pl.ds(dst, SUBLANES), :] = h2_ref[pl.ds(srcs[u], SUBLANES), :]
            return c

        lax.fori_loop(0, n_rows // MOE_GROUP, body, 0)

    def ffn(j, off):
        xb = _load_token_major(xg_ref.at[j], blk).astype(BF16)
        a = jnp.dot(xb, w1_ref[j], preferred_element_type=F32)
        g = jnp.dot(xb, w3_ref[j], preferred_element_type=F32)
        hid = (a * _sigmoid(a) * g).astype(BF16)
        y = jnp.dot(hid, w2_ref[j], preferred_element_type=F32)
        y_ref[j] = y * sw_ref[0, pl.ds(off, blk), :]

    def scatter(j, off, n_rows):
        slot0 = tok_base + off

        def body(gi, c):
            r0 = pl.multiple_of(gi * MOE_GROUP, MOE_GROUP)
            dsts = [pl.multiple_of(tok_ref[slot0 + r0 + u], SUBLANES) for u in range(MOE_GROUP)]
            for ch in range(SUBLANES):
                st_ref[pl.ds(ch, MOE_GROUP, stride=SUBLANES), :] = y_ref[j, pl.ds(r0, MOE_GROUP),
                                                                         ch * LANES:(ch + 1) * LANES]
            vals = [f_ref[pl.ds(dsts[u], SUBLANES), :] + st_ref[u * SUBLANES:(u + 1) * SUBLANES, :]
                    for u in range(MOE_GROUP)]
            for u in reversed(range(MOE_GROUP)):
                f_ref[pl.ds(dsts[u], SUBLANES), :] = vals[u]
            return c

        lax.fori_loop(0, n_rows // MOE_GROUP, body, 0)

    for j, (seg0, n_pad) in enumerate(segs):
        gather(j, pl.multiple_of(seg0, MOE_GROUP), jnp.minimum(n_pad, blk))
    for j, (seg0, _) in enumerate(segs):
        ffn(j, pl.multiple_of(seg0, MOE_GROUP))
    for j, (seg0, n_pad) in enumerate(segs):
        scatter(j, pl.multiple_of(seg0, MOE_GROUP), jnp.minimum(n_pad, blk))

    for j, (seg0, n_pad) in enumerate(segs):
        def extra_block(b, carry, j=j, seg0=seg0, n_pad=n_pad):
            off = pl.multiple_of(seg0 + b * blk, MOE_GROUP)
            n_rows = jnp.minimum(n_pad - b * blk, blk)
            gather(j, off, n_rows)
            ffn(j, off)
            scatter(j, off, n_rows)
            return carry

        lax.fori_loop(1, (n_pad + blk - 1) // blk, extra_block, 0)


def _moe_call(lay, route, h2, w, *, with_ctx):
    T, D, tt = lay["T"], lay["D"], lay["tt"]
    blk = lay["moe_blk"]
    n_tiles_all = T // tt
    tile_off = 0 if with_ctx else lay["ncb"] * ROW_BLK // tt
    n_tiles = n_tiles_all - tile_off
    n_assign = 2 * tt
    n_fill = N_EXPERTS * MOE_GROUP
    slots = n_assign + n_fill + blk

    eid = route[:, 0:2].astype(jnp.int32).reshape(n_tiles_all, n_assign)
    wts = route[:, 2:4].reshape(n_tiles_all, n_assign)
    stride = pl.next_power_of_2(n_assign + MOE_GROUP)
    ex = jnp.arange(N_EXPERTS, dtype=jnp.int32)
    counts = jnp.sum(eid[:, :, None] == ex[None, None, :], axis=1, dtype=jnp.int32)
    n_dummy = (-counts) % MOE_GROUP
    fill_i = jnp.arange(MOE_GROUP, dtype=jnp.int32)[None, None, :]
    fill_key = jnp.where(fill_i < n_dummy[:, :, None],
                         ex[None, :, None] * stride + n_assign + fill_i,
                         N_EXPERTS * stride + ex[None, :, None] * MOE_GROUP + fill_i)
    keys = jnp.concatenate([eid * stride + jnp.arange(n_assign, dtype=jnp.int32)[None, :],
                            fill_key.reshape(n_tiles_all, n_fill)], axis=1)
    keys, slot_w = lax.sort(
        (keys, jnp.concatenate([wts, jnp.zeros((n_tiles_all, n_fill), F32)], axis=1)),
        dimension=1, is_stable=False, num_keys=1)
    a_sorted = keys % stride
    is_real = (a_sorted < n_assign) & (keys < N_EXPERTS * stride)
    slot_tok = jnp.where(is_real, a_sorted // 2 * SUBLANES, 0)
    slot_tok = jnp.pad(slot_tok, ((0, 0), (0, blk)))
    slot_w = jnp.pad(slot_w, ((0, 0), (0, blk)))
    padded = counts + n_dummy
    seg0 = jnp.cumsum(padded, axis=1) - padded
    tab = jnp.stack([seg0, padded], axis=-1).reshape(-1).astype(jnp.int32)

    return pl.pallas_call(
        functools.partial(_moe_kernel, blk=blk, slots=slots, tile_off=tile_off),
        out_shape=jax.ShapeDtypeStruct((T * SUBLANES, LANES), F32),
        grid_spec=pltpu.PrefetchScalarGridSpec(
            num_scalar_prefetch=2,
            grid=(n_tiles, N_EXPERTS // MOE_EXPERTS_PER_STEP),
            in_specs=[
                pl.BlockSpec((tt * SUBLANES, LANES), lambda t, e, *_: (t + tile_off, 0)),
                pl.BlockSpec((1, slots, 1), lambda t, e, *_: (t + tile_off, 0, 0)),
                pl.BlockSpec((MOE_EXPERTS_PER_STEP, D, EXPERT_FF), lambda t, e, *_: (e, 0, 0)),
                pl.BlockSpec((MOE_EXPERTS_PER_STEP, D, EXPERT_FF), lambda t, e, *_: (e, 0, 0)),
                pl.BlockSpec((MOE_EXPERTS_PER_STEP, EXPERT_FF, D), lambda t, e, *_: (e, 0, 0)),
            ],
            out_specs=pl.BlockSpec((tt * SUBLANES, LANES), lambda t, e, *_: (t + tile_off, 0)),
            scratch_shapes=[pltpu.VMEM((MOE_EXPERTS_PER_STEP, blk * SUBLANES, LANES), F32),
                            pltpu.VMEM((MOE_EXPERTS_PER_STEP, blk, D), F32),
                            pltpu.VMEM((MOE_GROUP * SUBLANES, LANES), F32)],
        ),
        compiler_params=_cparams(("parallel", "arbitrary")),
        name="moe_experts",
    )(tab, slot_tok.reshape(-1), h2, slot_w.reshape(n_tiles_all, slots, 1),
      w["w_ff1"], w["w_ff3"], w["w_ff2"])


def _final_kernel(x_ref, f_ref, mod_ref, g_ref, o_ref, *, tm):
    xf = x_ref[...] + mod_ref[0, 5:6, :] * _load_token_major(f_ref, tm)
    o_ref[...] = _rms(xf) * g_ref[...]


def _final_call(lay, layer, x1, f, mod, g_final):
    T, D, tm = lay["T"], lay["D"], lay["tm"]
    nct, tpb = lay["nct"], lay["tpb"]
    n_lat = T // tm - nct
    return pl.pallas_call(
        functools.partial(_final_kernel, tm=tm),
        out_shape=jax.ShapeDtypeStruct((n_lat * tm, D), F32),
        grid=(n_lat,),
        in_specs=[pl.BlockSpec((tm, D), lambda i: (i + nct, 0)),
                  pl.BlockSpec((tm * SUBLANES, LANES), lambda i: (i + nct, 0)),
                  pl.BlockSpec((1, 6, D), lambda i: (layer * lay["mod_rows"] + i // tpb, 0, 0)),
                  pl.BlockSpec((1, D), lambda i: (0, 0))],
        out_specs=pl.BlockSpec((tm, D), lambda i: (i, 0)),
        compiler_params=_cparams(("parallel",)),
        name="final_norm",
    )(x1, f, mod, g_final)


def _select_cols(wm, segs, scale=None):
    parts = []
    for k, (start, width) in enumerate(segs):
        if start is None:
            parts.append(jnp.zeros((wm.shape[0], width), wm.dtype))
        else:
            blk = wm[:, start:start + width]
            parts.append(blk if scale is None or scale[k] is None else blk * scale[k])
    return jnp.concatenate(parts, axis=1)


def _prep_layer(l, p):
    a_cols = A_Q_RANK + A_KV_RANK + A_ROPE
    b_off = a_cols
    c_off = b_off + 512
    d_off = c_off + 768
    qk_scale = HEAD_DIM ** -0.5 * LOG2E
    gqa_q = lambda off: [(off + hh * HEAD_DIM, HEAD_DIM) for hh in _GQA_PERM]
    segs = ([(0, 256), (256, 128), (None, 64), (384, 32), (None, 32)]
            + gqa_q(b_off) + [(b_off + 256, 128), (b_off + 384, 128)]
            + [(c_off, 256), (c_off + 256, 256), (c_off + 512, 256)]
            + gqa_q(d_off) + [(d_off + 256, 128), (d_off + 384, 128)])
    scale = [None] * len(segs)
    for k in (5, 6, 7, 8, 11):
        scale[k] = qk_scale
    w_in = _select_cols(p["w_in"][l], segs, scale).astype(BF16)
    assert w_in.shape[1] == PROJ_COLS

    hq = A_NOPE + A_ROPE
    segs_q = []
    for hh in range(N_HEADS):
        segs_q += [(hh * hq, hq), (None, LANES - hq)]
    w_q_b = _select_cols(p["w_q_b"][l], segs_q).astype(BF16)
    hk = A_NOPE + HEAD_DIM
    segs_k = []
    for hh in range(N_HEADS):
        segs_k += [(hh * hk, A_NOPE), (None, LANES - A_NOPE)]
    segs_k += [(hh * hk + A_NOPE, HEAD_DIM) for hh in range(N_HEADS)]
    w_kv_b = _select_cols(p["w_kv_b"][l], segs_k).astype(BF16)

    wb = p["w_branch"][l]
    perm_rows = lambda m: jnp.concatenate([m[hh * HEAD_DIM:(hh + 1) * HEAD_DIM] for hh in _GQA_PERM], axis=0)
    w_branch = jnp.stack([wb[0], perm_rows(wb[1]), wb[2], perm_rows(wb[3])]).astype(BF16)

    d = p["w_in"].shape[1]
    w_route = jnp.zeros((d, LANES), F32)
    w_route = w_route.at[:, :N_EXPERTS].set(p["w_router"][l]).at[:, N_EXPERTS:N_EXPERTS + N_GROUPS].set(p["w_group"][l])
    b_route = jnp.zeros((1, LANES), F32)
    b_route = b_route.at[0, :N_EXPERTS].set(p["b_router"][l]).at[0, N_EXPERTS:N_EXPERTS + N_GROUPS].set(p["b_group"][l])
    return {
        "g_mix": p["g_norm_mix"][l][None, :],
        "w_in": w_in,
        "g_q_a": p["g_q_a"][l][None, :],
        "w_q_b": w_q_b,
        "g_kv_a": p["g_kv_a"][l][None, :],
        "w_kv_b": w_kv_b,
        "g_q_d": (jnp.tile(p["g_q_d"][l], 2) * qk_scale)[None, :],
        "g_k_d": jnp.tile(p["g_k_d"][l], 2)[None, :],
        "sink": p["sink_b"][l],
        "rpb": p["rpb_c"][l],
        "w_gate": p["w_gate"][l].astype(BF16),
        "b_gate": p["b_gate"][l][:, None, :],
        "w_branch": w_branch,
        "w_out": p["w_out"][l].astype(BF16),
        "g_ffn": p["g_norm_ffn"][l][None, :],
        "w_route": jnp.stack([w_route.astype(BF16),
                              (w_route - w_route.astype(BF16).astype(F32)).astype(BF16)]),
        "b_route": b_route,
        "w_ff1": p["w_ff1"][l].astype(BF16),
        "w_ff3": p["w_ff3"][l].astype(BF16),
        "w_ff2": p["w_ff2"][l].astype(BF16),
    }


def _rope_tables(S, tm):
    t = np.arange(S)
    rows = (t // GRID_W).astype(np.float32)
    cols = (t % GRID_W).astype(np.float32)

    def cs(rot):
        half = rot // 2
        inv = np.float32(ROPE_THETA) ** (-np.arange(0, half, 2, dtype=np.float32) / np.float32(half))
        ar_, ac_ = rows[:, None] * inv, cols[:, None] * inv
        cos = np.concatenate([np.cos(ar_), np.cos(ar_), np.cos(ac_), np.cos(ac_)], axis=-1)
        sin = np.concatenate([-np.sin(ar_), np.sin(ar_), -np.sin(ac_), np.sin(ac_)], axis=-1)
        return cos.astype(np.float32), sin.astype(np.float32)

    cos64, sin64 = cs(HEAD_DIM)
    cos32, sin32 = cs(A_ROPE)
    ones = lambda n: np.ones((S, n), np.float32)
    zeros = lambda n: np.zeros((S, n), np.float32)
    tabs = {
        "cos_h": np.concatenate([cos64, cos64], axis=-1),
        "sin_h": np.concatenate([sin64, sin64], axis=-1),
        "cos_a": np.concatenate([ones(A_NOPE), cos32, ones(LANES - A_NOPE - A_ROPE)], axis=-1),
        "sin_a": np.concatenate([zeros(A_NOPE), sin32, zeros(LANES - A_NOPE - A_ROPE)], axis=-1),
    }
    ident = {"cos_h": 1.0, "sin_h": 0.0, "cos_a": 1.0, "sin_a": 0.0}
    return {k: jnp.asarray(np.concatenate([v, np.full((tm, LANES), ident[k], np.float32)], axis=0))
            for k, v in tabs.items()}


def _layout(B, S, n_ctx, D):
    assert n_ctx == ROW_BLK and S % 1024 == 0 and S // GRID_W >= 3 * (ROW_BLK // GRID_W) and B < 16
    T = B * (n_ctx + S)
    tm = 512 if (B * n_ctx) % 512 == 0 else 256
    def largest_tile(cands):
        return next(t for t in cands if (B * n_ctx) % t == 0 and S % t == 0)

    tq_dense = largest_tile((1024, 512, 256))
    tk_dense = largest_tile((2048, 1024, 512, 256))
    tt = 2048 if (B * n_ctx) % 2048 == 0 else B * n_ctx
    assert S % tt == 0
    return {
        "B": B, "S": S, "D": D, "T": T, "tm": tm,
        "ncb": B * n_ctx // ROW_BLK,
        "lb": S // ROW_BLK,
        "nct": B * n_ctx // tm,
        "tpb": S // tm,
        "tk_dense": tk_dense,
        "tq_dense": tq_dense,
        "tq_band": tm,
        "tt": tt,
        "moe_blk": 160,
        "mod_rows": 16,
    }


def kernel(x, c, ctx, c_ctx, w_mod, b_mod, g_norm_mix, w_in, g_q_a, w_q_b, g_kv_a, w_kv_b, sink_b, rpb_c,
           g_q_d, g_k_d, w_gate, b_gate, w_branch, w_out, g_norm_ffn, w_group, b_group, w_router, b_router,
           w_ff1, w_ff3, w_ff2, g_final):
    B, S, D = x.shape
    n_ctx = ctx.shape[1]
    depth = w_mod.shape[0]
    lay = _layout(B, S, n_ctx, D)
    params = dict(w_in=w_in, g_norm_mix=g_norm_mix, g_q_a=g_q_a, w_q_b=w_q_b, g_kv_a=g_kv_a, w_kv_b=w_kv_b,
                  sink_b=sink_b, rpb_c=rpb_c, g_q_d=g_q_d, g_k_d=g_k_d, w_gate=w_gate, b_gate=b_gate,
                  w_branch=w_branch, w_out=w_out, g_norm_ffn=g_norm_ffn, w_group=w_group, b_group=b_group,
                  w_router=w_router, b_router=b_router, w_ff1=w_ff1, w_ff3=w_ff3, w_ff2=w_ff2)

    c_all = jnp.zeros((lay["mod_rows"], D), F32).at[:B].set(c).at[B].set(c_ctx)
    mod = _modulation(c_all, w_mod, b_mod).reshape(depth * lay["mod_rows"], 6, D)
    tabs = _rope_tables(S, lay["tm"])
    win_bias = _window_bias(S, lay["tq_band"])
    xf = (ctx.reshape(B * n_ctx, D), x.reshape(B * S, D))

    f = None
    for l in range(depth):
        with_ctx = l < depth - 1
        w = _prep_layer(l, params)
        xf, pr = _proj_call(lay, xf, f, mod, mod, l, w, tabs)
        o = {}
        for kind, kl in (("A", "a"), ("B", "b"), ("C", "c"), ("D", "d")):
            q, k, v = pr["q" + kl], pr["k" + kl], pr["v" + kl]
            sink = w["sink"] if kind == "B" else None
            bias = win_bias if kind == "B" else None
            if kind == "C":
                bias = _neighbourhood_bias(w["rpb"], S // GRID_W, lay["tq_band"])
            o_ctx = _attn_context_call(lay, kind, q, k, v, sink=sink) if with_ctx else None
            o[kind] = (o_ctx, _attn_latent_call(lay, kind, q, k, v, sink=sink, bias=bias))
        xf, h2, route = _merge_call(lay, l, pr["h"], o, xf, mod, w, with_ctx=with_ctx)
        f = _moe_call(lay, route, h2, w, with_ctx=with_ctx)
    out = _final_call(lay, depth - 1, xf, f, mod, g_final[None, :])
    return out.reshape(B, S, D)
```

```python
import functools

import numpy as np
import jax
import jax.numpy as jnp
from jax import lax
from jax.experimental import pallas as pl
from jax.experimental.pallas import tpu as pltpu

F32 = jnp.float32
BF16 = jnp.bfloat16
HIGHEST = lax.Precision.HIGHEST

GRID_W = 64
ROPE_THETA = 10000.0
EPS = 1e-6
NEG_INF = -1e30
LOG2E = 1.4426950408889634
HEAD_DIM = 64
N_HEADS = 4
BRANCH_W = 256
A_Q_RANK = 256
A_KV_RANK = 128
A_NOPE = 64
A_ROPE = 32
NA_KH = 8
NA_KW = 16
WINDOW = 128
N_GROUPS = 4
EXPERTS_PER_GROUP = 8
N_EXPERTS = 32
EXPERT_FF = 256

LANES = 128
SUBLANES = 8
ROW_BLK = 256
MERGE_TN = 256
DENSE_CHAIN_ROWS = 512
MOE_EXPERTS_PER_STEP = 4
MOE_GROUP = 16
VMEM_LIMIT = 56 * 1024 * 1024

_PROJ_GROUPS = (("cq", 256), ("ckv", 128), ("kr", 128), ("qb", 256), ("kb", 128), ("vb", 128),
                ("qc", 256), ("kc", 256), ("vc", 256), ("qd", 256), ("kd", 128), ("vd", 128))
_PROJ_OFF = {}
_o = 0
for _n, _w in _PROJ_GROUPS:
    _PROJ_OFF[_n] = (_o, _w)
    _o += _w
PROJ_COLS = _o
_GQA_PERM = (0, 2, 1, 3)


def _cparams(sem):
    return pltpu.CompilerParams(dimension_semantics=sem, vmem_limit_bytes=VMEM_LIMIT)


def _lane_iota(shape):
    return lax.broadcasted_iota(jnp.int32, shape, len(shape) - 1)


def _sigmoid(x):
    return 1.0 / (1.0 + jnp.exp(-x))


def _mod_kernel(c_ref, w_ref, b_ref, o_ref):
    cf = c_ref[...]
    s = cf * _sigmoid(cf)
    o_ref[0] = jnp.dot(s, w_ref[0], precision=HIGHEST, preferred_element_type=F32) + b_ref[0]


def _modulation(c_all, w_mod, b_mod):
    n_layers, d, n_out = w_mod.shape
    rows = c_all.shape[0]
    tn = n_out // 2
    return pl.pallas_call(
        _mod_kernel,
        out_shape=jax.ShapeDtypeStruct((n_layers, rows, n_out), F32),
        grid=(n_layers, n_out // tn),
        in_specs=[pl.BlockSpec((rows, d), lambda l, j: (0, 0)),
                  pl.BlockSpec((1, d, tn), lambda l, j: (l, 0, j)),
                  pl.BlockSpec((1, 1, tn), lambda l, j: (l, 0, j))],
        out_specs=pl.BlockSpec((1, rows, tn), lambda l, j: (l, 0, j)),
        compiler_params=_cparams(("arbitrary", "arbitrary")),
        name="modulation",
    )(c_all, w_mod, b_mod.reshape(n_layers, 1, n_out))


def _rms(x):
    return x * lax.rsqrt(jnp.mean(x * x, axis=-1, keepdims=True) + EPS)


def _swap_blocks(x, blk):
    lane = _lane_iota(x.shape)
    up = pltpu.roll(x, LANES - blk, 1)
    dn = pltpu.roll(x, blk, 1)
    return jnp.where((lane // blk) % 2 == 0, up, dn)


def _rope(x, cos, sin, blk):
    return x * cos + _swap_blocks(x, blk) * sin


def _pair_norm(x, g):
    lo = _lane_iota(x.shape) < HEAD_DIM
    sq = x * x
    s_lo = jnp.sum(jnp.where(lo, sq, 0.0), axis=-1, keepdims=True)
    s_hi = jnp.sum(jnp.where(lo, 0.0, sq), axis=-1, keepdims=True)
    ms = jnp.where(lo, s_lo, s_hi) * (1.0 / HEAD_DIM)
    return x * lax.rsqrt(ms + EPS) * g


def _load_token_major(ref, rows):
    return jnp.concatenate(
        [ref[pl.ds(c, rows, stride=SUBLANES), :] for c in range(SUBLANES)], axis=-1)


def _tile_rows(xc_ref, xl_ref, is_ctx):
    return jnp.where(is_ctx, xc_ref[...], xl_ref[...])


def _proj_kernel(*refs, with_f, tm, scale_a, nct):
    it = iter(refs)
    if with_f:
        x_ref = next(it)
        f_ref = next(it)
        modp_ref = next(it)
    else:
        xc_ref, xl_ref = next(it), next(it)
    mod_ref = next(it)
    gmix_ref, win_ref, gqa_ref, wqb_ref, gkva_ref, wkvb_ref, gqd_ref, gkd_ref = (next(it) for _ in range(8))
    cosh_ref, sinh_ref, cosa_ref, sina_ref = (next(it) for _ in range(4))
    if with_f:
        x2_ref = next(it)
    h_ref = next(it)
    qa_ref, ka_ref, va_ref, qb_ref, kb_ref, vb_ref, qc_ref, kc_ref, vc_ref, qd_ref, kd_ref, vd_ref = (
        next(it) for _ in range(12))

    if with_f:
        xf = x_ref[...] + modp_ref[0, 5:6, :] * _load_token_major(f_ref, tm)
        x2_ref[...] = xf
    else:
        xf = _tile_rows(xc_ref, xl_ref, pl.program_id(0) < nct)
    h = _rms(xf) * gmix_ref[...] * (1.0 + mod_ref[0, 1:2, :]) + mod_ref[0, 0:1, :]
    hb = h.astype(BF16)
    h_ref[...] = hb
    p = jnp.dot(hb, win_ref[...], preferred_element_type=F32)

    def grp(name):
        o, w = _PROJ_OFF[name]
        return p[:, o:o + w]

    cosh, sinh = cosh_ref[...], sinh_ref[...]
    cosa, sina = cosa_ref[...], sina_ref[...]

    cq = (_rms(grp("cq")) * gqa_ref[...]).astype(BF16)
    qa = jnp.dot(cq, wqb_ref[...], preferred_element_type=F32)
    for hd in range(N_HEADS):
        sl = slice(hd * LANES, (hd + 1) * LANES)
        qa_ref[:, sl] = (_rope(qa[:, sl], cosa, sina, 8) * scale_a).astype(BF16)
    ckv = (_rms(grp("ckv")) * gkva_ref[...]).astype(BF16)
    kva = jnp.dot(ckv, wkvb_ref[...], preferred_element_type=F32)
    kr = _rope(grp("kr"), cosa, sina, 8)
    for hd in range(N_HEADS):
        sl = slice(hd * LANES, (hd + 1) * LANES)
        ka_ref[:, sl] = (kva[:, sl] + kr).astype(BF16)
    va_ref[...] = kva[:, N_HEADS * LANES:].astype(BF16)

    qb = grp("qb")
    for j in range(2):
        sl = slice(j * LANES, (j + 1) * LANES)
        qb_ref[:, sl] = _rope(qb[:, sl], cosh, sinh, 16).astype(BF16)
    kb_ref[...] = _rope(grp("kb"), cosh, sinh, 16).astype(BF16)
    vb_ref[...] = grp("vb").astype(BF16)

    qc_ref[...] = grp("qc").astype(BF16)
    kc_ref[...] = grp("kc").astype(BF16)
    vc_ref[...] = grp("vc").astype(BF16)

    qd = grp("qd")
    for j in range(2):
        sl = slice(j * LANES, (j + 1) * LANES)
        qd_ref[:, sl] = _rope(_pair_norm(qd[:, sl], gqd_ref[...]), cosh, sinh, 16).astype(BF16)
    kd_ref[...] = _rope(_pair_norm(grp("kd"), gkd_ref[...]), cosh, sinh, 16).astype(BF16)
    vd_ref[...] = grp("vd").astype(BF16)


def _split_row_specs(lay, width, off=0):
    tm, nct = lay["tm"], lay["nct"]
    return [pl.BlockSpec((tm, width), lambda i: (jnp.minimum(i + off, nct - 1), 0)),
            pl.BlockSpec((tm, width), lambda i: (jnp.maximum(i + off - nct, 0), 0))]


def _proj_call(lay, x, f, modp, mod, layer, w, tabs):
    T, D, tm = lay["T"], lay["D"], lay["tm"]
    nct, tpb, B = lay["nct"], lay["tpb"], lay["B"]
    with_f = f is not None
    n_tiles = T // tm

    def mod_row(i):
        return jnp.where(i < nct, B, (i - nct) // tpb)

    def tab_blk(i):
        return jnp.where(i < nct, tpb, (i - nct) % tpb)

    row = lambda i: (i, 0)
    const = lambda i: (0, 0)
    if with_f:
        in_specs = [pl.BlockSpec((tm, D), row),
                    pl.BlockSpec((tm * SUBLANES, LANES), row),
                    pl.BlockSpec((1, 6, D), lambda i: ((layer - 1) * lay["mod_rows"] + mod_row(i), 0, 0))]
        args = [x, f, modp]
    else:
        in_specs = _split_row_specs(lay, D)
        args = list(x)
    in_specs += [pl.BlockSpec((1, 6, D), lambda i: (layer * lay["mod_rows"] + mod_row(i), 0, 0))]
    args += [mod]
    for name in ("g_mix", "w_in", "g_q_a", "w_q_b", "g_kv_a", "w_kv_b", "g_q_d", "g_k_d"):
        a = w[name]
        in_specs.append(pl.BlockSpec(a.shape, const))
        args.append(a)
    for tname in ("cos_h", "sin_h", "cos_a", "sin_a"):
        in_specs.append(pl.BlockSpec((tm, LANES), lambda i: (tab_blk(i), 0)))
        args.append(tabs[tname])

    widths = [("h", D), ("qa", 512), ("ka", 512), ("va", 256), ("qb", 256), ("kb", 128), ("vb", 128),
              ("qc", 256), ("kc", 256), ("vc", 256), ("qd", 256), ("kd", 128), ("vd", 128)]
    out_shape, out_specs = [], []
    if with_f:
        out_shape.append(jax.ShapeDtypeStruct((T, D), F32))
        out_specs.append(pl.BlockSpec((tm, D), row))
    for _, wd in widths:
        out_shape.append(jax.ShapeDtypeStruct((T, wd), BF16))
        out_specs.append(pl.BlockSpec((tm, wd), row))

    outs = pl.pallas_call(
        functools.partial(_proj_kernel, with_f=with_f, tm=tm, nct=nct,
                          scale_a=float((A_NOPE + A_ROPE) ** -0.5 * LOG2E)),
        out_shape=out_shape, grid=(n_tiles,), in_specs=in_specs, out_specs=out_specs,
        compiler_params=_cparams(("parallel",)),
        name="proj_in",
    )(*args)
    outs = list(outs)
    x2 = outs.pop(0) if with_f else x
    names = [n for n, _ in widths]
    return x2, dict(zip(names, outs))


_NT = (((1,), (1,)), ((), ()))


def _head_plan(kind):
    if kind == "A":
        return tuple((r, None, r, r // 2) for r in range(N_HEADS))
    if kind == "C":
        return tuple((r // 2, r % 2, r // 2, r // 2) for r in range(N_HEADS))
    return tuple((r // 2, r % 2, 0, 0) for r in range(N_HEADS))


def _head_query(q_ref, plan_r):
    qt, half, _, _ = plan_r
    src = q_ref[:, qt * LANES:(qt + 1) * LANES]
    if half is None:
        return src
    lane = _lane_iota(src.shape)
    keep = (lane < HEAD_DIM) if half == 0 else (lane >= HEAD_DIM)
    return jnp.where(keep, src, jnp.zeros_like(src))


def _score_chunks(q, k_blocks, bias_blocks):
    chunks = []
    for kb, bb in zip(k_blocks, bias_blocks):
        s = lax.dot_general(q, kb, _NT, preferred_element_type=F32)
        if bb is not None:
            s = s + bb
        chunks += [s[:, c * LANES:(c + 1) * LANES] for c in range(s.shape[1] // LANES)]
    return chunks


def _row_max(chunks):
    m = functools.reduce(jnp.maximum, chunks)
    return jnp.broadcast_to(jnp.max(m, axis=-1, keepdims=True), m.shape)


def _weighted_values(p_chunks, v_blocks, half):
    pv, idx = None, 0
    for vb in v_blocks:
        n = vb.shape[0] // LANES
        p = jnp.concatenate(p_chunks[idx:idx + n], axis=1).astype(BF16)
        idx += n
        lane = _lane_iota(vb.shape)
        own = (lane < HEAD_DIM) if half == 0 else (lane >= HEAD_DIM)
        d = jnp.dot(p, jnp.where(own, vb, jnp.ones_like(vb)), preferred_element_type=F32)
        pv = d if pv is None else pv + d
    return pv


def _softmax_once(s, v_aug, bias_blocks, sink):
    chunks = []
    for c in range(s.shape[1] // LANES):
        sc = s[:, c * LANES:(c + 1) * LANES]
        bb = bias_blocks[c * LANES // ROW_BLK]
        if bb is not None:
            lo = c * LANES % ROW_BLK
            sc = sc + bb[:, lo:lo + LANES]
        chunks.append(sc)
    m = _row_max(chunks)
    if sink is not None:
        m = jnp.maximum(m, sink)
    p = jnp.concatenate([jnp.exp2(c - m) for c in chunks], axis=1).astype(BF16)
    pv = jnp.dot(p, v_aug, preferred_element_type=F32)
    l = pltpu.roll(pv, HEAD_DIM, 1)
    if sink is not None:
        l = l + jnp.exp2(sink - m)
    return pv / l


def _store_heads(o_ref, outs):
    lane = _lane_iota(outs[0].shape)
    for g in range(2):
        o_ref[:, g * LANES:(g + 1) * LANES] = jnp.where(
            lane < HEAD_DIM, outs[2 * g], outs[2 * g + 1]).astype(o_ref.dtype)


def _attn_band_kernel(*refs, kind, lb, n_kv, tq):
    it = iter(refs)
    q_ref = next(it)
    k_refs = [next(it) for _ in range(n_kv)]
    v_refs = [next(it) for _ in range(n_kv)]
    bias_ref = next(it) if n_kv > 1 else None
    sink_ref = next(it) if kind == "B" else None
    o_ref = next(it)
    t = pl.program_id(0) % lb
    var = jnp.where(t == 0, 0, jnp.where(t == lb - 1, 2, 1))
    plan = _head_plan(kind)
    scores = {}
    for kt in sorted({p[2] for p in plan}):
        heads = [r for r in range(N_HEADS) if plan[r][2] == kt]
        k_all = jnp.concatenate([k[:, kt * LANES:(kt + 1) * LANES] for k in k_refs], axis=0)
        q_all = jnp.concatenate([_head_query(q_ref, plan[r]) for r in heads], axis=0)
        s_all = lax.dot_general(q_all, k_all, _NT, preferred_element_type=F32)
        for j, r in enumerate(heads):
            scores[r] = s_all[j * tq:(j + 1) * tq, :]
    outs = []
    v_aug = {}
    for r, plan_r in enumerate(plan):
        _, _, kt, vt = plan_r
        if (vt, r % 2) not in v_aug:
            vv = jnp.concatenate([v[:, vt * LANES:(vt + 1) * LANES] for v in v_refs], axis=0)
            lane = _lane_iota(vv.shape)
            own = (lane < HEAD_DIM) if r % 2 == 0 else (lane >= HEAD_DIM)
            v_aug[(vt, r % 2)] = jnp.where(own, vv, jnp.ones_like(vv))
        bias = [None]
        for j in range(n_kv - 1):
            if kind == "C":
                bias.append(bias_ref[var, j, r * tq:(r + 1) * tq, :])
            else:
                bias.append(bias_ref[var, j])
        sink = sink_ref[_GQA_PERM[r]] * LOG2E if kind == "B" else None
        outs.append(_softmax_once(scores[r], v_aug[(vt, r % 2)], bias, sink))
    _store_heads(o_ref, outs)


def _attn_dense_kernel(q_ref, kc_ref, vc_ref, kl_ref, vl_ref, o_ref, qs_ref, m_ref, acc_ref,
                       *, kind, n_steps, tq):
    s = pl.program_id(1)
    plan = _head_plan(kind)

    def update(first):
        if first:
            for r in range(N_HEADS):
                qs_ref[r * tq:(r + 1) * tq, :] = _head_query(q_ref, plan[r])
        scores = {}
        for kt in sorted({p[2] for p in plan}):
            heads = [r for r in range(N_HEADS) if plan[r][2] == kt]
            ksl = slice(kt * LANES, (kt + 1) * LANES)
            k_blocks = [kc_ref[:, ksl], kl_ref[:, ksl]] if first else [kl_ref[:, ksl]]
            q_all = qs_ref[heads[0] * tq:(heads[-1] + 1) * tq, :]
            chunks = _score_chunks(q_all, k_blocks, [None] * len(k_blocks))
            for j, r in enumerate(heads):
                scores[r] = [c[j * tq:(j + 1) * tq, :] for c in chunks]
        n_sub = max(1, tq // DENSE_CHAIN_ROWS)
        sub = tq // n_sub
        for r, h in [(r, h) for r in range(N_HEADS) for h in range(n_sub)]:
            vt = plan[r][3]
            rows = slice(r * tq + h * sub, r * tq + (h + 1) * sub)
            vsl = slice(vt * LANES, (vt + 1) * LANES)
            v_blocks = [vc_ref[:, vsl], vl_ref[:, vsl]] if first else [vl_ref[:, vsl]]
            chunks = [c[h * sub:(h + 1) * sub, :] for c in scores[r]]
            m_cur = _row_max(chunks)
            if first:
                m_new = m_cur
            else:
                m_prev = m_ref[rows, :]
                m_new = jnp.maximum(m_prev, m_cur)
                alpha = jnp.exp2(m_prev - m_new)
            p = [jnp.exp2(c - m_new) for c in chunks]
            pv = _weighted_values(p, v_blocks, r % 2)
            if first:
                acc_ref[rows, :] = pv
            else:
                acc_ref[rows, :] = alpha * acc_ref[rows, :] + pv
            m_ref[rows, :] = m_new

    @pl.when(s == 0)
    def _():
        update(True)

    @pl.when(s > 0)
    def _():
        update(False)

    @pl.when(s == n_steps - 1)
    def _():
        outs = []
        for r in range(N_HEADS):
            rows = slice(r * tq, (r + 1) * tq)
            acc = acc_ref[rows, :]
            outs.append(acc / pltpu.roll(acc, HEAD_DIM, 1))
        _store_heads(o_ref, outs)


def _attn_latent_call(lay, kind, q, k, v, *, sink=None, bias=None):
    T, B, ncb, lb, S = lay["T"], lay["B"], lay["ncb"], lay["lb"], lay["S"]
    tq = ROW_BLK
    qw, kw, vw = q.shape[1], k.shape[1], v.shape[1]
    out_shape = jax.ShapeDtypeStruct((B * S, BRANCH_W), BF16)
    if kind in ("A", "D"):
        tk, tq = lay["tk_dense"], lay["tq_dense"]
        n_steps = S // tk
        lat0 = ncb * ROW_BLK // tk
        q0 = ncb * ROW_BLK // tq
        qpb = S // tq
        lat_blk = lambda i, s: (lat0 + (i // qpb) * n_steps + s, 0)
        return pl.pallas_call(
            functools.partial(_attn_dense_kernel, kind=kind, n_steps=n_steps, tq=tq),
            out_shape=out_shape,
            grid=(B * qpb, n_steps),
            in_specs=[pl.BlockSpec((tq, qw), lambda i, s: (q0 + i, 0)),
                      pl.BlockSpec((ROW_BLK, kw), lambda i, s: (i // qpb, 0)),
                      pl.BlockSpec((ROW_BLK, vw), lambda i, s: (i // qpb, 0)),
                      pl.BlockSpec((tk, kw), lat_blk),
                      pl.BlockSpec((tk, vw), lat_blk)],
            out_specs=pl.BlockSpec((tq, BRANCH_W), lambda i, s: (i, 0)),
            scratch_shapes=[pltpu.VMEM((N_HEADS * tq, LANES), BF16),
                            pltpu.VMEM((N_HEADS * tq, LANES), F32),
                            pltpu.VMEM((N_HEADS * tq, LANES), F32)],
            compiler_params=_cparams(("parallel", "arbitrary")),
            name="attn_" + kind,
        )(q, k, v, k, v)

    tq = lay["tq_band"]
    qpb = S // tq
    q0 = ncb * ROW_BLK // tq
    bpq = tq // ROW_BLK

    def nb(i, d):
        return (ncb + (i // qpb) * lb + jnp.clip((i % qpb) * bpq + d, 0, lb - 1), 0)

    if kind == "B":
        def win(i, j):
            start = _window_block_start(i % qpb, j, tq, S, jnp.clip)
            return (pl.multiple_of(ncb * ROW_BLK + (i // qpb) * S + start, WINDOW), 0)

        blk_shape = lambda width: (pl.Element(ROW_BLK), pl.Element(width))
        kv_maps = [functools.partial(win, j=j) for j in range(_window_blocks(tq))]
    else:
        blk_shape = lambda width: (ROW_BLK, width)
        kv_maps = [functools.partial(nb, d=d) for d in _band_offsets(tq)]
    n_kv = 1 + len(kv_maps)
    ctx_map = lambda i: (i // qpb, 0)
    in_specs = [pl.BlockSpec((tq, qw), lambda i: (q0 + i, 0))]
    in_specs += [pl.BlockSpec((ROW_BLK, kw), ctx_map)] + [pl.BlockSpec(blk_shape(kw), m) for m in kv_maps]
    in_specs += [pl.BlockSpec((ROW_BLK, vw), ctx_map)] + [pl.BlockSpec(blk_shape(vw), m) for m in kv_maps]
    in_specs.append(pl.BlockSpec(bias.shape, lambda i: (0,) * bias.ndim, pipeline_mode=pl.Buffered(1)))
    args = [q] + [k] * n_kv + [v] * n_kv + [bias]
    if kind == "B":
        in_specs.append(pl.BlockSpec(memory_space=pltpu.SMEM))
        args.append(sink)
    return pl.pallas_call(
        functools.partial(_attn_band_kernel, kind=kind, lb=qpb, n_kv=n_kv, tq=tq),
        out_shape=out_shape,
        grid=(B * qpb,),
        in_specs=in_specs,
        out_specs=pl.BlockSpec((tq, BRANCH_W), lambda i: (i, 0)),
        compiler_params=_cparams(("parallel",)),
        name="attn_" + kind,
    )(*args)


def _attn_context_call(lay, kind, q, k, v, *, sink=None):
    ncb = lay["ncb"]
    qw, kw, vw = q.shape[1], k.shape[1], v.shape[1]
    blk = lambda i: (i, 0)
    in_specs = [pl.BlockSpec((ROW_BLK, qw), blk), pl.BlockSpec((ROW_BLK, kw), blk),
                pl.BlockSpec((ROW_BLK, vw), blk)]
    args = [q, k, v]
    if kind == "B":
        in_specs.append(pl.BlockSpec(memory_space=pltpu.SMEM))
        args.append(sink)
    return pl.pallas_call(
        functools.partial(_attn_band_kernel, kind=kind, lb=1, n_kv=1, tq=ROW_BLK),
        out_shape=jax.ShapeDtypeStruct((ncb * ROW_BLK, BRANCH_W), BF16),
        grid=(ncb,),
        in_specs=in_specs,
        out_specs=pl.BlockSpec((ROW_BLK, BRANCH_W), blk),
        compiler_params=_cparams(("parallel",)),
        name="attn_ctx_" + kind,
    )(*args)


def _band_offsets(tq):
    return tuple(range(-1, tq // ROW_BLK + 1))


def _window_blocks(tq):
    return (tq + 2 * WINDOW) // ROW_BLK


def _window_block_start(t, j, tq, S, clip):
    return clip(t * tq - WINDOW + j * ROW_BLK, 0, S - ROW_BLK)


def _window_bias(S, tq):
    qpb = S // tq
    n_blk = _window_blocks(tq)
    qa, ka = np.arange(tq), np.arange(ROW_BLK)
    out = np.full((3, n_blk, tq, ROW_BLK), NEG_INF, np.float32)
    for vi, t_rep in enumerate((0, 1, qpb - 1)):
        qpos = t_rep * tq + qa
        seen = np.zeros(S, bool)
        for j in range(n_blk):
            kpos = int(_window_block_start(t_rep, j, tq, S, np.clip)) + ka
            ok = (np.abs(qpos[:, None] - kpos[None, :]) <= WINDOW) & ~seen[kpos][None, :]
            out[vi, j] = np.where(ok, 0.0, NEG_INF)
            seen[kpos] = True
    return jnp.asarray(out)


def _neighbourhood_bias(rpb, rows_total, tq):
    lb = rows_total * GRID_W // ROW_BLK
    qpb, bpq = rows_total * GRID_W // tq, tq // ROW_BLK
    rpt = ROW_BLK // GRID_W
    rpq = tq // GRID_W
    kh = min(NA_KH, rows_total)
    qa, ka = np.arange(tq), np.arange(ROW_BLK)
    q_sub, q_col = qa // GRID_W, qa % GRID_W
    k_sub, k_col = ka // GRID_W, ka % GRID_W
    n_dr, n_dc = 2 * NA_KH - 1, 2 * NA_KW - 1
    col = np.arange(GRID_W)
    dc = np.clip(col[None, :] - col[:, None], -(NA_KW - 1), NA_KW - 1) + NA_KW - 1
    hot_c = (dc[:, :, None] == np.arange(n_dc)).astype(np.float32)
    by_col = jnp.einsum("huv,cdv->hucd", rpb.astype(F32) * LOG2E, jnp.asarray(hot_c),
                        precision=HIGHEST)
    offs = _band_offsets(tq)
    vals = []
    for d in offs:
        dr = np.clip(d * rpt + np.arange(rpt)[None, :] - np.arange(rpq)[:, None],
                     -(NA_KH - 1), NA_KH - 1) + NA_KH - 1
        hot_r = (dr[:, :, None] == np.arange(n_dr)).astype(np.float32)
        v = jnp.einsum("abu,hucd->hacbd", jnp.asarray(hot_r), by_col, precision=HIGHEST)
        vals.append(v.reshape(N_HEADS, tq, ROW_BLK))
    out = []
    for t_rep in (0, 1, qpb - 1):
        per_block = []
        for di, d in enumerate(offs):
            kt = t_rep * bpq + d
            q_row = t_rep * rpq + q_sub
            k_row = kt * rpt + k_sub
            r_start = np.clip(q_row - kh // 2, 0, rows_total - kh)
            row_ok = (k_row[None] >= r_start[:, None]) & (k_row[None] < r_start[:, None] + kh)
            c_start = np.clip(q_col - NA_KW // 2, 0, GRID_W - NA_KW)
            col_ok = (k_col[None] >= c_start[:, None]) & (k_col[None] < c_start[:, None] + NA_KW)
            ok = row_ok & col_ok & (0 <= kt < lb)
            per_block.append(jnp.where(jnp.asarray(ok)[None], vals[di], NEG_INF).reshape(-1, ROW_BLK))
        out.append(jnp.stack(per_block))
    return jnp.stack(out)


def _merge_kernel(*refs, tm, split_x, n_ctx_tiles):
    it = iter(refs)
    h_ref = next(it)
    is_ctx = pl.program_id(0) < n_ctx_tiles
    if n_ctx_tiles > 0:
        o_refs = [(next(it), next(it)) for _ in range(4)]
        o_vals = [_tile_rows(oc, ol, is_ctx) for oc, ol in o_refs]
    else:
        o_vals = [next(it)[...] for _ in range(4)]
    if split_x:
        xc_ref, xl_ref = next(it), next(it)
    else:
        x_ref = next(it)
    (mod_ref, gffn_ref, wg_ref, bg_ref, wb_ref, wout_ref, wr_ref, br_ref,
     x1_ref, h2_ref, route_ref, y_ref) = (next(it) for _ in range(12))
    hb = h_ref[...]
    d_model = hb.shape[1]
    for t in range(d_model // MERGE_TN):
        cs = slice(t * MERGE_TN, (t + 1) * MERGE_TN)
        y = None
        for n, o_n in enumerate(o_vals):
            gate = _sigmoid(jnp.dot(hb, wg_ref[n, :, cs], preferred_element_type=F32) + bg_ref[n, :, cs])
            u = gate * jnp.dot(o_n, wb_ref[n, :, cs], preferred_element_type=F32)
            y = u if y is None else y + u
        y_ref[:, cs] = y.astype(BF16)
    z = jnp.dot(y_ref[...], wout_ref[...], preferred_element_type=F32)
    x_in = _tile_rows(xc_ref, xl_ref, is_ctx) if split_x else x_ref[...]
    x1 = x_in + mod_ref[0, 2:3, :] * z
    x1_ref[...] = x1
    h2 = _rms(x1) * gffn_ref[...] * (1.0 + mod_ref[0, 4:5, :]) + mod_ref[0, 3:4, :]
    for c in range(SUBLANES):
        h2_ref[pl.ds(c, tm, stride=SUBLANES), :] = h2[:, c * LANES:(c + 1) * LANES]

    h2_hi = h2.astype(BF16)
    h2_lo = (h2 - h2_hi.astype(F32)).astype(BF16)
    logit = (jnp.dot(h2_hi, wr_ref[0], preferred_element_type=F32)
             + jnp.dot(h2_lo, wr_ref[0], preferred_element_type=F32)
             + jnp.dot(h2_hi, wr_ref[1], preferred_element_type=F32)) + br_ref[...]
    lane = _lane_iota(logit.shape)
    big = jnp.int32(1 << 20)
    is_g = (lane >= N_EXPERTS) & (lane < N_EXPERTS + N_GROUPS)
    gl = jnp.where(is_g, logit, NEG_INF)
    gmax = jnp.max(gl, axis=-1, keepdims=True)
    gsel = jnp.min(jnp.where(gl == gmax, lane - N_EXPERTS, big), axis=-1, keepdims=True)
    gw = 1.0 / jnp.sum(jnp.where(is_g, jnp.exp(gl - gmax), 0.0), axis=-1, keepdims=True)
    in_grp = (lane < N_EXPERTS) & ((lane // EXPERTS_PER_GROUP) == gsel)
    el = jnp.where(in_grp, logit, NEG_INF)
    v1 = jnp.max(el, axis=-1, keepdims=True)
    i1 = jnp.min(jnp.where(el == v1, lane, big), axis=-1, keepdims=True)
    el2 = jnp.where(lane == i1, NEG_INF, el)
    v2 = jnp.max(el2, axis=-1, keepdims=True)
    i2 = jnp.min(jnp.where(el2 == v2, lane, big), axis=-1, keepdims=True)
    e21 = jnp.exp(v2 - v1)
    w1 = gw / (1.0 + e21)
    w2 = gw * e21 / (1.0 + e21)
    route_ref[...] = jnp.where(lane == 0, i1.astype(F32),
                               jnp.where(lane == 1, i2.astype(F32),
                                         jnp.where(lane == 2, w1, jnp.where(lane == 3, w2, 0.0))))


def _merge_call(lay, layer, h, o, x, mod, w, *, with_ctx):
    T, D, tm = lay["T"], lay["D"], lay["tm"]
    nct, tpb, B = lay["nct"], lay["tpb"], lay["B"]
    off = 0 if with_ctx else nct
    n_tiles = T // tm - off

    def mod_row(i):
        return jnp.where(i < nct, B, (i - nct) // tpb)

    row = lambda i: (i + off, 0)
    const2 = lambda i: (0, 0)
    const3 = lambda i: (0, 0, 0)
    split_x = isinstance(x, (tuple, list))
    in_specs = [pl.BlockSpec((tm, D), row)]
    o_args = []
    for kind in ("A", "B", "C", "D"):
        o_ctx, o_lat = o[kind]
        if with_ctx:
            in_specs += _split_row_specs(lay, BRANCH_W)
            o_args += [o_ctx, o_lat]
        else:
            in_specs.append(pl.BlockSpec((tm, BRANCH_W), lambda i: (i, 0)))
            o_args.append(o_lat)
    in_specs += _split_row_specs(lay, D, off) if split_x else [pl.BlockSpec((tm, D), row)]
    in_specs += [pl.BlockSpec((1, 6, D), lambda i: (layer * lay["mod_rows"] + mod_row(i + off), 0, 0)),
                 pl.BlockSpec((1, D), const2),
                 pl.BlockSpec(w["w_gate"].shape, const3),
                 pl.BlockSpec(w["b_gate"].shape, const3),
                 pl.BlockSpec(w["w_branch"].shape, const3),
                 pl.BlockSpec((D, D), const2),
                 pl.BlockSpec((2, D, LANES), const3),
                 pl.BlockSpec((1, LANES), const2)]
    out_shape = [jax.ShapeDtypeStruct((T, D), F32),
                 jax.ShapeDtypeStruct((T * SUBLANES, LANES), F32),
                 jax.ShapeDtypeStruct((T, LANES), F32)]
    out_specs = [pl.BlockSpec((tm, D), row),
                 pl.BlockSpec((tm * SUBLANES, LANES), row),
                 pl.BlockSpec((tm, LANES), row)]
    return pl.pallas_call(
        functools.partial(_merge_kernel, tm=tm, split_x=split_x, n_ctx_tiles=nct - off),
        out_shape=out_shape, grid=(n_tiles,), in_specs=in_specs, out_specs=out_specs,
        scratch_shapes=[pltpu.VMEM((tm, D), BF16)],
        compiler_params=_cparams(("parallel",)),
        name="merge",
    )(h, *o_args, *(x if split_x else (x,)), mod, w["g_ffn"], w["w_gate"],
      w["b_gate"], w["w_branch"], w["w_out"], w["w_route"], w["b_route"])


def _moe_kernel(tab_ref, tok_ref, h2_ref, sw_ref, w1_ref, w3_ref, w2_ref, f_ref,
                xg_ref, y_ref, st_ref, *, blk, slots, tile_off):
    ti = pl.program_id(0)
    step = pl.program_id(1)

    @pl.when(step == 0)
    def _():
        f_ref[...] = jnp.zeros_like(f_ref)
        xg_ref[...] = jnp.zeros_like(xg_ref)

    tok_base = (ti + tile_off) * slots
    segs = []
    for j in range(MOE_EXPERTS_PER_STEP):
        base = ((ti + tile_off) * N_EXPERTS + step * MOE_EXPERTS_PER_STEP + j) * 2
        segs.append((tab_ref[base], tab_ref[base + 1]))

    def gather(j, off, n_rows):
        slot0 = tok_base + off

        def body(gi, c):
            r0 = gi * MOE_GROUP
            srcs = [pl.multiple_of(tok_ref[slot0 + r0 + u], SUBLANES) for u in range(MOE_GROUP)]
            for u in range(MOE_GROUP):
                dst = pl.multiple_of((r0 + u) * SUBLANES, SUBLANES)
                xg_ref[j, pl.ds(dst, SUBLANES), :] = h2_ref[pl.ds(srcs[u], SUBLANES), :]
            return c

        lax.fori_loop(0, n_rows // MOE_GROUP, body, 0)

    def ffn(j, off):
        xb = _load_token_major(xg_ref.at[j], blk).astype(BF16)
        a = jnp.dot(xb, w1_ref[j], preferred_element_type=F32)
        g = jnp.dot(xb, w3_ref[j], preferred_element_type=F32)
        hid = (a * _sigmoid(a) * g).astype(BF16)
        y = jnp.dot(hid, w2_ref[j], preferred_element_type=F32)
        y_ref[j] = y * sw_ref[0, pl.ds(off, blk), :]

    def scatter(j, off, n_rows):
        slot0 = tok_base + off

        def body(gi, c):
            r0 = pl.multiple_of(gi * MOE_GROUP, MOE_GROUP)
            dsts = [pl.multiple_of(tok_ref[slot0 + r0 + u], SUBLANES) for u in range(MOE_GROUP)]
            for ch in range(SUBLANES):
                st_ref[pl.ds(ch, MOE_GROUP, stride=SUBLANES), :] = y_ref[j, pl.ds(r0, MOE_GROUP),
                                                                         ch * LANES:(ch + 1) * LANES]
            vals = [f_ref[pl.ds(dsts[u], SUBLANES), :] + st_ref[u * SUBLANES:(u + 1) * SUBLANES, :]
                    for u in range(MOE_GROUP)]
            for u in reversed(range(MOE_GROUP)):
                f_ref[pl.ds(dsts[u], SUBLANES), :] = vals[u]
            return c

        lax.fori_loop(0, n_rows // MOE_GROUP, body, 0)

    for j, (seg0, n_pad) in enumerate(segs):
        gather(j, pl.multiple_of(seg0, MOE_GROUP), jnp.minimum(n_pad, blk))
    for j, (seg0, _) in enumerate(segs):
        ffn(j, pl.multiple_of(seg0, MOE_GROUP))
    for j, (seg0, n_pad) in enumerate(segs):
        scatter(j, pl.multiple_of(seg0, MOE_GROUP), jnp.minimum(n_pad, blk))

    for j, (seg0, n_pad) in enumerate(segs):
        def extra_block(b, carry, j=j, seg0=seg0, n_pad=n_pad):
            off = pl.multiple_of(seg0 + b * blk, MOE_GROUP)
            n_rows = jnp.minimum(n_pad - b * blk, blk)
            gather(j, off, n_rows)
            ffn(j, off)
            scatter(j, off, n_rows)
            return carry

        lax.fori_loop(1, (n_pad + blk - 1) // blk, extra_block, 0)


def _moe_call(lay, route, h2, w, *, with_ctx):
    T, D, tt = lay["T"], lay["D"], lay["tt"]
    blk = lay["moe_blk"]
    n_tiles_all = T // tt
    tile_off = 0 if with_ctx else lay["ncb"] * ROW_BLK // tt
    n_tiles = n_tiles_all - tile_off
    n_assign = 2 * tt
    n_fill = N_EXPERTS * MOE_GROUP
    slots = n_assign + n_fill + blk

    eid = route[:, 0:2].astype(jnp.int32).reshape(n_tiles_all, n_assign)
    wts = route[:, 2:4].reshape(n_tiles_all, n_assign)
    stride = pl.next_power_of_2(n_assign + MOE_GROUP)
    ex = jnp.arange(N_EXPERTS, dtype=jnp.int32)
    counts = jnp.sum(eid[:, :, None] == ex[None, None, :], axis=1, dtype=jnp.int32)
    n_dummy = (-counts) % MOE_GROUP
    fill_i = jnp.arange(MOE_GROUP, dtype=jnp.int32)[None, None, :]
    fill_key = jnp.where(fill_i < n_dummy[:, :, None],
                         ex[None, :, None] * stride + n_assign + fill_i,
                         N_EXPERTS * stride + ex[None, :, None] * MOE_GROUP + fill_i)
    keys = jnp.concatenate([eid * stride + jnp.arange(n_assign, dtype=jnp.int32)[None, :],
                            fill_key.reshape(n_tiles_all, n_fill)], axis=1)
    keys, slot_w = lax.sort(
        (keys, jnp.concatenate([wts, jnp.zeros((n_tiles_all, n_fill), F32)], axis=1)),
        dimension=1, is_stable=False, num_keys=1)
    a_sorted = keys % stride
    is_real = (a_sorted < n_assign) & (keys < N_EXPERTS * stride)
    slot_tok = jnp.where(is_real, a_sorted // 2 * SUBLANES, 0)
    slot_tok = jnp.pad(slot_tok, ((0, 0), (0, blk)))
    slot_w = jnp.pad(slot_w, ((0, 0), (0, blk)))
    padded = counts + n_dummy
    seg0 = jnp.cumsum(padded, axis=1) - padded
    tab = jnp.stack([seg0, padded], axis=-1).reshape(-1).astype(jnp.int32)

    return pl.pallas_call(
        functools.partial(_moe_kernel, blk=blk, slots=slots, tile_off=tile_off),
        out_shape=jax.ShapeDtypeStruct((T * SUBLANES, LANES), F32),
        grid_spec=pltpu.PrefetchScalarGridSpec(
            num_scalar_prefetch=2,
            grid=(n_tiles, N_EXPERTS // MOE_EXPERTS_PER_STEP),
            in_specs=[
                pl.BlockSpec((tt * SUBLANES, LANES), lambda t, e, *_: (t + tile_off, 0)),
                pl.BlockSpec((1, slots, 1), lambda t, e, *_: (t + tile_off, 0, 0)),
                pl.BlockSpec((MOE_EXPERTS_PER_STEP, D, EXPERT_FF), lambda t, e, *_: (e, 0, 0)),
                pl.BlockSpec((MOE_EXPERTS_PER_STEP, D, EXPERT_FF), lambda t, e, *_: (e, 0, 0)),
                pl.BlockSpec((MOE_EXPERTS_PER_STEP, EXPERT_FF, D), lambda t, e, *_: (e, 0, 0)),
            ],
            out_specs=pl.BlockSpec((tt * SUBLANES, LANES), lambda t, e, *_: (t + tile_off, 0)),
            scratch_shapes=[pltpu.VMEM((MOE_EXPERTS_PER_STEP, blk * SUBLANES, LANES), F32),
                            pltpu.VMEM((MOE_EXPERTS_PER_STEP, blk, D), F32),
                            pltpu.VMEM((MOE_GROUP * SUBLANES, LANES), F32)],
        ),
        compiler_params=_cparams(("parallel", "arbitrary")),
        name="moe_experts",
    )(tab, slot_tok.reshape(-1), h2, slot_w.reshape(n_tiles_all, slots, 1),
      w["w_ff1"], w["w_ff3"], w["w_ff2"])


def _final_kernel(x_ref, f_ref, mod_ref, g_ref, o_ref, *, tm):
    xf = x_ref[...] + mod_ref[0, 5:6, :] * _load_token_major(f_ref, tm)
    o_ref[...] = _rms(xf) * g_ref[...]


def _final_call(lay, layer, x1, f, mod, g_final):
    T, D, tm = lay["T"], lay["D"], lay["tm"]
    nct, tpb = lay["nct"], lay["tpb"]
    n_lat = T // tm - nct
    return pl.pallas_call(
        functools.partial(_final_kernel, tm=tm),
        out_shape=jax.ShapeDtypeStruct((n_lat * tm, D), F32),
        grid=(n_lat,),
        in_specs=[pl.BlockSpec((tm, D), lambda i: (i + nct, 0)),
                  pl.BlockSpec((tm * SUBLANES, LANES), lambda i: (i + nct, 0)),
                  pl.BlockSpec((1, 6, D), lambda i: (layer * lay["mod_rows"] + i // tpb, 0, 0)),
                  pl.BlockSpec((1, D), lambda i: (0, 0))],
        out_specs=pl.BlockSpec((tm, D), lambda i: (i, 0)),
        compiler_params=_cparams(("parallel",)),
        name="final_norm",
    )(x1, f, mod, g_final)


def _select_cols(wm, segs, scale=None):
    parts = []
    for k, (start, width) in enumerate(segs):
        if start is None:
            parts.append(jnp.zeros((wm.shape[0], width), wm.dtype))
        else:
            blk = wm[:, start:start + width]
            parts.append(blk if scale is None or scale[k] is None else blk * scale[k])
    return jnp.concatenate(parts, axis=1)


def _prep_layer(l, p):
    a_cols = A_Q_RANK + A_KV_RANK + A_ROPE
    b_off = a_cols
    c_off = b_off + 512
    d_off = c_off + 768
    qk_scale = HEAD_DIM ** -0.5 * LOG2E
    gqa_q = lambda off: [(off + hh * HEAD_DIM, HEAD_DIM) for hh in _GQA_PERM]
    segs = ([(0, 256), (256, 128), (None, 64), (384, 32), (None, 32)]
            + gqa_q(b_off) + [(b_off + 256, 128), (b_off + 384, 128)]
            + [(c_off, 256), (c_off + 256, 256), (c_off + 512, 256)]
            + gqa_q(d_off) + [(d_off + 256, 128), (d_off + 384, 128)])
    scale = [None] * len(segs)
    for k in (5, 6, 7, 8, 11):
        scale[k] = qk_scale
    w_in = _select_cols(p["w_in"][l], segs, scale).astype(BF16)
    assert w_in.shape[1] == PROJ_COLS

    hq = A_NOPE + A_ROPE
    segs_q = []
    for hh in range(N_HEADS):
        segs_q += [(hh * hq, hq), (None, LANES - hq)]
    w_q_b = _select_cols(p["w_q_b"][l], segs_q).astype(BF16)
    hk = A_NOPE + HEAD_DIM
    segs_k = []
    for hh in range(N_HEADS):
        segs_k += [(hh * hk, A_NOPE), (None, LANES - A_NOPE)]
    segs_k += [(hh * hk + A_NOPE, HEAD_DIM) for hh in range(N_HEADS)]
    w_kv_b = _select_cols(p["w_kv_b"][l], segs_k).astype(BF16)

    wb = p["w_branch"][l]
    perm_rows = lambda m: jnp.concatenate([m[hh * HEAD_DIM:(hh + 1) * HEAD_DIM] for hh in _GQA_PERM], axis=0)
    w_branch = jnp.stack([wb[0], perm_rows(wb[1]), wb[2], perm_rows(wb[3])]).astype(BF16)

    d = p["w_in"].shape[1]
    w_route = jnp.zeros((d, LANES), F32)
    w_route = w_route.at[:, :N_EXPERTS].set(p["w_router"][l]).at[:, N_EXPERTS:N_EXPERTS + N_GROUPS].set(p["w_group"][l])
    b_route = jnp.zeros((1, LANES), F32)
    b_route = b_route.at[0, :N_EXPERTS].set(p["b_router"][l]).at[0, N_EXPERTS:N_EXPERTS + N_GROUPS].set(p["b_group"][l])
    return {
        "g_mix": p["g_norm_mix"][l][None, :],
        "w_in": w_in,
        "g_q_a": p["g_q_a"][l][None, :],
        "w_q_b": w_q_b,
        "g_kv_a": p["g_kv_a"][l][None, :],
        "w_kv_b": w_kv_b,
        "g_q_d": (jnp.tile(p["g_q_d"][l], 2) * qk_scale)[None, :],
        "g_k_d": jnp.tile(p["g_k_d"][l], 2)[None, :],
        "sink": p["sink_b"][l],
        "rpb": p["rpb_c"][l],
        "w_gate": p["w_gate"][l].astype(BF16),
        "b_gate": p["b_gate"][l][:, None, :],
        "w_branch": w_branch,
        "w_out": p["w_out"][l].astype(BF16),
        "g_ffn": p["g_norm_ffn"][l][None, :],
        "w_route": jnp.stack([w_route.astype(BF16),
                              (w_route - w_route.astype(BF16).astype(F32)).astype(BF16)]),
        "b_route": b_route,
        "w_ff1": p["w_ff1"][l].astype(BF16),
        "w_ff3": p["w_ff3"][l].astype(BF16),
        "w_ff2": p["w_ff2"][l].astype(BF16),
    }


def _rope_tables(S, tm):
    t = np.arange(S)
    rows = (t // GRID_W).astype(np.float32)
    cols = (t % GRID_W).astype(np.float32)

    def cs(rot):
        half = rot // 2
        inv = np.float32(ROPE_THETA) ** (-np.arange(0, half, 2, dtype=np.float32) / np.float32(half))
        ar_, ac_ = rows[:, None] * inv, cols[:, None] * inv
        cos = np.concatenate([np.cos(ar_), np.cos(ar_), np.cos(ac_), np.cos(ac_)], axis=-1)
        sin = np.concatenate([-np.sin(ar_), np.sin(ar_), -np.sin(ac_), np.sin(ac_)], axis=-1)
        return cos.astype(np.float32), sin.astype(np.float32)

    cos64, sin64 = cs(HEAD_DIM)
    cos32, sin32 = cs(A_ROPE)
    ones = lambda n: np.ones((S, n), np.float32)
    zeros = lambda n: np.zeros((S, n), np.float32)
    tabs = {
        "cos_h": np.concatenate([cos64, cos64], axis=-1),
        "sin_h": np.concatenate([sin64, sin64], axis=-1),
        "cos_a": np.concatenate([ones(A_NOPE), cos32, ones(LANES - A_NOPE - A_ROPE)], axis=-1),
        "sin_a": np.concatenate([zeros(A_NOPE), sin32, zeros(LANES - A_NOPE - A_ROPE)], axis=-1),
    }
    ident = {"cos_h": 1.0, "sin_h": 0.0, "cos_a": 1.0, "sin_a": 0.0}
    return {k: jnp.asarray(np.concatenate([v, np.full((tm, LANES), ident[k], np.float32)], axis=0))
            for k, v in tabs.items()}


def _layout(B, S, n_ctx, D):
    assert n_ctx == ROW_BLK and S % 1024 == 0 and S // GRID_W >= 3 * (ROW_BLK // GRID_W) and B < 16
    T = B * (n_ctx + S)
    tm = 512 if (B * n_ctx) % 512 == 0 else 256
    def largest_tile(cands):
        return next(t for t in cands if (B * n_ctx) % t == 0 and S % t == 0)

    tq_dense = largest_tile((1024, 512, 256))
    tk_dense = largest_tile((2048, 1024, 512, 256))
    tt = 2048 if (B * n_ctx) % 2048 == 0 else B * n_ctx
    assert S % tt == 0
    return {
        "B": B, "S": S, "D": D, "T": T, "tm": tm,
        "ncb": B * n_ctx // ROW_BLK,
        "lb": S // ROW_BLK,
        "nct": B * n_ctx // tm,
        "tpb": S // tm,
        "tk_dense": tk_dense,
        "tq_dense": tq_dense,
        "tq_band": tm,
        "tt": tt,
        "moe_blk": 160,
        "mod_rows": 16,
    }


def kernel(x, c, ctx, c_ctx, w_mod, b_mod, g_norm_mix, w_in, g_q_a, w_q_b, g_kv_a, w_kv_b, sink_b, rpb_c,
           g_q_d, g_k_d, w_gate, b_gate, w_branch, w_out, g_norm_ffn, w_group, b_group, w_router, b_router,
           w_ff1, w_ff3, w_ff2, g_final):
    B, S, D = x.shape
    n_ctx = ctx.shape[1]
    depth = w_mod.shape[0]
    lay = _layout(B, S, n_ctx, D)
    params = dict(w_in=w_in, g_norm_mix=g_norm_mix, g_q_a=g_q_a, w_q_b=w_q_b, g_kv_a=g_kv_a, w_kv_b=w_kv_b,
                  sink_b=sink_b, rpb_c=rpb_c, g_q_d=g_q_d, g_k_d=g_k_d, w_gate=w_gate, b_gate=b_gate,
                  w_branch=w_branch, w_out=w_out, g_norm_ffn=g_norm_ffn, w_group=w_group, b_group=b_group,
                  w_router=w_router, b_router=b_router, w_ff1=w_ff1, w_ff3=w_ff3, w_ff2=w_ff2)

    c_all = jnp.zeros((lay["mod_rows"], D), F32).at[:B].set(c).at[B].set(c_ctx)
    mod = _modulation(c_all, w_mod, b_mod).reshape(depth * lay["mod_rows"], 6, D)
    tabs = _rope_tables(S, lay["tm"])
    win_bias = _window_bias(S, lay["tq_band"])
    xf = (ctx.reshape(B * n_ctx, D), x.reshape(B * S, D))

    f = None
    for l in range(depth):
        with_ctx = l < depth - 1
        w = _prep_layer(l, params)
        xf, pr = _proj_call(lay, xf, f, mod, mod, l, w, tabs)
        o = {}
        for kind, kl in (("A", "a"), ("B", "b"), ("C", "c"), ("D", "d")):
            q, k, v = pr["q" + kl], pr["k" + kl], pr["v" + kl]
            sink = w["sink"] if kind == "B" else None
            bias = win_bias if kind == "B" else None
            if kind == "C":
                bias = _neighbourhood_bias(w["rpb"], S // GRID_W, lay["tq_band"])
            o_ctx = _attn_context_call(lay, kind, q, k, v, sink=sink) if with_ctx else None
            o[kind] = (o_ctx, _attn_latent_call(lay, kind, q, k, v, sink=sink, bias=bias))
        xf, h2, route = _merge_call(lay, l, pr["h"], o, xf, mod, w, with_ctx=with_ctx)
        f = _moe_call(lay, route, h2, w, with_ctx=with_ctx)
    out = _final_call(lay, depth - 1, xf, f, mod, g_final[None, :])
    return out.reshape(B, S, D)
```
